```python
import math
import jax, jax.numpy as jnp
from jax import lax
import numpy as np

D_MODEL = 1024
BATCH = 4
SEQ = 4096
DEPTH = 4
DEC_BATCH = 128
DEC_SEQ = 1
PAST_LEN = 8192
PAGE_SIZE = 128

N_A_LAYERS = DEPTH // 2
N_B_LAYERS = DEPTH - N_A_LAYERS
GROUP_SIZE = 16
N_GROUPS = D_MODEL // GROUP_SIZE
STATE_DIM = 64
DT_MIN = 1e-3
DT_MAX = 1e-1
HEAD_DIM = 64
N_HEADS = D_MODEL // HEAD_DIM
N_KV_HEADS = 4
Q_PER_KV = N_HEADS // N_KV_HEADS
WINDOW = 128
ROPE_THETA = 10000.0
D_FF = 4 * D_MODEL
EPS = 1e-6

kernel_name = "yoco_s5_swa_sink_decoder_step"


def rms_norm(x, g):
    x32 = x.astype(jnp.float32)
    y = x32 * lax.rsqrt(jnp.mean(x32 * x32, axis=-1, keepdims=True) + EPS)
    return (y * g.astype(jnp.float32)).astype(x.dtype)


def rope(x, pos):
    half = HEAD_DIM // 2
    inv = ROPE_THETA ** (-jnp.arange(half, dtype=jnp.float32) / half)
    ang = pos.astype(jnp.float32)[:, None] * inv[None, :]
    cos = jnp.cos(ang)[:, None, :]
    sin = jnp.sin(ang)[:, None, :]
    x32 = x.astype(jnp.float32)
    x1, x2 = x32[..., :half], x32[..., half:]
    out = jnp.concatenate([x1 * cos - x2 * sin, x2 * cos + x1 * sin], axis=-1)
    return out.astype(x.dtype)


def s5_discretize(a_re, a_im, log_dt, b_re, b_im):
    dt = jnp.exp(log_dt.astype(jnp.float32))
    lam = lax.complex(a_re.astype(jnp.float32), a_im.astype(jnp.float32))
    lam_bar = jnp.exp(lam * dt)
    b = lax.complex(b_re.astype(jnp.float32), b_im.astype(jnp.float32))
    b_bar = ((lam_bar - 1.0) / lam)[..., None] * b
    return lam_bar, b_bar


def s5_mixer(u, s0, lam_bar, b_bar, c, d_skip, w_glu, b_glu):
    bsz, seq_len, _ = u.shape
    u32 = u.astype(jnp.float32)
    ug = u32.reshape(bsz, seq_len, N_GROUPS, GROUP_SIZE).astype(jnp.complex64)
    bu = jnp.einsum('blgc,gpc->blgp', ug, b_bar)
    if s0 is not None:
        bu = bu.at[:, 0].add(lam_bar * s0)
    a = jnp.broadcast_to(lam_bar, (1, seq_len) + lam_bar.shape)

    def combine(e1, e2):
        a1, b1 = e1
        a2, b2 = e2
        return a1 * a2, a2 * b1 + b2

    _, s = lax.associative_scan(combine, (a, bu), axis=1)
    y = jnp.real(jnp.einsum('blgp,gcp->blgc', s, c)).reshape(bsz, seq_len, D_MODEL)
    y = y + d_skip.astype(jnp.float32) * u32
    g = jax.nn.gelu(y).astype(u.dtype)
    z = g @ w_glu + b_glu
    out = z[..., :D_MODEL] * jax.nn.sigmoid(z[..., D_MODEL:])
    return out.astype(u.dtype), s[:, -1]


def sq_relu_mlp(h, w_in, w_out):
    return jnp.square(jax.nn.relu(h @ w_in)) @ w_out


def shared_kv(x, g_kv, w_k, w_v, k_gain, pos):
    bsz, seq_len, _ = x.shape
    h = rms_norm(x, g_kv)
    k = (h @ w_k).reshape(bsz, seq_len, N_KV_HEADS, HEAD_DIM)
    k = rope(rms_norm(k, k_gain), pos)
    v = (h @ w_v).reshape(bsz, seq_len, N_KV_HEADS, HEAD_DIM)
    return k, v


def queries(x, g, w_q, q_gain, pos):
    bsz, seq_len, _ = x.shape
    h = rms_norm(x, g)
    q = (h @ w_q).reshape(bsz, seq_len, N_HEADS, HEAD_DIM)
    q = rope(rms_norm(q, q_gain), pos)
    return q.reshape(bsz, seq_len, N_KV_HEADS, Q_PER_KV, HEAD_DIM)


def sink_attention(q, k, v, qpos, kpos, sinks):
    s = jnp.einsum('...qhgd,...khd->...hgqk', q.astype(jnp.float32), k.astype(jnp.float32)) * (HEAD_DIM ** -0.5)
    diff = qpos[..., :, None] - kpos[..., None, :]
    valid = (diff >= 0) & (diff < WINDOW) & (kpos[..., None, :] >= 0)
    s = jnp.where(valid[..., None, None, :, :], s, -jnp.inf)
    sink = sinks.astype(jnp.float32).reshape(N_KV_HEADS, Q_PER_KV)[:, :, None, None]
    m = jnp.maximum(jnp.max(s, axis=-1, keepdims=True), sink)
    p = jnp.exp(s - m)
    denom = jnp.sum(p, axis=-1, keepdims=True) + jnp.exp(sink - m)
    o = jnp.einsum('...hgqk,...khd->...qhgd', p / denom, v.astype(jnp.float32))
    return o.astype(q.dtype)


def swa_prompt(q, k, v, pos, sinks):
    bsz, seq_len = q.shape[:2]
    nb = seq_len // WINDOW
    qb = q.reshape(bsz, nb, WINDOW, N_KV_HEADS, Q_PER_KV, HEAD_DIM)

    def with_prev(t):
        tb = t.reshape(bsz, nb, WINDOW, N_KV_HEADS, HEAD_DIM)
        prev = jnp.concatenate([jnp.zeros_like(tb[:, :1]), tb[:, :-1]], axis=1)
        return jnp.concatenate([prev, tb], axis=2)

    qpos = pos.reshape(nb, WINDOW)
    kpos = jnp.concatenate([qpos - WINDOW, qpos], axis=1)
    o = sink_attention(qb, with_prev(k), with_prev(v), qpos, kpos, sinks)
    return o.reshape(bsz, seq_len, N_HEADS * HEAD_DIM)


def setup_inputs(seed: int = 0) -> dict:
    key = jax.random.key(seed)
    ks = jax.random.split(key, 32)
    f32 = jnp.float32

    def nrm(k, shape, scale):
        return jax.random.normal(k, shape, f32) * scale

    cache_rows = min(WINDOW, PAST_LEN)
    ssm_shape = (N_A_LAYERS, N_GROUPS, STATE_DIM)
    n_idx = jnp.arange(STATE_DIM, dtype=f32)
    return {
        'x_prompt': nrm(ks[0], (BATCH, SEQ, D_MODEL), 1.0),
        'x_sample': nrm(ks[1], (DEC_BATCH, DEC_SEQ, D_MODEL), 1.0),
        'state_ssm_re': nrm(ks[2], (N_A_LAYERS, DEC_BATCH, N_GROUPS, STATE_DIM), 0.1),
        'state_ssm_im': nrm(ks[3], (N_A_LAYERS, DEC_BATCH, N_GROUPS, STATE_DIM), 0.1),
        'cache_k': nrm(ks[4], (DEC_BATCH, cache_rows, N_KV_HEADS, HEAD_DIM), 1.0),
        'cache_v': nrm(ks[5], (DEC_BATCH, cache_rows, N_KV_HEADS, HEAD_DIM), 1.0),
        'norm_mix': 1.0 + nrm(ks[6], (DEPTH, D_MODEL), 0.02),
        'norm_mlp': 1.0 + nrm(ks[7], (DEPTH, D_MODEL), 0.02),
        'ssm_a_re': -0.5 + nrm(ks[8], ssm_shape, 0.01),
        'ssm_a_im': math.pi * n_idx + nrm(ks[9], ssm_shape, 0.01),
        'ssm_log_dt': jax.random.uniform(ks[10], ssm_shape, f32, math.log(DT_MIN), math.log(DT_MAX)),
        'ssm_b_re': nrm(ks[11], (N_A_LAYERS, N_GROUPS, STATE_DIM, GROUP_SIZE), (2 * GROUP_SIZE) ** -0.5),
        'ssm_b_im': nrm(ks[12], (N_A_LAYERS, N_GROUPS, STATE_DIM, GROUP_SIZE), (2 * GROUP_SIZE) ** -0.5),
        'ssm_c_re': nrm(ks[13], (N_A_LAYERS, N_GROUPS, GROUP_SIZE, STATE_DIM), STATE_DIM ** -0.5),
        'ssm_c_im': nrm(ks[14], (N_A_LAYERS, N_GROUPS, GROUP_SIZE, STATE_DIM), STATE_DIM ** -0.5),
        'ssm_d': nrm(ks[15], (N_A_LAYERS, D_MODEL), 1.0),
        'w_glu': nrm(ks[16], (N_A_LAYERS, D_MODEL, 2 * D_MODEL), D_MODEL ** -0.5),
        'b_glu': nrm(ks[17], (N_A_LAYERS, 2 * D_MODEL), 0.01),
        'norm_kv': 1.0 + nrm(ks[18], (D_MODEL,), 0.02),
        'w_k': nrm(ks[19], (D_MODEL, N_KV_HEADS * HEAD_DIM), D_MODEL ** -0.5),
        'w_v': nrm(ks[20], (D_MODEL, N_KV_HEADS * HEAD_DIM), D_MODEL ** -0.5),
        'k_norm': 1.0 + nrm(ks[21], (HEAD_DIM,), 0.02),
        'w_q': nrm(ks[22], (N_B_LAYERS, D_MODEL, N_HEADS * HEAD_DIM), D_MODEL ** -0.5),
        'q_norm': 1.0 + nrm(ks[23], (N_B_LAYERS, HEAD_DIM), 0.02),
        'attn_sinks': nrm(ks[24], (N_B_LAYERS, N_HEADS), 0.5),
        'w_o': nrm(ks[25], (N_B_LAYERS, N_HEADS * HEAD_DIM, D_MODEL), (N_HEADS * HEAD_DIM) ** -0.5),
        'w_mlp_in': nrm(ks[26], (DEPTH, D_MODEL, D_FF), D_MODEL ** -0.5),
        'w_mlp_out': nrm(ks[27], (DEPTH, D_FF, D_MODEL), D_FF ** -0.5),
    }


def reference(x_prompt, x_sample, state_ssm_re, state_ssm_im, cache_k, cache_v,
              norm_mix, norm_mlp, ssm_a_re, ssm_a_im, ssm_log_dt, ssm_b_re, ssm_b_im,
              ssm_c_re, ssm_c_im, ssm_d, w_glu, b_glu, norm_kv, w_k, w_v, k_norm,
              w_q, q_norm, attn_sinks, w_o, w_mlp_in, w_mlp_out):
    seq_p = x_prompt.shape[1]
    seq_s = x_sample.shape[1]
    pos_p = jnp.arange(seq_p, dtype=jnp.int32)
    pos_s = PAST_LEN + jnp.arange(seq_s, dtype=jnp.int32)
    n_cached = cache_k.shape[1]
    kpos_s = jnp.concatenate([PAST_LEN - n_cached + jnp.arange(n_cached, dtype=jnp.int32), pos_s])

    xp, xs = x_prompt, x_sample
    sp_re, sp_im, ss_re, ss_im = [], [], [], []
    kp = vp = k_all_s = v_all_s = None
    for layer in range(DEPTH):
        if layer < N_A_LAYERS:
            i = layer
            lam_bar, b_bar = s5_discretize(ssm_a_re[i], ssm_a_im[i], ssm_log_dt[i], ssm_b_re[i], ssm_b_im[i])
            c = lax.complex(ssm_c_re[i].astype(jnp.float32), ssm_c_im[i].astype(jnp.float32))
            s0 = lax.complex(state_ssm_re[i].astype(jnp.float32), state_ssm_im[i].astype(jnp.float32))
            yp, fin_p = s5_mixer(rms_norm(xp, norm_mix[layer]), None, lam_bar, b_bar, c, ssm_d[i], w_glu[i], b_glu[i])
            ys, fin_s = s5_mixer(rms_norm(xs, norm_mix[layer]), s0, lam_bar, b_bar, c, ssm_d[i], w_glu[i], b_glu[i])
            xp = xp + yp
            xs = xs + ys
            sp_re.append(jnp.real(fin_p))
            sp_im.append(jnp.imag(fin_p))
            ss_re.append(jnp.real(fin_s))
            ss_im.append(jnp.imag(fin_s))
        else:
            if layer == N_A_LAYERS:
                kp, vp = shared_kv(xp, norm_kv, w_k, w_v, k_norm, pos_p)
                ks_new, vs_new = shared_kv(xs, norm_kv, w_k, w_v, k_norm, pos_s)
                k_all_s = jnp.concatenate([cache_k.astype(ks_new.dtype), ks_new], axis=1)
                v_all_s = jnp.concatenate([cache_v.astype(vs_new.dtype), vs_new], axis=1)
            j = layer - N_A_LAYERS
            qp = queries(xp, norm_mix[layer], w_q[j], q_norm[j], pos_p)
            qs = queries(xs, norm_mix[layer], w_q[j], q_norm[j], pos_s)
            op = swa_prompt(qp, kp, vp, pos_p, attn_sinks[j])
            os_ = sink_attention(qs, k_all_s, v_all_s, pos_s, kpos_s, attn_sinks[j])
            os_ = os_.reshape(xs.shape[0], seq_s, N_HEADS * HEAD_DIM)
            xp = xp + op @ w_o[j]
            xs = xs + os_ @ w_o[j]
        xp = xp + sq_relu_mlp(rms_norm(xp, norm_mlp[layer]), w_mlp_in[layer], w_mlp_out[layer])
        xs = xs + sq_relu_mlp(rms_norm(xs, norm_mlp[layer]), w_mlp_in[layer], w_mlp_out[layer])

    keep_p = min(WINDOW, seq_p)
    keep_s = min(WINDOW, PAST_LEN + seq_s)
    return (xp, xs,
            jnp.stack(sp_re), jnp.stack(sp_im), kp[:, -keep_p:], vp[:, -keep_p:],
            jnp.stack(ss_re), jnp.stack(ss_im), k_all_s[:, -keep_s:], v_all_s[:, -keep_s:])
```

```python
import functools
import math

import jax
import jax.numpy as jnp
from jax import lax
from jax.experimental import pallas as pl
from jax.experimental.pallas import tpu as pltpu

F32 = jnp.float32
BF16 = jnp.bfloat16

D_MODEL = 1024
N_GROUPS = 64
GROUP_SIZE = 16
STATE_DIM = 64
HEAD_DIM = 64
N_HEADS = 16
N_KV_HEADS = 4
WINDOW = 128
ROPE_THETA = 10000.0
D_FF = 4 * D_MODEL
EPS = 1e-6

SUBLANES = 8
LANES = 128
MXU_TILE = 256
VMEM_LIMIT_BYTES = 56 * 1024 * 1024

N_CH_BLOCKS = D_MODEL // MXU_TILE
STATE_BLOCK = (MXU_TILE // GROUP_SIZE) * STATE_DIM
STATE_LANES = 2 * STATE_BLOCK * N_CH_BLOCKS

S5_TIME_CHUNK = 128
MLP_ROWS = 512
FF_CHUNK = 1024
KV_ROWS = 512


def _const_spec(shape):
    zeros = (0,) * len(shape)
    return pl.BlockSpec(shape, lambda *_: zeros, pipeline_mode=pl.Buffered(1))


def _params(sem):
    return pltpu.CompilerParams(dimension_semantics=sem, vmem_limit_bytes=VMEM_LIMIT_BYTES)


def _rms(x, g):
    return x * lax.rsqrt(jnp.mean(x * x, axis=-1, keepdims=True) + EPS) * g


def _dot(a, b):
    return jnp.dot(a, b, preferred_element_type=F32)


def _seg_mean_sq(x, ones_ref):
    sq = x * x
    hi = sq.astype(BF16)
    lo = (sq - hi.astype(F32)).astype(BF16)
    ones = ones_ref[...]
    outs = []
    for t in range(x.shape[-1] // MXU_TILE):
        sl = slice(MXU_TILE * t, MXU_TILE * (t + 1))
        outs.append(_dot(hi[:, sl], ones) + _dot(lo[:, sl], ones))
    return jnp.concatenate(outs, axis=-1) * (1.0 / HEAD_DIM)


def _tile_lanes(t, n):
    return jnp.concatenate([t] * (n // t.shape[-1]), axis=-1)


def _rope(x, cos, sin_signed):
    n = x.shape[-1]
    lane = lax.broadcasted_iota(jnp.int32, x.shape, 1)
    first = (lane & (HEAD_DIM - 1)) < (HEAD_DIM // 2)
    partner = jnp.where(first, pltpu.roll(x, n - HEAD_DIM // 2, 1), pltpu.roll(x, HEAD_DIM // 2, 1))
    return x * cos + partner * sin_signed


def _mlp_kernel(x_ref, g_ref, w1_ref, w2_ref, o_ref):
    x = x_ref[...]
    h = _rms(x, g_ref[...]).astype(BF16)
    acc = x
    for c in range(D_FF // FF_CHUNK):
        sl = slice(FF_CHUNK * c, FF_CHUNK * (c + 1))
        a = jnp.square(jnp.maximum(_dot(h, w1_ref[:, sl]), 0.0)).astype(BF16)
        acc = acc + _dot(a, w2_ref[sl, :])
    o_ref[...] = acc


def _mlp(x, g, w1, w2):
    rows = x.shape[0]
    tm = min(MLP_ROWS, rows)
    return pl.pallas_call(
        _mlp_kernel,
        grid=(rows // tm,),
        in_specs=[pl.BlockSpec((tm, D_MODEL), lambda i: (i, 0)),
                  _const_spec((1, D_MODEL)),
                  _const_spec((D_MODEL, D_FF)),
                  _const_spec((D_FF, D_MODEL))],
        out_specs=pl.BlockSpec((tm, D_MODEL), lambda i: (i, 0)),
        out_shape=jax.ShapeDtypeStruct((rows, D_MODEL), F32),
        compiler_params=_params(("parallel",)),
        name="mlp",
    )(x, g, w1, w2)


def _s5_tail(x, u, y, d_ref, wglu_ref, bglu_ref):
    y = y + d_ref[...] * u
    g = jax.nn.gelu(y).astype(BF16)
    z = _dot(g, wglu_ref[...]) + bglu_ref[...]
    return x + z[:, :D_MODEL] * jax.nn.sigmoid(z[:, D_MODEL:])


def _s5_prompt_kernel(x_ref, gn_ref, bhat_ref, lam_ref, chat_ref, d_ref, wglu_ref, bglu_ref,
                      o_ref, fin_ref, st_ref, carry_ref, *, batch):
    assert 2 * batch == SUBLANES

    @pl.when(pl.program_id(0) == 0)
    def _():
        carry_ref[...] = jnp.zeros_like(carry_ref)

    x = x_ref[...]
    u = _rms(x, gn_ref[...])
    ub = u.astype(BF16)
    for i in range(N_CH_BLOCKS):
        st_ref[:, 2 * STATE_BLOCK * i:2 * STATE_BLOCK * (i + 1)] = _dot(
            ub[:, MXU_TILE * i:MXU_TILE * (i + 1)], bhat_ref[i])

    lo = lax.broadcasted_iota(jnp.int32, (SUBLANES, LANES), 0) < batch
    n_vregs = x.shape[0] // SUBLANES
    cols_per_loop = 4
    for i in range(N_CH_BLOCKS):
        for jg in range(STATE_BLOCK // LANES // cols_per_loop):
            re_cols, im_cols, lam_r, lam_i = [], [], [], []
            for j in range(cols_per_loop):
                c = LANES * (cols_per_loop * jg + j)
                re_cols.append(2 * STATE_BLOCK * i + c)
                im_cols.append(2 * STATE_BLOCK * i + STATE_BLOCK + c)
                lc = STATE_BLOCK * i + c
                lam_r.append(jnp.broadcast_to(lam_ref[0:1, lc:lc + LANES], (SUBLANES, LANES)))
                lam_i.append(jnp.broadcast_to(lam_ref[1:2, lc:lc + LANES], (SUBLANES, LANES)))
            init = tuple(carry_ref[:, c:c + LANES] for c in re_cols + im_cols)

            def body(m, carry, re_cols=re_cols, im_cols=im_cols, lam_r=lam_r, lam_i=lam_i):
                r0 = pl.multiple_of(m * SUBLANES, SUBLANES)
                new_r, new_i = [], []
                for j in range(cols_per_loop):
                    xr = st_ref[pl.ds(r0, SUBLANES), re_cols[j]:re_cols[j] + LANES]
                    xi = st_ref[pl.ds(r0, SUBLANES), im_cols[j]:im_cols[j] + LANES]

                    def step(pr, pi, j=j, xr=xr, xi=xi):
                        rr = pltpu.roll(pr, batch, 0)
                        ri = pltpu.roll(pi, batch, 0)
                        return (lam_r[j] * rr - lam_i[j] * ri + xr,
                                lam_r[j] * ri + lam_i[j] * rr + xi)

                    c1r, c1i = step(carry[j], carry[cols_per_loop + j])
                    c2r, c2i = step(c1r, c1i)
                    vr = jnp.where(lo, c1r, c2r)
                    vi = jnp.where(lo, c1i, c2i)
                    st_ref[pl.ds(r0, SUBLANES), re_cols[j]:re_cols[j] + LANES] = vr
                    st_ref[pl.ds(r0, SUBLANES), im_cols[j]:im_cols[j] + LANES] = vi
                    new_r.append(vr)
                    new_i.append(vi)
                return tuple(new_r + new_i)

            fin = lax.fori_loop(0, n_vregs, body, init)
            for j, c in enumerate(re_cols + im_cols):
                carry_ref[:, c:c + LANES] = fin[j]

    ys = []
    for i in range(N_CH_BLOCKS):
        sb = st_ref[:, 2 * STATE_BLOCK * i:2 * STATE_BLOCK * (i + 1)].astype(BF16)
        ys.append(_dot(sb, chat_ref[i]))
    y = jnp.concatenate(ys, axis=-1)
    o_ref[...] = _s5_tail(x, u, y, d_ref, wglu_ref, bglu_ref)
    fin_ref[...] = carry_ref[...]


def _s5_prompt(x_tb, batch, gn, bhat, lam, chat, d, wglu, bglu):
    rows = x_tb.shape[0]
    tm = S5_TIME_CHUNK * batch
    return pl.pallas_call(
        functools.partial(_s5_prompt_kernel, batch=batch),
        grid=(rows // tm,),
        in_specs=[pl.BlockSpec((tm, D_MODEL), lambda i: (i, 0)),
                  _const_spec((1, D_MODEL)),
                  _const_spec(bhat.shape),
                  _const_spec(lam.shape),
                  _const_spec(chat.shape),
                  _const_spec((1, D_MODEL)),
                  _const_spec(wglu.shape),
                  _const_spec(bglu.shape)],
        out_specs=[pl.BlockSpec((tm, D_MODEL), lambda i: (i, 0)),
                   pl.BlockSpec((SUBLANES, STATE_LANES), lambda i: (0, 0))],
        out_shape=[jax.ShapeDtypeStruct((rows, D_MODEL), F32),
                   jax.ShapeDtypeStruct((SUBLANES, STATE_LANES), F32)],
        scratch_shapes=[pltpu.VMEM((tm, STATE_LANES), F32),
                        pltpu.VMEM((SUBLANES, STATE_LANES), F32)],
        compiler_params=_params(("arbitrary",)),
        name="s5_prompt",
    )(x_tb, gn, bhat, lam, chat, d, wglu, bglu)


def _s5_sample_kernel(x_ref, gn_ref, bhat_ref, lam_ref, chat_ref, d_ref, wglu_ref, bglu_ref, s0_ref,
                      o_ref, snew_ref):
    x = x_ref[...]
    u = _rms(x, gn_ref[...])
    ys = []
    for i in range(N_CH_BLOCKS):
        base = 2 * STATE_BLOCK * i
        bu = jnp.dot(u[:, MXU_TILE * i:MXU_TILE * (i + 1)], bhat_ref[i],
                     preferred_element_type=F32, precision=lax.Precision.HIGHEST)
        lr = lam_ref[0:1, STATE_BLOCK * i:STATE_BLOCK * (i + 1)]
        li = lam_ref[1:2, STATE_BLOCK * i:STATE_BLOCK * (i + 1)]
        s0r = s0_ref[:, base:base + STATE_BLOCK]
        s0i = s0_ref[:, base + STATE_BLOCK:base + 2 * STATE_BLOCK]
        sr = lr * s0r - li * s0i + bu[:, :STATE_BLOCK]
        si = lr * s0i + li * s0r + bu[:, STATE_BLOCK:]
        snew_ref[:, base:base + STATE_BLOCK] = sr
        snew_ref[:, base + STATE_BLOCK:base + 2 * STATE_BLOCK] = si
        sb = jnp.concatenate([sr, si], axis=-1).astype(BF16)
        ys.append(_dot(sb, chat_ref[i]))
    y = jnp.concatenate(ys, axis=-1)
    o_ref[...] = _s5_tail(x, u, y, d_ref, wglu_ref, bglu_ref)


def _s5_sample(x, gn, bhat32, lam, chat, d, wglu, bglu, s0):
    rows = x.shape[0]
    return pl.pallas_call(
        _s5_sample_kernel,
        out_shape=[jax.ShapeDtypeStruct((rows, D_MODEL), F32),
                   jax.ShapeDtypeStruct((rows, STATE_LANES), F32)],
        compiler_params=pltpu.CompilerParams(vmem_limit_bytes=VMEM_LIMIT_BYTES),
        name="s5_sample",
    )(x, gn, bhat32, lam, chat, d, wglu, bglu, s0)


def _s5_weights(a_re, a_im, log_dt, b_re, b_im, c_re, c_im):
    dt = jnp.exp(log_dt.astype(F32))
    lam = lax.complex(a_re.astype(F32), a_im.astype(F32))
    lam_bar = jnp.exp(lam * dt)
    b = lax.complex(b_re.astype(F32), b_im.astype(F32))
    b_bar = ((lam_bar - 1.0) / lam)[..., None] * b
    groups_per_block = MXU_TILE // GROUP_SIZE
    eye = jnp.eye(groups_per_block, dtype=F32)

    def expand_in(w):
        w4 = w.reshape(N_CH_BLOCKS, groups_per_block, STATE_DIM, GROUP_SIZE)
        return jnp.einsum("igpc,gh->igchp", w4, eye).reshape(N_CH_BLOCKS, MXU_TILE, STATE_BLOCK)

    def expand_out(w):
        w4 = w.reshape(N_CH_BLOCKS, groups_per_block, GROUP_SIZE, STATE_DIM)
        return jnp.einsum("igcp,gh->ihpgc", w4, eye).reshape(N_CH_BLOCKS, STATE_BLOCK, MXU_TILE)

    bhat = jnp.concatenate([expand_in(jnp.real(b_bar)), expand_in(jnp.imag(b_bar))], axis=-1)
    chat = jnp.concatenate([expand_out(c_re.astype(F32)), -expand_out(c_im.astype(F32))], axis=1)
    lam_rows = jnp.stack([jnp.real(lam_bar).reshape(-1), jnp.imag(lam_bar).reshape(-1)])
    return bhat, chat.astype(BF16), lam_rows


def _state_to_lanes(s_re, s_im):
    n = s_re.shape[0]
    re = s_re.astype(F32).reshape(n, N_CH_BLOCKS, STATE_BLOCK)
    im = s_im.astype(F32).reshape(n, N_CH_BLOCKS, STATE_BLOCK)
    return jnp.concatenate([re, im], axis=-1).reshape(n, STATE_LANES)


def _lanes_to_state(s):
    n = s.shape[0]
    s4 = s.reshape(n, N_CH_BLOCKS, 2, STATE_BLOCK)
    return (s4[:, :, 0].reshape(n, N_GROUPS, STATE_DIM), s4[:, :, 1].reshape(n, N_GROUPS, STATE_DIM))


def _kv_kernel(x_ref, g_ref, wkv_ref, kg_ref, ones_ref, cos_ref, sin_ref, k_ref, v_ref):
    h = _rms(x_ref[...], g_ref[...]).astype(BF16)
    kv = _dot(h, wkv_ref[...])
    n = k_ref.shape[-1]
    k = kv[:, :n]
    k = k * lax.rsqrt(_seg_mean_sq(k, ones_ref) + EPS) * kg_ref[...]
    k_ref[...] = _rope(k, _tile_lanes(cos_ref[...], n), _tile_lanes(sin_ref[...], n))
    v_ref[...] = kv[:, n:]


def _kv(x, g, wkv, kgain, ones, cos, sin, table_block):
    rows = x.shape[0]
    n = wkv.shape[1] // 2
    tm = min(KV_ROWS, rows)
    return pl.pallas_call(
        _kv_kernel,
        grid=(rows // tm,),
        in_specs=[pl.BlockSpec((tm, D_MODEL), lambda i: (i, 0)),
                  _const_spec((1, D_MODEL)),
                  _const_spec(wkv.shape),
                  _const_spec((1, n)),
                  _const_spec(ones.shape),
                  pl.BlockSpec((tm, LANES), lambda i: (table_block(i), 0)),
                  pl.BlockSpec((tm, LANES), lambda i: (table_block(i), 0))],
        out_specs=[pl.BlockSpec((tm, n), lambda i: (i, 0)),
                   pl.BlockSpec((tm, n), lambda i: (i, 0))],
        out_shape=[jax.ShapeDtypeStruct((rows, n), F32),
                   jax.ShapeDtypeStruct((rows, n), F32)],
        compiler_params=_params(("parallel",)),
        name="kv_proj",
    )(x, g, wkv, kgain, ones, cos, sin)


def _attn_prompt_kernel(sinks_ref, x_ref, g_ref, wq_ref, qg_ref, ones_ref, cos_ref, sin_ref,
                        kp_ref, kc_ref, vp_ref, vc_ref, wo_ref, o_ref):
    qblk = pl.program_id(1)
    x = x_ref[...]
    h = _rms(x, g_ref[...]).astype(BF16)
    q = _dot(h, wq_ref[...])
    q = q * lax.rsqrt(_seg_mean_sq(q, ones_ref) + EPS) * qg_ref[...]
    q = _rope(q, _tile_lanes(cos_ref[...], D_MODEL), _tile_lanes(sin_ref[...], D_MODEL))
    qb = (q * (HEAD_DIM ** -0.5)).astype(BF16)

    kcat = jnp.concatenate([kp_ref[...], kc_ref[...]], axis=0).astype(BF16)
    vcat = jnp.concatenate([vp_ref[...], vc_ref[...]], axis=0).astype(BF16)

    qi = lax.broadcasted_iota(jnp.int32, (WINDOW, 2 * WINDOW), 0)
    kj = lax.broadcasted_iota(jnp.int32, (WINDOW, 2 * WINDOW), 1)
    valid = (kj > qi) & (kj <= qi + WINDOW) & ((kj >= WINDOW) | (qblk > 0))
    low_half = lax.broadcasted_iota(jnp.int32, (WINDOW, LANES), 1) < HEAD_DIM

    outs = []
    for pb in range(N_HEADS // 2):
        kvh = pb // (N_HEADS // N_KV_HEADS // 2)
        qp = qb[:, LANES * pb:LANES * (pb + 1)]
        kg = kcat[:, LANES * kvh:LANES * (kvh + 1)]
        vg = vcat[:, LANES * kvh:LANES * (kvh + 1)]
        halves = []
        for par in range(2):
            keep = low_half if par == 0 else jnp.logical_not(low_half)
            qm = jnp.where(keep, qp, jnp.zeros_like(qp))
            s = lax.dot_general(qm, kg, (((1,), (1,)), ((), ())), preferred_element_type=F32)
            s = jnp.where(valid, s, -jnp.inf)
            sink = sinks_ref[2 * pb + par]
            m = jnp.maximum(jnp.max(s, axis=-1, keepdims=True), sink)
            p = jnp.exp(s - m)
            denom = jnp.sum(p, axis=-1, keepdims=True) + jnp.exp(sink - m)
            halves.append(_dot(p.astype(BF16), vg) / denom)
        outs.append(jnp.where(low_half, halves[0], halves[1]))
    attn = jnp.concatenate(outs, axis=-1).astype(BF16)
    o_ref[...] = x + _dot(attn, wo_ref[...])


def _attn_prompt(x, g, wq, qgain, ones, cos, sin, kd, vd, sinks, wo):
    bsz, seq, _ = x.shape
    nkv = kd.shape[-1]
    x_spec = pl.BlockSpec((None, WINDOW, D_MODEL), lambda b, i, *_: (b, i, 0))
    cur = pl.BlockSpec((None, WINDOW, nkv), lambda b, i, *_: (b, i, 0))
    prev = pl.BlockSpec((None, WINDOW, nkv), lambda b, i, *_: (b, jnp.maximum(i - 1, 0), 0))
    tab = pl.BlockSpec((WINDOW, LANES), lambda b, i, *_: (i, 0))
    grid_spec = pltpu.PrefetchScalarGridSpec(
        num_scalar_prefetch=1,
        grid=(bsz, seq // WINDOW),
        in_specs=[x_spec,
                  _const_spec((1, D_MODEL)),
                  _const_spec(wq.shape),
                  _const_spec((1, D_MODEL)),
                  _const_spec(ones.shape),
                  tab, tab, prev, cur, prev, cur,
                  _const_spec(wo.shape)],
        out_specs=x_spec)
    return pl.pallas_call(
        _attn_prompt_kernel,
        grid_spec=grid_spec,
        out_shape=jax.ShapeDtypeStruct(x.shape, F32),
        compiler_params=_params(("parallel", "arbitrary")),
        name="attn_prompt",
    )(sinks, x, g, wq, qgain, ones, cos, sin, kd, kd, vd, vd, wo)


def _q_sample_kernel(x_ref, g_ref, wq_ref, qg_ref, ones_ref, cos_ref, sin_ref, q_ref):
    h = _rms(x_ref[...], g_ref[...]).astype(BF16)
    q = _dot(h, wq_ref[...])
    n = q.shape[-1]
    q = q * lax.rsqrt(_seg_mean_sq(q, ones_ref) + EPS) * qg_ref[...]
    q = _rope(q, _tile_lanes(cos_ref[...], n), _tile_lanes(sin_ref[...], n))
    q_ref[...] = q * (HEAD_DIM ** -0.5)


def _q_sample(x, g, wq_pad, qgain_pad, ones, cos, sin):
    rows = x.shape[0]
    return pl.pallas_call(
        _q_sample_kernel,
        out_shape=jax.ShapeDtypeStruct((rows, wq_pad.shape[1]), F32),
        compiler_params=pltpu.CompilerParams(vmem_limit_bytes=VMEM_LIMIT_BYTES),
        name="q_sample",
    )(x, g, wq_pad, qgain_pad, ones, cos, sin)


def _attn_sample_kernel(q_ref, k_ref, v_ref, sink_ref, o_ref):
    sink = sink_ref[...]
    for b in range(q_ref.shape[0]):
        s = lax.dot_general(q_ref[b].astype(BF16), k_ref[b].astype(BF16), (((1,), (1,)), ((), ())),
                            preferred_element_type=F32)
        m = jnp.maximum(jnp.max(s, axis=-1, keepdims=True), sink)
        p = jnp.exp(s - m)
        denom = jnp.sum(p, axis=-1, keepdims=True) + jnp.exp(sink - m)
        o_ref[b] = _dot(p.astype(BF16), v_ref[b].astype(BF16)) / denom


def _attn_sample(q3, k, v, sink_col):
    bsz, nh, nkv = q3.shape
    keys = k.shape[1]
    bb = SUBLANES
    return pl.pallas_call(
        _attn_sample_kernel,
        grid=(bsz // bb,),
        in_specs=[pl.BlockSpec((bb, nh, nkv), lambda i: (i, 0, 0)),
                  pl.BlockSpec((bb, keys, nkv), lambda i: (i, 0, 0)),
                  pl.BlockSpec((bb, keys, nkv), lambda i: (i, 0, 0)),
                  _const_spec(sink_col.shape)],
        out_specs=pl.BlockSpec((bb, nh, nkv), lambda i: (i, 0, 0)),
        out_shape=jax.ShapeDtypeStruct((bsz, nh, nkv), F32),
        compiler_params=_params(("parallel",)),
        name="attn_sample",
    )(q3, k, v, sink_col)


def _proj_residual_kernel(x_ref, a_ref, w_ref, o_ref):
    o_ref[...] = x_ref[...] + _dot(a_ref[...].astype(BF16), w_ref[...])


def _proj_residual(x, a, w):
    return pl.pallas_call(
        _proj_residual_kernel,
        out_shape=jax.ShapeDtypeStruct(x.shape, F32),
        compiler_params=pltpu.CompilerParams(vmem_limit_bytes=VMEM_LIMIT_BYTES),
        name="o_proj_sample",
    )(x, a, w)


def _rope_tables(pos):
    half = HEAD_DIM // 2
    inv = ROPE_THETA ** (-jnp.arange(half, dtype=F32) / half)
    ang = pos.astype(F32)[:, None] * inv[None, :]
    cos, sin = jnp.cos(ang), jnp.sin(ang)
    reps = LANES // HEAD_DIM
    return (jnp.tile(jnp.concatenate([cos, cos], axis=-1), (1, reps)),
            jnp.tile(jnp.concatenate([-sin, sin], axis=-1), (1, reps)))


def _block_ones(width, block):
    idx = jnp.arange(width) // block
    return (idx[:, None] == idx[None, :]).astype(BF16)


def _dup_heads(w):
    w3 = w.reshape(w.shape[0], N_KV_HEADS, 1, HEAD_DIM)
    return jnp.broadcast_to(w3, (w.shape[0], N_KV_HEADS, LANES // HEAD_DIM, HEAD_DIM)).reshape(w.shape[0], -1)


def _pad_query_heads(w):
    lead = w.shape[:-1]
    nkv = N_KV_HEADS * HEAD_DIM
    w3 = w.reshape(lead + (N_KV_HEADS, N_HEADS // N_KV_HEADS, 1, HEAD_DIM))
    sel = jnp.eye(N_KV_HEADS, dtype=w.dtype)[:, None, :, None]
    return (w3 * sel).reshape(lead + (N_HEADS * nkv,))


def kernel(x_prompt, x_sample, state_ssm_re, state_ssm_im, cache_k, cache_v, norm_mix, norm_mlp, ssm_a_re, ssm_a_im, ssm_log_dt, ssm_b_re, ssm_b_im, ssm_c_re, ssm_c_im, ssm_d, w_glu, b_glu, norm_kv, w_k, w_v, k_norm, w_q, q_norm, attn_sinks, w_o, w_mlp_in, w_mlp_out):
    bsz, seq, _ = x_prompt.shape
    dec = x_sample.shape[0]
    n_a = ssm_a_re.shape[0]
    depth = norm_mix.shape[0]
    past = 8192
    nkv = N_KV_HEADS * HEAD_DIM
    row = lambda v: v.astype(F32).reshape(1, -1)

    xp = jnp.transpose(x_prompt, (1, 0, 2)).reshape(seq * bsz, D_MODEL)
    xs = x_sample.reshape(dec, D_MODEL)

    w1 = w_mlp_in.astype(BF16)
    w2 = w_mlp_out.astype(BF16)
    wglu = w_glu.astype(BF16)

    sp_re, sp_im, ss_re, ss_im = [], [], [], []
    for i in range(n_a):
        bhat, chat, lam = _s5_weights(ssm_a_re[i], ssm_a_im[i], ssm_log_dt[i], ssm_b_re[i], ssm_b_im[i],
                                      ssm_c_re[i], ssm_c_im[i])
        gn, d, bg = row(norm_mix[i]), row(ssm_d[i]), row(b_glu[i])
        xp, fin = _s5_prompt(xp, bsz, gn, bhat.astype(BF16), lam, chat, d, wglu[i], bg)
        re, im = _lanes_to_state(fin[bsz:])
        sp_re.append(re)
        sp_im.append(im)
        s0 = _state_to_lanes(state_ssm_re[i], state_ssm_im[i])
        xs, snew = _s5_sample(xs, gn, bhat, lam, chat, d, wglu[i], bg, s0)
        re, im = _lanes_to_state(snew)
        ss_re.append(re)
        ss_im.append(im)
        xp = _mlp(xp, row(norm_mlp[i]), w1[i], w2[i])
        xs = _mlp(xs, row(norm_mlp[i]), w1[i], w2[i])

    xp = jnp.transpose(xp.reshape(seq, bsz, D_MODEL), (1, 0, 2))

    cos_p, sin_p = _rope_tables(jnp.arange(seq, dtype=jnp.int32))
    cos_s, sin_s = _rope_tables(jnp.full((dec,), past, dtype=jnp.int32))
    ones_head = _block_ones(MXU_TILE, HEAD_DIM)
    ones_tile = _block_ones(MXU_TILE, MXU_TILE)
    kgain = row(k_norm)
    tile_gain = lambda gvec, n: jnp.tile(gvec, (1, n // HEAD_DIM))

    wkv_dup = jnp.concatenate([_dup_heads(w_k), _dup_heads(w_v)], axis=1).astype(BF16)
    wkv_std = jnp.concatenate([w_k, w_v], axis=1).astype(BF16)
    kd, vd = _kv(xp.reshape(bsz * seq, D_MODEL), row(norm_kv), wkv_dup, tile_gain(kgain, 2 * nkv), ones_head,
                 cos_p, sin_p, lambda i: i % (seq // KV_ROWS))
    ks_new, vs_new = _kv(xs, row(norm_kv), wkv_std, tile_gain(kgain, nkv), ones_head, cos_s, sin_s,
                         lambda i: i)
    kd = kd.reshape(bsz, seq, 2 * nkv)
    vd = vd.reshape(bsz, seq, 2 * nkv)
    undup = lambda t: t[:, -WINDOW:].reshape(bsz, WINDOW, N_KV_HEADS, LANES // HEAD_DIM, HEAD_DIM)[:, :, :, 0]
    new_k_p, new_v_p = undup(kd), undup(vd)
    new_k_s = jnp.concatenate([cache_k.astype(F32)[:, 1:], ks_new.reshape(dec, 1, N_KV_HEADS, HEAD_DIM)], axis=1)
    new_v_s = jnp.concatenate([cache_v.astype(F32)[:, 1:], vs_new.reshape(dec, 1, N_KV_HEADS, HEAD_DIM)], axis=1)
    keys_s = new_k_s.reshape(dec, WINDOW, nkv)
    vals_s = new_v_s.reshape(dec, WINDOW, nkv)

    for j in range(depth - n_a):
        layer = n_a + j
        gn = row(norm_mix[layer])
        qgain = row(q_norm[j])
        wq = w_q[j].astype(BF16)
        wo = w_o[j].astype(BF16)
        xp = _attn_prompt(xp, gn, wq, tile_gain(qgain, D_MODEL), ones_head, cos_p, sin_p, kd, vd,
                          attn_sinks[j].astype(F32), wo)
        q_pad = _q_sample(xs, gn, _pad_query_heads(w_q[j]).astype(BF16),
                          _pad_query_heads(tile_gain(qgain, D_MODEL)), ones_tile, cos_s, sin_s)
        o_pad = _attn_sample(q_pad.reshape(dec, N_HEADS, nkv), keys_s, vals_s,
                             attn_sinks[j].astype(F32).reshape(N_HEADS, 1))
        wo_pad = _pad_query_heads(w_o[j].T).T.astype(BF16)
        xs = _proj_residual(xs, o_pad.reshape(dec, N_HEADS * nkv), wo_pad)
        xp2 = _mlp(xp.reshape(bsz * seq, D_MODEL), row(norm_mlp[layer]), w1[layer], w2[layer])
        xp = xp2.reshape(bsz, seq, D_MODEL)
        xs = _mlp(xs, row(norm_mlp[layer]), w1[layer], w2[layer])

    return (xp, xs.reshape(dec, 1, D_MODEL),
            jnp.stack(sp_re), jnp.stack(sp_im), new_k_p, new_v_p,
            jnp.stack(ss_re), jnp.stack(ss_im), new_k_s, new_v_s)
```

```python
import functools
import math

import jax
import jax.numpy as jnp
from jax import lax
from jax.experimental import pallas as pl
from jax.experimental.pallas import tpu as pltpu

F32 = jnp.float32
BF16 = jnp.bfloat16

D_MODEL = 1024
N_GROUPS = 64
GROUP_SIZE = 16
STATE_DIM = 64
HEAD_DIM = 64
N_HEADS = 16
N_KV_HEADS = 4
WINDOW = 128
ROPE_THETA = 10000.0
D_FF = 4 * D_MODEL
EPS = 1e-6

SUBLANES = 8
LANES = 128
MXU_TILE = 256
VMEM_LIMIT_BYTES = 56 * 1024 * 1024

N_CH_BLOCKS = D_MODEL // MXU_TILE
STATE_BLOCK = (MXU_TILE // GROUP_SIZE) * STATE_DIM
STATE_LANES = 2 * STATE_BLOCK * N_CH_BLOCKS

S5_TIME_CHUNK = 128
MLP_ROWS = 512
FF_CHUNK = 1024
KV_ROWS = 512


def _const_spec(shape):
    zeros = (0,) * len(shape)
    return pl.BlockSpec(shape, lambda *_: zeros, pipeline_mode=pl.Buffered(1))


def _layer_spec(shape, layer):
    idx = (layer,) + (0,) * (len(shape) - 1)
    return pl.BlockSpec((None,) + tuple(shape[1:]), lambda *_: idx, pipeline_mode=pl.Buffered(1))


def _params(sem):
    return pltpu.CompilerParams(dimension_semantics=sem, vmem_limit_bytes=VMEM_LIMIT_BYTES)


def _rms(x, g):
    return x * lax.rsqrt(jnp.mean(x * x, axis=-1, keepdims=True) + EPS) * g


def _dot(a, b):
    return jnp.dot(a, b, preferred_element_type=F32)


def _seg_mean_sq(x, ones_ref):
    sq = x * x
    hi = sq.astype(BF16)
    lo = (sq - hi.astype(F32)).astype(BF16)
    ones = ones_ref[...]
    outs = []
    for t in range(x.shape[-1] // MXU_TILE):
        sl = slice(MXU_TILE * t, MXU_TILE * (t + 1))
        outs.append(_dot(hi[:, sl], ones) + _dot(lo[:, sl], ones))
    return jnp.concatenate(outs, axis=-1) * (1.0 / HEAD_DIM)


def _tile_lanes(t, n):
    return jnp.concatenate([t] * (n // t.shape[-1]), axis=-1)


def _rope(x, cos, sin_signed):
    n = x.shape[-1]
    lane = lax.broadcasted_iota(jnp.int32, x.shape, 1)
    first = (lane & (HEAD_DIM - 1)) < (HEAD_DIM // 2)
    partner = jnp.where(first, pltpu.roll(x, n - HEAD_DIM // 2, 1), pltpu.roll(x, HEAD_DIM // 2, 1))
    return x * cos + partner * sin_signed


def _mlp_kernel(x_ref, g_ref, w1_ref, w2_ref, o_ref):
    x = x_ref[...]
    h = _rms(x, g_ref[...]).astype(BF16)
    acc = x
    for c in range(D_FF // FF_CHUNK):
        sl = slice(FF_CHUNK * c, FF_CHUNK * (c + 1))
        a = jnp.square(jnp.maximum(_dot(h, w1_ref[:, sl]), 0.0)).astype(BF16)
        acc = acc + _dot(a, w2_ref[sl, :])
    o_ref[...] = acc


def _mlp(x, g, w1, w2, layer):
    rows = x.shape[0]
    tm = min(MLP_ROWS, rows)
    return pl.pallas_call(
        _mlp_kernel,
        grid=(rows // tm,),
        in_specs=[pl.BlockSpec((tm, D_MODEL), lambda i: (i, 0)),
                  _layer_spec(g.shape, layer),
                  _layer_spec(w1.shape, layer),
                  _layer_spec(w2.shape, layer)],
        out_specs=pl.BlockSpec((tm, D_MODEL), lambda i: (i, 0)),
        out_shape=jax.ShapeDtypeStruct((rows, D_MODEL), F32),
        compiler_params=_params(("parallel",)),
        name="mlp",
    )(x, g, w1, w2)


def _s5_tail(u, y, d_ref, wglu_ref, bglu_ref):
    y = y + d_ref[...] * u
    g = jax.nn.gelu(y).astype(BF16)
    z = _dot(g, wglu_ref[...]) + bglu_ref[...]
    return z[:, :D_MODEL] * jax.nn.sigmoid(z[:, D_MODEL:])


def _s5_prompt_kernel(x_ref, gn_ref, bhat_ref, lam_ref, chat_ref, d_ref, wglu_ref, bglu_ref,
                      o_ref, fin_ref, st_ref, carry_ref, tb_ref):
    batch, tc, _ = x_ref.shape
    assert 2 * batch == SUBLANES
    n_slabs = D_MODEL // LANES

    @pl.when(pl.program_id(0) == 0)
    def _():
        carry_ref[...] = jnp.zeros_like(carry_ref)

    for b in range(batch):
        ub_nat = _rms(x_ref[b], gn_ref[...])
        for c in range(n_slabs):
            tb_ref[c, pl.ds(b, tc, stride=batch), :] = ub_nat[:, LANES * c:LANES * (c + 1)]
    u = jnp.concatenate([tb_ref[c] for c in range(n_slabs)], axis=-1)
    ub = u.astype(BF16)
    for i in range(N_CH_BLOCKS):
        st_ref[:, 2 * STATE_BLOCK * i:2 * STATE_BLOCK * (i + 1)] = _dot(
            ub[:, MXU_TILE * i:MXU_TILE * (i + 1)], bhat_ref[i])

    lo = lax.broadcasted_iota(jnp.int32, (SUBLANES, LANES), 0) < batch
    n_vregs = batch * tc // SUBLANES
    cols_per_loop = 4
    for i in range(N_CH_BLOCKS):
        for jg in range(STATE_BLOCK // LANES // cols_per_loop):
            re_cols, im_cols, lam_r, lam_i = [], [], [], []
            for j in range(cols_per_loop):
                c = LANES * (cols_per_loop * jg + j)
                re_cols.append(2 * STATE_BLOCK * i + c)
                im_cols.append(2 * STATE_BLOCK * i + STATE_BLOCK + c)
                lc = STATE_BLOCK * i + c
                lam_r.append(jnp.broadcast_to(lam_ref[0:1, lc:lc + LANES], (SUBLANES, LANES)))
                lam_i.append(jnp.broadcast_to(lam_ref[1:2, lc:lc + LANES], (SUBLANES, LANES)))
            init = tuple(carry_ref[:, c:c + LANES] for c in re_cols + im_cols)

            def body(m, carry, re_cols=re_cols, im_cols=im_cols, lam_r=lam_r, lam_i=lam_i):
                r0 = pl.multiple_of(m * SUBLANES, SUBLANES)
                new_r, new_i = [], []
                for j in range(cols_per_loop):
                    xr = st_ref[pl.ds(r0, SUBLANES), re_cols[j]:re_cols[j] + LANES]
                    xi = st_ref[pl.ds(r0, SUBLANES), im_cols[j]:im_cols[j] + LANES]

                    def step(pr, pi, j=j, xr=xr, xi=xi):
                        rr = pltpu.roll(pr, batch, 0)
                        ri = pltpu.roll(pi, batch, 0)
                        return (lam_r[j] * rr - lam_i[j] * ri + xr,
                                lam_r[j] * ri + lam_i[j] * rr + xi)

                    c1r, c1i = step(carry[j], carry[cols_per_loop + j])
                    c2r, c2i = step(c1r, c1i)
                    vr = jnp.where(lo, c1r, c2r)
                    vi = jnp.where(lo, c1i, c2i)
                    st_ref[pl.ds(r0, SUBLANES), re_cols[j]:re_cols[j] + LANES] = vr
                    st_ref[pl.ds(r0, SUBLANES), im_cols[j]:im_cols[j] + LANES] = vi
                    new_r.append(vr)
                    new_i.append(vi)
                return tuple(new_r + new_i)

            fin = lax.fori_loop(0, n_vregs, body, init)
            for j, c in enumerate(re_cols + im_cols):
                carry_ref[:, c:c + LANES] = fin[j]

    ys = []
    for i in range(N_CH_BLOCKS):
        sb = st_ref[:, 2 * STATE_BLOCK * i:2 * STATE_BLOCK * (i + 1)].astype(BF16)
        ys.append(_dot(sb, chat_ref[i]))
    y = jnp.concatenate(ys, axis=-1)
    u = jnp.concatenate([tb_ref[c] for c in range(n_slabs)], axis=-1)
    upd = _s5_tail(u, y, d_ref, wglu_ref, bglu_ref)
    for c in range(n_slabs):
        tb_ref[c] = upd[:, LANES * c:LANES * (c + 1)]
    for b in range(batch):
        o_ref[b] = x_ref[b] + jnp.concatenate(
            [tb_ref[c, pl.ds(b, tc, stride=batch), :] for c in range(n_slabs)], axis=-1)
    fin_ref[...] = carry_ref[...]


def _s5_prompt(x, layer, gn, bhat, lam, chat, d, wglu, bglu):
    batch, seq, _ = x.shape
    tc = S5_TIME_CHUNK
    x_spec = pl.BlockSpec((batch, tc, D_MODEL), lambda i: (0, i, 0))
    return pl.pallas_call(
        _s5_prompt_kernel,
        grid=(seq // tc,),
        in_specs=[x_spec] + [_layer_spec(w.shape, layer) for w in (gn, bhat, lam, chat, d, wglu, bglu)],
        out_specs=[x_spec, pl.BlockSpec((SUBLANES, STATE_LANES), lambda i: (0, 0))],
        out_shape=[jax.ShapeDtypeStruct(x.shape, F32),
                   jax.ShapeDtypeStruct((SUBLANES, STATE_LANES), F32)],
        scratch_shapes=[pltpu.VMEM((batch * tc, STATE_LANES), F32),
                        pltpu.VMEM((SUBLANES, STATE_LANES), F32),
                        pltpu.VMEM((D_MODEL // LANES, batch * tc, LANES), F32)],
        compiler_params=_params(("arbitrary",)),
        name="s5_prompt",
    )(x, gn, bhat, lam, chat, d, wglu, bglu)


def _s5_sample_kernel(x_ref, gn_ref, bhat_ref, lam_ref, chat_ref, d_ref, wglu_ref, bglu_ref, s0_ref,
                      o_ref, snew_ref):
    x = x_ref[...]
    u = _rms(x, gn_ref[...])
    ys = []
    for i in range(N_CH_BLOCKS):
        base = 2 * STATE_BLOCK * i
        bu = jnp.dot(u[:, MXU_TILE * i:MXU_TILE * (i + 1)], bhat_ref[i],
                     preferred_element_type=F32, precision=lax.Precision.HIGHEST)
        lr = lam_ref[0:1, STATE_BLOCK * i:STATE_BLOCK * (i + 1)]
        li = lam_ref[1:2, STATE_BLOCK * i:STATE_BLOCK * (i + 1)]
        s0r = s0_ref[:, base:base + STATE_BLOCK]
        s0i = s0_ref[:, base + STATE_BLOCK:base + 2 * STATE_BLOCK]
        sr = lr * s0r - li * s0i + bu[:, :STATE_BLOCK]
        si = lr * s0i + li * s0r + bu[:, STATE_BLOCK:]
        snew_ref[:, base:base + STATE_BLOCK] = sr
        snew_ref[:, base + STATE_BLOCK:base + 2 * STATE_BLOCK] = si
        sb = jnp.concatenate([sr, si], axis=-1).astype(BF16)
        ys.append(_dot(sb, chat_ref[i]))
    y = jnp.concatenate(ys, axis=-1)
    o_ref[...] = x + _s5_tail(u, y, d_ref, wglu_ref, bglu_ref)


def _s5_sample(x, layer, gn, bhat32, lam, chat, d, wglu, bglu, s0):
    rows = x.shape[0]
    return pl.pallas_call(
        _s5_sample_kernel,
        grid=(1,),
        in_specs=[_const_spec(x.shape)] + [_layer_spec(w.shape, layer) for w in (gn, bhat32, lam, chat, d, wglu, bglu)]
        + [_const_spec(s0.shape)],
        out_specs=[pl.BlockSpec((rows, D_MODEL), lambda i: (0, 0)),
                   pl.BlockSpec((rows, STATE_LANES), lambda i: (0, 0))],
        out_shape=[jax.ShapeDtypeStruct((rows, D_MODEL), F32),
                   jax.ShapeDtypeStruct((rows, STATE_LANES), F32)],
        compiler_params=_params(("arbitrary",)),
        name="s5_sample",
    )(x, gn, bhat32, lam, chat, d, wglu, bglu, s0)


def _s5_weights(a_re, a_im, log_dt, b_re, b_im, c_re, c_im):
    n = a_re.shape[0]
    dt = jnp.exp(log_dt.astype(F32))
    lam = lax.complex(a_re.astype(F32), a_im.astype(F32))
    lam_bar = jnp.exp(lam * dt)
    b = lax.complex(b_re.astype(F32), b_im.astype(F32))
    b_bar = ((lam_bar - 1.0) / lam)[..., None] * b
    gpb = MXU_TILE // GROUP_SIZE
    same_group = jnp.eye(gpb, dtype=F32)[None, None, :, None, :, None]

    def expand_in(w):
        w5 = jnp.swapaxes(w.reshape(n, N_CH_BLOCKS, gpb, STATE_DIM, GROUP_SIZE), 3, 4)
        return (w5[:, :, :, :, None, :] * same_group).reshape(n, N_CH_BLOCKS, MXU_TILE, STATE_BLOCK)

    def expand_out(w):
        w5 = jnp.swapaxes(w.reshape(n, N_CH_BLOCKS, gpb, GROUP_SIZE, STATE_DIM), 3, 4)
        return (w5[:, :, :, :, None, :] * same_group).reshape(n, N_CH_BLOCKS, STATE_BLOCK, MXU_TILE)

    bhat = jnp.concatenate([expand_in(jnp.real(b_bar)), expand_in(jnp.imag(b_bar))], axis=-1)
    chat = jnp.concatenate([expand_out(c_re.astype(F32)), -expand_out(c_im.astype(F32))], axis=2)
    lam_rows = jnp.stack([jnp.real(lam_bar).reshape(n, -1), jnp.imag(lam_bar).reshape(n, -1)], axis=1)
    return bhat, chat.astype(BF16), lam_rows


def _state_to_lanes(s_re, s_im):
    n = s_re.shape[0]
    re = s_re.astype(F32).reshape(n, N_CH_BLOCKS, STATE_BLOCK)
    im = s_im.astype(F32).reshape(n, N_CH_BLOCKS, STATE_BLOCK)
    return jnp.concatenate([re, im], axis=-1).reshape(n, STATE_LANES)


def _lanes_to_state(s):
    n = s.shape[0]
    s4 = s.reshape(n, N_CH_BLOCKS, 2, STATE_BLOCK)
    return (s4[:, :, 0].reshape(n, N_GROUPS, STATE_DIM), s4[:, :, 1].reshape(n, N_GROUPS, STATE_DIM))


def _kv_kernel(x_ref, g_ref, wkv_ref, kg_ref, ones_ref, cos_ref, sin_ref, k_ref, v_ref):
    h = _rms(x_ref[...], g_ref[...]).astype(BF16)
    kv = _dot(h, wkv_ref[...])
    n = k_ref.shape[-1]
    k = kv[:, :n]
    k = k * lax.rsqrt(_seg_mean_sq(k, ones_ref) + EPS) * kg_ref[...]
    k_ref[...] = _rope(k, _tile_lanes(cos_ref[...], n), _tile_lanes(sin_ref[...], n))
    v_ref[...] = kv[:, n:]


def _kv(x, g, wkv, kgain, ones, cos, sin, table_block):
    rows = x.shape[0]
    n = wkv.shape[1] // 2
    tm = min(KV_ROWS, rows)
    return pl.pallas_call(
        _kv_kernel,
        grid=(rows // tm,),
        in_specs=[pl.BlockSpec((tm, D_MODEL), lambda i: (i, 0)),
                  _const_spec((1, D_MODEL)),
                  _const_spec(wkv.shape),
                  _const_spec((1, n)),
                  _const_spec(ones.shape),
                  pl.BlockSpec((tm, LANES), lambda i: (table_block(i), 0)),
                  pl.BlockSpec((tm, LANES), lambda i: (table_block(i), 0))],
        out_specs=[pl.BlockSpec((tm, n), lambda i: (i, 0)),
                   pl.BlockSpec((tm, n), lambda i: (i, 0))],
        out_shape=[jax.ShapeDtypeStruct((rows, n), F32),
                   jax.ShapeDtypeStruct((rows, n), F32)],
        compiler_params=_params(("parallel",)),
        name="kv_proj",
    )(x, g, wkv, kgain, ones, cos, sin)


def _attn_prompt_kernel(sinks_ref, x_ref, g_ref, wq_ref, qg_ref, ones_ref, cos_ref, sin_ref,
                        kp_ref, kc_ref, vp_ref, vc_ref, wo_ref, o_ref):
    qblk = pl.program_id(1)
    x = x_ref[...]
    h = _rms(x, g_ref[...]).astype(BF16)
    q = _dot(h, wq_ref[...])
    q = q * lax.rsqrt(_seg_mean_sq(q, ones_ref) + EPS) * qg_ref[...]
    q = _rope(q, _tile_lanes(cos_ref[...], D_MODEL), _tile_lanes(sin_ref[...], D_MODEL))
    qb = (q * (HEAD_DIM ** -0.5)).astype(BF16)

    kcat = jnp.concatenate([kp_ref[...], kc_ref[...]], axis=0).astype(BF16)
    vcat = jnp.concatenate([vp_ref[...], vc_ref[...]], axis=0).astype(BF16)

    qi = lax.broadcasted_iota(jnp.int32, (WINDOW, 2 * WINDOW), 0)
    kj = lax.broadcasted_iota(jnp.int32, (WINDOW, 2 * WINDOW), 1)
    valid = (kj > qi) & (kj <= qi + WINDOW) & ((kj >= WINDOW) | (qblk > 0))
    low_half = lax.broadcasted_iota(jnp.int32, (WINDOW, LANES), 1) < HEAD_DIM

    outs = []
    for pb in range(N_HEADS // 2):
        kvh = pb // (N_HEADS // N_KV_HEADS // 2)
        qp = qb[:, LANES * pb:LANES * (pb + 1)]
        kg = kcat[:, LANES * kvh:LANES * (kvh + 1)]
        vg = vcat[:, LANES * kvh:LANES * (kvh + 1)]
        halves = []
        for par in range(2):
            keep = low_half if par == 0 else jnp.logical_not(low_half)
            qm = jnp.where(keep, qp, jnp.zeros_like(qp))
            s = lax.dot_general(qm, kg, (((1,), (1,)), ((), ())), preferred_element_type=F32)
            s = jnp.where(valid, s, -jnp.inf)
            sink = sinks_ref[2 * pb + par]
            m = jnp.maximum(jnp.max(s, axis=-1, keepdims=True), sink)
            p = jnp.exp(s - m)
            denom = jnp.sum(p, axis=-1, keepdims=True) + jnp.exp(sink - m)
            halves.append(_dot(p.astype(BF16), vg) / denom)
        outs.append(jnp.where(low_half, halves[0], halves[1]))
    attn = jnp.concatenate(outs, axis=-1).astype(BF16)
    o_ref[...] = x + _dot(attn, wo_ref[...])


def _attn_prompt(x, g, wq, qgain, ones, cos, sin, kd, vd, sinks, wo):
    bsz, seq, _ = x.shape
    nkv = kd.shape[-1]
    x_spec = pl.BlockSpec((None, WINDOW, D_MODEL), lambda b, i, *_: (b, i, 0))
    cur = pl.BlockSpec((None, WINDOW, nkv), lambda b, i, *_: (b, i, 0))
    prev = pl.BlockSpec((None, WINDOW, nkv), lambda b, i, *_: (b, jnp.maximum(i - 1, 0), 0))
    tab = pl.BlockSpec((WINDOW, LANES), lambda b, i, *_: (i, 0))
    grid_spec = pltpu.PrefetchScalarGridSpec(
        num_scalar_prefetch=1,
        grid=(bsz, seq // WINDOW),
        in_specs=[x_spec,
                  _const_spec((1, D_MODEL)),
                  _const_spec(wq.shape),
                  _const_spec((1, D_MODEL)),
                  _const_spec(ones.shape),
                  tab, tab, prev, cur, prev, cur,
                  _const_spec(wo.shape)],
        out_specs=x_spec)
    return pl.pallas_call(
        _attn_prompt_kernel,
        grid_spec=grid_spec,
        out_shape=jax.ShapeDtypeStruct(x.shape, F32),
        compiler_params=_params(("parallel", "arbitrary")),
        name="attn_prompt",
    )(sinks, x, g, wq, qgain, ones, cos, sin, kd, kd, vd, vd, wo)


def _q_sample_kernel(x_ref, g_ref, wq_ref, qg_ref, ones_ref, cos_ref, sin_ref, q_ref):
    h = _rms(x_ref[...], g_ref[...]).astype(BF16)
    q = _dot(h, wq_ref[...])
    n = q.shape[-1]
    q = q * lax.rsqrt(_seg_mean_sq(q, ones_ref) + EPS) * qg_ref[...]
    q = _rope(q, _tile_lanes(cos_ref[...], n), _tile_lanes(sin_ref[...], n))
    q_ref[...] = q * (HEAD_DIM ** -0.5)


def _q_sample(x, g, wq_pad, qgain_pad, ones, cos, sin):
    rows = x.shape[0]
    return pl.pallas_call(
        _q_sample_kernel,
        out_shape=jax.ShapeDtypeStruct((rows, wq_pad.shape[1]), F32),
        compiler_params=pltpu.CompilerParams(vmem_limit_bytes=VMEM_LIMIT_BYTES),
        name="q_sample",
    )(x, g, wq_pad, qgain_pad, ones, cos, sin)


def _attn_sample_kernel(q_ref, k_ref, v_ref, sink_ref, o_ref):
    sink = sink_ref[...]
    for b in range(q_ref.shape[0]):
        s = lax.dot_general(q_ref[b].astype(BF16), k_ref[b].astype(BF16), (((1,), (1,)), ((), ())),
                            preferred_element_type=F32)
        m = jnp.maximum(jnp.max(s, axis=-1, keepdims=True), sink)
        p = jnp.exp(s - m)
        denom = jnp.sum(p, axis=-1, keepdims=True) + jnp.exp(sink - m)
        o_ref[b] = _dot(p.astype(BF16), v_ref[b].astype(BF16)) / denom


def _attn_sample(q3, k, v, sink_col):
    bsz, nh, nkv = q3.shape
    keys = k.shape[1]
    bb = SUBLANES
    return pl.pallas_call(
        _attn_sample_kernel,
        grid=(bsz // bb,),
        in_specs=[pl.BlockSpec((bb, nh, nkv), lambda i: (i, 0, 0)),
                  pl.BlockSpec((bb, keys, nkv), lambda i: (i, 0, 0)),
                  pl.BlockSpec((bb, keys, nkv), lambda i: (i, 0, 0)),
                  _const_spec(sink_col.shape)],
        out_specs=pl.BlockSpec((bb, nh, nkv), lambda i: (i, 0, 0)),
        out_shape=jax.ShapeDtypeStruct((bsz, nh, nkv), F32),
        compiler_params=_params(("parallel",)),
        name="attn_sample",
    )(q3, k, v, sink_col)


def _proj_residual_kernel(x_ref, a_ref, w_ref, o_ref):
    o_ref[...] = x_ref[...] + _dot(a_ref[...].astype(BF16), w_ref[...])


def _proj_residual(x, a, w):
    return pl.pallas_call(
        _proj_residual_kernel,
        out_shape=jax.ShapeDtypeStruct(x.shape, F32),
        compiler_params=pltpu.CompilerParams(vmem_limit_bytes=VMEM_LIMIT_BYTES),
        name="o_proj_sample",
    )(x, a, w)


def _rope_tables(pos):
    half = HEAD_DIM // 2
    inv = ROPE_THETA ** (-jnp.arange(half, dtype=F32) / half)
    ang = pos.astype(F32)[:, None] * inv[None, :]
    cos, sin = jnp.cos(ang), jnp.sin(ang)
    reps = LANES // HEAD_DIM
    return (jnp.tile(jnp.concatenate([cos, cos], axis=-1), (1, reps)),
            jnp.tile(jnp.concatenate([-sin, sin], axis=-1), (1, reps)))


def _block_ones(width, block):
    idx = jnp.arange(width) // block
    return (idx[:, None] == idx[None, :]).astype(BF16)


def _dup_heads(w):
    w3 = w.reshape(w.shape[0], N_KV_HEADS, 1, HEAD_DIM)
    return jnp.broadcast_to(w3, (w.shape[0], N_KV_HEADS, LANES // HEAD_DIM, HEAD_DIM)).reshape(w.shape[0], -1)


def _pad_query_heads(w):
    lead = w.shape[:-1]
    nkv = N_KV_HEADS * HEAD_DIM
    w3 = w.reshape(lead + (N_KV_HEADS, N_HEADS // N_KV_HEADS, 1, HEAD_DIM))
    sel = jnp.eye(N_KV_HEADS, dtype=w.dtype)[:, None, :, None]
    return (w3 * sel).reshape(lead + (N_HEADS * nkv,))


def kernel(x_prompt, x_sample, state_ssm_re, state_ssm_im, cache_k, cache_v, norm_mix, norm_mlp, ssm_a_re, ssm_a_im, ssm_log_dt, ssm_b_re, ssm_b_im, ssm_c_re, ssm_c_im, ssm_d, w_glu, b_glu, norm_kv, w_k, w_v, k_norm, w_q, q_norm, attn_sinks, w_o, w_mlp_in, w_mlp_out):
    bsz, seq, _ = x_prompt.shape
    dec = x_sample.shape[0]
    n_a = ssm_a_re.shape[0]
    depth = norm_mix.shape[0]
    past = 8192
    nkv = N_KV_HEADS * HEAD_DIM
    row = lambda v: v.astype(F32).reshape(1, -1)

    xp = x_prompt.astype(F32)
    xs = x_sample.reshape(dec, D_MODEL)
    rows3 = lambda v: v.astype(F32).reshape(v.shape[0], 1, -1)

    w1 = w_mlp_in.astype(BF16)
    w2 = w_mlp_out.astype(BF16)
    g_mlp = rows3(norm_mlp)
    mlp_p = lambda x, layer: _mlp(x.reshape(bsz * seq, D_MODEL), g_mlp, w1, w2, layer).reshape(bsz, seq, D_MODEL)

    bhat, chat, lam = _s5_weights(ssm_a_re, ssm_a_im, ssm_log_dt, ssm_b_re, ssm_b_im, ssm_c_re, ssm_c_im)
    bhat16 = bhat.astype(BF16)
    wglu = w_glu.astype(BF16)
    g_mix, d_skip, bg = rows3(norm_mix), rows3(ssm_d), rows3(b_glu)
    sp_re, sp_im, ss_re, ss_im = [], [], [], []
    for i in range(n_a):
        xp, fin = _s5_prompt(xp, i, g_mix, bhat16, lam, chat, d_skip, wglu, bg)
        re, im = _lanes_to_state(fin[bsz:])
        sp_re.append(re)
        sp_im.append(im)
        s0 = _state_to_lanes(state_ssm_re[i], state_ssm_im[i])
        xs, snew = _s5_sample(xs, i, g_mix, bhat, lam, chat, d_skip, wglu, bg, s0)
        re, im = _lanes_to_state(snew)
        ss_re.append(re)
        ss_im.append(im)
        xp = mlp_p(xp, i)
        xs = _mlp(xs, g_mlp, w1, w2, i)

    cos_p, sin_p = _rope_tables(jnp.arange(seq, dtype=jnp.int32))
    cos_s, sin_s = _rope_tables(jnp.full((dec,), past, dtype=jnp.int32))
    ones_head = _block_ones(MXU_TILE, HEAD_DIM)
    ones_tile = _block_ones(MXU_TILE, MXU_TILE)
    kgain = row(k_norm)
    tile_gain = lambda gvec, n: jnp.tile(gvec, (1, n // HEAD_DIM))

    wkv_dup = jnp.concatenate([_dup_heads(w_k), _dup_heads(w_v)], axis=1).astype(BF16)
    wkv_std = jnp.concatenate([w_k, w_v], axis=1).astype(BF16)
    kd, vd = _kv(xp.reshape(bsz * seq, D_MODEL), row(norm_kv), wkv_dup, tile_gain(kgain, 2 * nkv), ones_head,
                 cos_p, sin_p, lambda i: i % (seq // KV_ROWS))
    ks_new, vs_new = _kv(xs, row(norm_kv), wkv_std, tile_gain(kgain, nkv), ones_head, cos_s, sin_s,
                         lambda i: i)
    kd = kd.reshape(bsz, seq, 2 * nkv)
    vd = vd.reshape(bsz, seq, 2 * nkv)
    undup = lambda t: t[:, -WINDOW:].reshape(bsz, WINDOW, N_KV_HEADS, LANES // HEAD_DIM, HEAD_DIM)[:, :, :, 0]
    new_k_p, new_v_p = undup(kd), undup(vd)
    new_k_s = jnp.concatenate([cache_k.astype(F32)[:, 1:], ks_new.reshape(dec, 1, N_KV_HEADS, HEAD_DIM)], axis=1)
    new_v_s = jnp.concatenate([cache_v.astype(F32)[:, 1:], vs_new.reshape(dec, 1, N_KV_HEADS, HEAD_DIM)], axis=1)
    keys_s = new_k_s.reshape(dec, WINDOW, nkv)
    vals_s = new_v_s.reshape(dec, WINDOW, nkv)

    for j in range(depth - n_a):
        layer = n_a + j
        gn = row(norm_mix[layer])
        qgain = row(q_norm[j])
        wq = w_q[j].astype(BF16)
        wo = w_o[j].astype(BF16)
        xp = _attn_prompt(xp, gn, wq, tile_gain(qgain, D_MODEL), ones_head, cos_p, sin_p, kd, vd,
                          attn_sinks[j].astype(F32), wo)
        q_pad = _q_sample(xs, gn, _pad_query_heads(w_q[j]).astype(BF16),
                          _pad_query_heads(tile_gain(qgain, D_MODEL)), ones_tile, cos_s, sin_s)
        o_pad = _attn_sample(q_pad.reshape(dec, N_HEADS, nkv), keys_s, vals_s,
                             attn_sinks[j].astype(F32).reshape(N_HEADS, 1))
        wo_pad = _pad_query_heads(w_o[j].T).T.astype(BF16)
        xs = _proj_residual(xs, o_pad.reshape(dec, N_HEADS * nkv), wo_pad)
        xp = mlp_p(xp, layer)
        xs = _mlp(xs, g_mlp, w1, w2, layer)

    return (xp, xs.reshape(dec, 1, D_MODEL),
            jnp.stack(sp_re), jnp.stack(sp_im), new_k_p, new_v_p,
            jnp.stack(ss_re), jnp.stack(ss_im), new_k_s, new_v_s)
```

```python
import functools
import math

import jax
import jax.numpy as jnp
from jax import lax
from jax.experimental import pallas as pl
from jax.experimental.pallas import tpu as pltpu

F32 = jnp.float32
BF16 = jnp.bfloat16

D_MODEL = 1024
N_GROUPS = 64
GROUP_SIZE = 16
STATE_DIM = 64
HEAD_DIM = 64
N_HEADS = 16
N_KV_HEADS = 4
WINDOW = 128
ROPE_THETA = 10000.0
D_FF = 4 * D_MODEL
EPS = 1e-6

SUBLANES = 8
LANES = 128
MXU_TILE = 256
VMEM_LIMIT_BYTES = 56 * 1024 * 1024

N_CH_BLOCKS = D_MODEL // MXU_TILE
STATE_BLOCK = (MXU_TILE // GROUP_SIZE) * STATE_DIM
STATE_LANES = 2 * STATE_BLOCK * N_CH_BLOCKS

S5_TIME_CHUNK = 128
MLP_ROWS = 512
FF_CHUNK = 1024
KV_ROWS = 512
ATTN_BLOCKS = 2


def _const_spec(shape):
    zeros = (0,) * len(shape)
    return pl.BlockSpec(shape, lambda *_: zeros, pipeline_mode=pl.Buffered(1))


def _layer_spec(shape, layer):
    idx = (layer,) + (0,) * (len(shape) - 1)
    return pl.BlockSpec((None,) + tuple(shape[1:]), lambda *_: idx, pipeline_mode=pl.Buffered(1))


def _params(sem):
    return pltpu.CompilerParams(dimension_semantics=sem, vmem_limit_bytes=VMEM_LIMIT_BYTES)


def _rms(x, g):
    return x * lax.rsqrt(jnp.mean(x * x, axis=-1, keepdims=True) + EPS) * g


def _dot(a, b):
    return jnp.dot(a, b, preferred_element_type=F32)


def _seg_mean_sq(x, ones_ref):
    sq = x * x
    hi = sq.astype(BF16)
    lo = (sq - hi.astype(F32)).astype(BF16)
    ones = ones_ref[...]
    outs = []
    for t in range(x.shape[-1] // MXU_TILE):
        sl = slice(MXU_TILE * t, MXU_TILE * (t + 1))
        outs.append(_dot(hi[:, sl], ones) + _dot(lo[:, sl], ones))
    return jnp.concatenate(outs, axis=-1) * (1.0 / HEAD_DIM)


def _tile_lanes(t, n):
    return jnp.concatenate([t] * (n // t.shape[-1]), axis=-1)


def _rope(x, cos, sin_signed):
    n = x.shape[-1]
    lane = lax.broadcasted_iota(jnp.int32, x.shape, 1)
    first = (lane & (HEAD_DIM - 1)) < (HEAD_DIM // 2)
    partner = jnp.where(first, pltpu.roll(x, n - HEAD_DIM // 2, 1), pltpu.roll(x, HEAD_DIM // 2, 1))
    return x * cos + partner * sin_signed


def _mlp_kernel(x_ref, g_ref, w1_ref, w2_ref, o_ref):
    x = x_ref[...]
    h = _rms(x, g_ref[...]).astype(BF16)
    acc = x
    for c in range(D_FF // FF_CHUNK):
        sl = slice(FF_CHUNK * c, FF_CHUNK * (c + 1))
        a = jnp.square(jnp.maximum(_dot(h, w1_ref[:, sl]), 0.0)).astype(BF16)
        acc = acc + _dot(a, w2_ref[sl, :])
    o_ref[...] = acc


def _mlp(x, g, w1, w2, layer):
    rows = x.shape[0]
    tm = min(MLP_ROWS, rows)
    return pl.pallas_call(
        _mlp_kernel,
        grid=(rows // tm,),
        in_specs=[pl.BlockSpec((tm, D_MODEL), lambda i: (i, 0)),
                  _layer_spec(g.shape, layer),
                  _layer_spec(w1.shape, layer),
                  _layer_spec(w2.shape, layer)],
        out_specs=pl.BlockSpec((tm, D_MODEL), lambda i: (i, 0)),
        out_shape=jax.ShapeDtypeStruct((rows, D_MODEL), F32),
        compiler_params=_params(("parallel",)),
        name="mlp",
    )(x, g, w1, w2)


def _s5_tail(u, y, d_ref, wglu_ref, bglu_ref):
    y = y + d_ref[...] * u
    g = jax.nn.gelu(y).astype(BF16)
    z = _dot(g, wglu_ref[...]) + bglu_ref[...]
    return z[:, :D_MODEL] * jax.nn.sigmoid(z[:, D_MODEL:])


def _s5_prompt_kernel(x_ref, gn_ref, bhat_ref, lam_ref, chat_ref, d_ref, wglu_ref, bglu_ref,
                      o_ref, fin_ref, st_ref, carry_ref, tb_ref):
    batch, tc, _ = x_ref.shape
    assert 2 * batch == SUBLANES
    n_slabs = D_MODEL // LANES

    @pl.when(pl.program_id(0) == 0)
    def _():
        carry_ref[...] = jnp.zeros_like(carry_ref)

    for b in range(batch):
        ub_nat = _rms(x_ref[b], gn_ref[...])
        for c in range(n_slabs):
            tb_ref[c, pl.ds(b, tc, stride=batch), :] = ub_nat[:, LANES * c:LANES * (c + 1)]
    u = jnp.concatenate([tb_ref[c] for c in range(n_slabs)], axis=-1)
    ub = u.astype(BF16)
    lo = lax.broadcasted_iota(jnp.int32, (SUBLANES, LANES), 0) < batch
    n_vregs = batch * tc // SUBLANES
    n_cols = STATE_BLOCK // LANES
    ys = []
    for i in range(N_CH_BLOCKS):
        base = 2 * STATE_BLOCK * i
        st_ref[:, base:base + 2 * STATE_BLOCK] = _dot(ub[:, MXU_TILE * i:MXU_TILE * (i + 1)], bhat_ref[i])
        re_cols = [base + LANES * j for j in range(n_cols)]
        im_cols = [base + STATE_BLOCK + LANES * j for j in range(n_cols)]
        lam_r = [jnp.broadcast_to(lam_ref[0:1, STATE_BLOCK * i + LANES * j:STATE_BLOCK * i + LANES * (j + 1)],
                                  (SUBLANES, LANES)) for j in range(n_cols)]
        lam_i = [jnp.broadcast_to(lam_ref[1:2, STATE_BLOCK * i + LANES * j:STATE_BLOCK * i + LANES * (j + 1)],
                                  (SUBLANES, LANES)) for j in range(n_cols)]
        sr = [carry_ref[:, c:c + LANES] for c in re_cols]
        si = [carry_ref[:, c:c + LANES] for c in im_cols]
        for m in range(n_vregs):
            r0 = SUBLANES * m
            for j in range(n_cols):
                xr = st_ref[r0:r0 + SUBLANES, re_cols[j]:re_cols[j] + LANES]
                xi = st_ref[r0:r0 + SUBLANES, im_cols[j]:im_cols[j] + LANES]

                def step(pr, pi, j=j, xr=xr, xi=xi):
                    rr = pltpu.roll(pr, batch, 0)
                    ri = pltpu.roll(pi, batch, 0)
                    return (lam_r[j] * rr - lam_i[j] * ri + xr,
                            lam_r[j] * ri + lam_i[j] * rr + xi)

                c1r, c1i = step(sr[j], si[j])
                c2r, c2i = step(c1r, c1i)
                sr[j] = jnp.where(lo, c1r, c2r)
                si[j] = jnp.where(lo, c1i, c2i)
                st_ref[r0:r0 + SUBLANES, re_cols[j]:re_cols[j] + LANES] = sr[j]
                st_ref[r0:r0 + SUBLANES, im_cols[j]:im_cols[j] + LANES] = si[j]
        for j in range(n_cols):
            carry_ref[:, re_cols[j]:re_cols[j] + LANES] = sr[j]
            carry_ref[:, im_cols[j]:im_cols[j] + LANES] = si[j]
        ys.append(_dot(st_ref[:, base:base + 2 * STATE_BLOCK].astype(BF16), chat_ref[i]))
    y = jnp.concatenate(ys, axis=-1)
    u = jnp.concatenate([tb_ref[c] for c in range(n_slabs)], axis=-1)
    upd = _s5_tail(u, y, d_ref, wglu_ref, bglu_ref)
    for c in range(n_slabs):
        tb_ref[c] = upd[:, LANES * c:LANES * (c + 1)]
    for b in range(batch):
        o_ref[b] = x_ref[b] + jnp.concatenate(
            [tb_ref[c, pl.ds(b, tc, stride=batch), :] for c in range(n_slabs)], axis=-1)
    fin_ref[...] = carry_ref[...]


def _s5_prompt(x, layer, gn, bhat, lam, chat, d, wglu, bglu):
    batch, seq, _ = x.shape
    tc = S5_TIME_CHUNK
    x_spec = pl.BlockSpec((batch, tc, D_MODEL), lambda i: (0, i, 0))
    return pl.pallas_call(
        _s5_prompt_kernel,
        grid=(seq // tc,),
        in_specs=[x_spec] + [_layer_spec(w.shape, layer) for w in (gn, bhat, lam, chat, d, wglu, bglu)],
        out_specs=[x_spec, pl.BlockSpec((SUBLANES, STATE_LANES), lambda i: (0, 0))],
        out_shape=[jax.ShapeDtypeStruct(x.shape, F32),
                   jax.ShapeDtypeStruct((SUBLANES, STATE_LANES), F32)],
        scratch_shapes=[pltpu.VMEM((batch * tc, STATE_LANES), F32),
                        pltpu.VMEM((SUBLANES, STATE_LANES), F32),
                        pltpu.VMEM((D_MODEL // LANES, batch * tc, LANES), F32)],
        compiler_params=_params(("arbitrary",)),
        name="s5_prompt",
    )(x, gn, bhat, lam, chat, d, wglu, bglu)


def _s5_sample_kernel(x_ref, gn_ref, bhat_ref, lam_ref, chat_ref, d_ref, wglu_ref, bglu_ref, s0_ref,
                      o_ref, snew_ref):
    x = x_ref[...]
    u = _rms(x, gn_ref[...])
    ys = []
    for i in range(N_CH_BLOCKS):
        base = 2 * STATE_BLOCK * i
        bu = jnp.dot(u[:, MXU_TILE * i:MXU_TILE * (i + 1)], bhat_ref[i],
                     preferred_element_type=F32, precision=lax.Precision.HIGHEST)
        lr = lam_ref[0:1, STATE_BLOCK * i:STATE_BLOCK * (i + 1)]
        li = lam_ref[1:2, STATE_BLOCK * i:STATE_BLOCK * (i + 1)]
        s0r = s0_ref[:, base:base + STATE_BLOCK]
        s0i = s0_ref[:, base + STATE_BLOCK:base + 2 * STATE_BLOCK]
        sr = lr * s0r - li * s0i + bu[:, :STATE_BLOCK]
        si = lr * s0i + li * s0r + bu[:, STATE_BLOCK:]
        snew_ref[:, base:base + STATE_BLOCK] = sr
        snew_ref[:, base + STATE_BLOCK:base + 2 * STATE_BLOCK] = si
        sb = jnp.concatenate([sr, si], axis=-1).astype(BF16)
        ys.append(_dot(sb, chat_ref[i]))
    y = jnp.concatenate(ys, axis=-1)
    o_ref[...] = x + _s5_tail(u, y, d_ref, wglu_ref, bglu_ref)


def _s5_sample(x, layer, gn, bhat32, lam, chat, d, wglu, bglu, s0):
    rows = x.shape[0]
    return pl.pallas_call(
        _s5_sample_kernel,
        grid=(1,),
        in_specs=[_const_spec(x.shape)] + [_layer_spec(w.shape, layer) for w in (gn, bhat32, lam, chat, d, wglu, bglu)]
        + [_const_spec(s0.shape)],
        out_specs=[pl.BlockSpec((rows, D_MODEL), lambda i: (0, 0)),
                   pl.BlockSpec((rows, STATE_LANES), lambda i: (0, 0))],
        out_shape=[jax.ShapeDtypeStruct((rows, D_MODEL), F32),
                   jax.ShapeDtypeStruct((rows, STATE_LANES), F32)],
        compiler_params=_params(("arbitrary",)),
        name="s5_sample",
    )(x, gn, bhat32, lam, chat, d, wglu, bglu, s0)


def _s5_weights(a_re, a_im, log_dt, b_re, b_im, c_re, c_im):
    n = a_re.shape[0]
    dt = jnp.exp(log_dt.astype(F32))
    lam = lax.complex(a_re.astype(F32), a_im.astype(F32))
    lam_bar = jnp.exp(lam * dt)
    b = lax.complex(b_re.astype(F32), b_im.astype(F32))
    b_bar = ((lam_bar - 1.0) / lam)[..., None] * b
    gpb = MXU_TILE // GROUP_SIZE
    same_group = jnp.eye(gpb, dtype=F32)[None, None, :, None, :, None]

    def expand_in(w):
        w5 = jnp.swapaxes(w.reshape(n, N_CH_BLOCKS, gpb, STATE_DIM, GROUP_SIZE), 3, 4)
        return (w5[:, :, :, :, None, :] * same_group).reshape(n, N_CH_BLOCKS, MXU_TILE, STATE_BLOCK)

    def expand_out(w):
        w5 = jnp.swapaxes(w.reshape(n, N_CH_BLOCKS, gpb, GROUP_SIZE, STATE_DIM), 3, 4)
        return (w5[:, :, :, :, None, :] * same_group).reshape(n, N_CH_BLOCKS, STATE_BLOCK, MXU_TILE)

    bhat = jnp.concatenate([expand_in(jnp.real(b_bar)), expand_in(jnp.imag(b_bar))], axis=-1)
    chat = jnp.concatenate([expand_out(c_re.astype(F32)), -expand_out(c_im.astype(F32))], axis=2)
    lam_rows = jnp.stack([jnp.real(lam_bar).reshape(n, -1), jnp.imag(lam_bar).reshape(n, -1)], axis=1)
    return bhat, chat.astype(BF16), lam_rows


def _state_to_lanes(s_re, s_im):
    n = s_re.shape[0]
    re = s_re.astype(F32).reshape(n, N_CH_BLOCKS, STATE_BLOCK)
    im = s_im.astype(F32).reshape(n, N_CH_BLOCKS, STATE_BLOCK)
    return jnp.concatenate([re, im], axis=-1).reshape(n, STATE_LANES)


def _lanes_to_state(s):
    n = s.shape[0]
    s4 = s.reshape(n, N_CH_BLOCKS, 2, STATE_BLOCK)
    return (s4[:, :, 0].reshape(n, N_GROUPS, STATE_DIM), s4[:, :, 1].reshape(n, N_GROUPS, STATE_DIM))


def _kv_kernel(x_ref, g_ref, wkv_ref, kg_ref, ones_ref, cos_ref, sin_ref, k_ref, v_ref):
    h = _rms(x_ref[...], g_ref[...]).astype(BF16)
    kv = _dot(h, wkv_ref[...])
    n = k_ref.shape[-1]
    k = kv[:, :n]
    k = k * lax.rsqrt(_seg_mean_sq(k, ones_ref) + EPS) * kg_ref[...]
    k_ref[...] = _rope(k, _tile_lanes(cos_ref[...], n), _tile_lanes(sin_ref[...], n))
    v_ref[...] = kv[:, n:]


def _kv(x, g, wkv, kgain, ones, cos, sin, table_block):
    rows = x.shape[0]
    n = wkv.shape[1] // 2
    tm = min(KV_ROWS, rows)
    return pl.pallas_call(
        _kv_kernel,
        grid=(rows // tm,),
        in_specs=[pl.BlockSpec((tm, D_MODEL), lambda i: (i, 0)),
                  _const_spec((1, D_MODEL)),
                  _const_spec(wkv.shape),
                  _const_spec((1, n)),
                  _const_spec(ones.shape),
                  pl.BlockSpec((tm, LANES), lambda i: (table_block(i), 0)),
                  pl.BlockSpec((tm, LANES), lambda i: (table_block(i), 0))],
        out_specs=[pl.BlockSpec((tm, n), lambda i: (i, 0)),
                   pl.BlockSpec((tm, n), lambda i: (i, 0))],
        out_shape=[jax.ShapeDtypeStruct((rows, n), F32),
                   jax.ShapeDtypeStruct((rows, n), F32)],
        compiler_params=_params(("parallel",)),
        name="kv_proj",
    )(x, g, wkv, kgain, ones, cos, sin)


def _attn_prompt_kernel(sinks_ref, x_ref, g_ref, wq_ref, qg_ref, ones_ref, cos_ref, sin_ref,
                        kp_ref, kc_ref, vp_ref, vc_ref, wo_ref, o_ref):
    tile = pl.program_id(1)
    x = x_ref[...]
    n_blocks = x.shape[0] // WINDOW
    heads_per_kv = N_HEADS // N_KV_HEADS
    h = _rms(x, g_ref[...]).astype(BF16)
    q = _dot(h, wq_ref[...])
    q = q * lax.rsqrt(_seg_mean_sq(q, ones_ref) + EPS) * qg_ref[...]
    q = _rope(q, _tile_lanes(cos_ref[...], D_MODEL), _tile_lanes(sin_ref[...], D_MODEL))
    qb = (q * (HEAD_DIM ** -0.5)).astype(BF16)

    kall = jnp.concatenate([kp_ref[...], kc_ref[...]], axis=0).astype(BF16)
    vall = jnp.concatenate([vp_ref[...], vc_ref[...]], axis=0).astype(BF16)

    qi = lax.broadcasted_iota(jnp.int32, (WINDOW, WINDOW), 0)
    kj = lax.broadcasted_iota(jnp.int32, (WINDOW, WINDOW), 1)
    from_prev = kj > qi
    low_half = lax.broadcasted_iota(jnp.int32, (WINDOW, LANES), 1) < HEAD_DIM
    neg_inf = jnp.full((WINDOW, WINDOW), -jnp.inf, F32)

    block_outs = []
    for blk in range(n_blocks):
        has_prev = tile * n_blocks + blk > 0
        r0 = WINDOW * blk
        outs = []
        for pb in range(N_HEADS // 2):
            g = pb // (heads_per_kv // 2)
            kg = kall[r0:r0 + 2 * WINDOW, LANES * g:LANES * (g + 1)]
            vg = vall[r0:r0 + 2 * WINDOW, LANES * g:LANES * (g + 1)]
            qp = qb[r0:r0 + WINDOW, LANES * pb:LANES * (pb + 1)]
            halves = []
            for par in range(2):
                keep = low_half if par == 0 else jnp.logical_not(low_half)
                qm = jnp.where(keep, qp, jnp.zeros_like(qp))
                sink = sinks_ref[2 * pb + par]
                s = lax.dot_general(qm, kg, (((1,), (1,)), ((), ())), preferred_element_type=F32)
                s_prev = jnp.where(has_prev, s[:, :WINDOW], neg_inf)
                t = jnp.where(from_prev, s_prev, s[:, WINDOW:])
                m = jnp.maximum(jnp.max(t, axis=-1, keepdims=True), sink)
                p = jnp.exp(t - m)
                denom = jnp.sum(p, axis=-1, keepdims=True) + jnp.exp(sink - m)
                zero = jnp.zeros_like(p)
                pcat = jnp.concatenate([jnp.where(from_prev, p, zero), jnp.where(from_prev, zero, p)], axis=1)
                halves.append(_dot(pcat.astype(BF16), vg) / denom)
            outs.append(jnp.where(low_half, halves[0], halves[1]))
        block_outs.append(jnp.concatenate(outs, axis=-1))
    attn = jnp.concatenate(block_outs, axis=0).astype(BF16)
    o_ref[...] = x + _dot(attn, wo_ref[...])


def _attn_prompt(x, g, wq, qgain, ones, cos, sin, kd, vd, sinks, wo):
    bsz, seq, _ = x.shape
    nkv = kd.shape[-1]
    nb = ATTN_BLOCKS
    tq = nb * WINDOW
    x_spec = pl.BlockSpec((None, tq, D_MODEL), lambda b, i, *_: (b, i, 0))
    cur = pl.BlockSpec((None, tq, nkv), lambda b, i, *_: (b, i, 0))
    prev = pl.BlockSpec((None, WINDOW, nkv), lambda b, i, *_: (b, jnp.maximum(nb * i - 1, 0), 0))
    tab = pl.BlockSpec((tq, LANES), lambda b, i, *_: (i, 0))
    grid_spec = pltpu.PrefetchScalarGridSpec(
        num_scalar_prefetch=1,
        grid=(bsz, seq // tq),
        in_specs=[x_spec,
                  _const_spec((1, D_MODEL)),
                  _const_spec(wq.shape),
                  _const_spec((1, D_MODEL)),
                  _const_spec(ones.shape),
                  tab, tab, prev, cur, prev, cur,
                  _const_spec(wo.shape)],
        out_specs=x_spec)
    return pl.pallas_call(
        _attn_prompt_kernel,
        grid_spec=grid_spec,
        out_shape=jax.ShapeDtypeStruct(x.shape, F32),
        compiler_params=_params(("parallel", "arbitrary")),
        name="attn_prompt",
    )(sinks, x, g, wq, qgain, ones, cos, sin, kd, kd, vd, vd, wo)


def _q_sample_kernel(x_ref, g_ref, wq_ref, qg_ref, ones_ref, cos_ref, sin_ref, q_ref):
    h = _rms(x_ref[...], g_ref[...]).astype(BF16)
    q = _dot(h, wq_ref[...])
    n = q.shape[-1]
    q = q * lax.rsqrt(_seg_mean_sq(q, ones_ref) + EPS) * qg_ref[...]
    q = _rope(q, _tile_lanes(cos_ref[...], n), _tile_lanes(sin_ref[...], n))
    q_ref[...] = q * (HEAD_DIM ** -0.5)


def _q_sample(x, g, wq_pad, qgain_pad, ones, cos, sin):
    rows = x.shape[0]
    return pl.pallas_call(
        _q_sample_kernel,
        out_shape=jax.ShapeDtypeStruct((rows, wq_pad.shape[1]), F32),
        compiler_params=pltpu.CompilerParams(vmem_limit_bytes=VMEM_LIMIT_BYTES),
        name="q_sample",
    )(x, g, wq_pad, qgain_pad, ones, cos, sin)


def _attn_sample_kernel(q_ref, k_ref, v_ref, sink_ref, o_ref):
    sink = sink_ref[...]
    for b in range(q_ref.shape[0]):
        s = lax.dot_general(q_ref[b].astype(BF16), k_ref[b].astype(BF16), (((1,), (1,)), ((), ())),
                            preferred_element_type=F32)
        m = jnp.maximum(jnp.max(s, axis=-1, keepdims=True), sink)
        p = jnp.exp(s - m)
        denom = jnp.sum(p, axis=-1, keepdims=True) + jnp.exp(sink - m)
        o_ref[b] = _dot(p.astype(BF16), v_ref[b].astype(BF16)) / denom


def _attn_sample(q3, k, v, sink_col):
    bsz, nh, nkv = q3.shape
    keys = k.shape[1]
    bb = SUBLANES
    return pl.pallas_call(
        _attn_sample_kernel,
        grid=(bsz // bb,),
        in_specs=[pl.BlockSpec((bb, nh, nkv), lambda i: (i, 0, 0)),
                  pl.BlockSpec((bb, keys, nkv), lambda i: (i, 0, 0)),
                  pl.BlockSpec((bb, keys, nkv), lambda i: (i, 0, 0)),
                  _const_spec(sink_col.shape)],
        out_specs=pl.BlockSpec((bb, nh, nkv), lambda i: (i, 0, 0)),
        out_shape=jax.ShapeDtypeStruct((bsz, nh, nkv), F32),
        compiler_params=_params(("parallel",)),
        name="attn_sample",
    )(q3, k, v, sink_col)


def _proj_residual_kernel(x_ref, a_ref, w_ref, o_ref):
    o_ref[...] = x_ref[...] + _dot(a_ref[...].astype(BF16), w_ref[...])


def _proj_residual(x, a, w):
    return pl.pallas_call(
        _proj_residual_kernel,
        out_shape=jax.ShapeDtypeStruct(x.shape, F32),
        compiler_params=pltpu.CompilerParams(vmem_limit_bytes=VMEM_LIMIT_BYTES),
        name="o_proj_sample",
    )(x, a, w)


def _rope_tables(pos):
    half = HEAD_DIM // 2
    inv = ROPE_THETA ** (-jnp.arange(half, dtype=F32) / half)
    ang = pos.astype(F32)[:, None] * inv[None, :]
    cos, sin = jnp.cos(ang), jnp.sin(ang)
    reps = LANES // HEAD_DIM
    return (jnp.tile(jnp.concatenate([cos, cos], axis=-1), (1, reps)),
            jnp.tile(jnp.concatenate([-sin, sin], axis=-1), (1, reps)))


def _block_ones(width, block):
    idx = jnp.arange(width) // block
    return (idx[:, None] == idx[None, :]).astype(BF16)


def _dup_heads(w):
    w3 = w.reshape(w.shape[0], N_KV_HEADS, 1, HEAD_DIM)
    return jnp.broadcast_to(w3, (w.shape[0], N_KV_HEADS, LANES // HEAD_DIM, HEAD_DIM)).reshape(w.shape[0], -1)


def _pad_query_heads(w):
    lead = w.shape[:-1]
    nkv = N_KV_HEADS * HEAD_DIM
    w3 = w.reshape(lead + (N_KV_HEADS, N_HEADS // N_KV_HEADS, 1, HEAD_DIM))
    sel = jnp.eye(N_KV_HEADS, dtype=w.dtype)[:, None, :, None]
    return (w3 * sel).reshape(lead + (N_HEADS * nkv,))


def kernel(x_prompt, x_sample, state_ssm_re, state_ssm_im, cache_k, cache_v, norm_mix, norm_mlp, ssm_a_re, ssm_a_im, ssm_log_dt, ssm_b_re, ssm_b_im, ssm_c_re, ssm_c_im, ssm_d, w_glu, b_glu, norm_kv, w_k, w_v, k_norm, w_q, q_norm, attn_sinks, w_o, w_mlp_in, w_mlp_out):
    bsz, seq, _ = x_prompt.shape
    dec = x_sample.shape[0]
    n_a = ssm_a_re.shape[0]
    depth = norm_mix.shape[0]
    past = 8192
    nkv = N_KV_HEADS * HEAD_DIM
    row = lambda v: v.astype(F32).reshape(1, -1)

    xp = x_prompt.astype(F32)
    xs = x_sample.reshape(dec, D_MODEL)
    rows3 = lambda v: v.astype(F32).reshape(v.shape[0], 1, -1)

    w1 = w_mlp_in.astype(BF16)
    w2 = w_mlp_out.astype(BF16)
    g_mlp = rows3(norm_mlp)
    mlp_p = lambda x, layer: _mlp(x.reshape(bsz * seq, D_MODEL), g_mlp, w1, w2, layer).reshape(bsz, seq, D_MODEL)

    bhat, chat, lam = _s5_weights(ssm_a_re, ssm_a_im, ssm_log_dt, ssm_b_re, ssm_b_im, ssm_c_re, ssm_c_im)
    bhat16 = bhat.astype(BF16)
    wglu = w_glu.astype(BF16)
    g_mix, d_skip, bg = rows3(norm_mix), rows3(ssm_d), rows3(b_glu)
    sp_re, sp_im, ss_re, ss_im = [], [], [], []
    for i in range(n_a):
        xp, fin = _s5_prompt(xp, i, g_mix, bhat16, lam, chat, d_skip, wglu, bg)
        re, im = _lanes_to_state(fin[bsz:])
        sp_re.append(re)
        sp_im.append(im)
        s0 = _state_to_lanes(state_ssm_re[i], state_ssm_im[i])
        xs, snew = _s5_sample(xs, i, g_mix, bhat, lam, chat, d_skip, wglu, bg, s0)
        re, im = _lanes_to_state(snew)
        ss_re.append(re)
        ss_im.append(im)
        xp = mlp_p(xp, i)
        xs = _mlp(xs, g_mlp, w1, w2, i)

    cos_p, sin_p = _rope_tables(jnp.arange(seq, dtype=jnp.int32))
    cos_s, sin_s = _rope_tables(jnp.full((dec,), past, dtype=jnp.int32))
    ones_head = _block_ones(MXU_TILE, HEAD_DIM)
    ones_tile = _block_ones(MXU_TILE, MXU_TILE)
    kgain = row(k_norm)
    tile_gain = lambda gvec, n: jnp.tile(gvec, (1, n // HEAD_DIM))

    wkv_dup = jnp.concatenate([_dup_heads(w_k), _dup_heads(w_v)], axis=1).astype(BF16)
    wkv_std = jnp.concatenate([w_k, w_v], axis=1).astype(BF16)
    kd, vd = _kv(xp.reshape(bsz * seq, D_MODEL), row(norm_kv), wkv_dup, tile_gain(kgain, 2 * nkv), ones_head,
                 cos_p, sin_p, lambda i: i % (seq // KV_ROWS))
    ks_new, vs_new = _kv(xs, row(norm_kv), wkv_std, tile_gain(kgain, nkv), ones_head, cos_s, sin_s,
                         lambda i: i)
    kd = kd.reshape(bsz, seq, 2 * nkv)
    vd = vd.reshape(bsz, seq, 2 * nkv)
    undup = lambda t: t[:, -WINDOW:].reshape(bsz, WINDOW, N_KV_HEADS, LANES // HEAD_DIM, HEAD_DIM)[:, :, :, 0]
    new_k_p, new_v_p = undup(kd), undup(vd)
    new_k_s = jnp.concatenate([cache_k.astype(F32)[:, 1:], ks_new.reshape(dec, 1, N_KV_HEADS, HEAD_DIM)], axis=1)
    new_v_s = jnp.concatenate([cache_v.astype(F32)[:, 1:], vs_new.reshape(dec, 1, N_KV_HEADS, HEAD_DIM)], axis=1)
    keys_s = new_k_s.reshape(dec, WINDOW, nkv)
    vals_s = new_v_s.reshape(dec, WINDOW, nkv)

    for j in range(depth - n_a):
        layer = n_a + j
        gn = row(norm_mix[layer])
        qgain = row(q_norm[j])
        wq = w_q[j].astype(BF16)
        wo = w_o[j].astype(BF16)
        xp = _attn_prompt(xp, gn, wq, tile_gain(qgain, D_MODEL), ones_head, cos_p, sin_p, kd, vd,
                          attn_sinks[j].astype(F32), wo)
        q_pad = _q_sample(xs, gn, _pad_query_heads(w_q[j]).astype(BF16),
                          _pad_query_heads(tile_gain(qgain, D_MODEL)), ones_tile, cos_s, sin_s)
        o_pad = _attn_sample(q_pad.reshape(dec, N_HEADS, nkv), keys_s, vals_s,
                             attn_sinks[j].astype(F32).reshape(N_HEADS, 1))
        wo_pad = _pad_query_heads(w_o[j].T).T.astype(BF16)
        xs = _proj_residual(xs, o_pad.reshape(dec, N_HEADS * nkv), wo_pad)
        xp = mlp_p(xp, layer)
        xs = _mlp(xs, g_mlp, w1, w2, layer)

    return (xp, xs.reshape(dec, 1, D_MODEL),
            jnp.stack(sp_re), jnp.stack(sp_im), new_k_p, new_v_p,
            jnp.stack(ss_re), jnp.stack(ss_im), new_k_s, new_v_s)
```

```python
import functools
import math

import jax
import jax.numpy as jnp
from jax import lax
from jax.experimental import pallas as pl
from jax.experimental.pallas import tpu as pltpu

F32 = jnp.float32
BF16 = jnp.bfloat16

D_MODEL = 1024
N_GROUPS = 64
GROUP_SIZE = 16
STATE_DIM = 64
HEAD_DIM = 64
N_HEADS = 16
N_KV_HEADS = 4
WINDOW = 128
ROPE_THETA = 10000.0
D_FF = 4 * D_MODEL
EPS = 1e-6

SUBLANES = 8
LANES = 128
MXU_TILE = 256
VMEM_LIMIT_BYTES = 56 * 1024 * 1024

N_CH_BLOCKS = D_MODEL // MXU_TILE
STATE_BLOCK = (MXU_TILE // GROUP_SIZE) * STATE_DIM
STATE_LANES = 2 * STATE_BLOCK * N_CH_BLOCKS

S5_TIME_CHUNK = 128
MLP_ROWS = 512
FF_CHUNK = 1024
KV_ROWS = 512
ATTN_BLOCKS = 2


def _const_spec(shape):
    zeros = (0,) * len(shape)
    return pl.BlockSpec(shape, lambda *_: zeros, pipeline_mode=pl.Buffered(1))


def _layer_spec(shape, layer):
    idx = (layer,) + (0,) * (len(shape) - 1)
    return pl.BlockSpec((None,) + tuple(shape[1:]), lambda *_: idx, pipeline_mode=pl.Buffered(1))


def _params(sem):
    return pltpu.CompilerParams(dimension_semantics=sem, vmem_limit_bytes=VMEM_LIMIT_BYTES)


def _rms(x, g):
    return x * lax.rsqrt(jnp.mean(x * x, axis=-1, keepdims=True) + EPS) * g


def _dot(a, b):
    return jnp.dot(a, b, preferred_element_type=F32)


def _seg_mean_sq(x, ones_ref):
    sq = x * x
    hi = sq.astype(BF16)
    lo = (sq - hi.astype(F32)).astype(BF16)
    ones = ones_ref[...]
    outs = []
    for t in range(x.shape[-1] // MXU_TILE):
        sl = slice(MXU_TILE * t, MXU_TILE * (t + 1))
        outs.append(_dot(hi[:, sl], ones) + _dot(lo[:, sl], ones))
    return jnp.concatenate(outs, axis=-1) * (1.0 / HEAD_DIM)


def _tile_lanes(t, n):
    return jnp.concatenate([t] * (n // t.shape[-1]), axis=-1)


def _rope(x, cos, sin_signed):
    n = x.shape[-1]
    lane = lax.broadcasted_iota(jnp.int32, x.shape, 1)
    first = (lane & (HEAD_DIM - 1)) < (HEAD_DIM // 2)
    partner = jnp.where(first, pltpu.roll(x, n - HEAD_DIM // 2, 1), pltpu.roll(x, HEAD_DIM // 2, 1))
    return x * cos + partner * sin_signed


def _mlp_kernel(x_ref, g_ref, w1_ref, w2_ref, o_ref):
    x = x_ref[...]
    h = _rms(x, g_ref[...]).astype(BF16)
    acc = x
    for c in range(D_FF // FF_CHUNK):
        sl = slice(FF_CHUNK * c, FF_CHUNK * (c + 1))
        a = jnp.square(jnp.maximum(_dot(h, w1_ref[:, sl]), 0.0)).astype(BF16)
        acc = acc + _dot(a, w2_ref[sl, :])
    o_ref[...] = acc


def _mlp(x, g, w1, w2, layer):
    rows = x.shape[0]
    tm = min(MLP_ROWS, rows)
    return pl.pallas_call(
        _mlp_kernel,
        grid=(rows // tm,),
        in_specs=[pl.BlockSpec((tm, D_MODEL), lambda i: (i, 0)),
                  _layer_spec(g.shape, layer),
                  _layer_spec(w1.shape, layer),
                  _layer_spec(w2.shape, layer)],
        out_specs=pl.BlockSpec((tm, D_MODEL), lambda i: (i, 0)),
        out_shape=jax.ShapeDtypeStruct((rows, D_MODEL), F32),
        compiler_params=_params(("parallel",)),
        name="mlp",
    )(x, g, w1, w2)


def _s5_gate(y, wglu_ref, bglu_ref):
    g = jax.nn.gelu(y).astype(BF16)
    z = _dot(g, wglu_ref[...]) + bglu_ref[...]
    return z[:, :D_MODEL] * jax.nn.sigmoid(z[:, D_MODEL:])


def _s5_prompt_kernel(xa_ref, xb_ref, gn_ref, bhat_ref, lam_ref, chat_ref, d_ref, wglu_ref, bglu_ref,
                      o_ref, fin_ref, st_ref, carry_ref, tb_ref, tb2_ref, y_ref):
    batch, tc, _ = xa_ref.shape
    assert 2 * batch == SUBLANES
    n_slabs = D_MODEL // LANES
    step = pl.program_id(0)
    live = step < pl.num_programs(0) - 1

    @pl.when(step == 0)
    def _():
        carry_ref[...] = jnp.zeros_like(carry_ref)
        y_ref[...] = jnp.zeros_like(y_ref)

    upd = _s5_gate(y_ref[...], wglu_ref, bglu_ref)
    for c in range(n_slabs):
        tb2_ref[c] = upd[:, LANES * c:LANES * (c + 1)]
    for b in range(batch):
        o_ref[b] = xb_ref[b] + jnp.concatenate(
            [tb2_ref[c, pl.ds(b, tc, stride=batch), :] for c in range(n_slabs)], axis=-1)

    for b in range(batch):
        ub_nat = _rms(xa_ref[b], gn_ref[...])
        for c in range(n_slabs):
            tb_ref[c, pl.ds(b, tc, stride=batch), :] = ub_nat[:, LANES * c:LANES * (c + 1)]
    u = jnp.concatenate([tb_ref[c] for c in range(n_slabs)], axis=-1)
    ub = u.astype(BF16)

    lo = lax.broadcasted_iota(jnp.int32, (SUBLANES, LANES), 0) < batch
    swap = lambda v: pltpu.roll(v, batch, 0)
    n_vregs = batch * tc // SUBLANES
    half_cols = STATE_BLOCK // LANES // 2
    ys = []
    for i in range(N_CH_BLOCKS):
        base = 2 * STATE_BLOCK * i
        st_ref[:, base:base + 2 * STATE_BLOCK] = _dot(ub[:, MXU_TILE * i:MXU_TILE * (i + 1)], bhat_ref[i])
        cols, lam_r, lam_i, sr, si, old = [], [], [], [], [], []
        for j in range(half_cols):
            ca, cb = base + LANES * j, base + LANES * (j + half_cols)
            cols.append((ca, cb))
            la, lb = STATE_BLOCK * i + LANES * j, STATE_BLOCK * i + LANES * (j + half_cols)
            bc = lambda r, c: jnp.broadcast_to(lam_ref[r:r + 1, c:c + LANES], (SUBLANES, LANES))
            lam_r.append(jnp.where(lo, bc(0, la), bc(0, lb)))
            lam_i.append(jnp.where(lo, bc(1, la), bc(1, lb)))
            prev = [carry_ref[:, c:c + LANES] for c in (ca, cb, ca + STATE_BLOCK, cb + STATE_BLOCK)]
            old.append(prev)
            sr.append(jnp.where(lo, swap(prev[0]), prev[1]))
            si.append(jnp.where(lo, swap(prev[2]), prev[3]))
        last = [None] * half_cols
        for m in range(n_vregs):
            rows = slice(SUBLANES * m, SUBLANES * (m + 1))
            for j in range(half_cols):
                ca, cb = cols[j]
                ar, br = st_ref[rows, ca:ca + LANES], st_ref[rows, cb:cb + LANES]
                ai, bi = (st_ref[rows, ca + STATE_BLOCK:ca + STATE_BLOCK + LANES],
                          st_ref[rows, cb + STATE_BLOCK:cb + STATE_BLOCK + LANES])
                d0r, d1r = jnp.where(lo, ar, swap(br)), jnp.where(lo, swap(ar), br)
                d0i, d1i = jnp.where(lo, ai, swap(bi)), jnp.where(lo, swap(ai), bi)
                s0r = lam_r[j] * sr[j] - lam_i[j] * si[j] + d0r
                s0i = lam_r[j] * si[j] + lam_i[j] * sr[j] + d0i
                s1r = lam_r[j] * s0r - lam_i[j] * s0i + d1r
                s1i = lam_r[j] * s0i + lam_i[j] * s0r + d1i
                sr[j], si[j] = s1r, s1i
                out = (jnp.where(lo, s0r, swap(s1r)), jnp.where(lo, swap(s0r), s1r),
                       jnp.where(lo, s0i, swap(s1i)), jnp.where(lo, swap(s0i), s1i))
                st_ref[rows, ca:ca + LANES] = out[0]
                st_ref[rows, cb:cb + LANES] = out[1]
                st_ref[rows, ca + STATE_BLOCK:ca + STATE_BLOCK + LANES] = out[2]
                st_ref[rows, cb + STATE_BLOCK:cb + STATE_BLOCK + LANES] = out[3]
                last[j] = out
        for j in range(half_cols):
            ca, cb = cols[j]
            for c, new, prev in zip((ca, cb, ca + STATE_BLOCK, cb + STATE_BLOCK), last[j], old[j]):
                carry_ref[:, c:c + LANES] = jnp.where(live, new, prev)
        ys.append(_dot(st_ref[:, base:base + 2 * STATE_BLOCK].astype(BF16), chat_ref[i]))
    u = jnp.concatenate([tb_ref[c] for c in range(n_slabs)], axis=-1)
    y_ref[...] = jnp.concatenate(ys, axis=-1) + d_ref[...] * u
    fin_ref[...] = carry_ref[...]


def _s5_prompt(x, layer, gn, bhat, lam, chat, d, wglu, bglu):
    batch, seq, _ = x.shape
    tc = S5_TIME_CHUNK
    n_chunks = seq // tc
    blk = (batch, tc, D_MODEL)
    this_chunk = pl.BlockSpec(blk, lambda s: (0, jnp.minimum(s, n_chunks - 1), 0))
    prev_chunk = pl.BlockSpec(blk, lambda s: (0, jnp.maximum(s - 1, 0), 0))
    slab = pltpu.VMEM((D_MODEL // LANES, batch * tc, LANES), F32)
    return pl.pallas_call(
        _s5_prompt_kernel,
        grid=(n_chunks + 1,),
        in_specs=[this_chunk, prev_chunk]
        + [_layer_spec(w.shape, layer) for w in (gn, bhat, lam, chat, d, wglu, bglu)],
        out_specs=[prev_chunk, pl.BlockSpec((SUBLANES, STATE_LANES), lambda s: (0, 0))],
        out_shape=[jax.ShapeDtypeStruct(x.shape, F32),
                   jax.ShapeDtypeStruct((SUBLANES, STATE_LANES), F32)],
        scratch_shapes=[pltpu.VMEM((batch * tc, STATE_LANES), F32),
                        pltpu.VMEM((SUBLANES, STATE_LANES), F32),
                        slab, slab,
                        pltpu.VMEM((batch * tc, D_MODEL), F32)],
        compiler_params=_params(("arbitrary",)),
        name="s5_prompt",
    )(x, x, gn, bhat, lam, chat, d, wglu, bglu)


def _s5_sample_kernel(x_ref, gn_ref, bhat_ref, lam_ref, chat_ref, d_ref, wglu_ref, bglu_ref, s0_ref,
                      o_ref, snew_ref):
    x = x_ref[...]
    u = _rms(x, gn_ref[...])
    ys = []
    for i in range(N_CH_BLOCKS):
        base = 2 * STATE_BLOCK * i
        bu = jnp.dot(u[:, MXU_TILE * i:MXU_TILE * (i + 1)], bhat_ref[i],
                     preferred_element_type=F32, precision=lax.Precision.HIGHEST)
        lr = lam_ref[0:1, STATE_BLOCK * i:STATE_BLOCK * (i + 1)]
        li = lam_ref[1:2, STATE_BLOCK * i:STATE_BLOCK * (i + 1)]
        s0r = s0_ref[:, base:base + STATE_BLOCK]
        s0i = s0_ref[:, base + STATE_BLOCK:base + 2 * STATE_BLOCK]
        sr = lr * s0r - li * s0i + bu[:, :STATE_BLOCK]
        si = lr * s0i + li * s0r + bu[:, STATE_BLOCK:]
        snew_ref[:, base:base + STATE_BLOCK] = sr
        snew_ref[:, base + STATE_BLOCK:base + 2 * STATE_BLOCK] = si
        sb = jnp.concatenate([sr, si], axis=-1).astype(BF16)
        ys.append(_dot(sb, chat_ref[i]))
    y = jnp.concatenate(ys, axis=-1)
    o_ref[...] = x + _s5_gate(y + d_ref[...] * u, wglu_ref, bglu_ref)


def _s5_sample(x, layer, gn, bhat32, lam, chat, d, wglu, bglu, s0):
    rows = x.shape[0]
    return pl.pallas_call(
        _s5_sample_kernel,
        grid=(1,),
        in_specs=[_const_spec(x.shape)] + [_layer_spec(w.shape, layer) for w in (gn, bhat32, lam, chat, d, wglu, bglu)]
        + [_const_spec(s0.shape)],
        out_specs=[pl.BlockSpec((rows, D_MODEL), lambda i: (0, 0)),
                   pl.BlockSpec((rows, STATE_LANES), lambda i: (0, 0))],
        out_shape=[jax.ShapeDtypeStruct((rows, D_MODEL), F32),
                   jax.ShapeDtypeStruct((rows, STATE_LANES), F32)],
        compiler_params=_params(("arbitrary",)),
        name="s5_sample",
    )(x, gn, bhat32, lam, chat, d, wglu, bglu, s0)


def _s5_weights(a_re, a_im, log_dt, b_re, b_im, c_re, c_im):
    n = a_re.shape[0]
    dt = jnp.exp(log_dt.astype(F32))
    lam = lax.complex(a_re.astype(F32), a_im.astype(F32))
    lam_bar = jnp.exp(lam * dt)
    b = lax.complex(b_re.astype(F32), b_im.astype(F32))
    b_bar = ((lam_bar - 1.0) / lam)[..., None] * b
    gpb = MXU_TILE // GROUP_SIZE
    same_group = jnp.eye(gpb, dtype=F32)[None, None, :, None, :, None]

    def expand_in(w):
        w5 = jnp.swapaxes(w.reshape(n, N_CH_BLOCKS, gpb, STATE_DIM, GROUP_SIZE), 3, 4)
        return (w5[:, :, :, :, None, :] * same_group).reshape(n, N_CH_BLOCKS, MXU_TILE, STATE_BLOCK)

    def expand_out(w):
        w5 = jnp.swapaxes(w.reshape(n, N_CH_BLOCKS, gpb, GROUP_SIZE, STATE_DIM), 3, 4)
        return (w5[:, :, :, :, None, :] * same_group).reshape(n, N_CH_BLOCKS, STATE_BLOCK, MXU_TILE)

    bhat = jnp.concatenate([expand_in(jnp.real(b_bar)), expand_in(jnp.imag(b_bar))], axis=-1)
    chat = jnp.concatenate([expand_out(c_re.astype(F32)), -expand_out(c_im.astype(F32))], axis=2)
    lam_rows = jnp.stack([jnp.real(lam_bar).reshape(n, -1), jnp.imag(lam_bar).reshape(n, -1)], axis=1)
    return bhat, chat.astype(BF16), lam_rows


def _state_to_lanes(s_re, s_im):
    n = s_re.shape[0]
    re = s_re.astype(F32).reshape(n, N_CH_BLOCKS, STATE_BLOCK)
    im = s_im.astype(F32).reshape(n, N_CH_BLOCKS, STATE_BLOCK)
    return jnp.concatenate([re, im], axis=-1).reshape(n, STATE_LANES)


def _lanes_to_state(s):
    n = s.shape[0]
    s4 = s.reshape(n, N_CH_BLOCKS, 2, STATE_BLOCK)
    return (s4[:, :, 0].reshape(n, N_GROUPS, STATE_DIM), s4[:, :, 1].reshape(n, N_GROUPS, STATE_DIM))


def _kv_kernel(x_ref, g_ref, wkv_ref, kg_ref, ones_ref, cos_ref, sin_ref, k_ref, v_ref):
    h = _rms(x_ref[...], g_ref[...]).astype(BF16)
    kv = _dot(h, wkv_ref[...])
    n = k_ref.shape[-1]
    k = kv[:, :n]
    k = k * lax.rsqrt(_seg_mean_sq(k, ones_ref) + EPS) * kg_ref[...]
    k_ref[...] = _rope(k, _tile_lanes(cos_ref[...], n), _tile_lanes(sin_ref[...], n))
    v_ref[...] = kv[:, n:]


def _kv(x, g, wkv, kgain, ones, cos, sin, table_block):
    rows = x.shape[0]
    n = wkv.shape[1] // 2
    tm = min(KV_ROWS, rows)
    return pl.pallas_call(
        _kv_kernel,
        grid=(rows // tm,),
        in_specs=[pl.BlockSpec((tm, D_MODEL), lambda i: (i, 0)),
                  _const_spec((1, D_MODEL)),
                  _const_spec(wkv.shape),
                  _const_spec((1, n)),
                  _const_spec(ones.shape),
                  pl.BlockSpec((tm, LANES), lambda i: (table_block(i), 0)),
                  pl.BlockSpec((tm, LANES), lambda i: (table_block(i), 0))],
        out_specs=[pl.BlockSpec((tm, n), lambda i: (i, 0)),
                   pl.BlockSpec((tm, n), lambda i: (i, 0))],
        out_shape=[jax.ShapeDtypeStruct((rows, n), F32),
                   jax.ShapeDtypeStruct((rows, n), F32)],
        compiler_params=_params(("parallel",)),
        name="kv_proj",
    )(x, g, wkv, kgain, ones, cos, sin)


def _attn_prompt_kernel(sinks_ref, x_ref, g_ref, wq_ref, qg_ref, ones_ref, cos_ref, sin_ref,
                        kp_ref, kc_ref, vp_ref, vc_ref, wo_ref, o_ref):
    tile = pl.program_id(1)
    x = x_ref[...]
    n_blocks = x.shape[0] // WINDOW
    heads_per_kv = N_HEADS // N_KV_HEADS
    h = _rms(x, g_ref[...]).astype(BF16)
    q = _dot(h, wq_ref[...])
    q = q * lax.rsqrt(_seg_mean_sq(q, ones_ref) + EPS) * qg_ref[...]
    q = _rope(q, _tile_lanes(cos_ref[...], D_MODEL), _tile_lanes(sin_ref[...], D_MODEL))
    qb = (q * (HEAD_DIM ** -0.5)).astype(BF16)

    kall = jnp.concatenate([kp_ref[...], kc_ref[...]], axis=0).astype(BF16)
    vall = jnp.concatenate([vp_ref[...], vc_ref[...]], axis=0).astype(BF16)

    qi = lax.broadcasted_iota(jnp.int32, (WINDOW, WINDOW), 0)
    kj = lax.broadcasted_iota(jnp.int32, (WINDOW, WINDOW), 1)
    from_prev = kj > qi
    low_half = lax.broadcasted_iota(jnp.int32, (WINDOW, LANES), 1) < HEAD_DIM
    neg_inf = jnp.full((WINDOW, WINDOW), -jnp.inf, F32)

    block_outs = []
    for blk in range(n_blocks):
        has_prev = tile * n_blocks + blk > 0
        r0 = WINDOW * blk
        outs = []
        for pb in range(N_HEADS // 2):
            g = pb // (heads_per_kv // 2)
            kg = kall[r0:r0 + 2 * WINDOW, LANES * g:LANES * (g + 1)]
            vg = vall[r0:r0 + 2 * WINDOW, LANES * g:LANES * (g + 1)]
            qp = qb[r0:r0 + WINDOW, LANES * pb:LANES * (pb + 1)]
            halves = []
            for par in range(2):
                keep = low_half if par == 0 else jnp.logical_not(low_half)
                qm = jnp.where(keep, qp, jnp.zeros_like(qp))
                sink = sinks_ref[2 * pb + par]
                s = lax.dot_general(qm, kg, (((1,), (1,)), ((), ())), preferred_element_type=F32)
                s_prev = jnp.where(has_prev, s[:, :WINDOW], neg_inf)
                t = jnp.where(from_prev, s_prev, s[:, WINDOW:])
                m = jnp.maximum(jnp.max(t, axis=-1, keepdims=True), sink)
                p = jnp.exp(t - m)
                denom = jnp.sum(p, axis=-1, keepdims=True) + jnp.exp(sink - m)
                zero = jnp.zeros_like(p)
                pcat = jnp.concatenate([jnp.where(from_prev, p, zero), jnp.where(from_prev, zero, p)], axis=1)
                halves.append(_dot(pcat.astype(BF16), vg) / denom)
            outs.append(jnp.where(low_half, halves[0], halves[1]))
        block_outs.append(jnp.concatenate(outs, axis=-1))
    attn = jnp.concatenate(block_outs, axis=0).astype(BF16)
    o_ref[...] = x + _dot(attn, wo_ref[...])


def _attn_prompt(x, g, wq, qgain, ones, cos, sin, kd, vd, sinks, wo):
    bsz, seq, _ = x.shape
    nkv = kd.shape[-1]
    nb = ATTN_BLOCKS
    tq = nb * WINDOW
    x_spec = pl.BlockSpec((None, tq, D_MODEL), lambda b, i, *_: (b, i, 0))
    cur = pl.BlockSpec((None, tq, nkv), lambda b, i, *_: (b, i, 0))
    prev = pl.BlockSpec((None, WINDOW, nkv), lambda b, i, *_: (b, jnp.maximum(nb * i - 1, 0), 0))
    tab = pl.BlockSpec((tq, LANES), lambda b, i, *_: (i, 0))
    grid_spec = pltpu.PrefetchScalarGridSpec(
        num_scalar_prefetch=1,
        grid=(bsz, seq // tq),
        in_specs=[x_spec,
                  _const_spec((1, D_MODEL)),
                  _const_spec(wq.shape),
                  _const_spec((1, D_MODEL)),
                  _const_spec(ones.shape),
                  tab, tab, prev, cur, prev, cur,
                  _const_spec(wo.shape)],
        out_specs=x_spec)
    return pl.pallas_call(
        _attn_prompt_kernel,
        grid_spec=grid_spec,
        out_shape=jax.ShapeDtypeStruct(x.shape, F32),
        compiler_params=_params(("parallel", "arbitrary")),
        name="attn_prompt",
    )(sinks, x, g, wq, qgain, ones, cos, sin, kd, kd, vd, vd, wo)


def _q_sample_kernel(x_ref, g_ref, wq_ref, qg_ref, ones_ref, cos_ref, sin_ref, q_ref):
    h = _rms(x_ref[...], g_ref[...]).astype(BF16)
    q = _dot(h, wq_ref[...])
    n = q.shape[-1]
    q = q * lax.rsqrt(_seg_mean_sq(q, ones_ref) + EPS) * qg_ref[...]
    q = _rope(q, _tile_lanes(cos_ref[...], n), _tile_lanes(sin_ref[...], n))
    q_ref[...] = q * (HEAD_DIM ** -0.5)


def _q_sample(x, g, wq_pad, qgain_pad, ones, cos, sin):
    rows = x.shape[0]
    return pl.pallas_call(
        _q_sample_kernel,
        out_shape=jax.ShapeDtypeStruct((rows, wq_pad.shape[1]), F32),
        compiler_params=pltpu.CompilerParams(vmem_limit_bytes=VMEM_LIMIT_BYTES),
        name="q_sample",
    )(x, g, wq_pad, qgain_pad, ones, cos, sin)


def _attn_sample_kernel(q_ref, k_ref, v_ref, sink_ref, o_ref):
    sink = sink_ref[...]
    for b in range(q_ref.shape[0]):
        s = lax.dot_general(q_ref[b].astype(BF16), k_ref[b].astype(BF16), (((1,), (1,)), ((), ())),
                            preferred_element_type=F32)
        m = jnp.maximum(jnp.max(s, axis=-1, keepdims=True), sink)
        p = jnp.exp(s - m)
        denom = jnp.sum(p, axis=-1, keepdims=True) + jnp.exp(sink - m)
        o_ref[b] = _dot(p.astype(BF16), v_ref[b].astype(BF16)) / denom


def _attn_sample(q3, k, v, sink_col):
    bsz, nh, nkv = q3.shape
    keys = k.shape[1]
    bb = SUBLANES
    return pl.pallas_call(
        _attn_sample_kernel,
        grid=(bsz // bb,),
        in_specs=[pl.BlockSpec((bb, nh, nkv), lambda i: (i, 0, 0)),
                  pl.BlockSpec((bb, keys, nkv), lambda i: (i, 0, 0)),
                  pl.BlockSpec((bb, keys, nkv), lambda i: (i, 0, 0)),
                  _const_spec(sink_col.shape)],
        out_specs=pl.BlockSpec((bb, nh, nkv), lambda i: (i, 0, 0)),
        out_shape=jax.ShapeDtypeStruct((bsz, nh, nkv), F32),
        compiler_params=_params(("parallel",)),
        name="attn_sample",
    )(q3, k, v, sink_col)


def _proj_residual_kernel(x_ref, a_ref, w_ref, o_ref):
    o_ref[...] = x_ref[...] + _dot(a_ref[...].astype(BF16), w_ref[...])


def _proj_residual(x, a, w):
    return pl.pallas_call(
        _proj_residual_kernel,
        out_shape=jax.ShapeDtypeStruct(x.shape, F32),
        compiler_params=pltpu.CompilerParams(vmem_limit_bytes=VMEM_LIMIT_BYTES),
        name="o_proj_sample",
    )(x, a, w)


def _rope_tables(pos):
    half = HEAD_DIM // 2
    inv = ROPE_THETA ** (-jnp.arange(half, dtype=F32) / half)
    ang = pos.astype(F32)[:, None] * inv[None, :]
    cos, sin = jnp.cos(ang), jnp.sin(ang)
    reps = LANES // HEAD_DIM
    return (jnp.tile(jnp.concatenate([cos, cos], axis=-1), (1, reps)),
            jnp.tile(jnp.concatenate([-sin, sin], axis=-1), (1, reps)))


def _block_ones(width, block):
    idx = jnp.arange(width) // block
    return (idx[:, None] == idx[None, :]).astype(BF16)


def _dup_heads(w):
    w3 = w.reshape(w.shape[0], N_KV_HEADS, 1, HEAD_DIM)
    return jnp.broadcast_to(w3, (w.shape[0], N_KV_HEADS, LANES // HEAD_DIM, HEAD_DIM)).reshape(w.shape[0], -1)


def _pad_query_heads(w):
    lead = w.shape[:-1]
    nkv = N_KV_HEADS * HEAD_DIM
    w3 = w.reshape(lead + (N_KV_HEADS, N_HEADS // N_KV_HEADS, 1, HEAD_DIM))
    sel = jnp.eye(N_KV_HEADS, dtype=w.dtype)[:, None, :, None]
    return (w3 * sel).reshape(lead + (N_HEADS * nkv,))


def kernel(x_prompt, x_sample, state_ssm_re, state_ssm_im, cache_k, cache_v, norm_mix, norm_mlp, ssm_a_re, ssm_a_im, ssm_log_dt, ssm_b_re, ssm_b_im, ssm_c_re, ssm_c_im, ssm_d, w_glu, b_glu, norm_kv, w_k, w_v, k_norm, w_q, q_norm, attn_sinks, w_o, w_mlp_in, w_mlp_out):
    bsz, seq, _ = x_prompt.shape
    dec = x_sample.shape[0]
    n_a = ssm_a_re.shape[0]
    depth = norm_mix.shape[0]
    past = 8192
    nkv = N_KV_HEADS * HEAD_DIM
    row = lambda v: v.astype(F32).reshape(1, -1)

    xp = x_prompt.astype(F32)
    xs = x_sample.reshape(dec, D_MODEL)
    rows3 = lambda v: v.astype(F32).reshape(v.shape[0], 1, -1)

    w1 = w_mlp_in.astype(BF16)
    w2 = w_mlp_out.astype(BF16)
    g_mlp = rows3(norm_mlp)
    mlp_p = lambda x, layer: _mlp(x.reshape(bsz * seq, D_MODEL), g_mlp, w1, w2, layer).reshape(bsz, seq, D_MODEL)

    bhat, chat, lam = _s5_weights(ssm_a_re, ssm_a_im, ssm_log_dt, ssm_b_re, ssm_b_im, ssm_c_re, ssm_c_im)
    bhat16 = bhat.astype(BF16)
    wglu = w_glu.astype(BF16)
    g_mix, d_skip, bg = rows3(norm_mix), rows3(ssm_d), rows3(b_glu)
    sp_re, sp_im, ss_re, ss_im = [], [], [], []
    for i in range(n_a):
        xp, fin = _s5_prompt(xp, i, g_mix, bhat16, lam, chat, d_skip, wglu, bg)
        re, im = _lanes_to_state(fin[bsz:])
        sp_re.append(re)
        sp_im.append(im)
        s0 = _state_to_lanes(state_ssm_re[i], state_ssm_im[i])
        xs, snew = _s5_sample(xs, i, g_mix, bhat, lam, chat, d_skip, wglu, bg, s0)
        re, im = _lanes_to_state(snew)
        ss_re.append(re)
        ss_im.append(im)
        xp = mlp_p(xp, i)
        xs = _mlp(xs, g_mlp, w1, w2, i)

    cos_p, sin_p = _rope_tables(jnp.arange(seq, dtype=jnp.int32))
    cos_s, sin_s = _rope_tables(jnp.full((dec,), past, dtype=jnp.int32))
    ones_head = _block_ones(MXU_TILE, HEAD_DIM)
    ones_tile = _block_ones(MXU_TILE, MXU_TILE)
    kgain = row(k_norm)
    tile_gain = lambda gvec, n: jnp.tile(gvec, (1, n // HEAD_DIM))

    wkv_dup = jnp.concatenate([_dup_heads(w_k), _dup_heads(w_v)], axis=1).astype(BF16)
    wkv_std = jnp.concatenate([w_k, w_v], axis=1).astype(BF16)
    kd, vd = _kv(xp.reshape(bsz * seq, D_MODEL), row(norm_kv), wkv_dup, tile_gain(kgain, 2 * nkv), ones_head,
                 cos_p, sin_p, lambda i: i % (seq // KV_ROWS))
    ks_new, vs_new = _kv(xs, row(norm_kv), wkv_std, tile_gain(kgain, nkv), ones_head, cos_s, sin_s,
                         lambda i: i)
    kd = kd.reshape(bsz, seq, 2 * nkv)
    vd = vd.reshape(bsz, seq, 2 * nkv)
    undup = lambda t: t[:, -WINDOW:].reshape(bsz, WINDOW, N_KV_HEADS, LANES // HEAD_DIM, HEAD_DIM)[:, :, :, 0]
    new_k_p, new_v_p = undup(kd), undup(vd)
    new_k_s = jnp.concatenate([cache_k.astype(F32)[:, 1:], ks_new.reshape(dec, 1, N_KV_HEADS, HEAD_DIM)], axis=1)
    new_v_s = jnp.concatenate([cache_v.astype(F32)[:, 1:], vs_new.reshape(dec, 1, N_KV_HEADS, HEAD_DIM)], axis=1)
    keys_s = new_k_s.reshape(dec, WINDOW, nkv)
    vals_s = new_v_s.reshape(dec, WINDOW, nkv)

    for j in range(depth - n_a):
        layer = n_a + j
        gn = row(norm_mix[layer])
        qgain = row(q_norm[j])
        wq = w_q[j].astype(BF16)
        wo = w_o[j].astype(BF16)
        xp = _attn_prompt(xp, gn, wq, tile_gain(qgain, D_MODEL), ones_head, cos_p, sin_p, kd, vd,
                          attn_sinks[j].astype(F32), wo)
        q_pad = _q_sample(xs, gn, _pad_query_heads(w_q[j]).astype(BF16),
                          _pad_query_heads(tile_gain(qgain, D_MODEL)), ones_tile, cos_s, sin_s)
        o_pad = _attn_sample(q_pad.reshape(dec, N_HEADS, nkv), keys_s, vals_s,
                             attn_sinks[j].astype(F32).reshape(N_HEADS, 1))
        wo_pad = _pad_query_heads(w_o[j].T).T.astype(BF16)
        xs = _proj_residual(xs, o_pad.reshape(dec, N_HEADS * nkv), wo_pad)
        xp = mlp_p(xp, layer)
        xs = _mlp(xs, g_mlp, w1, w2, layer)

    return (xp, xs.reshape(dec, 1, D_MODEL),
            jnp.stack(sp_re), jnp.stack(sp_im), new_k_p, new_v_p,
            jnp.stack(ss_re), jnp.stack(ss_im), new_k_s, new_v_s)
```

```python
import functools
import math

import jax
import jax.numpy as jnp
from jax import lax
from jax.experimental import pallas as pl
from jax.experimental.pallas import tpu as pltpu

F32 = jnp.float32
BF16 = jnp.bfloat16

D_MODEL = 1024
N_GROUPS = 64
GROUP_SIZE = 16
STATE_DIM = 64
HEAD_DIM = 64
N_HEADS = 16
N_KV_HEADS = 4
WINDOW = 128
ROPE_THETA = 10000.0
D_FF = 4 * D_MODEL
EPS = 1e-6

SUBLANES = 8
LANES = 128
MXU_TILE = 256
VMEM_LIMIT_BYTES = 56 * 1024 * 1024

N_CH_BLOCKS = D_MODEL // MXU_TILE
STATE_BLOCK = (MXU_TILE // GROUP_SIZE) * STATE_DIM
STATE_LANES = 2 * STATE_BLOCK * N_CH_BLOCKS

S5_TIME_CHUNK = 128
MLP_ROWS = 512
FF_CHUNK = 1024
KV_ROWS = 512
ATTN_BLOCKS = 2


def _const_spec(shape):
    zeros = (0,) * len(shape)
    return pl.BlockSpec(shape, lambda *_: zeros, pipeline_mode=pl.Buffered(1))


def _layer_spec(shape, layer):
    idx = (layer,) + (0,) * (len(shape) - 1)
    return pl.BlockSpec((None,) + tuple(shape[1:]), lambda *_: idx, pipeline_mode=pl.Buffered(1))


def _params(sem):
    return pltpu.CompilerParams(dimension_semantics=sem, vmem_limit_bytes=VMEM_LIMIT_BYTES)


def _rms(x, g):
    return x * lax.rsqrt(jnp.mean(x * x, axis=-1, keepdims=True) + EPS) * g


def _dot(a, b):
    return jnp.dot(a, b, preferred_element_type=F32)


def _seg_mean_sq(x, ones_ref):
    sq = x * x
    hi = sq.astype(BF16)
    lo = (sq - hi.astype(F32)).astype(BF16)
    ones = ones_ref[...]
    outs = []
    for t in range(x.shape[-1] // MXU_TILE):
        sl = slice(MXU_TILE * t, MXU_TILE * (t + 1))
        outs.append(_dot(hi[:, sl], ones) + _dot(lo[:, sl], ones))
    return jnp.concatenate(outs, axis=-1) * (1.0 / HEAD_DIM)


def _tile_lanes(t, n):
    return jnp.concatenate([t] * (n // t.shape[-1]), axis=-1)


def _rope(x, cos, sin_signed):
    n = x.shape[-1]
    lane = lax.broadcasted_iota(jnp.int32, x.shape, 1)
    first = (lane & (HEAD_DIM - 1)) < (HEAD_DIM // 2)
    partner = jnp.where(first, pltpu.roll(x, n - HEAD_DIM // 2, 1), pltpu.roll(x, HEAD_DIM // 2, 1))
    return x * cos + partner * sin_signed


def _mlp_kernel(x_ref, g_ref, w1_ref, w2_ref, o_ref):
    x = x_ref[...]
    h = _rms(x, g_ref[...]).astype(BF16)
    acc = x
    for c in range(D_FF // FF_CHUNK):
        sl = slice(FF_CHUNK * c, FF_CHUNK * (c + 1))
        a = jnp.square(jnp.maximum(_dot(h, w1_ref[:, sl]), 0.0)).astype(BF16)
        acc = acc + _dot(a, w2_ref[sl, :])
    o_ref[...] = acc


def _mlp(x, g, w1, w2, layer):
    rows = x.shape[0]
    tm = min(MLP_ROWS, rows)
    return pl.pallas_call(
        _mlp_kernel,
        grid=(rows // tm,),
        in_specs=[pl.BlockSpec((tm, D_MODEL), lambda i: (i, 0)),
                  _layer_spec(g.shape, layer),
                  _layer_spec(w1.shape, layer),
                  _layer_spec(w2.shape, layer)],
        out_specs=pl.BlockSpec((tm, D_MODEL), lambda i: (i, 0)),
        out_shape=jax.ShapeDtypeStruct((rows, D_MODEL), F32),
        compiler_params=_params(("parallel",)),
        name="mlp",
    )(x, g, w1, w2)


def _s5_gate(y, wglu_ref, bglu_ref):
    g = jax.nn.gelu(y).astype(BF16)
    z = _dot(g, wglu_ref[...]) + bglu_ref[...]
    return z[:, :D_MODEL] * jax.nn.sigmoid(z[:, D_MODEL:])


def _s5_prompt_kernel(xa_ref, xb_ref, gn_ref, bhat_ref, lam_ref, chat_ref, d_ref, wglu_ref, bglu_ref,
                      o_ref, fin_ref, st_ref, carry_ref, tb_ref, tb2_ref, y_ref):
    batch, tc, _ = xa_ref.shape
    assert 2 * batch == SUBLANES
    n_slabs = D_MODEL // LANES
    step = pl.program_id(0)
    live = step < pl.num_programs(0) - 1

    @pl.when(step == 0)
    def _():
        carry_ref[...] = jnp.zeros_like(carry_ref)
        y_ref[...] = jnp.zeros_like(y_ref)

    upd = _s5_gate(y_ref[...], wglu_ref, bglu_ref)
    for c in range(n_slabs):
        tb2_ref[c] = upd[:, LANES * c:LANES * (c + 1)]
    for b in range(batch):
        o_ref[b] = xb_ref[b] + jnp.concatenate(
            [tb2_ref[c, pl.ds(b, tc, stride=batch), :] for c in range(n_slabs)], axis=-1)

    for b in range(batch):
        ub_nat = _rms(xa_ref[b], gn_ref[...])
        for c in range(n_slabs):
            tb_ref[c, pl.ds(b, tc, stride=batch), :] = ub_nat[:, LANES * c:LANES * (c + 1)]
    u = jnp.concatenate([tb_ref[c] for c in range(n_slabs)], axis=-1)
    ub = u.astype(BF16)

    lo = lax.broadcasted_iota(jnp.int32, (SUBLANES, LANES), 0) < batch
    swap = lambda v: pltpu.roll(v, batch, 0)
    n_vregs = batch * tc // SUBLANES
    half_cols = STATE_BLOCK // LANES // 2
    ys = []
    for i in range(N_CH_BLOCKS):
        base = 2 * STATE_BLOCK * i
        st_ref[:, base:base + 2 * STATE_BLOCK] = _dot(ub[:, MXU_TILE * i:MXU_TILE * (i + 1)], bhat_ref[i])
        cols, lam_r, lam_i, sr, si, old = [], [], [], [], [], []
        for j in range(half_cols):
            ca, cb = base + LANES * j, base + LANES * (j + half_cols)
            cols.append((ca, cb))
            la, lb = STATE_BLOCK * i + LANES * j, STATE_BLOCK * i + LANES * (j + half_cols)
            bc = lambda r, c: jnp.broadcast_to(lam_ref[r:r + 1, c:c + LANES], (SUBLANES, LANES))
            lam_r.append(jnp.where(lo, bc(0, la), bc(0, lb)))
            lam_i.append(jnp.where(lo, bc(1, la), bc(1, lb)))
            prev = [carry_ref[:, c:c + LANES] for c in (ca, cb, ca + STATE_BLOCK, cb + STATE_BLOCK)]
            old.append(prev)
            sr.append(jnp.where(lo, swap(prev[0]), prev[1]))
            si.append(jnp.where(lo, swap(prev[2]), prev[3]))
        last = [None] * half_cols
        for m in range(n_vregs):
            rows = slice(SUBLANES * m, SUBLANES * (m + 1))
            for j in range(half_cols):
                ca, cb = cols[j]
                ar, br = st_ref[rows, ca:ca + LANES], st_ref[rows, cb:cb + LANES]
                ai, bi = (st_ref[rows, ca + STATE_BLOCK:ca + STATE_BLOCK + LANES],
                          st_ref[rows, cb + STATE_BLOCK:cb + STATE_BLOCK + LANES])
                d0r, d1r = jnp.where(lo, ar, swap(br)), jnp.where(lo, swap(ar), br)
                d0i, d1i = jnp.where(lo, ai, swap(bi)), jnp.where(lo, swap(ai), bi)
                s0r = lam_r[j] * sr[j] - lam_i[j] * si[j] + d0r
                s0i = lam_r[j] * si[j] + lam_i[j] * sr[j] + d0i
                s1r = lam_r[j] * s0r - lam_i[j] * s0i + d1r
                s1i = lam_r[j] * s0i + lam_i[j] * s0r + d1i
                sr[j], si[j] = s1r, s1i
                out = (jnp.where(lo, s0r, swap(s1r)), jnp.where(lo, swap(s0r), s1r),
                       jnp.where(lo, s0i, swap(s1i)), jnp.where(lo, swap(s0i), s1i))
                st_ref[rows, ca:ca + LANES] = out[0]
                st_ref[rows, cb:cb + LANES] = out[1]
                st_ref[rows, ca + STATE_BLOCK:ca + STATE_BLOCK + LANES] = out[2]
                st_ref[rows, cb + STATE_BLOCK:cb + STATE_BLOCK + LANES] = out[3]
                last[j] = out
        for j in range(half_cols):
            ca, cb = cols[j]
            for c, new, prev in zip((ca, cb, ca + STATE_BLOCK, cb + STATE_BLOCK), last[j], old[j]):
                carry_ref[:, c:c + LANES] = jnp.where(live, new, prev)
        ys.append(_dot(st_ref[:, base:base + 2 * STATE_BLOCK].astype(BF16), chat_ref[i]))
    u = jnp.concatenate([tb_ref[c] for c in range(n_slabs)], axis=-1)
    y_ref[...] = jnp.concatenate(ys, axis=-1) + d_ref[...] * u
    fin_ref[...] = carry_ref[...]


def _s5_prompt(x, layer, gn, bhat, lam, chat, d, wglu, bglu):
    batch, seq, _ = x.shape
    tc = S5_TIME_CHUNK
    n_chunks = seq // tc
    blk = (batch, tc, D_MODEL)
    this_chunk = pl.BlockSpec(blk, lambda s: (0, jnp.minimum(s, n_chunks - 1), 0))
    prev_chunk = pl.BlockSpec(blk, lambda s: (0, jnp.maximum(s - 1, 0), 0))
    slab = pltpu.VMEM((D_MODEL // LANES, batch * tc, LANES), F32)
    return pl.pallas_call(
        _s5_prompt_kernel,
        grid=(n_chunks + 1,),
        in_specs=[this_chunk, prev_chunk]
        + [_layer_spec(w.shape, layer) for w in (gn, bhat, lam, chat, d, wglu, bglu)],
        out_specs=[prev_chunk, pl.BlockSpec((SUBLANES, STATE_LANES), lambda s: (0, 0))],
        out_shape=[jax.ShapeDtypeStruct(x.shape, F32),
                   jax.ShapeDtypeStruct((SUBLANES, STATE_LANES), F32)],
        scratch_shapes=[pltpu.VMEM((batch * tc, STATE_LANES), F32),
                        pltpu.VMEM((SUBLANES, STATE_LANES), F32),
                        slab, slab,
                        pltpu.VMEM((batch * tc, D_MODEL), F32)],
        compiler_params=_params(("arbitrary",)),
        name="s5_prompt",
    )(x, x, gn, bhat, lam, chat, d, wglu, bglu)


def _s5_sample_kernel(x_ref, gn_ref, bhat_ref, lam_ref, chat_ref, d_ref, wglu_ref, bglu_ref, s0_ref,
                      o_ref, snew_ref):
    x = x_ref[...]
    u = _rms(x, gn_ref[...])
    ys = []
    for i in range(N_CH_BLOCKS):
        base = 2 * STATE_BLOCK * i
        bu = jnp.dot(u[:, MXU_TILE * i:MXU_TILE * (i + 1)], bhat_ref[i],
                     preferred_element_type=F32, precision=lax.Precision.HIGHEST)
        lr = lam_ref[0:1, STATE_BLOCK * i:STATE_BLOCK * (i + 1)]
        li = lam_ref[1:2, STATE_BLOCK * i:STATE_BLOCK * (i + 1)]
        s0r = s0_ref[:, base:base + STATE_BLOCK]
        s0i = s0_ref[:, base + STATE_BLOCK:base + 2 * STATE_BLOCK]
        sr = lr * s0r - li * s0i + bu[:, :STATE_BLOCK]
        si = lr * s0i + li * s0r + bu[:, STATE_BLOCK:]
        snew_ref[:, base:base + STATE_BLOCK] = sr
        snew_ref[:, base + STATE_BLOCK:base + 2 * STATE_BLOCK] = si
        sb = jnp.concatenate([sr, si], axis=-1).astype(BF16)
        ys.append(_dot(sb, chat_ref[i]))
    y = jnp.concatenate(ys, axis=-1)
    o_ref[...] = x + _s5_gate(y + d_ref[...] * u, wglu_ref, bglu_ref)


def _s5_sample(x, layer, gn, bhat32, lam, chat, d, wglu, bglu, s0):
    rows = x.shape[0]
    return pl.pallas_call(
        _s5_sample_kernel,
        grid=(1,),
        in_specs=[_const_spec(x.shape)] + [_layer_spec(w.shape, layer) for w in (gn, bhat32, lam, chat, d, wglu, bglu)]
        + [_const_spec(s0.shape)],
        out_specs=[pl.BlockSpec((rows, D_MODEL), lambda i: (0, 0)),
                   pl.BlockSpec((rows, STATE_LANES), lambda i: (0, 0))],
        out_shape=[jax.ShapeDtypeStruct((rows, D_MODEL), F32),
                   jax.ShapeDtypeStruct((rows, STATE_LANES), F32)],
        compiler_params=_params(("arbitrary",)),
        name="s5_sample",
    )(x, gn, bhat32, lam, chat, d, wglu, bglu, s0)


def _s5_weights(a_re, a_im, log_dt, b_re, b_im, c_re, c_im):
    n = a_re.shape[0]
    dt = jnp.exp(log_dt.astype(F32))
    lam = lax.complex(a_re.astype(F32), a_im.astype(F32))
    lam_bar = jnp.exp(lam * dt)
    b = lax.complex(b_re.astype(F32), b_im.astype(F32))
    b_bar = ((lam_bar - 1.0) / lam)[..., None] * b
    gpb = MXU_TILE // GROUP_SIZE
    hi = lax.Precision.HIGHEST

    def compact_in(w):
        return jnp.swapaxes(w.reshape(n, N_CH_BLOCKS, gpb, STATE_DIM, GROUP_SIZE), 3, 4).reshape(
            n, N_CH_BLOCKS, MXU_TILE, STATE_DIM)

    def compact_out(w):
        return jnp.swapaxes(w.reshape(n, N_CH_BLOCKS, gpb, GROUP_SIZE, STATE_DIM), 3, 4).reshape(
            n, N_CH_BLOCKS, STATE_BLOCK, GROUP_SIZE)

    lane = jnp.arange(2 * STATE_BLOCK)
    tile_in = (jnp.arange(2 * STATE_DIM)[:, None] == (lane // STATE_BLOCK) * STATE_DIM + lane % STATE_DIM)
    own_in = (jnp.arange(MXU_TILE)[:, None] // GROUP_SIZE == (lane % STATE_BLOCK) // STATE_DIM)
    bc = jnp.concatenate([compact_in(jnp.real(b_bar)), compact_in(jnp.imag(b_bar))], axis=-1)
    bhat = jnp.einsum("nirk,kl->nirl", bc, tile_in.astype(F32), precision=hi) * own_in.astype(F32)

    col = jnp.arange(MXU_TILE)
    tile_out = (jnp.arange(GROUP_SIZE)[:, None] == col % GROUP_SIZE)
    own_out = ((lane[:, None] % STATE_BLOCK) // STATE_DIM == col // GROUP_SIZE)
    cc = jnp.concatenate([compact_out(c_re.astype(F32)), -compact_out(c_im.astype(F32))], axis=2)
    chat = jnp.einsum("nirc,cl->nirl", cc, tile_out.astype(F32), precision=hi) * own_out.astype(F32)
    lam_rows = jnp.stack([jnp.real(lam_bar).reshape(n, -1), jnp.imag(lam_bar).reshape(n, -1)], axis=1)
    return bhat, chat.astype(BF16), lam_rows


def _state_to_lanes(s_re, s_im):
    n = s_re.shape[0]
    re = s_re.astype(F32).reshape(n, N_CH_BLOCKS, STATE_BLOCK)
    im = s_im.astype(F32).reshape(n, N_CH_BLOCKS, STATE_BLOCK)
    return jnp.concatenate([re, im], axis=-1).reshape(n, STATE_LANES)


def _lanes_to_state(s):
    n = s.shape[0]
    s4 = s.reshape(n, N_CH_BLOCKS, 2, STATE_BLOCK)
    return (s4[:, :, 0].reshape(n, N_GROUPS, STATE_DIM), s4[:, :, 1].reshape(n, N_GROUPS, STATE_DIM))


def _rope_partner(n):
    lane = jnp.arange(n)
    return jnp.where(lane % HEAD_DIM < HEAD_DIM // 2, lane + HEAD_DIM // 2, lane - HEAD_DIM // 2)


def _kv_kernel(x_ref, g_ref, wkv_ref, kg_ref, kgp_ref, ones_ref, cos_ref, sin_ref, k_ref, v_ref):
    h = _rms(x_ref[...], g_ref[...]).astype(BF16)
    kv = _dot(h, wkv_ref[...])
    n = k_ref.shape[-1]
    k, k_partner = kv[:, :n], kv[:, n:2 * n]
    scale = lax.rsqrt(_seg_mean_sq(k, ones_ref) + EPS)
    k_ref[...] = scale * (k * (kg_ref[...] * _tile_lanes(cos_ref[...], n))
                          + k_partner * (kgp_ref[...] * _tile_lanes(sin_ref[...], n)))
    v_ref[...] = kv[:, 2 * n:]


def _kv(x, g, wk, wv, kgain, ones, cos, sin, table_block):
    rows = x.shape[0]
    n = wk.shape[1]
    partner = _rope_partner(n)
    wkv = jnp.concatenate([wk, wk[:, partner], wv], axis=1).astype(BF16)
    tm = min(KV_ROWS, rows)
    return pl.pallas_call(
        _kv_kernel,
        grid=(rows // tm,),
        in_specs=[pl.BlockSpec((tm, D_MODEL), lambda i: (i, 0)),
                  _const_spec((1, D_MODEL)),
                  _const_spec(wkv.shape),
                  _const_spec((1, n)),
                  _const_spec((1, n)),
                  _const_spec(ones.shape),
                  pl.BlockSpec((tm, LANES), lambda i: (table_block(i), 0)),
                  pl.BlockSpec((tm, LANES), lambda i: (table_block(i), 0))],
        out_specs=[pl.BlockSpec((tm, n), lambda i: (i, 0)),
                   pl.BlockSpec((tm, n), lambda i: (i, 0))],
        out_shape=[jax.ShapeDtypeStruct((rows, n), F32),
                   jax.ShapeDtypeStruct((rows, n), F32)],
        compiler_params=_params(("parallel",)),
        name="kv_proj",
    )(x, g, wkv, kgain, kgain[:, partner], ones, cos, sin)


def _attn_prompt_kernel(sinks_ref, x_ref, g_ref, wq_ref, qg_ref, ones_ref, cos_ref, sin_ref,
                        kp_ref, kc_ref, vp_ref, vc_ref, wo_ref, o_ref):
    tile = pl.program_id(1)
    x = x_ref[...]
    n_blocks = x.shape[0] // WINDOW
    heads_per_kv = N_HEADS // N_KV_HEADS
    h = _rms(x, g_ref[...]).astype(BF16)
    q = _dot(h, wq_ref[...])
    q = q * lax.rsqrt(_seg_mean_sq(q, ones_ref) + EPS) * qg_ref[...]
    q = _rope(q, _tile_lanes(cos_ref[...], D_MODEL), _tile_lanes(sin_ref[...], D_MODEL))
    qb = (q * (HEAD_DIM ** -0.5)).astype(BF16)

    kall = jnp.concatenate([kp_ref[...], kc_ref[...]], axis=0).astype(BF16)
    vall = jnp.concatenate([vp_ref[...], vc_ref[...]], axis=0).astype(BF16)

    qi = lax.broadcasted_iota(jnp.int32, (WINDOW, WINDOW), 0)
    kj = lax.broadcasted_iota(jnp.int32, (WINDOW, WINDOW), 1)
    from_prev = kj > qi
    low_half = lax.broadcasted_iota(jnp.int32, (WINDOW, LANES), 1) < HEAD_DIM
    neg_inf = jnp.full((WINDOW, WINDOW), -jnp.inf, F32)

    block_outs = []
    for blk in range(n_blocks):
        has_prev = tile * n_blocks + blk > 0
        r0 = WINDOW * blk
        outs = []
        for pb in range(N_HEADS // 2):
            g = pb // (heads_per_kv // 2)
            kg = kall[r0:r0 + 2 * WINDOW, LANES * g:LANES * (g + 1)]
            vg = vall[r0:r0 + 2 * WINDOW, LANES * g:LANES * (g + 1)]
            qp = qb[r0:r0 + WINDOW, LANES * pb:LANES * (pb + 1)]
            halves = []
            for par in range(2):
                keep = low_half if par == 0 else jnp.logical_not(low_half)
                qm = jnp.where(keep, qp, jnp.zeros_like(qp))
                sink = sinks_ref[2 * pb + par]
                s = lax.dot_general(qm, kg, (((1,), (1,)), ((), ())), preferred_element_type=F32)
                s_prev = jnp.where(has_prev, s[:, :WINDOW], neg_inf)
                t = jnp.where(from_prev, s_prev, s[:, WINDOW:])
                m = jnp.maximum(jnp.max(t, axis=-1, keepdims=True), sink)
                p = jnp.exp(t - m)
                denom = jnp.sum(p, axis=-1, keepdims=True) + jnp.exp(sink - m)
                zero = jnp.zeros_like(p)
                pcat = jnp.concatenate([jnp.where(from_prev, p, zero), jnp.where(from_prev, zero, p)], axis=1)
                halves.append(_dot(pcat.astype(BF16), vg) / denom)
            outs.append(jnp.where(low_half, halves[0], halves[1]))
        block_outs.append(jnp.concatenate(outs, axis=-1))
    attn = jnp.concatenate(block_outs, axis=0).astype(BF16)
    o_ref[...] = x + _dot(attn, wo_ref[...])


def _attn_prompt(x, g, wq, qgain, ones, cos, sin, kd, vd, sinks, wo):
    bsz, seq, _ = x.shape
    nkv = kd.shape[-1]
    nb = ATTN_BLOCKS
    tq = nb * WINDOW
    x_spec = pl.BlockSpec((None, tq, D_MODEL), lambda b, i, *_: (b, i, 0))
    cur = pl.BlockSpec((None, tq, nkv), lambda b, i, *_: (b, i, 0))
    prev = pl.BlockSpec((None, WINDOW, nkv), lambda b, i, *_: (b, jnp.maximum(nb * i - 1, 0), 0))
    tab = pl.BlockSpec((tq, LANES), lambda b, i, *_: (i, 0))
    grid_spec = pltpu.PrefetchScalarGridSpec(
        num_scalar_prefetch=1,
        grid=(bsz, seq // tq),
        in_specs=[x_spec,
                  _const_spec((1, D_MODEL)),
                  _const_spec(wq.shape),
                  _const_spec((1, D_MODEL)),
                  _const_spec(ones.shape),
                  tab, tab, prev, cur, prev, cur,
                  _const_spec(wo.shape)],
        out_specs=x_spec)
    return pl.pallas_call(
        _attn_prompt_kernel,
        grid_spec=grid_spec,
        out_shape=jax.ShapeDtypeStruct(x.shape, F32),
        compiler_params=_params(("parallel", "arbitrary")),
        name="attn_prompt",
    )(sinks, x, g, wq, qgain, ones, cos, sin, kd, kd, vd, vd, wo)


def _own_head(shape):
    row_head = lax.broadcasted_iota(jnp.int32, shape, 0) & (N_HEADS - 1)
    lane_head = lax.broadcasted_iota(jnp.int32, shape, 1) // HEAD_DIM
    return row_head == lane_head


def _q_sample_kernel(x_ref, g_ref, wq_ref, qg_ref, ones_ref, cos_ref, sin_ref, rep_ref, place_ref, q_ref):
    h = _rms(x_ref[...], g_ref[...]).astype(BF16)
    q = _dot(h, wq_ref[...])
    q = q * lax.rsqrt(_seg_mean_sq(q, ones_ref) + EPS) * qg_ref[...]
    q = _rope(q, _tile_lanes(cos_ref[...], D_MODEL), _tile_lanes(sin_ref[...], D_MODEL))
    qb = (q * (HEAD_DIM ** -0.5)).astype(BF16)
    q_rep = _dot(rep_ref[...], qb)
    q_own = jnp.where(_own_head(q_rep.shape), q_rep, 0.0).astype(BF16)
    q_ref[...] = _dot(q_own, place_ref[...])


def _q_sample(x, g, wq, qgain, ones, cos, sin, rep, place):
    rows = x.shape[0]
    return pl.pallas_call(
        _q_sample_kernel,
        out_shape=jax.ShapeDtypeStruct((rows * N_HEADS, place.shape[1]), F32),
        compiler_params=pltpu.CompilerParams(vmem_limit_bytes=VMEM_LIMIT_BYTES),
        name="q_sample",
    )(x, g, wq, qgain, ones, cos, sin, rep, place)


def _attn_sample_kernel(q_ref, k_ref, v_ref, sink_ref, o_ref):
    sink = sink_ref[...]
    for b in range(q_ref.shape[0]):
        s = lax.dot_general(q_ref[b].astype(BF16), k_ref[b].astype(BF16), (((1,), (1,)), ((), ())),
                            preferred_element_type=F32)
        m = jnp.maximum(jnp.max(s, axis=-1, keepdims=True), sink)
        p = jnp.exp(s - m)
        denom = jnp.sum(p, axis=-1, keepdims=True) + jnp.exp(sink - m)
        o_ref[b] = _dot(p.astype(BF16), v_ref[b].astype(BF16)) / denom


def _attn_sample(q3, k, v, sink_col):
    bsz, nh, nkv = q3.shape
    keys = k.shape[1]
    bb = SUBLANES
    return pl.pallas_call(
        _attn_sample_kernel,
        grid=(bsz // bb,),
        in_specs=[pl.BlockSpec((bb, nh, nkv), lambda i: (i, 0, 0)),
                  pl.BlockSpec((bb, keys, nkv), lambda i: (i, 0, 0)),
                  pl.BlockSpec((bb, keys, nkv), lambda i: (i, 0, 0)),
                  _const_spec(sink_col.shape)],
        out_specs=pl.BlockSpec((bb, nh, nkv), lambda i: (i, 0, 0)),
        out_shape=jax.ShapeDtypeStruct((bsz, nh, nkv), F32),
        compiler_params=_params(("parallel",)),
        name="attn_sample",
    )(q3, k, v, sink_col)


def _proj_residual_kernel(x_ref, a_ref, place_t_ref, rep_t_ref, w_ref, o_ref):
    z = _dot(a_ref[...].astype(BF16), place_t_ref[...])
    z_own = jnp.where(_own_head(z.shape), z, 0.0).astype(BF16)
    attn = _dot(rep_t_ref[...], z_own).astype(BF16)
    o_ref[...] = x_ref[...] + _dot(attn, w_ref[...])


def _proj_residual(x, a, place_t, rep_t, w):
    return pl.pallas_call(
        _proj_residual_kernel,
        out_shape=jax.ShapeDtypeStruct(x.shape, F32),
        compiler_params=pltpu.CompilerParams(vmem_limit_bytes=VMEM_LIMIT_BYTES),
        name="o_proj_sample",
    )(x, a, place_t, rep_t, w)


def _rope_tables(pos):
    half = HEAD_DIM // 2
    inv = ROPE_THETA ** (-jnp.arange(half, dtype=F32) / half)
    ang = pos.astype(F32)[:, None] * inv[None, :]
    cos, sin = jnp.cos(ang), jnp.sin(ang)
    reps = LANES // HEAD_DIM
    return (jnp.tile(jnp.concatenate([cos, cos], axis=-1), (1, reps)),
            jnp.tile(jnp.concatenate([-sin, sin], axis=-1), (1, reps)))


def _block_ones(width, block):
    idx = jnp.arange(width) // block
    return (idx[:, None] == idx[None, :]).astype(BF16)


def _dup_heads(w):
    w3 = w.reshape(w.shape[0], N_KV_HEADS, 1, HEAD_DIM)
    return jnp.broadcast_to(w3, (w.shape[0], N_KV_HEADS, LANES // HEAD_DIM, HEAD_DIM)).reshape(w.shape[0], -1)


def _head_placement():
    lane = jnp.arange(N_HEADS * HEAD_DIM)
    dst = (lane // HEAD_DIM) // (N_HEADS // N_KV_HEADS) * HEAD_DIM + lane % HEAD_DIM
    return (dst[:, None] == jnp.arange(N_KV_HEADS * HEAD_DIM)[None, :]).astype(BF16)


def _row_replication(n):
    return (jnp.arange(n * N_HEADS)[:, None] // N_HEADS == jnp.arange(n)[None, :]).astype(BF16)


def kernel(x_prompt, x_sample, state_ssm_re, state_ssm_im, cache_k, cache_v, norm_mix, norm_mlp, ssm_a_re, ssm_a_im, ssm_log_dt, ssm_b_re, ssm_b_im, ssm_c_re, ssm_c_im, ssm_d, w_glu, b_glu, norm_kv, w_k, w_v, k_norm, w_q, q_norm, attn_sinks, w_o, w_mlp_in, w_mlp_out):
    bsz, seq, _ = x_prompt.shape
    dec = x_sample.shape[0]
    n_a = ssm_a_re.shape[0]
    depth = norm_mix.shape[0]
    past = 8192
    nkv = N_KV_HEADS * HEAD_DIM
    row = lambda v: v.astype(F32).reshape(1, -1)

    xp = x_prompt.astype(F32)
    xs = x_sample.reshape(dec, D_MODEL)
    rows3 = lambda v: v.astype(F32).reshape(v.shape[0], 1, -1)

    w1 = w_mlp_in.astype(BF16)
    w2 = w_mlp_out.astype(BF16)
    g_mlp = rows3(norm_mlp)
    mlp_p = lambda x, layer: _mlp(x.reshape(bsz * seq, D_MODEL), g_mlp, w1, w2, layer).reshape(bsz, seq, D_MODEL)

    bhat, chat, lam = _s5_weights(ssm_a_re, ssm_a_im, ssm_log_dt, ssm_b_re, ssm_b_im, ssm_c_re, ssm_c_im)
    bhat16 = bhat.astype(BF16)
    wglu = w_glu.astype(BF16)
    g_mix, d_skip, bg = rows3(norm_mix), rows3(ssm_d), rows3(b_glu)
    sp_re, sp_im, ss_re, ss_im = [], [], [], []
    for i in range(n_a):
        xp, fin = _s5_prompt(xp, i, g_mix, bhat16, lam, chat, d_skip, wglu, bg)
        re, im = _lanes_to_state(fin[bsz:])
        sp_re.append(re)
        sp_im.append(im)
        s0 = _state_to_lanes(state_ssm_re[i], state_ssm_im[i])
        xs, snew = _s5_sample(xs, i, g_mix, bhat, lam, chat, d_skip, wglu, bg, s0)
        re, im = _lanes_to_state(snew)
        ss_re.append(re)
        ss_im.append(im)
        xp = mlp_p(xp, i)
        xs = _mlp(xs, g_mlp, w1, w2, i)

    cos_p, sin_p = _rope_tables(jnp.arange(seq, dtype=jnp.int32))
    cos_s, sin_s = _rope_tables(jnp.full((dec,), past, dtype=jnp.int32))
    ones_head = _block_ones(MXU_TILE, HEAD_DIM)
    place = _head_placement()
    rep = _row_replication(dec)
    kgain = row(k_norm)
    tile_gain = lambda gvec, n: jnp.tile(gvec, (1, n // HEAD_DIM))

    kd, vd = _kv(xp.reshape(bsz * seq, D_MODEL), row(norm_kv), _dup_heads(w_k), _dup_heads(w_v),
                 tile_gain(kgain, 2 * nkv), ones_head, cos_p, sin_p, lambda i: i % (seq // KV_ROWS))
    ks_new, vs_new = _kv(xs, row(norm_kv), w_k, w_v, tile_gain(kgain, nkv), ones_head, cos_s, sin_s,
                         lambda i: i)
    kd = kd.reshape(bsz, seq, 2 * nkv)
    vd = vd.reshape(bsz, seq, 2 * nkv)
    undup = lambda t: t[:, -WINDOW:].reshape(bsz, WINDOW, N_KV_HEADS, LANES // HEAD_DIM, HEAD_DIM)[:, :, :, 0]
    new_k_p, new_v_p = undup(kd), undup(vd)
    new_k_s = jnp.concatenate([cache_k.astype(F32)[:, 1:], ks_new.reshape(dec, 1, N_KV_HEADS, HEAD_DIM)], axis=1)
    new_v_s = jnp.concatenate([cache_v.astype(F32)[:, 1:], vs_new.reshape(dec, 1, N_KV_HEADS, HEAD_DIM)], axis=1)
    keys_s = new_k_s.reshape(dec, WINDOW, nkv)
    vals_s = new_v_s.reshape(dec, WINDOW, nkv)

    for j in range(depth - n_a):
        layer = n_a + j
        gn = row(norm_mix[layer])
        qgain = row(q_norm[j])
        wq = w_q[j].astype(BF16)
        wo = w_o[j].astype(BF16)
        xp = _attn_prompt(xp, gn, wq, tile_gain(qgain, D_MODEL), ones_head, cos_p, sin_p, kd, vd,
                          attn_sinks[j].astype(F32), wo)
        q_pad = _q_sample(xs, gn, wq, tile_gain(qgain, D_MODEL), ones_head, cos_s, sin_s, rep, place)
        o_pad = _attn_sample(q_pad.reshape(dec, N_HEADS, nkv), keys_s, vals_s,
                             attn_sinks[j].astype(F32).reshape(N_HEADS, 1))
        xs = _proj_residual(xs, o_pad.reshape(dec * N_HEADS, nkv), place.T, rep.T, wo)
        xp = mlp_p(xp, layer)
        xs = _mlp(xs, g_mlp, w1, w2, layer)

    return (xp, xs.reshape(dec, 1, D_MODEL),
            jnp.stack(sp_re), jnp.stack(sp_im), new_k_p, new_v_p,
            jnp.stack(ss_re), jnp.stack(ss_im), new_k_s, new_v_s)
```

```python
import functools
import math

import jax
import jax.numpy as jnp
from jax import lax
from jax.experimental import pallas as pl
from jax.experimental.pallas import tpu as pltpu

F32 = jnp.float32
BF16 = jnp.bfloat16

D_MODEL = 1024
N_GROUPS = 64
GROUP_SIZE = 16
STATE_DIM = 64
HEAD_DIM = 64
N_HEADS = 16
N_KV_HEADS = 4
WINDOW = 128
ROPE_THETA = 10000.0
D_FF = 4 * D_MODEL
EPS = 1e-6

SUBLANES = 8
LANES = 128
MXU_TILE = 256
VMEM_LIMIT_BYTES = 56 * 1024 * 1024

N_CH_BLOCKS = D_MODEL // MXU_TILE
STATE_BLOCK = (MXU_TILE // GROUP_SIZE) * STATE_DIM
STATE_LANES = 2 * STATE_BLOCK * N_CH_BLOCKS

S5_TIME_CHUNK = 128
MLP_ROWS = 512
FF_CHUNK = 1024
KV_ROWS = 512
ATTN_BLOCKS = 2


def _const_spec(shape):
    zeros = (0,) * len(shape)
    return pl.BlockSpec(shape, lambda *_: zeros, pipeline_mode=pl.Buffered(1))


def _layer_spec(shape, layer):
    idx = (layer,) + (0,) * (len(shape) - 1)
    return pl.BlockSpec((None,) + tuple(shape[1:]), lambda *_: idx, pipeline_mode=pl.Buffered(1))


def _params(sem):
    return pltpu.CompilerParams(dimension_semantics=sem, vmem_limit_bytes=VMEM_LIMIT_BYTES)


def _rms(x, g):
    return x * lax.rsqrt(jnp.mean(x * x, axis=-1, keepdims=True) + EPS) * g


def _dot(a, b):
    return jnp.dot(a, b, preferred_element_type=F32)


def _seg_mean_sq(x, ones_ref):
    sq = x * x
    hi = sq.astype(BF16)
    lo = (sq - hi.astype(F32)).astype(BF16)
    ones = ones_ref[...]
    outs = []
    for t in range(x.shape[-1] // MXU_TILE):
        sl = slice(MXU_TILE * t, MXU_TILE * (t + 1))
        outs.append(_dot(hi[:, sl], ones) + _dot(lo[:, sl], ones))
    return jnp.concatenate(outs, axis=-1) * (1.0 / HEAD_DIM)


def _tile_lanes(t, n):
    return jnp.concatenate([t] * (n // t.shape[-1]), axis=-1)


def _rope(x, cos, sin_signed):
    n = x.shape[-1]
    lane = lax.broadcasted_iota(jnp.int32, x.shape, 1)
    first = (lane & (HEAD_DIM - 1)) < (HEAD_DIM // 2)
    partner = jnp.where(first, pltpu.roll(x, n - HEAD_DIM // 2, 1), pltpu.roll(x, HEAD_DIM // 2, 1))
    return x * cos + partner * sin_signed


def _mlp_kernel(x_ref, g_ref, w1_ref, w2_ref, o_ref):
    x = x_ref[...]
    h = _rms(x, g_ref[...]).astype(BF16)
    acc = x
    for c in range(D_FF // FF_CHUNK):
        sl = slice(FF_CHUNK * c, FF_CHUNK * (c + 1))
        a = jnp.square(jnp.maximum(_dot(h, w1_ref[:, sl]), 0.0)).astype(BF16)
        acc = acc + _dot(a, w2_ref[sl, :])
    o_ref[...] = acc


def _mlp(x, g, w1, w2, layer):
    rows = x.shape[0]
    tm = min(MLP_ROWS, rows)
    return pl.pallas_call(
        _mlp_kernel,
        grid=(rows // tm,),
        in_specs=[pl.BlockSpec((tm, D_MODEL), lambda i: (i, 0)),
                  _layer_spec(g.shape, layer),
                  _layer_spec(w1.shape, layer),
                  _layer_spec(w2.shape, layer)],
        out_specs=pl.BlockSpec((tm, D_MODEL), lambda i: (i, 0)),
        out_shape=jax.ShapeDtypeStruct((rows, D_MODEL), F32),
        compiler_params=_params(("parallel",)),
        name="mlp",
    )(x, g, w1, w2)


def _s5_gate(y, wglu_ref, bglu_ref):
    g = jax.nn.gelu(y).astype(BF16)
    z = _dot(g, wglu_ref[...]) + bglu_ref[...]
    return z[:, :D_MODEL] * jax.nn.sigmoid(z[:, D_MODEL:])


def _s5_prompt_kernel(xa_ref, xb_ref, gn_ref, bhat_ref, lam_ref, chat_ref, d_ref, wglu_ref, bglu_ref,
                      o_ref, fin_ref, st_ref, carry_ref, tb_ref, tb2_ref, y_ref):
    batch, tc, _ = xa_ref.shape
    assert 2 * batch == SUBLANES
    n_slabs = D_MODEL // LANES
    step = pl.program_id(0)
    live = step < pl.num_programs(0) - 1

    @pl.when(step == 0)
    def _():
        carry_ref[...] = jnp.zeros_like(carry_ref)
        y_ref[...] = jnp.zeros_like(y_ref)

    upd = _s5_gate(y_ref[...], wglu_ref, bglu_ref)
    for c in range(n_slabs):
        tb2_ref[c] = upd[:, LANES * c:LANES * (c + 1)]
    for b in range(batch):
        o_ref[b] = xb_ref[b] + jnp.concatenate(
            [tb2_ref[c, pl.ds(b, tc, stride=batch), :] for c in range(n_slabs)], axis=-1)

    for b in range(batch):
        ub_nat = _rms(xa_ref[b], gn_ref[...])
        for c in range(n_slabs):
            tb_ref[c, pl.ds(b, tc, stride=batch), :] = ub_nat[:, LANES * c:LANES * (c + 1)]
    u = jnp.concatenate([tb_ref[c] for c in range(n_slabs)], axis=-1)
    ub = u.astype(BF16)

    lo = lax.broadcasted_iota(jnp.int32, (SUBLANES, LANES), 0) < batch
    swap = lambda v: pltpu.roll(v, batch, 0)
    n_vregs = batch * tc // SUBLANES
    half_cols = STATE_BLOCK // LANES // 2
    ys = []
    for i in range(N_CH_BLOCKS):
        base = 2 * STATE_BLOCK * i
        st_ref[:, base:base + 2 * STATE_BLOCK] = _dot(ub[:, MXU_TILE * i:MXU_TILE * (i + 1)], bhat_ref[i])
        cols, lam_r, lam_i, sr, si, old = [], [], [], [], [], []
        for j in range(half_cols):
            ca, cb = base + LANES * j, base + LANES * (j + half_cols)
            cols.append((ca, cb))
            la, lb = STATE_BLOCK * i + LANES * j, STATE_BLOCK * i + LANES * (j + half_cols)
            bc = lambda r, c: jnp.broadcast_to(lam_ref[r:r + 1, c:c + LANES], (SUBLANES, LANES))
            lam_r.append(jnp.where(lo, bc(0, la), bc(0, lb)))
            lam_i.append(jnp.where(lo, bc(1, la), bc(1, lb)))
            prev = [carry_ref[:, c:c + LANES] for c in (ca, cb, ca + STATE_BLOCK, cb + STATE_BLOCK)]
            old.append(prev)
            sr.append(jnp.where(lo, swap(prev[0]), prev[1]))
            si.append(jnp.where(lo, swap(prev[2]), prev[3]))
        last = [None] * half_cols
        for m in range(n_vregs):
            rows = slice(SUBLANES * m, SUBLANES * (m + 1))
            for j in range(half_cols):
                ca, cb = cols[j]
                ar, br = st_ref[rows, ca:ca + LANES], st_ref[rows, cb:cb + LANES]
                ai, bi = (st_ref[rows, ca + STATE_BLOCK:ca + STATE_BLOCK + LANES],
                          st_ref[rows, cb + STATE_BLOCK:cb + STATE_BLOCK + LANES])
                d0r, d1r = jnp.where(lo, ar, swap(br)), jnp.where(lo, swap(ar), br)
                d0i, d1i = jnp.where(lo, ai, swap(bi)), jnp.where(lo, swap(ai), bi)
                s0r = lam_r[j] * sr[j] - lam_i[j] * si[j] + d0r
                s0i = lam_r[j] * si[j] + lam_i[j] * sr[j] + d0i
                s1r = lam_r[j] * s0r - lam_i[j] * s0i + d1r
                s1i = lam_r[j] * s0i + lam_i[j] * s0r + d1i
                sr[j], si[j] = s1r, s1i
                out = (jnp.where(lo, s0r, swap(s1r)), jnp.where(lo, swap(s0r), s1r),
                       jnp.where(lo, s0i, swap(s1i)), jnp.where(lo, swap(s0i), s1i))
                st_ref[rows, ca:ca + LANES] = out[0]
                st_ref[rows, cb:cb + LANES] = out[1]
                st_ref[rows, ca + STATE_BLOCK:ca + STATE_BLOCK + LANES] = out[2]
                st_ref[rows, cb + STATE_BLOCK:cb + STATE_BLOCK + LANES] = out[3]
                last[j] = out
        for j in range(half_cols):
            ca, cb = cols[j]
            for c, new, prev in zip((ca, cb, ca + STATE_BLOCK, cb + STATE_BLOCK), last[j], old[j]):
                carry_ref[:, c:c + LANES] = jnp.where(live, new, prev)
        ys.append(_dot(st_ref[:, base:base + 2 * STATE_BLOCK].astype(BF16), chat_ref[i]))
    u = jnp.concatenate([tb_ref[c] for c in range(n_slabs)], axis=-1)
    y_ref[...] = jnp.concatenate(ys, axis=-1) + d_ref[...] * u
    fin_ref[...] = carry_ref[...]


def _s5_prompt(x, layer, gn, bhat, lam, chat, d, wglu, bglu):
    batch, seq, _ = x.shape
    tc = S5_TIME_CHUNK
    n_chunks = seq // tc
    blk = (batch, tc, D_MODEL)
    this_chunk = pl.BlockSpec(blk, lambda s: (0, jnp.minimum(s, n_chunks - 1), 0))
    prev_chunk = pl.BlockSpec(blk, lambda s: (0, jnp.maximum(s - 1, 0), 0))
    slab = pltpu.VMEM((D_MODEL // LANES, batch * tc, LANES), F32)
    return pl.pallas_call(
        _s5_prompt_kernel,
        grid=(n_chunks + 1,),
        in_specs=[this_chunk, prev_chunk]
        + [_layer_spec(w.shape, layer) for w in (gn, bhat, lam, chat, d, wglu, bglu)],
        out_specs=[prev_chunk, pl.BlockSpec((SUBLANES, STATE_LANES), lambda s: (0, 0))],
        out_shape=[jax.ShapeDtypeStruct(x.shape, F32),
                   jax.ShapeDtypeStruct((SUBLANES, STATE_LANES), F32)],
        scratch_shapes=[pltpu.VMEM((batch * tc, STATE_LANES), F32),
                        pltpu.VMEM((SUBLANES, STATE_LANES), F32),
                        slab, slab,
                        pltpu.VMEM((batch * tc, D_MODEL), F32)],
        compiler_params=_params(("arbitrary",)),
        name="s5_prompt",
    )(x, x, gn, bhat, lam, chat, d, wglu, bglu)


def _s5_sample_kernel(x_ref, gn_ref, bhat_ref, lam_ref, chat_ref, d_ref, wglu_ref, bglu_ref, s0_ref,
                      o_ref, snew_ref):
    x = x_ref[...]
    u = _rms(x, gn_ref[...])
    ys = []
    for i in range(N_CH_BLOCKS):
        base = 2 * STATE_BLOCK * i
        bu = jnp.dot(u[:, MXU_TILE * i:MXU_TILE * (i + 1)], bhat_ref[i],
                     preferred_element_type=F32, precision=lax.Precision.HIGHEST)
        lr = lam_ref[0:1, STATE_BLOCK * i:STATE_BLOCK * (i + 1)]
        li = lam_ref[1:2, STATE_BLOCK * i:STATE_BLOCK * (i + 1)]
        s0r = s0_ref[:, base:base + STATE_BLOCK]
        s0i = s0_ref[:, base + STATE_BLOCK:base + 2 * STATE_BLOCK]
        sr = lr * s0r - li * s0i + bu[:, :STATE_BLOCK]
        si = lr * s0i + li * s0r + bu[:, STATE_BLOCK:]
        snew_ref[:, base:base + STATE_BLOCK] = sr
        snew_ref[:, base + STATE_BLOCK:base + 2 * STATE_BLOCK] = si
        sb = jnp.concatenate([sr, si], axis=-1).astype(BF16)
        ys.append(_dot(sb, chat_ref[i]))
    y = jnp.concatenate(ys, axis=-1)
    o_ref[...] = x + _s5_gate(y + d_ref[...] * u, wglu_ref, bglu_ref)


def _s5_sample(x, layer, gn, bhat32, lam, chat, d, wglu, bglu, s0):
    rows = x.shape[0]
    return pl.pallas_call(
        _s5_sample_kernel,
        grid=(1,),
        in_specs=[_const_spec(x.shape)] + [_layer_spec(w.shape, layer) for w in (gn, bhat32, lam, chat, d, wglu, bglu)]
        + [_const_spec(s0.shape)],
        out_specs=[pl.BlockSpec((rows, D_MODEL), lambda i: (0, 0)),
                   pl.BlockSpec((rows, STATE_LANES), lambda i: (0, 0))],
        out_shape=[jax.ShapeDtypeStruct((rows, D_MODEL), F32),
                   jax.ShapeDtypeStruct((rows, STATE_LANES), F32)],
        compiler_params=_params(("arbitrary",)),
        name="s5_sample",
    )(x, gn, bhat32, lam, chat, d, wglu, bglu, s0)


def _s5_weights(a_re, a_im, log_dt, b_re, b_im, c_re, c_im):
    n = a_re.shape[0]
    dt = jnp.exp(log_dt.astype(F32))
    lam = lax.complex(a_re.astype(F32), a_im.astype(F32))
    lam_bar = jnp.exp(lam * dt)
    b = lax.complex(b_re.astype(F32), b_im.astype(F32))
    b_bar = ((lam_bar - 1.0) / lam)[..., None] * b
    gpb = MXU_TILE // GROUP_SIZE
    hi = lax.Precision.HIGHEST

    def compact_in(w):
        return jnp.swapaxes(w.reshape(n, N_CH_BLOCKS, gpb, STATE_DIM, GROUP_SIZE), 3, 4).reshape(
            n, N_CH_BLOCKS, MXU_TILE, STATE_DIM)

    def compact_out(w):
        return jnp.swapaxes(w.reshape(n, N_CH_BLOCKS, gpb, GROUP_SIZE, STATE_DIM), 3, 4).reshape(
            n, N_CH_BLOCKS, STATE_BLOCK, GROUP_SIZE)

    lane = jnp.arange(2 * STATE_BLOCK)
    tile_in = (jnp.arange(2 * STATE_DIM)[:, None] == (lane // STATE_BLOCK) * STATE_DIM + lane % STATE_DIM)
    own_in = (jnp.arange(MXU_TILE)[:, None] // GROUP_SIZE == (lane % STATE_BLOCK) // STATE_DIM)
    bc = jnp.concatenate([compact_in(jnp.real(b_bar)), compact_in(jnp.imag(b_bar))], axis=-1)
    bhat = jnp.einsum("nirk,kl->nirl", bc, tile_in.astype(F32), precision=hi) * own_in.astype(F32)

    col = jnp.arange(MXU_TILE)
    tile_out = (jnp.arange(GROUP_SIZE)[:, None] == col % GROUP_SIZE)
    own_out = ((lane[:, None] % STATE_BLOCK) // STATE_DIM == col // GROUP_SIZE)
    cc = jnp.concatenate([compact_out(c_re.astype(F32)), -compact_out(c_im.astype(F32))], axis=2)
    chat = jnp.einsum("nirc,cl->nirl", cc, tile_out.astype(F32), precision=hi) * own_out.astype(F32)
    lam_rows = jnp.stack([jnp.real(lam_bar).reshape(n, -1), jnp.imag(lam_bar).reshape(n, -1)], axis=1)
    return bhat, chat.astype(BF16), lam_rows


def _state_to_lanes(s_re, s_im):
    n = s_re.shape[0]
    re = s_re.astype(F32).reshape(n, N_CH_BLOCKS, STATE_BLOCK)
    im = s_im.astype(F32).reshape(n, N_CH_BLOCKS, STATE_BLOCK)
    return jnp.concatenate([re, im], axis=-1).reshape(n, STATE_LANES)


def _lanes_to_state(s):
    n = s.shape[0]
    s4 = s.reshape(n, N_CH_BLOCKS, 2, STATE_BLOCK)
    return (s4[:, :, 0].reshape(n, N_GROUPS, STATE_DIM), s4[:, :, 1].reshape(n, N_GROUPS, STATE_DIM))


def _rope_partner(n):
    lane = jnp.arange(n)
    return jnp.where(lane % HEAD_DIM < HEAD_DIM // 2, lane + HEAD_DIM // 2, lane - HEAD_DIM // 2)


def _kv_kernel(x_ref, g_ref, wkv_ref, kg_ref, kgp_ref, ones_ref, cos_ref, sin_ref, k_ref, v_ref):
    h = _rms(x_ref[...], g_ref[...]).astype(BF16)
    kv = _dot(h, wkv_ref[...])
    n = k_ref.shape[-1]
    k, k_partner = kv[:, :n], kv[:, n:2 * n]
    scale = lax.rsqrt(_seg_mean_sq(k, ones_ref) + EPS)
    k_ref[...] = scale * (k * (kg_ref[...] * _tile_lanes(cos_ref[...], n))
                          + k_partner * (kgp_ref[...] * _tile_lanes(sin_ref[...], n)))
    v_ref[...] = kv[:, 2 * n:]


def _kv(x, g, wk, wv, kgain, ones, cos, sin, table_block):
    rows = x.shape[0]
    n = wk.shape[1]
    partner = _rope_partner(n)
    wkv = jnp.concatenate([wk, wk[:, partner], wv], axis=1).astype(BF16)
    tm = min(KV_ROWS, rows)
    return pl.pallas_call(
        _kv_kernel,
        grid=(rows // tm,),
        in_specs=[pl.BlockSpec((tm, D_MODEL), lambda i: (i, 0)),
                  _const_spec((1, D_MODEL)),
                  _const_spec(wkv.shape),
                  _const_spec((1, n)),
                  _const_spec((1, n)),
                  _const_spec(ones.shape),
                  pl.BlockSpec((tm, LANES), lambda i: (table_block(i), 0)),
                  pl.BlockSpec((tm, LANES), lambda i: (table_block(i), 0))],
        out_specs=[pl.BlockSpec((tm, n), lambda i: (i, 0)),
                   pl.BlockSpec((tm, n), lambda i: (i, 0))],
        out_shape=[jax.ShapeDtypeStruct((rows, n), F32),
                   jax.ShapeDtypeStruct((rows, n), F32)],
        compiler_params=_params(("parallel",)),
        name="kv_proj",
    )(x, g, wkv, kgain, kgain[:, partner], ones, cos, sin)


def _attn_prompt_kernel(sinks_ref, x_ref, g_ref, wq_ref, qg_ref, ones_ref, cos_ref, sin_ref,
                        kp_ref, kc_ref, vp_ref, vc_ref, wo_ref, o_ref):
    tile = pl.program_id(1)
    x = x_ref[...]
    n_blocks = x.shape[0] // WINDOW
    heads_per_kv = N_HEADS // N_KV_HEADS
    h = _rms(x, g_ref[...]).astype(BF16)
    q = _dot(h, wq_ref[...])
    q = q * lax.rsqrt(_seg_mean_sq(q, ones_ref) + EPS) * qg_ref[...]
    q = _rope(q, _tile_lanes(cos_ref[...], D_MODEL), _tile_lanes(sin_ref[...], D_MODEL))
    qb = (q * (HEAD_DIM ** -0.5)).astype(BF16)

    kall = jnp.concatenate([kp_ref[...], kc_ref[...]], axis=0).astype(BF16)
    vall = jnp.concatenate([vp_ref[...], vc_ref[...]], axis=0).astype(BF16)

    qi = lax.broadcasted_iota(jnp.int32, (WINDOW, WINDOW), 0)
    kj = lax.broadcasted_iota(jnp.int32, (WINDOW, WINDOW), 1)
    from_prev = kj > qi
    low_half = lax.broadcasted_iota(jnp.int32, (WINDOW, LANES), 1) < HEAD_DIM
    neg_inf = jnp.full((WINDOW, WINDOW), -jnp.inf, F32)

    block_outs = []
    for blk in range(n_blocks):
        has_prev = tile * n_blocks + blk > 0
        r0 = WINDOW * blk
        outs = []
        for pb in range(N_HEADS // 2):
            g = pb // (heads_per_kv // 2)
            kg = kall[r0:r0 + 2 * WINDOW, LANES * g:LANES * (g + 1)]
            vg = vall[r0:r0 + 2 * WINDOW, LANES * g:LANES * (g + 1)]
            qp = qb[r0:r0 + WINDOW, LANES * pb:LANES * (pb + 1)]
            halves = []
            for par in range(2):
                keep = low_half if par == 0 else jnp.logical_not(low_half)
                qm = jnp.where(keep, qp, jnp.zeros_like(qp))
                sink = sinks_ref[2 * pb + par]
                s = lax.dot_general(qm, kg, (((1,), (1,)), ((), ())), preferred_element_type=F32)
                s_prev = jnp.where(has_prev, s[:, :WINDOW], neg_inf)
                t = jnp.where(from_prev, s_prev, s[:, WINDOW:])
                m = jnp.maximum(jnp.max(t, axis=-1, keepdims=True), sink)
                p = jnp.exp(t - m)
                denom = jnp.sum(p, axis=-1, keepdims=True) + jnp.exp(sink - m)
                zero = jnp.zeros_like(p)
                pcat = jnp.concatenate([jnp.where(from_prev, p, zero), jnp.where(from_prev, zero, p)], axis=1)
                halves.append(_dot(pcat.astype(BF16), vg) / denom)
            outs.append(jnp.where(low_half, halves[0], halves[1]))
        block_outs.append(jnp.concatenate(outs, axis=-1))
    attn = jnp.concatenate(block_outs, axis=0).astype(BF16)
    o_ref[...] = x + _dot(attn, wo_ref[...])


def _attn_prompt(x, g, wq, qgain, ones, cos, sin, kd, vd, sinks, wo):
    bsz, seq, _ = x.shape
    nkv = kd.shape[-1]
    nb = ATTN_BLOCKS
    tq = nb * WINDOW
    x_spec = pl.BlockSpec((None, tq, D_MODEL), lambda b, i, *_: (b, i, 0))
    cur = pl.BlockSpec((None, tq, nkv), lambda b, i, *_: (b, i, 0))
    prev = pl.BlockSpec((None, WINDOW, nkv), lambda b, i, *_: (b, jnp.maximum(nb * i - 1, 0), 0))
    tab = pl.BlockSpec((tq, LANES), lambda b, i, *_: (i, 0))
    grid_spec = pltpu.PrefetchScalarGridSpec(
        num_scalar_prefetch=1,
        grid=(bsz, seq // tq),
        in_specs=[x_spec,
                  _const_spec((1, D_MODEL)),
                  _const_spec(wq.shape),
                  _const_spec((1, D_MODEL)),
                  _const_spec(ones.shape),
                  tab, tab, prev, cur, prev, cur,
                  _const_spec(wo.shape)],
        out_specs=x_spec)
    return pl.pallas_call(
        _attn_prompt_kernel,
        grid_spec=grid_spec,
        out_shape=jax.ShapeDtypeStruct(x.shape, F32),
        compiler_params=_params(("parallel", "arbitrary")),
        name="attn_prompt",
    )(sinks, x, g, wq, qgain, ones, cos, sin, kd, kd, vd, vd, wo)


def _own_head(shape):
    row_head = lax.broadcasted_iota(jnp.int32, shape, 0) & (N_HEADS - 1)
    lane_head = lax.broadcasted_iota(jnp.int32, shape, 1) // HEAD_DIM
    return row_head == lane_head


def _q_sample_kernel(x_ref, g_ref, wq_ref, qg_ref, ones_ref, cos_ref, sin_ref, rep_ref, place_ref, q_ref):
    h = _rms(x_ref[...], g_ref[...]).astype(BF16)
    q = _dot(h, wq_ref[...])
    q = q * lax.rsqrt(_seg_mean_sq(q, ones_ref) + EPS) * qg_ref[...]
    q = _rope(q, _tile_lanes(cos_ref[...], D_MODEL), _tile_lanes(sin_ref[...], D_MODEL))
    qb = (q * (HEAD_DIM ** -0.5)).astype(BF16)
    q_rep = _dot(rep_ref[...], qb)
    q_own = jnp.where(_own_head(q_rep.shape), q_rep, 0.0).astype(BF16)
    q_ref[...] = _dot(q_own, place_ref[...])


def _q_sample(x, g, wq, qgain, ones, cos, sin, rep, place):
    rows = x.shape[0]
    return pl.pallas_call(
        _q_sample_kernel,
        out_shape=jax.ShapeDtypeStruct((rows * N_HEADS, place.shape[1]), F32),
        compiler_params=pltpu.CompilerParams(vmem_limit_bytes=VMEM_LIMIT_BYTES),
        name="q_sample",
    )(x, g, wq, qgain, ones, cos, sin, rep, place)


def _cache_append_kernel(k_ref, v_ref, kn_ref, vn_ref, ko_ref, vo_ref):
    nb, lanes_kv, keys = k_ref.shape
    first = pl.program_id(0) * nb
    key = lax.broadcasted_iota(jnp.int32, (lanes_kv, keys), 1)
    seq = lax.broadcasted_iota(jnp.int32, (kn_ref.shape[1], keys), 0)

    def split3(x):
        hi = x.astype(BF16)
        r = x - hi.astype(F32)
        mid = r.astype(BF16)
        return hi, mid, (r - mid.astype(F32)).astype(BF16)

    parts = [(split3(kn_ref[...]), k_ref, ko_ref), (split3(vn_ref[...]), v_ref, vo_ref)]
    for b in range(nb):
        pick = (seq == first + b).astype(BF16)
        for (hi, mid, lo), old_ref, out_ref in parts:
            new_col = _dot(hi, pick) + _dot(mid, pick) + _dot(lo, pick)
            out_ref[b] = jnp.where(key == keys - 1, new_col, pltpu.roll(old_ref[b], keys - 1, 1))


def _cache_append(kt, vt, kn_t, vn_t):
    bsz, lanes_kv, keys = kt.shape
    nb = SUBLANES
    blk = pl.BlockSpec((nb, lanes_kv, keys), lambda i: (i, 0, 0))
    return pl.pallas_call(
        _cache_append_kernel,
        grid=(bsz // nb,),
        in_specs=[blk, blk, _const_spec(kn_t.shape), _const_spec(vn_t.shape)],
        out_specs=[blk, blk],
        out_shape=[jax.ShapeDtypeStruct(kt.shape, F32), jax.ShapeDtypeStruct(vt.shape, F32)],
        compiler_params=_params(("parallel",)),
        name="cache_append",
    )(kt, vt, kn_t, vn_t)


def _attn_sample_kernel(q_ref, k_ref, v_ref, sink_ref, o_ref):
    sink = sink_ref[...]
    for b in range(q_ref.shape[0]):
        s = _dot(q_ref[b].astype(BF16), k_ref[b].astype(BF16))
        m = jnp.maximum(jnp.max(s, axis=-1, keepdims=True), sink)
        p = jnp.exp(s - m)
        denom = jnp.sum(p, axis=-1, keepdims=True) + jnp.exp(sink - m)
        o = lax.dot_general(p.astype(BF16), v_ref[b].astype(BF16), (((1,), (1,)), ((), ())),
                            preferred_element_type=F32)
        o_ref[b] = o / denom


def _attn_sample(q3, kt, vt, sink_col):
    bsz, nh, nkv = q3.shape
    keys = kt.shape[2]
    bb = SUBLANES
    return pl.pallas_call(
        _attn_sample_kernel,
        grid=(bsz // bb,),
        in_specs=[pl.BlockSpec((bb, nh, nkv), lambda i: (i, 0, 0)),
                  pl.BlockSpec((bb, nkv, keys), lambda i: (i, 0, 0)),
                  pl.BlockSpec((bb, nkv, keys), lambda i: (i, 0, 0)),
                  _const_spec(sink_col.shape)],
        out_specs=pl.BlockSpec((bb, nh, nkv), lambda i: (i, 0, 0)),
        out_shape=jax.ShapeDtypeStruct((bsz, nh, nkv), F32),
        compiler_params=_params(("parallel",)),
        name="attn_sample",
    )(q3, kt, vt, sink_col)


def _proj_residual_kernel(x_ref, a_ref, place_t_ref, rep_t_ref, w_ref, o_ref):
    z = _dot(a_ref[...].astype(BF16), place_t_ref[...])
    z_own = jnp.where(_own_head(z.shape), z, 0.0).astype(BF16)
    attn = _dot(rep_t_ref[...], z_own).astype(BF16)
    o_ref[...] = x_ref[...] + _dot(attn, w_ref[...])


def _proj_residual(x, a, place_t, rep_t, w):
    return pl.pallas_call(
        _proj_residual_kernel,
        out_shape=jax.ShapeDtypeStruct(x.shape, F32),
        compiler_params=pltpu.CompilerParams(vmem_limit_bytes=VMEM_LIMIT_BYTES),
        name="o_proj_sample",
    )(x, a, place_t, rep_t, w)


def _rope_tables(pos):
    half = HEAD_DIM // 2
    inv = ROPE_THETA ** (-jnp.arange(half, dtype=F32) / half)
    ang = pos.astype(F32)[:, None] * inv[None, :]
    cos, sin = jnp.cos(ang), jnp.sin(ang)
    reps = LANES // HEAD_DIM
    return (jnp.tile(jnp.concatenate([cos, cos], axis=-1), (1, reps)),
            jnp.tile(jnp.concatenate([-sin, sin], axis=-1), (1, reps)))


def _block_ones(width, block):
    idx = jnp.arange(width) // block
    return (idx[:, None] == idx[None, :]).astype(BF16)


def _dup_heads(w):
    w3 = w.reshape(w.shape[0], N_KV_HEADS, 1, HEAD_DIM)
    return jnp.broadcast_to(w3, (w.shape[0], N_KV_HEADS, LANES // HEAD_DIM, HEAD_DIM)).reshape(w.shape[0], -1)


def _head_placement():
    lane = jnp.arange(N_HEADS * HEAD_DIM)
    dst = (lane // HEAD_DIM) // (N_HEADS // N_KV_HEADS) * HEAD_DIM + lane % HEAD_DIM
    return (dst[:, None] == jnp.arange(N_KV_HEADS * HEAD_DIM)[None, :]).astype(BF16)


def _row_replication(n):
    return (jnp.arange(n * N_HEADS)[:, None] // N_HEADS == jnp.arange(n)[None, :]).astype(BF16)


def kernel(x_prompt, x_sample, state_ssm_re, state_ssm_im, cache_k, cache_v, norm_mix, norm_mlp, ssm_a_re, ssm_a_im, ssm_log_dt, ssm_b_re, ssm_b_im, ssm_c_re, ssm_c_im, ssm_d, w_glu, b_glu, norm_kv, w_k, w_v, k_norm, w_q, q_norm, attn_sinks, w_o, w_mlp_in, w_mlp_out):
    bsz, seq, _ = x_prompt.shape
    dec = x_sample.shape[0]
    n_a = ssm_a_re.shape[0]
    depth = norm_mix.shape[0]
    past = 8192
    nkv = N_KV_HEADS * HEAD_DIM
    row = lambda v: v.astype(F32).reshape(1, -1)

    xp = x_prompt.astype(F32)
    xs = x_sample.reshape(dec, D_MODEL)
    rows3 = lambda v: v.astype(F32).reshape(v.shape[0], 1, -1)

    w1 = w_mlp_in.astype(BF16)
    w2 = w_mlp_out.astype(BF16)
    g_mlp = rows3(norm_mlp)
    mlp_p = lambda x, layer: _mlp(x.reshape(bsz * seq, D_MODEL), g_mlp, w1, w2, layer).reshape(bsz, seq, D_MODEL)

    bhat, chat, lam = _s5_weights(ssm_a_re, ssm_a_im, ssm_log_dt, ssm_b_re, ssm_b_im, ssm_c_re, ssm_c_im)
    bhat16 = bhat.astype(BF16)
    wglu = w_glu.astype(BF16)
    g_mix, d_skip, bg = rows3(norm_mix), rows3(ssm_d), rows3(b_glu)
    sp_re, sp_im, ss_re, ss_im = [], [], [], []
    for i in range(n_a):
        xp, fin = _s5_prompt(xp, i, g_mix, bhat16, lam, chat, d_skip, wglu, bg)
        re, im = _lanes_to_state(fin[bsz:])
        sp_re.append(re)
        sp_im.append(im)
        s0 = _state_to_lanes(state_ssm_re[i], state_ssm_im[i])
        xs, snew = _s5_sample(xs, i, g_mix, bhat, lam, chat, d_skip, wglu, bg, s0)
        re, im = _lanes_to_state(snew)
        ss_re.append(re)
        ss_im.append(im)
        xp = mlp_p(xp, i)
        xs = _mlp(xs, g_mlp, w1, w2, i)

    cos_p, sin_p = _rope_tables(jnp.arange(seq, dtype=jnp.int32))
    cos_s, sin_s = _rope_tables(jnp.full((dec,), past, dtype=jnp.int32))
    ones_head = _block_ones(MXU_TILE, HEAD_DIM)
    place = _head_placement()
    rep = _row_replication(dec)
    kgain = row(k_norm)
    tile_gain = lambda gvec, n: jnp.tile(gvec, (1, n // HEAD_DIM))

    kd, vd = _kv(xp.reshape(bsz * seq, D_MODEL), row(norm_kv), _dup_heads(w_k), _dup_heads(w_v),
                 tile_gain(kgain, 2 * nkv), ones_head, cos_p, sin_p, lambda i: i % (seq // KV_ROWS))
    ks_new, vs_new = _kv(xs, row(norm_kv), w_k, w_v, tile_gain(kgain, nkv), ones_head, cos_s, sin_s,
                         lambda i: i)
    kd = kd.reshape(bsz, seq, 2 * nkv)
    vd = vd.reshape(bsz, seq, 2 * nkv)
    undup = lambda t: t[:, -WINDOW:].reshape(bsz, WINDOW, N_KV_HEADS, LANES // HEAD_DIM, HEAD_DIM)[:, :, :, 0]
    new_k_p, new_v_p = undup(kd), undup(vd)
    key_minor = lambda c: jnp.transpose(c.astype(F32), (0, 2, 3, 1)).reshape(dec, nkv, c.shape[1])
    keys_s, vals_s = _cache_append(key_minor(cache_k), key_minor(cache_v), ks_new.T, vs_new.T)
    key_major = lambda c: jnp.transpose(c.reshape(dec, N_KV_HEADS, HEAD_DIM, c.shape[2]), (0, 3, 1, 2))
    new_k_s, new_v_s = key_major(keys_s), key_major(vals_s)

    for j in range(depth - n_a):
        layer = n_a + j
        gn = row(norm_mix[layer])
        qgain = row(q_norm[j])
        wq = w_q[j].astype(BF16)
        wo = w_o[j].astype(BF16)
        xp = _attn_prompt(xp, gn, wq, tile_gain(qgain, D_MODEL), ones_head, cos_p, sin_p, kd, vd,
                          attn_sinks[j].astype(F32), wo)
        q_pad = _q_sample(xs, gn, wq, tile_gain(qgain, D_MODEL), ones_head, cos_s, sin_s, rep, place)
        o_pad = _attn_sample(q_pad.reshape(dec, N_HEADS, nkv), keys_s, vals_s,
                             attn_sinks[j].astype(F32).reshape(N_HEADS, 1))
        xs = _proj_residual(xs, o_pad.reshape(dec * N_HEADS, nkv), place.T, rep.T, wo)
        xp = mlp_p(xp, layer)
        xs = _mlp(xs, g_mlp, w1, w2, layer)

    return (xp, xs.reshape(dec, 1, D_MODEL),
            jnp.stack(sp_re), jnp.stack(sp_im), new_k_p, new_v_p,
            jnp.stack(ss_re), jnp.stack(ss_im), new_k_s, new_v_s)
```

```python
import functools
import math

import jax
import jax.numpy as jnp
from jax import lax
from jax.experimental import pallas as pl
from jax.experimental.pallas import tpu as pltpu

F32 = jnp.float32
BF16 = jnp.bfloat16

D_MODEL = 1024
N_GROUPS = 64
GROUP_SIZE = 16
STATE_DIM = 64
HEAD_DIM = 64
N_HEADS = 16
N_KV_HEADS = 4
WINDOW = 128
ROPE_THETA = 10000.0
D_FF = 4 * D_MODEL
EPS = 1e-6

SUBLANES = 8
LANES = 128
MXU_TILE = 256
VMEM_LIMIT_BYTES = 56 * 1024 * 1024

N_CH_BLOCKS = D_MODEL // MXU_TILE
STATE_BLOCK = (MXU_TILE // GROUP_SIZE) * STATE_DIM
STATE_LANES = 2 * STATE_BLOCK * N_CH_BLOCKS

S5_TIME_CHUNK = 128
MLP_ROWS = 512
FF_CHUNK = 1024
KV_ROWS = 512
ATTN_BLOCKS = 2


def _const_spec(shape):
    zeros = (0,) * len(shape)
    return pl.BlockSpec(shape, lambda *_: zeros, pipeline_mode=pl.Buffered(1))


def _layer_spec(shape, layer):
    idx = (layer,) + (0,) * (len(shape) - 1)
    return pl.BlockSpec((None,) + tuple(shape[1:]), lambda *_: idx, pipeline_mode=pl.Buffered(1))


def _params(sem):
    return pltpu.CompilerParams(dimension_semantics=sem, vmem_limit_bytes=VMEM_LIMIT_BYTES)


def _rms(x, g):
    return x * lax.rsqrt(jnp.mean(x * x, axis=-1, keepdims=True) + EPS) * g


def _dot(a, b):
    return jnp.dot(a, b, preferred_element_type=F32)


def _seg_mean_sq(x, ones_ref):
    sq = x * x
    hi = sq.astype(BF16)
    lo = (sq - hi.astype(F32)).astype(BF16)
    ones = ones_ref[...]
    outs = []
    for t in range(x.shape[-1] // MXU_TILE):
        sl = slice(MXU_TILE * t, MXU_TILE * (t + 1))
        outs.append(_dot(hi[:, sl], ones) + _dot(lo[:, sl], ones))
    return jnp.concatenate(outs, axis=-1) * (1.0 / HEAD_DIM)


def _tile_lanes(t, n):
    return jnp.concatenate([t] * (n // t.shape[-1]), axis=-1)


def _rope(x, cos, sin_signed):
    n = x.shape[-1]
    lane = lax.broadcasted_iota(jnp.int32, x.shape, 1)
    first = (lane & (HEAD_DIM - 1)) < (HEAD_DIM // 2)
    partner = jnp.where(first, pltpu.roll(x, n - HEAD_DIM // 2, 1), pltpu.roll(x, HEAD_DIM // 2, 1))
    return x * cos + partner * sin_signed


def _mlp_kernel(x_ref, g_ref, w1_ref, w2_ref, o_ref):
    x = x_ref[...]
    h = _rms(x, g_ref[...]).astype(BF16)
    acc = x
    for c in range(D_FF // FF_CHUNK):
        sl = slice(FF_CHUNK * c, FF_CHUNK * (c + 1))
        a = jnp.square(jnp.maximum(_dot(h, w1_ref[:, sl]), 0.0)).astype(BF16)
        acc = acc + _dot(a, w2_ref[sl, :])
    o_ref[...] = acc


def _mlp(x, g, w1, w2, layer):
    rows = x.shape[0]
    tm = min(MLP_ROWS, rows)
    return pl.pallas_call(
        _mlp_kernel,
        grid=(rows // tm,),
        in_specs=[pl.BlockSpec((tm, D_MODEL), lambda i: (i, 0)),
                  _layer_spec(g.shape, layer),
                  _layer_spec(w1.shape, layer),
                  _layer_spec(w2.shape, layer)],
        out_specs=pl.BlockSpec((tm, D_MODEL), lambda i: (i, 0)),
        out_shape=jax.ShapeDtypeStruct((rows, D_MODEL), F32),
        compiler_params=_params(("parallel",)),
        name="mlp",
    )(x, g, w1, w2)


def _s5_gate(y, wglu_ref, bglu_ref):
    g = jax.nn.gelu(y).astype(BF16)
    z = _dot(g, wglu_ref[...]) + bglu_ref[...]
    return z[:, :D_MODEL] * jax.nn.sigmoid(z[:, D_MODEL:])


def _s5_prompt_kernel(xa_ref, xb_ref, gn_ref, bhat_ref, lam_ref, chat_ref, d_ref, wglu_ref, bglu_ref,
                      o_ref, fin_ref, st_ref, carry_ref, tb_ref, tb2_ref, y_ref):
    batch, tc, _ = xa_ref.shape
    assert 2 * batch == SUBLANES
    n_slabs = D_MODEL // LANES
    step = pl.program_id(0)
    live = step < pl.num_programs(0) - 1

    @pl.when(step == 0)
    def _():
        carry_ref[...] = jnp.zeros_like(carry_ref)
        y_ref[...] = jnp.zeros_like(y_ref)

    upd = _s5_gate(y_ref[...], wglu_ref, bglu_ref)
    for c in range(n_slabs):
        tb2_ref[c] = upd[:, LANES * c:LANES * (c + 1)]
    for b in range(batch):
        o_ref[b] = xb_ref[b] + jnp.concatenate(
            [tb2_ref[c, pl.ds(b, tc, stride=batch), :] for c in range(n_slabs)], axis=-1)

    for b in range(batch):
        ub_nat = _rms(xa_ref[b], gn_ref[...])
        for c in range(n_slabs):
            tb_ref[c, pl.ds(b, tc, stride=batch), :] = ub_nat[:, LANES * c:LANES * (c + 1)]
    u = jnp.concatenate([tb_ref[c] for c in range(n_slabs)], axis=-1)
    ub = u.astype(BF16)

    lo = lax.broadcasted_iota(jnp.int32, (SUBLANES, LANES), 0) < batch
    swap = lambda v: pltpu.roll(v, batch, 0)
    n_vregs = batch * tc // SUBLANES
    half_cols = STATE_BLOCK // LANES // 2
    ys = []
    for i in range(N_CH_BLOCKS):
        base = 2 * STATE_BLOCK * i
        st_ref[:, base:base + 2 * STATE_BLOCK] = _dot(ub[:, MXU_TILE * i:MXU_TILE * (i + 1)], bhat_ref[i])
        cols, lam_r, lam_i, sr, si, old = [], [], [], [], [], []
        for j in range(half_cols):
            ca, cb = base + LANES * j, base + LANES * (j + half_cols)
            cols.append((ca, cb))
            la, lb = STATE_BLOCK * i + LANES * j, STATE_BLOCK * i + LANES * (j + half_cols)
            bc = lambda r, c: jnp.broadcast_to(lam_ref[r:r + 1, c:c + LANES], (SUBLANES, LANES))
            lam_r.append(jnp.where(lo, bc(0, la), bc(0, lb)))
            lam_i.append(jnp.where(lo, bc(1, la), bc(1, lb)))
            prev = [carry_ref[:, c:c + LANES] for c in (ca, cb, ca + STATE_BLOCK, cb + STATE_BLOCK)]
            old.append(prev)
            sr.append(jnp.where(lo, swap(prev[0]), prev[1]))
            si.append(jnp.where(lo, swap(prev[2]), prev[3]))
        last = [None] * half_cols
        for m in range(n_vregs):
            rows = slice(SUBLANES * m, SUBLANES * (m + 1))
            for j in range(half_cols):
                ca, cb = cols[j]
                ar, br = st_ref[rows, ca:ca + LANES], st_ref[rows, cb:cb + LANES]
                ai, bi = (st_ref[rows, ca + STATE_BLOCK:ca + STATE_BLOCK + LANES],
                          st_ref[rows, cb + STATE_BLOCK:cb + STATE_BLOCK + LANES])
                d0r, d1r = jnp.where(lo, ar, swap(br)), jnp.where(lo, swap(ar), br)
                d0i, d1i = jnp.where(lo, ai, swap(bi)), jnp.where(lo, swap(ai), bi)
                s0r = lam_r[j] * sr[j] - lam_i[j] * si[j] + d0r
                s0i = lam_r[j] * si[j] + lam_i[j] * sr[j] + d0i
                s1r = lam_r[j] * s0r - lam_i[j] * s0i + d1r
                s1i = lam_r[j] * s0i + lam_i[j] * s0r + d1i
                sr[j], si[j] = s1r, s1i
                out = (jnp.where(lo, s0r, swap(s1r)), jnp.where(lo, swap(s0r), s1r),
                       jnp.where(lo, s0i, swap(s1i)), jnp.where(lo, swap(s0i), s1i))
                st_ref[rows, ca:ca + LANES] = out[0]
                st_ref[rows, cb:cb + LANES] = out[1]
                st_ref[rows, ca + STATE_BLOCK:ca + STATE_BLOCK + LANES] = out[2]
                st_ref[rows, cb + STATE_BLOCK:cb + STATE_BLOCK + LANES] = out[3]
                last[j] = out
        for j in range(half_cols):
            ca, cb = cols[j]
            for c, new, prev in zip((ca, cb, ca + STATE_BLOCK, cb + STATE_BLOCK), last[j], old[j]):
                carry_ref[:, c:c + LANES] = jnp.where(live, new, prev)
        ys.append(_dot(st_ref[:, base:base + 2 * STATE_BLOCK].astype(BF16), chat_ref[i]))
    u = jnp.concatenate([tb_ref[c] for c in range(n_slabs)], axis=-1)
    y_ref[...] = jnp.concatenate(ys, axis=-1) + d_ref[...] * u
    fin_ref[...] = carry_ref[...]


def _s5_prompt(x, layer, gn, bhat, lam, chat, d, wglu, bglu):
    batch, seq, _ = x.shape
    tc = S5_TIME_CHUNK
    n_chunks = seq // tc
    blk = (batch, tc, D_MODEL)
    this_chunk = pl.BlockSpec(blk, lambda s: (0, jnp.minimum(s, n_chunks - 1), 0))
    prev_chunk = pl.BlockSpec(blk, lambda s: (0, jnp.maximum(s - 1, 0), 0))
    slab = pltpu.VMEM((D_MODEL // LANES, batch * tc, LANES), F32)
    return pl.pallas_call(
        _s5_prompt_kernel,
        grid=(n_chunks + 1,),
        in_specs=[this_chunk, prev_chunk]
        + [_layer_spec(w.shape, layer) for w in (gn, bhat, lam, chat, d, wglu, bglu)],
        out_specs=[prev_chunk, pl.BlockSpec((SUBLANES, STATE_LANES), lambda s: (0, 0))],
        out_shape=[jax.ShapeDtypeStruct(x.shape, F32),
                   jax.ShapeDtypeStruct((SUBLANES, STATE_LANES), F32)],
        scratch_shapes=[pltpu.VMEM((batch * tc, STATE_LANES), F32),
                        pltpu.VMEM((SUBLANES, STATE_LANES), F32),
                        slab, slab,
                        pltpu.VMEM((batch * tc, D_MODEL), F32)],
        compiler_params=_params(("arbitrary",)),
        name="s5_prompt",
    )(x, x, gn, bhat, lam, chat, d, wglu, bglu)


def _s5_sample_kernel(x_ref, gn_ref, bhat_ref, lam_ref, chat_ref, d_ref, wglu_ref, bglu_ref, s0_ref,
                      o_ref, snew_ref):
    x = x_ref[...]
    u = _rms(x, gn_ref[...])
    ys = []
    for i in range(N_CH_BLOCKS):
        base = 2 * STATE_BLOCK * i
        bu = jnp.dot(u[:, MXU_TILE * i:MXU_TILE * (i + 1)], bhat_ref[i],
                     preferred_element_type=F32, precision=lax.Precision.HIGHEST)
        lr = lam_ref[0:1, STATE_BLOCK * i:STATE_BLOCK * (i + 1)]
        li = lam_ref[1:2, STATE_BLOCK * i:STATE_BLOCK * (i + 1)]
        s0r = s0_ref[:, base:base + STATE_BLOCK]
        s0i = s0_ref[:, base + STATE_BLOCK:base + 2 * STATE_BLOCK]
        sr = lr * s0r - li * s0i + bu[:, :STATE_BLOCK]
        si = lr * s0i + li * s0r + bu[:, STATE_BLOCK:]
        snew_ref[:, base:base + STATE_BLOCK] = sr
        snew_ref[:, base + STATE_BLOCK:base + 2 * STATE_BLOCK] = si
        sb = jnp.concatenate([sr, si], axis=-1).astype(BF16)
        ys.append(_dot(sb, chat_ref[i]))
    y = jnp.concatenate(ys, axis=-1)
    o_ref[...] = x + _s5_gate(y + d_ref[...] * u, wglu_ref, bglu_ref)


def _s5_sample(x, layer, gn, bhat32, lam, chat, d, wglu, bglu, s0):
    rows = x.shape[0]
    return pl.pallas_call(
        _s5_sample_kernel,
        grid=(1,),
        in_specs=[_const_spec(x.shape)] + [_layer_spec(w.shape, layer) for w in (gn, bhat32, lam, chat, d, wglu, bglu)]
        + [_const_spec(s0.shape)],
        out_specs=[pl.BlockSpec((rows, D_MODEL), lambda i: (0, 0)),
                   pl.BlockSpec((rows, STATE_LANES), lambda i: (0, 0))],
        out_shape=[jax.ShapeDtypeStruct((rows, D_MODEL), F32),
                   jax.ShapeDtypeStruct((rows, STATE_LANES), F32)],
        compiler_params=_params(("arbitrary",)),
        name="s5_sample",
    )(x, gn, bhat32, lam, chat, d, wglu, bglu, s0)


def _s5_weights(a_re, a_im, log_dt, b_re, b_im, c_re, c_im):
    n = a_re.shape[0]
    dt = jnp.exp(log_dt.astype(F32))
    lam = lax.complex(a_re.astype(F32), a_im.astype(F32))
    lam_bar = jnp.exp(lam * dt)
    b = lax.complex(b_re.astype(F32), b_im.astype(F32))
    b_bar = ((lam_bar - 1.0) / lam)[..., None] * b
    gpb = MXU_TILE // GROUP_SIZE
    hi = lax.Precision.HIGHEST

    def compact_in(w):
        return jnp.swapaxes(w.reshape(n, N_CH_BLOCKS, gpb, STATE_DIM, GROUP_SIZE), 3, 4).reshape(
            n, N_CH_BLOCKS, MXU_TILE, STATE_DIM)

    def compact_out(w):
        return jnp.swapaxes(w.reshape(n, N_CH_BLOCKS, gpb, GROUP_SIZE, STATE_DIM), 3, 4).reshape(
            n, N_CH_BLOCKS, STATE_BLOCK, GROUP_SIZE)

    lane = jnp.arange(2 * STATE_BLOCK)
    tile_in = (jnp.arange(2 * STATE_DIM)[:, None] == (lane // STATE_BLOCK) * STATE_DIM + lane % STATE_DIM)
    own_in = (jnp.arange(MXU_TILE)[:, None] // GROUP_SIZE == (lane % STATE_BLOCK) // STATE_DIM)
    bc = jnp.concatenate([compact_in(jnp.real(b_bar)), compact_in(jnp.imag(b_bar))], axis=-1)
    bhat = jnp.einsum("nirk,kl->nirl", bc, tile_in.astype(F32), precision=hi) * own_in.astype(F32)

    col = jnp.arange(MXU_TILE)
    tile_out = (jnp.arange(GROUP_SIZE)[:, None] == col % GROUP_SIZE)
    own_out = ((lane[:, None] % STATE_BLOCK) // STATE_DIM == col // GROUP_SIZE)
    cc = jnp.concatenate([compact_out(c_re.astype(F32)), -compact_out(c_im.astype(F32))], axis=2)
    chat = jnp.einsum("nirc,cl->nirl", cc, tile_out.astype(F32), precision=hi) * own_out.astype(F32)
    lam_rows = jnp.stack([jnp.real(lam_bar).reshape(n, -1), jnp.imag(lam_bar).reshape(n, -1)], axis=1)
    return bhat, chat.astype(BF16), lam_rows


def _state_to_lanes(s_re, s_im):
    n = s_re.shape[0]
    re = s_re.astype(F32).reshape(n, N_CH_BLOCKS, STATE_BLOCK)
    im = s_im.astype(F32).reshape(n, N_CH_BLOCKS, STATE_BLOCK)
    return jnp.concatenate([re, im], axis=-1).reshape(n, STATE_LANES)


def _lanes_to_state(s):
    n = s.shape[0]
    s4 = s.reshape(n, N_CH_BLOCKS, 2, STATE_BLOCK)
    return (s4[:, :, 0].reshape(n, N_GROUPS, STATE_DIM), s4[:, :, 1].reshape(n, N_GROUPS, STATE_DIM))


def _rope_partner(n):
    lane = jnp.arange(n)
    return jnp.where(lane % HEAD_DIM < HEAD_DIM // 2, lane + HEAD_DIM // 2, lane - HEAD_DIM // 2)


def _kv_kernel(x_ref, g_ref, wkv_ref, kg_ref, kgp_ref, ones_ref, cos_ref, sin_ref, *rest):
    if len(rest) == 4:
        wvt_ref, k_ref, v_ref, vt_ref = rest
    else:
        (k_ref, v_ref), wvt_ref, vt_ref = rest, None, None
    h = _rms(x_ref[...], g_ref[...]).astype(BF16)
    kv = _dot(h, wkv_ref[...])
    n = k_ref.shape[-1]
    k, k_partner = kv[:, :n], kv[:, n:2 * n]
    scale = lax.rsqrt(_seg_mean_sq(k, ones_ref) + EPS)
    k_ref[...] = scale * (k * (kg_ref[...] * _tile_lanes(cos_ref[...], n))
                          + k_partner * (kgp_ref[...] * _tile_lanes(sin_ref[...], n)))
    v_ref[...] = kv[:, 2 * n:]
    if vt_ref is not None:
        vt_ref[...] = lax.dot_general(wvt_ref[...], h, (((1,), (1,)), ((), ())), preferred_element_type=F32)


def _kv(x, g, wk, wv, kgain, ones, cos, sin, table_block, seq=None):
    rows = x.shape[0]
    n, nv = wk.shape[1], wv.shape[1]
    partner = _rope_partner(n)
    wkv = jnp.concatenate([wk, wk[:, partner], wv], axis=1).astype(BF16)
    tm = min(KV_ROWS, rows)
    in_specs = [pl.BlockSpec((tm, D_MODEL), lambda i: (i, 0)),
                _const_spec((1, D_MODEL)),
                _const_spec(wkv.shape),
                _const_spec((1, n)),
                _const_spec((1, n)),
                _const_spec(ones.shape),
                pl.BlockSpec((tm, LANES), lambda i: (table_block(i), 0)),
                pl.BlockSpec((tm, LANES), lambda i: (table_block(i), 0))]
    out_specs = [pl.BlockSpec((tm, n), lambda i: (i, 0)),
                 pl.BlockSpec((tm, nv), lambda i: (i, 0))]
    out_shape = [jax.ShapeDtypeStruct((rows, n), F32),
                 jax.ShapeDtypeStruct((rows, nv), F32)]
    args = [x, g, wkv, kgain, kgain[:, partner], ones, cos, sin]
    if seq is not None:
        per_seq = seq // tm
        in_specs.append(_const_spec((nv, D_MODEL)))
        args.append(wv.T.astype(BF16))
        out_specs.append(pl.BlockSpec((None, nv, tm), lambda i: (i // per_seq, 0, i % per_seq)))
        out_shape.append(jax.ShapeDtypeStruct((rows // seq, nv, seq), F32))
    return pl.pallas_call(
        _kv_kernel,
        grid=(rows // tm,),
        in_specs=in_specs,
        out_specs=out_specs,
        out_shape=out_shape,
        compiler_params=_params(("parallel",)),
        name="kv_proj",
    )(*args)


def _attn_prompt_kernel(sinks_ref, x_ref, g_ref, wq_ref, qg_ref, ones_ref, cos_ref, sin_ref,
                        kp_ref, kc_ref, vp_ref, vc_ref, wo_ref, o_ref):
    tile = pl.program_id(1)
    x = x_ref[...]
    n_blocks = x.shape[0] // WINDOW
    heads_per_kv = N_HEADS // N_KV_HEADS
    cols = heads_per_kv * WINDOW
    h = _rms(x, g_ref[...]).astype(BF16)
    q = _dot(h, wq_ref[...])
    q = q * lax.rsqrt(_seg_mean_sq(q, ones_ref) + EPS) * qg_ref[...]
    q = _rope(q, _tile_lanes(cos_ref[...], D_MODEL), _tile_lanes(sin_ref[...], D_MODEL))
    qb = (q * (HEAD_DIM ** -0.5)).astype(BF16)

    kall = jnp.concatenate([kp_ref[...], kc_ref[...]], axis=0).astype(BF16)
    vall = jnp.concatenate([vp_ref[...], vc_ref[...]], axis=1).astype(BF16)

    kj = lax.broadcasted_iota(jnp.int32, (WINDOW, cols), 0)
    col = lax.broadcasted_iota(jnp.int32, (WINDOW, cols), 1)
    from_prev = kj > (col & (WINDOW - 1))
    col_head = lax.broadcasted_iota(jnp.int32, (1, cols), 1) // WINDOW
    low_half = lax.broadcasted_iota(jnp.int32, (WINDOW, LANES), 1) < HEAD_DIM
    neg_inf = jnp.full((WINDOW, cols), -jnp.inf, F32)

    pairs = [(blk, g) for blk in range(n_blocks) for g in range(N_KV_HEADS)]
    scores, sinks = [], []
    for blk, g in pairs:
        r0 = WINDOW * blk
        kg = kall[r0:r0 + 2 * WINDOW, LANES * g:LANES * (g + 1)]
        qs = []
        sink = jnp.zeros((1, cols), F32)
        for hh in range(heads_per_kv):
            head = heads_per_kv * g + hh
            qp = qb[r0:r0 + WINDOW, LANES * (head // 2):LANES * (head // 2 + 1)]
            keep = low_half if head % 2 == 0 else jnp.logical_not(low_half)
            qs.append(jnp.where(keep, qp, jnp.zeros_like(qp)))
            sink = jnp.where(col_head == hh, sinks_ref[head], sink)
        sinks.append(sink)
        scores.append(lax.dot_general(kg, jnp.concatenate(qs, axis=0), (((1,), (1,)), ((), ())),
                                      preferred_element_type=F32))
    probs, denoms = [], []
    for (blk, g), s, sink in zip(pairs, scores, sinks):
        has_prev = tile * n_blocks + blk > 0
        t = jnp.where(from_prev, jnp.where(has_prev, s[:WINDOW], neg_inf), s[WINDOW:])
        m = jnp.maximum(jnp.max(t, axis=0, keepdims=True), sink)
        p = jnp.exp(t - m)
        denoms.append(jnp.sum(p, axis=0, keepdims=True) + jnp.exp(sink - m))
        zero = jnp.zeros_like(p)
        probs.append(jnp.concatenate([jnp.where(from_prev, p, zero), jnp.where(from_prev, zero, p)],
                                     axis=0).astype(BF16))
    head_outs = [[] for _ in range(n_blocks)]
    for (blk, g), pcat, denom in zip(pairs, probs, denoms):
        r0 = WINDOW * blk
        vg = vall[HEAD_DIM * g:HEAD_DIM * (g + 1), r0:r0 + 2 * WINDOW]
        o = _dot(vg, pcat) / denom
        head_outs[blk] += [o[:, WINDOW * hh:WINDOW * (hh + 1)] for hh in range(heads_per_kv)]
    block_outs = [jnp.concatenate(outs, axis=0).T for outs in head_outs]
    attn = jnp.concatenate(block_outs, axis=0).astype(BF16)
    o_ref[...] = x + _dot(attn, wo_ref[...])


def _attn_prompt(x, g, wq, qgain, ones, cos, sin, kd, vt, sinks, wo):
    bsz, seq, _ = x.shape
    nk, nv = kd.shape[-1], vt.shape[1]
    nb = ATTN_BLOCKS
    tq = nb * WINDOW
    x_spec = pl.BlockSpec((None, tq, D_MODEL), lambda b, i, *_: (b, i, 0))
    k_cur = pl.BlockSpec((None, tq, nk), lambda b, i, *_: (b, i, 0))
    k_prev = pl.BlockSpec((None, WINDOW, nk), lambda b, i, *_: (b, jnp.maximum(nb * i - 1, 0), 0))
    v_cur = pl.BlockSpec((None, nv, tq), lambda b, i, *_: (b, 0, i))
    v_prev = pl.BlockSpec((None, nv, WINDOW), lambda b, i, *_: (b, 0, jnp.maximum(nb * i - 1, 0)))
    tab = pl.BlockSpec((tq, LANES), lambda b, i, *_: (i, 0))
    grid_spec = pltpu.PrefetchScalarGridSpec(
        num_scalar_prefetch=1,
        grid=(bsz, seq // tq),
        in_specs=[x_spec,
                  _const_spec((1, D_MODEL)),
                  _const_spec(wq.shape),
                  _const_spec((1, D_MODEL)),
                  _const_spec(ones.shape),
                  tab, tab, k_prev, k_cur, v_prev, v_cur,
                  _const_spec(wo.shape)],
        out_specs=x_spec)
    return pl.pallas_call(
        _attn_prompt_kernel,
        grid_spec=grid_spec,
        out_shape=jax.ShapeDtypeStruct(x.shape, F32),
        compiler_params=_params(("parallel", "arbitrary")),
        name="attn_prompt",
    )(sinks, x, g, wq, qgain, ones, cos, sin, kd, kd, vt, vt, wo)


def _own_head(shape):
    row_head = lax.broadcasted_iota(jnp.int32, shape, 0) & (N_HEADS - 1)
    lane_head = lax.broadcasted_iota(jnp.int32, shape, 1) // HEAD_DIM
    return row_head == lane_head


def _q_sample_kernel(x_ref, g_ref, wq_ref, qg_ref, ones_ref, cos_ref, sin_ref, rep_ref, place_ref, q_ref):
    h = _rms(x_ref[...], g_ref[...]).astype(BF16)
    q = _dot(h, wq_ref[...])
    q = q * lax.rsqrt(_seg_mean_sq(q, ones_ref) + EPS) * qg_ref[...]
    q = _rope(q, _tile_lanes(cos_ref[...], D_MODEL), _tile_lanes(sin_ref[...], D_MODEL))
    qb = (q * (HEAD_DIM ** -0.5)).astype(BF16)
    q_rep = _dot(rep_ref[...], qb)
    q_own = jnp.where(_own_head(q_rep.shape), q_rep, 0.0).astype(BF16)
    q_ref[...] = _dot(q_own, place_ref[...])


def _q_sample(x, g, wq, qgain, ones, cos, sin, rep, place):
    rows = x.shape[0]
    return pl.pallas_call(
        _q_sample_kernel,
        out_shape=jax.ShapeDtypeStruct((rows * N_HEADS, place.shape[1]), F32),
        compiler_params=pltpu.CompilerParams(vmem_limit_bytes=VMEM_LIMIT_BYTES),
        name="q_sample",
    )(x, g, wq, qgain, ones, cos, sin, rep, place)


def _cache_append_kernel(k_ref, v_ref, kn_ref, vn_ref, ko_ref, vo_ref):
    nb, lanes_kv, keys = k_ref.shape
    first = pl.program_id(0) * nb
    key = lax.broadcasted_iota(jnp.int32, (lanes_kv, keys), 1)
    seq = lax.broadcasted_iota(jnp.int32, (kn_ref.shape[1], keys), 0)

    def split3(x):
        hi = x.astype(BF16)
        r = x - hi.astype(F32)
        mid = r.astype(BF16)
        return hi, mid, (r - mid.astype(F32)).astype(BF16)

    parts = [(split3(kn_ref[...]), k_ref, ko_ref), (split3(vn_ref[...]), v_ref, vo_ref)]
    for b in range(nb):
        pick = (seq == first + b).astype(BF16)
        for (hi, mid, lo), old_ref, out_ref in parts:
            new_col = _dot(hi, pick) + _dot(mid, pick) + _dot(lo, pick)
            out_ref[b] = jnp.where(key == keys - 1, new_col, pltpu.roll(old_ref[b], keys - 1, 1))


def _cache_append(kt, vt, kn_t, vn_t):
    bsz, lanes_kv, keys = kt.shape
    nb = SUBLANES
    blk = pl.BlockSpec((nb, lanes_kv, keys), lambda i: (i, 0, 0))
    return pl.pallas_call(
        _cache_append_kernel,
        grid=(bsz // nb,),
        in_specs=[blk, blk, _const_spec(kn_t.shape), _const_spec(vn_t.shape)],
        out_specs=[blk, blk],
        out_shape=[jax.ShapeDtypeStruct(kt.shape, F32), jax.ShapeDtypeStruct(vt.shape, F32)],
        compiler_params=_params(("parallel",)),
        name="cache_append",
    )(kt, vt, kn_t, vn_t)


def _attn_sample_kernel(q_ref, k_ref, v_ref, sink_ref, o_ref):
    sink = sink_ref[...]
    for b in range(q_ref.shape[0]):
        s = _dot(q_ref[b].astype(BF16), k_ref[b].astype(BF16))
        m = jnp.maximum(jnp.max(s, axis=-1, keepdims=True), sink)
        p = jnp.exp(s - m)
        denom = jnp.sum(p, axis=-1, keepdims=True) + jnp.exp(sink - m)
        o = lax.dot_general(p.astype(BF16), v_ref[b].astype(BF16), (((1,), (1,)), ((), ())),
                            preferred_element_type=F32)
        o_ref[b] = o / denom


def _attn_sample(q3, kt, vt, sink_col):
    bsz, nh, nkv = q3.shape
    keys = kt.shape[2]
    bb = SUBLANES
    return pl.pallas_call(
        _attn_sample_kernel,
        grid=(bsz // bb,),
        in_specs=[pl.BlockSpec((bb, nh, nkv), lambda i: (i, 0, 0)),
                  pl.BlockSpec((bb, nkv, keys), lambda i: (i, 0, 0)),
                  pl.BlockSpec((bb, nkv, keys), lambda i: (i, 0, 0)),
                  _const_spec(sink_col.shape)],
        out_specs=pl.BlockSpec((bb, nh, nkv), lambda i: (i, 0, 0)),
        out_shape=jax.ShapeDtypeStruct((bsz, nh, nkv), F32),
        compiler_params=_params(("parallel",)),
        name="attn_sample",
    )(q3, kt, vt, sink_col)


def _proj_residual_kernel(x_ref, a_ref, place_t_ref, rep_t_ref, w_ref, o_ref):
    z = _dot(a_ref[...].astype(BF16), place_t_ref[...])
    z_own = jnp.where(_own_head(z.shape), z, 0.0).astype(BF16)
    attn = _dot(rep_t_ref[...], z_own).astype(BF16)
    o_ref[...] = x_ref[...] + _dot(attn, w_ref[...])


def _proj_residual(x, a, place_t, rep_t, w):
    return pl.pallas_call(
        _proj_residual_kernel,
        out_shape=jax.ShapeDtypeStruct(x.shape, F32),
        compiler_params=pltpu.CompilerParams(vmem_limit_bytes=VMEM_LIMIT_BYTES),
        name="o_proj_sample",
    )(x, a, place_t, rep_t, w)


def _rope_tables(pos):
    half = HEAD_DIM // 2
    inv = ROPE_THETA ** (-jnp.arange(half, dtype=F32) / half)
    ang = pos.astype(F32)[:, None] * inv[None, :]
    cos, sin = jnp.cos(ang), jnp.sin(ang)
    reps = LANES // HEAD_DIM
    return (jnp.tile(jnp.concatenate([cos, cos], axis=-1), (1, reps)),
            jnp.tile(jnp.concatenate([-sin, sin], axis=-1), (1, reps)))


def _block_ones(width, block):
    idx = jnp.arange(width) // block
    return (idx[:, None] == idx[None, :]).astype(BF16)


def _dup_heads(w):
    w3 = w.reshape(w.shape[0], N_KV_HEADS, 1, HEAD_DIM)
    return jnp.broadcast_to(w3, (w.shape[0], N_KV_HEADS, LANES // HEAD_DIM, HEAD_DIM)).reshape(w.shape[0], -1)


def _head_placement():
    lane = jnp.arange(N_HEADS * HEAD_DIM)
    dst = (lane // HEAD_DIM) // (N_HEADS // N_KV_HEADS) * HEAD_DIM + lane % HEAD_DIM
    return (dst[:, None] == jnp.arange(N_KV_HEADS * HEAD_DIM)[None, :]).astype(BF16)


def _row_replication(n):
    return (jnp.arange(n * N_HEADS)[:, None] // N_HEADS == jnp.arange(n)[None, :]).astype(BF16)


def kernel(x_prompt, x_sample, state_ssm_re, state_ssm_im, cache_k, cache_v, norm_mix, norm_mlp, ssm_a_re, ssm_a_im, ssm_log_dt, ssm_b_re, ssm_b_im, ssm_c_re, ssm_c_im, ssm_d, w_glu, b_glu, norm_kv, w_k, w_v, k_norm, w_q, q_norm, attn_sinks, w_o, w_mlp_in, w_mlp_out):
    bsz, seq, _ = x_prompt.shape
    dec = x_sample.shape[0]
    n_a = ssm_a_re.shape[0]
    depth = norm_mix.shape[0]
    past = 8192
    nkv = N_KV_HEADS * HEAD_DIM
    row = lambda v: v.astype(F32).reshape(1, -1)

    xp = x_prompt.astype(F32)
    xs = x_sample.reshape(dec, D_MODEL)
    rows3 = lambda v: v.astype(F32).reshape(v.shape[0], 1, -1)

    w1 = w_mlp_in.astype(BF16)
    w2 = w_mlp_out.astype(BF16)
    g_mlp = rows3(norm_mlp)
    mlp_p = lambda x, layer: _mlp(x.reshape(bsz * seq, D_MODEL), g_mlp, w1, w2, layer).reshape(bsz, seq, D_MODEL)

    bhat, chat, lam = _s5_weights(ssm_a_re, ssm_a_im, ssm_log_dt, ssm_b_re, ssm_b_im, ssm_c_re, ssm_c_im)
    bhat16 = bhat.astype(BF16)
    wglu = w_glu.astype(BF16)
    g_mix, d_skip, bg = rows3(norm_mix), rows3(ssm_d), rows3(b_glu)
    sp_re, sp_im, ss_re, ss_im = [], [], [], []
    for i in range(n_a):
        xp, fin = _s5_prompt(xp, i, g_mix, bhat16, lam, chat, d_skip, wglu, bg)
        re, im = _lanes_to_state(fin[bsz:])
        sp_re.append(re)
        sp_im.append(im)
        s0 = _state_to_lanes(state_ssm_re[i], state_ssm_im[i])
        xs, snew = _s5_sample(xs, i, g_mix, bhat, lam, chat, d_skip, wglu, bg, s0)
        re, im = _lanes_to_state(snew)
        ss_re.append(re)
        ss_im.append(im)
        xp = mlp_p(xp, i)
        xs = _mlp(xs, g_mlp, w1, w2, i)

    cos_p, sin_p = _rope_tables(jnp.arange(seq, dtype=jnp.int32))
    cos_s, sin_s = _rope_tables(jnp.full((dec,), past, dtype=jnp.int32))
    ones_head = _block_ones(MXU_TILE, HEAD_DIM)
    place = _head_placement()
    rep = _row_replication(dec)
    kgain = row(k_norm)
    tile_gain = lambda gvec, n: jnp.tile(gvec, (1, n // HEAD_DIM))

    kd, vp, vt = _kv(xp.reshape(bsz * seq, D_MODEL), row(norm_kv), _dup_heads(w_k), w_v,
                     tile_gain(kgain, 2 * nkv), ones_head, cos_p, sin_p, lambda i: i % (seq // KV_ROWS), seq=seq)
    ks_new, vs_new = _kv(xs, row(norm_kv), w_k, w_v, tile_gain(kgain, nkv), ones_head, cos_s, sin_s,
                         lambda i: i)
    kd = kd.reshape(bsz, seq, 2 * nkv)
    new_k_p = kd[:, -WINDOW:].reshape(bsz, WINDOW, N_KV_HEADS, LANES // HEAD_DIM, HEAD_DIM)[:, :, :, 0]
    new_v_p = vp.reshape(bsz, seq, N_KV_HEADS, HEAD_DIM)[:, -WINDOW:]
    key_minor = lambda c: jnp.transpose(c.astype(F32), (0, 2, 3, 1)).reshape(dec, nkv, c.shape[1])
    keys_s, vals_s = _cache_append(key_minor(cache_k), key_minor(cache_v), ks_new.T, vs_new.T)
    key_major = lambda c: jnp.transpose(c.reshape(dec, N_KV_HEADS, HEAD_DIM, c.shape[2]), (0, 3, 1, 2))
    new_k_s, new_v_s = key_major(keys_s), key_major(vals_s)

    for j in range(depth - n_a):
        layer = n_a + j
        gn = row(norm_mix[layer])
        qgain = row(q_norm[j])
        wq = w_q[j].astype(BF16)
        wo = w_o[j].astype(BF16)
        xp = _attn_prompt(xp, gn, wq, tile_gain(qgain, D_MODEL), ones_head, cos_p, sin_p, kd, vt,
                          attn_sinks[j].astype(F32), wo)
        q_pad = _q_sample(xs, gn, wq, tile_gain(qgain, D_MODEL), ones_head, cos_s, sin_s, rep, place)
        o_pad = _attn_sample(q_pad.reshape(dec, N_HEADS, nkv), keys_s, vals_s,
                             attn_sinks[j].astype(F32).reshape(N_HEADS, 1))
        xs = _proj_residual(xs, o_pad.reshape(dec * N_HEADS, nkv), place.T, rep.T, wo)
        xp = mlp_p(xp, layer)
        xs = _mlp(xs, g_mlp, w1, w2, layer)

    return (xp, xs.reshape(dec, 1, D_MODEL),
            jnp.stack(sp_re), jnp.stack(sp_im), new_k_p, new_v_p,
            jnp.stack(ss_re), jnp.stack(ss_im), new_k_s, new_v_s)
```

```python
import functools
import math

import jax
import jax.numpy as jnp
from jax import lax
from jax.experimental import pallas as pl
from jax.experimental.pallas import tpu as pltpu

F32 = jnp.float32
BF16 = jnp.bfloat16

D_MODEL = 1024
N_GROUPS = 64
GROUP_SIZE = 16
STATE_DIM = 64
HEAD_DIM = 64
N_HEADS = 16
N_KV_HEADS = 4
WINDOW = 128
ROPE_THETA = 10000.0
D_FF = 4 * D_MODEL
EPS = 1e-6

SUBLANES = 8
LANES = 128
MXU_TILE = 256
VMEM_LIMIT_BYTES = 60 * 1024 * 1024

N_CH_BLOCKS = D_MODEL // MXU_TILE
STATE_BLOCK = (MXU_TILE // GROUP_SIZE) * STATE_DIM
STATE_LANES = 2 * STATE_BLOCK * N_CH_BLOCKS

S5_TIME_CHUNK = 128
MLP_ROWS = 512
FF_CHUNK = 1024
KV_ROWS = 512
ATTN_BLOCKS = 2


def _const_spec(shape):
    zeros = (0,) * len(shape)
    return pl.BlockSpec(shape, lambda *_: zeros, pipeline_mode=pl.Buffered(1))


def _layer_spec(shape, layer):
    idx = (layer,) + (0,) * (len(shape) - 1)
    return pl.BlockSpec((None,) + tuple(shape[1:]), lambda *_: idx, pipeline_mode=pl.Buffered(1))


def _params(sem):
    return pltpu.CompilerParams(dimension_semantics=sem, vmem_limit_bytes=VMEM_LIMIT_BYTES)


def _rms(x, g):
    return x * lax.rsqrt(jnp.mean(x * x, axis=-1, keepdims=True) + EPS) * g


def _dot(a, b):
    return jnp.dot(a, b, preferred_element_type=F32)


def _seg_mean_sq(x, ones_ref):
    sq = x * x
    hi = sq.astype(BF16)
    lo = (sq - hi.astype(F32)).astype(BF16)
    ones = ones_ref[...]
    outs = []
    for t in range(x.shape[-1] // MXU_TILE):
        sl = slice(MXU_TILE * t, MXU_TILE * (t + 1))
        outs.append(_dot(hi[:, sl], ones) + _dot(lo[:, sl], ones))
    return jnp.concatenate(outs, axis=-1) * (1.0 / HEAD_DIM)


def _tile_lanes(t, n):
    return jnp.concatenate([t] * (n // t.shape[-1]), axis=-1)


def _rope(x, cos, sin_signed):
    n = x.shape[-1]
    lane = lax.broadcasted_iota(jnp.int32, x.shape, 1)
    first = (lane & (HEAD_DIM - 1)) < (HEAD_DIM // 2)
    partner = jnp.where(first, pltpu.roll(x, n - HEAD_DIM // 2, 1), pltpu.roll(x, HEAD_DIM // 2, 1))
    return x * cos + partner * sin_signed


def _mlp_kernel(x_ref, g_ref, w1_ref, w2_ref, o_ref):
    x = x_ref[...]
    h = _rms(x, g_ref[...]).astype(BF16)
    acc = x
    for c in range(D_FF // FF_CHUNK):
        sl = slice(FF_CHUNK * c, FF_CHUNK * (c + 1))
        a = jnp.square(jnp.maximum(_dot(h, w1_ref[:, sl]), 0.0)).astype(BF16)
        acc = acc + _dot(a, w2_ref[sl, :])
    o_ref[...] = acc


def _mlp(x, g, w1, w2, layer):
    rows = x.shape[0]
    tm = min(MLP_ROWS, rows)
    return pl.pallas_call(
        _mlp_kernel,
        grid=(rows // tm,),
        in_specs=[pl.BlockSpec((tm, D_MODEL), lambda i: (i, 0)),
                  _layer_spec(g.shape, layer),
                  _layer_spec(w1.shape, layer),
                  _layer_spec(w2.shape, layer)],
        out_specs=pl.BlockSpec((tm, D_MODEL), lambda i: (i, 0)),
        out_shape=jax.ShapeDtypeStruct((rows, D_MODEL), F32),
        compiler_params=_params(("parallel",)),
        name="mlp",
    )(x, g, w1, w2)


def _s5_gate(y, wglu_ref, bglu_ref):
    g = jax.nn.gelu(y).astype(BF16)
    z = _dot(g, wglu_ref[...]) + bglu_ref[...]
    return z[:, :D_MODEL] * jax.nn.sigmoid(z[:, D_MODEL:])


def _s5_prompt_kernel(xa_ref, xb_ref, gn_ref, bhat_ref, lam_ref, chat_ref, d_ref, wglu_ref, bglu_ref,
                      o_ref, fin_ref, st_ref, sb_ref, carry_ref, tb_ref, tb2_ref, y_ref):
    batch, tc, _ = xa_ref.shape
    assert 2 * batch == SUBLANES
    n_slabs = D_MODEL // LANES
    step = pl.program_id(0)
    live = step < pl.num_programs(0) - 1

    @pl.when(step == 0)
    def _():
        carry_ref[...] = jnp.zeros_like(carry_ref)
        y_ref[...] = jnp.zeros_like(y_ref)

    g_prev = jax.nn.gelu(y_ref[...]).astype(BF16)
    for b in range(batch):
        ub_nat = _rms(xa_ref[b], gn_ref[...])
        for c in range(n_slabs):
            tb_ref[c, pl.ds(b, tc, stride=batch), :] = ub_nat[:, LANES * c:LANES * (c + 1)]
    ub = jnp.concatenate([tb_ref[c] for c in range(n_slabs)], axis=-1).astype(BF16)

    def project_in(i):
        base = 2 * STATE_BLOCK * i
        st_ref[:, base:base + 2 * STATE_BLOCK] = _dot(ub[:, MXU_TILE * i:MXU_TILE * (i + 1)], bhat_ref[i])

    ys = [None] * N_CH_BLOCKS

    def project_out(i):
        base = 2 * STATE_BLOCK * i
        ys[i] = _dot(sb_ref[:, base:base + 2 * STATE_BLOCK], chat_ref[i])

    glu_chunks = 4
    glu_cols = 2 * D_MODEL // glu_chunks
    z = [None] * glu_chunks

    def glu_part(c):
        sl = slice(glu_cols * c, glu_cols * (c + 1))
        z[c] = _dot(g_prev, wglu_ref[:, sl]) + bglu_ref[:, sl]

    def finish_prev():
        half = glu_chunks // 2
        upd = jnp.concatenate(z[:half], axis=-1) * jax.nn.sigmoid(jnp.concatenate(z[half:], axis=-1))
        for c in range(n_slabs):
            tb2_ref[c] = upd[:, LANES * c:LANES * (c + 1)]
        for b in range(batch):
            o_ref[b] = xb_ref[b] + jnp.concatenate(
                [tb2_ref[c, pl.ds(b, tc, stride=batch), :] for c in range(n_slabs)], axis=-1)

    quarters = 4
    between = {
        (0, 0): [lambda: glu_part(0)], (0, 1): [lambda: glu_part(1)],
        (0, 2): [lambda: glu_part(2)], (0, 3): [lambda: glu_part(3)],
        (1, 0): [lambda: project_in(2)], (1, 1): [finish_prev], (1, 2): [lambda: project_in(3)],
        (1, 3): [lambda: project_out(0)],
        (2, 1): [lambda: project_out(1)],
        (3, 1): [lambda: project_out(2)],
    }
    project_in(0)
    project_in(1)

    lo = lax.broadcasted_iota(jnp.int32, (SUBLANES, LANES), 0) < batch
    swap = lambda v: pltpu.roll(v, batch, 0)
    n_vregs = batch * tc // SUBLANES
    half_cols = STATE_BLOCK // LANES // 2
    for i in range(N_CH_BLOCKS):
        base = 2 * STATE_BLOCK * i
        cols, lam_r, lam_i, sr, si, old = [], [], [], [], [], []
        for j in range(half_cols):
            ca, cb = base + LANES * j, base + LANES * (j + half_cols)
            cols.append((ca, cb))
            la, lb = STATE_BLOCK * i + LANES * j, STATE_BLOCK * i + LANES * (j + half_cols)
            bc = lambda r, c: jnp.broadcast_to(lam_ref[r:r + 1, c:c + LANES], (SUBLANES, LANES))
            lam_r.append(jnp.where(lo, bc(0, la), bc(0, lb)))
            lam_i.append(jnp.where(lo, bc(1, la), bc(1, lb)))
            prev = [carry_ref[:, c:c + LANES] for c in (ca, cb, ca + STATE_BLOCK, cb + STATE_BLOCK)]
            old.append(prev)
            sr.append(jnp.where(lo, swap(prev[0]), prev[1]))
            si.append(jnp.where(lo, swap(prev[2]), prev[3]))
        last = [None] * half_cols
        held = [None] * half_cols
        for m in range(n_vregs):
            rows = slice(SUBLANES * m, SUBLANES * (m + 1))
            for j in range(half_cols):
                ca, cb = cols[j]
                ar, br = st_ref[rows, ca:ca + LANES], st_ref[rows, cb:cb + LANES]
                ai, bi = (st_ref[rows, ca + STATE_BLOCK:ca + STATE_BLOCK + LANES],
                          st_ref[rows, cb + STATE_BLOCK:cb + STATE_BLOCK + LANES])
                d0r, d1r = jnp.where(lo, ar, swap(br)), jnp.where(lo, swap(ar), br)
                d0i, d1i = jnp.where(lo, ai, swap(bi)), jnp.where(lo, swap(ai), bi)
                s0r = lam_r[j] * sr[j] - lam_i[j] * si[j] + d0r
                s0i = lam_r[j] * si[j] + lam_i[j] * sr[j] + d0i
                s1r = lam_r[j] * s0r - lam_i[j] * s0i + d1r
                s1i = lam_r[j] * s0i + lam_i[j] * s0r + d1i
                sr[j], si[j] = s1r, s1i
                out = (jnp.where(lo, s0r, swap(s1r)), jnp.where(lo, swap(s0r), s1r),
                       jnp.where(lo, s0i, swap(s1i)), jnp.where(lo, swap(s0i), s1i))
                if m % 2 == 0:
                    held[j] = out
                else:
                    rows2 = slice(SUBLANES * (m - 1), SUBLANES * (m + 1))
                    for c, even, odd in zip((ca, cb, ca + STATE_BLOCK, cb + STATE_BLOCK), held[j], out):
                        sb_ref[rows2, c:c + LANES] = jnp.concatenate([even, odd], axis=0).astype(BF16)
                last[j] = out
            if (m + 1) % (n_vregs // quarters) == 0:
                for work in between.get((i, (m + 1) // (n_vregs // quarters) - 1), []):
                    work()
        for j in range(half_cols):
            ca, cb = cols[j]
            for c, new, prev in zip((ca, cb, ca + STATE_BLOCK, cb + STATE_BLOCK), last[j], old[j]):
                carry_ref[:, c:c + LANES] = jnp.where(live, new, prev)
    project_out(N_CH_BLOCKS - 1)
    u = jnp.concatenate([tb_ref[c] for c in range(n_slabs)], axis=-1)
    y_ref[...] = jnp.concatenate(ys, axis=-1) + d_ref[...] * u
    fin_ref[...] = carry_ref[...]


def _s5_prompt(x, layer, gn, bhat, lam, chat, d, wglu, bglu):
    batch, seq, _ = x.shape
    tc = S5_TIME_CHUNK
    n_chunks = seq // tc
    blk = (batch, tc, D_MODEL)
    this_chunk = pl.BlockSpec(blk, lambda s: (0, jnp.minimum(s, n_chunks - 1), 0))
    prev_chunk = pl.BlockSpec(blk, lambda s: (0, jnp.maximum(s - 1, 0), 0))
    slab = pltpu.VMEM((D_MODEL // LANES, batch * tc, LANES), F32)
    return pl.pallas_call(
        _s5_prompt_kernel,
        grid=(n_chunks + 1,),
        in_specs=[this_chunk, prev_chunk]
        + [_layer_spec(w.shape, layer) for w in (gn, bhat, lam, chat, d, wglu, bglu)],
        out_specs=[prev_chunk, pl.BlockSpec((SUBLANES, STATE_LANES), lambda s: (0, 0))],
        out_shape=[jax.ShapeDtypeStruct(x.shape, F32),
                   jax.ShapeDtypeStruct((SUBLANES, STATE_LANES), F32)],
        scratch_shapes=[pltpu.VMEM((batch * tc, STATE_LANES), F32),
                        pltpu.VMEM((batch * tc, STATE_LANES), BF16),
                        pltpu.VMEM((SUBLANES, STATE_LANES), F32),
                        slab, slab,
                        pltpu.VMEM((batch * tc, D_MODEL), F32)],
        compiler_params=_params(("arbitrary",)),
        name="s5_prompt",
    )(x, x, gn, bhat, lam, chat, d, wglu, bglu)


def _s5_sample_kernel(x_ref, gn_ref, bhat_ref, lam_ref, chat_ref, d_ref, wglu_ref, bglu_ref, s0_ref,
                      o_ref, snew_ref):
    x = x_ref[...]
    u = _rms(x, gn_ref[...])
    ys = []
    for i in range(N_CH_BLOCKS):
        base = 2 * STATE_BLOCK * i
        bu = jnp.dot(u[:, MXU_TILE * i:MXU_TILE * (i + 1)], bhat_ref[i],
                     preferred_element_type=F32, precision=lax.Precision.HIGHEST)
        lr = lam_ref[0:1, STATE_BLOCK * i:STATE_BLOCK * (i + 1)]
        li = lam_ref[1:2, STATE_BLOCK * i:STATE_BLOCK * (i + 1)]
        s0r = s0_ref[:, base:base + STATE_BLOCK]
        s0i = s0_ref[:, base + STATE_BLOCK:base + 2 * STATE_BLOCK]
        sr = lr * s0r - li * s0i + bu[:, :STATE_BLOCK]
        si = lr * s0i + li * s0r + bu[:, STATE_BLOCK:]
        snew_ref[:, base:base + STATE_BLOCK] = sr
        snew_ref[:, base + STATE_BLOCK:base + 2 * STATE_BLOCK] = si
        sb = jnp.concatenate([sr, si], axis=-1).astype(BF16)
        ys.append(_dot(sb, chat_ref[i]))
    y = jnp.concatenate(ys, axis=-1)
    o_ref[...] = x + _s5_gate(y + d_ref[...] * u, wglu_ref, bglu_ref)


def _s5_sample(x, layer, gn, bhat32, lam, chat, d, wglu, bglu, s0):
    rows = x.shape[0]
    return pl.pallas_call(
        _s5_sample_kernel,
        grid=(1,),
        in_specs=[_const_spec(x.shape)] + [_layer_spec(w.shape, layer) for w in (gn, bhat32, lam, chat, d, wglu, bglu)]
        + [_const_spec(s0.shape)],
        out_specs=[pl.BlockSpec((rows, D_MODEL), lambda i: (0, 0)),
                   pl.BlockSpec((rows, STATE_LANES), lambda i: (0, 0))],
        out_shape=[jax.ShapeDtypeStruct((rows, D_MODEL), F32),
                   jax.ShapeDtypeStruct((rows, STATE_LANES), F32)],
        compiler_params=_params(("arbitrary",)),
        name="s5_sample",
    )(x, gn, bhat32, lam, chat, d, wglu, bglu, s0)


def _s5_weights(a_re, a_im, log_dt, b_re, b_im, c_re, c_im):
    n = a_re.shape[0]
    dt = jnp.exp(log_dt.astype(F32))
    lam = lax.complex(a_re.astype(F32), a_im.astype(F32))
    lam_bar = jnp.exp(lam * dt)
    b = lax.complex(b_re.astype(F32), b_im.astype(F32))
    b_bar = ((lam_bar - 1.0) / lam)[..., None] * b
    gpb = MXU_TILE // GROUP_SIZE
    hi = lax.Precision.HIGHEST

    def compact_in(w):
        return jnp.swapaxes(w.reshape(n, N_CH_BLOCKS, gpb, STATE_DIM, GROUP_SIZE), 3, 4).reshape(
            n, N_CH_BLOCKS, MXU_TILE, STATE_DIM)

    def compact_out(w):
        return jnp.swapaxes(w.reshape(n, N_CH_BLOCKS, gpb, GROUP_SIZE, STATE_DIM), 3, 4).reshape(
            n, N_CH_BLOCKS, STATE_BLOCK, GROUP_SIZE)

    lane = jnp.arange(2 * STATE_BLOCK)
    tile_in = (jnp.arange(2 * STATE_DIM)[:, None] == (lane // STATE_BLOCK) * STATE_DIM + lane % STATE_DIM)
    own_in = (jnp.arange(MXU_TILE)[:, None] // GROUP_SIZE == (lane % STATE_BLOCK) // STATE_DIM)
    bc = jnp.concatenate([compact_in(jnp.real(b_bar)), compact_in(jnp.imag(b_bar))], axis=-1)
    bhat = jnp.einsum("nirk,kl->nirl", bc, tile_in.astype(F32), precision=hi) * own_in.astype(F32)

    col = jnp.arange(MXU_TILE)
    tile_out = (jnp.arange(GROUP_SIZE)[:, None] == col % GROUP_SIZE)
    own_out = ((lane[:, None] % STATE_BLOCK) // STATE_DIM == col // GROUP_SIZE)
    cc = jnp.concatenate([compact_out(c_re.astype(F32)), -compact_out(c_im.astype(F32))], axis=2)
    chat = jnp.einsum("nirc,cl->nirl", cc, tile_out.astype(F32), precision=hi) * own_out.astype(F32)
    lam_rows = jnp.stack([jnp.real(lam_bar).reshape(n, -1), jnp.imag(lam_bar).reshape(n, -1)], axis=1)
    return bhat, chat.astype(BF16), lam_rows


def _state_to_lanes(s_re, s_im):
    n = s_re.shape[0]
    re = s_re.astype(F32).reshape(n, N_CH_BLOCKS, STATE_BLOCK)
    im = s_im.astype(F32).reshape(n, N_CH_BLOCKS, STATE_BLOCK)
    return jnp.concatenate([re, im], axis=-1).reshape(n, STATE_LANES)


def _lanes_to_state(s):
    n = s.shape[0]
    s4 = s.reshape(n, N_CH_BLOCKS, 2, STATE_BLOCK)
    return (s4[:, :, 0].reshape(n, N_GROUPS, STATE_DIM), s4[:, :, 1].reshape(n, N_GROUPS, STATE_DIM))


def _rope_partner(n):
    lane = jnp.arange(n)
    return jnp.where(lane % HEAD_DIM < HEAD_DIM // 2, lane + HEAD_DIM // 2, lane - HEAD_DIM // 2)


def _kv_kernel(x_ref, g_ref, wkv_ref, kg_ref, kgp_ref, ones_ref, cos_ref, sin_ref, *rest):
    if len(rest) == 4:
        wvt_ref, k_ref, v_ref, vt_ref = rest
    else:
        (k_ref, v_ref), wvt_ref, vt_ref = rest, None, None
    h = _rms(x_ref[...], g_ref[...]).astype(BF16)
    kv = _dot(h, wkv_ref[...])
    n = k_ref.shape[-1]
    k, k_partner = kv[:, :n], kv[:, n:2 * n]
    scale = lax.rsqrt(_seg_mean_sq(k, ones_ref) + EPS)
    k_ref[...] = scale * (k * (kg_ref[...] * _tile_lanes(cos_ref[...], n))
                          + k_partner * (kgp_ref[...] * _tile_lanes(sin_ref[...], n)))
    v_ref[...] = kv[:, 2 * n:]
    if vt_ref is not None:
        vt_ref[...] = lax.dot_general(wvt_ref[...], h, (((1,), (1,)), ((), ())), preferred_element_type=F32)


def _kv(x, g, wk, wv, kgain, ones, cos, sin, table_block, seq=None):
    rows = x.shape[0]
    n, nv = wk.shape[1], wv.shape[1]
    partner = _rope_partner(n)
    wkv = jnp.concatenate([wk, wk[:, partner], wv], axis=1).astype(BF16)
    tm = min(KV_ROWS, rows)
    in_specs = [pl.BlockSpec((tm, D_MODEL), lambda i: (i, 0)),
                _const_spec((1, D_MODEL)),
                _const_spec(wkv.shape),
                _const_spec((1, n)),
                _const_spec((1, n)),
                _const_spec(ones.shape),
                pl.BlockSpec((tm, LANES), lambda i: (table_block(i), 0)),
                pl.BlockSpec((tm, LANES), lambda i: (table_block(i), 0))]
    out_specs = [pl.BlockSpec((tm, n), lambda i: (i, 0)),
                 pl.BlockSpec((tm, nv), lambda i: (i, 0))]
    out_shape = [jax.ShapeDtypeStruct((rows, n), F32),
                 jax.ShapeDtypeStruct((rows, nv), F32)]
    args = [x, g, wkv, kgain, kgain[:, partner], ones, cos, sin]
    if seq is not None:
        per_seq = seq // tm
        in_specs.append(_const_spec((nv, D_MODEL)))
        args.append(wv.T.astype(BF16))
        out_specs.append(pl.BlockSpec((None, nv, tm), lambda i: (i // per_seq, 0, i % per_seq)))
        out_shape.append(jax.ShapeDtypeStruct((rows // seq, nv, seq), F32))
    return pl.pallas_call(
        _kv_kernel,
        grid=(rows // tm,),
        in_specs=in_specs,
        out_specs=out_specs,
        out_shape=out_shape,
        compiler_params=_params(("parallel",)),
        name="kv_proj",
    )(*args)


def _attn_prompt_kernel(sinks_ref, x_ref, g_ref, wq_ref, qg_ref, ones_ref, cos_ref, sin_ref,
                        kp_ref, kc_ref, vp_ref, vc_ref, wo_ref, o_ref):
    tile = pl.program_id(1)
    x = x_ref[...]
    n_blocks = x.shape[0] // WINDOW
    heads_per_kv = N_HEADS // N_KV_HEADS
    cols = heads_per_kv * WINDOW
    h = _rms(x, g_ref[...]).astype(BF16)
    q = _dot(h, wq_ref[...])
    q = q * lax.rsqrt(_seg_mean_sq(q, ones_ref) + EPS) * qg_ref[...]
    q = _rope(q, _tile_lanes(cos_ref[...], D_MODEL), _tile_lanes(sin_ref[...], D_MODEL))
    qb = (q * (HEAD_DIM ** -0.5)).astype(BF16)

    kall = jnp.concatenate([kp_ref[...], kc_ref[...]], axis=0).astype(BF16)
    vall = jnp.concatenate([vp_ref[...], vc_ref[...]], axis=1).astype(BF16)

    kj = lax.broadcasted_iota(jnp.int32, (WINDOW, cols), 0)
    col = lax.broadcasted_iota(jnp.int32, (WINDOW, cols), 1)
    from_prev = kj > (col & (WINDOW - 1))
    col_head = lax.broadcasted_iota(jnp.int32, (1, cols), 1) // WINDOW
    low_half = lax.broadcasted_iota(jnp.int32, (WINDOW, LANES), 1) < HEAD_DIM
    neg_inf = jnp.full((WINDOW, cols), -jnp.inf, F32)

    pairs = [(blk, g) for blk in range(n_blocks) for g in range(N_KV_HEADS)]
    scores, sinks = [], []
    for blk, g in pairs:
        r0 = WINDOW * blk
        kg = kall[r0:r0 + 2 * WINDOW, LANES * g:LANES * (g + 1)]
        qs = []
        sink = jnp.zeros((1, cols), F32)
        for hh in range(heads_per_kv):
            head = heads_per_kv * g + hh
            qp = qb[r0:r0 + WINDOW, LANES * (head // 2):LANES * (head // 2 + 1)]
            keep = low_half if head % 2 == 0 else jnp.logical_not(low_half)
            qs.append(jnp.where(keep, qp, jnp.zeros_like(qp)))
            sink = jnp.where(col_head == hh, sinks_ref[head], sink)
        sinks.append(sink)
        scores.append(lax.dot_general(kg, jnp.concatenate(qs, axis=0), (((1,), (1,)), ((), ())),
                                      preferred_element_type=F32))
    probs, denoms = [], []
    for (blk, g), s, sink in zip(pairs, scores, sinks):
        has_prev = tile * n_blocks + blk > 0
        t = jnp.where(from_prev, jnp.where(has_prev, s[:WINDOW], neg_inf), s[WINDOW:])
        m = jnp.maximum(jnp.max(t, axis=0, keepdims=True), sink)
        p = jnp.exp(t - m)
        denoms.append(jnp.sum(p, axis=0, keepdims=True) + jnp.exp(sink - m))
        zero = jnp.zeros_like(p)
        probs.append(jnp.concatenate([jnp.where(from_prev, p, zero), jnp.where(from_prev, zero, p)],
                                     axis=0).astype(BF16))
    head_outs = [[] for _ in range(n_blocks)]
    for (blk, g), pcat, denom in zip(pairs, probs, denoms):
        r0 = WINDOW * blk
        vg = vall[HEAD_DIM * g:HEAD_DIM * (g + 1), r0:r0 + 2 * WINDOW]
        o = _dot(vg, pcat) / denom
        head_outs[blk] += [o[:, WINDOW * hh:WINDOW * (hh + 1)] for hh in range(heads_per_kv)]
    block_outs = [jnp.concatenate(outs, axis=0).T for outs in head_outs]
    attn = jnp.concatenate(block_outs, axis=0).astype(BF16)
    o_ref[...] = x + _dot(attn, wo_ref[...])


def _attn_prompt(x, g, wq, qgain, ones, cos, sin, kd, vt, sinks, wo):
    bsz, seq, _ = x.shape
    nk, nv = kd.shape[-1], vt.shape[1]
    nb = ATTN_BLOCKS
    tq = nb * WINDOW
    x_spec = pl.BlockSpec((None, tq, D_MODEL), lambda b, i, *_: (b, i, 0))
    k_cur = pl.BlockSpec((None, tq, nk), lambda b, i, *_: (b, i, 0))
    k_prev = pl.BlockSpec((None, WINDOW, nk), lambda b, i, *_: (b, jnp.maximum(nb * i - 1, 0), 0))
    v_cur = pl.BlockSpec((None, nv, tq), lambda b, i, *_: (b, 0, i))
    v_prev = pl.BlockSpec((None, nv, WINDOW), lambda b, i, *_: (b, 0, jnp.maximum(nb * i - 1, 0)))
    tab = pl.BlockSpec((tq, LANES), lambda b, i, *_: (i, 0))
    grid_spec = pltpu.PrefetchScalarGridSpec(
        num_scalar_prefetch=1,
        grid=(bsz, seq // tq),
        in_specs=[x_spec,
                  _const_spec((1, D_MODEL)),
                  _const_spec(wq.shape),
                  _const_spec((1, D_MODEL)),
                  _const_spec(ones.shape),
                  tab, tab, k_prev, k_cur, v_prev, v_cur,
                  _const_spec(wo.shape)],
        out_specs=x_spec)
    return pl.pallas_call(
        _attn_prompt_kernel,
        grid_spec=grid_spec,
        out_shape=jax.ShapeDtypeStruct(x.shape, F32),
        compiler_params=_params(("parallel", "arbitrary")),
        name="attn_prompt",
    )(sinks, x, g, wq, qgain, ones, cos, sin, kd, kd, vt, vt, wo)


def _own_head(shape):
    row_head = lax.broadcasted_iota(jnp.int32, shape, 0) & (N_HEADS - 1)
    lane_head = lax.broadcasted_iota(jnp.int32, shape, 1) // HEAD_DIM
    return row_head == lane_head


def _q_sample_kernel(x_ref, g_ref, wq_ref, qg_ref, ones_ref, cos_ref, sin_ref, rep_ref, place_ref, q_ref):
    h = _rms(x_ref[...], g_ref[...]).astype(BF16)
    q = _dot(h, wq_ref[...])
    q = q * lax.rsqrt(_seg_mean_sq(q, ones_ref) + EPS) * qg_ref[...]
    q = _rope(q, _tile_lanes(cos_ref[...], D_MODEL), _tile_lanes(sin_ref[...], D_MODEL))
    qb = (q * (HEAD_DIM ** -0.5)).astype(BF16)
    q_rep = _dot(rep_ref[...], qb)
    q_own = jnp.where(_own_head(q_rep.shape), q_rep, 0.0).astype(BF16)
    q_ref[...] = _dot(q_own, place_ref[...])


def _q_sample(x, g, wq, qgain, ones, cos, sin, rep, place):
    rows = x.shape[0]
    return pl.pallas_call(
        _q_sample_kernel,
        out_shape=jax.ShapeDtypeStruct((rows * N_HEADS, place.shape[1]), F32),
        compiler_params=pltpu.CompilerParams(vmem_limit_bytes=VMEM_LIMIT_BYTES),
        name="q_sample",
    )(x, g, wq, qgain, ones, cos, sin, rep, place)


def _cache_append_kernel(k_ref, v_ref, kn_ref, vn_ref, ko_ref, vo_ref):
    nb, lanes_kv, keys = k_ref.shape
    first = pl.program_id(0) * nb
    key = lax.broadcasted_iota(jnp.int32, (lanes_kv, keys), 1)
    seq = lax.broadcasted_iota(jnp.int32, (kn_ref.shape[1], keys), 0)

    def split3(x):
        hi = x.astype(BF16)
        r = x - hi.astype(F32)
        mid = r.astype(BF16)
        return hi, mid, (r - mid.astype(F32)).astype(BF16)

    parts = [(split3(kn_ref[...]), k_ref, ko_ref), (split3(vn_ref[...]), v_ref, vo_ref)]
    for b in range(nb):
        pick = (seq == first + b).astype(BF16)
        for (hi, mid, lo), old_ref, out_ref in parts:
            new_col = _dot(hi, pick) + _dot(mid, pick) + _dot(lo, pick)
            out_ref[b] = jnp.where(key == keys - 1, new_col, pltpu.roll(old_ref[b], keys - 1, 1))


def _cache_append(kt, vt, kn_t, vn_t):
    bsz, lanes_kv, keys = kt.shape
    nb = SUBLANES
    blk = pl.BlockSpec((nb, lanes_kv, keys), lambda i: (i, 0, 0))
    return pl.pallas_call(
        _cache_append_kernel,
        grid=(bsz // nb,),
        in_specs=[blk, blk, _const_spec(kn_t.shape), _const_spec(vn_t.shape)],
        out_specs=[blk, blk],
        out_shape=[jax.ShapeDtypeStruct(kt.shape, F32), jax.ShapeDtypeStruct(vt.shape, F32)],
        compiler_params=_params(("parallel",)),
        name="cache_append",
    )(kt, vt, kn_t, vn_t)


def _attn_sample_kernel(q_ref, k_ref, v_ref, sink_ref, o_ref):
    sink = sink_ref[...]
    for b in range(q_ref.shape[0]):
        s = _dot(q_ref[b].astype(BF16), k_ref[b].astype(BF16))
        m = jnp.maximum(jnp.max(s, axis=-1, keepdims=True), sink)
        p = jnp.exp(s - m)
        denom = jnp.sum(p, axis=-1, keepdims=True) + jnp.exp(sink - m)
        o = lax.dot_general(p.astype(BF16), v_ref[b].astype(BF16), (((1,), (1,)), ((), ())),
                            preferred_element_type=F32)
        o_ref[b] = o / denom


def _attn_sample(q3, kt, vt, sink_col):
    bsz, nh, nkv = q3.shape
    keys = kt.shape[2]
    bb = SUBLANES
    return pl.pallas_call(
        _attn_sample_kernel,
        grid=(bsz // bb,),
        in_specs=[pl.BlockSpec((bb, nh, nkv), lambda i: (i, 0, 0)),
                  pl.BlockSpec((bb, nkv, keys), lambda i: (i, 0, 0)),
                  pl.BlockSpec((bb, nkv, keys), lambda i: (i, 0, 0)),
                  _const_spec(sink_col.shape)],
        out_specs=pl.BlockSpec((bb, nh, nkv), lambda i: (i, 0, 0)),
        out_shape=jax.ShapeDtypeStruct((bsz, nh, nkv), F32),
        compiler_params=_params(("parallel",)),
        name="attn_sample",
    )(q3, kt, vt, sink_col)


def _proj_residual_kernel(x_ref, a_ref, place_t_ref, rep_t_ref, w_ref, o_ref):
    z = _dot(a_ref[...].astype(BF16), place_t_ref[...])
    z_own = jnp.where(_own_head(z.shape), z, 0.0).astype(BF16)
    attn = _dot(rep_t_ref[...], z_own).astype(BF16)
    o_ref[...] = x_ref[...] + _dot(attn, w_ref[...])


def _proj_residual(x, a, place_t, rep_t, w):
    return pl.pallas_call(
        _proj_residual_kernel,
        out_shape=jax.ShapeDtypeStruct(x.shape, F32),
        compiler_params=pltpu.CompilerParams(vmem_limit_bytes=VMEM_LIMIT_BYTES),
        name="o_proj_sample",
    )(x, a, place_t, rep_t, w)


def _rope_tables(pos):
    half = HEAD_DIM // 2
    inv = ROPE_THETA ** (-jnp.arange(half, dtype=F32) / half)
    ang = pos.astype(F32)[:, None] * inv[None, :]
    cos, sin = jnp.cos(ang), jnp.sin(ang)
    reps = LANES // HEAD_DIM
    return (jnp.tile(jnp.concatenate([cos, cos], axis=-1), (1, reps)),
            jnp.tile(jnp.concatenate([-sin, sin], axis=-1), (1, reps)))


def _block_ones(width, block):
    idx = jnp.arange(width) // block
    return (idx[:, None] == idx[None, :]).astype(BF16)


def _dup_heads(w):
    w3 = w.reshape(w.shape[0], N_KV_HEADS, 1, HEAD_DIM)
    return jnp.broadcast_to(w3, (w.shape[0], N_KV_HEADS, LANES // HEAD_DIM, HEAD_DIM)).reshape(w.shape[0], -1)


def _head_placement():
    lane = jnp.arange(N_HEADS * HEAD_DIM)
    dst = (lane // HEAD_DIM) // (N_HEADS // N_KV_HEADS) * HEAD_DIM + lane % HEAD_DIM
    return (dst[:, None] == jnp.arange(N_KV_HEADS * HEAD_DIM)[None, :]).astype(BF16)


def _row_replication(n):
    return (jnp.arange(n * N_HEADS)[:, None] // N_HEADS == jnp.arange(n)[None, :]).astype(BF16)


def kernel(x_prompt, x_sample, state_ssm_re, state_ssm_im, cache_k, cache_v, norm_mix, norm_mlp, ssm_a_re, ssm_a_im, ssm_log_dt, ssm_b_re, ssm_b_im, ssm_c_re, ssm_c_im, ssm_d, w_glu, b_glu, norm_kv, w_k, w_v, k_norm, w_q, q_norm, attn_sinks, w_o, w_mlp_in, w_mlp_out):
    bsz, seq, _ = x_prompt.shape
    dec = x_sample.shape[0]
    n_a = ssm_a_re.shape[0]
    depth = norm_mix.shape[0]
    past = 8192
    nkv = N_KV_HEADS * HEAD_DIM
    row = lambda v: v.astype(F32).reshape(1, -1)

    xp = x_prompt.astype(F32)
    xs = x_sample.reshape(dec, D_MODEL)
    rows3 = lambda v: v.astype(F32).reshape(v.shape[0], 1, -1)

    w1 = w_mlp_in.astype(BF16)
    w2 = w_mlp_out.astype(BF16)
    g_mlp = rows3(norm_mlp)
    mlp_p = lambda x, layer: _mlp(x.reshape(bsz * seq, D_MODEL), g_mlp, w1, w2, layer).reshape(bsz, seq, D_MODEL)

    bhat, chat, lam = _s5_weights(ssm_a_re, ssm_a_im, ssm_log_dt, ssm_b_re, ssm_b_im, ssm_c_re, ssm_c_im)
    bhat16 = bhat.astype(BF16)
    wglu = w_glu.astype(BF16)
    g_mix, d_skip, bg = rows3(norm_mix), rows3(ssm_d), rows3(b_glu)
    sp_re, sp_im, ss_re, ss_im = [], [], [], []
    for i in range(n_a):
        xp, fin = _s5_prompt(xp, i, g_mix, bhat16, lam, chat, d_skip, wglu, bg)
        re, im = _lanes_to_state(fin[bsz:])
        sp_re.append(re)
        sp_im.append(im)
        s0 = _state_to_lanes(state_ssm_re[i], state_ssm_im[i])
        xs, snew = _s5_sample(xs, i, g_mix, bhat, lam, chat, d_skip, wglu, bg, s0)
        re, im = _lanes_to_state(snew)
        ss_re.append(re)
        ss_im.append(im)
        xp = mlp_p(xp, i)
        xs = _mlp(xs, g_mlp, w1, w2, i)

    cos_p, sin_p = _rope_tables(jnp.arange(seq, dtype=jnp.int32))
    cos_s, sin_s = _rope_tables(jnp.full((dec,), past, dtype=jnp.int32))
    ones_head = _block_ones(MXU_TILE, HEAD_DIM)
    place = _head_placement()
    rep = _row_replication(dec)
    kgain = row(k_norm)
    tile_gain = lambda gvec, n: jnp.tile(gvec, (1, n // HEAD_DIM))

    kd, vp, vt = _kv(xp.reshape(bsz * seq, D_MODEL), row(norm_kv), _dup_heads(w_k), w_v,
                     tile_gain(kgain, 2 * nkv), ones_head, cos_p, sin_p, lambda i: i % (seq // KV_ROWS), seq=seq)
    ks_new, vs_new = _kv(xs, row(norm_kv), w_k, w_v, tile_gain(kgain, nkv), ones_head, cos_s, sin_s,
                         lambda i: i)
    kd = kd.reshape(bsz, seq, 2 * nkv)
    new_k_p = kd[:, -WINDOW:].reshape(bsz, WINDOW, N_KV_HEADS, LANES // HEAD_DIM, HEAD_DIM)[:, :, :, 0]
    new_v_p = vp.reshape(bsz, seq, N_KV_HEADS, HEAD_DIM)[:, -WINDOW:]
    key_minor = lambda c: jnp.transpose(c.astype(F32), (0, 2, 3, 1)).reshape(dec, nkv, c.shape[1])
    keys_s, vals_s = _cache_append(key_minor(cache_k), key_minor(cache_v), ks_new.T, vs_new.T)
    key_major = lambda c: jnp.transpose(c.reshape(dec, N_KV_HEADS, HEAD_DIM, c.shape[2]), (0, 3, 1, 2))
    new_k_s, new_v_s = key_major(keys_s), key_major(vals_s)

    for j in range(depth - n_a):
        layer = n_a + j
        gn = row(norm_mix[layer])
        qgain = row(q_norm[j])
        wq = w_q[j].astype(BF16)
        wo = w_o[j].astype(BF16)
        xp = _attn_prompt(xp, gn, wq, tile_gain(qgain, D_MODEL), ones_head, cos_p, sin_p, kd, vt,
                          attn_sinks[j].astype(F32), wo)
        q_pad = _q_sample(xs, gn, wq, tile_gain(qgain, D_MODEL), ones_head, cos_s, sin_s, rep, place)
        o_pad = _attn_sample(q_pad.reshape(dec, N_HEADS, nkv), keys_s, vals_s,
                             attn_sinks[j].astype(F32).reshape(N_HEADS, 1))
        xs = _proj_residual(xs, o_pad.reshape(dec * N_HEADS, nkv), place.T, rep.T, wo)
        xp = mlp_p(xp, layer)
        xs = _mlp(xs, g_mlp, w1, w2, layer)

    return (xp, xs.reshape(dec, 1, D_MODEL),
            jnp.stack(sp_re), jnp.stack(sp_im), new_k_p, new_v_p,
            jnp.stack(ss_re), jnp.stack(ss_im), new_k_s, new_v_s)
```

```python
import functools
import math

import jax
import jax.numpy as jnp
from jax import lax
from jax.experimental import pallas as pl
from jax.experimental.pallas import tpu as pltpu

F32 = jnp.float32
BF16 = jnp.bfloat16

D_MODEL = 1024
N_GROUPS = 64
GROUP_SIZE = 16
STATE_DIM = 64
HEAD_DIM = 64
N_HEADS = 16
N_KV_HEADS = 4
WINDOW = 128
ROPE_THETA = 10000.0
D_FF = 4 * D_MODEL
EPS = 1e-6

SUBLANES = 8
LANES = 128
MXU_TILE = 256
VMEM_LIMIT_BYTES = 56 * 1024 * 1024

N_CH_BLOCKS = D_MODEL // MXU_TILE
STATE_BLOCK = (MXU_TILE // GROUP_SIZE) * STATE_DIM
STATE_LANES = 2 * STATE_BLOCK * N_CH_BLOCKS

S5_TIME_CHUNK = 128
MLP_ROWS = 1024
FF_CHUNK = 1024
KV_ROWS = 512
ATTN_BLOCKS = 4


def _const_spec(shape):
    zeros = (0,) * len(shape)
    return pl.BlockSpec(shape, lambda *_: zeros, pipeline_mode=pl.Buffered(1))


def _layer_spec(shape, layer):
    idx = (layer,) + (0,) * (len(shape) - 1)
    return pl.BlockSpec((None,) + tuple(shape[1:]), lambda *_: idx, pipeline_mode=pl.Buffered(1))


def _params(sem):
    return pltpu.CompilerParams(dimension_semantics=sem, vmem_limit_bytes=VMEM_LIMIT_BYTES)


def _rms(x, g):
    return x * lax.rsqrt(jnp.mean(x * x, axis=-1, keepdims=True) + EPS) * g


def _dot(a, b):
    return jnp.dot(a, b, preferred_element_type=F32)


def _seg_mean_sq(x, ones_ref):
    sq = x * x
    hi = sq.astype(BF16)
    lo = (sq - hi.astype(F32)).astype(BF16)
    ones = ones_ref[...]
    outs = []
    for t in range(x.shape[-1] // MXU_TILE):
        sl = slice(MXU_TILE * t, MXU_TILE * (t + 1))
        outs.append(_dot(hi[:, sl], ones) + _dot(lo[:, sl], ones))
    return jnp.concatenate(outs, axis=-1) * (1.0 / HEAD_DIM)


def _tile_lanes(t, n):
    return jnp.concatenate([t] * (n // t.shape[-1]), axis=-1)


def _rope(x, cos, sin_signed):
    n = x.shape[-1]
    lane = lax.broadcasted_iota(jnp.int32, x.shape, 1)
    first = (lane & (HEAD_DIM - 1)) < (HEAD_DIM // 2)
    partner = jnp.where(first, pltpu.roll(x, n - HEAD_DIM // 2, 1), pltpu.roll(x, HEAD_DIM // 2, 1))
    return x * cos + partner * sin_signed


def _mlp_kernel(x_ref, g_ref, w1_ref, w2_ref, o_ref):
    x = x_ref[...]
    h = _rms(x, g_ref[...]).astype(BF16)
    acc = x
    for c in range(D_FF // FF_CHUNK):
        sl = slice(FF_CHUNK * c, FF_CHUNK * (c + 1))
        a = jnp.square(jnp.maximum(_dot(h, w1_ref[:, sl]), 0.0)).astype(BF16)
        acc = acc + _dot(a, w2_ref[sl, :])
    o_ref[...] = acc


def _mlp(x, g, w1, w2, layer):
    rows = x.shape[0]
    tm = min(MLP_ROWS, rows)
    return pl.pallas_call(
        _mlp_kernel,
        grid=(rows // tm,),
        in_specs=[pl.BlockSpec((tm, D_MODEL), lambda i: (i, 0)),
                  _layer_spec(g.shape, layer),
                  _layer_spec(w1.shape, layer),
                  _layer_spec(w2.shape, layer)],
        out_specs=pl.BlockSpec((tm, D_MODEL), lambda i: (i, 0)),
        out_shape=jax.ShapeDtypeStruct((rows, D_MODEL), F32),
        compiler_params=_params(("parallel",)),
        name="mlp",
    )(x, g, w1, w2)


def _s5_gate(y, wglu_ref, bglu_ref):
    g = jax.nn.gelu(y).astype(BF16)
    z = _dot(g, wglu_ref[...]) + bglu_ref[...]
    return z[:, :D_MODEL] * jax.nn.sigmoid(z[:, D_MODEL:])


def _s5_prompt_kernel(xa_ref, xb_ref, gn_ref, bhat_ref, lam_ref, chat_ref, d_ref, wglu_ref, bglu_ref,
                      o_ref, fin_ref, st_ref, carry_ref, tb_ref, tb2_ref, y_ref):
    batch, tc, _ = xa_ref.shape
    assert 2 * batch == SUBLANES
    n_slabs = D_MODEL // LANES
    step = pl.program_id(0)
    live = step < pl.num_programs(0) - 1

    @pl.when(step == 0)
    def _():
        carry_ref[...] = jnp.zeros_like(carry_ref)
        y_ref[...] = jnp.zeros_like(y_ref)

    upd = _s5_gate(y_ref[...], wglu_ref, bglu_ref)
    for c in range(n_slabs):
        tb2_ref[c] = upd[:, LANES * c:LANES * (c + 1)]
    for b in range(batch):
        o_ref[b] = xb_ref[b] + jnp.concatenate(
            [tb2_ref[c, pl.ds(b, tc, stride=batch), :] for c in range(n_slabs)], axis=-1)

    for b in range(batch):
        ub_nat = _rms(xa_ref[b], gn_ref[...])
        for c in range(n_slabs):
            tb_ref[c, pl.ds(b, tc, stride=batch), :] = ub_nat[:, LANES * c:LANES * (c + 1)]
    u = jnp.concatenate([tb_ref[c] for c in range(n_slabs)], axis=-1)
    ub = u.astype(BF16)

    lo = lax.broadcasted_iota(jnp.int32, (SUBLANES, LANES), 0) < batch
    swap = lambda v: pltpu.roll(v, batch, 0)
    n_vregs = batch * tc // SUBLANES
    half_cols = STATE_BLOCK // LANES // 2
    ys = []
    for i in range(N_CH_BLOCKS):
        base = 2 * STATE_BLOCK * i
        st_ref[:, base:base + 2 * STATE_BLOCK] = _dot(ub[:, MXU_TILE * i:MXU_TILE * (i + 1)], bhat_ref[i])
        cols, lam_r, lam_i, sr, si, old = [], [], [], [], [], []
        for j in range(half_cols):
            ca, cb = base + LANES * j, base + LANES * (j + half_cols)
            cols.append((ca, cb))
            la, lb = STATE_BLOCK * i + LANES * j, STATE_BLOCK * i + LANES * (j + half_cols)
            bc = lambda r, c: jnp.broadcast_to(lam_ref[r:r + 1, c:c + LANES], (SUBLANES, LANES))
            lam_r.append(jnp.where(lo, bc(0, la), bc(0, lb)))
            lam_i.append(jnp.where(lo, bc(1, la), bc(1, lb)))
            prev = [carry_ref[:, c:c + LANES] for c in (ca, cb, ca + STATE_BLOCK, cb + STATE_BLOCK)]
            old.append(prev)
            sr.append(jnp.where(lo, swap(prev[0]), prev[1]))
            si.append(jnp.where(lo, swap(prev[2]), prev[3]))
        last = [None] * half_cols
        for m in range(n_vregs):
            rows = slice(SUBLANES * m, SUBLANES * (m + 1))
            for j in range(half_cols):
                ca, cb = cols[j]
                ar, br = st_ref[rows, ca:ca + LANES], st_ref[rows, cb:cb + LANES]
                ai, bi = (st_ref[rows, ca + STATE_BLOCK:ca + STATE_BLOCK + LANES],
                          st_ref[rows, cb + STATE_BLOCK:cb + STATE_BLOCK + LANES])
                d0r, d1r = jnp.where(lo, ar, swap(br)), jnp.where(lo, swap(ar), br)
                d0i, d1i = jnp.where(lo, ai, swap(bi)), jnp.where(lo, swap(ai), bi)
                s0r = lam_r[j] * sr[j] - lam_i[j] * si[j] + d0r
                s0i = lam_r[j] * si[j] + lam_i[j] * sr[j] + d0i
                s1r = lam_r[j] * s0r - lam_i[j] * s0i + d1r
                s1i = lam_r[j] * s0i + lam_i[j] * s0r + d1i
                sr[j], si[j] = s1r, s1i
                out = (jnp.where(lo, s0r, swap(s1r)), jnp.where(lo, swap(s0r), s1r),
                       jnp.where(lo, s0i, swap(s1i)), jnp.where(lo, swap(s0i), s1i))
                st_ref[rows, ca:ca + LANES] = out[0]
                st_ref[rows, cb:cb + LANES] = out[1]
                st_ref[rows, ca + STATE_BLOCK:ca + STATE_BLOCK + LANES] = out[2]
                st_ref[rows, cb + STATE_BLOCK:cb + STATE_BLOCK + LANES] = out[3]
                last[j] = out
        for j in range(half_cols):
            ca, cb = cols[j]
            for c, new, prev in zip((ca, cb, ca + STATE_BLOCK, cb + STATE_BLOCK), last[j], old[j]):
                carry_ref[:, c:c + LANES] = jnp.where(live, new, prev)
        ys.append(_dot(st_ref[:, base:base + 2 * STATE_BLOCK].astype(BF16), chat_ref[i]))
    u = jnp.concatenate([tb_ref[c] for c in range(n_slabs)], axis=-1)
    y_ref[...] = jnp.concatenate(ys, axis=-1) + d_ref[...] * u
    fin_ref[...] = carry_ref[...]


def _s5_prompt(x, layer, gn, bhat, lam, chat, d, wglu, bglu):
    batch, seq, _ = x.shape
    tc = S5_TIME_CHUNK
    n_chunks = seq // tc
    blk = (batch, tc, D_MODEL)
    this_chunk = pl.BlockSpec(blk, lambda s: (0, jnp.minimum(s, n_chunks - 1), 0))
    prev_chunk = pl.BlockSpec(blk, lambda s: (0, jnp.maximum(s - 1, 0), 0))
    slab = pltpu.VMEM((D_MODEL // LANES, batch * tc, LANES), F32)
    return pl.pallas_call(
        _s5_prompt_kernel,
        grid=(n_chunks + 1,),
        in_specs=[this_chunk, prev_chunk]
        + [_layer_spec(w.shape, layer) for w in (gn, bhat, lam, chat, d, wglu, bglu)],
        out_specs=[prev_chunk, pl.BlockSpec((SUBLANES, STATE_LANES), lambda s: (0, 0))],
        out_shape=[jax.ShapeDtypeStruct(x.shape, F32),
                   jax.ShapeDtypeStruct((SUBLANES, STATE_LANES), F32)],
        scratch_shapes=[pltpu.VMEM((batch * tc, STATE_LANES), F32),
                        pltpu.VMEM((SUBLANES, STATE_LANES), F32),
                        slab, slab,
                        pltpu.VMEM((batch * tc, D_MODEL), F32)],
        compiler_params=_params(("arbitrary",)),
        name="s5_prompt",
    )(x, x, gn, bhat, lam, chat, d, wglu, bglu)


def _s5_sample_kernel(x_ref, gn_ref, bhat_ref, lam_ref, chat_ref, d_ref, wglu_ref, bglu_ref, s0_ref,
                      o_ref, snew_ref):
    x = x_ref[...]
    u = _rms(x, gn_ref[...])
    ys = []
    for i in range(N_CH_BLOCKS):
        base = 2 * STATE_BLOCK * i
        bu = jnp.dot(u[:, MXU_TILE * i:MXU_TILE * (i + 1)], bhat_ref[i],
                     preferred_element_type=F32, precision=lax.Precision.HIGHEST)
        lr = lam_ref[0:1, STATE_BLOCK * i:STATE_BLOCK * (i + 1)]
        li = lam_ref[1:2, STATE_BLOCK * i:STATE_BLOCK * (i + 1)]
        s0r = s0_ref[:, base:base + STATE_BLOCK]
        s0i = s0_ref[:, base + STATE_BLOCK:base + 2 * STATE_BLOCK]
        sr = lr * s0r - li * s0i + bu[:, :STATE_BLOCK]
        si = lr * s0i + li * s0r + bu[:, STATE_BLOCK:]
        snew_ref[:, base:base + STATE_BLOCK] = sr
        snew_ref[:, base + STATE_BLOCK:base + 2 * STATE_BLOCK] = si
        sb = jnp.concatenate([sr, si], axis=-1).astype(BF16)
        ys.append(_dot(sb, chat_ref[i]))
    y = jnp.concatenate(ys, axis=-1)
    o_ref[...] = x + _s5_gate(y + d_ref[...] * u, wglu_ref, bglu_ref)


def _s5_sample(x, layer, gn, bhat32, lam, chat, d, wglu, bglu, s0):
    rows = x.shape[0]
    return pl.pallas_call(
        _s5_sample_kernel,
        grid=(1,),
        in_specs=[_const_spec(x.shape)] + [_layer_spec(w.shape, layer) for w in (gn, bhat32, lam, chat, d, wglu, bglu)]
        + [_const_spec(s0.shape)],
        out_specs=[pl.BlockSpec((rows, D_MODEL), lambda i: (0, 0)),
                   pl.BlockSpec((rows, STATE_LANES), lambda i: (0, 0))],
        out_shape=[jax.ShapeDtypeStruct((rows, D_MODEL), F32),
                   jax.ShapeDtypeStruct((rows, STATE_LANES), F32)],
        compiler_params=_params(("arbitrary",)),
        name="s5_sample",
    )(x, gn, bhat32, lam, chat, d, wglu, bglu, s0)


def _s5_weights(a_re, a_im, log_dt, b_re, b_im, c_re, c_im):
    n = a_re.shape[0]
    dt = jnp.exp(log_dt.astype(F32))
    lam = lax.complex(a_re.astype(F32), a_im.astype(F32))
    lam_bar = jnp.exp(lam * dt)
    b = lax.complex(b_re.astype(F32), b_im.astype(F32))
    b_bar = ((lam_bar - 1.0) / lam)[..., None] * b
    gpb = MXU_TILE // GROUP_SIZE
    hi = lax.Precision.HIGHEST

    def compact_in(w):
        return jnp.swapaxes(w.reshape(n, N_CH_BLOCKS, gpb, STATE_DIM, GROUP_SIZE), 3, 4).reshape(
            n, N_CH_BLOCKS, MXU_TILE, STATE_DIM)

    def compact_out(w):
        return jnp.swapaxes(w.reshape(n, N_CH_BLOCKS, gpb, GROUP_SIZE, STATE_DIM), 3, 4).reshape(
            n, N_CH_BLOCKS, STATE_BLOCK, GROUP_SIZE)

    lane = jnp.arange(2 * STATE_BLOCK)
    tile_in = (jnp.arange(2 * STATE_DIM)[:, None] == (lane // STATE_BLOCK) * STATE_DIM + lane % STATE_DIM)
    own_in = (jnp.arange(MXU_TILE)[:, None] // GROUP_SIZE == (lane % STATE_BLOCK) // STATE_DIM)
    bc = jnp.concatenate([compact_in(jnp.real(b_bar)), compact_in(jnp.imag(b_bar))], axis=-1)
    bhat = jnp.einsum("nirk,kl->nirl", bc, tile_in.astype(F32), precision=hi) * own_in.astype(F32)

    col = jnp.arange(MXU_TILE)
    tile_out = (jnp.arange(GROUP_SIZE)[:, None] == col % GROUP_SIZE)
    own_out = ((lane[:, None] % STATE_BLOCK) // STATE_DIM == col // GROUP_SIZE)
    cc = jnp.concatenate([compact_out(c_re.astype(F32)), -compact_out(c_im.astype(F32))], axis=2)
    chat = jnp.einsum("nirc,cl->nirl", cc, tile_out.astype(F32), precision=hi) * own_out.astype(F32)
    lam_rows = jnp.stack([jnp.real(lam_bar).reshape(n, -1), jnp.imag(lam_bar).reshape(n, -1)], axis=1)
    return bhat, chat.astype(BF16), lam_rows


def _state_to_lanes(s_re, s_im):
    n = s_re.shape[0]
    re = s_re.astype(F32).reshape(n, N_CH_BLOCKS, STATE_BLOCK)
    im = s_im.astype(F32).reshape(n, N_CH_BLOCKS, STATE_BLOCK)
    return jnp.concatenate([re, im], axis=-1).reshape(n, STATE_LANES)


def _lanes_to_state(s):
    n = s.shape[0]
    s4 = s.reshape(n, N_CH_BLOCKS, 2, STATE_BLOCK)
    return (s4[:, :, 0].reshape(n, N_GROUPS, STATE_DIM), s4[:, :, 1].reshape(n, N_GROUPS, STATE_DIM))


def _rope_partner(n):
    lane = jnp.arange(n)
    return jnp.where(lane % HEAD_DIM < HEAD_DIM // 2, lane + HEAD_DIM // 2, lane - HEAD_DIM // 2)


def _kv_kernel(x_ref, g_ref, wkv_ref, kg_ref, kgp_ref, ones_ref, cos_ref, sin_ref, *rest):
    if len(rest) == 4:
        wvt_ref, k_ref, v_ref, vt_ref = rest
    else:
        (k_ref, v_ref), wvt_ref, vt_ref = rest, None, None
    h = _rms(x_ref[...], g_ref[...]).astype(BF16)
    kv = _dot(h, wkv_ref[...])
    n = k_ref.shape[-1]
    k, k_partner = kv[:, :n], kv[:, n:2 * n]
    scale = lax.rsqrt(_seg_mean_sq(k, ones_ref) + EPS)
    k_ref[...] = scale * (k * (kg_ref[...] * _tile_lanes(cos_ref[...], n))
                          + k_partner * (kgp_ref[...] * _tile_lanes(sin_ref[...], n)))
    v_ref[...] = kv[:, 2 * n:]
    if vt_ref is not None:
        vt_ref[...] = lax.dot_general(wvt_ref[...], h, (((1,), (1,)), ((), ())), preferred_element_type=F32)


def _kv(x, g, wk, wv, kgain, ones, cos, sin, table_block, seq=None):
    rows = x.shape[0]
    n, nv = wk.shape[1], wv.shape[1]
    partner = _rope_partner(n)
    wkv = jnp.concatenate([wk, wk[:, partner], wv], axis=1).astype(BF16)
    tm = min(KV_ROWS, rows)
    in_specs = [pl.BlockSpec((tm, D_MODEL), lambda i: (i, 0)),
                _const_spec((1, D_MODEL)),
                _const_spec(wkv.shape),
                _const_spec((1, n)),
                _const_spec((1, n)),
                _const_spec(ones.shape),
                pl.BlockSpec((tm, LANES), lambda i: (table_block(i), 0)),
                pl.BlockSpec((tm, LANES), lambda i: (table_block(i), 0))]
    out_specs = [pl.BlockSpec((tm, n), lambda i: (i, 0)),
                 pl.BlockSpec((tm, nv), lambda i: (i, 0))]
    out_shape = [jax.ShapeDtypeStruct((rows, n), F32),
                 jax.ShapeDtypeStruct((rows, nv), F32)]
    args = [x, g, wkv, kgain, kgain[:, partner], ones, cos, sin]
    if seq is not None:
        per_seq = seq // tm
        in_specs.append(_const_spec((nv, D_MODEL)))
        args.append(wv.T.astype(BF16))
        out_specs.append(pl.BlockSpec((None, nv, tm), lambda i: (i // per_seq, 0, i % per_seq)))
        out_shape.append(jax.ShapeDtypeStruct((rows // seq, nv, seq), F32))
    return pl.pallas_call(
        _kv_kernel,
        grid=(rows // tm,),
        in_specs=in_specs,
        out_specs=out_specs,
        out_shape=out_shape,
        compiler_params=_params(("parallel",)),
        name="kv_proj",
    )(*args)


def _attn_prompt_kernel(sinks_ref, x_ref, g_ref, wq_ref, qg_ref, ones_ref, cos_ref, sin_ref,
                        kp_ref, kc_ref, vp_ref, vc_ref, wo_ref, o_ref):
    tile = pl.program_id(1)
    x = x_ref[...]
    n_blocks = x.shape[0] // WINDOW
    heads_per_kv = N_HEADS // N_KV_HEADS
    cols = heads_per_kv * WINDOW
    h = _rms(x, g_ref[...]).astype(BF16)
    q = _dot(h, wq_ref[...])
    q = q * lax.rsqrt(_seg_mean_sq(q, ones_ref) + EPS) * qg_ref[...]
    q = _rope(q, _tile_lanes(cos_ref[...], D_MODEL), _tile_lanes(sin_ref[...], D_MODEL))
    qb = (q * (HEAD_DIM ** -0.5)).astype(BF16)

    kall = jnp.concatenate([kp_ref[...], kc_ref[...]], axis=0).astype(BF16)
    vall = jnp.concatenate([vp_ref[...], vc_ref[...]], axis=1).astype(BF16)

    kj = lax.broadcasted_iota(jnp.int32, (WINDOW, cols), 0)
    col = lax.broadcasted_iota(jnp.int32, (WINDOW, cols), 1)
    from_prev = kj > (col & (WINDOW - 1))
    col_head = lax.broadcasted_iota(jnp.int32, (1, cols), 1) // WINDOW
    low_half = lax.broadcasted_iota(jnp.int32, (WINDOW, LANES), 1) < HEAD_DIM
    neg_inf = jnp.full((WINDOW, cols), -jnp.inf, F32)

    pairs = [(blk, g) for blk in range(n_blocks) for g in range(N_KV_HEADS)]
    scores, sinks = [], []
    for blk, g in pairs:
        r0 = WINDOW * blk
        kg = kall[r0:r0 + 2 * WINDOW, LANES * g:LANES * (g + 1)]
        qs = []
        sink = jnp.zeros((1, cols), F32)
        for hh in range(heads_per_kv):
            head = heads_per_kv * g + hh
            qp = qb[r0:r0 + WINDOW, LANES * (head // 2):LANES * (head // 2 + 1)]
            keep = low_half if head % 2 == 0 else jnp.logical_not(low_half)
            qs.append(jnp.where(keep, qp, jnp.zeros_like(qp)))
            sink = jnp.where(col_head == hh, sinks_ref[head], sink)
        sinks.append(sink)
        scores.append(lax.dot_general(kg, jnp.concatenate(qs, axis=0), (((1,), (1,)), ((), ())),
                                      preferred_element_type=F32))
    probs, denoms = [], []
    for (blk, g), s, sink in zip(pairs, scores, sinks):
        has_prev = tile * n_blocks + blk > 0
        t = jnp.where(from_prev, jnp.where(has_prev, s[:WINDOW], neg_inf), s[WINDOW:])
        m = jnp.maximum(jnp.max(t, axis=0, keepdims=True), sink)
        p = jnp.exp(t - m)
        denoms.append(jnp.sum(p, axis=0, keepdims=True) + jnp.exp(sink - m))
        zero = jnp.zeros_like(p)
        probs.append(jnp.concatenate([jnp.where(from_prev, p, zero), jnp.where(from_prev, zero, p)],
                                     axis=0).astype(BF16))
    head_outs = [[] for _ in range(n_blocks)]
    for (blk, g), pcat, denom in zip(pairs, probs, denoms):
        r0 = WINDOW * blk
        vg = vall[HEAD_DIM * g:HEAD_DIM * (g + 1), r0:r0 + 2 * WINDOW]
        o = _dot(vg, pcat) / denom
        head_outs[blk] += [o[:, WINDOW * hh:WINDOW * (hh + 1)] for hh in range(heads_per_kv)]
    block_outs = [jnp.concatenate(outs, axis=0).T for outs in head_outs]
    attn = jnp.concatenate(block_outs, axis=0).astype(BF16)
    o_ref[...] = x + _dot(attn, wo_ref[...])


def _attn_prompt(x, g, wq, qgain, ones, cos, sin, kd, vt, sinks, wo):
    bsz, seq, _ = x.shape
    nk, nv = kd.shape[-1], vt.shape[1]
    nb = ATTN_BLOCKS
    tq = nb * WINDOW
    x_spec = pl.BlockSpec((None, tq, D_MODEL), lambda b, i, *_: (b, i, 0))
    k_cur = pl.BlockSpec((None, tq, nk), lambda b, i, *_: (b, i, 0))
    k_prev = pl.BlockSpec((None, WINDOW, nk), lambda b, i, *_: (b, jnp.maximum(nb * i - 1, 0), 0))
    v_cur = pl.BlockSpec((None, nv, tq), lambda b, i, *_: (b, 0, i))
    v_prev = pl.BlockSpec((None, nv, WINDOW), lambda b, i, *_: (b, 0, jnp.maximum(nb * i - 1, 0)))
    tab = pl.BlockSpec((tq, LANES), lambda b, i, *_: (i, 0))
    grid_spec = pltpu.PrefetchScalarGridSpec(
        num_scalar_prefetch=1,
        grid=(bsz, seq // tq),
        in_specs=[x_spec,
                  _const_spec((1, D_MODEL)),
                  _const_spec(wq.shape),
                  _const_spec((1, D_MODEL)),
                  _const_spec(ones.shape),
                  tab, tab, k_prev, k_cur, v_prev, v_cur,
                  _const_spec(wo.shape)],
        out_specs=x_spec)
    return pl.pallas_call(
        _attn_prompt_kernel,
        grid_spec=grid_spec,
        out_shape=jax.ShapeDtypeStruct(x.shape, F32),
        compiler_params=_params(("parallel", "arbitrary")),
        name="attn_prompt",
    )(sinks, x, g, wq, qgain, ones, cos, sin, kd, kd, vt, vt, wo)


def _own_head(shape):
    row_head = lax.broadcasted_iota(jnp.int32, shape, 0) & (N_HEADS - 1)
    lane_head = lax.broadcasted_iota(jnp.int32, shape, 1) // HEAD_DIM
    return row_head == lane_head


def _q_sample_kernel(x_ref, g_ref, wq_ref, qg_ref, ones_ref, cos_ref, sin_ref, rep_ref, place_ref, q_ref):
    h = _rms(x_ref[...], g_ref[...]).astype(BF16)
    q = _dot(h, wq_ref[...])
    q = q * lax.rsqrt(_seg_mean_sq(q, ones_ref) + EPS) * qg_ref[...]
    q = _rope(q, _tile_lanes(cos_ref[...], D_MODEL), _tile_lanes(sin_ref[...], D_MODEL))
    qb = (q * (HEAD_DIM ** -0.5)).astype(BF16)
    q_rep = _dot(rep_ref[...], qb)
    q_own = jnp.where(_own_head(q_rep.shape), q_rep, 0.0).astype(BF16)
    q_ref[...] = _dot(q_own, place_ref[...])


def _q_sample(x, g, wq, qgain, ones, cos, sin, rep, place):
    rows = x.shape[0]
    return pl.pallas_call(
        _q_sample_kernel,
        out_shape=jax.ShapeDtypeStruct((rows * N_HEADS, place.shape[1]), F32),
        compiler_params=pltpu.CompilerParams(vmem_limit_bytes=VMEM_LIMIT_BYTES),
        name="q_sample",
    )(x, g, wq, qgain, ones, cos, sin, rep, place)


def _cache_append_kernel(k_ref, v_ref, kn_ref, vn_ref, ko_ref, vo_ref):
    nb, lanes_kv, keys = k_ref.shape
    first = pl.program_id(0) * nb
    key = lax.broadcasted_iota(jnp.int32, (lanes_kv, keys), 1)
    seq = lax.broadcasted_iota(jnp.int32, (kn_ref.shape[1], keys), 0)

    def split3(x):
        hi = x.astype(BF16)
        r = x - hi.astype(F32)
        mid = r.astype(BF16)
        return hi, mid, (r - mid.astype(F32)).astype(BF16)

    parts = [(split3(kn_ref[...]), k_ref, ko_ref), (split3(vn_ref[...]), v_ref, vo_ref)]
    for b in range(nb):
        pick = (seq == first + b).astype(BF16)
        for (hi, mid, lo), old_ref, out_ref in parts:
            new_col = _dot(hi, pick) + _dot(mid, pick) + _dot(lo, pick)
            out_ref[b] = jnp.where(key == keys - 1, new_col, pltpu.roll(old_ref[b], keys - 1, 1))


def _cache_append(kt, vt, kn_t, vn_t):
    bsz, lanes_kv, keys = kt.shape
    nb = SUBLANES
    blk = pl.BlockSpec((nb, lanes_kv, keys), lambda i: (i, 0, 0))
    return pl.pallas_call(
        _cache_append_kernel,
        grid=(bsz // nb,),
        in_specs=[blk, blk, _const_spec(kn_t.shape), _const_spec(vn_t.shape)],
        out_specs=[blk, blk],
        out_shape=[jax.ShapeDtypeStruct(kt.shape, F32), jax.ShapeDtypeStruct(vt.shape, F32)],
        compiler_params=_params(("parallel",)),
        name="cache_append",
    )(kt, vt, kn_t, vn_t)


def _attn_sample_kernel(q_ref, k_ref, v_ref, sink_ref, o_ref):
    sink = sink_ref[...]
    for b in range(q_ref.shape[0]):
        s = _dot(q_ref[b].astype(BF16), k_ref[b].astype(BF16))
        m = jnp.maximum(jnp.max(s, axis=-1, keepdims=True), sink)
        p = jnp.exp(s - m)
        denom = jnp.sum(p, axis=-1, keepdims=True) + jnp.exp(sink - m)
        o = lax.dot_general(p.astype(BF16), v_ref[b].astype(BF16), (((1,), (1,)), ((), ())),
                            preferred_element_type=F32)
        o_ref[b] = o / denom


def _attn_sample(q3, kt, vt, sink_col):
    bsz, nh, nkv = q3.shape
    keys = kt.shape[2]
    bb = SUBLANES
    return pl.pallas_call(
        _attn_sample_kernel,
        grid=(bsz // bb,),
        in_specs=[pl.BlockSpec((bb, nh, nkv), lambda i: (i, 0, 0)),
                  pl.BlockSpec((bb, nkv, keys), lambda i: (i, 0, 0)),
                  pl.BlockSpec((bb, nkv, keys), lambda i: (i, 0, 0)),
                  _const_spec(sink_col.shape)],
        out_specs=pl.BlockSpec((bb, nh, nkv), lambda i: (i, 0, 0)),
        out_shape=jax.ShapeDtypeStruct((bsz, nh, nkv), F32),
        compiler_params=_params(("parallel",)),
        name="attn_sample",
    )(q3, kt, vt, sink_col)


def _proj_residual_kernel(x_ref, a_ref, place_t_ref, rep_t_ref, w_ref, o_ref):
    z = _dot(a_ref[...].astype(BF16), place_t_ref[...])
    z_own = jnp.where(_own_head(z.shape), z, 0.0).astype(BF16)
    attn = _dot(rep_t_ref[...], z_own).astype(BF16)
    o_ref[...] = x_ref[...] + _dot(attn, w_ref[...])


def _proj_residual(x, a, place_t, rep_t, w):
    return pl.pallas_call(
        _proj_residual_kernel,
        out_shape=jax.ShapeDtypeStruct(x.shape, F32),
        compiler_params=pltpu.CompilerParams(vmem_limit_bytes=VMEM_LIMIT_BYTES),
        name="o_proj_sample",
    )(x, a, place_t, rep_t, w)


def _rope_tables(pos):
    half = HEAD_DIM // 2
    inv = ROPE_THETA ** (-jnp.arange(half, dtype=F32) / half)
    ang = pos.astype(F32)[:, None] * inv[None, :]
    cos, sin = jnp.cos(ang), jnp.sin(ang)
    reps = LANES // HEAD_DIM
    return (jnp.tile(jnp.concatenate([cos, cos], axis=-1), (1, reps)),
            jnp.tile(jnp.concatenate([-sin, sin], axis=-1), (1, reps)))


def _block_ones(width, block):
    idx = jnp.arange(width) // block
    return (idx[:, None] == idx[None, :]).astype(BF16)


def _dup_heads(w):
    w3 = w.reshape(w.shape[0], N_KV_HEADS, 1, HEAD_DIM)
    return jnp.broadcast_to(w3, (w.shape[0], N_KV_HEADS, LANES // HEAD_DIM, HEAD_DIM)).reshape(w.shape[0], -1)


def _head_placement():
    lane = jnp.arange(N_HEADS * HEAD_DIM)
    dst = (lane // HEAD_DIM) // (N_HEADS // N_KV_HEADS) * HEAD_DIM + lane % HEAD_DIM
    return (dst[:, None] == jnp.arange(N_KV_HEADS * HEAD_DIM)[None, :]).astype(BF16)


def _row_replication(n):
    return (jnp.arange(n * N_HEADS)[:, None] // N_HEADS == jnp.arange(n)[None, :]).astype(BF16)


def kernel(x_prompt, x_sample, state_ssm_re, state_ssm_im, cache_k, cache_v, norm_mix, norm_mlp, ssm_a_re, ssm_a_im, ssm_log_dt, ssm_b_re, ssm_b_im, ssm_c_re, ssm_c_im, ssm_d, w_glu, b_glu, norm_kv, w_k, w_v, k_norm, w_q, q_norm, attn_sinks, w_o, w_mlp_in, w_mlp_out):
    bsz, seq, _ = x_prompt.shape
    dec = x_sample.shape[0]
    n_a = ssm_a_re.shape[0]
    depth = norm_mix.shape[0]
    past = 8192
    nkv = N_KV_HEADS * HEAD_DIM
    row = lambda v: v.astype(F32).reshape(1, -1)

    xp = x_prompt.astype(F32)
    xs = x_sample.reshape(dec, D_MODEL)
    rows3 = lambda v: v.astype(F32).reshape(v.shape[0], 1, -1)

    w1 = w_mlp_in.astype(BF16)
    w2 = w_mlp_out.astype(BF16)
    g_mlp = rows3(norm_mlp)
    mlp_p = lambda x, layer: _mlp(x.reshape(bsz * seq, D_MODEL), g_mlp, w1, w2, layer).reshape(bsz, seq, D_MODEL)

    bhat, chat, lam = _s5_weights(ssm_a_re, ssm_a_im, ssm_log_dt, ssm_b_re, ssm_b_im, ssm_c_re, ssm_c_im)
    bhat16 = bhat.astype(BF16)
    wglu = w_glu.astype(BF16)
    g_mix, d_skip, bg = rows3(norm_mix), rows3(ssm_d), rows3(b_glu)
    sp_re, sp_im, ss_re, ss_im = [], [], [], []
    for i in range(n_a):
        xp, fin = _s5_prompt(xp, i, g_mix, bhat16, lam, chat, d_skip, wglu, bg)
        re, im = _lanes_to_state(fin[bsz:])
        sp_re.append(re)
        sp_im.append(im)
        s0 = _state_to_lanes(state_ssm_re[i], state_ssm_im[i])
        xs, snew = _s5_sample(xs, i, g_mix, bhat, lam, chat, d_skip, wglu, bg, s0)
        re, im = _lanes_to_state(snew)
        ss_re.append(re)
        ss_im.append(im)
        xp = mlp_p(xp, i)
        xs = _mlp(xs, g_mlp, w1, w2, i)

    cos_p, sin_p = _rope_tables(jnp.arange(seq, dtype=jnp.int32))
    cos_s, sin_s = _rope_tables(jnp.full((dec,), past, dtype=jnp.int32))
    ones_head = _block_ones(MXU_TILE, HEAD_DIM)
    place = _head_placement()
    rep = _row_replication(dec)
    kgain = row(k_norm)
    tile_gain = lambda gvec, n: jnp.tile(gvec, (1, n // HEAD_DIM))

    kd, vp, vt = _kv(xp.reshape(bsz * seq, D_MODEL), row(norm_kv), _dup_heads(w_k), w_v,
                     tile_gain(kgain, 2 * nkv), ones_head, cos_p, sin_p, lambda i: i % (seq // KV_ROWS), seq=seq)
    ks_new, vs_new = _kv(xs, row(norm_kv), w_k, w_v, tile_gain(kgain, nkv), ones_head, cos_s, sin_s,
                         lambda i: i)
    kd = kd.reshape(bsz, seq, 2 * nkv)
    new_k_p = kd[:, -WINDOW:].reshape(bsz, WINDOW, N_KV_HEADS, LANES // HEAD_DIM, HEAD_DIM)[:, :, :, 0]
    new_v_p = vp.reshape(bsz, seq, N_KV_HEADS, HEAD_DIM)[:, -WINDOW:]
    key_minor = lambda c: jnp.transpose(c.astype(F32), (0, 2, 3, 1)).reshape(dec, nkv, c.shape[1])
    keys_s, vals_s = _cache_append(key_minor(cache_k), key_minor(cache_v), ks_new.T, vs_new.T)
    key_major = lambda c: jnp.transpose(c.reshape(dec, N_KV_HEADS, HEAD_DIM, c.shape[2]), (0, 3, 1, 2))
    new_k_s, new_v_s = key_major(keys_s), key_major(vals_s)

    for j in range(depth - n_a):
        layer = n_a + j
        gn = row(norm_mix[layer])
        qgain = row(q_norm[j])
        wq = w_q[j].astype(BF16)
        wo = w_o[j].astype(BF16)
        xp = _attn_prompt(xp, gn, wq, tile_gain(qgain, D_MODEL), ones_head, cos_p, sin_p, kd, vt,
                          attn_sinks[j].astype(F32), wo)
        q_pad = _q_sample(xs, gn, wq, tile_gain(qgain, D_MODEL), ones_head, cos_s, sin_s, rep, place)
        o_pad = _attn_sample(q_pad.reshape(dec, N_HEADS, nkv), keys_s, vals_s,
                             attn_sinks[j].astype(F32).reshape(N_HEADS, 1))
        xs = _proj_residual(xs, o_pad.reshape(dec * N_HEADS, nkv), place.T, rep.T, wo)
        xp = mlp_p(xp, layer)
        xs = _mlp(xs, g_mlp, w1, w2, layer)

    return (xp, xs.reshape(dec, 1, D_MODEL),
            jnp.stack(sp_re), jnp.stack(sp_im), new_k_p, new_v_p,
            jnp.stack(ss_re), jnp.stack(ss_im), new_k_s, new_v_s)
```

```python
import functools
import math

import jax
import jax.numpy as jnp
from jax import lax
from jax.experimental import pallas as pl
from jax.experimental.pallas import tpu as pltpu

F32 = jnp.float32
BF16 = jnp.bfloat16

D_MODEL = 1024
N_GROUPS = 64
GROUP_SIZE = 16
STATE_DIM = 64
HEAD_DIM = 64
N_HEADS = 16
N_KV_HEADS = 4
WINDOW = 128
ROPE_THETA = 10000.0
D_FF = 4 * D_MODEL
EPS = 1e-6

SUBLANES = 8
LANES = 128
MXU_TILE = 256
VMEM_LIMIT_BYTES = 56 * 1024 * 1024

N_CH_BLOCKS = D_MODEL // MXU_TILE
STATE_BLOCK = (MXU_TILE // GROUP_SIZE) * STATE_DIM
STATE_LANES = 2 * STATE_BLOCK * N_CH_BLOCKS

S5_TIME_CHUNK = 128
MLP_ROWS = 1024
FF_CHUNK = 1024
KV_ROWS = 1024
ATTN_BLOCKS = 8


def _const_spec(shape):
    zeros = (0,) * len(shape)
    return pl.BlockSpec(shape, lambda *_: zeros, pipeline_mode=pl.Buffered(1))


def _layer_spec(shape, layer):
    idx = (layer,) + (0,) * (len(shape) - 1)
    return pl.BlockSpec((None,) + tuple(shape[1:]), lambda *_: idx, pipeline_mode=pl.Buffered(1))


def _params(sem):
    return pltpu.CompilerParams(dimension_semantics=sem, vmem_limit_bytes=VMEM_LIMIT_BYTES)


def _rms(x, g):
    return x * lax.rsqrt(jnp.mean(x * x, axis=-1, keepdims=True) + EPS) * g


def _dot(a, b):
    return jnp.dot(a, b, preferred_element_type=F32)


def _seg_mean_sq(x, ones_ref):
    sq = x * x
    hi = sq.astype(BF16)
    lo = (sq - hi.astype(F32)).astype(BF16)
    ones = ones_ref[...]
    outs = []
    for t in range(x.shape[-1] // MXU_TILE):
        sl = slice(MXU_TILE * t, MXU_TILE * (t + 1))
        outs.append(_dot(hi[:, sl], ones) + _dot(lo[:, sl], ones))
    return jnp.concatenate(outs, axis=-1) * (1.0 / HEAD_DIM)


def _tile_lanes(t, n):
    return jnp.concatenate([t] * (n // t.shape[-1]), axis=-1)


def _rope(x, cos, sin_signed):
    n = x.shape[-1]
    lane = lax.broadcasted_iota(jnp.int32, x.shape, 1)
    first = (lane & (HEAD_DIM - 1)) < (HEAD_DIM // 2)
    partner = jnp.where(first, pltpu.roll(x, n - HEAD_DIM // 2, 1), pltpu.roll(x, HEAD_DIM // 2, 1))
    return x * cos + partner * sin_signed


def _mlp_kernel(x_ref, g_ref, w1_ref, w2_ref, o_ref):
    x = x_ref[...]
    h = _rms(x, g_ref[...]).astype(BF16)
    acc = x
    for c in range(D_FF // FF_CHUNK):
        sl = slice(FF_CHUNK * c, FF_CHUNK * (c + 1))
        a = jnp.square(jnp.maximum(_dot(h, w1_ref[:, sl]), 0.0)).astype(BF16)
        acc = acc + _dot(a, w2_ref[sl, :])
    o_ref[...] = acc


def _mlp(x, g, w1, w2, layer):
    rows = x.shape[0]
    tm = min(MLP_ROWS, rows)
    return pl.pallas_call(
        _mlp_kernel,
        grid=(rows // tm,),
        in_specs=[pl.BlockSpec((tm, D_MODEL), lambda i: (i, 0)),
                  _layer_spec(g.shape, layer),
                  _layer_spec(w1.shape, layer),
                  _layer_spec(w2.shape, layer)],
        out_specs=pl.BlockSpec((tm, D_MODEL), lambda i: (i, 0)),
        out_shape=jax.ShapeDtypeStruct((rows, D_MODEL), F32),
        compiler_params=_params(("parallel",)),
        name="mlp",
    )(x, g, w1, w2)


def _s5_gate(y, wglu_ref, bglu_ref):
    g = jax.nn.gelu(y).astype(BF16)
    z = _dot(g, wglu_ref[...]) + bglu_ref[...]
    return z[:, :D_MODEL] * jax.nn.sigmoid(z[:, D_MODEL:])


def _s5_prompt_kernel(xa_ref, xb_ref, gn_ref, bhat_ref, lam_ref, chat_ref, d_ref, wglu_ref, bglu_ref,
                      o_ref, fin_ref, st_ref, carry_ref, tb_ref, tb2_ref, y_ref):
    batch, tc, _ = xa_ref.shape
    assert 2 * batch == SUBLANES
    n_slabs = D_MODEL // LANES
    step = pl.program_id(0)
    live = step < pl.num_programs(0) - 1

    @pl.when(step == 0)
    def _():
        carry_ref[...] = jnp.zeros_like(carry_ref)
        y_ref[...] = jnp.zeros_like(y_ref)

    upd = _s5_gate(y_ref[...], wglu_ref, bglu_ref)
    for c in range(n_slabs):
        tb2_ref[c] = upd[:, LANES * c:LANES * (c + 1)]
    for b in range(batch):
        o_ref[b] = xb_ref[b] + jnp.concatenate(
            [tb2_ref[c, pl.ds(b, tc, stride=batch), :] for c in range(n_slabs)], axis=-1)

    for b in range(batch):
        ub_nat = _rms(xa_ref[b], gn_ref[...])
        for c in range(n_slabs):
            tb_ref[c, pl.ds(b, tc, stride=batch), :] = ub_nat[:, LANES * c:LANES * (c + 1)]
    u = jnp.concatenate([tb_ref[c] for c in range(n_slabs)], axis=-1)
    ub = u.astype(BF16)

    lo = lax.broadcasted_iota(jnp.int32, (SUBLANES, LANES), 0) < batch
    swap = lambda v: pltpu.roll(v, batch, 0)
    n_vregs = batch * tc // SUBLANES
    half_cols = STATE_BLOCK // LANES // 2
    ys = []
    for i in range(N_CH_BLOCKS):
        base = 2 * STATE_BLOCK * i
        st_ref[:, base:base + 2 * STATE_BLOCK] = _dot(ub[:, MXU_TILE * i:MXU_TILE * (i + 1)], bhat_ref[i])
        cols, lam_r, lam_i, sr, si, old = [], [], [], [], [], []
        for j in range(half_cols):
            ca, cb = base + LANES * j, base + LANES * (j + half_cols)
            cols.append((ca, cb))
            la, lb = STATE_BLOCK * i + LANES * j, STATE_BLOCK * i + LANES * (j + half_cols)
            bc = lambda r, c: jnp.broadcast_to(lam_ref[r:r + 1, c:c + LANES], (SUBLANES, LANES))
            lam_r.append(jnp.where(lo, bc(0, la), bc(0, lb)))
            lam_i.append(jnp.where(lo, bc(1, la), bc(1, lb)))
            prev = [carry_ref[:, c:c + LANES] for c in (ca, cb, ca + STATE_BLOCK, cb + STATE_BLOCK)]
            old.append(prev)
            sr.append(jnp.where(lo, swap(prev[0]), prev[1]))
            si.append(jnp.where(lo, swap(prev[2]), prev[3]))
        last = [None] * half_cols
        for m in range(n_vregs):
            rows = slice(SUBLANES * m, SUBLANES * (m + 1))
            for j in range(half_cols):
                ca, cb = cols[j]
                ar, br = st_ref[rows, ca:ca + LANES], st_ref[rows, cb:cb + LANES]
                ai, bi = (st_ref[rows, ca + STATE_BLOCK:ca + STATE_BLOCK + LANES],
                          st_ref[rows, cb + STATE_BLOCK:cb + STATE_BLOCK + LANES])
                d0r, d1r = jnp.where(lo, ar, swap(br)), jnp.where(lo, swap(ar), br)
                d0i, d1i = jnp.where(lo, ai, swap(bi)), jnp.where(lo, swap(ai), bi)
                s0r = lam_r[j] * sr[j] - lam_i[j] * si[j] + d0r
                s0i = lam_r[j] * si[j] + lam_i[j] * sr[j] + d0i
                s1r = lam_r[j] * s0r - lam_i[j] * s0i + d1r
                s1i = lam_r[j] * s0i + lam_i[j] * s0r + d1i
                sr[j], si[j] = s1r, s1i
                out = (jnp.where(lo, s0r, swap(s1r)), jnp.where(lo, swap(s0r), s1r),
                       jnp.where(lo, s0i, swap(s1i)), jnp.where(lo, swap(s0i), s1i))
                st_ref[rows, ca:ca + LANES] = out[0]
                st_ref[rows, cb:cb + LANES] = out[1]
                st_ref[rows, ca + STATE_BLOCK:ca + STATE_BLOCK + LANES] = out[2]
                st_ref[rows, cb + STATE_BLOCK:cb + STATE_BLOCK + LANES] = out[3]
                last[j] = out
        for j in range(half_cols):
            ca, cb = cols[j]
            for c, new, prev in zip((ca, cb, ca + STATE_BLOCK, cb + STATE_BLOCK), last[j], old[j]):
                carry_ref[:, c:c + LANES] = jnp.where(live, new, prev)
        ys.append(_dot(st_ref[:, base:base + 2 * STATE_BLOCK].astype(BF16), chat_ref[i]))
    u = jnp.concatenate([tb_ref[c] for c in range(n_slabs)], axis=-1)
    y_ref[...] = jnp.concatenate(ys, axis=-1) + d_ref[...] * u
    fin_ref[...] = carry_ref[...]


def _s5_prompt(x, layer, gn, bhat, lam, chat, d, wglu, bglu):
    batch, seq, _ = x.shape
    tc = S5_TIME_CHUNK
    n_chunks = seq // tc
    blk = (batch, tc, D_MODEL)
    this_chunk = pl.BlockSpec(blk, lambda s: (0, jnp.minimum(s, n_chunks - 1), 0))
    prev_chunk = pl.BlockSpec(blk, lambda s: (0, jnp.maximum(s - 1, 0), 0))
    slab = pltpu.VMEM((D_MODEL // LANES, batch * tc, LANES), F32)
    return pl.pallas_call(
        _s5_prompt_kernel,
        grid=(n_chunks + 1,),
        in_specs=[this_chunk, prev_chunk]
        + [_layer_spec(w.shape, layer) for w in (gn, bhat, lam, chat, d, wglu, bglu)],
        out_specs=[prev_chunk, pl.BlockSpec((SUBLANES, STATE_LANES), lambda s: (0, 0))],
        out_shape=[jax.ShapeDtypeStruct(x.shape, F32),
                   jax.ShapeDtypeStruct((SUBLANES, STATE_LANES), F32)],
        scratch_shapes=[pltpu.VMEM((batch * tc, STATE_LANES), F32),
                        pltpu.VMEM((SUBLANES, STATE_LANES), F32),
                        slab, slab,
                        pltpu.VMEM((batch * tc, D_MODEL), F32)],
        compiler_params=_params(("arbitrary",)),
        name="s5_prompt",
    )(x, x, gn, bhat, lam, chat, d, wglu, bglu)


def _s5_sample_kernel(x_ref, gn_ref, bhat_ref, lam_ref, chat_ref, d_ref, wglu_ref, bglu_ref, s0_ref,
                      o_ref, snew_ref):
    x = x_ref[...]
    u = _rms(x, gn_ref[...])
    ys = []
    for i in range(N_CH_BLOCKS):
        base = 2 * STATE_BLOCK * i
        bu = jnp.dot(u[:, MXU_TILE * i:MXU_TILE * (i + 1)], bhat_ref[i],
                     preferred_element_type=F32, precision=lax.Precision.HIGHEST)
        lr = lam_ref[0:1, STATE_BLOCK * i:STATE_BLOCK * (i + 1)]
        li = lam_ref[1:2, STATE_BLOCK * i:STATE_BLOCK * (i + 1)]
        s0r = s0_ref[:, base:base + STATE_BLOCK]
        s0i = s0_ref[:, base + STATE_BLOCK:base + 2 * STATE_BLOCK]
        sr = lr * s0r - li * s0i + bu[:, :STATE_BLOCK]
        si = lr * s0i + li * s0r + bu[:, STATE_BLOCK:]
        snew_ref[:, base:base + STATE_BLOCK] = sr
        snew_ref[:, base + STATE_BLOCK:base + 2 * STATE_BLOCK] = si
        sb = jnp.concatenate([sr, si], axis=-1).astype(BF16)
        ys.append(_dot(sb, chat_ref[i]))
    y = jnp.concatenate(ys, axis=-1)
    o_ref[...] = x + _s5_gate(y + d_ref[...] * u, wglu_ref, bglu_ref)


def _s5_sample(x, layer, gn, bhat32, lam, chat, d, wglu, bglu, s0):
    rows = x.shape[0]
    return pl.pallas_call(
        _s5_sample_kernel,
        grid=(1,),
        in_specs=[_const_spec(x.shape)] + [_layer_spec(w.shape, layer) for w in (gn, bhat32, lam, chat, d, wglu, bglu)]
        + [_const_spec(s0.shape)],
        out_specs=[pl.BlockSpec((rows, D_MODEL), lambda i: (0, 0)),
                   pl.BlockSpec((rows, STATE_LANES), lambda i: (0, 0))],
        out_shape=[jax.ShapeDtypeStruct((rows, D_MODEL), F32),
                   jax.ShapeDtypeStruct((rows, STATE_LANES), F32)],
        compiler_params=_params(("arbitrary",)),
        name="s5_sample",
    )(x, gn, bhat32, lam, chat, d, wglu, bglu, s0)


def _s5_weights(a_re, a_im, log_dt, b_re, b_im, c_re, c_im):
    n = a_re.shape[0]
    dt = jnp.exp(log_dt.astype(F32))
    lam = lax.complex(a_re.astype(F32), a_im.astype(F32))
    lam_bar = jnp.exp(lam * dt)
    b = lax.complex(b_re.astype(F32), b_im.astype(F32))
    b_bar = ((lam_bar - 1.0) / lam)[..., None] * b
    gpb = MXU_TILE // GROUP_SIZE
    hi = lax.Precision.HIGHEST

    def compact_in(w):
        return jnp.swapaxes(w.reshape(n, N_CH_BLOCKS, gpb, STATE_DIM, GROUP_SIZE), 3, 4).reshape(
            n, N_CH_BLOCKS, MXU_TILE, STATE_DIM)

    def compact_out(w):
        return jnp.swapaxes(w.reshape(n, N_CH_BLOCKS, gpb, GROUP_SIZE, STATE_DIM), 3, 4).reshape(
            n, N_CH_BLOCKS, STATE_BLOCK, GROUP_SIZE)

    lane = jnp.arange(2 * STATE_BLOCK)
    tile_in = (jnp.arange(2 * STATE_DIM)[:, None] == (lane // STATE_BLOCK) * STATE_DIM + lane % STATE_DIM)
    own_in = (jnp.arange(MXU_TILE)[:, None] // GROUP_SIZE == (lane % STATE_BLOCK) // STATE_DIM)
    bc = jnp.concatenate([compact_in(jnp.real(b_bar)), compact_in(jnp.imag(b_bar))], axis=-1)
    bhat = jnp.einsum("nirk,kl->nirl", bc, tile_in.astype(F32), precision=hi) * own_in.astype(F32)

    col = jnp.arange(MXU_TILE)
    tile_out = (jnp.arange(GROUP_SIZE)[:, None] == col % GROUP_SIZE)
    own_out = ((lane[:, None] % STATE_BLOCK) // STATE_DIM == col // GROUP_SIZE)
    cc = jnp.concatenate([compact_out(c_re.astype(F32)), -compact_out(c_im.astype(F32))], axis=2)
    chat = jnp.einsum("nirc,cl->nirl", cc, tile_out.astype(F32), precision=hi) * own_out.astype(F32)
    lam_rows = jnp.stack([jnp.real(lam_bar).reshape(n, -1), jnp.imag(lam_bar).reshape(n, -1)], axis=1)
    return bhat, chat.astype(BF16), lam_rows


def _state_to_lanes(s_re, s_im):
    n = s_re.shape[0]
    re = s_re.astype(F32).reshape(n, N_CH_BLOCKS, STATE_BLOCK)
    im = s_im.astype(F32).reshape(n, N_CH_BLOCKS, STATE_BLOCK)
    return jnp.concatenate([re, im], axis=-1).reshape(n, STATE_LANES)


def _lanes_to_state(s):
    n = s.shape[0]
    s4 = s.reshape(n, N_CH_BLOCKS, 2, STATE_BLOCK)
    return (s4[:, :, 0].reshape(n, N_GROUPS, STATE_DIM), s4[:, :, 1].reshape(n, N_GROUPS, STATE_DIM))


def _rope_partner(n):
    lane = jnp.arange(n)
    return jnp.where(lane % HEAD_DIM < HEAD_DIM // 2, lane + HEAD_DIM // 2, lane - HEAD_DIM // 2)


def _kv_kernel(x_ref, g_ref, wkv_ref, kg_ref, kgp_ref, ones_ref, cos_ref, sin_ref, *rest):
    if len(rest) == 4:
        wvt_ref, k_ref, v_ref, vt_ref = rest
    else:
        (k_ref, v_ref), wvt_ref, vt_ref = rest, None, None
    h = _rms(x_ref[...], g_ref[...]).astype(BF16)
    kv = _dot(h, wkv_ref[...])
    n = k_ref.shape[-1]
    k, k_partner = kv[:, :n], kv[:, n:2 * n]
    scale = lax.rsqrt(_seg_mean_sq(k, ones_ref) + EPS)
    k_ref[...] = scale * (k * (kg_ref[...] * _tile_lanes(cos_ref[...], n))
                          + k_partner * (kgp_ref[...] * _tile_lanes(sin_ref[...], n)))
    v_ref[...] = kv[:, 2 * n:]
    if vt_ref is not None:
        vt_ref[...] = lax.dot_general(wvt_ref[...], h, (((1,), (1,)), ((), ())), preferred_element_type=F32)


def _kv(x, g, wk, wv, kgain, ones, cos, sin, table_block, seq=None):
    rows = x.shape[0]
    n, nv = wk.shape[1], wv.shape[1]
    partner = _rope_partner(n)
    wkv = jnp.concatenate([wk, wk[:, partner], wv], axis=1).astype(BF16)
    tm = min(KV_ROWS, rows)
    in_specs = [pl.BlockSpec((tm, D_MODEL), lambda i: (i, 0)),
                _const_spec((1, D_MODEL)),
                _const_spec(wkv.shape),
                _const_spec((1, n)),
                _const_spec((1, n)),
                _const_spec(ones.shape),
                pl.BlockSpec((tm, LANES), lambda i: (table_block(i), 0)),
                pl.BlockSpec((tm, LANES), lambda i: (table_block(i), 0))]
    out_specs = [pl.BlockSpec((tm, n), lambda i: (i, 0)),
                 pl.BlockSpec((tm, nv), lambda i: (i, 0))]
    out_shape = [jax.ShapeDtypeStruct((rows, n), F32),
                 jax.ShapeDtypeStruct((rows, nv), F32)]
    args = [x, g, wkv, kgain, kgain[:, partner], ones, cos, sin]
    if seq is not None:
        per_seq = seq // tm
        in_specs.append(_const_spec((nv, D_MODEL)))
        args.append(wv.T.astype(BF16))
        out_specs.append(pl.BlockSpec((None, nv, tm), lambda i: (i // per_seq, 0, i % per_seq)))
        out_shape.append(jax.ShapeDtypeStruct((rows // seq, nv, seq), F32))
    return pl.pallas_call(
        _kv_kernel,
        grid=(rows // tm,),
        in_specs=in_specs,
        out_specs=out_specs,
        out_shape=out_shape,
        compiler_params=_params(("parallel",)),
        name="kv_proj",
    )(*args)


def _attn_prompt_kernel(sinks_ref, x_ref, g_ref, wq_ref, qg_ref, ones_ref, cos_ref, sin_ref,
                        kp_ref, kc_ref, vp_ref, vc_ref, wo_ref, o_ref):
    tile = pl.program_id(1)
    x = x_ref[...]
    n_blocks = x.shape[0] // WINDOW
    heads_per_kv = N_HEADS // N_KV_HEADS
    cols = heads_per_kv * WINDOW
    h = _rms(x, g_ref[...]).astype(BF16)
    q = _dot(h, wq_ref[...])
    q = q * lax.rsqrt(_seg_mean_sq(q, ones_ref) + EPS) * qg_ref[...]
    q = _rope(q, _tile_lanes(cos_ref[...], D_MODEL), _tile_lanes(sin_ref[...], D_MODEL))
    qb = (q * (HEAD_DIM ** -0.5)).astype(BF16)

    kall = jnp.concatenate([kp_ref[...], kc_ref[...]], axis=0).astype(BF16)
    vall = jnp.concatenate([vp_ref[...], vc_ref[...]], axis=1).astype(BF16)

    kj = lax.broadcasted_iota(jnp.int32, (WINDOW, cols), 0)
    col = lax.broadcasted_iota(jnp.int32, (WINDOW, cols), 1)
    from_prev = kj > (col & (WINDOW - 1))
    col_head = lax.broadcasted_iota(jnp.int32, (1, cols), 1) // WINDOW
    low_half = lax.broadcasted_iota(jnp.int32, (WINDOW, LANES), 1) < HEAD_DIM
    neg_inf = jnp.full((WINDOW, cols), -jnp.inf, F32)

    pairs = [(blk, g) for blk in range(n_blocks) for g in range(N_KV_HEADS)]
    scores, sinks = [], []
    for blk, g in pairs:
        r0 = WINDOW * blk
        kg = kall[r0:r0 + 2 * WINDOW, LANES * g:LANES * (g + 1)]
        qs = []
        sink = jnp.zeros((1, cols), F32)
        for hh in range(heads_per_kv):
            head = heads_per_kv * g + hh
            qp = qb[r0:r0 + WINDOW, LANES * (head // 2):LANES * (head // 2 + 1)]
            keep = low_half if head % 2 == 0 else jnp.logical_not(low_half)
            qs.append(jnp.where(keep, qp, jnp.zeros_like(qp)))
            sink = jnp.where(col_head == hh, sinks_ref[head], sink)
        sinks.append(sink)
        scores.append(lax.dot_general(kg, jnp.concatenate(qs, axis=0), (((1,), (1,)), ((), ())),
                                      preferred_element_type=F32))
    probs, denoms = [], []
    for (blk, g), s, sink in zip(pairs, scores, sinks):
        has_prev = tile * n_blocks + blk > 0
        t = jnp.where(from_prev, jnp.where(has_prev, s[:WINDOW], neg_inf), s[WINDOW:])
        m = jnp.maximum(jnp.max(t, axis=0, keepdims=True), sink)
        p = jnp.exp(t - m)
        denoms.append(jnp.sum(p, axis=0, keepdims=True) + jnp.exp(sink - m))
        zero = jnp.zeros_like(p)
        probs.append(jnp.concatenate([jnp.where(from_prev, p, zero), jnp.where(from_prev, zero, p)],
                                     axis=0).astype(BF16))
    head_outs = [[] for _ in range(n_blocks)]
    for (blk, g), pcat, denom in zip(pairs, probs, denoms):
        r0 = WINDOW * blk
        vg = vall[HEAD_DIM * g:HEAD_DIM * (g + 1), r0:r0 + 2 * WINDOW]
        o = _dot(vg, pcat) / denom
        head_outs[blk] += [o[:, WINDOW * hh:WINDOW * (hh + 1)] for hh in range(heads_per_kv)]
    block_outs = [jnp.concatenate(outs, axis=0).T for outs in head_outs]
    attn = jnp.concatenate(block_outs, axis=0).astype(BF16)
    o_ref[...] = x + _dot(attn, wo_ref[...])


def _attn_prompt(x, g, wq, qgain, ones, cos, sin, kd, vt, sinks, wo):
    bsz, seq, _ = x.shape
    nk, nv = kd.shape[-1], vt.shape[1]
    nb = ATTN_BLOCKS
    tq = nb * WINDOW
    x_spec = pl.BlockSpec((None, tq, D_MODEL), lambda b, i, *_: (b, i, 0))
    k_cur = pl.BlockSpec((None, tq, nk), lambda b, i, *_: (b, i, 0))
    k_prev = pl.BlockSpec((None, WINDOW, nk), lambda b, i, *_: (b, jnp.maximum(nb * i - 1, 0), 0))
    v_cur = pl.BlockSpec((None, nv, tq), lambda b, i, *_: (b, 0, i))
    v_prev = pl.BlockSpec((None, nv, WINDOW), lambda b, i, *_: (b, 0, jnp.maximum(nb * i - 1, 0)))
    tab = pl.BlockSpec((tq, LANES), lambda b, i, *_: (i, 0))
    grid_spec = pltpu.PrefetchScalarGridSpec(
        num_scalar_prefetch=1,
        grid=(bsz, seq // tq),
        in_specs=[x_spec,
                  _const_spec((1, D_MODEL)),
                  _const_spec(wq.shape),
                  _const_spec((1, D_MODEL)),
                  _const_spec(ones.shape),
                  tab, tab, k_prev, k_cur, v_prev, v_cur,
                  _const_spec(wo.shape)],
        out_specs=x_spec)
    return pl.pallas_call(
        _attn_prompt_kernel,
        grid_spec=grid_spec,
        out_shape=jax.ShapeDtypeStruct(x.shape, F32),
        compiler_params=_params(("parallel", "arbitrary")),
        name="attn_prompt",
    )(sinks, x, g, wq, qgain, ones, cos, sin, kd, kd, vt, vt, wo)


def _own_head(shape):
    row_head = lax.broadcasted_iota(jnp.int32, shape, 0) & (N_HEADS - 1)
    lane_head = lax.broadcasted_iota(jnp.int32, shape, 1) // HEAD_DIM
    return row_head == lane_head


def _q_sample_kernel(x_ref, g_ref, wq_ref, qg_ref, ones_ref, cos_ref, sin_ref, rep_ref, place_ref, q_ref):
    h = _rms(x_ref[...], g_ref[...]).astype(BF16)
    q = _dot(h, wq_ref[...])
    q = q * lax.rsqrt(_seg_mean_sq(q, ones_ref) + EPS) * qg_ref[...]
    q = _rope(q, _tile_lanes(cos_ref[...], D_MODEL), _tile_lanes(sin_ref[...], D_MODEL))
    qb = (q * (HEAD_DIM ** -0.5)).astype(BF16)
    q_rep = _dot(rep_ref[...], qb)
    q_own = jnp.where(_own_head(q_rep.shape), q_rep, 0.0).astype(BF16)
    q_ref[...] = _dot(q_own, place_ref[...])


def _q_sample(x, g, wq, qgain, ones, cos, sin, rep, place):
    rows = x.shape[0]
    return pl.pallas_call(
        _q_sample_kernel,
        out_shape=jax.ShapeDtypeStruct((rows * N_HEADS, place.shape[1]), F32),
        compiler_params=pltpu.CompilerParams(vmem_limit_bytes=VMEM_LIMIT_BYTES),
        name="q_sample",
    )(x, g, wq, qgain, ones, cos, sin, rep, place)


def _cache_append_kernel(k_ref, v_ref, kn_ref, vn_ref, ko_ref, vo_ref):
    nb, lanes_kv, keys = k_ref.shape
    first = pl.program_id(0) * nb
    key = lax.broadcasted_iota(jnp.int32, (lanes_kv, keys), 1)
    seq = lax.broadcasted_iota(jnp.int32, (kn_ref.shape[1], keys), 0)

    def split3(x):
        hi = x.astype(BF16)
        r = x - hi.astype(F32)
        mid = r.astype(BF16)
        return hi, mid, (r - mid.astype(F32)).astype(BF16)

    parts = [(split3(kn_ref[...]), k_ref, ko_ref), (split3(vn_ref[...]), v_ref, vo_ref)]
    for b in range(nb):
        pick = (seq == first + b).astype(BF16)
        for (hi, mid, lo), old_ref, out_ref in parts:
            new_col = _dot(hi, pick) + _dot(mid, pick) + _dot(lo, pick)
            out_ref[b] = jnp.where(key == keys - 1, new_col, pltpu.roll(old_ref[b], keys - 1, 1))


def _cache_append(kt, vt, kn_t, vn_t):
    bsz, lanes_kv, keys = kt.shape
    nb = SUBLANES
    blk = pl.BlockSpec((nb, lanes_kv, keys), lambda i: (i, 0, 0))
    return pl.pallas_call(
        _cache_append_kernel,
        grid=(bsz // nb,),
        in_specs=[blk, blk, _const_spec(kn_t.shape), _const_spec(vn_t.shape)],
        out_specs=[blk, blk],
        out_shape=[jax.ShapeDtypeStruct(kt.shape, F32), jax.ShapeDtypeStruct(vt.shape, F32)],
        compiler_params=_params(("parallel",)),
        name="cache_append",
    )(kt, vt, kn_t, vn_t)


def _attn_sample_kernel(q_ref, k_ref, v_ref, sink_ref, o_ref):
    sink = sink_ref[...]
    for b in range(q_ref.shape[0]):
        s = _dot(q_ref[b].astype(BF16), k_ref[b].astype(BF16))
        m = jnp.maximum(jnp.max(s, axis=-1, keepdims=True), sink)
        p = jnp.exp(s - m)
        denom = jnp.sum(p, axis=-1, keepdims=True) + jnp.exp(sink - m)
        o = lax.dot_general(p.astype(BF16), v_ref[b].astype(BF16), (((1,), (1,)), ((), ())),
                            preferred_element_type=F32)
        o_ref[b] = o / denom


def _attn_sample(q3, kt, vt, sink_col):
    bsz, nh, nkv = q3.shape
    keys = kt.shape[2]
    bb = SUBLANES
    return pl.pallas_call(
        _attn_sample_kernel,
        grid=(bsz // bb,),
        in_specs=[pl.BlockSpec((bb, nh, nkv), lambda i: (i, 0, 0)),
                  pl.BlockSpec((bb, nkv, keys), lambda i: (i, 0, 0)),
                  pl.BlockSpec((bb, nkv, keys), lambda i: (i, 0, 0)),
                  _const_spec(sink_col.shape)],
        out_specs=pl.BlockSpec((bb, nh, nkv), lambda i: (i, 0, 0)),
        out_shape=jax.ShapeDtypeStruct((bsz, nh, nkv), F32),
        compiler_params=_params(("parallel",)),
        name="attn_sample",
    )(q3, kt, vt, sink_col)


def _proj_residual_kernel(x_ref, a_ref, place_t_ref, rep_t_ref, w_ref, o_ref):
    z = _dot(a_ref[...].astype(BF16), place_t_ref[...])
    z_own = jnp.where(_own_head(z.shape), z, 0.0).astype(BF16)
    attn = _dot(rep_t_ref[...], z_own).astype(BF16)
    o_ref[...] = x_ref[...] + _dot(attn, w_ref[...])


def _proj_residual(x, a, place_t, rep_t, w):
    return pl.pallas_call(
        _proj_residual_kernel,
        out_shape=jax.ShapeDtypeStruct(x.shape, F32),
        compiler_params=pltpu.CompilerParams(vmem_limit_bytes=VMEM_LIMIT_BYTES),
        name="o_proj_sample",
    )(x, a, place_t, rep_t, w)


def _rope_tables(pos):
    half = HEAD_DIM // 2
    inv = ROPE_THETA ** (-jnp.arange(half, dtype=F32) / half)
    ang = pos.astype(F32)[:, None] * inv[None, :]
    cos, sin = jnp.cos(ang), jnp.sin(ang)
    reps = LANES // HEAD_DIM
    return (jnp.tile(jnp.concatenate([cos, cos], axis=-1), (1, reps)),
            jnp.tile(jnp.concatenate([-sin, sin], axis=-1), (1, reps)))


def _block_ones(width, block):
    idx = jnp.arange(width) // block
    return (idx[:, None] == idx[None, :]).astype(BF16)


def _dup_heads(w):
    w3 = w.reshape(w.shape[0], N_KV_HEADS, 1, HEAD_DIM)
    return jnp.broadcast_to(w3, (w.shape[0], N_KV_HEADS, LANES // HEAD_DIM, HEAD_DIM)).reshape(w.shape[0], -1)


def _head_placement():
    lane = jnp.arange(N_HEADS * HEAD_DIM)
    dst = (lane // HEAD_DIM) // (N_HEADS // N_KV_HEADS) * HEAD_DIM + lane % HEAD_DIM
    return (dst[:, None] == jnp.arange(N_KV_HEADS * HEAD_DIM)[None, :]).astype(BF16)


def _row_replication(n):
    return (jnp.arange(n * N_HEADS)[:, None] // N_HEADS == jnp.arange(n)[None, :]).astype(BF16)


def kernel(x_prompt, x_sample, state_ssm_re, state_ssm_im, cache_k, cache_v, norm_mix, norm_mlp, ssm_a_re, ssm_a_im, ssm_log_dt, ssm_b_re, ssm_b_im, ssm_c_re, ssm_c_im, ssm_d, w_glu, b_glu, norm_kv, w_k, w_v, k_norm, w_q, q_norm, attn_sinks, w_o, w_mlp_in, w_mlp_out):
    bsz, seq, _ = x_prompt.shape
    dec = x_sample.shape[0]
    n_a = ssm_a_re.shape[0]
    depth = norm_mix.shape[0]
    past = 8192
    nkv = N_KV_HEADS * HEAD_DIM
    row = lambda v: v.astype(F32).reshape(1, -1)

    xp = x_prompt.astype(F32)
    xs = x_sample.reshape(dec, D_MODEL)
    rows3 = lambda v: v.astype(F32).reshape(v.shape[0], 1, -1)

    w1 = w_mlp_in.astype(BF16)
    w2 = w_mlp_out.astype(BF16)
    g_mlp = rows3(norm_mlp)
    mlp_p = lambda x, layer: _mlp(x.reshape(bsz * seq, D_MODEL), g_mlp, w1, w2, layer).reshape(bsz, seq, D_MODEL)

    bhat, chat, lam = _s5_weights(ssm_a_re, ssm_a_im, ssm_log_dt, ssm_b_re, ssm_b_im, ssm_c_re, ssm_c_im)
    bhat16 = bhat.astype(BF16)
    wglu = w_glu.astype(BF16)
    g_mix, d_skip, bg = rows3(norm_mix), rows3(ssm_d), rows3(b_glu)
    sp_re, sp_im, ss_re, ss_im = [], [], [], []
    for i in range(n_a):
        xp, fin = _s5_prompt(xp, i, g_mix, bhat16, lam, chat, d_skip, wglu, bg)
        re, im = _lanes_to_state(fin[bsz:])
        sp_re.append(re)
        sp_im.append(im)
        s0 = _state_to_lanes(state_ssm_re[i], state_ssm_im[i])
        xs, snew = _s5_sample(xs, i, g_mix, bhat, lam, chat, d_skip, wglu, bg, s0)
        re, im = _lanes_to_state(snew)
        ss_re.append(re)
        ss_im.append(im)
        xp = mlp_p(xp, i)
        xs = _mlp(xs, g_mlp, w1, w2, i)

    cos_p, sin_p = _rope_tables(jnp.arange(seq, dtype=jnp.int32))
    cos_s, sin_s = _rope_tables(jnp.full((dec,), past, dtype=jnp.int32))
    ones_head = _block_ones(MXU_TILE, HEAD_DIM)
    place = _head_placement()
    rep = _row_replication(dec)
    kgain = row(k_norm)
    tile_gain = lambda gvec, n: jnp.tile(gvec, (1, n // HEAD_DIM))

    kd, vp, vt = _kv(xp.reshape(bsz * seq, D_MODEL), row(norm_kv), _dup_heads(w_k), w_v,
                     tile_gain(kgain, 2 * nkv), ones_head, cos_p, sin_p, lambda i: i % (seq // KV_ROWS), seq=seq)
    ks_new, vs_new = _kv(xs, row(norm_kv), w_k, w_v, tile_gain(kgain, nkv), ones_head, cos_s, sin_s,
                         lambda i: i)
    kd = kd.reshape(bsz, seq, 2 * nkv)
    new_k_p = kd[:, -WINDOW:].reshape(bsz, WINDOW, N_KV_HEADS, LANES // HEAD_DIM, HEAD_DIM)[:, :, :, 0]
    new_v_p = vp.reshape(bsz, seq, N_KV_HEADS, HEAD_DIM)[:, -WINDOW:]
    key_minor = lambda c: jnp.transpose(c.astype(F32), (0, 2, 3, 1)).reshape(dec, nkv, c.shape[1])
    keys_s, vals_s = _cache_append(key_minor(cache_k), key_minor(cache_v), ks_new.T, vs_new.T)
    key_major = lambda c: jnp.transpose(c.reshape(dec, N_KV_HEADS, HEAD_DIM, c.shape[2]), (0, 3, 1, 2))
    new_k_s, new_v_s = key_major(keys_s), key_major(vals_s)

    for j in range(depth - n_a):
        layer = n_a + j
        gn = row(norm_mix[layer])
        qgain = row(q_norm[j])
        wq = w_q[j].astype(BF16)
        wo = w_o[j].astype(BF16)
        xp = _attn_prompt(xp, gn, wq, tile_gain(qgain, D_MODEL), ones_head, cos_p, sin_p, kd, vt,
                          attn_sinks[j].astype(F32), wo)
        q_pad = _q_sample(xs, gn, wq, tile_gain(qgain, D_MODEL), ones_head, cos_s, sin_s, rep, place)
        o_pad = _attn_sample(q_pad.reshape(dec, N_HEADS, nkv), keys_s, vals_s,
                             attn_sinks[j].astype(F32).reshape(N_HEADS, 1))
        xs = _proj_residual(xs, o_pad.reshape(dec * N_HEADS, nkv), place.T, rep.T, wo)
        xp = mlp_p(xp, layer)
        xs = _mlp(xs, g_mlp, w1, w2, layer)

    return (xp, xs.reshape(dec, 1, D_MODEL),
            jnp.stack(sp_re), jnp.stack(sp_im), new_k_p, new_v_p,
            jnp.stack(ss_re), jnp.stack(ss_im), new_k_s, new_v_s)
```

```python
import functools
import math

import jax
import jax.numpy as jnp
from jax import lax
from jax.experimental import pallas as pl
from jax.experimental.pallas import tpu as pltpu

F32 = jnp.float32
BF16 = jnp.bfloat16

D_MODEL = 1024
N_GROUPS = 64
GROUP_SIZE = 16
STATE_DIM = 64
HEAD_DIM = 64
N_HEADS = 16
N_KV_HEADS = 4
WINDOW = 128
ROPE_THETA = 10000.0
D_FF = 4 * D_MODEL
EPS = 1e-6

SUBLANES = 8
LANES = 128
MXU_TILE = 256
VMEM_LIMIT_BYTES = 56 * 1024 * 1024

N_OCTETS = D_MODEL // LANES
OCTET_STATE = (LANES // GROUP_SIZE) * STATE_DIM
STATE_LANES = 2 * OCTET_STATE * N_OCTETS

S5_TIME_CHUNK = 128
MLP_ROWS = 1024
FF_CHUNK = 1024
KV_ROWS = 1024
ATTN_BLOCKS = 8


def _const_spec(shape):
    zeros = (0,) * len(shape)
    return pl.BlockSpec(shape, lambda *_: zeros, pipeline_mode=pl.Buffered(1))


def _layer_spec(shape, layer):
    idx = (layer,) + (0,) * (len(shape) - 1)
    return pl.BlockSpec((None,) + tuple(shape[1:]), lambda *_: idx, pipeline_mode=pl.Buffered(1))


def _params(sem):
    return pltpu.CompilerParams(dimension_semantics=sem, vmem_limit_bytes=VMEM_LIMIT_BYTES)


def _rms(x, g):
    return x * lax.rsqrt(jnp.mean(x * x, axis=-1, keepdims=True) + EPS) * g


def _dot(a, b):
    return jnp.dot(a, b, preferred_element_type=F32)


def _seg_mean_sq(x, ones_ref):
    sq = x * x
    hi = sq.astype(BF16)
    lo = (sq - hi.astype(F32)).astype(BF16)
    ones = ones_ref[...]
    outs = []
    for t in range(x.shape[-1] // MXU_TILE):
        sl = slice(MXU_TILE * t, MXU_TILE * (t + 1))
        outs.append(_dot(hi[:, sl], ones) + _dot(lo[:, sl], ones))
    return jnp.concatenate(outs, axis=-1) * (1.0 / HEAD_DIM)


def _tile_lanes(t, n):
    return jnp.concatenate([t] * (n // t.shape[-1]), axis=-1)


def _rope(x, cos, sin_signed):
    n = x.shape[-1]
    lane = lax.broadcasted_iota(jnp.int32, x.shape, 1)
    first = (lane & (HEAD_DIM - 1)) < (HEAD_DIM // 2)
    partner = jnp.where(first, pltpu.roll(x, n - HEAD_DIM // 2, 1), pltpu.roll(x, HEAD_DIM // 2, 1))
    return x * cos + partner * sin_signed


def _mlp_kernel(x_ref, g_ref, w1_ref, w2_ref, o_ref):
    x = x_ref[...]
    h = _rms(x, g_ref[...]).astype(BF16)
    acc = x
    for c in range(D_FF // FF_CHUNK):
        sl = slice(FF_CHUNK * c, FF_CHUNK * (c + 1))
        a = jnp.square(jnp.maximum(_dot(h, w1_ref[:, sl]), 0.0)).astype(BF16)
        acc = acc + _dot(a, w2_ref[sl, :])
    o_ref[...] = acc


def _mlp(x, g, w1, w2, layer):
    rows = x.shape[0]
    tm = min(MLP_ROWS, rows)
    return pl.pallas_call(
        _mlp_kernel,
        grid=(rows // tm,),
        in_specs=[pl.BlockSpec((tm, D_MODEL), lambda i: (i, 0)),
                  _layer_spec(g.shape, layer),
                  _layer_spec(w1.shape, layer),
                  _layer_spec(w2.shape, layer)],
        out_specs=pl.BlockSpec((tm, D_MODEL), lambda i: (i, 0)),
        out_shape=jax.ShapeDtypeStruct((rows, D_MODEL), F32),
        compiler_params=_params(("parallel",)),
        name="mlp",
    )(x, g, w1, w2)


def _s5_gate(y, wglu_ref, bglu_ref):
    g = jax.nn.gelu(y).astype(BF16)
    z = _dot(g, wglu_ref[...]) + bglu_ref[...]
    return z[:, :D_MODEL] * jax.nn.sigmoid(z[:, D_MODEL:])


def _slabs(ref, rows=slice(None)):
    return jnp.concatenate([ref[c, rows, :] for c in range(ref.shape[0])], axis=-1)


def _s5_prompt_kernel(xa_ref, xb_ref, gn_ref, b2_ref, lam_ref, c2_ref, m0_ref, d_ref, wglu_ref, bglu_ref,
                      o_ref, fin_ref, st_ref, carry_ref, nat_ref, ue_ref, uo_ref, up_ref, tb2_ref, nat2_ref, y_ref):
    batch, tc, _ = xa_ref.shape
    assert 2 * batch == SUBLANES
    n_slabs = D_MODEL // LANES
    pairs = tc // 2
    rows_k = pairs * batch
    step = pl.program_id(0)
    n_chunks = pl.num_programs(0) - 1
    live = step < n_chunks

    @pl.when(step == 0)
    def _():
        carry_ref[...] = jnp.zeros_like(carry_ref)
        y_ref[...] = jnp.zeros_like(y_ref)
        up_ref[...] = jnp.zeros_like(up_ref)

    upd = _s5_gate(y_ref[...], wglu_ref, bglu_ref)
    for c in range(n_slabs):
        tb2_ref[c] = upd[:, LANES * c:LANES * (c + 1)]
    for b in range(batch):
        for c in range(n_slabs):
            nat2_ref[c, pl.ds(0, pairs, stride=2), :] = tb2_ref[c, pl.ds(b, pairs, stride=batch), :]
            nat2_ref[c, pl.ds(1, pairs, stride=2), :] = tb2_ref[c, pl.ds(rows_k + b, pairs, stride=batch), :]
        o_ref[b] = xb_ref[b] + _slabs(nat2_ref)

    for c in range(n_slabs):
        up_ref[c, 0:batch, :] = up_ref[c, rows_k:rows_k + batch, :]
    for b in range(batch):
        u_nat = _rms(xa_ref[b], gn_ref[...])
        for c in range(n_slabs):
            nat_ref[c] = u_nat[:, LANES * c:LANES * (c + 1)]
        for c in range(n_slabs):
            even = nat_ref[c, pl.ds(0, pairs, stride=2), :]
            odd = nat_ref[c, pl.ds(1, pairs, stride=2), :]
            ue_ref[c, pl.ds(b, pairs, stride=batch), :] = even
            uo_ref[c, pl.ds(b, pairs, stride=batch), :] = odd
            up_ref[c, pl.ds(batch + b, pairs, stride=batch), :] = odd
    ue, uo = _slabs(ue_ref), _slabs(uo_ref)
    ue_b, uo_b, up_b = ue.astype(BF16), uo.astype(BF16), _slabs(up_ref, slice(0, rows_k)).astype(BF16)

    lo = lax.broadcasted_iota(jnp.int32, (SUBLANES, LANES), 0) < batch
    swap = lambda v: pltpu.roll(v, batch, 0)
    n_vregs = rows_k // SUBLANES
    half = OCTET_STATE // LANES
    y_even, y_odd = [], []
    for q0 in range(0, N_OCTETS, 2):
        octets = (q0, q0 + 1)
        for q in octets:
            ch = slice(LANES * q, LANES * (q + 1))
            lhs = jnp.concatenate([up_b[:, ch], ue_b[:, ch]], axis=1)
            st_ref[:, 2 * OCTET_STATE * q:2 * OCTET_STATE * (q + 1)] = _dot(lhs, b2_ref[q])
        chains = []
        for q in octets:
            for j in range(half // 2):
                cols = [2 * OCTET_STATE * q + LANES * jj for jj in (j, j + half // 2)]
                lanes = [OCTET_STATE * q + LANES * jj for jj in (j, j + half // 2)]
                chains.append((cols, lanes))
        lam_r, lam_i, sr, si, old = [], [], [], [], []
        for (ca, cb), (la, lb) in chains:
            bc = lambda r, c: jnp.broadcast_to(lam_ref[r:r + 1, c:c + LANES], (SUBLANES, LANES))
            lam_r.append(jnp.where(lo, bc(2, la), bc(2, lb)))
            lam_i.append(jnp.where(lo, bc(3, la), bc(3, lb)))
            prev = [carry_ref[:, c:c + LANES] for c in (ca, cb, ca + OCTET_STATE, cb + OCTET_STATE)]
            old.append(prev)
            sr.append(jnp.where(lo, swap(prev[0]), prev[1]))
            si.append(jnp.where(lo, swap(prev[2]), prev[3]))
        last = [None] * len(chains)
        for m in range(n_vregs):
            rows = slice(SUBLANES * m, SUBLANES * (m + 1))
            for j, ((ca, cb), _) in enumerate(chains):
                ar, br = st_ref[rows, ca:ca + LANES], st_ref[rows, cb:cb + LANES]
                ai, bi = (st_ref[rows, ca + OCTET_STATE:ca + OCTET_STATE + LANES],
                          st_ref[rows, cb + OCTET_STATE:cb + OCTET_STATE + LANES])
                d0r, d1r = jnp.where(lo, ar, swap(br)), jnp.where(lo, swap(ar), br)
                d0i, d1i = jnp.where(lo, ai, swap(bi)), jnp.where(lo, swap(ai), bi)
                s0r = lam_r[j] * sr[j] - lam_i[j] * si[j] + d0r
                s0i = lam_r[j] * si[j] + lam_i[j] * sr[j] + d0i
                s1r = lam_r[j] * s0r - lam_i[j] * s0i + d1r
                s1i = lam_r[j] * s0i + lam_i[j] * s0r + d1i
                sr[j], si[j] = s1r, s1i
                out = (jnp.where(lo, s0r, swap(s1r)), jnp.where(lo, swap(s0r), s1r),
                       jnp.where(lo, s0i, swap(s1i)), jnp.where(lo, swap(s0i), s1i))
                st_ref[rows, ca:ca + LANES] = out[0]
                st_ref[rows, cb:cb + LANES] = out[1]
                st_ref[rows, ca + OCTET_STATE:ca + OCTET_STATE + LANES] = out[2]
                st_ref[rows, cb + OCTET_STATE:cb + OCTET_STATE + LANES] = out[3]
                last[j] = out
        for j, ((ca, cb), _) in enumerate(chains):
            for c, new, prev in zip((ca, cb, ca + OCTET_STATE, cb + OCTET_STATE), last[j], old[j]):
                carry_ref[:, c:c + LANES] = jnp.where(live, new, prev)
        for q in octets:
            ch = slice(LANES * q, LANES * (q + 1))
            y2 = _dot(st_ref[:, 2 * OCTET_STATE * q:2 * OCTET_STATE * (q + 1)].astype(BF16), c2_ref[q])
            y_even.append(y2[:, :LANES])
            y_odd.append(y2[:, LANES:] + _dot(uo_b[:, ch], m0_ref[q]))
    y_ref[0:rows_k, :] = jnp.concatenate(y_even, axis=-1) + d_ref[...] * ue
    y_ref[rows_k:2 * rows_k, :] = jnp.concatenate(y_odd, axis=-1) + d_ref[...] * uo

    @pl.when(step == n_chunks - 1)
    def _():
        tail = uo_b[rows_k - 2 * SUBLANES:, :]
        zeros = jnp.zeros((2 * SUBLANES, LANES), BF16)
        for q in range(N_OCTETS):
            bu = _dot(jnp.concatenate([zeros, tail[:, LANES * q:LANES * (q + 1)]], axis=1), b2_ref[q])[SUBLANES:]
            base = 2 * OCTET_STATE * q
            lr = lam_ref[0:1, OCTET_STATE * q:OCTET_STATE * (q + 1)]
            li = lam_ref[1:2, OCTET_STATE * q:OCTET_STATE * (q + 1)]
            s_r = carry_ref[:, base:base + OCTET_STATE]
            s_i = carry_ref[:, base + OCTET_STATE:base + 2 * OCTET_STATE]
            fin_ref[:, base:base + OCTET_STATE] = lr * s_r - li * s_i + bu[:, :OCTET_STATE]
            fin_ref[:, base + OCTET_STATE:base + 2 * OCTET_STATE] = lr * s_i + li * s_r + bu[:, OCTET_STATE:]


def _s5_prompt(x, layer, gn, b2, lam, c2, m0, d, wglu, bglu):
    batch, seq, _ = x.shape
    tc = S5_TIME_CHUNK
    n_chunks = seq // tc
    rows_k = tc // 2 * batch
    n_slabs = D_MODEL // LANES
    blk = (batch, tc, D_MODEL)
    this_chunk = pl.BlockSpec(blk, lambda s: (0, jnp.minimum(s, n_chunks - 1), 0))
    prev_chunk = pl.BlockSpec(blk, lambda s: (0, jnp.maximum(s - 1, 0), 0))
    slab = lambda rows: pltpu.VMEM((n_slabs, rows, LANES), F32)
    return pl.pallas_call(
        _s5_prompt_kernel,
        grid=(n_chunks + 1,),
        in_specs=[this_chunk, prev_chunk]
        + [_layer_spec(w.shape, layer) for w in (gn, b2, lam, c2, m0, d, wglu, bglu)],
        out_specs=[prev_chunk, pl.BlockSpec((SUBLANES, STATE_LANES), lambda s: (0, 0))],
        out_shape=[jax.ShapeDtypeStruct(x.shape, F32),
                   jax.ShapeDtypeStruct((SUBLANES, STATE_LANES), F32)],
        scratch_shapes=[pltpu.VMEM((rows_k, STATE_LANES), F32),
                        pltpu.VMEM((SUBLANES, STATE_LANES), F32),
                        slab(tc), slab(rows_k), slab(rows_k), slab(rows_k + SUBLANES),
                        slab(2 * rows_k), slab(tc),
                        pltpu.VMEM((2 * rows_k, D_MODEL), F32)],
        compiler_params=_params(("arbitrary",)),
        name="s5_prompt",
    )(x, x, gn, b2, lam, c2, m0, d, wglu, bglu)


def _s5_sample_kernel(x_ref, gn_ref, b2_ref, lam_ref, c2_ref, d_ref, wglu_ref, bglu_ref, s0_ref,
                      o_ref, snew_ref):
    x = x_ref[...]
    u = _rms(x, gn_ref[...])
    ys = []
    for q in range(N_OCTETS):
        base = 2 * OCTET_STATE * q
        bu = jnp.dot(u[:, LANES * q:LANES * (q + 1)], b2_ref[q, LANES:, :],
                     preferred_element_type=F32, precision=lax.Precision.HIGHEST)
        lr = lam_ref[0:1, OCTET_STATE * q:OCTET_STATE * (q + 1)]
        li = lam_ref[1:2, OCTET_STATE * q:OCTET_STATE * (q + 1)]
        s0r = s0_ref[:, base:base + OCTET_STATE]
        s0i = s0_ref[:, base + OCTET_STATE:base + 2 * OCTET_STATE]
        sr = lr * s0r - li * s0i + bu[:, :OCTET_STATE]
        si = lr * s0i + li * s0r + bu[:, OCTET_STATE:]
        snew_ref[:, base:base + OCTET_STATE] = sr
        snew_ref[:, base + OCTET_STATE:base + 2 * OCTET_STATE] = si
        sb = jnp.concatenate([sr, si], axis=-1).astype(BF16)
        ys.append(_dot(sb, c2_ref[q, :, :LANES]))
    y = jnp.concatenate(ys, axis=-1)
    o_ref[...] = x + _s5_gate(y + d_ref[...] * u, wglu_ref, bglu_ref)


def _s5_sample(x, layer, gn, b2_f32, lam, c2, d, wglu, bglu, s0):
    rows = x.shape[0]
    return pl.pallas_call(
        _s5_sample_kernel,
        grid=(1,),
        in_specs=[_const_spec(x.shape)] + [_layer_spec(w.shape, layer) for w in (gn, b2_f32, lam, c2, d, wglu, bglu)]
        + [_const_spec(s0.shape)],
        out_specs=[pl.BlockSpec((rows, D_MODEL), lambda i: (0, 0)),
                   pl.BlockSpec((rows, STATE_LANES), lambda i: (0, 0))],
        out_shape=[jax.ShapeDtypeStruct((rows, D_MODEL), F32),
                   jax.ShapeDtypeStruct((rows, STATE_LANES), F32)],
        compiler_params=_params(("arbitrary",)),
        name="s5_sample",
    )(x, gn, b2_f32, lam, c2, d, wglu, bglu, s0)


def _block_diag(w):
    n, octets, groups, r, c = w.shape
    col = jnp.arange(groups * c)
    tile = (jnp.arange(c)[:, None] == col % c).astype(F32)
    own = (jnp.arange(groups * r)[:, None] // r == col // c).astype(F32)
    tiled = jnp.einsum("nqrc,cl->nqrl", w.reshape(n, octets, groups * r, c), tile, precision=lax.Precision.HIGHEST)
    return tiled * own


def _s5_weights(a_re, a_im, log_dt, b_re, b_im, c_re, c_im):
    n = a_re.shape[0]
    dt = jnp.exp(log_dt.astype(F32))
    lam = lax.complex(a_re.astype(F32), a_im.astype(F32))
    lam_bar = jnp.exp(lam * dt)
    b = lax.complex(b_re.astype(F32), b_im.astype(F32))
    b_bar = ((lam_bar - 1.0) / lam)[..., None] * b
    c = lax.complex(c_re.astype(F32), c_im.astype(F32))
    gpo = N_GROUPS // N_OCTETS
    octets = lambda w: w.reshape((n, N_OCTETS, gpo) + w.shape[2:])

    def into_state(w):
        wt = jnp.swapaxes(octets(w), 3, 4)
        return jnp.concatenate([_block_diag(jnp.real(wt)), _block_diag(jnp.imag(wt))], axis=-1)

    def out_of_state(w):
        wt = jnp.swapaxes(octets(w), 3, 4)
        return jnp.concatenate([_block_diag(jnp.real(wt)), -_block_diag(jnp.imag(wt))], axis=2)

    b2 = jnp.concatenate([into_state(lam_bar[..., None] * b_bar), into_state(b_bar)], axis=2)
    c2 = jnp.concatenate([out_of_state(c), out_of_state(c * lam_bar[:, :, None, :])], axis=-1)
    cb = jnp.real(jnp.einsum("ngcp,ngpd->ngdc", c, b_bar))
    m0 = _block_diag(octets(cb))
    lam2 = lam_bar * lam_bar
    lam_rows = jnp.stack([jnp.real(lam_bar).reshape(n, -1), jnp.imag(lam_bar).reshape(n, -1),
                          jnp.real(lam2).reshape(n, -1), jnp.imag(lam2).reshape(n, -1)], axis=1)
    return b2, c2.astype(BF16), m0.astype(BF16), lam_rows


def _state_to_lanes(s_re, s_im):
    n = s_re.shape[0]
    re = s_re.astype(F32).reshape(n, N_OCTETS, OCTET_STATE)
    im = s_im.astype(F32).reshape(n, N_OCTETS, OCTET_STATE)
    return jnp.concatenate([re, im], axis=-1).reshape(n, STATE_LANES)


def _lanes_to_state(s):
    n = s.shape[0]
    s4 = s.reshape(n, N_OCTETS, 2, OCTET_STATE)
    return (s4[:, :, 0].reshape(n, N_GROUPS, STATE_DIM), s4[:, :, 1].reshape(n, N_GROUPS, STATE_DIM))


def _rope_partner(n):
    lane = jnp.arange(n)
    return jnp.where(lane % HEAD_DIM < HEAD_DIM // 2, lane + HEAD_DIM // 2, lane - HEAD_DIM // 2)


def _kv_kernel(x_ref, g_ref, wkv_ref, kg_ref, kgp_ref, ones_ref, cos_ref, sin_ref, *rest):
    if len(rest) == 4:
        wvt_ref, k_ref, v_ref, vt_ref = rest
    else:
        (k_ref, v_ref), wvt_ref, vt_ref = rest, None, None
    h = _rms(x_ref[...], g_ref[...]).astype(BF16)
    kv = _dot(h, wkv_ref[...])
    n = k_ref.shape[-1]
    k, k_partner = kv[:, :n], kv[:, n:2 * n]
    scale = lax.rsqrt(_seg_mean_sq(k, ones_ref) + EPS)
    k_ref[...] = scale * (k * (kg_ref[...] * _tile_lanes(cos_ref[...], n))
                          + k_partner * (kgp_ref[...] * _tile_lanes(sin_ref[...], n)))
    v_ref[...] = kv[:, 2 * n:]
    if vt_ref is not None:
        vt_ref[...] = lax.dot_general(wvt_ref[...], h, (((1,), (1,)), ((), ())), preferred_element_type=F32)


def _kv(x, g, wk, wv, kgain, ones, cos, sin, table_block, seq=None):
    rows = x.shape[0]
    n, nv = wk.shape[1], wv.shape[1]
    partner = _rope_partner(n)
    wkv = jnp.concatenate([wk, wk[:, partner], wv], axis=1).astype(BF16)
    tm = min(KV_ROWS, rows)
    in_specs = [pl.BlockSpec((tm, D_MODEL), lambda i: (i, 0)),
                _const_spec((1, D_MODEL)),
                _const_spec(wkv.shape),
                _const_spec((1, n)),
                _const_spec((1, n)),
                _const_spec(ones.shape),
                pl.BlockSpec((tm, LANES), lambda i: (table_block(i), 0)),
                pl.BlockSpec((tm, LANES), lambda i: (table_block(i), 0))]
    out_specs = [pl.BlockSpec((tm, n), lambda i: (i, 0)),
                 pl.BlockSpec((tm, nv), lambda i: (i, 0))]
    out_shape = [jax.ShapeDtypeStruct((rows, n), F32),
                 jax.ShapeDtypeStruct((rows, nv), F32)]
    args = [x, g, wkv, kgain, kgain[:, partner], ones, cos, sin]
    if seq is not None:
        per_seq = seq // tm
        in_specs.append(_const_spec((nv, D_MODEL)))
        args.append(wv.T.astype(BF16))
        out_specs.append(pl.BlockSpec((None, nv, tm), lambda i: (i // per_seq, 0, i % per_seq)))
        out_shape.append(jax.ShapeDtypeStruct((rows // seq, nv, seq), F32))
    return pl.pallas_call(
        _kv_kernel,
        grid=(rows // tm,),
        in_specs=in_specs,
        out_specs=out_specs,
        out_shape=out_shape,
        compiler_params=_params(("parallel",)),
        name="kv_proj",
    )(*args)


def _attn_prompt_kernel(sinks_ref, x_ref, g_ref, wq_ref, qg_ref, ones_ref, cos_ref, sin_ref,
                        kp_ref, kc_ref, vp_ref, vc_ref, wo_ref, o_ref):
    tile = pl.program_id(1)
    x = x_ref[...]
    n_blocks = x.shape[0] // WINDOW
    heads_per_kv = N_HEADS // N_KV_HEADS
    cols = heads_per_kv * WINDOW
    h = _rms(x, g_ref[...]).astype(BF16)
    q = _dot(h, wq_ref[...])
    q = q * lax.rsqrt(_seg_mean_sq(q, ones_ref) + EPS) * qg_ref[...]
    q = _rope(q, _tile_lanes(cos_ref[...], D_MODEL), _tile_lanes(sin_ref[...], D_MODEL))
    qb = (q * (HEAD_DIM ** -0.5)).astype(BF16)

    kall = jnp.concatenate([kp_ref[...], kc_ref[...]], axis=0).astype(BF16)
    vall = jnp.concatenate([vp_ref[...], vc_ref[...]], axis=1).astype(BF16)

    kj = lax.broadcasted_iota(jnp.int32, (WINDOW, cols), 0)
    col = lax.broadcasted_iota(jnp.int32, (WINDOW, cols), 1)
    from_prev = kj > (col & (WINDOW - 1))
    col_head = lax.broadcasted_iota(jnp.int32, (1, cols), 1) // WINDOW
    low_half = lax.broadcasted_iota(jnp.int32, (WINDOW, LANES), 1) < HEAD_DIM
    neg_inf = jnp.full((WINDOW, cols), -jnp.inf, F32)

    pairs = [(blk, g) for blk in range(n_blocks) for g in range(N_KV_HEADS)]
    scores, sinks = [], []
    for blk, g in pairs:
        r0 = WINDOW * blk
        kg = kall[r0:r0 + 2 * WINDOW, LANES * g:LANES * (g + 1)]
        qs = []
        sink = jnp.zeros((1, cols), F32)
        for hh in range(heads_per_kv):
            head = heads_per_kv * g + hh
            qp = qb[r0:r0 + WINDOW, LANES * (head // 2):LANES * (head // 2 + 1)]
            keep = low_half if head % 2 == 0 else jnp.logical_not(low_half)
            qs.append(jnp.where(keep, qp, jnp.zeros_like(qp)))
            sink = jnp.where(col_head == hh, sinks_ref[head], sink)
        sinks.append(sink)
        scores.append(lax.dot_general(kg, jnp.concatenate(qs, axis=0), (((1,), (1,)), ((), ())),
                                      preferred_element_type=F32))
    probs, denoms = [], []
    for (blk, g), s, sink in zip(pairs, scores, sinks):
        has_prev = tile * n_blocks + blk > 0
        t = jnp.where(from_prev, jnp.where(has_prev, s[:WINDOW], neg_inf), s[WINDOW:])
        m = jnp.maximum(jnp.max(t, axis=0, keepdims=True), sink)
        p = jnp.exp(t - m)
        denoms.append(jnp.sum(p, axis=0, keepdims=True) + jnp.exp(sink - m))
        zero = jnp.zeros_like(p)
        probs.append(jnp.concatenate([jnp.where(from_prev, p, zero), jnp.where(from_prev, zero, p)],
                                     axis=0).astype(BF16))
    head_outs = [[] for _ in range(n_blocks)]
    for (blk, g), pcat, denom in zip(pairs, probs, denoms):
        r0 = WINDOW * blk
        vg = vall[HEAD_DIM * g:HEAD_DIM * (g + 1), r0:r0 + 2 * WINDOW]
        o = _dot(vg, pcat) / denom
        head_outs[blk] += [o[:, WINDOW * hh:WINDOW * (hh + 1)] for hh in range(heads_per_kv)]
    block_outs = [jnp.concatenate(outs, axis=0).T for outs in head_outs]
    attn = jnp.concatenate(block_outs, axis=0).astype(BF16)
    o_ref[...] = x + _dot(attn, wo_ref[...])


def _attn_prompt(x, g, wq, qgain, ones, cos, sin, kd, vt, sinks, wo):
    bsz, seq, _ = x.shape
    nk, nv = kd.shape[-1], vt.shape[1]
    nb = ATTN_BLOCKS
    tq = nb * WINDOW
    x_spec = pl.BlockSpec((None, tq, D_MODEL), lambda b, i, *_: (b, i, 0))
    k_cur = pl.BlockSpec((None, tq, nk), lambda b, i, *_: (b, i, 0))
    k_prev = pl.BlockSpec((None, WINDOW, nk), lambda b, i, *_: (b, jnp.maximum(nb * i - 1, 0), 0))
    v_cur = pl.BlockSpec((None, nv, tq), lambda b, i, *_: (b, 0, i))
    v_prev = pl.BlockSpec((None, nv, WINDOW), lambda b, i, *_: (b, 0, jnp.maximum(nb * i - 1, 0)))
    tab = pl.BlockSpec((tq, LANES), lambda b, i, *_: (i, 0))
    grid_spec = pltpu.PrefetchScalarGridSpec(
        num_scalar_prefetch=1,
        grid=(bsz, seq // tq),
        in_specs=[x_spec,
                  _const_spec((1, D_MODEL)),
                  _const_spec(wq.shape),
                  _const_spec((1, D_MODEL)),
                  _const_spec(ones.shape),
                  tab, tab, k_prev, k_cur, v_prev, v_cur,
                  _const_spec(wo.shape)],
        out_specs=x_spec)
    return pl.pallas_call(
        _attn_prompt_kernel,
        grid_spec=grid_spec,
        out_shape=jax.ShapeDtypeStruct(x.shape, F32),
        compiler_params=_params(("parallel", "arbitrary")),
        name="attn_prompt",
    )(sinks, x, g, wq, qgain, ones, cos, sin, kd, kd, vt, vt, wo)


def _own_head(shape):
    row_head = lax.broadcasted_iota(jnp.int32, shape, 0) & (N_HEADS - 1)
    lane_head = lax.broadcasted_iota(jnp.int32, shape, 1) // HEAD_DIM
    return row_head == lane_head


def _q_sample_kernel(x_ref, g_ref, wq_ref, qg_ref, ones_ref, cos_ref, sin_ref, rep_ref, place_ref, q_ref):
    h = _rms(x_ref[...], g_ref[...]).astype(BF16)
    q = _dot(h, wq_ref[...])
    q = q * lax.rsqrt(_seg_mean_sq(q, ones_ref) + EPS) * qg_ref[...]
    q = _rope(q, _tile_lanes(cos_ref[...], D_MODEL), _tile_lanes(sin_ref[...], D_MODEL))
    qb = (q * (HEAD_DIM ** -0.5)).astype(BF16)
    q_rep = _dot(rep_ref[...], qb)
    q_own = jnp.where(_own_head(q_rep.shape), q_rep, 0.0).astype(BF16)
    q_ref[...] = _dot(q_own, place_ref[...])


def _q_sample(x, g, wq, qgain, ones, cos, sin, rep, place):
    rows = x.shape[0]
    return pl.pallas_call(
        _q_sample_kernel,
        out_shape=jax.ShapeDtypeStruct((rows * N_HEADS, place.shape[1]), F32),
        compiler_params=pltpu.CompilerParams(vmem_limit_bytes=VMEM_LIMIT_BYTES),
        name="q_sample",
    )(x, g, wq, qgain, ones, cos, sin, rep, place)


def _cache_append_kernel(k_ref, v_ref, kn_ref, vn_ref, ko_ref, vo_ref):
    nb, lanes_kv, keys = k_ref.shape
    first = pl.program_id(0) * nb
    key = lax.broadcasted_iota(jnp.int32, (lanes_kv, keys), 1)
    seq = lax.broadcasted_iota(jnp.int32, (kn_ref.shape[1], keys), 0)

    def split3(x):
        hi = x.astype(BF16)
        r = x - hi.astype(F32)
        mid = r.astype(BF16)
        return hi, mid, (r - mid.astype(F32)).astype(BF16)

    parts = [(split3(kn_ref[...]), k_ref, ko_ref), (split3(vn_ref[...]), v_ref, vo_ref)]
    for b in range(nb):
        pick = (seq == first + b).astype(BF16)
        for (hi, mid, lo), old_ref, out_ref in parts:
            new_col = _dot(hi, pick) + _dot(mid, pick) + _dot(lo, pick)
            out_ref[b] = jnp.where(key == keys - 1, new_col, pltpu.roll(old_ref[b], keys - 1, 1))


def _cache_append(kt, vt, kn_t, vn_t):
    bsz, lanes_kv, keys = kt.shape
    nb = SUBLANES
    blk = pl.BlockSpec((nb, lanes_kv, keys), lambda i: (i, 0, 0))
    return pl.pallas_call(
        _cache_append_kernel,
        grid=(bsz // nb,),
        in_specs=[blk, blk, _const_spec(kn_t.shape), _const_spec(vn_t.shape)],
        out_specs=[blk, blk],
        out_shape=[jax.ShapeDtypeStruct(kt.shape, F32), jax.ShapeDtypeStruct(vt.shape, F32)],
        compiler_params=_params(("parallel",)),
        name="cache_append",
    )(kt, vt, kn_t, vn_t)


def _attn_sample_kernel(q_ref, k_ref, v_ref, sink_ref, o_ref):
    sink = sink_ref[...]
    for b in range(q_ref.shape[0]):
        s = _dot(q_ref[b].astype(BF16), k_ref[b].astype(BF16))
        m = jnp.maximum(jnp.max(s, axis=-1, keepdims=True), sink)
        p = jnp.exp(s - m)
        denom = jnp.sum(p, axis=-1, keepdims=True) + jnp.exp(sink - m)
        o = lax.dot_general(p.astype(BF16), v_ref[b].astype(BF16), (((1,), (1,)), ((), ())),
                            preferred_element_type=F32)
        o_ref[b] = o / denom


def _attn_sample(q3, kt, vt, sink_col):
    bsz, nh, nkv = q3.shape
    keys = kt.shape[2]
    bb = SUBLANES
    return pl.pallas_call(
        _attn_sample_kernel,
        grid=(bsz // bb,),
        in_specs=[pl.BlockSpec((bb, nh, nkv), lambda i: (i, 0, 0)),
                  pl.BlockSpec((bb, nkv, keys), lambda i: (i, 0, 0)),
                  pl.BlockSpec((bb, nkv, keys), lambda i: (i, 0, 0)),
                  _const_spec(sink_col.shape)],
        out_specs=pl.BlockSpec((bb, nh, nkv), lambda i: (i, 0, 0)),
        out_shape=jax.ShapeDtypeStruct((bsz, nh, nkv), F32),
        compiler_params=_params(("parallel",)),
        name="attn_sample",
    )(q3, kt, vt, sink_col)


def _proj_residual_kernel(x_ref, a_ref, place_t_ref, rep_t_ref, w_ref, o_ref):
    z = _dot(a_ref[...].astype(BF16), place_t_ref[...])
    z_own = jnp.where(_own_head(z.shape), z, 0.0).astype(BF16)
    attn = _dot(rep_t_ref[...], z_own).astype(BF16)
    o_ref[...] = x_ref[...] + _dot(attn, w_ref[...])


def _proj_residual(x, a, place_t, rep_t, w):
    return pl.pallas_call(
        _proj_residual_kernel,
        out_shape=jax.ShapeDtypeStruct(x.shape, F32),
        compiler_params=pltpu.CompilerParams(vmem_limit_bytes=VMEM_LIMIT_BYTES),
        name="o_proj_sample",
    )(x, a, place_t, rep_t, w)


def _rope_tables(pos):
    half = HEAD_DIM // 2
    inv = ROPE_THETA ** (-jnp.arange(half, dtype=F32) / half)
    ang = pos.astype(F32)[:, None] * inv[None, :]
    cos, sin = jnp.cos(ang), jnp.sin(ang)
    reps = LANES // HEAD_DIM
    return (jnp.tile(jnp.concatenate([cos, cos], axis=-1), (1, reps)),
            jnp.tile(jnp.concatenate([-sin, sin], axis=-1), (1, reps)))


def _block_ones(width, block):
    idx = jnp.arange(width) // block
    return (idx[:, None] == idx[None, :]).astype(BF16)


def _dup_heads(w):
    w3 = w.reshape(w.shape[0], N_KV_HEADS, 1, HEAD_DIM)
    return jnp.broadcast_to(w3, (w.shape[0], N_KV_HEADS, LANES // HEAD_DIM, HEAD_DIM)).reshape(w.shape[0], -1)


def _head_placement():
    lane = jnp.arange(N_HEADS * HEAD_DIM)
    dst = (lane // HEAD_DIM) // (N_HEADS // N_KV_HEADS) * HEAD_DIM + lane % HEAD_DIM
    return (dst[:, None] == jnp.arange(N_KV_HEADS * HEAD_DIM)[None, :]).astype(BF16)


def _row_replication(n):
    return (jnp.arange(n * N_HEADS)[:, None] // N_HEADS == jnp.arange(n)[None, :]).astype(BF16)


def kernel(x_prompt, x_sample, state_ssm_re, state_ssm_im, cache_k, cache_v, norm_mix, norm_mlp, ssm_a_re, ssm_a_im, ssm_log_dt, ssm_b_re, ssm_b_im, ssm_c_re, ssm_c_im, ssm_d, w_glu, b_glu, norm_kv, w_k, w_v, k_norm, w_q, q_norm, attn_sinks, w_o, w_mlp_in, w_mlp_out):
    bsz, seq, _ = x_prompt.shape
    dec = x_sample.shape[0]
    n_a = ssm_a_re.shape[0]
    depth = norm_mix.shape[0]
    past = 8192
    nkv = N_KV_HEADS * HEAD_DIM
    row = lambda v: v.astype(F32).reshape(1, -1)

    xp = x_prompt.astype(F32)
    xs = x_sample.reshape(dec, D_MODEL)
    rows3 = lambda v: v.astype(F32).reshape(v.shape[0], 1, -1)

    w1 = w_mlp_in.astype(BF16)
    w2 = w_mlp_out.astype(BF16)
    g_mlp = rows3(norm_mlp)
    mlp_p = lambda x, layer: _mlp(x.reshape(bsz * seq, D_MODEL), g_mlp, w1, w2, layer).reshape(bsz, seq, D_MODEL)

    b2, c2, m0, lam = _s5_weights(ssm_a_re, ssm_a_im, ssm_log_dt, ssm_b_re, ssm_b_im, ssm_c_re, ssm_c_im)
    b2_16 = b2.astype(BF16)
    wglu = w_glu.astype(BF16)
    g_mix, d_skip, bg = rows3(norm_mix), rows3(ssm_d), rows3(b_glu)
    sp_re, sp_im, ss_re, ss_im = [], [], [], []
    for i in range(n_a):
        xp, fin = _s5_prompt(xp, i, g_mix, b2_16, lam, c2, m0, d_skip, wglu, bg)
        re, im = _lanes_to_state(fin[bsz:])
        sp_re.append(re)
        sp_im.append(im)
        s0 = _state_to_lanes(state_ssm_re[i], state_ssm_im[i])
        xs, snew = _s5_sample(xs, i, g_mix, b2, lam, c2, d_skip, wglu, bg, s0)
        re, im = _lanes_to_state(snew)
        ss_re.append(re)
        ss_im.append(im)
        xp = mlp_p(xp, i)
        xs = _mlp(xs, g_mlp, w1, w2, i)

    cos_p, sin_p = _rope_tables(jnp.arange(seq, dtype=jnp.int32))
    cos_s, sin_s = _rope_tables(jnp.full((dec,), past, dtype=jnp.int32))
    ones_head = _block_ones(MXU_TILE, HEAD_DIM)
    place = _head_placement()
    rep = _row_replication(dec)
    kgain = row(k_norm)
    tile_gain = lambda gvec, n: jnp.tile(gvec, (1, n // HEAD_DIM))

    kd, vp, vt = _kv(xp.reshape(bsz * seq, D_MODEL), row(norm_kv), _dup_heads(w_k), w_v,
                     tile_gain(kgain, 2 * nkv), ones_head, cos_p, sin_p, lambda i: i % (seq // KV_ROWS), seq=seq)
    ks_new, vs_new = _kv(xs, row(norm_kv), w_k, w_v, tile_gain(kgain, nkv), ones_head, cos_s, sin_s,
                         lambda i: i)
    kd = kd.reshape(bsz, seq, 2 * nkv)
    new_k_p = kd[:, -WINDOW:].reshape(bsz, WINDOW, N_KV_HEADS, LANES // HEAD_DIM, HEAD_DIM)[:, :, :, 0]
    new_v_p = vp.reshape(bsz, seq, N_KV_HEADS, HEAD_DIM)[:, -WINDOW:]
    key_minor = lambda c: jnp.transpose(c.astype(F32), (0, 2, 3, 1)).reshape(dec, nkv, c.shape[1])
    keys_s, vals_s = _cache_append(key_minor(cache_k), key_minor(cache_v), ks_new.T, vs_new.T)
    key_major = lambda c: jnp.transpose(c.reshape(dec, N_KV_HEADS, HEAD_DIM, c.shape[2]), (0, 3, 1, 2))
    new_k_s, new_v_s = key_major(keys_s), key_major(vals_s)

    for j in range(depth - n_a):
        layer = n_a + j
        gn = row(norm_mix[layer])
        qgain = row(q_norm[j])
        wq = w_q[j].astype(BF16)
        wo = w_o[j].astype(BF16)
        xp = _attn_prompt(xp, gn, wq, tile_gain(qgain, D_MODEL), ones_head, cos_p, sin_p, kd, vt,
                          attn_sinks[j].astype(F32), wo)
        q_pad = _q_sample(xs, gn, wq, tile_gain(qgain, D_MODEL), ones_head, cos_s, sin_s, rep, place)
        o_pad = _attn_sample(q_pad.reshape(dec, N_HEADS, nkv), keys_s, vals_s,
                             attn_sinks[j].astype(F32).reshape(N_HEADS, 1))
        xs = _proj_residual(xs, o_pad.reshape(dec * N_HEADS, nkv), place.T, rep.T, wo)
        xp = mlp_p(xp, layer)
        xs = _mlp(xs, g_mlp, w1, w2, layer)

    return (xp, xs.reshape(dec, 1, D_MODEL),
            jnp.stack(sp_re), jnp.stack(sp_im), new_k_p, new_v_p,
            jnp.stack(ss_re), jnp.stack(ss_im), new_k_s, new_v_s)
```

```python
import functools
import math

import jax
import jax.numpy as jnp
from jax import lax
from jax.experimental import pallas as pl
from jax.experimental.pallas import tpu as pltpu

F32 = jnp.float32
BF16 = jnp.bfloat16

D_MODEL = 1024
N_GROUPS = 64
GROUP_SIZE = 16
STATE_DIM = 64
HEAD_DIM = 64
N_HEADS = 16
N_KV_HEADS = 4
WINDOW = 128
ROPE_THETA = 10000.0
D_FF = 4 * D_MODEL
EPS = 1e-6

SUBLANES = 8
LANES = 128
MXU_TILE = 256
VMEM_LIMIT_BYTES = 56 * 1024 * 1024

N_OCTETS = D_MODEL // LANES
OCTET_STATE = (LANES // GROUP_SIZE) * STATE_DIM
STATE_LANES = 2 * OCTET_STATE * N_OCTETS

S5_TIME_CHUNK = 128
MLP_ROWS = 1024
FF_CHUNK = 1024
KV_ROWS = 1024
ATTN_BLOCKS = 8


def _const_spec(shape):
    zeros = (0,) * len(shape)
    return pl.BlockSpec(shape, lambda *_: zeros, pipeline_mode=pl.Buffered(1))


def _layer_spec(shape, layer):
    idx = (layer,) + (0,) * (len(shape) - 1)
    return pl.BlockSpec((None,) + tuple(shape[1:]), lambda *_: idx, pipeline_mode=pl.Buffered(1))


def _params(sem):
    return pltpu.CompilerParams(dimension_semantics=sem, vmem_limit_bytes=VMEM_LIMIT_BYTES)


def _rms(x, g):
    return x * lax.rsqrt(jnp.mean(x * x, axis=-1, keepdims=True) + EPS) * g


def _dot(a, b):
    return jnp.dot(a, b, preferred_element_type=F32)


def _seg_mean_sq(x, ones_ref):
    sq = x * x
    hi = sq.astype(BF16)
    lo = (sq - hi.astype(F32)).astype(BF16)
    ones = ones_ref[...]
    outs = []
    for t in range(x.shape[-1] // MXU_TILE):
        sl = slice(MXU_TILE * t, MXU_TILE * (t + 1))
        outs.append(_dot(hi[:, sl], ones) + _dot(lo[:, sl], ones))
    return jnp.concatenate(outs, axis=-1) * (1.0 / HEAD_DIM)


def _tile_lanes(t, n):
    return jnp.concatenate([t] * (n // t.shape[-1]), axis=-1)


def _rope(x, cos, sin_signed):
    n = x.shape[-1]
    lane = lax.broadcasted_iota(jnp.int32, x.shape, 1)
    first = (lane & (HEAD_DIM - 1)) < (HEAD_DIM // 2)
    partner = jnp.where(first, pltpu.roll(x, n - HEAD_DIM // 2, 1), pltpu.roll(x, HEAD_DIM // 2, 1))
    return x * cos + partner * sin_signed


def _mlp_kernel(x_ref, g_ref, w1_ref, w2_ref, o_ref):
    x = x_ref[...]
    h = _rms(x, g_ref[...]).astype(BF16)
    acc = x
    for c in range(D_FF // FF_CHUNK):
        sl = slice(FF_CHUNK * c, FF_CHUNK * (c + 1))
        a = jnp.square(jnp.maximum(_dot(h, w1_ref[:, sl]), 0.0)).astype(BF16)
        acc = acc + _dot(a, w2_ref[sl, :])
    o_ref[...] = acc


def _mlp(x, g, w1, w2, layer):
    rows = x.shape[0]
    tm = min(MLP_ROWS, rows)
    return pl.pallas_call(
        _mlp_kernel,
        grid=(rows // tm,),
        in_specs=[pl.BlockSpec((tm, D_MODEL), lambda i: (i, 0)),
                  _layer_spec(g.shape, layer),
                  _layer_spec(w1.shape, layer),
                  _layer_spec(w2.shape, layer)],
        out_specs=pl.BlockSpec((tm, D_MODEL), lambda i: (i, 0)),
        out_shape=jax.ShapeDtypeStruct((rows, D_MODEL), F32),
        compiler_params=_params(("parallel",)),
        name="mlp",
    )(x, g, w1, w2)


def _s5_gate(y, wglu_ref, bglu_ref):
    g = jax.nn.gelu(y).astype(BF16)
    z = _dot(g, wglu_ref[...]) + bglu_ref[...]
    return z[:, :D_MODEL] * jax.nn.sigmoid(z[:, D_MODEL:])


def _slabs(ref, rows=slice(None)):
    return jnp.concatenate([ref[c, rows, :] for c in range(ref.shape[0])], axis=-1)


def _s5_prompt_kernel(xa_ref, xb_ref, gn_ref, b2_ref, lam_ref, c2_ref, m0_ref, d_ref, wglu_ref, bglu_ref,
                      o_ref, fin_ref, st_ref, carry_ref, nat_ref, ue_ref, uo_ref, up_ref, tb2_ref, nat2_ref, y_ref):
    batch, tc, _ = xa_ref.shape
    assert 2 * batch == SUBLANES
    n_slabs = D_MODEL // LANES
    pairs = tc // 2
    rows_k = pairs * batch
    step = pl.program_id(0)
    n_chunks = pl.num_programs(0) - 1
    live = step < n_chunks

    @pl.when(step == 0)
    def _():
        carry_ref[...] = jnp.zeros_like(carry_ref)
        y_ref[...] = jnp.zeros_like(y_ref)
        up_ref[...] = jnp.zeros_like(up_ref)

    upd = _s5_gate(y_ref[...], wglu_ref, bglu_ref)
    for c in range(n_slabs):
        tb2_ref[c] = upd[:, LANES * c:LANES * (c + 1)]
    for b in range(batch):
        for c in range(n_slabs):
            nat2_ref[c, pl.ds(0, pairs, stride=2), :] = tb2_ref[c, pl.ds(b, pairs, stride=batch), :]
            nat2_ref[c, pl.ds(1, pairs, stride=2), :] = tb2_ref[c, pl.ds(rows_k + b, pairs, stride=batch), :]
        o_ref[b] = xb_ref[b] + _slabs(nat2_ref)

    for c in range(n_slabs):
        up_ref[c, 0:batch, :] = up_ref[c, rows_k:rows_k + batch, :]
    for b in range(batch):
        u_nat = _rms(xa_ref[b], gn_ref[...])
        for c in range(n_slabs):
            nat_ref[c] = u_nat[:, LANES * c:LANES * (c + 1)]
        for c in range(n_slabs):
            even = nat_ref[c, pl.ds(0, pairs, stride=2), :]
            odd = nat_ref[c, pl.ds(1, pairs, stride=2), :]
            ue_ref[c, pl.ds(b, pairs, stride=batch), :] = even
            uo_ref[c, pl.ds(b, pairs, stride=batch), :] = odd
            up_ref[c, pl.ds(batch + b, pairs, stride=batch), :] = odd
    ue, uo = _slabs(ue_ref), _slabs(uo_ref)
    ue_b, uo_b, up_b = ue.astype(BF16), uo.astype(BF16), _slabs(up_ref, slice(0, rows_k)).astype(BF16)

    lo = lax.broadcasted_iota(jnp.int32, (SUBLANES, LANES), 0) < batch
    swap = lambda v: pltpu.roll(v, batch, 0)
    n_vregs = rows_k // SUBLANES
    half = OCTET_STATE // LANES
    y_even, y_odd = [], []
    for q0 in range(0, N_OCTETS, 2):
        octets = (q0, q0 + 1)
        for q in octets:
            ch = slice(LANES * q, LANES * (q + 1))
            lhs = jnp.concatenate([up_b[:, ch], ue_b[:, ch]], axis=1)
            st_ref[:, 2 * OCTET_STATE * q:2 * OCTET_STATE * (q + 1)] = _dot(lhs, b2_ref[q])
        chains = []
        for q in octets:
            for j in range(half // 2):
                cols = [2 * OCTET_STATE * q + LANES * jj for jj in (j, j + half // 2)]
                lanes = [OCTET_STATE * q + LANES * jj for jj in (j, j + half // 2)]
                chains.append((cols, lanes))
        lam_r, lam_i, sr, si, old = [], [], [], [], []
        for (ca, cb), (la, lb) in chains:
            bc = lambda r, c: jnp.broadcast_to(lam_ref[r:r + 1, c:c + LANES], (SUBLANES, LANES))
            lam_r.append(jnp.where(lo, bc(2, la), bc(2, lb)))
            lam_i.append(jnp.where(lo, bc(3, la), bc(3, lb)))
            prev = [carry_ref[:, c:c + LANES] for c in (ca, cb, ca + OCTET_STATE, cb + OCTET_STATE)]
            old.append(prev)
            sr.append(jnp.where(lo, swap(prev[0]), prev[1]))
            si.append(jnp.where(lo, swap(prev[2]), prev[3]))
        last = [None] * len(chains)
        for m in range(n_vregs):
            rows = slice(SUBLANES * m, SUBLANES * (m + 1))
            for j, ((ca, cb), _) in enumerate(chains):
                ar, br = st_ref[rows, ca:ca + LANES], st_ref[rows, cb:cb + LANES]
                ai, bi = (st_ref[rows, ca + OCTET_STATE:ca + OCTET_STATE + LANES],
                          st_ref[rows, cb + OCTET_STATE:cb + OCTET_STATE + LANES])
                d0r, d1r = jnp.where(lo, ar, swap(br)), jnp.where(lo, swap(ar), br)
                d0i, d1i = jnp.where(lo, ai, swap(bi)), jnp.where(lo, swap(ai), bi)
                s0r = lam_r[j] * sr[j] - lam_i[j] * si[j] + d0r
                s0i = lam_r[j] * si[j] + lam_i[j] * sr[j] + d0i
                s1r = lam_r[j] * s0r - lam_i[j] * s0i + d1r
                s1i = lam_r[j] * s0i + lam_i[j] * s0r + d1i
                sr[j], si[j] = s1r, s1i
                out = (jnp.where(lo, s0r, swap(s1r)), jnp.where(lo, swap(s0r), s1r),
                       jnp.where(lo, s0i, swap(s1i)), jnp.where(lo, swap(s0i), s1i))
                st_ref[rows, ca:ca + LANES] = out[0]
                st_ref[rows, cb:cb + LANES] = out[1]
                st_ref[rows, ca + OCTET_STATE:ca + OCTET_STATE + LANES] = out[2]
                st_ref[rows, cb + OCTET_STATE:cb + OCTET_STATE + LANES] = out[3]
                last[j] = out
        for j, ((ca, cb), _) in enumerate(chains):
            for c, new, prev in zip((ca, cb, ca + OCTET_STATE, cb + OCTET_STATE), last[j], old[j]):
                carry_ref[:, c:c + LANES] = jnp.where(live, new, prev)
        for q in octets:
            ch = slice(LANES * q, LANES * (q + 1))
            y2 = _dot(st_ref[:, 2 * OCTET_STATE * q:2 * OCTET_STATE * (q + 1)].astype(BF16), c2_ref[q])
            y_even.append(y2[:, :LANES])
            y_odd.append(y2[:, LANES:] + _dot(uo_b[:, ch], m0_ref[q]))
    y_ref[0:rows_k, :] = jnp.concatenate(y_even, axis=-1) + d_ref[...] * ue
    y_ref[rows_k:2 * rows_k, :] = jnp.concatenate(y_odd, axis=-1) + d_ref[...] * uo

    @pl.when(step == n_chunks - 1)
    def _():
        tail = uo_b[rows_k - 2 * SUBLANES:, :]
        zeros = jnp.zeros((2 * SUBLANES, LANES), BF16)
        for q in range(N_OCTETS):
            bu = _dot(jnp.concatenate([zeros, tail[:, LANES * q:LANES * (q + 1)]], axis=1), b2_ref[q])[SUBLANES:]
            base = 2 * OCTET_STATE * q
            lr = lam_ref[0:1, OCTET_STATE * q:OCTET_STATE * (q + 1)]
            li = lam_ref[1:2, OCTET_STATE * q:OCTET_STATE * (q + 1)]
            s_r = carry_ref[:, base:base + OCTET_STATE]
            s_i = carry_ref[:, base + OCTET_STATE:base + 2 * OCTET_STATE]
            fin_ref[:, base:base + OCTET_STATE] = lr * s_r - li * s_i + bu[:, :OCTET_STATE]
            fin_ref[:, base + OCTET_STATE:base + 2 * OCTET_STATE] = lr * s_i + li * s_r + bu[:, OCTET_STATE:]


def _s5_prompt(x, layer, gn, b2, lam, c2, m0, d, wglu, bglu):
    batch, seq, _ = x.shape
    tc = S5_TIME_CHUNK
    n_chunks = seq // tc
    rows_k = tc // 2 * batch
    n_slabs = D_MODEL // LANES
    blk = (batch, tc, D_MODEL)
    this_chunk = pl.BlockSpec(blk, lambda s: (0, jnp.minimum(s, n_chunks - 1), 0))
    prev_chunk = pl.BlockSpec(blk, lambda s: (0, jnp.maximum(s - 1, 0), 0))
    slab = lambda rows: pltpu.VMEM((n_slabs, rows, LANES), F32)
    return pl.pallas_call(
        _s5_prompt_kernel,
        grid=(n_chunks + 1,),
        in_specs=[this_chunk, prev_chunk]
        + [_layer_spec(w.shape, layer) for w in (gn, b2, lam, c2, m0, d, wglu, bglu)],
        out_specs=[prev_chunk, pl.BlockSpec((SUBLANES, STATE_LANES), lambda s: (0, 0))],
        out_shape=[jax.ShapeDtypeStruct(x.shape, F32),
                   jax.ShapeDtypeStruct((SUBLANES, STATE_LANES), F32)],
        scratch_shapes=[pltpu.VMEM((rows_k, STATE_LANES), F32),
                        pltpu.VMEM((SUBLANES, STATE_LANES), F32),
                        slab(tc), slab(rows_k), slab(rows_k), slab(rows_k + SUBLANES),
                        slab(2 * rows_k), slab(tc),
                        pltpu.VMEM((2 * rows_k, D_MODEL), F32)],
        compiler_params=_params(("arbitrary",)),
        name="s5_prompt",
    )(x, x, gn, b2, lam, c2, m0, d, wglu, bglu)


def _s5_sample_kernel(x_ref, gn_ref, b2_ref, lam_ref, c2_ref, d_ref, wglu_ref, bglu_ref, s0_ref,
                      o_ref, snew_ref):
    x = x_ref[...]
    u = _rms(x, gn_ref[...])
    ys = []
    for q in range(N_OCTETS):
        base = 2 * OCTET_STATE * q
        bu = jnp.dot(u[:, LANES * q:LANES * (q + 1)], b2_ref[q, LANES:, :],
                     preferred_element_type=F32, precision=lax.Precision.HIGHEST)
        lr = lam_ref[0:1, OCTET_STATE * q:OCTET_STATE * (q + 1)]
        li = lam_ref[1:2, OCTET_STATE * q:OCTET_STATE * (q + 1)]
        s0r = s0_ref[:, base:base + OCTET_STATE]
        s0i = s0_ref[:, base + OCTET_STATE:base + 2 * OCTET_STATE]
        sr = lr * s0r - li * s0i + bu[:, :OCTET_STATE]
        si = lr * s0i + li * s0r + bu[:, OCTET_STATE:]
        snew_ref[:, base:base + OCTET_STATE] = sr
        snew_ref[:, base + OCTET_STATE:base + 2 * OCTET_STATE] = si
        sb = jnp.concatenate([sr, si], axis=-1).astype(BF16)
        ys.append(_dot(sb, c2_ref[q, :, :LANES]))
    y = jnp.concatenate(ys, axis=-1)
    o_ref[...] = x + _s5_gate(y + d_ref[...] * u, wglu_ref, bglu_ref)


def _s5_sample(x, layer, gn, b2_f32, lam, c2, d, wglu, bglu, s0):
    rows = x.shape[0]
    return pl.pallas_call(
        _s5_sample_kernel,
        grid=(1,),
        in_specs=[_const_spec(x.shape)] + [_layer_spec(w.shape, layer) for w in (gn, b2_f32, lam, c2, d, wglu, bglu)]
        + [_const_spec(s0.shape)],
        out_specs=[pl.BlockSpec((rows, D_MODEL), lambda i: (0, 0)),
                   pl.BlockSpec((rows, STATE_LANES), lambda i: (0, 0))],
        out_shape=[jax.ShapeDtypeStruct((rows, D_MODEL), F32),
                   jax.ShapeDtypeStruct((rows, STATE_LANES), F32)],
        compiler_params=_params(("arbitrary",)),
        name="s5_sample",
    )(x, gn, b2_f32, lam, c2, d, wglu, bglu, s0)


def _block_diag(w):
    n, octets, groups, r, c = w.shape
    col = jnp.arange(groups * c)
    tile = (jnp.arange(c)[:, None] == col % c).astype(F32)
    own = (jnp.arange(groups * r)[:, None] // r == col // c).astype(F32)
    tiled = jnp.einsum("nqrc,cl->nqrl", w.reshape(n, octets, groups * r, c), tile, precision=lax.Precision.HIGHEST)
    return tiled * own


def _s5_weights(a_re, a_im, log_dt, b_re, b_im, c_re, c_im):
    n = a_re.shape[0]
    dt = jnp.exp(log_dt.astype(F32))
    lam = lax.complex(a_re.astype(F32), a_im.astype(F32))
    lam_bar = jnp.exp(lam * dt)
    b = lax.complex(b_re.astype(F32), b_im.astype(F32))
    b_bar = ((lam_bar - 1.0) / lam)[..., None] * b
    c = lax.complex(c_re.astype(F32), c_im.astype(F32))
    gpo = N_GROUPS // N_OCTETS
    octets = lambda w: w.reshape((n, N_OCTETS, gpo) + w.shape[2:])

    def into_state(w):
        wt = jnp.swapaxes(octets(w), 3, 4)
        return jnp.concatenate([_block_diag(jnp.real(wt)), _block_diag(jnp.imag(wt))], axis=-1)

    def out_of_state(w):
        wt = jnp.swapaxes(octets(w), 3, 4)
        return jnp.concatenate([_block_diag(jnp.real(wt)), -_block_diag(jnp.imag(wt))], axis=2)

    b2 = jnp.concatenate([into_state(lam_bar[..., None] * b_bar), into_state(b_bar)], axis=2)
    c2 = jnp.concatenate([out_of_state(c), out_of_state(c * lam_bar[:, :, None, :])], axis=-1)
    cb = jnp.real(jnp.einsum("ngcp,ngpd->ngdc", c, b_bar))
    m0 = _block_diag(octets(cb))
    lam2 = lam_bar * lam_bar
    lam_rows = jnp.stack([jnp.real(lam_bar).reshape(n, -1), jnp.imag(lam_bar).reshape(n, -1),
                          jnp.real(lam2).reshape(n, -1), jnp.imag(lam2).reshape(n, -1)], axis=1)
    return b2, c2.astype(BF16), m0.astype(BF16), lam_rows


def _state_to_lanes(s_re, s_im):
    n = s_re.shape[0]
    re = s_re.astype(F32).reshape(n, N_OCTETS, OCTET_STATE)
    im = s_im.astype(F32).reshape(n, N_OCTETS, OCTET_STATE)
    return jnp.concatenate([re, im], axis=-1).reshape(n, STATE_LANES)


def _lanes_to_state(s):
    n = s.shape[0]
    s4 = s.reshape(n, N_OCTETS, 2, OCTET_STATE)
    return (s4[:, :, 0].reshape(n, N_GROUPS, STATE_DIM), s4[:, :, 1].reshape(n, N_GROUPS, STATE_DIM))


def _rope_partner(n):
    lane = jnp.arange(n)
    return jnp.where(lane % HEAD_DIM < HEAD_DIM // 2, lane + HEAD_DIM // 2, lane - HEAD_DIM // 2)


def _kv_kernel(x_ref, g_ref, wkv_ref, kg_ref, kgp_ref, ones_ref, cos_ref, sin_ref, *rest):
    if len(rest) == 3:
        wvt_ref, k_ref, vt_ref = rest
        v_ref = None
    else:
        (k_ref, v_ref), wvt_ref, vt_ref = rest, None, None
    h = _rms(x_ref[...], g_ref[...]).astype(BF16)
    kv = _dot(h, wkv_ref[...])
    n = k_ref.shape[-1]
    k, k_partner = kv[:, :n], kv[:, n:2 * n]
    scale = lax.rsqrt(_seg_mean_sq(k, ones_ref) + EPS)
    k_ref[...] = scale * (k * (kg_ref[...] * _tile_lanes(cos_ref[...], n))
                          + k_partner * (kgp_ref[...] * _tile_lanes(sin_ref[...], n)))
    if v_ref is not None:
        v_ref[...] = kv[:, 2 * n:]
    else:
        vt_ref[...] = lax.dot_general(wvt_ref[...], h, (((1,), (1,)), ((), ())), preferred_element_type=F32)


def _kv(x, g, wk, wv, kgain, ones, cos, sin, table_block, seq=None):
    rows = x.shape[0]
    n, nv = wk.shape[1], wv.shape[1]
    partner = _rope_partner(n)
    wkv = jnp.concatenate([wk, wk[:, partner]] + ([wv] if seq is None else []), axis=1).astype(BF16)
    tm = min(KV_ROWS, rows)
    in_specs = [pl.BlockSpec((tm, D_MODEL), lambda i: (i, 0)),
                _const_spec((1, D_MODEL)),
                _const_spec(wkv.shape),
                _const_spec((1, n)),
                _const_spec((1, n)),
                _const_spec(ones.shape),
                pl.BlockSpec((tm, LANES), lambda i: (table_block(i), 0)),
                pl.BlockSpec((tm, LANES), lambda i: (table_block(i), 0))]
    out_specs = [pl.BlockSpec((tm, n), lambda i: (i, 0))]
    out_shape = [jax.ShapeDtypeStruct((rows, n), F32)]
    args = [x, g, wkv, kgain, kgain[:, partner], ones, cos, sin]
    if seq is None:
        out_specs.append(pl.BlockSpec((tm, nv), lambda i: (i, 0)))
        out_shape.append(jax.ShapeDtypeStruct((rows, nv), F32))
    else:
        per_seq = seq // tm
        in_specs.append(_const_spec((nv, D_MODEL)))
        args.append(wv.T.astype(BF16))
        out_specs.append(pl.BlockSpec((None, nv, tm), lambda i: (i // per_seq, 0, i % per_seq)))
        out_shape.append(jax.ShapeDtypeStruct((rows // seq, nv, seq), F32))
    return pl.pallas_call(
        _kv_kernel,
        grid=(rows // tm,),
        in_specs=in_specs,
        out_specs=out_specs,
        out_shape=out_shape,
        compiler_params=_params(("parallel",)),
        name="kv_proj",
    )(*args)


def _attn_prompt_kernel(sinks_ref, x_ref, g_ref, wq_ref, qg_ref, ones_ref, cos_ref, sin_ref,
                        kp_ref, kc_ref, vp_ref, vc_ref, wo_ref, o_ref):
    tile = pl.program_id(1)
    x = x_ref[...]
    n_blocks = x.shape[0] // WINDOW
    heads_per_kv = N_HEADS // N_KV_HEADS
    cols = heads_per_kv * WINDOW
    h = _rms(x, g_ref[...]).astype(BF16)
    q = _dot(h, wq_ref[...])
    q = q * lax.rsqrt(_seg_mean_sq(q, ones_ref) + EPS) * qg_ref[...]
    q = _rope(q, _tile_lanes(cos_ref[...], D_MODEL), _tile_lanes(sin_ref[...], D_MODEL))
    qb = (q * (HEAD_DIM ** -0.5)).astype(BF16)

    kall = jnp.concatenate([kp_ref[...], kc_ref[...]], axis=0).astype(BF16)
    vall = jnp.concatenate([vp_ref[...], vc_ref[...]], axis=1).astype(BF16)

    kj = lax.broadcasted_iota(jnp.int32, (WINDOW, cols), 0)
    col = lax.broadcasted_iota(jnp.int32, (WINDOW, cols), 1)
    from_prev = kj > (col & (WINDOW - 1))
    col_head = lax.broadcasted_iota(jnp.int32, (1, cols), 1) // WINDOW
    low_half = lax.broadcasted_iota(jnp.int32, (WINDOW, LANES), 1) < HEAD_DIM
    neg_inf = jnp.full((WINDOW, cols), -jnp.inf, F32)

    pairs = [(blk, g) for blk in range(n_blocks) for g in range(N_KV_HEADS)]
    scores, sinks = [], []
    for blk, g in pairs:
        r0 = WINDOW * blk
        kg = kall[r0:r0 + 2 * WINDOW, LANES * g:LANES * (g + 1)]
        qs = []
        sink = jnp.zeros((1, cols), F32)
        for hh in range(heads_per_kv):
            head = heads_per_kv * g + hh
            qp = qb[r0:r0 + WINDOW, LANES * (head // 2):LANES * (head // 2 + 1)]
            keep = low_half if head % 2 == 0 else jnp.logical_not(low_half)
            qs.append(jnp.where(keep, qp, jnp.zeros_like(qp)))
            sink = jnp.where(col_head == hh, sinks_ref[head], sink)
        sinks.append(sink)
        scores.append(lax.dot_general(kg, jnp.concatenate(qs, axis=0), (((1,), (1,)), ((), ())),
                                      preferred_element_type=F32))
    probs, denoms = [], []
    for (blk, g), s, sink in zip(pairs, scores, sinks):
        has_prev = tile * n_blocks + blk > 0
        t = jnp.where(from_prev, jnp.where(has_prev, s[:WINDOW], neg_inf), s[WINDOW:])
        m = jnp.maximum(jnp.max(t, axis=0, keepdims=True), sink)
        p = jnp.exp(t - m)
        denoms.append(jnp.sum(p, axis=0, keepdims=True) + jnp.exp(sink - m))
        zero = jnp.zeros_like(p)
        probs.append(jnp.concatenate([jnp.where(from_prev, p, zero), jnp.where(from_prev, zero, p)],
                                     axis=0).astype(BF16))
    head_outs = [[] for _ in range(n_blocks)]
    for (blk, g), pcat, denom in zip(pairs, probs, denoms):
        r0 = WINDOW * blk
        vg = vall[HEAD_DIM * g:HEAD_DIM * (g + 1), r0:r0 + 2 * WINDOW]
        o = _dot(vg, pcat) / denom
        head_outs[blk] += [o[:, WINDOW * hh:WINDOW * (hh + 1)] for hh in range(heads_per_kv)]
    block_outs = [jnp.concatenate(outs, axis=0).T for outs in head_outs]
    attn = jnp.concatenate(block_outs, axis=0).astype(BF16)
    o_ref[...] = x + _dot(attn, wo_ref[...])


def _attn_prompt(x, g, wq, qgain, ones, cos, sin, kd, vt, sinks, wo):
    bsz, seq, _ = x.shape
    nk, nv = kd.shape[-1], vt.shape[1]
    nb = ATTN_BLOCKS
    tq = nb * WINDOW
    x_spec = pl.BlockSpec((None, tq, D_MODEL), lambda b, i, *_: (b, i, 0))
    k_cur = pl.BlockSpec((None, tq, nk), lambda b, i, *_: (b, i, 0))
    k_prev = pl.BlockSpec((None, WINDOW, nk), lambda b, i, *_: (b, jnp.maximum(nb * i - 1, 0), 0))
    v_cur = pl.BlockSpec((None, nv, tq), lambda b, i, *_: (b, 0, i))
    v_prev = pl.BlockSpec((None, nv, WINDOW), lambda b, i, *_: (b, 0, jnp.maximum(nb * i - 1, 0)))
    tab = pl.BlockSpec((tq, LANES), lambda b, i, *_: (i, 0))
    grid_spec = pltpu.PrefetchScalarGridSpec(
        num_scalar_prefetch=1,
        grid=(bsz, seq // tq),
        in_specs=[x_spec,
                  _const_spec((1, D_MODEL)),
                  _const_spec(wq.shape),
                  _const_spec((1, D_MODEL)),
                  _const_spec(ones.shape),
                  tab, tab, k_prev, k_cur, v_prev, v_cur,
                  _const_spec(wo.shape)],
        out_specs=x_spec)
    return pl.pallas_call(
        _attn_prompt_kernel,
        grid_spec=grid_spec,
        out_shape=jax.ShapeDtypeStruct(x.shape, F32),
        compiler_params=_params(("parallel", "arbitrary")),
        name="attn_prompt",
    )(sinks, x, g, wq, qgain, ones, cos, sin, kd, kd, vt, vt, wo)


def _own_head(shape):
    row_head = lax.broadcasted_iota(jnp.int32, shape, 0) & (N_HEADS - 1)
    lane_head = lax.broadcasted_iota(jnp.int32, shape, 1) // HEAD_DIM
    return row_head == lane_head


def _q_sample_kernel(x_ref, g_ref, wq_ref, qg_ref, ones_ref, cos_ref, sin_ref, rep_ref, place_ref, q_ref):
    h = _rms(x_ref[...], g_ref[...]).astype(BF16)
    q = _dot(h, wq_ref[...])
    q = q * lax.rsqrt(_seg_mean_sq(q, ones_ref) + EPS) * qg_ref[...]
    q = _rope(q, _tile_lanes(cos_ref[...], D_MODEL), _tile_lanes(sin_ref[...], D_MODEL))
    qb = (q * (HEAD_DIM ** -0.5)).astype(BF16)
    q_rep = _dot(rep_ref[...], qb)
    q_own = jnp.where(_own_head(q_rep.shape), q_rep, 0.0).astype(BF16)
    q_ref[...] = _dot(q_own, place_ref[...])


def _q_sample(x, g, wq, qgain, ones, cos, sin, rep, place):
    rows = x.shape[0]
    return pl.pallas_call(
        _q_sample_kernel,
        out_shape=jax.ShapeDtypeStruct((rows * N_HEADS, place.shape[1]), F32),
        compiler_params=pltpu.CompilerParams(vmem_limit_bytes=VMEM_LIMIT_BYTES),
        name="q_sample",
    )(x, g, wq, qgain, ones, cos, sin, rep, place)


def _cache_append_kernel(k_ref, v_ref, kn_ref, vn_ref, ko_ref, vo_ref):
    nb, lanes_kv, keys = k_ref.shape
    first = pl.program_id(0) * nb
    key = lax.broadcasted_iota(jnp.int32, (lanes_kv, keys), 1)
    seq = lax.broadcasted_iota(jnp.int32, (kn_ref.shape[1], keys), 0)

    def split3(x):
        hi = x.astype(BF16)
        r = x - hi.astype(F32)
        mid = r.astype(BF16)
        return hi, mid, (r - mid.astype(F32)).astype(BF16)

    parts = [(split3(kn_ref[...]), k_ref, ko_ref), (split3(vn_ref[...]), v_ref, vo_ref)]
    for b in range(nb):
        pick = (seq == first + b).astype(BF16)
        for (hi, mid, lo), old_ref, out_ref in parts:
            new_col = _dot(hi, pick) + _dot(mid, pick) + _dot(lo, pick)
            out_ref[b] = jnp.where(key == keys - 1, new_col, pltpu.roll(old_ref[b], keys - 1, 1))


def _cache_append(kt, vt, kn_t, vn_t):
    bsz, lanes_kv, keys = kt.shape
    nb = SUBLANES
    blk = pl.BlockSpec((nb, lanes_kv, keys), lambda i: (i, 0, 0))
    return pl.pallas_call(
        _cache_append_kernel,
        grid=(bsz // nb,),
        in_specs=[blk, blk, _const_spec(kn_t.shape), _const_spec(vn_t.shape)],
        out_specs=[blk, blk],
        out_shape=[jax.ShapeDtypeStruct(kt.shape, F32), jax.ShapeDtypeStruct(vt.shape, F32)],
        compiler_params=_params(("parallel",)),
        name="cache_append",
    )(kt, vt, kn_t, vn_t)


def _attn_sample_kernel(q_ref, k_ref, v_ref, sink_ref, o_ref):
    sink = sink_ref[...]
    for b in range(q_ref.shape[0]):
        s = _dot(q_ref[b].astype(BF16), k_ref[b].astype(BF16))
        m = jnp.maximum(jnp.max(s, axis=-1, keepdims=True), sink)
        p = jnp.exp(s - m)
        denom = jnp.sum(p, axis=-1, keepdims=True) + jnp.exp(sink - m)
        o = lax.dot_general(p.astype(BF16), v_ref[b].astype(BF16), (((1,), (1,)), ((), ())),
                            preferred_element_type=F32)
        o_ref[b] = o / denom


def _attn_sample(q3, kt, vt, sink_col):
    bsz, nh, nkv = q3.shape
    keys = kt.shape[2]
    bb = SUBLANES
    return pl.pallas_call(
        _attn_sample_kernel,
        grid=(bsz // bb,),
        in_specs=[pl.BlockSpec((bb, nh, nkv), lambda i: (i, 0, 0)),
                  pl.BlockSpec((bb, nkv, keys), lambda i: (i, 0, 0)),
                  pl.BlockSpec((bb, nkv, keys), lambda i: (i, 0, 0)),
                  _const_spec(sink_col.shape)],
        out_specs=pl.BlockSpec((bb, nh, nkv), lambda i: (i, 0, 0)),
        out_shape=jax.ShapeDtypeStruct((bsz, nh, nkv), F32),
        compiler_params=_params(("parallel",)),
        name="attn_sample",
    )(q3, kt, vt, sink_col)


def _proj_residual_kernel(x_ref, a_ref, place_t_ref, rep_t_ref, w_ref, o_ref):
    z = _dot(a_ref[...].astype(BF16), place_t_ref[...])
    z_own = jnp.where(_own_head(z.shape), z, 0.0).astype(BF16)
    attn = _dot(rep_t_ref[...], z_own).astype(BF16)
    o_ref[...] = x_ref[...] + _dot(attn, w_ref[...])


def _proj_residual(x, a, place_t, rep_t, w):
    return pl.pallas_call(
        _proj_residual_kernel,
        out_shape=jax.ShapeDtypeStruct(x.shape, F32),
        compiler_params=pltpu.CompilerParams(vmem_limit_bytes=VMEM_LIMIT_BYTES),
        name="o_proj_sample",
    )(x, a, place_t, rep_t, w)


def _rope_tables(pos):
    half = HEAD_DIM // 2
    inv = ROPE_THETA ** (-jnp.arange(half, dtype=F32) / half)
    ang = pos.astype(F32)[:, None] * inv[None, :]
    cos, sin = jnp.cos(ang), jnp.sin(ang)
    reps = LANES // HEAD_DIM
    return (jnp.tile(jnp.concatenate([cos, cos], axis=-1), (1, reps)),
            jnp.tile(jnp.concatenate([-sin, sin], axis=-1), (1, reps)))


def _block_ones(width, block):
    idx = jnp.arange(width) // block
    return (idx[:, None] == idx[None, :]).astype(BF16)


def _dup_heads(w):
    w3 = w.reshape(w.shape[0], N_KV_HEADS, 1, HEAD_DIM)
    return jnp.broadcast_to(w3, (w.shape[0], N_KV_HEADS, LANES // HEAD_DIM, HEAD_DIM)).reshape(w.shape[0], -1)


def _head_placement():
    lane = jnp.arange(N_HEADS * HEAD_DIM)
    dst = (lane // HEAD_DIM) // (N_HEADS // N_KV_HEADS) * HEAD_DIM + lane % HEAD_DIM
    return (dst[:, None] == jnp.arange(N_KV_HEADS * HEAD_DIM)[None, :]).astype(BF16)


def _row_replication(n):
    return (jnp.arange(n * N_HEADS)[:, None] // N_HEADS == jnp.arange(n)[None, :]).astype(BF16)


def kernel(x_prompt, x_sample, state_ssm_re, state_ssm_im, cache_k, cache_v, norm_mix, norm_mlp, ssm_a_re, ssm_a_im, ssm_log_dt, ssm_b_re, ssm_b_im, ssm_c_re, ssm_c_im, ssm_d, w_glu, b_glu, norm_kv, w_k, w_v, k_norm, w_q, q_norm, attn_sinks, w_o, w_mlp_in, w_mlp_out):
    bsz, seq, _ = x_prompt.shape
    dec = x_sample.shape[0]
    n_a = ssm_a_re.shape[0]
    depth = norm_mix.shape[0]
    past = 8192
    nkv = N_KV_HEADS * HEAD_DIM
    row = lambda v: v.astype(F32).reshape(1, -1)

    xp = x_prompt.astype(F32)
    xs = x_sample.reshape(dec, D_MODEL)
    rows3 = lambda v: v.astype(F32).reshape(v.shape[0], 1, -1)

    w1 = w_mlp_in.astype(BF16)
    w2 = w_mlp_out.astype(BF16)
    g_mlp = rows3(norm_mlp)
    mlp_p = lambda x, layer: _mlp(x.reshape(bsz * seq, D_MODEL), g_mlp, w1, w2, layer).reshape(bsz, seq, D_MODEL)

    b2, c2, m0, lam = _s5_weights(ssm_a_re, ssm_a_im, ssm_log_dt, ssm_b_re, ssm_b_im, ssm_c_re, ssm_c_im)
    b2_16 = b2.astype(BF16)
    wglu = w_glu.astype(BF16)
    g_mix, d_skip, bg = rows3(norm_mix), rows3(ssm_d), rows3(b_glu)
    sp_re, sp_im, ss_re, ss_im = [], [], [], []
    for i in range(n_a):
        xp, fin = _s5_prompt(xp, i, g_mix, b2_16, lam, c2, m0, d_skip, wglu, bg)
        re, im = _lanes_to_state(fin[bsz:])
        sp_re.append(re)
        sp_im.append(im)
        s0 = _state_to_lanes(state_ssm_re[i], state_ssm_im[i])
        xs, snew = _s5_sample(xs, i, g_mix, b2, lam, c2, d_skip, wglu, bg, s0)
        re, im = _lanes_to_state(snew)
        ss_re.append(re)
        ss_im.append(im)
        xp = mlp_p(xp, i)
        xs = _mlp(xs, g_mlp, w1, w2, i)

    cos_p, sin_p = _rope_tables(jnp.arange(seq, dtype=jnp.int32))
    cos_s, sin_s = _rope_tables(jnp.full((dec,), past, dtype=jnp.int32))
    ones_head = _block_ones(MXU_TILE, HEAD_DIM)
    place = _head_placement()
    rep = _row_replication(dec)
    kgain = row(k_norm)
    tile_gain = lambda gvec, n: jnp.tile(gvec, (1, n // HEAD_DIM))

    kd, vt = _kv(xp.reshape(bsz * seq, D_MODEL), row(norm_kv), _dup_heads(w_k), w_v,
                     tile_gain(kgain, 2 * nkv), ones_head, cos_p, sin_p, lambda i: i % (seq // KV_ROWS), seq=seq)
    ks_new, vs_new = _kv(xs, row(norm_kv), w_k, w_v, tile_gain(kgain, nkv), ones_head, cos_s, sin_s,
                         lambda i: i)
    kd = kd.reshape(bsz, seq, 2 * nkv)
    new_k_p = kd[:, -WINDOW:].reshape(bsz, WINDOW, N_KV_HEADS, LANES // HEAD_DIM, HEAD_DIM)[:, :, :, 0]
    new_v_p = jnp.transpose(vt[:, :, -WINDOW:].reshape(bsz, N_KV_HEADS, HEAD_DIM, WINDOW), (0, 3, 1, 2))
    key_minor = lambda c: jnp.transpose(c.astype(F32), (0, 2, 3, 1)).reshape(dec, nkv, c.shape[1])
    keys_s, vals_s = _cache_append(key_minor(cache_k), key_minor(cache_v), ks_new.T, vs_new.T)
    key_major = lambda c: jnp.transpose(c.reshape(dec, N_KV_HEADS, HEAD_DIM, c.shape[2]), (0, 3, 1, 2))
    new_k_s, new_v_s = key_major(keys_s), key_major(vals_s)

    for j in range(depth - n_a):
        layer = n_a + j
        gn = row(norm_mix[layer])
        qgain = row(q_norm[j])
        wq = w_q[j].astype(BF16)
        wo = w_o[j].astype(BF16)
        xp = _attn_prompt(xp, gn, wq, tile_gain(qgain, D_MODEL), ones_head, cos_p, sin_p, kd, vt,
                          attn_sinks[j].astype(F32), wo)
        q_pad = _q_sample(xs, gn, wq, tile_gain(qgain, D_MODEL), ones_head, cos_s, sin_s, rep, place)
        o_pad = _attn_sample(q_pad.reshape(dec, N_HEADS, nkv), keys_s, vals_s,
                             attn_sinks[j].astype(F32).reshape(N_HEADS, 1))
        xs = _proj_residual(xs, o_pad.reshape(dec * N_HEADS, nkv), place.T, rep.T, wo)
        xp = mlp_p(xp, layer)
        xs = _mlp(xs, g_mlp, w1, w2, layer)

    return (xp, xs.reshape(dec, 1, D_MODEL),
            jnp.stack(sp_re), jnp.stack(sp_im), new_k_p, new_v_p,
            jnp.stack(ss_re), jnp.stack(ss_im), new_k_s, new_v_s)
```

```python
import functools
import math

import jax
import jax.numpy as jnp
from jax import lax
from jax.experimental import pallas as pl
from jax.experimental.pallas import tpu as pltpu

F32 = jnp.float32
BF16 = jnp.bfloat16

D_MODEL = 1024
N_GROUPS = 64
GROUP_SIZE = 16
STATE_DIM = 64
HEAD_DIM = 64
N_HEADS = 16
N_KV_HEADS = 4
WINDOW = 128
ROPE_THETA = 10000.0
D_FF = 4 * D_MODEL
EPS = 1e-6

SUBLANES = 8
LANES = 128
MXU_TILE = 256
VMEM_LIMIT_BYTES = 56 * 1024 * 1024

N_OCTETS = D_MODEL // LANES
OCTET_STATE = (LANES // GROUP_SIZE) * STATE_DIM
STATE_LANES = 2 * OCTET_STATE * N_OCTETS

S5_TIME_CHUNK = 128
MLP_ROWS = 1024
FF_CHUNK = 1024
KV_ROWS = 1024
ATTN_BLOCKS = 8


def _const_spec(shape):
    zeros = (0,) * len(shape)
    return pl.BlockSpec(shape, lambda *_: zeros, pipeline_mode=pl.Buffered(1))


def _layer_spec(shape, layer):
    idx = (layer,) + (0,) * (len(shape) - 1)
    return pl.BlockSpec((None,) + tuple(shape[1:]), lambda *_: idx, pipeline_mode=pl.Buffered(1))


def _params(sem):
    return pltpu.CompilerParams(dimension_semantics=sem, vmem_limit_bytes=VMEM_LIMIT_BYTES)


def _rms(x, g):
    return x * lax.rsqrt(jnp.mean(x * x, axis=-1, keepdims=True) + EPS) * g


def _dot(a, b):
    return jnp.dot(a, b, preferred_element_type=F32)


def _seg_mean_sq(x, ones_ref):
    sq = x * x
    hi = sq.astype(BF16)
    lo = (sq - hi.astype(F32)).astype(BF16)
    ones = ones_ref[...]
    outs = []
    for t in range(x.shape[-1] // MXU_TILE):
        sl = slice(MXU_TILE * t, MXU_TILE * (t + 1))
        outs.append(_dot(hi[:, sl], ones) + _dot(lo[:, sl], ones))
    return jnp.concatenate(outs, axis=-1) * (1.0 / HEAD_DIM)


def _tile_lanes(t, n):
    return jnp.concatenate([t] * (n // t.shape[-1]), axis=-1)


def _rope(x, cos, sin_signed):
    n = x.shape[-1]
    lane = lax.broadcasted_iota(jnp.int32, x.shape, 1)
    first = (lane & (HEAD_DIM - 1)) < (HEAD_DIM // 2)
    partner = jnp.where(first, pltpu.roll(x, n - HEAD_DIM // 2, 1), pltpu.roll(x, HEAD_DIM // 2, 1))
    return x * cos + partner * sin_signed


def _mlp_kernel(x_ref, g_ref, w1_ref, w2_ref, o_ref):
    x = x_ref[...]
    h = _rms(x, g_ref[...]).astype(BF16)
    acc = x
    for c in range(D_FF // FF_CHUNK):
        sl = slice(FF_CHUNK * c, FF_CHUNK * (c + 1))
        a = jnp.square(jnp.maximum(_dot(h, w1_ref[:, sl]), 0.0)).astype(BF16)
        acc = acc + _dot(a, w2_ref[sl, :])
    o_ref[...] = acc


def _mlp(x, g, w1, w2, layer):
    rows = x.shape[0]
    tm = min(MLP_ROWS, rows)
    return pl.pallas_call(
        _mlp_kernel,
        grid=(rows // tm,),
        in_specs=[pl.BlockSpec((tm, D_MODEL), lambda i: (i, 0)),
                  _layer_spec(g.shape, layer),
                  _layer_spec(w1.shape, layer),
                  _layer_spec(w2.shape, layer)],
        out_specs=pl.BlockSpec((tm, D_MODEL), lambda i: (i, 0)),
        out_shape=jax.ShapeDtypeStruct((rows, D_MODEL), F32),
        compiler_params=_params(("parallel",)),
        name="mlp",
    )(x, g, w1, w2)


def _s5_gate(y, wglu_ref, bglu_ref):
    g = jax.nn.gelu(y).astype(BF16)
    z = _dot(g, wglu_ref[...]) + bglu_ref[...]
    return z[:, :D_MODEL] * jax.nn.sigmoid(z[:, D_MODEL:])


def _slabs(ref, rows=slice(None)):
    return jnp.concatenate([ref[c, rows, :] for c in range(ref.shape[0])], axis=-1)


def _s5_prompt_kernel(xa_ref, xb_ref, gn_ref, b2_ref, lam_ref, c2_ref, m0_ref, d_ref, wglu_ref, bglu_ref,
                      o_ref, fin_ref, st_ref, carry_ref, nat_ref, ue_ref, uo_ref, up_ref, tb2_ref, nat2_ref, y_ref):
    batch, tc, _ = xa_ref.shape
    assert 2 * batch == SUBLANES
    n_slabs = D_MODEL // LANES
    pairs = tc // 2
    rows_k = pairs * batch
    step = pl.program_id(0)
    n_chunks = pl.num_programs(0) - 1
    live = step < n_chunks

    @pl.when(step == 0)
    def _():
        carry_ref[...] = jnp.zeros_like(carry_ref)
        y_ref[...] = jnp.zeros_like(y_ref)
        up_ref[...] = jnp.zeros_like(up_ref)

    g_prev = jax.nn.gelu(y_ref[...]).astype(BF16)
    glu_cols = 2 * D_MODEL // (N_OCTETS // 2)
    z = []

    def glu_part(c):
        sl = slice(glu_cols * c, glu_cols * (c + 1))
        z.append(_dot(g_prev, wglu_ref[:, sl]) + bglu_ref[:, sl])

    def finish_prev():
        zs = jnp.concatenate(z, axis=-1)
        upd = zs[:, :D_MODEL] * jax.nn.sigmoid(zs[:, D_MODEL:])
        for c in range(n_slabs):
            tb2_ref[c] = upd[:, LANES * c:LANES * (c + 1)]
        for b in range(batch):
            for c in range(n_slabs):
                nat2_ref[c, pl.ds(0, pairs, stride=2), :] = tb2_ref[c, pl.ds(b, pairs, stride=batch), :]
                nat2_ref[c, pl.ds(1, pairs, stride=2), :] = tb2_ref[c, pl.ds(rows_k + b, pairs, stride=batch), :]
            o_ref[b] = xb_ref[b] + _slabs(nat2_ref)

    for c in range(n_slabs):
        up_ref[c, 0:batch, :] = up_ref[c, rows_k:rows_k + batch, :]
    for b in range(batch):
        u_nat = _rms(xa_ref[b], gn_ref[...])
        for c in range(n_slabs):
            nat_ref[c] = u_nat[:, LANES * c:LANES * (c + 1)]
        for c in range(n_slabs):
            even = nat_ref[c, pl.ds(0, pairs, stride=2), :]
            odd = nat_ref[c, pl.ds(1, pairs, stride=2), :]
            ue_ref[c, pl.ds(b, pairs, stride=batch), :] = even
            uo_ref[c, pl.ds(b, pairs, stride=batch), :] = odd
            up_ref[c, pl.ds(batch + b, pairs, stride=batch), :] = odd
    ue, uo = _slabs(ue_ref), _slabs(uo_ref)
    ue_b, uo_b, up_b = ue.astype(BF16), uo.astype(BF16), _slabs(up_ref, slice(0, rows_k)).astype(BF16)

    lo = lax.broadcasted_iota(jnp.int32, (SUBLANES, LANES), 0) < batch
    swap = lambda v: pltpu.roll(v, batch, 0)
    n_vregs = rows_k // SUBLANES
    half = OCTET_STATE // LANES
    y_even, y_odd = [], []
    for q in range(N_OCTETS):
        ch = slice(LANES * q, LANES * (q + 1))
        lhs = jnp.concatenate([up_b[:, ch], ue_b[:, ch]], axis=1)
        st_ref[:, 2 * OCTET_STATE * q:2 * OCTET_STATE * (q + 1)] = _dot(lhs, b2_ref[q])

    def project_out(octets):
        for q in octets:
            ch = slice(LANES * q, LANES * (q + 1))
            y2 = _dot(st_ref[:, 2 * OCTET_STATE * q:2 * OCTET_STATE * (q + 1)].astype(BF16), c2_ref[q])
            y_even.append(y2[:, :LANES])
            y_odd.append(y2[:, LANES:] + _dot(uo_b[:, ch], m0_ref[q]))

    for q0 in range(0, N_OCTETS, 2):
        octets = (q0, q0 + 1)
        glu_part(q0 // 2)
        if q0 > 0:
            project_out((q0 - 2, q0 - 1))
        if q0 == N_OCTETS - 2:
            finish_prev()
        chains = []
        for q in octets:
            for j in range(half // 2):
                cols = [2 * OCTET_STATE * q + LANES * jj for jj in (j, j + half // 2)]
                lanes = [OCTET_STATE * q + LANES * jj for jj in (j, j + half // 2)]
                chains.append((cols, lanes))
        lam_r, lam_i, sr, si, old = [], [], [], [], []
        for (ca, cb), (la, lb) in chains:
            bc = lambda r, c: jnp.broadcast_to(lam_ref[r:r + 1, c:c + LANES], (SUBLANES, LANES))
            lam_r.append(jnp.where(lo, bc(2, la), bc(2, lb)))
            lam_i.append(jnp.where(lo, bc(3, la), bc(3, lb)))
            prev = [carry_ref[:, c:c + LANES] for c in (ca, cb, ca + OCTET_STATE, cb + OCTET_STATE)]
            old.append(prev)
            sr.append(jnp.where(lo, swap(prev[0]), prev[1]))
            si.append(jnp.where(lo, swap(prev[2]), prev[3]))
        last = [None] * len(chains)
        for m in range(n_vregs):
            rows = slice(SUBLANES * m, SUBLANES * (m + 1))
            for j, ((ca, cb), _) in enumerate(chains):
                ar, br = st_ref[rows, ca:ca + LANES], st_ref[rows, cb:cb + LANES]
                ai, bi = (st_ref[rows, ca + OCTET_STATE:ca + OCTET_STATE + LANES],
                          st_ref[rows, cb + OCTET_STATE:cb + OCTET_STATE + LANES])
                d0r, d1r = jnp.where(lo, ar, swap(br)), jnp.where(lo, swap(ar), br)
                d0i, d1i = jnp.where(lo, ai, swap(bi)), jnp.where(lo, swap(ai), bi)
                s0r = lam_r[j] * sr[j] - lam_i[j] * si[j] + d0r
                s0i = lam_r[j] * si[j] + lam_i[j] * sr[j] + d0i
                s1r = lam_r[j] * s0r - lam_i[j] * s0i + d1r
                s1i = lam_r[j] * s0i + lam_i[j] * s0r + d1i
                sr[j], si[j] = s1r, s1i
                out = (jnp.where(lo, s0r, swap(s1r)), jnp.where(lo, swap(s0r), s1r),
                       jnp.where(lo, s0i, swap(s1i)), jnp.where(lo, swap(s0i), s1i))
                st_ref[rows, ca:ca + LANES] = out[0]
                st_ref[rows, cb:cb + LANES] = out[1]
                st_ref[rows, ca + OCTET_STATE:ca + OCTET_STATE + LANES] = out[2]
                st_ref[rows, cb + OCTET_STATE:cb + OCTET_STATE + LANES] = out[3]
                last[j] = out
        for j, ((ca, cb), _) in enumerate(chains):
            for c, new, prev in zip((ca, cb, ca + OCTET_STATE, cb + OCTET_STATE), last[j], old[j]):
                carry_ref[:, c:c + LANES] = jnp.where(live, new, prev)
    project_out((N_OCTETS - 2, N_OCTETS - 1))
    y_ref[0:rows_k, :] = jnp.concatenate(y_even, axis=-1) + d_ref[...] * ue
    y_ref[rows_k:2 * rows_k, :] = jnp.concatenate(y_odd, axis=-1) + d_ref[...] * uo

    @pl.when(step == n_chunks - 1)
    def _():
        tail = uo_b[rows_k - 2 * SUBLANES:, :]
        zeros = jnp.zeros((2 * SUBLANES, LANES), BF16)
        for q in range(N_OCTETS):
            bu = _dot(jnp.concatenate([zeros, tail[:, LANES * q:LANES * (q + 1)]], axis=1), b2_ref[q])[SUBLANES:]
            base = 2 * OCTET_STATE * q
            lr = lam_ref[0:1, OCTET_STATE * q:OCTET_STATE * (q + 1)]
            li = lam_ref[1:2, OCTET_STATE * q:OCTET_STATE * (q + 1)]
            s_r = carry_ref[:, base:base + OCTET_STATE]
            s_i = carry_ref[:, base + OCTET_STATE:base + 2 * OCTET_STATE]
            fin_ref[:, base:base + OCTET_STATE] = lr * s_r - li * s_i + bu[:, :OCTET_STATE]
            fin_ref[:, base + OCTET_STATE:base + 2 * OCTET_STATE] = lr * s_i + li * s_r + bu[:, OCTET_STATE:]


def _s5_prompt(x, layer, gn, b2, lam, c2, m0, d, wglu, bglu):
    batch, seq, _ = x.shape
    tc = S5_TIME_CHUNK
    n_chunks = seq // tc
    rows_k = tc // 2 * batch
    n_slabs = D_MODEL // LANES
    blk = (batch, tc, D_MODEL)
    this_chunk = pl.BlockSpec(blk, lambda s: (0, jnp.minimum(s, n_chunks - 1), 0))
    prev_chunk = pl.BlockSpec(blk, lambda s: (0, jnp.maximum(s - 1, 0), 0))
    slab = lambda rows: pltpu.VMEM((n_slabs, rows, LANES), F32)
    return pl.pallas_call(
        _s5_prompt_kernel,
        grid=(n_chunks + 1,),
        in_specs=[this_chunk, prev_chunk]
        + [_layer_spec(w.shape, layer) for w in (gn, b2, lam, c2, m0, d, wglu, bglu)],
        out_specs=[prev_chunk, pl.BlockSpec((SUBLANES, STATE_LANES), lambda s: (0, 0))],
        out_shape=[jax.ShapeDtypeStruct(x.shape, F32),
                   jax.ShapeDtypeStruct((SUBLANES, STATE_LANES), F32)],
        scratch_shapes=[pltpu.VMEM((rows_k, STATE_LANES), F32),
                        pltpu.VMEM((SUBLANES, STATE_LANES), F32),
                        slab(tc), slab(rows_k), slab(rows_k), slab(rows_k + SUBLANES),
                        slab(2 * rows_k), slab(tc),
                        pltpu.VMEM((2 * rows_k, D_MODEL), F32)],
        compiler_params=_params(("arbitrary",)),
        name="s5_prompt",
    )(x, x, gn, b2, lam, c2, m0, d, wglu, bglu)


def _s5_sample_kernel(x_ref, gn_ref, b2_ref, lam_ref, c2_ref, d_ref, wglu_ref, bglu_ref, s0_ref,
                      o_ref, snew_ref):
    x = x_ref[...]
    u = _rms(x, gn_ref[...])
    ys = []
    for q in range(N_OCTETS):
        base = 2 * OCTET_STATE * q
        bu = jnp.dot(u[:, LANES * q:LANES * (q + 1)], b2_ref[q, LANES:, :],
                     preferred_element_type=F32, precision=lax.Precision.HIGHEST)
        lr = lam_ref[0:1, OCTET_STATE * q:OCTET_STATE * (q + 1)]
        li = lam_ref[1:2, OCTET_STATE * q:OCTET_STATE * (q + 1)]
        s0r = s0_ref[:, base:base + OCTET_STATE]
        s0i = s0_ref[:, base + OCTET_STATE:base + 2 * OCTET_STATE]
        sr = lr * s0r - li * s0i + bu[:, :OCTET_STATE]
        si = lr * s0i + li * s0r + bu[:, OCTET_STATE:]
        snew_ref[:, base:base + OCTET_STATE] = sr
        snew_ref[:, base + OCTET_STATE:base + 2 * OCTET_STATE] = si
        sb = jnp.concatenate([sr, si], axis=-1).astype(BF16)
        ys.append(_dot(sb, c2_ref[q, :, :LANES]))
    y = jnp.concatenate(ys, axis=-1)
    o_ref[...] = x + _s5_gate(y + d_ref[...] * u, wglu_ref, bglu_ref)


def _s5_sample(x, layer, gn, b2_f32, lam, c2, d, wglu, bglu, s0):
    rows = x.shape[0]
    return pl.pallas_call(
        _s5_sample_kernel,
        grid=(1,),
        in_specs=[_const_spec(x.shape)] + [_layer_spec(w.shape, layer) for w in (gn, b2_f32, lam, c2, d, wglu, bglu)]
        + [_const_spec(s0.shape)],
        out_specs=[pl.BlockSpec((rows, D_MODEL), lambda i: (0, 0)),
                   pl.BlockSpec((rows, STATE_LANES), lambda i: (0, 0))],
        out_shape=[jax.ShapeDtypeStruct((rows, D_MODEL), F32),
                   jax.ShapeDtypeStruct((rows, STATE_LANES), F32)],
        compiler_params=_params(("arbitrary",)),
        name="s5_sample",
    )(x, gn, b2_f32, lam, c2, d, wglu, bglu, s0)


def _block_diag(w):
    n, octets, groups, r, c = w.shape
    col = jnp.arange(groups * c)
    tile = (jnp.arange(c)[:, None] == col % c).astype(F32)
    own = (jnp.arange(groups * r)[:, None] // r == col // c).astype(F32)
    tiled = jnp.einsum("nqrc,cl->nqrl", w.reshape(n, octets, groups * r, c), tile, precision=lax.Precision.HIGHEST)
    return tiled * own


def _s5_weights(a_re, a_im, log_dt, b_re, b_im, c_re, c_im):
    n = a_re.shape[0]
    dt = jnp.exp(log_dt.astype(F32))
    lam = lax.complex(a_re.astype(F32), a_im.astype(F32))
    lam_bar = jnp.exp(lam * dt)
    b = lax.complex(b_re.astype(F32), b_im.astype(F32))
    b_bar = ((lam_bar - 1.0) / lam)[..., None] * b
    c = lax.complex(c_re.astype(F32), c_im.astype(F32))
    gpo = N_GROUPS // N_OCTETS
    octets = lambda w: w.reshape((n, N_OCTETS, gpo) + w.shape[2:])

    def into_state(w):
        wt = jnp.swapaxes(octets(w), 3, 4)
        return jnp.concatenate([_block_diag(jnp.real(wt)), _block_diag(jnp.imag(wt))], axis=-1)

    def out_of_state(w):
        wt = jnp.swapaxes(octets(w), 3, 4)
        return jnp.concatenate([_block_diag(jnp.real(wt)), -_block_diag(jnp.imag(wt))], axis=2)

    b2 = jnp.concatenate([into_state(lam_bar[..., None] * b_bar), into_state(b_bar)], axis=2)
    c2 = jnp.concatenate([out_of_state(c), out_of_state(c * lam_bar[:, :, None, :])], axis=-1)
    cb = jnp.real(jnp.einsum("ngcp,ngpd->ngdc", c, b_bar))
    m0 = _block_diag(octets(cb))
    lam2 = lam_bar * lam_bar
    lam_rows = jnp.stack([jnp.real(lam_bar).reshape(n, -1), jnp.imag(lam_bar).reshape(n, -1),
                          jnp.real(lam2).reshape(n, -1), jnp.imag(lam2).reshape(n, -1)], axis=1)
    return b2, c2.astype(BF16), m0.astype(BF16), lam_rows


def _state_to_lanes(s_re, s_im):
    n = s_re.shape[0]
    re = s_re.astype(F32).reshape(n, N_OCTETS, OCTET_STATE)
    im = s_im.astype(F32).reshape(n, N_OCTETS, OCTET_STATE)
    return jnp.concatenate([re, im], axis=-1).reshape(n, STATE_LANES)


def _lanes_to_state(s):
    n = s.shape[0]
    s4 = s.reshape(n, N_OCTETS, 2, OCTET_STATE)
    return (s4[:, :, 0].reshape(n, N_GROUPS, STATE_DIM), s4[:, :, 1].reshape(n, N_GROUPS, STATE_DIM))


def _rope_partner(n):
    lane = jnp.arange(n)
    return jnp.where(lane % HEAD_DIM < HEAD_DIM // 2, lane + HEAD_DIM // 2, lane - HEAD_DIM // 2)


def _kv_kernel(x_ref, g_ref, wkv_ref, kg_ref, kgp_ref, ones_ref, cos_ref, sin_ref, *rest):
    if len(rest) == 3:
        wvt_ref, k_ref, vt_ref = rest
        v_ref = None
    else:
        (k_ref, v_ref), wvt_ref, vt_ref = rest, None, None
    h = _rms(x_ref[...], g_ref[...]).astype(BF16)
    kv = _dot(h, wkv_ref[...])
    n = k_ref.shape[-1]
    k, k_partner = kv[:, :n], kv[:, n:2 * n]
    scale = lax.rsqrt(_seg_mean_sq(k, ones_ref) + EPS)
    k_ref[...] = scale * (k * (kg_ref[...] * _tile_lanes(cos_ref[...], n))
                          + k_partner * (kgp_ref[...] * _tile_lanes(sin_ref[...], n)))
    if v_ref is not None:
        v_ref[...] = kv[:, 2 * n:]
    else:
        vt_ref[...] = lax.dot_general(wvt_ref[...], h, (((1,), (1,)), ((), ())), preferred_element_type=F32)


def _kv(x, g, wk, wv, kgain, ones, cos, sin, table_block, seq=None):
    rows = x.shape[0]
    n, nv = wk.shape[1], wv.shape[1]
    partner = _rope_partner(n)
    wkv = jnp.concatenate([wk, wk[:, partner]] + ([wv] if seq is None else []), axis=1).astype(BF16)
    tm = min(KV_ROWS, rows)
    in_specs = [pl.BlockSpec((tm, D_MODEL), lambda i: (i, 0)),
                _const_spec((1, D_MODEL)),
                _const_spec(wkv.shape),
                _const_spec((1, n)),
                _const_spec((1, n)),
                _const_spec(ones.shape),
                pl.BlockSpec((tm, LANES), lambda i: (table_block(i), 0)),
                pl.BlockSpec((tm, LANES), lambda i: (table_block(i), 0))]
    out_specs = [pl.BlockSpec((tm, n), lambda i: (i, 0))]
    out_shape = [jax.ShapeDtypeStruct((rows, n), F32)]
    args = [x, g, wkv, kgain, kgain[:, partner], ones, cos, sin]
    if seq is None:
        out_specs.append(pl.BlockSpec((tm, nv), lambda i: (i, 0)))
        out_shape.append(jax.ShapeDtypeStruct((rows, nv), F32))
    else:
        per_seq = seq // tm
        in_specs.append(_const_spec((nv, D_MODEL)))
        args.append(wv.T.astype(BF16))
        out_specs.append(pl.BlockSpec((None, nv, tm), lambda i: (i // per_seq, 0, i % per_seq)))
        out_shape.append(jax.ShapeDtypeStruct((rows // seq, nv, seq), F32))
    return pl.pallas_call(
        _kv_kernel,
        grid=(rows // tm,),
        in_specs=in_specs,
        out_specs=out_specs,
        out_shape=out_shape,
        compiler_params=_params(("parallel",)),
        name="kv_proj",
    )(*args)


def _attn_prompt_kernel(sinks_ref, x_ref, g_ref, wq_ref, qg_ref, ones_ref, cos_ref, sin_ref,
                        kp_ref, kc_ref, vp_ref, vc_ref, wo_ref, o_ref):
    tile = pl.program_id(1)
    x = x_ref[...]
    n_blocks = x.shape[0] // WINDOW
    heads_per_kv = N_HEADS // N_KV_HEADS
    cols = heads_per_kv * WINDOW
    h = _rms(x, g_ref[...]).astype(BF16)
    q = _dot(h, wq_ref[...])
    q = q * lax.rsqrt(_seg_mean_sq(q, ones_ref) + EPS) * qg_ref[...]
    q = _rope(q, _tile_lanes(cos_ref[...], D_MODEL), _tile_lanes(sin_ref[...], D_MODEL))
    qb = (q * (HEAD_DIM ** -0.5)).astype(BF16)

    kall = jnp.concatenate([kp_ref[...], kc_ref[...]], axis=0).astype(BF16)
    vall = jnp.concatenate([vp_ref[...], vc_ref[...]], axis=1).astype(BF16)

    kj = lax.broadcasted_iota(jnp.int32, (WINDOW, cols), 0)
    col = lax.broadcasted_iota(jnp.int32, (WINDOW, cols), 1)
    from_prev = kj > (col & (WINDOW - 1))
    col_head = lax.broadcasted_iota(jnp.int32, (1, cols), 1) // WINDOW
    low_half = lax.broadcasted_iota(jnp.int32, (WINDOW, LANES), 1) < HEAD_DIM
    neg_inf = jnp.full((WINDOW, cols), -jnp.inf, F32)

    pairs = [(blk, g) for blk in range(n_blocks) for g in range(N_KV_HEADS)]
    scores, sinks = [], []
    for blk, g in pairs:
        r0 = WINDOW * blk
        kg = kall[r0:r0 + 2 * WINDOW, LANES * g:LANES * (g + 1)]
        qs = []
        sink = jnp.zeros((1, cols), F32)
        for hh in range(heads_per_kv):
            head = heads_per_kv * g + hh
            qp = qb[r0:r0 + WINDOW, LANES * (head // 2):LANES * (head // 2 + 1)]
            keep = low_half if head % 2 == 0 else jnp.logical_not(low_half)
            qs.append(jnp.where(keep, qp, jnp.zeros_like(qp)))
            sink = jnp.where(col_head == hh, sinks_ref[head], sink)
        sinks.append(sink)
        scores.append(lax.dot_general(kg, jnp.concatenate(qs, axis=0), (((1,), (1,)), ((), ())),
                                      preferred_element_type=F32))
    probs, denoms = [], []
    for (blk, g), s, sink in zip(pairs, scores, sinks):
        has_prev = tile * n_blocks + blk > 0
        t = jnp.where(from_prev, jnp.where(has_prev, s[:WINDOW], neg_inf), s[WINDOW:])
        m = jnp.maximum(jnp.max(t, axis=0, keepdims=True), sink)
        p = jnp.exp(t - m)
        denoms.append(jnp.sum(p, axis=0, keepdims=True) + jnp.exp(sink - m))
        zero = jnp.zeros_like(p)
        probs.append(jnp.concatenate([jnp.where(from_prev, p, zero), jnp.where(from_prev, zero, p)],
                                     axis=0).astype(BF16))
    head_outs = [[] for _ in range(n_blocks)]
    for (blk, g), pcat, denom in zip(pairs, probs, denoms):
        r0 = WINDOW * blk
        vg = vall[HEAD_DIM * g:HEAD_DIM * (g + 1), r0:r0 + 2 * WINDOW]
        o = _dot(vg, pcat) / denom
        head_outs[blk] += [o[:, WINDOW * hh:WINDOW * (hh + 1)] for hh in range(heads_per_kv)]
    block_outs = [jnp.concatenate(outs, axis=0).T for outs in head_outs]
    attn = jnp.concatenate(block_outs, axis=0).astype(BF16)
    o_ref[...] = x + _dot(attn, wo_ref[...])


def _attn_prompt(x, g, wq, qgain, ones, cos, sin, kd, vt, sinks, wo):
    bsz, seq, _ = x.shape
    nk, nv = kd.shape[-1], vt.shape[1]
    nb = ATTN_BLOCKS
    tq = nb * WINDOW
    x_spec = pl.BlockSpec((None, tq, D_MODEL), lambda b, i, *_: (b, i, 0))
    k_cur = pl.BlockSpec((None, tq, nk), lambda b, i, *_: (b, i, 0))
    k_prev = pl.BlockSpec((None, WINDOW, nk), lambda b, i, *_: (b, jnp.maximum(nb * i - 1, 0), 0))
    v_cur = pl.BlockSpec((None, nv, tq), lambda b, i, *_: (b, 0, i))
    v_prev = pl.BlockSpec((None, nv, WINDOW), lambda b, i, *_: (b, 0, jnp.maximum(nb * i - 1, 0)))
    tab = pl.BlockSpec((tq, LANES), lambda b, i, *_: (i, 0))
    grid_spec = pltpu.PrefetchScalarGridSpec(
        num_scalar_prefetch=1,
        grid=(bsz, seq // tq),
        in_specs=[x_spec,
                  _const_spec((1, D_MODEL)),
                  _const_spec(wq.shape),
                  _const_spec((1, D_MODEL)),
                  _const_spec(ones.shape),
                  tab, tab, k_prev, k_cur, v_prev, v_cur,
                  _const_spec(wo.shape)],
        out_specs=x_spec)
    return pl.pallas_call(
        _attn_prompt_kernel,
        grid_spec=grid_spec,
        out_shape=jax.ShapeDtypeStruct(x.shape, F32),
        compiler_params=_params(("parallel", "arbitrary")),
        name="attn_prompt",
    )(sinks, x, g, wq, qgain, ones, cos, sin, kd, kd, vt, vt, wo)


def _own_head(shape):
    row_head = lax.broadcasted_iota(jnp.int32, shape, 0) & (N_HEADS - 1)
    lane_head = lax.broadcasted_iota(jnp.int32, shape, 1) // HEAD_DIM
    return row_head == lane_head


def _q_sample_kernel(x_ref, g_ref, wq_ref, qg_ref, ones_ref, cos_ref, sin_ref, rep_ref, place_ref, q_ref):
    h = _rms(x_ref[...], g_ref[...]).astype(BF16)
    q = _dot(h, wq_ref[...])
    q = q * lax.rsqrt(_seg_mean_sq(q, ones_ref) + EPS) * qg_ref[...]
    q = _rope(q, _tile_lanes(cos_ref[...], D_MODEL), _tile_lanes(sin_ref[...], D_MODEL))
    qb = (q * (HEAD_DIM ** -0.5)).astype(BF16)
    q_rep = _dot(rep_ref[...], qb)
    q_own = jnp.where(_own_head(q_rep.shape), q_rep, 0.0).astype(BF16)
    q_ref[...] = _dot(q_own, place_ref[...])


def _q_sample(x, g, wq, qgain, ones, cos, sin, rep, place):
    rows = x.shape[0]
    return pl.pallas_call(
        _q_sample_kernel,
        out_shape=jax.ShapeDtypeStruct((rows * N_HEADS, place.shape[1]), F32),
        compiler_params=pltpu.CompilerParams(vmem_limit_bytes=VMEM_LIMIT_BYTES),
        name="q_sample",
    )(x, g, wq, qgain, ones, cos, sin, rep, place)


def _cache_append_kernel(k_ref, v_ref, kn_ref, vn_ref, ko_ref, vo_ref):
    nb, lanes_kv, keys = k_ref.shape
    first = pl.program_id(0) * nb
    key = lax.broadcasted_iota(jnp.int32, (lanes_kv, keys), 1)
    seq = lax.broadcasted_iota(jnp.int32, (kn_ref.shape[1], keys), 0)

    def split3(x):
        hi = x.astype(BF16)
        r = x - hi.astype(F32)
        mid = r.astype(BF16)
        return hi, mid, (r - mid.astype(F32)).astype(BF16)

    parts = [(split3(kn_ref[...]), k_ref, ko_ref), (split3(vn_ref[...]), v_ref, vo_ref)]
    for b in range(nb):
        pick = (seq == first + b).astype(BF16)
        for (hi, mid, lo), old_ref, out_ref in parts:
            new_col = _dot(hi, pick) + _dot(mid, pick) + _dot(lo, pick)
            out_ref[b] = jnp.where(key == keys - 1, new_col, pltpu.roll(old_ref[b], keys - 1, 1))


def _cache_append(kt, vt, kn_t, vn_t):
    bsz, lanes_kv, keys = kt.shape
    nb = SUBLANES
    blk = pl.BlockSpec((nb, lanes_kv, keys), lambda i: (i, 0, 0))
    return pl.pallas_call(
        _cache_append_kernel,
        grid=(bsz // nb,),
        in_specs=[blk, blk, _const_spec(kn_t.shape), _const_spec(vn_t.shape)],
        out_specs=[blk, blk],
        out_shape=[jax.ShapeDtypeStruct(kt.shape, F32), jax.ShapeDtypeStruct(vt.shape, F32)],
        compiler_params=_params(("parallel",)),
        name="cache_append",
    )(kt, vt, kn_t, vn_t)


def _attn_sample_kernel(q_ref, k_ref, v_ref, sink_ref, o_ref):
    sink = sink_ref[...]
    for b in range(q_ref.shape[0]):
        s = _dot(q_ref[b].astype(BF16), k_ref[b].astype(BF16))
        m = jnp.maximum(jnp.max(s, axis=-1, keepdims=True), sink)
        p = jnp.exp(s - m)
        denom = jnp.sum(p, axis=-1, keepdims=True) + jnp.exp(sink - m)
        o = lax.dot_general(p.astype(BF16), v_ref[b].astype(BF16), (((1,), (1,)), ((), ())),
                            preferred_element_type=F32)
        o_ref[b] = o / denom


def _attn_sample(q3, kt, vt, sink_col):
    bsz, nh, nkv = q3.shape
    keys = kt.shape[2]
    bb = SUBLANES
    return pl.pallas_call(
        _attn_sample_kernel,
        grid=(bsz // bb,),
        in_specs=[pl.BlockSpec((bb, nh, nkv), lambda i: (i, 0, 0)),
                  pl.BlockSpec((bb, nkv, keys), lambda i: (i, 0, 0)),
                  pl.BlockSpec((bb, nkv, keys), lambda i: (i, 0, 0)),
                  _const_spec(sink_col.shape)],
        out_specs=pl.BlockSpec((bb, nh, nkv), lambda i: (i, 0, 0)),
        out_shape=jax.ShapeDtypeStruct((bsz, nh, nkv), F32),
        compiler_params=_params(("parallel",)),
        name="attn_sample",
    )(q3, kt, vt, sink_col)


def _proj_residual_kernel(x_ref, a_ref, place_t_ref, rep_t_ref, w_ref, o_ref):
    z = _dot(a_ref[...].astype(BF16), place_t_ref[...])
    z_own = jnp.where(_own_head(z.shape), z, 0.0).astype(BF16)
    attn = _dot(rep_t_ref[...], z_own).astype(BF16)
    o_ref[...] = x_ref[...] + _dot(attn, w_ref[...])


def _proj_residual(x, a, place_t, rep_t, w):
    return pl.pallas_call(
        _proj_residual_kernel,
        out_shape=jax.ShapeDtypeStruct(x.shape, F32),
        compiler_params=pltpu.CompilerParams(vmem_limit_bytes=VMEM_LIMIT_BYTES),
        name="o_proj_sample",
    )(x, a, place_t, rep_t, w)


def _rope_tables(pos):
    half = HEAD_DIM // 2
    inv = ROPE_THETA ** (-jnp.arange(half, dtype=F32) / half)
    ang = pos.astype(F32)[:, None] * inv[None, :]
    cos, sin = jnp.cos(ang), jnp.sin(ang)
    reps = LANES // HEAD_DIM
    return (jnp.tile(jnp.concatenate([cos, cos], axis=-1), (1, reps)),
            jnp.tile(jnp.concatenate([-sin, sin], axis=-1), (1, reps)))


def _block_ones(width, block):
    idx = jnp.arange(width) // block
    return (idx[:, None] == idx[None, :]).astype(BF16)


def _dup_heads(w):
    w3 = w.reshape(w.shape[0], N_KV_HEADS, 1, HEAD_DIM)
    return jnp.broadcast_to(w3, (w.shape[0], N_KV_HEADS, LANES // HEAD_DIM, HEAD_DIM)).reshape(w.shape[0], -1)


def _head_placement():
    lane = jnp.arange(N_HEADS * HEAD_DIM)
    dst = (lane // HEAD_DIM) // (N_HEADS // N_KV_HEADS) * HEAD_DIM + lane % HEAD_DIM
    return (dst[:, None] == jnp.arange(N_KV_HEADS * HEAD_DIM)[None, :]).astype(BF16)


def _row_replication(n):
    return (jnp.arange(n * N_HEADS)[:, None] // N_HEADS == jnp.arange(n)[None, :]).astype(BF16)


def kernel(x_prompt, x_sample, state_ssm_re, state_ssm_im, cache_k, cache_v, norm_mix, norm_mlp, ssm_a_re, ssm_a_im, ssm_log_dt, ssm_b_re, ssm_b_im, ssm_c_re, ssm_c_im, ssm_d, w_glu, b_glu, norm_kv, w_k, w_v, k_norm, w_q, q_norm, attn_sinks, w_o, w_mlp_in, w_mlp_out):
    bsz, seq, _ = x_prompt.shape
    dec = x_sample.shape[0]
    n_a = ssm_a_re.shape[0]
    depth = norm_mix.shape[0]
    past = 8192
    nkv = N_KV_HEADS * HEAD_DIM
    row = lambda v: v.astype(F32).reshape(1, -1)

    xp = x_prompt.astype(F32)
    xs = x_sample.reshape(dec, D_MODEL)
    rows3 = lambda v: v.astype(F32).reshape(v.shape[0], 1, -1)

    w1 = w_mlp_in.astype(BF16)
    w2 = w_mlp_out.astype(BF16)
    g_mlp = rows3(norm_mlp)
    mlp_p = lambda x, layer: _mlp(x.reshape(bsz * seq, D_MODEL), g_mlp, w1, w2, layer).reshape(bsz, seq, D_MODEL)

    b2, c2, m0, lam = _s5_weights(ssm_a_re, ssm_a_im, ssm_log_dt, ssm_b_re, ssm_b_im, ssm_c_re, ssm_c_im)
    b2_16 = b2.astype(BF16)
    wglu = w_glu.astype(BF16)
    g_mix, d_skip, bg = rows3(norm_mix), rows3(ssm_d), rows3(b_glu)
    sp_re, sp_im, ss_re, ss_im = [], [], [], []
    for i in range(n_a):
        xp, fin = _s5_prompt(xp, i, g_mix, b2_16, lam, c2, m0, d_skip, wglu, bg)
        re, im = _lanes_to_state(fin[bsz:])
        sp_re.append(re)
        sp_im.append(im)
        s0 = _state_to_lanes(state_ssm_re[i], state_ssm_im[i])
        xs, snew = _s5_sample(xs, i, g_mix, b2, lam, c2, d_skip, wglu, bg, s0)
        re, im = _lanes_to_state(snew)
        ss_re.append(re)
        ss_im.append(im)
        xp = mlp_p(xp, i)
        xs = _mlp(xs, g_mlp, w1, w2, i)

    cos_p, sin_p = _rope_tables(jnp.arange(seq, dtype=jnp.int32))
    cos_s, sin_s = _rope_tables(jnp.full((dec,), past, dtype=jnp.int32))
    ones_head = _block_ones(MXU_TILE, HEAD_DIM)
    place = _head_placement()
    rep = _row_replication(dec)
    kgain = row(k_norm)
    tile_gain = lambda gvec, n: jnp.tile(gvec, (1, n // HEAD_DIM))

    kd, vt = _kv(xp.reshape(bsz * seq, D_MODEL), row(norm_kv), _dup_heads(w_k), w_v,
                     tile_gain(kgain, 2 * nkv), ones_head, cos_p, sin_p, lambda i: i % (seq // KV_ROWS), seq=seq)
    ks_new, vs_new = _kv(xs, row(norm_kv), w_k, w_v, tile_gain(kgain, nkv), ones_head, cos_s, sin_s,
                         lambda i: i)
    kd = kd.reshape(bsz, seq, 2 * nkv)
    new_k_p = kd[:, -WINDOW:].reshape(bsz, WINDOW, N_KV_HEADS, LANES // HEAD_DIM, HEAD_DIM)[:, :, :, 0]
    new_v_p = jnp.transpose(vt[:, :, -WINDOW:].reshape(bsz, N_KV_HEADS, HEAD_DIM, WINDOW), (0, 3, 1, 2))
    key_minor = lambda c: jnp.transpose(c.astype(F32), (0, 2, 3, 1)).reshape(dec, nkv, c.shape[1])
    keys_s, vals_s = _cache_append(key_minor(cache_k), key_minor(cache_v), ks_new.T, vs_new.T)
    key_major = lambda c: jnp.transpose(c.reshape(dec, N_KV_HEADS, HEAD_DIM, c.shape[2]), (0, 3, 1, 2))
    new_k_s, new_v_s = key_major(keys_s), key_major(vals_s)

    for j in range(depth - n_a):
        layer = n_a + j
        gn = row(norm_mix[layer])
        qgain = row(q_norm[j])
        wq = w_q[j].astype(BF16)
        wo = w_o[j].astype(BF16)
        xp = _attn_prompt(xp, gn, wq, tile_gain(qgain, D_MODEL), ones_head, cos_p, sin_p, kd, vt,
                          attn_sinks[j].astype(F32), wo)
        q_pad = _q_sample(xs, gn, wq, tile_gain(qgain, D_MODEL), ones_head, cos_s, sin_s, rep, place)
        o_pad = _attn_sample(q_pad.reshape(dec, N_HEADS, nkv), keys_s, vals_s,
                             attn_sinks[j].astype(F32).reshape(N_HEADS, 1))
        xs = _proj_residual(xs, o_pad.reshape(dec * N_HEADS, nkv), place.T, rep.T, wo)
        xp = mlp_p(xp, layer)
        xs = _mlp(xs, g_mlp, w1, w2, layer)

    return (xp, xs.reshape(dec, 1, D_MODEL),
            jnp.stack(sp_re), jnp.stack(sp_im), new_k_p, new_v_p,
            jnp.stack(ss_re), jnp.stack(ss_im), new_k_s, new_v_s)
```

```python
import functools
import math

import jax
import jax.numpy as jnp
from jax import lax
from jax.experimental import pallas as pl
from jax.experimental.pallas import tpu as pltpu

F32 = jnp.float32
BF16 = jnp.bfloat16

D_MODEL = 1024
N_GROUPS = 64
GROUP_SIZE = 16
STATE_DIM = 64
HEAD_DIM = 64
N_HEADS = 16
N_KV_HEADS = 4
WINDOW = 128
ROPE_THETA = 10000.0
D_FF = 4 * D_MODEL
EPS = 1e-6

SUBLANES = 8
LANES = 128
MXU_TILE = 256
VMEM_LIMIT_BYTES = 56 * 1024 * 1024

N_OCTETS = D_MODEL // LANES
OCTET_STATE = (LANES // GROUP_SIZE) * STATE_DIM
STATE_LANES = 2 * OCTET_STATE * N_OCTETS

S5_TIME_CHUNK = 128
MLP_ROWS = 1024
FF_CHUNK = 1024
KV_ROWS = 1024
ATTN_BLOCKS = 8


def _const_spec(shape):
    zeros = (0,) * len(shape)
    return pl.BlockSpec(shape, lambda *_: zeros, pipeline_mode=pl.Buffered(1))


def _layer_spec(shape, layer):
    idx = (layer,) + (0,) * (len(shape) - 1)
    return pl.BlockSpec((None,) + tuple(shape[1:]), lambda *_: idx, pipeline_mode=pl.Buffered(1))


def _params(sem):
    return pltpu.CompilerParams(dimension_semantics=sem, vmem_limit_bytes=VMEM_LIMIT_BYTES)


def _rms(x, g):
    return x * lax.rsqrt(jnp.mean(x * x, axis=-1, keepdims=True) + EPS) * g


def _dot(a, b):
    return jnp.dot(a, b, preferred_element_type=F32)


def _seg_mean_sq(x, ones_ref):
    sq = x * x
    hi = sq.astype(BF16)
    lo = (sq - hi.astype(F32)).astype(BF16)
    ones = ones_ref[...]
    outs = []
    for t in range(x.shape[-1] // MXU_TILE):
        sl = slice(MXU_TILE * t, MXU_TILE * (t + 1))
        outs.append(_dot(hi[:, sl], ones) + _dot(lo[:, sl], ones))
    return jnp.concatenate(outs, axis=-1) * (1.0 / HEAD_DIM)


def _tile_lanes(t, n):
    return jnp.concatenate([t] * (n // t.shape[-1]), axis=-1)


def _rope(x, cos, sin_signed):
    n = x.shape[-1]
    lane = lax.broadcasted_iota(jnp.int32, x.shape, 1)
    first = (lane & (HEAD_DIM - 1)) < (HEAD_DIM // 2)
    partner = jnp.where(first, pltpu.roll(x, n - HEAD_DIM // 2, 1), pltpu.roll(x, HEAD_DIM // 2, 1))
    return x * cos + partner * sin_signed


def _mlp_kernel(x_ref, g_ref, w1_ref, w2_ref, o_ref):
    x = x_ref[...]
    h = _rms(x, g_ref[...]).astype(BF16)
    acc = x
    for c in range(D_FF // FF_CHUNK):
        sl = slice(FF_CHUNK * c, FF_CHUNK * (c + 1))
        a = jnp.square(jnp.maximum(_dot(h, w1_ref[:, sl]), 0.0)).astype(BF16)
        acc = acc + _dot(a, w2_ref[sl, :])
    o_ref[...] = acc


def _mlp(x, g, w1, w2, layer):
    rows = x.shape[0]
    tm = min(MLP_ROWS, rows)
    return pl.pallas_call(
        _mlp_kernel,
        grid=(rows // tm,),
        in_specs=[pl.BlockSpec((tm, D_MODEL), lambda i: (i, 0)),
                  _layer_spec(g.shape, layer),
                  _layer_spec(w1.shape, layer),
                  _layer_spec(w2.shape, layer)],
        out_specs=pl.BlockSpec((tm, D_MODEL), lambda i: (i, 0)),
        out_shape=jax.ShapeDtypeStruct((rows, D_MODEL), F32),
        compiler_params=_params(("parallel",)),
        name="mlp",
    )(x, g, w1, w2)


def _s5_gate(y, wglu_ref, bglu_ref):
    g = jax.nn.gelu(y).astype(BF16)
    z = _dot(g, wglu_ref[...]) + bglu_ref[...]
    return z[:, :D_MODEL] * jax.nn.sigmoid(z[:, D_MODEL:])


def _slabs(ref, rows=slice(None)):
    return jnp.concatenate([ref[c, rows, :] for c in range(ref.shape[0])], axis=-1)


def _s5_prompt_kernel(xa_ref, xb_ref, gn_ref, b2_ref, lam_ref, c2_ref, m0_ref, d_ref, wglu_ref, bglu_ref,
                      o_ref, fin_ref, st_ref, carry_ref, nat_ref, ue_ref, uo_ref, up_ref, tb2_ref, nat2_ref, y_ref):
    batch, tc, _ = xa_ref.shape
    assert 2 * batch == SUBLANES
    n_slabs = D_MODEL // LANES
    pairs = tc // 2
    rows_k = pairs * batch
    step = pl.program_id(0)
    n_chunks = pl.num_programs(0) - 1
    live = step < n_chunks

    @pl.when(step == 0)
    def _():
        carry_ref[...] = jnp.zeros_like(carry_ref)
        y_ref[...] = jnp.zeros_like(y_ref)
        up_ref[...] = jnp.zeros_like(up_ref)

    g_prev = jax.nn.gelu(y_ref[...]).astype(BF16)
    glu_cols = 2 * D_MODEL // (N_OCTETS // 2)
    z = []

    def glu_part(c):
        sl = slice(glu_cols * c, glu_cols * (c + 1))
        z.append(_dot(g_prev, wglu_ref[:, sl]) + bglu_ref[:, sl])

    def finish_prev():
        zs = jnp.concatenate(z, axis=-1)
        upd = zs[:, :D_MODEL] * jax.nn.sigmoid(zs[:, D_MODEL:])
        for c in range(n_slabs):
            tb2_ref[c] = upd[:, LANES * c:LANES * (c + 1)]
        for b in range(batch):
            for c in range(n_slabs):
                nat2_ref[c, pl.ds(0, pairs, stride=2), :] = tb2_ref[c, pl.ds(b, pairs, stride=batch), :]
                nat2_ref[c, pl.ds(1, pairs, stride=2), :] = tb2_ref[c, pl.ds(rows_k + b, pairs, stride=batch), :]
            o_ref[b] = xb_ref[b] + _slabs(nat2_ref)

    for c in range(n_slabs):
        up_ref[c, 0:batch, :] = up_ref[c, rows_k:rows_k + batch, :]
    for b in range(batch):
        u_nat = _rms(xa_ref[b], gn_ref[...])
        for c in range(n_slabs):
            nat_ref[c] = u_nat[:, LANES * c:LANES * (c + 1)]
        for c in range(n_slabs):
            even = nat_ref[c, pl.ds(0, pairs, stride=2), :]
            odd = nat_ref[c, pl.ds(1, pairs, stride=2), :]
            ue_ref[c, pl.ds(b, pairs, stride=batch), :] = even
            uo_ref[c, pl.ds(b, pairs, stride=batch), :] = odd
            up_ref[c, pl.ds(batch + b, pairs, stride=batch), :] = odd
    ue, uo = _slabs(ue_ref), _slabs(uo_ref)
    ue_b, uo_b, up_b = ue.astype(BF16), uo.astype(BF16), _slabs(up_ref, slice(0, rows_k)).astype(BF16)

    lo = lax.broadcasted_iota(jnp.int32, (SUBLANES, LANES), 0) < batch
    swap = lambda v: pltpu.roll(v, batch, 0)
    n_vregs = rows_k // SUBLANES
    half = OCTET_STATE // LANES
    y_even, y_odd = [], []
    for q in range(N_OCTETS):
        ch = slice(LANES * q, LANES * (q + 1))
        lhs = jnp.concatenate([up_b[:, ch], ue_b[:, ch]], axis=1)
        st_ref[:, 2 * OCTET_STATE * q:2 * OCTET_STATE * (q + 1)] = _dot(lhs, b2_ref[q])

    def project_out(octets):
        for q in octets:
            ch = slice(LANES * q, LANES * (q + 1))
            y2 = _dot(st_ref[:, 2 * OCTET_STATE * q:2 * OCTET_STATE * (q + 1)].astype(BF16), c2_ref[q])
            y_even.append(y2[:, :LANES])
            y_odd.append(y2[:, LANES:] + _dot(uo_b[:, ch], m0_ref[q]))

    for q0 in range(0, N_OCTETS, 2):
        octets = (q0, q0 + 1)
        glu_part(q0 // 2)
        if q0 > 0:
            project_out((q0 - 2, q0 - 1))
        if q0 == N_OCTETS - 2:
            finish_prev()
        chains = []
        for q in octets:
            for j in range(half // 2):
                cols = [2 * OCTET_STATE * q + LANES * jj for jj in (j, j + half // 2)]
                lanes = [OCTET_STATE * q + LANES * jj for jj in (j, j + half // 2)]
                chains.append((cols, lanes))
        lam_r, lam_i, sr, si, old = [], [], [], [], []
        for (ca, cb), (la, lb) in chains:
            bc = lambda r, c: jnp.broadcast_to(lam_ref[r:r + 1, c:c + LANES], (SUBLANES, LANES))
            lam_r.append(jnp.where(lo, bc(2, la), bc(2, lb)))
            lam_i.append(jnp.where(lo, bc(3, la), bc(3, lb)))
            prev = [carry_ref[:, c:c + LANES] for c in (ca, cb, ca + OCTET_STATE, cb + OCTET_STATE)]
            old.append(prev)
            sr.append(jnp.where(lo, swap(prev[0]), prev[1]))
            si.append(jnp.where(lo, swap(prev[2]), prev[3]))
        last = [None] * len(chains)
        for m in range(n_vregs):
            rows = slice(SUBLANES * m, SUBLANES * (m + 1))
            for j, ((ca, cb), _) in enumerate(chains):
                ar, br = st_ref[rows, ca:ca + LANES], st_ref[rows, cb:cb + LANES]
                ai, bi = (st_ref[rows, ca + OCTET_STATE:ca + OCTET_STATE + LANES],
                          st_ref[rows, cb + OCTET_STATE:cb + OCTET_STATE + LANES])
                d0r, d1r = jnp.where(lo, ar, swap(br)), jnp.where(lo, swap(ar), br)
                d0i, d1i = jnp.where(lo, ai, swap(bi)), jnp.where(lo, swap(ai), bi)
                s0r = lam_r[j] * sr[j] - lam_i[j] * si[j] + d0r
                s0i = lam_r[j] * si[j] + lam_i[j] * sr[j] + d0i
                s1r = lam_r[j] * s0r - lam_i[j] * s0i + d1r
                s1i = lam_r[j] * s0i + lam_i[j] * s0r + d1i
                sr[j], si[j] = s1r, s1i
                out = (jnp.where(lo, s0r, swap(s1r)), jnp.where(lo, swap(s0r), s1r),
                       jnp.where(lo, s0i, swap(s1i)), jnp.where(lo, swap(s0i), s1i))
                st_ref[rows, ca:ca + LANES] = out[0]
                st_ref[rows, cb:cb + LANES] = out[1]
                st_ref[rows, ca + OCTET_STATE:ca + OCTET_STATE + LANES] = out[2]
                st_ref[rows, cb + OCTET_STATE:cb + OCTET_STATE + LANES] = out[3]
                last[j] = out
        for j, ((ca, cb), _) in enumerate(chains):
            for c, new, prev in zip((ca, cb, ca + OCTET_STATE, cb + OCTET_STATE), last[j], old[j]):
                carry_ref[:, c:c + LANES] = jnp.where(live, new, prev)
    project_out((N_OCTETS - 2, N_OCTETS - 1))
    y_ref[0:rows_k, :] = jnp.concatenate(y_even, axis=-1) + d_ref[...] * ue
    y_ref[rows_k:2 * rows_k, :] = jnp.concatenate(y_odd, axis=-1) + d_ref[...] * uo

    @pl.when(step == n_chunks - 1)
    def _():
        tail = uo_b[rows_k - 2 * SUBLANES:, :]
        zeros = jnp.zeros((2 * SUBLANES, LANES), BF16)
        for q in range(N_OCTETS):
            bu = _dot(jnp.concatenate([zeros, tail[:, LANES * q:LANES * (q + 1)]], axis=1), b2_ref[q])[SUBLANES:]
            base = 2 * OCTET_STATE * q
            lr = lam_ref[0:1, OCTET_STATE * q:OCTET_STATE * (q + 1)]
            li = lam_ref[1:2, OCTET_STATE * q:OCTET_STATE * (q + 1)]
            s_r = carry_ref[:, base:base + OCTET_STATE]
            s_i = carry_ref[:, base + OCTET_STATE:base + 2 * OCTET_STATE]
            fin_ref[:, base:base + OCTET_STATE] = lr * s_r - li * s_i + bu[:, :OCTET_STATE]
            fin_ref[:, base + OCTET_STATE:base + 2 * OCTET_STATE] = lr * s_i + li * s_r + bu[:, OCTET_STATE:]


def _s5_prompt(x, layer, gn, b2, lam, c2, m0, d, wglu, bglu):
    batch, seq, _ = x.shape
    tc = S5_TIME_CHUNK
    n_chunks = seq // tc
    rows_k = tc // 2 * batch
    n_slabs = D_MODEL // LANES
    blk = (batch, tc, D_MODEL)
    this_chunk = pl.BlockSpec(blk, lambda s: (0, jnp.minimum(s, n_chunks - 1), 0))
    prev_chunk = pl.BlockSpec(blk, lambda s: (0, jnp.maximum(s - 1, 0), 0))
    slab = lambda rows: pltpu.VMEM((n_slabs, rows, LANES), F32)
    return pl.pallas_call(
        _s5_prompt_kernel,
        grid=(n_chunks + 1,),
        in_specs=[this_chunk, prev_chunk]
        + [_layer_spec(w.shape, layer) for w in (gn, b2, lam, c2, m0, d, wglu, bglu)],
        out_specs=[prev_chunk, pl.BlockSpec((SUBLANES, STATE_LANES), lambda s: (0, 0))],
        out_shape=[jax.ShapeDtypeStruct(x.shape, F32),
                   jax.ShapeDtypeStruct((SUBLANES, STATE_LANES), F32)],
        scratch_shapes=[pltpu.VMEM((rows_k, STATE_LANES), F32),
                        pltpu.VMEM((SUBLANES, STATE_LANES), F32),
                        slab(tc), slab(rows_k), slab(rows_k), slab(rows_k + SUBLANES),
                        slab(2 * rows_k), slab(tc),
                        pltpu.VMEM((2 * rows_k, D_MODEL), F32)],
        compiler_params=_params(("arbitrary",)),
        name="s5_prompt",
    )(x, x, gn, b2, lam, c2, m0, d, wglu, bglu)


def _s5_sample_kernel(x_ref, gn_ref, b2_ref, lam_ref, c2_ref, d_ref, wglu_ref, bglu_ref, s0_ref,
                      o_ref, snew_ref):
    x = x_ref[...]
    u = _rms(x, gn_ref[...])
    ys = []
    for q in range(N_OCTETS):
        base = 2 * OCTET_STATE * q
        bu = jnp.dot(u[:, LANES * q:LANES * (q + 1)], b2_ref[q, LANES:, :],
                     preferred_element_type=F32, precision=lax.Precision.HIGHEST)
        lr = lam_ref[0:1, OCTET_STATE * q:OCTET_STATE * (q + 1)]
        li = lam_ref[1:2, OCTET_STATE * q:OCTET_STATE * (q + 1)]
        s0r = s0_ref[:, base:base + OCTET_STATE]
        s0i = s0_ref[:, base + OCTET_STATE:base + 2 * OCTET_STATE]
        sr = lr * s0r - li * s0i + bu[:, :OCTET_STATE]
        si = lr * s0i + li * s0r + bu[:, OCTET_STATE:]
        snew_ref[:, base:base + OCTET_STATE] = sr
        snew_ref[:, base + OCTET_STATE:base + 2 * OCTET_STATE] = si
        sb = jnp.concatenate([sr, si], axis=-1).astype(BF16)
        ys.append(_dot(sb, c2_ref[q, :, :LANES]))
    y = jnp.concatenate(ys, axis=-1)
    o_ref[...] = x + _s5_gate(y + d_ref[...] * u, wglu_ref, bglu_ref)


def _s5_sample(x, layer, gn, b2_f32, lam, c2, d, wglu, bglu, s0):
    rows = x.shape[0]
    return pl.pallas_call(
        _s5_sample_kernel,
        grid=(1,),
        in_specs=[_const_spec(x.shape)] + [_layer_spec(w.shape, layer) for w in (gn, b2_f32, lam, c2, d, wglu, bglu)]
        + [_const_spec(s0.shape)],
        out_specs=[pl.BlockSpec((rows, D_MODEL), lambda i: (0, 0)),
                   pl.BlockSpec((rows, STATE_LANES), lambda i: (0, 0))],
        out_shape=[jax.ShapeDtypeStruct((rows, D_MODEL), F32),
                   jax.ShapeDtypeStruct((rows, STATE_LANES), F32)],
        compiler_params=_params(("arbitrary",)),
        name="s5_sample",
    )(x, gn, b2_f32, lam, c2, d, wglu, bglu, s0)


def _block_diag(w):
    n, octets, groups, r, c = w.shape
    col = jnp.arange(groups * c)
    tile = (jnp.arange(c)[:, None] == col % c).astype(F32)
    own = (jnp.arange(groups * r)[:, None] // r == col // c).astype(F32)
    tiled = jnp.einsum("nqrc,cl->nqrl", w.reshape(n, octets, groups * r, c), tile, precision=lax.Precision.HIGHEST)
    return tiled * own


def _s5_weights(a_re, a_im, log_dt, b_re, b_im, c_re, c_im):
    n = a_re.shape[0]
    dt = jnp.exp(log_dt.astype(F32))
    lam = lax.complex(a_re.astype(F32), a_im.astype(F32))
    lam_bar = jnp.exp(lam * dt)
    b = lax.complex(b_re.astype(F32), b_im.astype(F32))
    b_bar = ((lam_bar - 1.0) / lam)[..., None] * b
    c = lax.complex(c_re.astype(F32), c_im.astype(F32))
    gpo = N_GROUPS // N_OCTETS
    octets = lambda w: w.reshape((n, N_OCTETS, gpo) + w.shape[2:])

    def into_state(w):
        wt = jnp.swapaxes(octets(w), 3, 4)
        return jnp.concatenate([_block_diag(jnp.real(wt)), _block_diag(jnp.imag(wt))], axis=-1)

    def out_of_state(w):
        wt = jnp.swapaxes(octets(w), 3, 4)
        return jnp.concatenate([_block_diag(jnp.real(wt)), -_block_diag(jnp.imag(wt))], axis=2)

    b2 = jnp.concatenate([into_state(lam_bar[..., None] * b_bar), into_state(b_bar)], axis=2)
    c2 = jnp.concatenate([out_of_state(c), out_of_state(c * lam_bar[:, :, None, :])], axis=-1)
    cb = jnp.real(jnp.einsum("ngcp,ngpd->ngdc", c, b_bar))
    m0 = _block_diag(octets(cb))
    lam2 = lam_bar * lam_bar
    lam_rows = jnp.stack([jnp.real(lam_bar).reshape(n, -1), jnp.imag(lam_bar).reshape(n, -1),
                          jnp.real(lam2).reshape(n, -1), jnp.imag(lam2).reshape(n, -1)], axis=1)
    return b2, c2.astype(BF16), m0.astype(BF16), lam_rows


def _state_to_lanes(s_re, s_im):
    n = s_re.shape[0]
    re = s_re.astype(F32).reshape(n, N_OCTETS, OCTET_STATE)
    im = s_im.astype(F32).reshape(n, N_OCTETS, OCTET_STATE)
    return jnp.concatenate([re, im], axis=-1).reshape(n, STATE_LANES)


def _lanes_to_state(s):
    n = s.shape[0]
    s4 = s.reshape(n, N_OCTETS, 2, OCTET_STATE)
    return (s4[:, :, 0].reshape(n, N_GROUPS, STATE_DIM), s4[:, :, 1].reshape(n, N_GROUPS, STATE_DIM))


def _rope_partner(n):
    lane = jnp.arange(n)
    return jnp.where(lane % HEAD_DIM < HEAD_DIM // 2, lane + HEAD_DIM // 2, lane - HEAD_DIM // 2)


def _kv_kernel(x_ref, g_ref, wkv_ref, kg_ref, kgp_ref, ones_ref, cos_ref, sin_ref, *rest):
    if len(rest) == 3:
        wvt_ref, k_ref, vt_ref = rest
        v_ref = None
    else:
        (k_ref, v_ref), wvt_ref, vt_ref = rest, None, None
    h = _rms(x_ref[...], g_ref[...]).astype(BF16)
    kv = _dot(h, wkv_ref[...])
    n = k_ref.shape[-1]
    k, k_partner = kv[:, :n], kv[:, n:2 * n]
    scale = lax.rsqrt(_seg_mean_sq(k, ones_ref) + EPS)
    k_ref[...] = (scale * (k * (kg_ref[...] * _tile_lanes(cos_ref[...], n))
                           + k_partner * (kgp_ref[...] * _tile_lanes(sin_ref[...], n)))).astype(k_ref.dtype)
    if v_ref is not None:
        v_ref[...] = kv[:, 2 * n:]
    else:
        vt_ref[...] = lax.dot_general(wvt_ref[...], h, (((1,), (1,)), ((), ())),
                                      preferred_element_type=F32).astype(vt_ref.dtype)


def _kv(x, g, wk, wv, kgain, ones, cos, sin, table_block, seq=None):
    rows = x.shape[0]
    n, nv = wk.shape[1], wv.shape[1]
    partner = _rope_partner(n)
    wkv = jnp.concatenate([wk, wk[:, partner]] + ([wv] if seq is None else []), axis=1).astype(BF16)
    tm = min(KV_ROWS, rows)
    in_specs = [pl.BlockSpec((tm, D_MODEL), lambda i: (i, 0)),
                _const_spec((1, D_MODEL)),
                _const_spec(wkv.shape),
                _const_spec((1, n)),
                _const_spec((1, n)),
                _const_spec(ones.shape),
                pl.BlockSpec((tm, LANES), lambda i: (table_block(i), 0)),
                pl.BlockSpec((tm, LANES), lambda i: (table_block(i), 0))]
    out_specs = [pl.BlockSpec((tm, n), lambda i: (i, 0))]
    out_shape = [jax.ShapeDtypeStruct((rows, n), F32 if seq is None else BF16)]
    args = [x, g, wkv, kgain, kgain[:, partner], ones, cos, sin]
    if seq is None:
        out_specs.append(pl.BlockSpec((tm, nv), lambda i: (i, 0)))
        out_shape.append(jax.ShapeDtypeStruct((rows, nv), F32))
    else:
        per_seq = seq // tm
        in_specs.append(_const_spec((nv, D_MODEL)))
        args.append(wv.T.astype(BF16))
        out_specs.append(pl.BlockSpec((None, nv, tm), lambda i: (i // per_seq, 0, i % per_seq)))
        out_shape.append(jax.ShapeDtypeStruct((rows // seq, nv, seq), BF16))
    return pl.pallas_call(
        _kv_kernel,
        grid=(rows // tm,),
        in_specs=in_specs,
        out_specs=out_specs,
        out_shape=out_shape,
        compiler_params=_params(("parallel",)),
        name="kv_proj",
    )(*args)


def _attn_prompt_kernel(sinks_ref, x_ref, g_ref, wq_ref, qg_ref, ones_ref, cos_ref, sin_ref,
                        kp_ref, kc_ref, vp_ref, vc_ref, wo_ref, o_ref):
    tile = pl.program_id(1)
    x = x_ref[...]
    n_blocks = x.shape[0] // WINDOW
    heads_per_kv = N_HEADS // N_KV_HEADS
    cols = heads_per_kv * WINDOW
    h = _rms(x, g_ref[...]).astype(BF16)
    q = _dot(h, wq_ref[...])
    q = q * lax.rsqrt(_seg_mean_sq(q, ones_ref) + EPS) * qg_ref[...]
    q = _rope(q, _tile_lanes(cos_ref[...], D_MODEL), _tile_lanes(sin_ref[...], D_MODEL))
    qb = (q * (HEAD_DIM ** -0.5)).astype(BF16)

    kall = jnp.concatenate([kp_ref[...], kc_ref[...]], axis=0).astype(BF16)
    vall = jnp.concatenate([vp_ref[...], vc_ref[...]], axis=1).astype(BF16)

    kj = lax.broadcasted_iota(jnp.int32, (WINDOW, cols), 0)
    col = lax.broadcasted_iota(jnp.int32, (WINDOW, cols), 1)
    from_prev = kj > (col & (WINDOW - 1))
    col_head = lax.broadcasted_iota(jnp.int32, (1, cols), 1) // WINDOW
    low_half = lax.broadcasted_iota(jnp.int32, (WINDOW, LANES), 1) < HEAD_DIM
    neg_inf = jnp.full((WINDOW, cols), -jnp.inf, F32)

    pairs = [(blk, g) for blk in range(n_blocks) for g in range(N_KV_HEADS)]
    scores, sinks = [], []
    for blk, g in pairs:
        r0 = WINDOW * blk
        kg = kall[r0:r0 + 2 * WINDOW, LANES * g:LANES * (g + 1)]
        qs = []
        sink = jnp.zeros((1, cols), F32)
        for hh in range(heads_per_kv):
            head = heads_per_kv * g + hh
            qp = qb[r0:r0 + WINDOW, LANES * (head // 2):LANES * (head // 2 + 1)]
            keep = low_half if head % 2 == 0 else jnp.logical_not(low_half)
            qs.append(jnp.where(keep, qp, jnp.zeros_like(qp)))
            sink = jnp.where(col_head == hh, sinks_ref[head], sink)
        sinks.append(sink)
        scores.append(lax.dot_general(kg, jnp.concatenate(qs, axis=0), (((1,), (1,)), ((), ())),
                                      preferred_element_type=F32))
    probs, denoms = [], []
    for (blk, g), s, sink in zip(pairs, scores, sinks):
        has_prev = tile * n_blocks + blk > 0
        t = jnp.where(from_prev, jnp.where(has_prev, s[:WINDOW], neg_inf), s[WINDOW:])
        m = jnp.maximum(jnp.max(t, axis=0, keepdims=True), sink)
        p = jnp.exp(t - m)
        denoms.append(jnp.sum(p, axis=0, keepdims=True) + jnp.exp(sink - m))
        zero = jnp.zeros_like(p)
        probs.append(jnp.concatenate([jnp.where(from_prev, p, zero), jnp.where(from_prev, zero, p)],
                                     axis=0).astype(BF16))
    head_outs = [[] for _ in range(n_blocks)]
    for (blk, g), pcat, denom in zip(pairs, probs, denoms):
        r0 = WINDOW * blk
        vg = vall[HEAD_DIM * g:HEAD_DIM * (g + 1), r0:r0 + 2 * WINDOW]
        o = _dot(vg, pcat) / denom
        head_outs[blk] += [o[:, WINDOW * hh:WINDOW * (hh + 1)] for hh in range(heads_per_kv)]
    block_outs = [jnp.concatenate(outs, axis=0).T for outs in head_outs]
    attn = jnp.concatenate(block_outs, axis=0).astype(BF16)
    o_ref[...] = x + _dot(attn, wo_ref[...])


def _attn_prompt(x, g, wq, qgain, ones, cos, sin, kd, vt, sinks, wo):
    bsz, seq, _ = x.shape
    nk, nv = kd.shape[-1], vt.shape[1]
    nb = ATTN_BLOCKS
    tq = nb * WINDOW
    x_spec = pl.BlockSpec((None, tq, D_MODEL), lambda b, i, *_: (b, i, 0))
    k_cur = pl.BlockSpec((None, tq, nk), lambda b, i, *_: (b, i, 0))
    k_prev = pl.BlockSpec((None, WINDOW, nk), lambda b, i, *_: (b, jnp.maximum(nb * i - 1, 0), 0))
    v_cur = pl.BlockSpec((None, nv, tq), lambda b, i, *_: (b, 0, i))
    v_prev = pl.BlockSpec((None, nv, WINDOW), lambda b, i, *_: (b, 0, jnp.maximum(nb * i - 1, 0)))
    tab = pl.BlockSpec((tq, LANES), lambda b, i, *_: (i, 0))
    grid_spec = pltpu.PrefetchScalarGridSpec(
        num_scalar_prefetch=1,
        grid=(bsz, seq // tq),
        in_specs=[x_spec,
                  _const_spec((1, D_MODEL)),
                  _const_spec(wq.shape),
                  _const_spec((1, D_MODEL)),
                  _const_spec(ones.shape),
                  tab, tab, k_prev, k_cur, v_prev, v_cur,
                  _const_spec(wo.shape)],
        out_specs=x_spec)
    return pl.pallas_call(
        _attn_prompt_kernel,
        grid_spec=grid_spec,
        out_shape=jax.ShapeDtypeStruct(x.shape, F32),
        compiler_params=_params(("parallel", "arbitrary")),
        name="attn_prompt",
    )(sinks, x, g, wq, qgain, ones, cos, sin, kd, kd, vt, vt, wo)


def _own_head(shape):
    row_head = lax.broadcasted_iota(jnp.int32, shape, 0) & (N_HEADS - 1)
    lane_head = lax.broadcasted_iota(jnp.int32, shape, 1) // HEAD_DIM
    return row_head == lane_head


def _q_sample_kernel(x_ref, g_ref, wq_ref, qg_ref, ones_ref, cos_ref, sin_ref, rep_ref, place_ref, q_ref):
    h = _rms(x_ref[...], g_ref[...]).astype(BF16)
    q = _dot(h, wq_ref[...])
    q = q * lax.rsqrt(_seg_mean_sq(q, ones_ref) + EPS) * qg_ref[...]
    q = _rope(q, _tile_lanes(cos_ref[...], D_MODEL), _tile_lanes(sin_ref[...], D_MODEL))
    qb = (q * (HEAD_DIM ** -0.5)).astype(BF16)
    q_rep = _dot(rep_ref[...], qb)
    q_own = jnp.where(_own_head(q_rep.shape), q_rep, 0.0).astype(BF16)
    q_ref[...] = _dot(q_own, place_ref[...])


def _q_sample(x, g, wq, qgain, ones, cos, sin, rep, place):
    rows = x.shape[0]
    return pl.pallas_call(
        _q_sample_kernel,
        out_shape=jax.ShapeDtypeStruct((rows * N_HEADS, place.shape[1]), F32),
        compiler_params=pltpu.CompilerParams(vmem_limit_bytes=VMEM_LIMIT_BYTES),
        name="q_sample",
    )(x, g, wq, qgain, ones, cos, sin, rep, place)


def _cache_append_kernel(k_ref, v_ref, kn_ref, vn_ref, ko_ref, vo_ref):
    nb, lanes_kv, keys = k_ref.shape
    first = pl.program_id(0) * nb
    key = lax.broadcasted_iota(jnp.int32, (lanes_kv, keys), 1)
    seq = lax.broadcasted_iota(jnp.int32, (kn_ref.shape[1], keys), 0)

    def split3(x):
        hi = x.astype(BF16)
        r = x - hi.astype(F32)
        mid = r.astype(BF16)
        return hi, mid, (r - mid.astype(F32)).astype(BF16)

    parts = [(split3(kn_ref[...]), k_ref, ko_ref), (split3(vn_ref[...]), v_ref, vo_ref)]
    for b in range(nb):
        pick = (seq == first + b).astype(BF16)
        for (hi, mid, lo), old_ref, out_ref in parts:
            new_col = _dot(hi, pick) + _dot(mid, pick) + _dot(lo, pick)
            out_ref[b] = jnp.where(key == keys - 1, new_col, pltpu.roll(old_ref[b], keys - 1, 1))


def _cache_append(kt, vt, kn_t, vn_t):
    bsz, lanes_kv, keys = kt.shape
    nb = SUBLANES
    blk = pl.BlockSpec((nb, lanes_kv, keys), lambda i: (i, 0, 0))
    return pl.pallas_call(
        _cache_append_kernel,
        grid=(bsz // nb,),
        in_specs=[blk, blk, _const_spec(kn_t.shape), _const_spec(vn_t.shape)],
        out_specs=[blk, blk],
        out_shape=[jax.ShapeDtypeStruct(kt.shape, F32), jax.ShapeDtypeStruct(vt.shape, F32)],
        compiler_params=_params(("parallel",)),
        name="cache_append",
    )(kt, vt, kn_t, vn_t)


def _attn_sample_kernel(q_ref, k_ref, v_ref, sink_ref, o_ref):
    sink = sink_ref[...]
    for b in range(q_ref.shape[0]):
        s = _dot(q_ref[b].astype(BF16), k_ref[b].astype(BF16))
        m = jnp.maximum(jnp.max(s, axis=-1, keepdims=True), sink)
        p = jnp.exp(s - m)
        denom = jnp.sum(p, axis=-1, keepdims=True) + jnp.exp(sink - m)
        o = lax.dot_general(p.astype(BF16), v_ref[b].astype(BF16), (((1,), (1,)), ((), ())),
                            preferred_element_type=F32)
        o_ref[b] = o / denom


def _attn_sample(q3, kt, vt, sink_col):
    bsz, nh, nkv = q3.shape
    keys = kt.shape[2]
    bb = SUBLANES
    return pl.pallas_call(
        _attn_sample_kernel,
        grid=(bsz // bb,),
        in_specs=[pl.BlockSpec((bb, nh, nkv), lambda i: (i, 0, 0)),
                  pl.BlockSpec((bb, nkv, keys), lambda i: (i, 0, 0)),
                  pl.BlockSpec((bb, nkv, keys), lambda i: (i, 0, 0)),
                  _const_spec(sink_col.shape)],
        out_specs=pl.BlockSpec((bb, nh, nkv), lambda i: (i, 0, 0)),
        out_shape=jax.ShapeDtypeStruct((bsz, nh, nkv), F32),
        compiler_params=_params(("parallel",)),
        name="attn_sample",
    )(q3, kt, vt, sink_col)


def _proj_residual_kernel(x_ref, a_ref, place_t_ref, rep_t_ref, w_ref, o_ref):
    z = _dot(a_ref[...].astype(BF16), place_t_ref[...])
    z_own = jnp.where(_own_head(z.shape), z, 0.0).astype(BF16)
    attn = _dot(rep_t_ref[...], z_own).astype(BF16)
    o_ref[...] = x_ref[...] + _dot(attn, w_ref[...])


def _proj_residual(x, a, place_t, rep_t, w):
    return pl.pallas_call(
        _proj_residual_kernel,
        out_shape=jax.ShapeDtypeStruct(x.shape, F32),
        compiler_params=pltpu.CompilerParams(vmem_limit_bytes=VMEM_LIMIT_BYTES),
        name="o_proj_sample",
    )(x, a, place_t, rep_t, w)


def _rope_tables(pos):
    half = HEAD_DIM // 2
    inv = ROPE_THETA ** (-jnp.arange(half, dtype=F32) / half)
    ang = pos.astype(F32)[:, None] * inv[None, :]
    cos, sin = jnp.cos(ang), jnp.sin(ang)
    reps = LANES // HEAD_DIM
    return (jnp.tile(jnp.concatenate([cos, cos], axis=-1), (1, reps)),
            jnp.tile(jnp.concatenate([-sin, sin], axis=-1), (1, reps)))


def _block_ones(width, block):
    idx = jnp.arange(width) // block
    return (idx[:, None] == idx[None, :]).astype(BF16)


def _dup_heads(w):
    w3 = w.reshape(w.shape[0], N_KV_HEADS, 1, HEAD_DIM)
    return jnp.broadcast_to(w3, (w.shape[0], N_KV_HEADS, LANES // HEAD_DIM, HEAD_DIM)).reshape(w.shape[0], -1)


def _head_placement():
    lane = jnp.arange(N_HEADS * HEAD_DIM)
    dst = (lane // HEAD_DIM) // (N_HEADS // N_KV_HEADS) * HEAD_DIM + lane % HEAD_DIM
    return (dst[:, None] == jnp.arange(N_KV_HEADS * HEAD_DIM)[None, :]).astype(BF16)


def _row_replication(n):
    return (jnp.arange(n * N_HEADS)[:, None] // N_HEADS == jnp.arange(n)[None, :]).astype(BF16)


def kernel(x_prompt, x_sample, state_ssm_re, state_ssm_im, cache_k, cache_v, norm_mix, norm_mlp, ssm_a_re, ssm_a_im, ssm_log_dt, ssm_b_re, ssm_b_im, ssm_c_re, ssm_c_im, ssm_d, w_glu, b_glu, norm_kv, w_k, w_v, k_norm, w_q, q_norm, attn_sinks, w_o, w_mlp_in, w_mlp_out):
    bsz, seq, _ = x_prompt.shape
    dec = x_sample.shape[0]
    n_a = ssm_a_re.shape[0]
    depth = norm_mix.shape[0]
    past = 8192
    nkv = N_KV_HEADS * HEAD_DIM
    row = lambda v: v.astype(F32).reshape(1, -1)

    xp = x_prompt.astype(F32)
    xs = x_sample.reshape(dec, D_MODEL)
    rows3 = lambda v: v.astype(F32).reshape(v.shape[0], 1, -1)

    w1 = w_mlp_in.astype(BF16)
    w2 = w_mlp_out.astype(BF16)
    g_mlp = rows3(norm_mlp)
    mlp_p = lambda x, layer: _mlp(x.reshape(bsz * seq, D_MODEL), g_mlp, w1, w2, layer).reshape(bsz, seq, D_MODEL)

    b2, c2, m0, lam = _s5_weights(ssm_a_re, ssm_a_im, ssm_log_dt, ssm_b_re, ssm_b_im, ssm_c_re, ssm_c_im)
    b2_16 = b2.astype(BF16)
    wglu = w_glu.astype(BF16)
    g_mix, d_skip, bg = rows3(norm_mix), rows3(ssm_d), rows3(b_glu)
    sp_re, sp_im, ss_re, ss_im = [], [], [], []
    for i in range(n_a):
        xp, fin = _s5_prompt(xp, i, g_mix, b2_16, lam, c2, m0, d_skip, wglu, bg)
        re, im = _lanes_to_state(fin[bsz:])
        sp_re.append(re)
        sp_im.append(im)
        s0 = _state_to_lanes(state_ssm_re[i], state_ssm_im[i])
        xs, snew = _s5_sample(xs, i, g_mix, b2, lam, c2, d_skip, wglu, bg, s0)
        re, im = _lanes_to_state(snew)
        ss_re.append(re)
        ss_im.append(im)
        xp = mlp_p(xp, i)
        xs = _mlp(xs, g_mlp, w1, w2, i)

    cos_p, sin_p = _rope_tables(jnp.arange(seq, dtype=jnp.int32))
    cos_s, sin_s = _rope_tables(jnp.full((dec,), past, dtype=jnp.int32))
    ones_head = _block_ones(MXU_TILE, HEAD_DIM)
    place = _head_placement()
    rep = _row_replication(dec)
    kgain = row(k_norm)
    tile_gain = lambda gvec, n: jnp.tile(gvec, (1, n // HEAD_DIM))

    kd, vt = _kv(xp.reshape(bsz * seq, D_MODEL), row(norm_kv), _dup_heads(w_k), w_v,
                     tile_gain(kgain, 2 * nkv), ones_head, cos_p, sin_p, lambda i: i % (seq // KV_ROWS), seq=seq)
    ks_new, vs_new = _kv(xs, row(norm_kv), w_k, w_v, tile_gain(kgain, nkv), ones_head, cos_s, sin_s,
                         lambda i: i)
    kd = kd.reshape(bsz, seq, 2 * nkv)
    k_last, v_last = _kv(xp[:, -WINDOW:].reshape(bsz * WINDOW, D_MODEL), row(norm_kv), w_k, w_v,
                         tile_gain(kgain, nkv), ones_head, jnp.tile(cos_p[-WINDOW:], (bsz, 1)),
                         jnp.tile(sin_p[-WINDOW:], (bsz, 1)), lambda i: i)
    new_k_p = k_last.reshape(bsz, WINDOW, N_KV_HEADS, HEAD_DIM)
    new_v_p = v_last.reshape(bsz, WINDOW, N_KV_HEADS, HEAD_DIM)
    key_minor = lambda c: jnp.transpose(c.astype(F32), (0, 2, 3, 1)).reshape(dec, nkv, c.shape[1])
    keys_s, vals_s = _cache_append(key_minor(cache_k), key_minor(cache_v), ks_new.T, vs_new.T)
    key_major = lambda c: jnp.transpose(c.reshape(dec, N_KV_HEADS, HEAD_DIM, c.shape[2]), (0, 3, 1, 2))
    new_k_s, new_v_s = key_major(keys_s), key_major(vals_s)

    for j in range(depth - n_a):
        layer = n_a + j
        gn = row(norm_mix[layer])
        qgain = row(q_norm[j])
        wq = w_q[j].astype(BF16)
        wo = w_o[j].astype(BF16)
        xp = _attn_prompt(xp, gn, wq, tile_gain(qgain, D_MODEL), ones_head, cos_p, sin_p, kd, vt,
                          attn_sinks[j].astype(F32), wo)
        q_pad = _q_sample(xs, gn, wq, tile_gain(qgain, D_MODEL), ones_head, cos_s, sin_s, rep, place)
        o_pad = _attn_sample(q_pad.reshape(dec, N_HEADS, nkv), keys_s, vals_s,
                             attn_sinks[j].astype(F32).reshape(N_HEADS, 1))
        xs = _proj_residual(xs, o_pad.reshape(dec * N_HEADS, nkv), place.T, rep.T, wo)
        xp = mlp_p(xp, layer)
        xs = _mlp(xs, g_mlp, w1, w2, layer)

    return (xp, xs.reshape(dec, 1, D_MODEL),
            jnp.stack(sp_re), jnp.stack(sp_im), new_k_p, new_v_p,
            jnp.stack(ss_re), jnp.stack(ss_im), new_k_s, new_v_s)
```

```python
import functools
import math

import jax
import jax.numpy as jnp
from jax import lax
from jax.experimental import pallas as pl
from jax.experimental.pallas import tpu as pltpu

F32 = jnp.float32
BF16 = jnp.bfloat16

D_MODEL = 1024
N_GROUPS = 64
GROUP_SIZE = 16
STATE_DIM = 64
HEAD_DIM = 64
N_HEADS = 16
N_KV_HEADS = 4
WINDOW = 128
ROPE_THETA = 10000.0
D_FF = 4 * D_MODEL
EPS = 1e-6

SUBLANES = 8
LANES = 128
MXU_TILE = 256
VMEM_LIMIT_BYTES = 56 * 1024 * 1024

N_OCTETS = D_MODEL // LANES
OCTET_STATE = (LANES // GROUP_SIZE) * STATE_DIM
STATE_LANES = 2 * OCTET_STATE * N_OCTETS

S5_TIME_CHUNK = 128
MLP_ROWS = 1024
FF_CHUNK = 1024
KV_ROWS = 1024
ATTN_BLOCKS = 8


def _const_spec(shape):
    zeros = (0,) * len(shape)
    return pl.BlockSpec(shape, lambda *_: zeros, pipeline_mode=pl.Buffered(1))


def _layer_spec(shape, layer):
    idx = (layer,) + (0,) * (len(shape) - 1)
    return pl.BlockSpec((None,) + tuple(shape[1:]), lambda *_: idx, pipeline_mode=pl.Buffered(1))


def _params(sem):
    return pltpu.CompilerParams(dimension_semantics=sem, vmem_limit_bytes=VMEM_LIMIT_BYTES)


def _rms(x, g):
    return x * lax.rsqrt(jnp.mean(x * x, axis=-1, keepdims=True) + EPS) * g


def _dot(a, b):
    return jnp.dot(a, b, preferred_element_type=F32)


def _seg_mean_sq(x, ones_ref):
    sq = x * x
    hi = sq.astype(BF16)
    lo = (sq - hi.astype(F32)).astype(BF16)
    ones = ones_ref[...]
    outs = []
    for t in range(x.shape[-1] // MXU_TILE):
        sl = slice(MXU_TILE * t, MXU_TILE * (t + 1))
        outs.append(_dot(hi[:, sl], ones) + _dot(lo[:, sl], ones))
    return jnp.concatenate(outs, axis=-1) * (1.0 / HEAD_DIM)


def _tile_lanes(t, n):
    return jnp.concatenate([t] * (n // t.shape[-1]), axis=-1)


def _rope(x, cos, sin_signed):
    n = x.shape[-1]
    lane = lax.broadcasted_iota(jnp.int32, x.shape, 1)
    first = (lane & (HEAD_DIM - 1)) < (HEAD_DIM // 2)
    partner = jnp.where(first, pltpu.roll(x, n - HEAD_DIM // 2, 1), pltpu.roll(x, HEAD_DIM // 2, 1))
    return x * cos + partner * sin_signed


def _mlp_rows(x, g_ref, w1_ref, w2_ref):
    h = _rms(x, g_ref[...]).astype(BF16)
    acc = x
    for c in range(D_FF // FF_CHUNK):
        sl = slice(FF_CHUNK * c, FF_CHUNK * (c + 1))
        a = jnp.square(jnp.maximum(_dot(h, w1_ref[:, sl]), 0.0)).astype(BF16)
        acc = acc + _dot(a, w2_ref[sl, :])
    return acc


def _mlp_kernel(x_ref, xs_ref, g_ref, w1_ref, w2_ref, o_ref, os_ref):
    last = pl.num_programs(0) - 1

    @pl.when(pl.program_id(0) < last)
    def _():
        o_ref[...] = _mlp_rows(x_ref[...], g_ref, w1_ref, w2_ref)

    @pl.when(pl.program_id(0) == last)
    def _():
        os_ref[...] = _mlp_rows(xs_ref[...], g_ref, w1_ref, w2_ref)


def _mlp(x, xs, g, w1, w2, layer):
    rows = x.shape[0]
    tm = MLP_ROWS
    tiles = rows // tm
    x_spec = pl.BlockSpec((tm, D_MODEL), lambda i: (jnp.minimum(i, tiles - 1), 0))
    xs_spec = pl.BlockSpec(xs.shape, lambda i: (0, 0))
    return pl.pallas_call(
        _mlp_kernel,
        grid=(tiles + 1,),
        in_specs=[x_spec, xs_spec,
                  _layer_spec(g.shape, layer),
                  _layer_spec(w1.shape, layer),
                  _layer_spec(w2.shape, layer)],
        out_specs=[x_spec, xs_spec],
        out_shape=[jax.ShapeDtypeStruct((rows, D_MODEL), F32), jax.ShapeDtypeStruct(xs.shape, F32)],
        compiler_params=_params(("arbitrary",)),
        name="mlp",
    )(x, xs, g, w1, w2)


def _s5_gate(y, wglu_ref, bglu_ref):
    g = jax.nn.gelu(y).astype(BF16)
    z = _dot(g, wglu_ref[...]) + bglu_ref[...]
    return z[:, :D_MODEL] * jax.nn.sigmoid(z[:, D_MODEL:])


def _slabs(ref, rows=slice(None)):
    return jnp.concatenate([ref[c, rows, :] for c in range(ref.shape[0])], axis=-1)


def _s5_prompt_kernel(xa_ref, xb_ref, gn_ref, b2_ref, lam_ref, c2_ref, m0_ref, d_ref, wglu_ref, bglu_ref,
                      o_ref, fin_ref, st_ref, carry_ref, nat_ref, ue_ref, uo_ref, up_ref, tb2_ref, nat2_ref, y_ref):
    batch, tc, _ = xa_ref.shape
    assert 2 * batch == SUBLANES
    n_slabs = D_MODEL // LANES
    pairs = tc // 2
    rows_k = pairs * batch
    step = pl.program_id(0)
    n_chunks = pl.num_programs(0) - 1
    live = step < n_chunks

    @pl.when(step == 0)
    def _():
        carry_ref[...] = jnp.zeros_like(carry_ref)
        y_ref[...] = jnp.zeros_like(y_ref)
        up_ref[...] = jnp.zeros_like(up_ref)

    g_prev = jax.nn.gelu(y_ref[...]).astype(BF16)
    glu_cols = 2 * D_MODEL // (N_OCTETS // 2)
    z = []

    def glu_part(c):
        sl = slice(glu_cols * c, glu_cols * (c + 1))
        z.append(_dot(g_prev, wglu_ref[:, sl]) + bglu_ref[:, sl])

    def finish_prev():
        zs = jnp.concatenate(z, axis=-1)
        upd = zs[:, :D_MODEL] * jax.nn.sigmoid(zs[:, D_MODEL:])
        for c in range(n_slabs):
            tb2_ref[c] = upd[:, LANES * c:LANES * (c + 1)]
        for b in range(batch):
            for c in range(n_slabs):
                nat2_ref[c, pl.ds(0, pairs, stride=2), :] = tb2_ref[c, pl.ds(b, pairs, stride=batch), :]
                nat2_ref[c, pl.ds(1, pairs, stride=2), :] = tb2_ref[c, pl.ds(rows_k + b, pairs, stride=batch), :]
            o_ref[b] = xb_ref[b] + _slabs(nat2_ref)

    for c in range(n_slabs):
        up_ref[c, 0:batch, :] = up_ref[c, rows_k:rows_k + batch, :]
    for b in range(batch):
        u_nat = _rms(xa_ref[b], gn_ref[...])
        for c in range(n_slabs):
            nat_ref[c] = u_nat[:, LANES * c:LANES * (c + 1)]
        for c in range(n_slabs):
            even = nat_ref[c, pl.ds(0, pairs, stride=2), :]
            odd = nat_ref[c, pl.ds(1, pairs, stride=2), :]
            ue_ref[c, pl.ds(b, pairs, stride=batch), :] = even
            uo_ref[c, pl.ds(b, pairs, stride=batch), :] = odd
            up_ref[c, pl.ds(batch + b, pairs, stride=batch), :] = odd
    ue, uo = _slabs(ue_ref), _slabs(uo_ref)
    ue_b, uo_b, up_b = ue.astype(BF16), uo.astype(BF16), _slabs(up_ref, slice(0, rows_k)).astype(BF16)

    lo = lax.broadcasted_iota(jnp.int32, (SUBLANES, LANES), 0) < batch
    swap = lambda v: pltpu.roll(v, batch, 0)
    n_vregs = rows_k // SUBLANES
    half = OCTET_STATE // LANES
    y_even, y_odd = [], []
    for q in range(N_OCTETS):
        ch = slice(LANES * q, LANES * (q + 1))
        lhs = jnp.concatenate([up_b[:, ch], ue_b[:, ch]], axis=1)
        st_ref[:, 2 * OCTET_STATE * q:2 * OCTET_STATE * (q + 1)] = _dot(lhs, b2_ref[q])

    def project_out(octets):
        for q in octets:
            ch = slice(LANES * q, LANES * (q + 1))
            y2 = _dot(st_ref[:, 2 * OCTET_STATE * q:2 * OCTET_STATE * (q + 1)].astype(BF16), c2_ref[q])
            y_even.append(y2[:, :LANES])
            y_odd.append(y2[:, LANES:] + _dot(uo_b[:, ch], m0_ref[q]))

    for q0 in range(0, N_OCTETS, 2):
        octets = (q0, q0 + 1)
        glu_part(q0 // 2)
        if q0 > 0:
            project_out((q0 - 2, q0 - 1))
        if q0 == N_OCTETS - 2:
            finish_prev()
        chains = []
        for q in octets:
            for j in range(half // 2):
                cols = [2 * OCTET_STATE * q + LANES * jj for jj in (j, j + half // 2)]
                lanes = [OCTET_STATE * q + LANES * jj for jj in (j, j + half // 2)]
                chains.append((cols, lanes))
        lam_r, lam_i, sr, si, old = [], [], [], [], []
        for (ca, cb), (la, lb) in chains:
            bc = lambda r, c: jnp.broadcast_to(lam_ref[r:r + 1, c:c + LANES], (SUBLANES, LANES))
            lam_r.append(jnp.where(lo, bc(2, la), bc(2, lb)))
            lam_i.append(jnp.where(lo, bc(3, la), bc(3, lb)))
            prev = [carry_ref[:, c:c + LANES] for c in (ca, cb, ca + OCTET_STATE, cb + OCTET_STATE)]
            old.append(prev)
            sr.append(jnp.where(lo, swap(prev[0]), prev[1]))
            si.append(jnp.where(lo, swap(prev[2]), prev[3]))
        last = [None] * len(chains)
        for m in range(n_vregs):
            rows = slice(SUBLANES * m, SUBLANES * (m + 1))
            for j, ((ca, cb), _) in enumerate(chains):
                ar, br = st_ref[rows, ca:ca + LANES], st_ref[rows, cb:cb + LANES]
                ai, bi = (st_ref[rows, ca + OCTET_STATE:ca + OCTET_STATE + LANES],
                          st_ref[rows, cb + OCTET_STATE:cb + OCTET_STATE + LANES])
                d0r, d1r = jnp.where(lo, ar, swap(br)), jnp.where(lo, swap(ar), br)
                d0i, d1i = jnp.where(lo, ai, swap(bi)), jnp.where(lo, swap(ai), bi)
                s0r = lam_r[j] * sr[j] - lam_i[j] * si[j] + d0r
                s0i = lam_r[j] * si[j] + lam_i[j] * sr[j] + d0i
                s1r = lam_r[j] * s0r - lam_i[j] * s0i + d1r
                s1i = lam_r[j] * s0i + lam_i[j] * s0r + d1i
                sr[j], si[j] = s1r, s1i
                out = (jnp.where(lo, s0r, swap(s1r)), jnp.where(lo, swap(s0r), s1r),
                       jnp.where(lo, s0i, swap(s1i)), jnp.where(lo, swap(s0i), s1i))
                st_ref[rows, ca:ca + LANES] = out[0]
                st_ref[rows, cb:cb + LANES] = out[1]
                st_ref[rows, ca + OCTET_STATE:ca + OCTET_STATE + LANES] = out[2]
                st_ref[rows, cb + OCTET_STATE:cb + OCTET_STATE + LANES] = out[3]
                last[j] = out
        for j, ((ca, cb), _) in enumerate(chains):
            for c, new, prev in zip((ca, cb, ca + OCTET_STATE, cb + OCTET_STATE), last[j], old[j]):
                carry_ref[:, c:c + LANES] = jnp.where(live, new, prev)
    project_out((N_OCTETS - 2, N_OCTETS - 1))
    y_ref[0:rows_k, :] = jnp.concatenate(y_even, axis=-1) + d_ref[...] * ue
    y_ref[rows_k:2 * rows_k, :] = jnp.concatenate(y_odd, axis=-1) + d_ref[...] * uo

    @pl.when(step == n_chunks - 1)
    def _():
        tail = uo_b[rows_k - 2 * SUBLANES:, :]
        zeros = jnp.zeros((2 * SUBLANES, LANES), BF16)
        for q in range(N_OCTETS):
            bu = _dot(jnp.concatenate([zeros, tail[:, LANES * q:LANES * (q + 1)]], axis=1), b2_ref[q])[SUBLANES:]
            base = 2 * OCTET_STATE * q
            lr = lam_ref[0:1, OCTET_STATE * q:OCTET_STATE * (q + 1)]
            li = lam_ref[1:2, OCTET_STATE * q:OCTET_STATE * (q + 1)]
            s_r = carry_ref[:, base:base + OCTET_STATE]
            s_i = carry_ref[:, base + OCTET_STATE:base + 2 * OCTET_STATE]
            fin_ref[:, base:base + OCTET_STATE] = lr * s_r - li * s_i + bu[:, :OCTET_STATE]
            fin_ref[:, base + OCTET_STATE:base + 2 * OCTET_STATE] = lr * s_i + li * s_r + bu[:, OCTET_STATE:]


def _s5_prompt(x, layer, gn, b2, lam, c2, m0, d, wglu, bglu):
    batch, seq, _ = x.shape
    tc = S5_TIME_CHUNK
    n_chunks = seq // tc
    rows_k = tc // 2 * batch
    n_slabs = D_MODEL // LANES
    blk = (batch, tc, D_MODEL)
    this_chunk = pl.BlockSpec(blk, lambda s: (0, jnp.minimum(s, n_chunks - 1), 0))
    prev_chunk = pl.BlockSpec(blk, lambda s: (0, jnp.maximum(s - 1, 0), 0))
    slab = lambda rows: pltpu.VMEM((n_slabs, rows, LANES), F32)
    return pl.pallas_call(
        _s5_prompt_kernel,
        grid=(n_chunks + 1,),
        in_specs=[this_chunk, prev_chunk]
        + [_layer_spec(w.shape, layer) for w in (gn, b2, lam, c2, m0, d, wglu, bglu)],
        out_specs=[prev_chunk, pl.BlockSpec((SUBLANES, STATE_LANES), lambda s: (0, 0))],
        out_shape=[jax.ShapeDtypeStruct(x.shape, F32),
                   jax.ShapeDtypeStruct((SUBLANES, STATE_LANES), F32)],
        scratch_shapes=[pltpu.VMEM((rows_k, STATE_LANES), F32),
                        pltpu.VMEM((SUBLANES, STATE_LANES), F32),
                        slab(tc), slab(rows_k), slab(rows_k), slab(rows_k + SUBLANES),
                        slab(2 * rows_k), slab(tc),
                        pltpu.VMEM((2 * rows_k, D_MODEL), F32)],
        compiler_params=_params(("arbitrary",)),
        name="s5_prompt",
    )(x, x, gn, b2, lam, c2, m0, d, wglu, bglu)


def _s5_sample_kernel(x_ref, gn_ref, b2_ref, lam_ref, c2_ref, d_ref, wglu_ref, bglu_ref, s0_ref,
                      o_ref, snew_ref):
    x = x_ref[...]
    u = _rms(x, gn_ref[...])
    ys = []
    for q in range(N_OCTETS):
        base = 2 * OCTET_STATE * q
        bu = jnp.dot(u[:, LANES * q:LANES * (q + 1)], b2_ref[q, LANES:, :],
                     preferred_element_type=F32, precision=lax.Precision.HIGHEST)
        lr = lam_ref[0:1, OCTET_STATE * q:OCTET_STATE * (q + 1)]
        li = lam_ref[1:2, OCTET_STATE * q:OCTET_STATE * (q + 1)]
        s0r = s0_ref[:, base:base + OCTET_STATE]
        s0i = s0_ref[:, base + OCTET_STATE:base + 2 * OCTET_STATE]
        sr = lr * s0r - li * s0i + bu[:, :OCTET_STATE]
        si = lr * s0i + li * s0r + bu[:, OCTET_STATE:]
        snew_ref[:, base:base + OCTET_STATE] = sr
        snew_ref[:, base + OCTET_STATE:base + 2 * OCTET_STATE] = si
        sb = jnp.concatenate([sr, si], axis=-1).astype(BF16)
        ys.append(_dot(sb, c2_ref[q, :, :LANES]))
    y = jnp.concatenate(ys, axis=-1)
    o_ref[...] = x + _s5_gate(y + d_ref[...] * u, wglu_ref, bglu_ref)


def _s5_sample(x, layer, gn, b2_f32, lam, c2, d, wglu, bglu, s0):
    rows = x.shape[0]
    return pl.pallas_call(
        _s5_sample_kernel,
        grid=(1,),
        in_specs=[_const_spec(x.shape)] + [_layer_spec(w.shape, layer) for w in (gn, b2_f32, lam, c2, d, wglu, bglu)]
        + [_const_spec(s0.shape)],
        out_specs=[pl.BlockSpec((rows, D_MODEL), lambda i: (0, 0)),
                   pl.BlockSpec((rows, STATE_LANES), lambda i: (0, 0))],
        out_shape=[jax.ShapeDtypeStruct((rows, D_MODEL), F32),
                   jax.ShapeDtypeStruct((rows, STATE_LANES), F32)],
        compiler_params=_params(("arbitrary",)),
        name="s5_sample",
    )(x, gn, b2_f32, lam, c2, d, wglu, bglu, s0)


def _block_diag(w):
    n, octets, groups, r, c = w.shape
    col = jnp.arange(groups * c)
    tile = (jnp.arange(c)[:, None] == col % c).astype(F32)
    own = (jnp.arange(groups * r)[:, None] // r == col // c).astype(F32)
    tiled = jnp.einsum("nqrc,cl->nqrl", w.reshape(n, octets, groups * r, c), tile, precision=lax.Precision.HIGHEST)
    return tiled * own


def _s5_weights(a_re, a_im, log_dt, b_re, b_im, c_re, c_im):
    n = a_re.shape[0]
    dt = jnp.exp(log_dt.astype(F32))
    lam = lax.complex(a_re.astype(F32), a_im.astype(F32))
    lam_bar = jnp.exp(lam * dt)
    b = lax.complex(b_re.astype(F32), b_im.astype(F32))
    b_bar = ((lam_bar - 1.0) / lam)[..., None] * b
    c = lax.complex(c_re.astype(F32), c_im.astype(F32))
    gpo = N_GROUPS // N_OCTETS
    octets = lambda w: w.reshape((n, N_OCTETS, gpo) + w.shape[2:])

    def into_state(w):
        wt = jnp.swapaxes(octets(w), 3, 4)
        return jnp.concatenate([_block_diag(jnp.real(wt)), _block_diag(jnp.imag(wt))], axis=-1)

    def out_of_state(w):
        wt = jnp.swapaxes(octets(w), 3, 4)
        return jnp.concatenate([_block_diag(jnp.real(wt)), -_block_diag(jnp.imag(wt))], axis=2)

    b2 = jnp.concatenate([into_state(lam_bar[..., None] * b_bar), into_state(b_bar)], axis=2)
    c2 = jnp.concatenate([out_of_state(c), out_of_state(c * lam_bar[:, :, None, :])], axis=-1)
    cb = jnp.real(jnp.einsum("ngcp,ngpd->ngdc", c, b_bar))
    m0 = _block_diag(octets(cb))
    lam2 = lam_bar * lam_bar
    lam_rows = jnp.stack([jnp.real(lam_bar).reshape(n, -1), jnp.imag(lam_bar).reshape(n, -1),
                          jnp.real(lam2).reshape(n, -1), jnp.imag(lam2).reshape(n, -1)], axis=1)
    return b2, c2.astype(BF16), m0.astype(BF16), lam_rows


def _state_to_lanes(s_re, s_im):
    n = s_re.shape[0]
    re = s_re.astype(F32).reshape(n, N_OCTETS, OCTET_STATE)
    im = s_im.astype(F32).reshape(n, N_OCTETS, OCTET_STATE)
    return jnp.concatenate([re, im], axis=-1).reshape(n, STATE_LANES)


def _lanes_to_state(s):
    n = s.shape[0]
    s4 = s.reshape(n, N_OCTETS, 2, OCTET_STATE)
    return (s4[:, :, 0].reshape(n, N_GROUPS, STATE_DIM), s4[:, :, 1].reshape(n, N_GROUPS, STATE_DIM))


def _rope_partner(n):
    lane = jnp.arange(n)
    return jnp.where(lane % HEAD_DIM < HEAD_DIM // 2, lane + HEAD_DIM // 2, lane - HEAD_DIM // 2)


def _kv_kernel(x_ref, g_ref, wkv_ref, kg_ref, kgp_ref, ones_ref, cos_ref, sin_ref, *rest):
    if len(rest) == 3:
        wvt_ref, k_ref, vt_ref = rest
        v_ref = None
    else:
        (k_ref, v_ref), wvt_ref, vt_ref = rest, None, None
    h = _rms(x_ref[...], g_ref[...]).astype(BF16)
    kv = _dot(h, wkv_ref[...])
    n = k_ref.shape[-1]
    k, k_partner = kv[:, :n], kv[:, n:2 * n]
    scale = lax.rsqrt(_seg_mean_sq(k, ones_ref) + EPS)
    k_ref[...] = scale * (k * (kg_ref[...] * _tile_lanes(cos_ref[...], n))
                          + k_partner * (kgp_ref[...] * _tile_lanes(sin_ref[...], n)))
    if v_ref is not None:
        v_ref[...] = kv[:, 2 * n:]
    else:
        vt_ref[...] = lax.dot_general(wvt_ref[...], h, (((1,), (1,)), ((), ())), preferred_element_type=F32)


def _kv(x, g, wk, wv, kgain, ones, cos, sin, table_block, seq=None):
    rows = x.shape[0]
    n, nv = wk.shape[1], wv.shape[1]
    partner = _rope_partner(n)
    wkv = jnp.concatenate([wk, wk[:, partner]] + ([wv] if seq is None else []), axis=1).astype(BF16)
    tm = min(KV_ROWS, rows)
    in_specs = [pl.BlockSpec((tm, D_MODEL), lambda i: (i, 0)),
                _const_spec((1, D_MODEL)),
                _const_spec(wkv.shape),
                _const_spec((1, n)),
                _const_spec((1, n)),
                _const_spec(ones.shape),
                pl.BlockSpec((tm, LANES), lambda i: (table_block(i), 0)),
                pl.BlockSpec((tm, LANES), lambda i: (table_block(i), 0))]
    out_specs = [pl.BlockSpec((tm, n), lambda i: (i, 0))]
    out_shape = [jax.ShapeDtypeStruct((rows, n), F32)]
    args = [x, g, wkv, kgain, kgain[:, partner], ones, cos, sin]
    if seq is None:
        out_specs.append(pl.BlockSpec((tm, nv), lambda i: (i, 0)))
        out_shape.append(jax.ShapeDtypeStruct((rows, nv), F32))
    else:
        per_seq = seq // tm
        in_specs.append(_const_spec((nv, D_MODEL)))
        args.append(wv.T.astype(BF16))
        out_specs.append(pl.BlockSpec((None, nv, tm), lambda i: (i // per_seq, 0, i % per_seq)))
        out_shape.append(jax.ShapeDtypeStruct((rows // seq, nv, seq), F32))
    return pl.pallas_call(
        _kv_kernel,
        grid=(rows // tm,),
        in_specs=in_specs,
        out_specs=out_specs,
        out_shape=out_shape,
        compiler_params=_params(("parallel",)),
        name="kv_proj",
    )(*args)


def _attn_prompt_kernel(sinks_ref, x_ref, g_ref, wq_ref, qg_ref, ones_ref, cos_ref, sin_ref,
                        kp_ref, kc_ref, vp_ref, vc_ref, wo_ref, o_ref):
    tile = pl.program_id(1)
    x = x_ref[...]
    n_blocks = x.shape[0] // WINDOW
    heads_per_kv = N_HEADS // N_KV_HEADS
    cols = heads_per_kv * WINDOW
    h = _rms(x, g_ref[...]).astype(BF16)
    q = _dot(h, wq_ref[...])
    q = q * lax.rsqrt(_seg_mean_sq(q, ones_ref) + EPS) * qg_ref[...]
    q = _rope(q, _tile_lanes(cos_ref[...], D_MODEL), _tile_lanes(sin_ref[...], D_MODEL))
    qb = (q * (HEAD_DIM ** -0.5)).astype(BF16)

    kall = jnp.concatenate([kp_ref[...], kc_ref[...]], axis=0).astype(BF16)
    vall = jnp.concatenate([vp_ref[...], vc_ref[...]], axis=1).astype(BF16)

    kj = lax.broadcasted_iota(jnp.int32, (WINDOW, cols), 0)
    col = lax.broadcasted_iota(jnp.int32, (WINDOW, cols), 1)
    from_prev = kj > (col & (WINDOW - 1))
    col_head = lax.broadcasted_iota(jnp.int32, (1, cols), 1) // WINDOW
    low_half = lax.broadcasted_iota(jnp.int32, (WINDOW, LANES), 1) < HEAD_DIM
    neg_inf = jnp.full((WINDOW, cols), -jnp.inf, F32)

    pairs = [(blk, g) for blk in range(n_blocks) for g in range(N_KV_HEADS)]
    scores, sinks = [], []
    for blk, g in pairs:
        r0 = WINDOW * blk
        kg = kall[r0:r0 + 2 * WINDOW, LANES * g:LANES * (g + 1)]
        qs = []
        sink = jnp.zeros((1, cols), F32)
        for hh in range(heads_per_kv):
            head = heads_per_kv * g + hh
            qp = qb[r0:r0 + WINDOW, LANES * (head // 2):LANES * (head // 2 + 1)]
            keep = low_half if head % 2 == 0 else jnp.logical_not(low_half)
            qs.append(jnp.where(keep, qp, jnp.zeros_like(qp)))
            sink = jnp.where(col_head == hh, sinks_ref[head], sink)
        sinks.append(sink)
        scores.append(lax.dot_general(kg, jnp.concatenate(qs, axis=0), (((1,), (1,)), ((), ())),
                                      preferred_element_type=F32))
    probs, denoms = [], []
    for (blk, g), s, sink in zip(pairs, scores, sinks):
        has_prev = tile * n_blocks + blk > 0
        t = jnp.where(from_prev, jnp.where(has_prev, s[:WINDOW], neg_inf), s[WINDOW:])
        m = jnp.maximum(jnp.max(t, axis=0, keepdims=True), sink)
        p = jnp.exp(t - m)
        denoms.append(jnp.sum(p, axis=0, keepdims=True) + jnp.exp(sink - m))
        zero = jnp.zeros_like(p)
        probs.append(jnp.concatenate([jnp.where(from_prev, p, zero), jnp.where(from_prev, zero, p)],
                                     axis=0).astype(BF16))
    head_outs = [[] for _ in range(n_blocks)]
    for (blk, g), pcat, denom in zip(pairs, probs, denoms):
        r0 = WINDOW * blk
        vg = vall[HEAD_DIM * g:HEAD_DIM * (g + 1), r0:r0 + 2 * WINDOW]
        o = _dot(vg, pcat) / denom
        head_outs[blk] += [o[:, WINDOW * hh:WINDOW * (hh + 1)] for hh in range(heads_per_kv)]
    block_outs = [jnp.concatenate(outs, axis=0).T for outs in head_outs]
    attn = jnp.concatenate(block_outs, axis=0).astype(BF16)
    o_ref[...] = x + _dot(attn, wo_ref[...])


def _attn_prompt(x, g, wq, qgain, ones, cos, sin, kd, vt, sinks, wo):
    bsz, seq, _ = x.shape
    nk, nv = kd.shape[-1], vt.shape[1]
    nb = ATTN_BLOCKS
    tq = nb * WINDOW
    x_spec = pl.BlockSpec((None, tq, D_MODEL), lambda b, i, *_: (b, i, 0))
    k_cur = pl.BlockSpec((None, tq, nk), lambda b, i, *_: (b, i, 0))
    k_prev = pl.BlockSpec((None, WINDOW, nk), lambda b, i, *_: (b, jnp.maximum(nb * i - 1, 0), 0))
    v_cur = pl.BlockSpec((None, nv, tq), lambda b, i, *_: (b, 0, i))
    v_prev = pl.BlockSpec((None, nv, WINDOW), lambda b, i, *_: (b, 0, jnp.maximum(nb * i - 1, 0)))
    tab = pl.BlockSpec((tq, LANES), lambda b, i, *_: (i, 0))
    grid_spec = pltpu.PrefetchScalarGridSpec(
        num_scalar_prefetch=1,
        grid=(bsz, seq // tq),
        in_specs=[x_spec,
                  _const_spec((1, D_MODEL)),
                  _const_spec(wq.shape),
                  _const_spec((1, D_MODEL)),
                  _const_spec(ones.shape),
                  tab, tab, k_prev, k_cur, v_prev, v_cur,
                  _const_spec(wo.shape)],
        out_specs=x_spec)
    return pl.pallas_call(
        _attn_prompt_kernel,
        grid_spec=grid_spec,
        out_shape=jax.ShapeDtypeStruct(x.shape, F32),
        compiler_params=_params(("parallel", "arbitrary")),
        name="attn_prompt",
    )(sinks, x, g, wq, qgain, ones, cos, sin, kd, kd, vt, vt, wo)


def _own_head(shape):
    row_head = lax.broadcasted_iota(jnp.int32, shape, 0) & (N_HEADS - 1)
    lane_head = lax.broadcasted_iota(jnp.int32, shape, 1) // HEAD_DIM
    return row_head == lane_head


def _q_sample_kernel(x_ref, g_ref, wq_ref, qg_ref, ones_ref, cos_ref, sin_ref, rep_ref, place_ref, q_ref):
    h = _rms(x_ref[...], g_ref[...]).astype(BF16)
    q = _dot(h, wq_ref[...])
    q = q * lax.rsqrt(_seg_mean_sq(q, ones_ref) + EPS) * qg_ref[...]
    q = _rope(q, _tile_lanes(cos_ref[...], D_MODEL), _tile_lanes(sin_ref[...], D_MODEL))
    qb = (q * (HEAD_DIM ** -0.5)).astype(BF16)
    q_rep = _dot(rep_ref[...], qb)
    q_own = jnp.where(_own_head(q_rep.shape), q_rep, 0.0).astype(BF16)
    q_ref[...] = _dot(q_own, place_ref[...])


def _q_sample(x, g, wq, qgain, ones, cos, sin, rep, place):
    rows = x.shape[0]
    return pl.pallas_call(
        _q_sample_kernel,
        out_shape=jax.ShapeDtypeStruct((rows * N_HEADS, place.shape[1]), F32),
        compiler_params=pltpu.CompilerParams(vmem_limit_bytes=VMEM_LIMIT_BYTES),
        name="q_sample",
    )(x, g, wq, qgain, ones, cos, sin, rep, place)


def _cache_append_kernel(k_ref, v_ref, kn_ref, vn_ref, ko_ref, vo_ref):
    nb, lanes_kv, keys = k_ref.shape
    first = pl.program_id(0) * nb
    key = lax.broadcasted_iota(jnp.int32, (lanes_kv, keys), 1)
    seq = lax.broadcasted_iota(jnp.int32, (kn_ref.shape[1], keys), 0)

    def split3(x):
        hi = x.astype(BF16)
        r = x - hi.astype(F32)
        mid = r.astype(BF16)
        return hi, mid, (r - mid.astype(F32)).astype(BF16)

    parts = [(split3(kn_ref[...]), k_ref, ko_ref), (split3(vn_ref[...]), v_ref, vo_ref)]
    for b in range(nb):
        pick = (seq == first + b).astype(BF16)
        for (hi, mid, lo), old_ref, out_ref in parts:
            new_col = _dot(hi, pick) + _dot(mid, pick) + _dot(lo, pick)
            out_ref[b] = jnp.where(key == keys - 1, new_col, pltpu.roll(old_ref[b], keys - 1, 1))


def _cache_append(kt, vt, kn_t, vn_t):
    bsz, lanes_kv, keys = kt.shape
    nb = SUBLANES
    blk = pl.BlockSpec((nb, lanes_kv, keys), lambda i: (i, 0, 0))
    return pl.pallas_call(
        _cache_append_kernel,
        grid=(bsz // nb,),
        in_specs=[blk, blk, _const_spec(kn_t.shape), _const_spec(vn_t.shape)],
        out_specs=[blk, blk],
        out_shape=[jax.ShapeDtypeStruct(kt.shape, F32), jax.ShapeDtypeStruct(vt.shape, F32)],
        compiler_params=_params(("parallel",)),
        name="cache_append",
    )(kt, vt, kn_t, vn_t)


def _attn_sample_kernel(q_ref, k_ref, v_ref, sink_ref, o_ref):
    sink = sink_ref[...]
    for b in range(q_ref.shape[0]):
        s = _dot(q_ref[b].astype(BF16), k_ref[b].astype(BF16))
        m = jnp.maximum(jnp.max(s, axis=-1, keepdims=True), sink)
        p = jnp.exp(s - m)
        denom = jnp.sum(p, axis=-1, keepdims=True) + jnp.exp(sink - m)
        o = lax.dot_general(p.astype(BF16), v_ref[b].astype(BF16), (((1,), (1,)), ((), ())),
                            preferred_element_type=F32)
        o_ref[b] = o / denom


def _attn_sample(q3, kt, vt, sink_col):
    bsz, nh, nkv = q3.shape
    keys = kt.shape[2]
    bb = SUBLANES
    return pl.pallas_call(
        _attn_sample_kernel,
        grid=(bsz // bb,),
        in_specs=[pl.BlockSpec((bb, nh, nkv), lambda i: (i, 0, 0)),
                  pl.BlockSpec((bb, nkv, keys), lambda i: (i, 0, 0)),
                  pl.BlockSpec((bb, nkv, keys), lambda i: (i, 0, 0)),
                  _const_spec(sink_col.shape)],
        out_specs=pl.BlockSpec((bb, nh, nkv), lambda i: (i, 0, 0)),
        out_shape=jax.ShapeDtypeStruct((bsz, nh, nkv), F32),
        compiler_params=_params(("parallel",)),
        name="attn_sample",
    )(q3, kt, vt, sink_col)


def _proj_residual_kernel(x_ref, a_ref, place_t_ref, rep_t_ref, w_ref, o_ref):
    z = _dot(a_ref[...].astype(BF16), place_t_ref[...])
    z_own = jnp.where(_own_head(z.shape), z, 0.0).astype(BF16)
    attn = _dot(rep_t_ref[...], z_own).astype(BF16)
    o_ref[...] = x_ref[...] + _dot(attn, w_ref[...])


def _proj_residual(x, a, place_t, rep_t, w):
    return pl.pallas_call(
        _proj_residual_kernel,
        out_shape=jax.ShapeDtypeStruct(x.shape, F32),
        compiler_params=pltpu.CompilerParams(vmem_limit_bytes=VMEM_LIMIT_BYTES),
        name="o_proj_sample",
    )(x, a, place_t, rep_t, w)


def _rope_tables(pos):
    half = HEAD_DIM // 2
    inv = ROPE_THETA ** (-jnp.arange(half, dtype=F32) / half)
    ang = pos.astype(F32)[:, None] * inv[None, :]
    cos, sin = jnp.cos(ang), jnp.sin(ang)
    reps = LANES // HEAD_DIM
    return (jnp.tile(jnp.concatenate([cos, cos], axis=-1), (1, reps)),
            jnp.tile(jnp.concatenate([-sin, sin], axis=-1), (1, reps)))


def _block_ones(width, block):
    idx = jnp.arange(width) // block
    return (idx[:, None] == idx[None, :]).astype(BF16)


def _dup_heads(w):
    w3 = w.reshape(w.shape[0], N_KV_HEADS, 1, HEAD_DIM)
    return jnp.broadcast_to(w3, (w.shape[0], N_KV_HEADS, LANES // HEAD_DIM, HEAD_DIM)).reshape(w.shape[0], -1)


def _head_placement():
    lane = jnp.arange(N_HEADS * HEAD_DIM)
    dst = (lane // HEAD_DIM) // (N_HEADS // N_KV_HEADS) * HEAD_DIM + lane % HEAD_DIM
    return (dst[:, None] == jnp.arange(N_KV_HEADS * HEAD_DIM)[None, :]).astype(BF16)


def _row_replication(n):
    return (jnp.arange(n * N_HEADS)[:, None] // N_HEADS == jnp.arange(n)[None, :]).astype(BF16)


def kernel(x_prompt, x_sample, state_ssm_re, state_ssm_im, cache_k, cache_v, norm_mix, norm_mlp, ssm_a_re, ssm_a_im, ssm_log_dt, ssm_b_re, ssm_b_im, ssm_c_re, ssm_c_im, ssm_d, w_glu, b_glu, norm_kv, w_k, w_v, k_norm, w_q, q_norm, attn_sinks, w_o, w_mlp_in, w_mlp_out):
    bsz, seq, _ = x_prompt.shape
    dec = x_sample.shape[0]
    n_a = ssm_a_re.shape[0]
    depth = norm_mix.shape[0]
    past = 8192
    nkv = N_KV_HEADS * HEAD_DIM
    row = lambda v: v.astype(F32).reshape(1, -1)

    xp = x_prompt.astype(F32)
    xs = x_sample.reshape(dec, D_MODEL)
    rows3 = lambda v: v.astype(F32).reshape(v.shape[0], 1, -1)

    w1 = w_mlp_in.astype(BF16)
    w2 = w_mlp_out.astype(BF16)
    g_mlp = rows3(norm_mlp)
    def mlp_both(x, xs, layer):
        y, ys = _mlp(x.reshape(bsz * seq, D_MODEL), xs, g_mlp, w1, w2, layer)
        return y.reshape(bsz, seq, D_MODEL), ys


    b2, c2, m0, lam = _s5_weights(ssm_a_re, ssm_a_im, ssm_log_dt, ssm_b_re, ssm_b_im, ssm_c_re, ssm_c_im)
    b2_16 = b2.astype(BF16)
    wglu = w_glu.astype(BF16)
    g_mix, d_skip, bg = rows3(norm_mix), rows3(ssm_d), rows3(b_glu)
    sp_re, sp_im, ss_re, ss_im = [], [], [], []
    for i in range(n_a):
        xp, fin = _s5_prompt(xp, i, g_mix, b2_16, lam, c2, m0, d_skip, wglu, bg)
        re, im = _lanes_to_state(fin[bsz:])
        sp_re.append(re)
        sp_im.append(im)
        s0 = _state_to_lanes(state_ssm_re[i], state_ssm_im[i])
        xs, snew = _s5_sample(xs, i, g_mix, b2, lam, c2, d_skip, wglu, bg, s0)
        re, im = _lanes_to_state(snew)
        ss_re.append(re)
        ss_im.append(im)
        xp, xs = mlp_both(xp, xs, i)

    cos_p, sin_p = _rope_tables(jnp.arange(seq, dtype=jnp.int32))
    cos_s, sin_s = _rope_tables(jnp.full((dec,), past, dtype=jnp.int32))
    ones_head = _block_ones(MXU_TILE, HEAD_DIM)
    place = _head_placement()
    rep = _row_replication(dec)
    kgain = row(k_norm)
    tile_gain = lambda gvec, n: jnp.tile(gvec, (1, n // HEAD_DIM))

    kd, vt = _kv(xp.reshape(bsz * seq, D_MODEL), row(norm_kv), _dup_heads(w_k), w_v,
                     tile_gain(kgain, 2 * nkv), ones_head, cos_p, sin_p, lambda i: i % (seq // KV_ROWS), seq=seq)
    ks_new, vs_new = _kv(xs, row(norm_kv), w_k, w_v, tile_gain(kgain, nkv), ones_head, cos_s, sin_s,
                         lambda i: i)
    kd = kd.reshape(bsz, seq, 2 * nkv)
    new_k_p = kd[:, -WINDOW:].reshape(bsz, WINDOW, N_KV_HEADS, LANES // HEAD_DIM, HEAD_DIM)[:, :, :, 0]
    new_v_p = jnp.transpose(vt[:, :, -WINDOW:].reshape(bsz, N_KV_HEADS, HEAD_DIM, WINDOW), (0, 3, 1, 2))
    key_minor = lambda c: jnp.transpose(c.astype(F32), (0, 2, 3, 1)).reshape(dec, nkv, c.shape[1])
    keys_s, vals_s = _cache_append(key_minor(cache_k), key_minor(cache_v), ks_new.T, vs_new.T)
    key_major = lambda c: jnp.transpose(c.reshape(dec, N_KV_HEADS, HEAD_DIM, c.shape[2]), (0, 3, 1, 2))
    new_k_s, new_v_s = key_major(keys_s), key_major(vals_s)

    for j in range(depth - n_a):
        layer = n_a + j
        gn = row(norm_mix[layer])
        qgain = row(q_norm[j])
        wq = w_q[j].astype(BF16)
        wo = w_o[j].astype(BF16)
        xp = _attn_prompt(xp, gn, wq, tile_gain(qgain, D_MODEL), ones_head, cos_p, sin_p, kd, vt,
                          attn_sinks[j].astype(F32), wo)
        q_pad = _q_sample(xs, gn, wq, tile_gain(qgain, D_MODEL), ones_head, cos_s, sin_s, rep, place)
        o_pad = _attn_sample(q_pad.reshape(dec, N_HEADS, nkv), keys_s, vals_s,
                             attn_sinks[j].astype(F32).reshape(N_HEADS, 1))
        xs = _proj_residual(xs, o_pad.reshape(dec * N_HEADS, nkv), place.T, rep.T, wo)
        xp, xs = mlp_both(xp, xs, layer)

    return (xp, xs.reshape(dec, 1, D_MODEL),
            jnp.stack(sp_re), jnp.stack(sp_im), new_k_p, new_v_p,
            jnp.stack(ss_re), jnp.stack(ss_im), new_k_s, new_v_s)
```

```python
import functools
import math

import jax
import jax.numpy as jnp
from jax import lax
from jax.experimental import pallas as pl
from jax.experimental.pallas import tpu as pltpu

F32 = jnp.float32
BF16 = jnp.bfloat16

D_MODEL = 1024
N_GROUPS = 64
GROUP_SIZE = 16
STATE_DIM = 64
HEAD_DIM = 64
N_HEADS = 16
N_KV_HEADS = 4
WINDOW = 128
ROPE_THETA = 10000.0
D_FF = 4 * D_MODEL
EPS = 1e-6

SUBLANES = 8
LANES = 128
MXU_TILE = 256
VMEM_LIMIT_BYTES = 56 * 1024 * 1024

N_OCTETS = D_MODEL // LANES
OCTET_STATE = (LANES // GROUP_SIZE) * STATE_DIM
STATE_LANES = 2 * OCTET_STATE * N_OCTETS

S5_TIME_CHUNK = 128
MLP_ROWS = 1024
FF_CHUNK = 1024
KV_ROWS = 1024
ATTN_BLOCKS = 4


def _const_spec(shape):
    zeros = (0,) * len(shape)
    return pl.BlockSpec(shape, lambda *_: zeros, pipeline_mode=pl.Buffered(1))


def _layer_spec(shape, layer):
    idx = (layer,) + (0,) * (len(shape) - 1)
    return pl.BlockSpec((None,) + tuple(shape[1:]), lambda *_: idx, pipeline_mode=pl.Buffered(1))


def _params(sem):
    return pltpu.CompilerParams(dimension_semantics=sem, vmem_limit_bytes=VMEM_LIMIT_BYTES)


def _rms(x, g):
    return x * lax.rsqrt(jnp.mean(x * x, axis=-1, keepdims=True) + EPS) * g


def _dot(a, b):
    return jnp.dot(a, b, preferred_element_type=F32)


def _seg_mean_sq(x, ones_ref):
    sq = x * x
    hi = sq.astype(BF16)
    lo = (sq - hi.astype(F32)).astype(BF16)
    ones = ones_ref[...]
    outs = []
    for t in range(x.shape[-1] // MXU_TILE):
        sl = slice(MXU_TILE * t, MXU_TILE * (t + 1))
        outs.append(_dot(hi[:, sl], ones) + _dot(lo[:, sl], ones))
    return jnp.concatenate(outs, axis=-1) * (1.0 / HEAD_DIM)


def _tile_lanes(t, n):
    return jnp.concatenate([t] * (n // t.shape[-1]), axis=-1)


def _rope(x, cos, sin_signed):
    n = x.shape[-1]
    lane = lax.broadcasted_iota(jnp.int32, x.shape, 1)
    first = (lane & (HEAD_DIM - 1)) < (HEAD_DIM // 2)
    partner = jnp.where(first, pltpu.roll(x, n - HEAD_DIM // 2, 1), pltpu.roll(x, HEAD_DIM // 2, 1))
    return x * cos + partner * sin_signed


def _mlp_rows(x, g_ref, w1_ref, w2_ref):
    h = _rms(x, g_ref[...]).astype(BF16)
    acc = x
    for c in range(D_FF // FF_CHUNK):
        sl = slice(FF_CHUNK * c, FF_CHUNK * (c + 1))
        a = jnp.square(jnp.maximum(_dot(h, w1_ref[:, sl]), 0.0)).astype(BF16)
        acc = acc + _dot(a, w2_ref[sl, :])
    return acc


def _mlp_kernel(x_ref, xs_ref, g_ref, w1_ref, w2_ref, o_ref, os_ref):
    last = pl.num_programs(0) - 1

    @pl.when(pl.program_id(0) < last)
    def _():
        o_ref[...] = _mlp_rows(x_ref[...], g_ref, w1_ref, w2_ref)

    @pl.when(pl.program_id(0) == last)
    def _():
        os_ref[...] = _mlp_rows(xs_ref[...], g_ref, w1_ref, w2_ref)


def _mlp(x, xs, g, w1, w2, layer):
    rows = x.shape[0]
    tm = MLP_ROWS
    tiles = rows // tm
    x_spec = pl.BlockSpec((tm, D_MODEL), lambda i: (jnp.minimum(i, tiles - 1), 0))
    xs_spec = pl.BlockSpec(xs.shape, lambda i: (0, 0))
    return pl.pallas_call(
        _mlp_kernel,
        grid=(tiles + 1,),
        in_specs=[x_spec, xs_spec,
                  _layer_spec(g.shape, layer),
                  _layer_spec(w1.shape, layer),
                  _layer_spec(w2.shape, layer)],
        out_specs=[x_spec, xs_spec],
        out_shape=[jax.ShapeDtypeStruct((rows, D_MODEL), F32), jax.ShapeDtypeStruct(xs.shape, F32)],
        compiler_params=_params(("arbitrary",)),
        name="mlp",
    )(x, xs, g, w1, w2)


def _mlp_small_kernel(x_ref, g_ref, w1_ref, w2_ref, o_ref):
    o_ref[...] = _mlp_rows(x_ref[...], g_ref, w1_ref, w2_ref)


def _mlp_small(x, g, w1, w2, layer):
    return pl.pallas_call(
        _mlp_small_kernel,
        grid=(1,),
        in_specs=[_const_spec(x.shape), _layer_spec(g.shape, layer), _layer_spec(w1.shape, layer),
                  _layer_spec(w2.shape, layer)],
        out_specs=pl.BlockSpec(x.shape, lambda i: (0, 0)),
        out_shape=jax.ShapeDtypeStruct(x.shape, F32),
        compiler_params=_params(("arbitrary",)),
        name="mlp_sample",
    )(x, g, w1, w2)


def _s5_gate(y, wglu_ref, bglu_ref):
    g = jax.nn.gelu(y).astype(BF16)
    z = _dot(g, wglu_ref[...]) + bglu_ref[...]
    return z[:, :D_MODEL] * jax.nn.sigmoid(z[:, D_MODEL:])


def _slabs(ref, rows=slice(None)):
    return jnp.concatenate([ref[c, rows, :] for c in range(ref.shape[0])], axis=-1)


def _s5_prompt_kernel(xa_ref, xb_ref, gn_ref, b2_ref, lam_ref, c2_ref, m0_ref, d_ref, wglu_ref, bglu_ref,
                      o_ref, fin_ref, st_ref, carry_ref, nat_ref, ue_ref, uo_ref, up_ref, tb2_ref, nat2_ref, y_ref):
    batch, tc, _ = xa_ref.shape
    assert 2 * batch == SUBLANES
    n_slabs = D_MODEL // LANES
    pairs = tc // 2
    rows_k = pairs * batch
    step = pl.program_id(0)
    n_chunks = pl.num_programs(0) - 1
    live = step < n_chunks

    @pl.when(step == 0)
    def _():
        carry_ref[...] = jnp.zeros_like(carry_ref)
        y_ref[...] = jnp.zeros_like(y_ref)
        up_ref[...] = jnp.zeros_like(up_ref)

    g_prev = jax.nn.gelu(y_ref[...]).astype(BF16)
    glu_cols = 2 * D_MODEL // (N_OCTETS // 2)
    z = []

    def glu_part(c):
        sl = slice(glu_cols * c, glu_cols * (c + 1))
        z.append(_dot(g_prev, wglu_ref[:, sl]) + bglu_ref[:, sl])

    def finish_prev():
        zs = jnp.concatenate(z, axis=-1)
        upd = zs[:, :D_MODEL] * jax.nn.sigmoid(zs[:, D_MODEL:])
        for c in range(n_slabs):
            tb2_ref[c] = upd[:, LANES * c:LANES * (c + 1)]
        for b in range(batch):
            for c in range(n_slabs):
                nat2_ref[c, pl.ds(0, pairs, stride=2), :] = tb2_ref[c, pl.ds(b, pairs, stride=batch), :]
                nat2_ref[c, pl.ds(1, pairs, stride=2), :] = tb2_ref[c, pl.ds(rows_k + b, pairs, stride=batch), :]
            o_ref[b] = xb_ref[b] + _slabs(nat2_ref)

    for c in range(n_slabs):
        up_ref[c, 0:batch, :] = up_ref[c, rows_k:rows_k + batch, :]
    for b in range(batch):
        u_nat = _rms(xa_ref[b], gn_ref[...])
        for c in range(n_slabs):
            nat_ref[c] = u_nat[:, LANES * c:LANES * (c + 1)]
        for c in range(n_slabs):
            even = nat_ref[c, pl.ds(0, pairs, stride=2), :]
            odd = nat_ref[c, pl.ds(1, pairs, stride=2), :]
            ue_ref[c, pl.ds(b, pairs, stride=batch), :] = even
            uo_ref[c, pl.ds(b, pairs, stride=batch), :] = odd
            up_ref[c, pl.ds(batch + b, pairs, stride=batch), :] = odd
    ue, uo = _slabs(ue_ref), _slabs(uo_ref)
    ue_b, uo_b, up_b = ue.astype(BF16), uo.astype(BF16), _slabs(up_ref, slice(0, rows_k)).astype(BF16)

    lo = lax.broadcasted_iota(jnp.int32, (SUBLANES, LANES), 0) < batch
    swap = lambda v: pltpu.roll(v, batch, 0)
    n_vregs = rows_k // SUBLANES
    half = OCTET_STATE // LANES
    y_even, y_odd = [], []
    for q in range(N_OCTETS):
        ch = slice(LANES * q, LANES * (q + 1))
        lhs = jnp.concatenate([up_b[:, ch], ue_b[:, ch]], axis=1)
        st_ref[:, 2 * OCTET_STATE * q:2 * OCTET_STATE * (q + 1)] = _dot(lhs, b2_ref[q])

    def project_out(octets):
        for q in octets:
            ch = slice(LANES * q, LANES * (q + 1))
            y2 = _dot(st_ref[:, 2 * OCTET_STATE * q:2 * OCTET_STATE * (q + 1)].astype(BF16), c2_ref[q])
            y_even.append(y2[:, :LANES])
            y_odd.append(y2[:, LANES:] + _dot(uo_b[:, ch], m0_ref[q]))

    for q0 in range(0, N_OCTETS, 2):
        octets = (q0, q0 + 1)
        glu_part(q0 // 2)
        if q0 > 0:
            project_out((q0 - 2, q0 - 1))
        if q0 == N_OCTETS - 2:
            finish_prev()
        chains = []
        for q in octets:
            for j in range(half // 2):
                cols = [2 * OCTET_STATE * q + LANES * jj for jj in (j, j + half // 2)]
                lanes = [OCTET_STATE * q + LANES * jj for jj in (j, j + half // 2)]
                chains.append((cols, lanes))
        lam_r, lam_i, sr, si, old = [], [], [], [], []
        for (ca, cb), (la, lb) in chains:
            bc = lambda r, c: jnp.broadcast_to(lam_ref[r:r + 1, c:c + LANES], (SUBLANES, LANES))
            lam_r.append(jnp.where(lo, bc(2, la), bc(2, lb)))
            lam_i.append(jnp.where(lo, bc(3, la), bc(3, lb)))
            prev = [carry_ref[:, c:c + LANES] for c in (ca, cb, ca + OCTET_STATE, cb + OCTET_STATE)]
            old.append(prev)
            sr.append(jnp.where(lo, swap(prev[0]), prev[1]))
            si.append(jnp.where(lo, swap(prev[2]), prev[3]))
        last = [None] * len(chains)
        for m in range(n_vregs):
            rows = slice(SUBLANES * m, SUBLANES * (m + 1))
            for j, ((ca, cb), _) in enumerate(chains):
                ar, br = st_ref[rows, ca:ca + LANES], st_ref[rows, cb:cb + LANES]
                ai, bi = (st_ref[rows, ca + OCTET_STATE:ca + OCTET_STATE + LANES],
                          st_ref[rows, cb + OCTET_STATE:cb + OCTET_STATE + LANES])
                d0r, d1r = jnp.where(lo, ar, swap(br)), jnp.where(lo, swap(ar), br)
                d0i, d1i = jnp.where(lo, ai, swap(bi)), jnp.where(lo, swap(ai), bi)
                s0r = lam_r[j] * sr[j] - lam_i[j] * si[j] + d0r
                s0i = lam_r[j] * si[j] + lam_i[j] * sr[j] + d0i
                s1r = lam_r[j] * s0r - lam_i[j] * s0i + d1r
                s1i = lam_r[j] * s0i + lam_i[j] * s0r + d1i
                sr[j], si[j] = s1r, s1i
                out = (jnp.where(lo, s0r, swap(s1r)), jnp.where(lo, swap(s0r), s1r),
                       jnp.where(lo, s0i, swap(s1i)), jnp.where(lo, swap(s0i), s1i))
                st_ref[rows, ca:ca + LANES] = out[0]
                st_ref[rows, cb:cb + LANES] = out[1]
                st_ref[rows, ca + OCTET_STATE:ca + OCTET_STATE + LANES] = out[2]
                st_ref[rows, cb + OCTET_STATE:cb + OCTET_STATE + LANES] = out[3]
                last[j] = out
        for j, ((ca, cb), _) in enumerate(chains):
            for c, new, prev in zip((ca, cb, ca + OCTET_STATE, cb + OCTET_STATE), last[j], old[j]):
                carry_ref[:, c:c + LANES] = jnp.where(live, new, prev)
    project_out((N_OCTETS - 2, N_OCTETS - 1))
    y_ref[0:rows_k, :] = jnp.concatenate(y_even, axis=-1) + d_ref[...] * ue
    y_ref[rows_k:2 * rows_k, :] = jnp.concatenate(y_odd, axis=-1) + d_ref[...] * uo

    @pl.when(step == n_chunks - 1)
    def _():
        tail = uo_b[rows_k - 2 * SUBLANES:, :]
        zeros = jnp.zeros((2 * SUBLANES, LANES), BF16)
        for q in range(N_OCTETS):
            bu = _dot(jnp.concatenate([zeros, tail[:, LANES * q:LANES * (q + 1)]], axis=1), b2_ref[q])[SUBLANES:]
            base = 2 * OCTET_STATE * q
            lr = lam_ref[0:1, OCTET_STATE * q:OCTET_STATE * (q + 1)]
            li = lam_ref[1:2, OCTET_STATE * q:OCTET_STATE * (q + 1)]
            s_r = carry_ref[:, base:base + OCTET_STATE]
            s_i = carry_ref[:, base + OCTET_STATE:base + 2 * OCTET_STATE]
            fin_ref[:, base:base + OCTET_STATE] = lr * s_r - li * s_i + bu[:, :OCTET_STATE]
            fin_ref[:, base + OCTET_STATE:base + 2 * OCTET_STATE] = lr * s_i + li * s_r + bu[:, OCTET_STATE:]


def _s5_prompt(x, layer, gn, b2, lam, c2, m0, d, wglu, bglu):
    batch, seq, _ = x.shape
    tc = S5_TIME_CHUNK
    n_chunks = seq // tc
    rows_k = tc // 2 * batch
    n_slabs = D_MODEL // LANES
    blk = (batch, tc, D_MODEL)
    this_chunk = pl.BlockSpec(blk, lambda s: (0, jnp.minimum(s, n_chunks - 1), 0))
    prev_chunk = pl.BlockSpec(blk, lambda s: (0, jnp.maximum(s - 1, 0), 0))
    slab = lambda rows: pltpu.VMEM((n_slabs, rows, LANES), F32)
    return pl.pallas_call(
        _s5_prompt_kernel,
        grid=(n_chunks + 1,),
        in_specs=[this_chunk, prev_chunk]
        + [_layer_spec(w.shape, layer) for w in (gn, b2, lam, c2, m0, d, wglu, bglu)],
        out_specs=[prev_chunk, pl.BlockSpec((SUBLANES, STATE_LANES), lambda s: (0, 0))],
        out_shape=[jax.ShapeDtypeStruct(x.shape, F32),
                   jax.ShapeDtypeStruct((SUBLANES, STATE_LANES), F32)],
        scratch_shapes=[pltpu.VMEM((rows_k, STATE_LANES), F32),
                        pltpu.VMEM((SUBLANES, STATE_LANES), F32),
                        slab(tc), slab(rows_k), slab(rows_k), slab(rows_k + SUBLANES),
                        slab(2 * rows_k), slab(tc),
                        pltpu.VMEM((2 * rows_k, D_MODEL), F32)],
        compiler_params=_params(("arbitrary",)),
        name="s5_prompt",
    )(x, x, gn, b2, lam, c2, m0, d, wglu, bglu)


def _s5_sample_kernel(x_ref, gn_ref, b2_ref, lam_ref, c2_ref, d_ref, wglu_ref, bglu_ref, s0_ref,
                      o_ref, snew_ref):
    x = x_ref[...]
    u = _rms(x, gn_ref[...])
    ys = []
    for q in range(N_OCTETS):
        base = 2 * OCTET_STATE * q
        bu = jnp.dot(u[:, LANES * q:LANES * (q + 1)], b2_ref[q, LANES:, :],
                     preferred_element_type=F32, precision=lax.Precision.HIGHEST)
        lr = lam_ref[0:1, OCTET_STATE * q:OCTET_STATE * (q + 1)]
        li = lam_ref[1:2, OCTET_STATE * q:OCTET_STATE * (q + 1)]
        s0r = s0_ref[:, base:base + OCTET_STATE]
        s0i = s0_ref[:, base + OCTET_STATE:base + 2 * OCTET_STATE]
        sr = lr * s0r - li * s0i + bu[:, :OCTET_STATE]
        si = lr * s0i + li * s0r + bu[:, OCTET_STATE:]
        snew_ref[:, base:base + OCTET_STATE] = sr
        snew_ref[:, base + OCTET_STATE:base + 2 * OCTET_STATE] = si
        sb = jnp.concatenate([sr, si], axis=-1).astype(BF16)
        ys.append(_dot(sb, c2_ref[q, :, :LANES]))
    y = jnp.concatenate(ys, axis=-1)
    o_ref[...] = x + _s5_gate(y + d_ref[...] * u, wglu_ref, bglu_ref)


def _s5_sample(x, layer, gn, b2_f32, lam, c2, d, wglu, bglu, s0):
    rows = x.shape[0]
    return pl.pallas_call(
        _s5_sample_kernel,
        grid=(1,),
        in_specs=[_const_spec(x.shape)] + [_layer_spec(w.shape, layer) for w in (gn, b2_f32, lam, c2, d, wglu, bglu)]
        + [_const_spec(s0.shape)],
        out_specs=[pl.BlockSpec((rows, D_MODEL), lambda i: (0, 0)),
                   pl.BlockSpec((rows, STATE_LANES), lambda i: (0, 0))],
        out_shape=[jax.ShapeDtypeStruct((rows, D_MODEL), F32),
                   jax.ShapeDtypeStruct((rows, STATE_LANES), F32)],
        compiler_params=_params(("arbitrary",)),
        name="s5_sample",
    )(x, gn, b2_f32, lam, c2, d, wglu, bglu, s0)


def _block_diag(w):
    n, octets, groups, r, c = w.shape
    col = jnp.arange(groups * c)
    tile = (jnp.arange(c)[:, None] == col % c).astype(F32)
    own = (jnp.arange(groups * r)[:, None] // r == col // c).astype(F32)
    tiled = jnp.einsum("nqrc,cl->nqrl", w.reshape(n, octets, groups * r, c), tile, precision=lax.Precision.HIGHEST)
    return tiled * own


def _s5_weights(a_re, a_im, log_dt, b_re, b_im, c_re, c_im):
    n = a_re.shape[0]
    dt = jnp.exp(log_dt.astype(F32))
    lam = lax.complex(a_re.astype(F32), a_im.astype(F32))
    lam_bar = jnp.exp(lam * dt)
    b = lax.complex(b_re.astype(F32), b_im.astype(F32))
    b_bar = ((lam_bar - 1.0) / lam)[..., None] * b
    c = lax.complex(c_re.astype(F32), c_im.astype(F32))
    gpo = N_GROUPS // N_OCTETS
    octets = lambda w: w.reshape((n, N_OCTETS, gpo) + w.shape[2:])

    def into_state(w):
        wt = jnp.swapaxes(octets(w), 3, 4)
        return jnp.concatenate([_block_diag(jnp.real(wt)), _block_diag(jnp.imag(wt))], axis=-1)

    def out_of_state(w):
        wt = jnp.swapaxes(octets(w), 3, 4)
        return jnp.concatenate([_block_diag(jnp.real(wt)), -_block_diag(jnp.imag(wt))], axis=2)

    b2 = jnp.concatenate([into_state(lam_bar[..., None] * b_bar), into_state(b_bar)], axis=2)
    c2 = jnp.concatenate([out_of_state(c), out_of_state(c * lam_bar[:, :, None, :])], axis=-1)
    cb = jnp.real(jnp.einsum("ngcp,ngpd->ngdc", c, b_bar))
    m0 = _block_diag(octets(cb))
    lam2 = lam_bar * lam_bar
    lam_rows = jnp.stack([jnp.real(lam_bar).reshape(n, -1), jnp.imag(lam_bar).reshape(n, -1),
                          jnp.real(lam2).reshape(n, -1), jnp.imag(lam2).reshape(n, -1)], axis=1)
    return b2, c2.astype(BF16), m0.astype(BF16), lam_rows


def _state_to_lanes(s_re, s_im):
    n = s_re.shape[0]
    re = s_re.astype(F32).reshape(n, N_OCTETS, OCTET_STATE)
    im = s_im.astype(F32).reshape(n, N_OCTETS, OCTET_STATE)
    return jnp.concatenate([re, im], axis=-1).reshape(n, STATE_LANES)


def _lanes_to_state(s):
    n = s.shape[0]
    s4 = s.reshape(n, N_OCTETS, 2, OCTET_STATE)
    return (s4[:, :, 0].reshape(n, N_GROUPS, STATE_DIM), s4[:, :, 1].reshape(n, N_GROUPS, STATE_DIM))


def _rope_partner(n):
    lane = jnp.arange(n)
    return jnp.where(lane % HEAD_DIM < HEAD_DIM // 2, lane + HEAD_DIM // 2, lane - HEAD_DIM // 2)


def _kv_kernel(x_ref, g_ref, wkv_ref, kg_ref, kgp_ref, ones_ref, cos_ref, sin_ref, *rest):
    if len(rest) == 3:
        wvt_ref, k_ref, vt_ref = rest
        v_ref = None
    else:
        (k_ref, v_ref), wvt_ref, vt_ref = rest, None, None
    h = _rms(x_ref[...], g_ref[...]).astype(BF16)
    kv = _dot(h, wkv_ref[...])
    n = k_ref.shape[-1]
    k, k_partner = kv[:, :n], kv[:, n:2 * n]
    scale = lax.rsqrt(_seg_mean_sq(k, ones_ref) + EPS)
    k_ref[...] = scale * (k * (kg_ref[...] * _tile_lanes(cos_ref[...], n))
                          + k_partner * (kgp_ref[...] * _tile_lanes(sin_ref[...], n)))
    if v_ref is not None:
        v_ref[...] = kv[:, 2 * n:]
    else:
        vt_ref[...] = lax.dot_general(wvt_ref[...], h, (((1,), (1,)), ((), ())), preferred_element_type=F32)


def _kv(x, g, wk, wv, kgain, ones, cos, sin, table_block, seq=None):
    rows = x.shape[0]
    n, nv = wk.shape[1], wv.shape[1]
    partner = _rope_partner(n)
    wkv = jnp.concatenate([wk, wk[:, partner]] + ([wv] if seq is None else []), axis=1).astype(BF16)
    tm = min(KV_ROWS, rows)
    in_specs = [pl.BlockSpec((tm, D_MODEL), lambda i: (i, 0)),
                _const_spec((1, D_MODEL)),
                _const_spec(wkv.shape),
                _const_spec((1, n)),
                _const_spec((1, n)),
                _const_spec(ones.shape),
                pl.BlockSpec((tm, LANES), lambda i: (table_block(i), 0)),
                pl.BlockSpec((tm, LANES), lambda i: (table_block(i), 0))]
    out_specs = [pl.BlockSpec((tm, n), lambda i: (i, 0))]
    out_shape = [jax.ShapeDtypeStruct((rows, n), F32)]
    args = [x, g, wkv, kgain, kgain[:, partner], ones, cos, sin]
    if seq is None:
        out_specs.append(pl.BlockSpec((tm, nv), lambda i: (i, 0)))
        out_shape.append(jax.ShapeDtypeStruct((rows, nv), F32))
    else:
        per_seq = seq // tm
        in_specs.append(_const_spec((nv, D_MODEL)))
        args.append(wv.T.astype(BF16))
        out_specs.append(pl.BlockSpec((None, nv, tm), lambda i: (i // per_seq, 0, i % per_seq)))
        out_shape.append(jax.ShapeDtypeStruct((rows // seq, nv, seq), F32))
    return pl.pallas_call(
        _kv_kernel,
        grid=(rows // tm,),
        in_specs=in_specs,
        out_specs=out_specs,
        out_shape=out_shape,
        compiler_params=_params(("parallel",)),
        name="kv_proj",
    )(*args)


def _attn_prompt_kernel(sinks_ref, x_ref, g_ref, wq_ref, qg_ref, ones_ref, cos_ref, sin_ref,
                        kp_ref, kc_ref, vp_ref, vc_ref, wo_ref, gm_ref, w1_ref, w2_ref, o_ref, mid_ref,
                        *, tiles_per_seq, n_tiles):
    step = pl.program_id(0)
    tile = lax.rem(jnp.minimum(step, n_tiles - 1), tiles_per_seq)

    @pl.when(step == 0)
    def _():
        mid_ref[...] = jnp.zeros_like(mid_ref)

    o_ref[...] = _mlp_rows(mid_ref[...], gm_ref, w1_ref, w2_ref)

    x = x_ref[...]
    n_blocks = x.shape[0] // WINDOW
    heads_per_kv = N_HEADS // N_KV_HEADS
    cols = heads_per_kv * WINDOW
    h = _rms(x, g_ref[...]).astype(BF16)
    q = _dot(h, wq_ref[...])
    q = q * lax.rsqrt(_seg_mean_sq(q, ones_ref) + EPS) * qg_ref[...]
    q = _rope(q, _tile_lanes(cos_ref[...], D_MODEL), _tile_lanes(sin_ref[...], D_MODEL))
    qb = (q * (HEAD_DIM ** -0.5)).astype(BF16)

    kall = jnp.concatenate([kp_ref[...], kc_ref[...]], axis=0).astype(BF16)
    vall = jnp.concatenate([vp_ref[...], vc_ref[...]], axis=1).astype(BF16)

    kj = lax.broadcasted_iota(jnp.int32, (WINDOW, cols), 0)
    col = lax.broadcasted_iota(jnp.int32, (WINDOW, cols), 1)
    from_prev = kj > (col & (WINDOW - 1))
    col_head = lax.broadcasted_iota(jnp.int32, (1, cols), 1) // WINDOW
    low_half = lax.broadcasted_iota(jnp.int32, (WINDOW, LANES), 1) < HEAD_DIM
    neg_inf = jnp.full((WINDOW, cols), -jnp.inf, F32)

    pairs = [(blk, g) for blk in range(n_blocks) for g in range(N_KV_HEADS)]
    scores, sinks = [], []
    for blk, g in pairs:
        r0 = WINDOW * blk
        kg = kall[r0:r0 + 2 * WINDOW, LANES * g:LANES * (g + 1)]
        qs = []
        sink = jnp.zeros((1, cols), F32)
        for hh in range(heads_per_kv):
            head = heads_per_kv * g + hh
            qp = qb[r0:r0 + WINDOW, LANES * (head // 2):LANES * (head // 2 + 1)]
            keep = low_half if head % 2 == 0 else jnp.logical_not(low_half)
            qs.append(jnp.where(keep, qp, jnp.zeros_like(qp)))
            sink = jnp.where(col_head == hh, sinks_ref[head], sink)
        sinks.append(sink)
        scores.append(lax.dot_general(kg, jnp.concatenate(qs, axis=0), (((1,), (1,)), ((), ())),
                                      preferred_element_type=F32))
    probs, denoms = [], []
    for (blk, g), s, sink in zip(pairs, scores, sinks):
        has_prev = tile * n_blocks + blk > 0
        t = jnp.where(from_prev, jnp.where(has_prev, s[:WINDOW], neg_inf), s[WINDOW:])
        m = jnp.maximum(jnp.max(t, axis=0, keepdims=True), sink)
        p = jnp.exp(t - m)
        denoms.append(jnp.sum(p, axis=0, keepdims=True) + jnp.exp(sink - m))
        zero = jnp.zeros_like(p)
        probs.append(jnp.concatenate([jnp.where(from_prev, p, zero), jnp.where(from_prev, zero, p)],
                                     axis=0).astype(BF16))
    head_outs = [[] for _ in range(n_blocks)]
    for (blk, g), pcat, denom in zip(pairs, probs, denoms):
        r0 = WINDOW * blk
        vg = vall[HEAD_DIM * g:HEAD_DIM * (g + 1), r0:r0 + 2 * WINDOW]
        o = _dot(vg, pcat) / denom
        head_outs[blk] += [o[:, WINDOW * hh:WINDOW * (hh + 1)] for hh in range(heads_per_kv)]
    block_outs = [jnp.concatenate(outs, axis=0).T for outs in head_outs]
    attn = jnp.concatenate(block_outs, axis=0).astype(BF16)
    mid_ref[...] = x + _dot(attn, wo_ref[...])


def _attn_prompt(x, g, wq, qgain, ones, cos, sin, kd, vt, sinks, wo, g_mlp, w1, w2, layer):
    bsz, seq, _ = x.shape
    nk, nv = kd.shape[-1], vt.shape[1]
    nb = ATTN_BLOCKS
    tq = nb * WINDOW
    per_seq = seq // tq
    n_tiles = bsz * per_seq
    cur = lambda s: jnp.minimum(s, n_tiles - 1)
    prev = lambda s: jnp.maximum(s - 1, 0)
    x_spec = pl.BlockSpec((None, tq, D_MODEL), lambda s, *_: (cur(s) // per_seq, cur(s) % per_seq, 0))
    o_spec = pl.BlockSpec((None, tq, D_MODEL), lambda s, *_: (prev(s) // per_seq, prev(s) % per_seq, 0))
    k_cur = pl.BlockSpec((None, tq, nk), lambda s, *_: (cur(s) // per_seq, cur(s) % per_seq, 0))
    k_prev = pl.BlockSpec((None, WINDOW, nk),
                          lambda s, *_: (cur(s) // per_seq, jnp.maximum(nb * (cur(s) % per_seq) - 1, 0), 0))
    v_cur = pl.BlockSpec((None, nv, tq), lambda s, *_: (cur(s) // per_seq, 0, cur(s) % per_seq))
    v_prev = pl.BlockSpec((None, nv, WINDOW),
                          lambda s, *_: (cur(s) // per_seq, 0, jnp.maximum(nb * (cur(s) % per_seq) - 1, 0)))
    tab = pl.BlockSpec((tq, LANES), lambda s, *_: (cur(s) % per_seq, 0))
    grid_spec = pltpu.PrefetchScalarGridSpec(
        num_scalar_prefetch=1,
        grid=(n_tiles + 1,),
        in_specs=[x_spec,
                  _const_spec((1, D_MODEL)),
                  _const_spec(wq.shape),
                  _const_spec((1, D_MODEL)),
                  _const_spec(ones.shape),
                  tab, tab, k_prev, k_cur, v_prev, v_cur,
                  _const_spec(wo.shape),
                  _layer_spec(g_mlp.shape, layer), _layer_spec(w1.shape, layer), _layer_spec(w2.shape, layer)],
        out_specs=o_spec,
        scratch_shapes=[pltpu.VMEM((tq, D_MODEL), F32)])
    return pl.pallas_call(
        functools.partial(_attn_prompt_kernel, tiles_per_seq=per_seq, n_tiles=n_tiles),
        grid_spec=grid_spec,
        out_shape=jax.ShapeDtypeStruct(x.shape, F32),
        compiler_params=_params(("arbitrary",)),
        name="attn_mlp_prompt",
    )(sinks, x, g, wq, qgain, ones, cos, sin, kd, kd, vt, vt, wo, g_mlp, w1, w2)


def _own_head(shape):
    row_head = lax.broadcasted_iota(jnp.int32, shape, 0) & (N_HEADS - 1)
    lane_head = lax.broadcasted_iota(jnp.int32, shape, 1) // HEAD_DIM
    return row_head == lane_head


def _q_sample_kernel(x_ref, g_ref, wq_ref, qg_ref, ones_ref, cos_ref, sin_ref, rep_ref, place_ref, q_ref):
    h = _rms(x_ref[...], g_ref[...]).astype(BF16)
    q = _dot(h, wq_ref[...])
    q = q * lax.rsqrt(_seg_mean_sq(q, ones_ref) + EPS) * qg_ref[...]
    q = _rope(q, _tile_lanes(cos_ref[...], D_MODEL), _tile_lanes(sin_ref[...], D_MODEL))
    qb = (q * (HEAD_DIM ** -0.5)).astype(BF16)
    q_rep = _dot(rep_ref[...], qb)
    q_own = jnp.where(_own_head(q_rep.shape), q_rep, 0.0).astype(BF16)
    q_ref[...] = _dot(q_own, place_ref[...])


def _q_sample(x, g, wq, qgain, ones, cos, sin, rep, place):
    rows = x.shape[0]
    return pl.pallas_call(
        _q_sample_kernel,
        out_shape=jax.ShapeDtypeStruct((rows * N_HEADS, place.shape[1]), F32),
        compiler_params=pltpu.CompilerParams(vmem_limit_bytes=VMEM_LIMIT_BYTES),
        name="q_sample",
    )(x, g, wq, qgain, ones, cos, sin, rep, place)


def _cache_append_kernel(k_ref, v_ref, kn_ref, vn_ref, ko_ref, vo_ref):
    nb, lanes_kv, keys = k_ref.shape
    first = pl.program_id(0) * nb
    key = lax.broadcasted_iota(jnp.int32, (lanes_kv, keys), 1)
    seq = lax.broadcasted_iota(jnp.int32, (kn_ref.shape[1], keys), 0)

    def split3(x):
        hi = x.astype(BF16)
        r = x - hi.astype(F32)
        mid = r.astype(BF16)
        return hi, mid, (r - mid.astype(F32)).astype(BF16)

    parts = [(split3(kn_ref[...]), k_ref, ko_ref), (split3(vn_ref[...]), v_ref, vo_ref)]
    for b in range(nb):
        pick = (seq == first + b).astype(BF16)
        for (hi, mid, lo), old_ref, out_ref in parts:
            new_col = _dot(hi, pick) + _dot(mid, pick) + _dot(lo, pick)
            out_ref[b] = jnp.where(key == keys - 1, new_col, pltpu.roll(old_ref[b], keys - 1, 1))


def _cache_append(kt, vt, kn_t, vn_t):
    bsz, lanes_kv, keys = kt.shape
    nb = SUBLANES
    blk = pl.BlockSpec((nb, lanes_kv, keys), lambda i: (i, 0, 0))
    return pl.pallas_call(
        _cache_append_kernel,
        grid=(bsz // nb,),
        in_specs=[blk, blk, _const_spec(kn_t.shape), _const_spec(vn_t.shape)],
        out_specs=[blk, blk],
        out_shape=[jax.ShapeDtypeStruct(kt.shape, F32), jax.ShapeDtypeStruct(vt.shape, F32)],
        compiler_params=_params(("parallel",)),
        name="cache_append",
    )(kt, vt, kn_t, vn_t)


def _attn_sample_kernel(q_ref, k_ref, v_ref, sink_ref, o_ref):
    sink = sink_ref[...]
    for b in range(q_ref.shape[0]):
        s = _dot(q_ref[b].astype(BF16), k_ref[b].astype(BF16))
        m = jnp.maximum(jnp.max(s, axis=-1, keepdims=True), sink)
        p = jnp.exp(s - m)
        denom = jnp.sum(p, axis=-1, keepdims=True) + jnp.exp(sink - m)
        o = lax.dot_general(p.astype(BF16), v_ref[b].astype(BF16), (((1,), (1,)), ((), ())),
                            preferred_element_type=F32)
        o_ref[b] = o / denom


def _attn_sample(q3, kt, vt, sink_col):
    bsz, nh, nkv = q3.shape
    keys = kt.shape[2]
    bb = SUBLANES
    return pl.pallas_call(
        _attn_sample_kernel,
        grid=(bsz // bb,),
        in_specs=[pl.BlockSpec((bb, nh, nkv), lambda i: (i, 0, 0)),
                  pl.BlockSpec((bb, nkv, keys), lambda i: (i, 0, 0)),
                  pl.BlockSpec((bb, nkv, keys), lambda i: (i, 0, 0)),
                  _const_spec(sink_col.shape)],
        out_specs=pl.BlockSpec((bb, nh, nkv), lambda i: (i, 0, 0)),
        out_shape=jax.ShapeDtypeStruct((bsz, nh, nkv), F32),
        compiler_params=_params(("parallel",)),
        name="attn_sample",
    )(q3, kt, vt, sink_col)


def _proj_residual_kernel(x_ref, a_ref, place_t_ref, rep_t_ref, w_ref, o_ref):
    z = _dot(a_ref[...].astype(BF16), place_t_ref[...])
    z_own = jnp.where(_own_head(z.shape), z, 0.0).astype(BF16)
    attn = _dot(rep_t_ref[...], z_own).astype(BF16)
    o_ref[...] = x_ref[...] + _dot(attn, w_ref[...])


def _proj_residual(x, a, place_t, rep_t, w):
    return pl.pallas_call(
        _proj_residual_kernel,
        out_shape=jax.ShapeDtypeStruct(x.shape, F32),
        compiler_params=pltpu.CompilerParams(vmem_limit_bytes=VMEM_LIMIT_BYTES),
        name="o_proj_sample",
    )(x, a, place_t, rep_t, w)


def _rope_tables(pos):
    half = HEAD_DIM // 2
    inv = ROPE_THETA ** (-jnp.arange(half, dtype=F32) / half)
    ang = pos.astype(F32)[:, None] * inv[None, :]
    cos, sin = jnp.cos(ang), jnp.sin(ang)
    reps = LANES // HEAD_DIM
    return (jnp.tile(jnp.concatenate([cos, cos], axis=-1), (1, reps)),
            jnp.tile(jnp.concatenate([-sin, sin], axis=-1), (1, reps)))


def _block_ones(width, block):
    idx = jnp.arange(width) // block
    return (idx[:, None] == idx[None, :]).astype(BF16)


def _dup_heads(w):
    w3 = w.reshape(w.shape[0], N_KV_HEADS, 1, HEAD_DIM)
    return jnp.broadcast_to(w3, (w.shape[0], N_KV_HEADS, LANES // HEAD_DIM, HEAD_DIM)).reshape(w.shape[0], -1)


def _head_placement():
    lane = jnp.arange(N_HEADS * HEAD_DIM)
    dst = (lane // HEAD_DIM) // (N_HEADS // N_KV_HEADS) * HEAD_DIM + lane % HEAD_DIM
    return (dst[:, None] == jnp.arange(N_KV_HEADS * HEAD_DIM)[None, :]).astype(BF16)


def _row_replication(n):
    return (jnp.arange(n * N_HEADS)[:, None] // N_HEADS == jnp.arange(n)[None, :]).astype(BF16)


def kernel(x_prompt, x_sample, state_ssm_re, state_ssm_im, cache_k, cache_v, norm_mix, norm_mlp, ssm_a_re, ssm_a_im, ssm_log_dt, ssm_b_re, ssm_b_im, ssm_c_re, ssm_c_im, ssm_d, w_glu, b_glu, norm_kv, w_k, w_v, k_norm, w_q, q_norm, attn_sinks, w_o, w_mlp_in, w_mlp_out):
    bsz, seq, _ = x_prompt.shape
    dec = x_sample.shape[0]
    n_a = ssm_a_re.shape[0]
    depth = norm_mix.shape[0]
    past = 8192
    nkv = N_KV_HEADS * HEAD_DIM
    row = lambda v: v.astype(F32).reshape(1, -1)

    xp = x_prompt.astype(F32)
    xs = x_sample.reshape(dec, D_MODEL)
    rows3 = lambda v: v.astype(F32).reshape(v.shape[0], 1, -1)

    w1 = w_mlp_in.astype(BF16)
    w2 = w_mlp_out.astype(BF16)
    g_mlp = rows3(norm_mlp)
    def mlp_both(x, xs, layer):
        y, ys = _mlp(x.reshape(bsz * seq, D_MODEL), xs, g_mlp, w1, w2, layer)
        return y.reshape(bsz, seq, D_MODEL), ys


    b2, c2, m0, lam = _s5_weights(ssm_a_re, ssm_a_im, ssm_log_dt, ssm_b_re, ssm_b_im, ssm_c_re, ssm_c_im)
    b2_16 = b2.astype(BF16)
    wglu = w_glu.astype(BF16)
    g_mix, d_skip, bg = rows3(norm_mix), rows3(ssm_d), rows3(b_glu)
    sp_re, sp_im, ss_re, ss_im = [], [], [], []
    for i in range(n_a):
        xp, fin = _s5_prompt(xp, i, g_mix, b2_16, lam, c2, m0, d_skip, wglu, bg)
        re, im = _lanes_to_state(fin[bsz:])
        sp_re.append(re)
        sp_im.append(im)
        s0 = _state_to_lanes(state_ssm_re[i], state_ssm_im[i])
        xs, snew = _s5_sample(xs, i, g_mix, b2, lam, c2, d_skip, wglu, bg, s0)
        re, im = _lanes_to_state(snew)
        ss_re.append(re)
        ss_im.append(im)
        xp, xs = mlp_both(xp, xs, i)

    cos_p, sin_p = _rope_tables(jnp.arange(seq, dtype=jnp.int32))
    cos_s, sin_s = _rope_tables(jnp.full((dec,), past, dtype=jnp.int32))
    ones_head = _block_ones(MXU_TILE, HEAD_DIM)
    place = _head_placement()
    rep = _row_replication(dec)
    kgain = row(k_norm)
    tile_gain = lambda gvec, n: jnp.tile(gvec, (1, n // HEAD_DIM))

    kd, vt = _kv(xp.reshape(bsz * seq, D_MODEL), row(norm_kv), _dup_heads(w_k), w_v,
                     tile_gain(kgain, 2 * nkv), ones_head, cos_p, sin_p, lambda i: i % (seq // KV_ROWS), seq=seq)
    ks_new, vs_new = _kv(xs, row(norm_kv), w_k, w_v, tile_gain(kgain, nkv), ones_head, cos_s, sin_s,
                         lambda i: i)
    kd = kd.reshape(bsz, seq, 2 * nkv)
    new_k_p = kd[:, -WINDOW:].reshape(bsz, WINDOW, N_KV_HEADS, LANES // HEAD_DIM, HEAD_DIM)[:, :, :, 0]
    new_v_p = jnp.transpose(vt[:, :, -WINDOW:].reshape(bsz, N_KV_HEADS, HEAD_DIM, WINDOW), (0, 3, 1, 2))
    key_minor = lambda c: jnp.transpose(c.astype(F32), (0, 2, 3, 1)).reshape(dec, nkv, c.shape[1])
    keys_s, vals_s = _cache_append(key_minor(cache_k), key_minor(cache_v), ks_new.T, vs_new.T)
    key_major = lambda c: jnp.transpose(c.reshape(dec, N_KV_HEADS, HEAD_DIM, c.shape[2]), (0, 3, 1, 2))
    new_k_s, new_v_s = key_major(keys_s), key_major(vals_s)

    for j in range(depth - n_a):
        layer = n_a + j
        gn = row(norm_mix[layer])
        qgain = row(q_norm[j])
        wq = w_q[j].astype(BF16)
        wo = w_o[j].astype(BF16)
        xp = _attn_prompt(xp, gn, wq, tile_gain(qgain, D_MODEL), ones_head, cos_p, sin_p, kd, vt,
                          attn_sinks[j].astype(F32), wo, g_mlp, w1, w2, layer)
        q_pad = _q_sample(xs, gn, wq, tile_gain(qgain, D_MODEL), ones_head, cos_s, sin_s, rep, place)
        o_pad = _attn_sample(q_pad.reshape(dec, N_HEADS, nkv), keys_s, vals_s,
                             attn_sinks[j].astype(F32).reshape(N_HEADS, 1))
        xs = _proj_residual(xs, o_pad.reshape(dec * N_HEADS, nkv), place.T, rep.T, wo)
        xs = _mlp_small(xs, g_mlp, w1, w2, layer)

    return (xp, xs.reshape(dec, 1, D_MODEL),
            jnp.stack(sp_re), jnp.stack(sp_im), new_k_p, new_v_p,
            jnp.stack(ss_re), jnp.stack(ss_im), new_k_s, new_v_s)
```

```python
import functools
import math

import jax
import jax.numpy as jnp
from jax import lax
from jax.experimental import pallas as pl
from jax.experimental.pallas import tpu as pltpu

F32 = jnp.float32
BF16 = jnp.bfloat16

D_MODEL = 1024
N_GROUPS = 64
GROUP_SIZE = 16
STATE_DIM = 64
HEAD_DIM = 64
N_HEADS = 16
N_KV_HEADS = 4
WINDOW = 128
ROPE_THETA = 10000.0
D_FF = 4 * D_MODEL
EPS = 1e-6

SUBLANES = 8
LANES = 128
MXU_TILE = 256
VMEM_LIMIT_BYTES = 56 * 1024 * 1024

N_OCTETS = D_MODEL // LANES
OCTET_STATE = (LANES // GROUP_SIZE) * STATE_DIM
STATE_LANES = 2 * OCTET_STATE * N_OCTETS

S5_TIME_CHUNK = 128
MLP_ROWS = 1024
FF_CHUNK = 1024
KV_ROWS = 1024
ATTN_BLOCKS = 4


def _const_spec(shape):
    zeros = (0,) * len(shape)
    return pl.BlockSpec(shape, lambda *_: zeros, pipeline_mode=pl.Buffered(1))


def _layer_spec(shape, layer):
    idx = (layer,) + (0,) * (len(shape) - 1)
    return pl.BlockSpec((None,) + tuple(shape[1:]), lambda *_: idx, pipeline_mode=pl.Buffered(1))


def _params(sem):
    return pltpu.CompilerParams(dimension_semantics=sem, vmem_limit_bytes=VMEM_LIMIT_BYTES)


def _rms(x, g):
    return x * lax.rsqrt(jnp.mean(x * x, axis=-1, keepdims=True) + EPS) * g


def _dot(a, b):
    return jnp.dot(a, b, preferred_element_type=F32)


def _seg_mean_sq(x, ones_ref):
    sq = x * x
    hi = sq.astype(BF16)
    lo = (sq - hi.astype(F32)).astype(BF16)
    ones = ones_ref[...]
    outs = []
    for t in range(x.shape[-1] // MXU_TILE):
        sl = slice(MXU_TILE * t, MXU_TILE * (t + 1))
        outs.append(_dot(hi[:, sl], ones) + _dot(lo[:, sl], ones))
    return jnp.concatenate(outs, axis=-1) * (1.0 / HEAD_DIM)


def _tile_lanes(t, n):
    return jnp.concatenate([t] * (n // t.shape[-1]), axis=-1)


def _rope(x, cos, sin_signed):
    n = x.shape[-1]
    lane = lax.broadcasted_iota(jnp.int32, x.shape, 1)
    first = (lane & (HEAD_DIM - 1)) < (HEAD_DIM // 2)
    partner = jnp.where(first, pltpu.roll(x, n - HEAD_DIM // 2, 1), pltpu.roll(x, HEAD_DIM // 2, 1))
    return x * cos + partner * sin_signed


def _mlp_rows(x, g_ref, w1_ref, w2_ref):
    h = _rms(x, g_ref[...]).astype(BF16)
    acc = x
    for c in range(D_FF // FF_CHUNK):
        sl = slice(FF_CHUNK * c, FF_CHUNK * (c + 1))
        a = jnp.square(jnp.maximum(_dot(h, w1_ref[:, sl]), 0.0)).astype(BF16)
        acc = acc + _dot(a, w2_ref[sl, :])
    return acc


def _mlp_kernel(x_ref, xs_ref, g_ref, w1_ref, w2_ref, o_ref, os_ref):
    last = pl.num_programs(0) - 1

    @pl.when(pl.program_id(0) < last)
    def _():
        o_ref[...] = _mlp_rows(x_ref[...], g_ref, w1_ref, w2_ref)

    @pl.when(pl.program_id(0) == last)
    def _():
        os_ref[...] = _mlp_rows(xs_ref[...], g_ref, w1_ref, w2_ref)


def _mlp(x, xs, g, w1, w2, layer):
    rows = x.shape[0]
    tm = MLP_ROWS
    tiles = rows // tm
    x_spec = pl.BlockSpec((tm, D_MODEL), lambda i: (jnp.minimum(i, tiles - 1), 0))
    xs_spec = pl.BlockSpec(xs.shape, lambda i: (0, 0))
    return pl.pallas_call(
        _mlp_kernel,
        grid=(tiles + 1,),
        in_specs=[x_spec, xs_spec,
                  _layer_spec(g.shape, layer),
                  _layer_spec(w1.shape, layer),
                  _layer_spec(w2.shape, layer)],
        out_specs=[x_spec, xs_spec],
        out_shape=[jax.ShapeDtypeStruct((rows, D_MODEL), F32), jax.ShapeDtypeStruct(xs.shape, F32)],
        compiler_params=_params(("arbitrary",)),
        name="mlp",
    )(x, xs, g, w1, w2)


def _mlp_small_kernel(x_ref, g_ref, w1_ref, w2_ref, o_ref):
    o_ref[...] = _mlp_rows(x_ref[...], g_ref, w1_ref, w2_ref)


def _mlp_small(x, g, w1, w2, layer):
    return pl.pallas_call(
        _mlp_small_kernel,
        grid=(1,),
        in_specs=[_const_spec(x.shape), _layer_spec(g.shape, layer), _layer_spec(w1.shape, layer),
                  _layer_spec(w2.shape, layer)],
        out_specs=pl.BlockSpec(x.shape, lambda i: (0, 0)),
        out_shape=jax.ShapeDtypeStruct(x.shape, F32),
        compiler_params=_params(("arbitrary",)),
        name="mlp_sample",
    )(x, g, w1, w2)


def _s5_gate(y, wglu_ref, bglu_ref):
    g = jax.nn.gelu(y).astype(BF16)
    z = _dot(g, wglu_ref[...]) + bglu_ref[...]
    return z[:, :D_MODEL] * jax.nn.sigmoid(z[:, D_MODEL:])


def _slabs(ref, rows=slice(None)):
    return jnp.concatenate([ref[c, rows, :] for c in range(ref.shape[0])], axis=-1)


def _s5_prompt_kernel(xa_ref, xb_ref, gn_ref, b2_ref, lam_ref, c2_ref, m0_ref, d_ref, wglu_ref, bglu_ref,
                      o_ref, fin_ref, st_ref, carry_ref, nat_ref, ue_ref, uo_ref, up_ref, tb2_ref, nat2_ref, y_ref):
    batch, tc, _ = xa_ref.shape
    assert 2 * batch == SUBLANES
    n_slabs = D_MODEL // LANES
    pairs = tc // 2
    rows_k = pairs * batch
    step = pl.program_id(0)
    n_chunks = pl.num_programs(0) - 1
    live = step < n_chunks

    @pl.when(step == 0)
    def _():
        carry_ref[...] = jnp.zeros_like(carry_ref)
        y_ref[...] = jnp.zeros_like(y_ref)
        up_ref[...] = jnp.zeros_like(up_ref)

    g_prev = jax.nn.gelu(y_ref[...]).astype(BF16)
    glu_cols = 2 * D_MODEL // (N_OCTETS // 2)
    z = []

    def glu_part(c):
        sl = slice(glu_cols * c, glu_cols * (c + 1))
        z.append(_dot(g_prev, wglu_ref[:, sl]) + bglu_ref[:, sl])

    def finish_prev():
        zs = jnp.concatenate(z, axis=-1)
        upd = zs[:, :D_MODEL] * jax.nn.sigmoid(zs[:, D_MODEL:])
        for c in range(n_slabs):
            tb2_ref[c] = upd[:, LANES * c:LANES * (c + 1)]
        for b in range(batch):
            for c in range(n_slabs):
                nat2_ref[c, pl.ds(0, pairs, stride=2), :] = tb2_ref[c, pl.ds(b, pairs, stride=batch), :]
                nat2_ref[c, pl.ds(1, pairs, stride=2), :] = tb2_ref[c, pl.ds(rows_k + b, pairs, stride=batch), :]
            o_ref[b] = xb_ref[b] + _slabs(nat2_ref)

    for c in range(n_slabs):
        up_ref[c, 0:batch, :] = up_ref[c, rows_k:rows_k + batch, :]
    for b in range(batch):
        u_nat = _rms(xa_ref[b], gn_ref[...])
        for c in range(n_slabs):
            nat_ref[c] = u_nat[:, LANES * c:LANES * (c + 1)]
        for c in range(n_slabs):
            even = nat_ref[c, pl.ds(0, pairs, stride=2), :]
            odd = nat_ref[c, pl.ds(1, pairs, stride=2), :]
            ue_ref[c, pl.ds(b, pairs, stride=batch), :] = even
            uo_ref[c, pl.ds(b, pairs, stride=batch), :] = odd
            up_ref[c, pl.ds(batch + b, pairs, stride=batch), :] = odd
    ue, uo = _slabs(ue_ref), _slabs(uo_ref)
    ue_b, uo_b, up_b = ue.astype(BF16), uo.astype(BF16), _slabs(up_ref, slice(0, rows_k)).astype(BF16)

    lo = lax.broadcasted_iota(jnp.int32, (SUBLANES, LANES), 0) < batch
    swap = lambda v: pltpu.roll(v, batch, 0)
    n_vregs = rows_k // SUBLANES
    half = OCTET_STATE // LANES
    y_even, y_odd = [], []
    for q in range(N_OCTETS):
        ch = slice(LANES * q, LANES * (q + 1))
        lhs = jnp.concatenate([up_b[:, ch], ue_b[:, ch]], axis=1)
        st_ref[:, 2 * OCTET_STATE * q:2 * OCTET_STATE * (q + 1)] = _dot(lhs, b2_ref[q])

    def project_out(octets):
        for q in octets:
            ch = slice(LANES * q, LANES * (q + 1))
            y2 = _dot(st_ref[:, 2 * OCTET_STATE * q:2 * OCTET_STATE * (q + 1)].astype(BF16), c2_ref[q])
            y_even.append(y2[:, :LANES])
            y_odd.append(y2[:, LANES:] + _dot(uo_b[:, ch], m0_ref[q]))

    for q0 in range(0, N_OCTETS, 2):
        octets = (q0, q0 + 1)
        glu_part(q0 // 2)
        if q0 > 0:
            project_out((q0 - 2, q0 - 1))
        if q0 == N_OCTETS - 2:
            finish_prev()
        chains = []
        for q in octets:
            for j in range(half // 2):
                cols = [2 * OCTET_STATE * q + LANES * jj for jj in (j, j + half // 2)]
                lanes = [OCTET_STATE * q + LANES * jj for jj in (j, j + half // 2)]
                chains.append((cols, lanes))
        lam_r, lam_i, sr, si, old = [], [], [], [], []
        for (ca, cb), (la, lb) in chains:
            bc = lambda r, c: jnp.broadcast_to(lam_ref[r:r + 1, c:c + LANES], (SUBLANES, LANES))
            lam_r.append(jnp.where(lo, bc(2, la), bc(2, lb)))
            lam_i.append(jnp.where(lo, bc(3, la), bc(3, lb)))
            prev = [carry_ref[:, c:c + LANES] for c in (ca, cb, ca + OCTET_STATE, cb + OCTET_STATE)]
            old.append(prev)
            sr.append(jnp.where(lo, swap(prev[0]), prev[1]))
            si.append(jnp.where(lo, swap(prev[2]), prev[3]))
        last = [None] * len(chains)
        for m in range(n_vregs):
            rows = slice(SUBLANES * m, SUBLANES * (m + 1))
            for j, ((ca, cb), _) in enumerate(chains):
                ar, br = st_ref[rows, ca:ca + LANES], st_ref[rows, cb:cb + LANES]
                ai, bi = (st_ref[rows, ca + OCTET_STATE:ca + OCTET_STATE + LANES],
                          st_ref[rows, cb + OCTET_STATE:cb + OCTET_STATE + LANES])
                d0r, d1r = jnp.where(lo, ar, swap(br)), jnp.where(lo, swap(ar), br)
                d0i, d1i = jnp.where(lo, ai, swap(bi)), jnp.where(lo, swap(ai), bi)
                s0r = lam_r[j] * sr[j] - lam_i[j] * si[j] + d0r
                s0i = lam_r[j] * si[j] + lam_i[j] * sr[j] + d0i
                s1r = lam_r[j] * s0r - lam_i[j] * s0i + d1r
                s1i = lam_r[j] * s0i + lam_i[j] * s0r + d1i
                sr[j], si[j] = s1r, s1i
                out = (jnp.where(lo, s0r, swap(s1r)), jnp.where(lo, swap(s0r), s1r),
                       jnp.where(lo, s0i, swap(s1i)), jnp.where(lo, swap(s0i), s1i))
                st_ref[rows, ca:ca + LANES] = out[0]
                st_ref[rows, cb:cb + LANES] = out[1]
                st_ref[rows, ca + OCTET_STATE:ca + OCTET_STATE + LANES] = out[2]
                st_ref[rows, cb + OCTET_STATE:cb + OCTET_STATE + LANES] = out[3]
                last[j] = out
        for j, ((ca, cb), _) in enumerate(chains):
            for c, new, prev in zip((ca, cb, ca + OCTET_STATE, cb + OCTET_STATE), last[j], old[j]):
                carry_ref[:, c:c + LANES] = jnp.where(live, new, prev)
    project_out((N_OCTETS - 2, N_OCTETS - 1))
    y_ref[0:rows_k, :] = jnp.concatenate(y_even, axis=-1) + d_ref[...] * ue
    y_ref[rows_k:2 * rows_k, :] = jnp.concatenate(y_odd, axis=-1) + d_ref[...] * uo

    @pl.when(step == n_chunks - 1)
    def _():
        tail = uo_b[rows_k - 2 * SUBLANES:, :]
        zeros = jnp.zeros((2 * SUBLANES, LANES), BF16)
        for q in range(N_OCTETS):
            bu = _dot(jnp.concatenate([zeros, tail[:, LANES * q:LANES * (q + 1)]], axis=1), b2_ref[q])[SUBLANES:]
            base = 2 * OCTET_STATE * q
            lr = lam_ref[0:1, OCTET_STATE * q:OCTET_STATE * (q + 1)]
            li = lam_ref[1:2, OCTET_STATE * q:OCTET_STATE * (q + 1)]
            s_r = carry_ref[:, base:base + OCTET_STATE]
            s_i = carry_ref[:, base + OCTET_STATE:base + 2 * OCTET_STATE]
            fin_ref[:, base:base + OCTET_STATE] = lr * s_r - li * s_i + bu[:, :OCTET_STATE]
            fin_ref[:, base + OCTET_STATE:base + 2 * OCTET_STATE] = lr * s_i + li * s_r + bu[:, OCTET_STATE:]


def _s5_prompt(x, layer, gn, b2, lam, c2, m0, d, wglu, bglu):
    batch, seq, _ = x.shape
    tc = S5_TIME_CHUNK
    n_chunks = seq // tc
    rows_k = tc // 2 * batch
    n_slabs = D_MODEL // LANES
    blk = (batch, tc, D_MODEL)
    this_chunk = pl.BlockSpec(blk, lambda s: (0, jnp.minimum(s, n_chunks - 1), 0))
    prev_chunk = pl.BlockSpec(blk, lambda s: (0, jnp.maximum(s - 1, 0), 0))
    slab = lambda rows: pltpu.VMEM((n_slabs, rows, LANES), F32)
    return pl.pallas_call(
        _s5_prompt_kernel,
        grid=(n_chunks + 1,),
        in_specs=[this_chunk, prev_chunk]
        + [_layer_spec(w.shape, layer) for w in (gn, b2, lam, c2, m0, d, wglu, bglu)],
        out_specs=[prev_chunk, pl.BlockSpec((SUBLANES, STATE_LANES), lambda s: (0, 0))],
        out_shape=[jax.ShapeDtypeStruct(x.shape, F32),
                   jax.ShapeDtypeStruct((SUBLANES, STATE_LANES), F32)],
        scratch_shapes=[pltpu.VMEM((rows_k, STATE_LANES), F32),
                        pltpu.VMEM((SUBLANES, STATE_LANES), F32),
                        slab(tc), slab(rows_k), slab(rows_k), slab(rows_k + SUBLANES),
                        slab(2 * rows_k), slab(tc),
                        pltpu.VMEM((2 * rows_k, D_MODEL), F32)],
        compiler_params=_params(("arbitrary",)),
        name="s5_prompt",
    )(x, x, gn, b2, lam, c2, m0, d, wglu, bglu)


def _s5_sample_kernel(x_ref, gn_ref, b2_ref, lam_ref, c2_ref, d_ref, wglu_ref, bglu_ref, s0_ref,
                      o_ref, snew_ref):
    x = x_ref[...]
    u = _rms(x, gn_ref[...])
    ys = []
    for q in range(N_OCTETS):
        base = 2 * OCTET_STATE * q
        bu = jnp.dot(u[:, LANES * q:LANES * (q + 1)], b2_ref[q, LANES:, :],
                     preferred_element_type=F32, precision=lax.Precision.HIGHEST)
        lr = lam_ref[0:1, OCTET_STATE * q:OCTET_STATE * (q + 1)]
        li = lam_ref[1:2, OCTET_STATE * q:OCTET_STATE * (q + 1)]
        s0r = s0_ref[:, base:base + OCTET_STATE]
        s0i = s0_ref[:, base + OCTET_STATE:base + 2 * OCTET_STATE]
        sr = lr * s0r - li * s0i + bu[:, :OCTET_STATE]
        si = lr * s0i + li * s0r + bu[:, OCTET_STATE:]
        snew_ref[:, base:base + OCTET_STATE] = sr
        snew_ref[:, base + OCTET_STATE:base + 2 * OCTET_STATE] = si
        sb = jnp.concatenate([sr, si], axis=-1).astype(BF16)
        ys.append(_dot(sb, c2_ref[q, :, :LANES]))
    y = jnp.concatenate(ys, axis=-1)
    o_ref[...] = x + _s5_gate(y + d_ref[...] * u, wglu_ref, bglu_ref)


def _s5_sample(x, layer, gn, b2_f32, lam, c2, d, wglu, bglu, s0):
    rows = x.shape[0]
    return pl.pallas_call(
        _s5_sample_kernel,
        grid=(1,),
        in_specs=[_const_spec(x.shape)] + [_layer_spec(w.shape, layer) for w in (gn, b2_f32, lam, c2, d, wglu, bglu)]
        + [_const_spec(s0.shape)],
        out_specs=[pl.BlockSpec((rows, D_MODEL), lambda i: (0, 0)),
                   pl.BlockSpec((rows, STATE_LANES), lambda i: (0, 0))],
        out_shape=[jax.ShapeDtypeStruct((rows, D_MODEL), F32),
                   jax.ShapeDtypeStruct((rows, STATE_LANES), F32)],
        compiler_params=_params(("arbitrary",)),
        name="s5_sample",
    )(x, gn, b2_f32, lam, c2, d, wglu, bglu, s0)


def _block_diag(w):
    n, octets, groups, r, c = w.shape
    col = jnp.arange(groups * c)
    tile = (jnp.arange(c)[:, None] == col % c).astype(F32)
    own = (jnp.arange(groups * r)[:, None] // r == col // c).astype(F32)
    tiled = jnp.einsum("nqrc,cl->nqrl", w.reshape(n, octets, groups * r, c), tile, precision=lax.Precision.HIGHEST)
    return tiled * own


def _s5_weights(a_re, a_im, log_dt, b_re, b_im, c_re, c_im):
    n = a_re.shape[0]
    dt = jnp.exp(log_dt.astype(F32))
    lam = lax.complex(a_re.astype(F32), a_im.astype(F32))
    lam_bar = jnp.exp(lam * dt)
    b = lax.complex(b_re.astype(F32), b_im.astype(F32))
    b_bar = ((lam_bar - 1.0) / lam)[..., None] * b
    c = lax.complex(c_re.astype(F32), c_im.astype(F32))
    gpo = N_GROUPS // N_OCTETS
    octets = lambda w: w.reshape((n, N_OCTETS, gpo) + w.shape[2:])

    def into_state(w):
        wt = jnp.swapaxes(octets(w), 3, 4)
        return jnp.concatenate([_block_diag(jnp.real(wt)), _block_diag(jnp.imag(wt))], axis=-1)

    def out_of_state(w):
        wt = jnp.swapaxes(octets(w), 3, 4)
        return jnp.concatenate([_block_diag(jnp.real(wt)), -_block_diag(jnp.imag(wt))], axis=2)

    b2 = jnp.concatenate([into_state(lam_bar[..., None] * b_bar), into_state(b_bar)], axis=2)
    c2 = jnp.concatenate([out_of_state(c), out_of_state(c * lam_bar[:, :, None, :])], axis=-1)
    cb = jnp.real(jnp.einsum("ngcp,ngpd->ngdc", c, b_bar))
    m0 = _block_diag(octets(cb))
    lam2 = lam_bar * lam_bar
    lam_rows = jnp.stack([jnp.real(lam_bar).reshape(n, -1), jnp.imag(lam_bar).reshape(n, -1),
                          jnp.real(lam2).reshape(n, -1), jnp.imag(lam2).reshape(n, -1)], axis=1)
    return b2, c2.astype(BF16), m0.astype(BF16), lam_rows


def _state_to_lanes(s_re, s_im):
    n = s_re.shape[0]
    re = s_re.astype(F32).reshape(n, N_OCTETS, OCTET_STATE)
    im = s_im.astype(F32).reshape(n, N_OCTETS, OCTET_STATE)
    return jnp.concatenate([re, im], axis=-1).reshape(n, STATE_LANES)


def _lanes_to_state(s):
    n = s.shape[0]
    s4 = s.reshape(n, N_OCTETS, 2, OCTET_STATE)
    return (s4[:, :, 0].reshape(n, N_GROUPS, STATE_DIM), s4[:, :, 1].reshape(n, N_GROUPS, STATE_DIM))


def _rope_partner(n):
    lane = jnp.arange(n)
    return jnp.where(lane % HEAD_DIM < HEAD_DIM // 2, lane + HEAD_DIM // 2, lane - HEAD_DIM // 2)


def _kv_kernel(x_ref, g_ref, wkv_ref, kg_ref, kgp_ref, ones_ref, cos_ref, sin_ref, *rest):
    if len(rest) == 3:
        wvt_ref, k_ref, vt_ref = rest
        v_ref = None
    else:
        (k_ref, v_ref), wvt_ref, vt_ref = rest, None, None
    h = _rms(x_ref[...], g_ref[...]).astype(BF16)
    kv = _dot(h, wkv_ref[...])
    n = k_ref.shape[-1]
    k, k_partner = kv[:, :n], kv[:, n:2 * n]
    scale = lax.rsqrt(_seg_mean_sq(k, ones_ref) + EPS)
    k_ref[...] = scale * (k * (kg_ref[...] * _tile_lanes(cos_ref[...], n))
                          + k_partner * (kgp_ref[...] * _tile_lanes(sin_ref[...], n)))
    if v_ref is not None:
        v_ref[...] = kv[:, 2 * n:]
    else:
        vt_ref[...] = lax.dot_general(wvt_ref[...], h, (((1,), (1,)), ((), ())), preferred_element_type=F32)


def _kv(x, g, wk, wv, kgain, ones, cos, sin, table_block, seq=None):
    rows = x.shape[0]
    n, nv = wk.shape[1], wv.shape[1]
    partner = _rope_partner(n)
    wkv = jnp.concatenate([wk, wk[:, partner]] + ([wv] if seq is None else []), axis=1).astype(BF16)
    tm = min(KV_ROWS, rows)
    in_specs = [pl.BlockSpec((tm, D_MODEL), lambda i: (i, 0)),
                _const_spec((1, D_MODEL)),
                _const_spec(wkv.shape),
                _const_spec((1, n)),
                _const_spec((1, n)),
                _const_spec(ones.shape),
                pl.BlockSpec((tm, LANES), lambda i: (table_block(i), 0)),
                pl.BlockSpec((tm, LANES), lambda i: (table_block(i), 0))]
    out_specs = [pl.BlockSpec((tm, n), lambda i: (i, 0))]
    out_shape = [jax.ShapeDtypeStruct((rows, n), F32)]
    args = [x, g, wkv, kgain, kgain[:, partner], ones, cos, sin]
    if seq is None:
        out_specs.append(pl.BlockSpec((tm, nv), lambda i: (i, 0)))
        out_shape.append(jax.ShapeDtypeStruct((rows, nv), F32))
    else:
        per_seq = seq // tm
        in_specs.append(_const_spec((nv, D_MODEL)))
        args.append(wv.T.astype(BF16))
        out_specs.append(pl.BlockSpec((None, nv, tm), lambda i: (i // per_seq, 0, i % per_seq)))
        out_shape.append(jax.ShapeDtypeStruct((rows // seq, nv, seq), F32))
    return pl.pallas_call(
        _kv_kernel,
        grid=(rows // tm,),
        in_specs=in_specs,
        out_specs=out_specs,
        out_shape=out_shape,
        compiler_params=_params(("parallel",)),
        name="kv_proj",
    )(*args)


def _attn_prompt_kernel(sinks_ref, x_ref, g_ref, wq_ref, qg_ref, ones_ref, cos_ref, sin_ref,
                        kp_ref, kc_ref, vp_ref, vc_ref, wo_ref, gm_ref, w1_ref, w2_ref, o_ref, mid_ref,
                        *, tiles_per_seq, n_tiles):
    step = pl.program_id(0)
    tile = lax.rem(jnp.minimum(step, n_tiles - 1), tiles_per_seq)

    @pl.when(step == 0)
    def _():
        mid_ref[...] = jnp.zeros_like(mid_ref)

    mid = mid_ref[...]
    h_mid = _rms(mid, gm_ref[...]).astype(BF16)
    acc = [mid]

    def mlp_part(c):
        sl = slice(FF_CHUNK * c, FF_CHUNK * (c + 1))
        a = jnp.square(jnp.maximum(_dot(h_mid, w1_ref[:, sl]), 0.0)).astype(BF16)
        acc[0] = acc[0] + _dot(a, w2_ref[sl, :])

    x = x_ref[...]
    n_blocks = x.shape[0] // WINDOW
    heads_per_kv = N_HEADS // N_KV_HEADS
    cols = heads_per_kv * WINDOW
    h = _rms(x, g_ref[...]).astype(BF16)
    q = _dot(h, wq_ref[...])
    q = q * lax.rsqrt(_seg_mean_sq(q, ones_ref) + EPS) * qg_ref[...]
    q = _rope(q, _tile_lanes(cos_ref[...], D_MODEL), _tile_lanes(sin_ref[...], D_MODEL))
    qb = (q * (HEAD_DIM ** -0.5)).astype(BF16)

    kall = jnp.concatenate([kp_ref[...], kc_ref[...]], axis=0).astype(BF16)
    vall = jnp.concatenate([vp_ref[...], vc_ref[...]], axis=1).astype(BF16)

    kj = lax.broadcasted_iota(jnp.int32, (WINDOW, cols), 0)
    col = lax.broadcasted_iota(jnp.int32, (WINDOW, cols), 1)
    from_prev = kj > (col & (WINDOW - 1))
    col_head = lax.broadcasted_iota(jnp.int32, (1, cols), 1) // WINDOW
    low_half = lax.broadcasted_iota(jnp.int32, (WINDOW, LANES), 1) < HEAD_DIM
    neg_inf = jnp.full((WINDOW, cols), -jnp.inf, F32)

    pairs = [(blk, g) for blk in range(n_blocks) for g in range(N_KV_HEADS)]
    scores, sinks = [], []
    for blk, g in pairs:
        r0 = WINDOW * blk
        kg = kall[r0:r0 + 2 * WINDOW, LANES * g:LANES * (g + 1)]
        qs = []
        sink = jnp.zeros((1, cols), F32)
        for hh in range(heads_per_kv):
            head = heads_per_kv * g + hh
            qp = qb[r0:r0 + WINDOW, LANES * (head // 2):LANES * (head // 2 + 1)]
            keep = low_half if head % 2 == 0 else jnp.logical_not(low_half)
            qs.append(jnp.where(keep, qp, jnp.zeros_like(qp)))
            sink = jnp.where(col_head == hh, sinks_ref[head], sink)
        sinks.append(sink)
        scores.append(lax.dot_general(kg, jnp.concatenate(qs, axis=0), (((1,), (1,)), ((), ())),
                                      preferred_element_type=F32))
    probs, denoms = [], []
    ff_chunks = D_FF // FF_CHUNK
    for idx, ((blk, g), s, sink) in enumerate(zip(pairs, scores, sinks)):
        if idx % (len(pairs) // ff_chunks) == 0:
            mlp_part(idx // (len(pairs) // ff_chunks))
        has_prev = tile * n_blocks + blk > 0
        t = jnp.where(from_prev, jnp.where(has_prev, s[:WINDOW], neg_inf), s[WINDOW:])
        m = jnp.maximum(jnp.max(t, axis=0, keepdims=True), sink)
        p = jnp.exp(t - m)
        denoms.append(jnp.sum(p, axis=0, keepdims=True) + jnp.exp(sink - m))
        zero = jnp.zeros_like(p)
        probs.append(jnp.concatenate([jnp.where(from_prev, p, zero), jnp.where(from_prev, zero, p)],
                                     axis=0).astype(BF16))
    head_outs = [[] for _ in range(n_blocks)]
    for (blk, g), pcat, denom in zip(pairs, probs, denoms):
        r0 = WINDOW * blk
        vg = vall[HEAD_DIM * g:HEAD_DIM * (g + 1), r0:r0 + 2 * WINDOW]
        o = _dot(vg, pcat) / denom
        head_outs[blk] += [o[:, WINDOW * hh:WINDOW * (hh + 1)] for hh in range(heads_per_kv)]
    block_outs = [jnp.concatenate(outs, axis=0).T for outs in head_outs]
    attn = jnp.concatenate(block_outs, axis=0).astype(BF16)
    o_ref[...] = acc[0]
    mid_ref[...] = x + _dot(attn, wo_ref[...])


def _attn_prompt(x, g, wq, qgain, ones, cos, sin, kd, vt, sinks, wo, g_mlp, w1, w2, layer):
    bsz, seq, _ = x.shape
    nk, nv = kd.shape[-1], vt.shape[1]
    nb = ATTN_BLOCKS
    tq = nb * WINDOW
    per_seq = seq // tq
    n_tiles = bsz * per_seq
    cur = lambda s: jnp.minimum(s, n_tiles - 1)
    prev = lambda s: jnp.maximum(s - 1, 0)
    x_spec = pl.BlockSpec((None, tq, D_MODEL), lambda s, *_: (cur(s) // per_seq, cur(s) % per_seq, 0))
    o_spec = pl.BlockSpec((None, tq, D_MODEL), lambda s, *_: (prev(s) // per_seq, prev(s) % per_seq, 0))
    k_cur = pl.BlockSpec((None, tq, nk), lambda s, *_: (cur(s) // per_seq, cur(s) % per_seq, 0))
    k_prev = pl.BlockSpec((None, WINDOW, nk),
                          lambda s, *_: (cur(s) // per_seq, jnp.maximum(nb * (cur(s) % per_seq) - 1, 0), 0))
    v_cur = pl.BlockSpec((None, nv, tq), lambda s, *_: (cur(s) // per_seq, 0, cur(s) % per_seq))
    v_prev = pl.BlockSpec((None, nv, WINDOW),
                          lambda s, *_: (cur(s) // per_seq, 0, jnp.maximum(nb * (cur(s) % per_seq) - 1, 0)))
    tab = pl.BlockSpec((tq, LANES), lambda s, *_: (cur(s) % per_seq, 0))
    grid_spec = pltpu.PrefetchScalarGridSpec(
        num_scalar_prefetch=1,
        grid=(n_tiles + 1,),
        in_specs=[x_spec,
                  _const_spec((1, D_MODEL)),
                  _const_spec(wq.shape),
                  _const_spec((1, D_MODEL)),
                  _const_spec(ones.shape),
                  tab, tab, k_prev, k_cur, v_prev, v_cur,
                  _const_spec(wo.shape),
                  _layer_spec(g_mlp.shape, layer), _layer_spec(w1.shape, layer), _layer_spec(w2.shape, layer)],
        out_specs=o_spec,
        scratch_shapes=[pltpu.VMEM((tq, D_MODEL), F32)])
    return pl.pallas_call(
        functools.partial(_attn_prompt_kernel, tiles_per_seq=per_seq, n_tiles=n_tiles),
        grid_spec=grid_spec,
        out_shape=jax.ShapeDtypeStruct(x.shape, F32),
        compiler_params=_params(("arbitrary",)),
        name="attn_mlp_prompt",
    )(sinks, x, g, wq, qgain, ones, cos, sin, kd, kd, vt, vt, wo, g_mlp, w1, w2)


def _own_head(shape):
    row_head = lax.broadcasted_iota(jnp.int32, shape, 0) & (N_HEADS - 1)
    lane_head = lax.broadcasted_iota(jnp.int32, shape, 1) // HEAD_DIM
    return row_head == lane_head


def _q_sample_kernel(x_ref, g_ref, wq_ref, qg_ref, ones_ref, cos_ref, sin_ref, rep_ref, place_ref, q_ref):
    h = _rms(x_ref[...], g_ref[...]).astype(BF16)
    q = _dot(h, wq_ref[...])
    q = q * lax.rsqrt(_seg_mean_sq(q, ones_ref) + EPS) * qg_ref[...]
    q = _rope(q, _tile_lanes(cos_ref[...], D_MODEL), _tile_lanes(sin_ref[...], D_MODEL))
    qb = (q * (HEAD_DIM ** -0.5)).astype(BF16)
    q_rep = _dot(rep_ref[...], qb)
    q_own = jnp.where(_own_head(q_rep.shape), q_rep, 0.0).astype(BF16)
    q_ref[...] = _dot(q_own, place_ref[...])


def _q_sample(x, g, wq, qgain, ones, cos, sin, rep, place):
    rows = x.shape[0]
    return pl.pallas_call(
        _q_sample_kernel,
        out_shape=jax.ShapeDtypeStruct((rows * N_HEADS, place.shape[1]), F32),
        compiler_params=pltpu.CompilerParams(vmem_limit_bytes=VMEM_LIMIT_BYTES),
        name="q_sample",
    )(x, g, wq, qgain, ones, cos, sin, rep, place)


def _cache_append_kernel(k_ref, v_ref, kn_ref, vn_ref, ko_ref, vo_ref):
    nb, lanes_kv, keys = k_ref.shape
    first = pl.program_id(0) * nb
    key = lax.broadcasted_iota(jnp.int32, (lanes_kv, keys), 1)
    seq = lax.broadcasted_iota(jnp.int32, (kn_ref.shape[1], keys), 0)

    def split3(x):
        hi = x.astype(BF16)
        r = x - hi.astype(F32)
        mid = r.astype(BF16)
        return hi, mid, (r - mid.astype(F32)).astype(BF16)

    parts = [(split3(kn_ref[...]), k_ref, ko_ref), (split3(vn_ref[...]), v_ref, vo_ref)]
    for b in range(nb):
        pick = (seq == first + b).astype(BF16)
        for (hi, mid, lo), old_ref, out_ref in parts:
            new_col = _dot(hi, pick) + _dot(mid, pick) + _dot(lo, pick)
            out_ref[b] = jnp.where(key == keys - 1, new_col, pltpu.roll(old_ref[b], keys - 1, 1))


def _cache_append(kt, vt, kn_t, vn_t):
    bsz, lanes_kv, keys = kt.shape
    nb = SUBLANES
    blk = pl.BlockSpec((nb, lanes_kv, keys), lambda i: (i, 0, 0))
    return pl.pallas_call(
        _cache_append_kernel,
        grid=(bsz // nb,),
        in_specs=[blk, blk, _const_spec(kn_t.shape), _const_spec(vn_t.shape)],
        out_specs=[blk, blk],
        out_shape=[jax.ShapeDtypeStruct(kt.shape, F32), jax.ShapeDtypeStruct(vt.shape, F32)],
        compiler_params=_params(("parallel",)),
        name="cache_append",
    )(kt, vt, kn_t, vn_t)


def _attn_sample_kernel(q_ref, k_ref, v_ref, sink_ref, o_ref):
    sink = sink_ref[...]
    for b in range(q_ref.shape[0]):
        s = _dot(q_ref[b].astype(BF16), k_ref[b].astype(BF16))
        m = jnp.maximum(jnp.max(s, axis=-1, keepdims=True), sink)
        p = jnp.exp(s - m)
        denom = jnp.sum(p, axis=-1, keepdims=True) + jnp.exp(sink - m)
        o = lax.dot_general(p.astype(BF16), v_ref[b].astype(BF16), (((1,), (1,)), ((), ())),
                            preferred_element_type=F32)
        o_ref[b] = o / denom


def _attn_sample(q3, kt, vt, sink_col):
    bsz, nh, nkv = q3.shape
    keys = kt.shape[2]
    bb = SUBLANES
    return pl.pallas_call(
        _attn_sample_kernel,
        grid=(bsz // bb,),
        in_specs=[pl.BlockSpec((bb, nh, nkv), lambda i: (i, 0, 0)),
                  pl.BlockSpec((bb, nkv, keys), lambda i: (i, 0, 0)),
                  pl.BlockSpec((bb, nkv, keys), lambda i: (i, 0, 0)),
                  _const_spec(sink_col.shape)],
        out_specs=pl.BlockSpec((bb, nh, nkv), lambda i: (i, 0, 0)),
        out_shape=jax.ShapeDtypeStruct((bsz, nh, nkv), F32),
        compiler_params=_params(("parallel",)),
        name="attn_sample",
    )(q3, kt, vt, sink_col)


def _proj_residual_kernel(x_ref, a_ref, place_t_ref, rep_t_ref, w_ref, o_ref):
    z = _dot(a_ref[...].astype(BF16), place_t_ref[...])
    z_own = jnp.where(_own_head(z.shape), z, 0.0).astype(BF16)
    attn = _dot(rep_t_ref[...], z_own).astype(BF16)
    o_ref[...] = x_ref[...] + _dot(attn, w_ref[...])


def _proj_residual(x, a, place_t, rep_t, w):
    return pl.pallas_call(
        _proj_residual_kernel,
        out_shape=jax.ShapeDtypeStruct(x.shape, F32),
        compiler_params=pltpu.CompilerParams(vmem_limit_bytes=VMEM_LIMIT_BYTES),
        name="o_proj_sample",
    )(x, a, place_t, rep_t, w)


def _rope_tables(pos):
    half = HEAD_DIM // 2
    inv = ROPE_THETA ** (-jnp.arange(half, dtype=F32) / half)
    ang = pos.astype(F32)[:, None] * inv[None, :]
    cos, sin = jnp.cos(ang), jnp.sin(ang)
    reps = LANES // HEAD_DIM
    return (jnp.tile(jnp.concatenate([cos, cos], axis=-1), (1, reps)),
            jnp.tile(jnp.concatenate([-sin, sin], axis=-1), (1, reps)))


def _block_ones(width, block):
    idx = jnp.arange(width) // block
    return (idx[:, None] == idx[None, :]).astype(BF16)


def _dup_heads(w):
    w3 = w.reshape(w.shape[0], N_KV_HEADS, 1, HEAD_DIM)
    return jnp.broadcast_to(w3, (w.shape[0], N_KV_HEADS, LANES // HEAD_DIM, HEAD_DIM)).reshape(w.shape[0], -1)


def _head_placement():
    lane = jnp.arange(N_HEADS * HEAD_DIM)
    dst = (lane // HEAD_DIM) // (N_HEADS // N_KV_HEADS) * HEAD_DIM + lane % HEAD_DIM
    return (dst[:, None] == jnp.arange(N_KV_HEADS * HEAD_DIM)[None, :]).astype(BF16)


def _row_replication(n):
    return (jnp.arange(n * N_HEADS)[:, None] // N_HEADS == jnp.arange(n)[None, :]).astype(BF16)


def kernel(x_prompt, x_sample, state_ssm_re, state_ssm_im, cache_k, cache_v, norm_mix, norm_mlp, ssm_a_re, ssm_a_im, ssm_log_dt, ssm_b_re, ssm_b_im, ssm_c_re, ssm_c_im, ssm_d, w_glu, b_glu, norm_kv, w_k, w_v, k_norm, w_q, q_norm, attn_sinks, w_o, w_mlp_in, w_mlp_out):
    bsz, seq, _ = x_prompt.shape
    dec = x_sample.shape[0]
    n_a = ssm_a_re.shape[0]
    depth = norm_mix.shape[0]
    past = 8192
    nkv = N_KV_HEADS * HEAD_DIM
    row = lambda v: v.astype(F32).reshape(1, -1)

    xp = x_prompt.astype(F32)
    xs = x_sample.reshape(dec, D_MODEL)
    rows3 = lambda v: v.astype(F32).reshape(v.shape[0], 1, -1)

    w1 = w_mlp_in.astype(BF16)
    w2 = w_mlp_out.astype(BF16)
    g_mlp = rows3(norm_mlp)
    def mlp_both(x, xs, layer):
        y, ys = _mlp(x.reshape(bsz * seq, D_MODEL), xs, g_mlp, w1, w2, layer)
        return y.reshape(bsz, seq, D_MODEL), ys


    b2, c2, m0, lam = _s5_weights(ssm_a_re, ssm_a_im, ssm_log_dt, ssm_b_re, ssm_b_im, ssm_c_re, ssm_c_im)
    b2_16 = b2.astype(BF16)
    wglu = w_glu.astype(BF16)
    g_mix, d_skip, bg = rows3(norm_mix), rows3(ssm_d), rows3(b_glu)
    sp_re, sp_im, ss_re, ss_im = [], [], [], []
    for i in range(n_a):
        xp, fin = _s5_prompt(xp, i, g_mix, b2_16, lam, c2, m0, d_skip, wglu, bg)
        re, im = _lanes_to_state(fin[bsz:])
        sp_re.append(re)
        sp_im.append(im)
        s0 = _state_to_lanes(state_ssm_re[i], state_ssm_im[i])
        xs, snew = _s5_sample(xs, i, g_mix, b2, lam, c2, d_skip, wglu, bg, s0)
        re, im = _lanes_to_state(snew)
        ss_re.append(re)
        ss_im.append(im)
        xp, xs = mlp_both(xp, xs, i)

    cos_p, sin_p = _rope_tables(jnp.arange(seq, dtype=jnp.int32))
    cos_s, sin_s = _rope_tables(jnp.full((dec,), past, dtype=jnp.int32))
    ones_head = _block_ones(MXU_TILE, HEAD_DIM)
    place = _head_placement()
    rep = _row_replication(dec)
    kgain = row(k_norm)
    tile_gain = lambda gvec, n: jnp.tile(gvec, (1, n // HEAD_DIM))

    kd, vt = _kv(xp.reshape(bsz * seq, D_MODEL), row(norm_kv), _dup_heads(w_k), w_v,
                     tile_gain(kgain, 2 * nkv), ones_head, cos_p, sin_p, lambda i: i % (seq // KV_ROWS), seq=seq)
    ks_new, vs_new = _kv(xs, row(norm_kv), w_k, w_v, tile_gain(kgain, nkv), ones_head, cos_s, sin_s,
                         lambda i: i)
    kd = kd.reshape(bsz, seq, 2 * nkv)
    new_k_p = kd[:, -WINDOW:].reshape(bsz, WINDOW, N_KV_HEADS, LANES // HEAD_DIM, HEAD_DIM)[:, :, :, 0]
    new_v_p = jnp.transpose(vt[:, :, -WINDOW:].reshape(bsz, N_KV_HEADS, HEAD_DIM, WINDOW), (0, 3, 1, 2))
    key_minor = lambda c: jnp.transpose(c.astype(F32), (0, 2, 3, 1)).reshape(dec, nkv, c.shape[1])
    keys_s, vals_s = _cache_append(key_minor(cache_k), key_minor(cache_v), ks_new.T, vs_new.T)
    key_major = lambda c: jnp.transpose(c.reshape(dec, N_KV_HEADS, HEAD_DIM, c.shape[2]), (0, 3, 1, 2))
    new_k_s, new_v_s = key_major(keys_s), key_major(vals_s)

    for j in range(depth - n_a):
        layer = n_a + j
        gn = row(norm_mix[layer])
        qgain = row(q_norm[j])
        wq = w_q[j].astype(BF16)
        wo = w_o[j].astype(BF16)
        xp = _attn_prompt(xp, gn, wq, tile_gain(qgain, D_MODEL), ones_head, cos_p, sin_p, kd, vt,
                          attn_sinks[j].astype(F32), wo, g_mlp, w1, w2, layer)
        q_pad = _q_sample(xs, gn, wq, tile_gain(qgain, D_MODEL), ones_head, cos_s, sin_s, rep, place)
        o_pad = _attn_sample(q_pad.reshape(dec, N_HEADS, nkv), keys_s, vals_s,
                             attn_sinks[j].astype(F32).reshape(N_HEADS, 1))
        xs = _proj_residual(xs, o_pad.reshape(dec * N_HEADS, nkv), place.T, rep.T, wo)
        xs = _mlp_small(xs, g_mlp, w1, w2, layer)

    return (xp, xs.reshape(dec, 1, D_MODEL),
            jnp.stack(sp_re), jnp.stack(sp_im), new_k_p, new_v_p,
            jnp.stack(ss_re), jnp.stack(ss_im), new_k_s, new_v_s)
```

```python
import jax
import jax.numpy as jnp
from jax import lax
from jax.experimental import pallas as pl
from jax.experimental.pallas import tpu as pltpu

F32 = jnp.float32
BF16 = jnp.bfloat16

D_MODEL = 1024
N_GROUPS = 64
GROUP_SIZE = 16
STATE_DIM = 64
HEAD_DIM = 64
N_HEADS = 16
N_KV_HEADS = 4
WINDOW = 128
ROPE_THETA = 10000.0
D_FF = 4 * D_MODEL
EPS = 1e-6

SUBLANES = 8
LANES = 128
MXU_TILE = 256
VMEM_LIMIT_BYTES = 56 * 1024 * 1024

N_OCTETS = D_MODEL // LANES
OCTET_STATE = (LANES // GROUP_SIZE) * STATE_DIM
STATE_LANES = 2 * OCTET_STATE * N_OCTETS

S5_TIME_CHUNK = 128
MLP_ROWS = 1024
FF_CHUNK = 1024
KV_ROWS = 1024
ATTN_BLOCKS = 8


def _const_spec(shape):
    zeros = (0,) * len(shape)
    return pl.BlockSpec(shape, lambda *_: zeros, pipeline_mode=pl.Buffered(1))


def _layer_spec(shape, layer):
    idx = (layer,) + (0,) * (len(shape) - 1)
    return pl.BlockSpec((None,) + tuple(shape[1:]), lambda *_: idx, pipeline_mode=pl.Buffered(1))


def _params(sem):
    return pltpu.CompilerParams(dimension_semantics=sem, vmem_limit_bytes=VMEM_LIMIT_BYTES)


def _rms(x, g):
    return x * lax.rsqrt(jnp.mean(x * x, axis=-1, keepdims=True) + EPS) * g


def _dot(a, b):
    return jnp.dot(a, b, preferred_element_type=F32)


def _seg_mean_sq(x, ones_ref):
    sq = x * x
    hi = sq.astype(BF16)
    lo = (sq - hi.astype(F32)).astype(BF16)
    ones = ones_ref[...]
    outs = []
    for t in range(x.shape[-1] // MXU_TILE):
        sl = slice(MXU_TILE * t, MXU_TILE * (t + 1))
        outs.append(_dot(hi[:, sl], ones) + _dot(lo[:, sl], ones))
    return jnp.concatenate(outs, axis=-1) * (1.0 / HEAD_DIM)


def _tile_lanes(t, n):
    return jnp.concatenate([t] * (n // t.shape[-1]), axis=-1)


def _rope(x, cos, sin_signed):
    n = x.shape[-1]
    lane = lax.broadcasted_iota(jnp.int32, x.shape, 1)
    first = (lane & (HEAD_DIM - 1)) < (HEAD_DIM // 2)
    partner = jnp.where(first, pltpu.roll(x, n - HEAD_DIM // 2, 1), pltpu.roll(x, HEAD_DIM // 2, 1))
    return x * cos + partner * sin_signed


def _mlp_rows(x, g_ref, w1_ref, w2_ref):
    h = _rms(x, g_ref[...]).astype(BF16)
    acc = x
    for c in range(D_FF // FF_CHUNK):
        sl = slice(FF_CHUNK * c, FF_CHUNK * (c + 1))
        a = jnp.square(jnp.maximum(_dot(h, w1_ref[:, sl]), 0.0)).astype(BF16)
        acc = acc + _dot(a, w2_ref[sl, :])
    return acc


def _mlp_kernel(x_ref, xs_ref, g_ref, w1_ref, w2_ref, o_ref, os_ref):
    last = pl.num_programs(0) - 1

    @pl.when(pl.program_id(0) < last)
    def _():
        o_ref[...] = _mlp_rows(x_ref[...], g_ref, w1_ref, w2_ref)

    @pl.when(pl.program_id(0) == last)
    def _():
        os_ref[...] = _mlp_rows(xs_ref[...], g_ref, w1_ref, w2_ref)


def _mlp(x, xs, g, w1, w2, layer):
    rows = x.shape[0]
    tm = MLP_ROWS
    tiles = rows // tm
    x_spec = pl.BlockSpec((tm, D_MODEL), lambda i: (jnp.minimum(i, tiles - 1), 0))
    xs_spec = pl.BlockSpec(xs.shape, lambda i: (0, 0))
    return pl.pallas_call(
        _mlp_kernel,
        grid=(tiles + 1,),
        in_specs=[x_spec, xs_spec,
                  _layer_spec(g.shape, layer),
                  _layer_spec(w1.shape, layer),
                  _layer_spec(w2.shape, layer)],
        out_specs=[x_spec, xs_spec],
        out_shape=[jax.ShapeDtypeStruct((rows, D_MODEL), F32), jax.ShapeDtypeStruct(xs.shape, F32)],
        compiler_params=_params(("arbitrary",)),
        name="mlp",
    )(x, xs, g, w1, w2)


def _s5_gate(y, wglu_ref, bglu_ref):
    g = jax.nn.gelu(y).astype(BF16)
    z = _dot(g, wglu_ref[...]) + bglu_ref[...]
    return z[:, :D_MODEL] * jax.nn.sigmoid(z[:, D_MODEL:])


def _slabs(ref, rows=slice(None)):
    return jnp.concatenate([ref[c, rows, :] for c in range(ref.shape[0])], axis=-1)


def _s5_prompt_kernel(xa_ref, xb_ref, gn_ref, b2_ref, lam_ref, c2_ref, m0_ref, d_ref, wglu_ref, bglu_ref,
                      o_ref, fin_ref, st_ref, carry_ref, nat_ref, ue_ref, uo_ref, up_ref, tb2_ref, nat2_ref, y_ref):
    batch, tc, _ = xa_ref.shape
    assert 2 * batch == SUBLANES
    n_slabs = D_MODEL // LANES
    pairs = tc // 2
    rows_k = pairs * batch
    step = pl.program_id(0)
    n_chunks = pl.num_programs(0) - 1
    live = step < n_chunks

    @pl.when(step == 0)
    def _():
        carry_ref[...] = jnp.zeros_like(carry_ref)
        y_ref[...] = jnp.zeros_like(y_ref)
        up_ref[...] = jnp.zeros_like(up_ref)

    g_prev = jax.nn.gelu(y_ref[...]).astype(BF16)
    glu_cols = 2 * D_MODEL // (N_OCTETS // 2)
    z = []

    def glu_part(c):
        sl = slice(glu_cols * c, glu_cols * (c + 1))
        z.append(_dot(g_prev, wglu_ref[:, sl]) + bglu_ref[:, sl])

    def finish_prev():
        zs = jnp.concatenate(z, axis=-1)
        upd = zs[:, :D_MODEL] * jax.nn.sigmoid(zs[:, D_MODEL:])
        for c in range(n_slabs):
            tb2_ref[c] = upd[:, LANES * c:LANES * (c + 1)]
        for b in range(batch):
            for c in range(n_slabs):
                nat2_ref[c, pl.ds(0, pairs, stride=2), :] = tb2_ref[c, pl.ds(b, pairs, stride=batch), :]
                nat2_ref[c, pl.ds(1, pairs, stride=2), :] = tb2_ref[c, pl.ds(rows_k + b, pairs, stride=batch), :]
            o_ref[b] = xb_ref[b] + _slabs(nat2_ref)

    for c in range(n_slabs):
        up_ref[c, 0:batch, :] = up_ref[c, rows_k:rows_k + batch, :]
    for b in range(batch):
        u_nat = _rms(xa_ref[b], gn_ref[...])
        for c in range(n_slabs):
            nat_ref[c] = u_nat[:, LANES * c:LANES * (c + 1)]
        for c in range(n_slabs):
            even = nat_ref[c, pl.ds(0, pairs, stride=2), :]
            odd = nat_ref[c, pl.ds(1, pairs, stride=2), :]
            ue_ref[c, pl.ds(b, pairs, stride=batch), :] = even
            uo_ref[c, pl.ds(b, pairs, stride=batch), :] = odd
            up_ref[c, pl.ds(batch + b, pairs, stride=batch), :] = odd
    ue, uo = _slabs(ue_ref), _slabs(uo_ref)
    ue_b, uo_b, up_b = ue.astype(BF16), uo.astype(BF16), _slabs(up_ref, slice(0, rows_k)).astype(BF16)

    lo = lax.broadcasted_iota(jnp.int32, (SUBLANES, LANES), 0) < batch
    swap = lambda v: pltpu.roll(v, batch, 0)
    n_vregs = rows_k // SUBLANES
    half = OCTET_STATE // LANES
    y_even, y_odd = [], []
    for q in range(N_OCTETS):
        ch = slice(LANES * q, LANES * (q + 1))
        lhs = jnp.concatenate([up_b[:, ch], ue_b[:, ch]], axis=1)
        st_ref[:, 2 * OCTET_STATE * q:2 * OCTET_STATE * (q + 1)] = _dot(lhs, b2_ref[q])

    def project_out(octets):
        for q in octets:
            ch = slice(LANES * q, LANES * (q + 1))
            y2 = _dot(st_ref[:, 2 * OCTET_STATE * q:2 * OCTET_STATE * (q + 1)].astype(BF16), c2_ref[q])
            y_even.append(y2[:, :LANES])
            y_odd.append(y2[:, LANES:] + _dot(uo_b[:, ch], m0_ref[q]))

    for q0 in range(0, N_OCTETS, 2):
        octets = (q0, q0 + 1)
        glu_part(q0 // 2)
        if q0 > 0:
            project_out((q0 - 2, q0 - 1))
        if q0 == N_OCTETS - 2:
            finish_prev()
        chains = []
        for q in octets:
            for j in range(half // 2):
                cols = [2 * OCTET_STATE * q + LANES * jj for jj in (j, j + half // 2)]
                lanes = [OCTET_STATE * q + LANES * jj for jj in (j, j + half // 2)]
                chains.append((cols, lanes))
        lam_r, lam_i, sr, si, old = [], [], [], [], []
        for (ca, cb), (la, lb) in chains:
            bc = lambda r, c: jnp.broadcast_to(lam_ref[r:r + 1, c:c + LANES], (SUBLANES, LANES))
            lam_r.append(jnp.where(lo, bc(2, la), bc(2, lb)))
            lam_i.append(jnp.where(lo, bc(3, la), bc(3, lb)))
            prev = [carry_ref[:, c:c + LANES] for c in (ca, cb, ca + OCTET_STATE, cb + OCTET_STATE)]
            old.append(prev)
            sr.append(jnp.where(lo, swap(prev[0]), prev[1]))
            si.append(jnp.where(lo, swap(prev[2]), prev[3]))
        last = [None] * len(chains)
        for m in range(n_vregs):
            rows = slice(SUBLANES * m, SUBLANES * (m + 1))
            for j, ((ca, cb), _) in enumerate(chains):
                ar, br = st_ref[rows, ca:ca + LANES], st_ref[rows, cb:cb + LANES]
                ai, bi = (st_ref[rows, ca + OCTET_STATE:ca + OCTET_STATE + LANES],
                          st_ref[rows, cb + OCTET_STATE:cb + OCTET_STATE + LANES])
                d0r, d1r = jnp.where(lo, ar, swap(br)), jnp.where(lo, swap(ar), br)
                d0i, d1i = jnp.where(lo, ai, swap(bi)), jnp.where(lo, swap(ai), bi)
                s0r = lam_r[j] * sr[j] - lam_i[j] * si[j] + d0r
                s0i = lam_r[j] * si[j] + lam_i[j] * sr[j] + d0i
                s1r = lam_r[j] * s0r - lam_i[j] * s0i + d1r
                s1i = lam_r[j] * s0i + lam_i[j] * s0r + d1i
                sr[j], si[j] = s1r, s1i
                out = (jnp.where(lo, s0r, swap(s1r)), jnp.where(lo, swap(s0r), s1r),
                       jnp.where(lo, s0i, swap(s1i)), jnp.where(lo, swap(s0i), s1i))
                st_ref[rows, ca:ca + LANES] = out[0]
                st_ref[rows, cb:cb + LANES] = out[1]
                st_ref[rows, ca + OCTET_STATE:ca + OCTET_STATE + LANES] = out[2]
                st_ref[rows, cb + OCTET_STATE:cb + OCTET_STATE + LANES] = out[3]
                last[j] = out
        for j, ((ca, cb), _) in enumerate(chains):
            for c, new, prev in zip((ca, cb, ca + OCTET_STATE, cb + OCTET_STATE), last[j], old[j]):
                carry_ref[:, c:c + LANES] = jnp.where(live, new, prev)
    project_out((N_OCTETS - 2, N_OCTETS - 1))
    y_ref[0:rows_k, :] = jnp.concatenate(y_even, axis=-1) + d_ref[...] * ue
    y_ref[rows_k:2 * rows_k, :] = jnp.concatenate(y_odd, axis=-1) + d_ref[...] * uo

    @pl.when(step == n_chunks - 1)
    def _():
        tail = uo_b[rows_k - 2 * SUBLANES:, :]
        zeros = jnp.zeros((2 * SUBLANES, LANES), BF16)
        for q in range(N_OCTETS):
            bu = _dot(jnp.concatenate([zeros, tail[:, LANES * q:LANES * (q + 1)]], axis=1), b2_ref[q])[SUBLANES:]
            base = 2 * OCTET_STATE * q
            lr = lam_ref[0:1, OCTET_STATE * q:OCTET_STATE * (q + 1)]
            li = lam_ref[1:2, OCTET_STATE * q:OCTET_STATE * (q + 1)]
            s_r = carry_ref[:, base:base + OCTET_STATE]
            s_i = carry_ref[:, base + OCTET_STATE:base + 2 * OCTET_STATE]
            fin_ref[:, base:base + OCTET_STATE] = lr * s_r - li * s_i + bu[:, :OCTET_STATE]
            fin_ref[:, base + OCTET_STATE:base + 2 * OCTET_STATE] = lr * s_i + li * s_r + bu[:, OCTET_STATE:]


def _s5_prompt(x, layer, gn, b2, lam, c2, m0, d, wglu, bglu):
    batch, seq, _ = x.shape
    tc = S5_TIME_CHUNK
    n_chunks = seq // tc
    rows_k = tc // 2 * batch
    n_slabs = D_MODEL // LANES
    blk = (batch, tc, D_MODEL)
    this_chunk = pl.BlockSpec(blk, lambda s: (0, jnp.minimum(s, n_chunks - 1), 0))
    prev_chunk = pl.BlockSpec(blk, lambda s: (0, jnp.maximum(s - 1, 0), 0))
    slab = lambda rows: pltpu.VMEM((n_slabs, rows, LANES), F32)
    return pl.pallas_call(
        _s5_prompt_kernel,
        grid=(n_chunks + 1,),
        in_specs=[this_chunk, prev_chunk]
        + [_layer_spec(w.shape, layer) for w in (gn, b2, lam, c2, m0, d, wglu, bglu)],
        out_specs=[prev_chunk, pl.BlockSpec((SUBLANES, STATE_LANES), lambda s: (0, 0))],
        out_shape=[jax.ShapeDtypeStruct(x.shape, F32),
                   jax.ShapeDtypeStruct((SUBLANES, STATE_LANES), F32)],
        scratch_shapes=[pltpu.VMEM((rows_k, STATE_LANES), F32),
                        pltpu.VMEM((SUBLANES, STATE_LANES), F32),
                        slab(tc), slab(rows_k), slab(rows_k), slab(rows_k + SUBLANES),
                        slab(2 * rows_k), slab(tc),
                        pltpu.VMEM((2 * rows_k, D_MODEL), F32)],
        compiler_params=_params(("arbitrary",)),
        name="s5_prompt",
    )(x, x, gn, b2, lam, c2, m0, d, wglu, bglu)


def _s5_sample_kernel(x_ref, gn_ref, b2_ref, lam_ref, c2_ref, d_ref, wglu_ref, bglu_ref, s0_ref,
                      o_ref, snew_ref):
    x = x_ref[...]
    u = _rms(x, gn_ref[...])
    ys = []
    for q in range(N_OCTETS):
        base = 2 * OCTET_STATE * q
        bu = jnp.dot(u[:, LANES * q:LANES * (q + 1)], b2_ref[q, LANES:, :],
                     preferred_element_type=F32, precision=lax.Precision.HIGHEST)
        lr = lam_ref[0:1, OCTET_STATE * q:OCTET_STATE * (q + 1)]
        li = lam_ref[1:2, OCTET_STATE * q:OCTET_STATE * (q + 1)]
        s0r = s0_ref[:, base:base + OCTET_STATE]
        s0i = s0_ref[:, base + OCTET_STATE:base + 2 * OCTET_STATE]
        sr = lr * s0r - li * s0i + bu[:, :OCTET_STATE]
        si = lr * s0i + li * s0r + bu[:, OCTET_STATE:]
        snew_ref[:, base:base + OCTET_STATE] = sr
        snew_ref[:, base + OCTET_STATE:base + 2 * OCTET_STATE] = si
        sb = jnp.concatenate([sr, si], axis=-1).astype(BF16)
        ys.append(_dot(sb, c2_ref[q, :, :LANES]))
    y = jnp.concatenate(ys, axis=-1)
    o_ref[...] = x + _s5_gate(y + d_ref[...] * u, wglu_ref, bglu_ref)


def _s5_sample(x, layer, gn, b2_f32, lam, c2, d, wglu, bglu, s0):
    rows = x.shape[0]
    return pl.pallas_call(
        _s5_sample_kernel,
        grid=(1,),
        in_specs=[_const_spec(x.shape)] + [_layer_spec(w.shape, layer) for w in (gn, b2_f32, lam, c2, d, wglu, bglu)]
        + [_const_spec(s0.shape)],
        out_specs=[pl.BlockSpec((rows, D_MODEL), lambda i: (0, 0)),
                   pl.BlockSpec((rows, STATE_LANES), lambda i: (0, 0))],
        out_shape=[jax.ShapeDtypeStruct((rows, D_MODEL), F32),
                   jax.ShapeDtypeStruct((rows, STATE_LANES), F32)],
        compiler_params=_params(("arbitrary",)),
        name="s5_sample",
    )(x, gn, b2_f32, lam, c2, d, wglu, bglu, s0)


def _block_diag(w):
    n, octets, groups, r, c = w.shape
    col = jnp.arange(groups * c)
    tile = (jnp.arange(c)[:, None] == col % c).astype(F32)
    own = (jnp.arange(groups * r)[:, None] // r == col // c).astype(F32)
    tiled = jnp.einsum("nqrc,cl->nqrl", w.reshape(n, octets, groups * r, c), tile, precision=lax.Precision.HIGHEST)
    return tiled * own


def _s5_weights(a_re, a_im, log_dt, b_re, b_im, c_re, c_im):
    n = a_re.shape[0]
    dt = jnp.exp(log_dt.astype(F32))
    lam = lax.complex(a_re.astype(F32), a_im.astype(F32))
    lam_bar = jnp.exp(lam * dt)
    b = lax.complex(b_re.astype(F32), b_im.astype(F32))
    b_bar = ((lam_bar - 1.0) / lam)[..., None] * b
    c = lax.complex(c_re.astype(F32), c_im.astype(F32))
    gpo = N_GROUPS // N_OCTETS
    octets = lambda w: w.reshape((n, N_OCTETS, gpo) + w.shape[2:])

    def into_state(w):
        wt = jnp.swapaxes(octets(w), 3, 4)
        return jnp.concatenate([_block_diag(jnp.real(wt)), _block_diag(jnp.imag(wt))], axis=-1)

    def out_of_state(w):
        wt = jnp.swapaxes(octets(w), 3, 4)
        return jnp.concatenate([_block_diag(jnp.real(wt)), -_block_diag(jnp.imag(wt))], axis=2)

    b2 = jnp.concatenate([into_state(lam_bar[..., None] * b_bar), into_state(b_bar)], axis=2)
    c2 = jnp.concatenate([out_of_state(c), out_of_state(c * lam_bar[:, :, None, :])], axis=-1)
    cb = jnp.real(jnp.einsum("ngcp,ngpd->ngdc", c, b_bar))
    m0 = _block_diag(octets(cb))
    lam2 = lam_bar * lam_bar
    lam_rows = jnp.stack([jnp.real(lam_bar).reshape(n, -1), jnp.imag(lam_bar).reshape(n, -1),
                          jnp.real(lam2).reshape(n, -1), jnp.imag(lam2).reshape(n, -1)], axis=1)
    return b2, c2.astype(BF16), m0.astype(BF16), lam_rows


def _state_to_lanes(s_re, s_im):
    n = s_re.shape[0]
    re = s_re.astype(F32).reshape(n, N_OCTETS, OCTET_STATE)
    im = s_im.astype(F32).reshape(n, N_OCTETS, OCTET_STATE)
    return jnp.concatenate([re, im], axis=-1).reshape(n, STATE_LANES)


def _lanes_to_state(s):
    n = s.shape[0]
    s4 = s.reshape(n, N_OCTETS, 2, OCTET_STATE)
    return (s4[:, :, 0].reshape(n, N_GROUPS, STATE_DIM), s4[:, :, 1].reshape(n, N_GROUPS, STATE_DIM))


def _rope_partner(n):
    lane = jnp.arange(n)
    return jnp.where(lane % HEAD_DIM < HEAD_DIM // 2, lane + HEAD_DIM // 2, lane - HEAD_DIM // 2)


def _kv_kernel(x_ref, g_ref, wkv_ref, kg_ref, kgp_ref, ones_ref, cos_ref, sin_ref, *rest):
    if len(rest) == 3:
        wvt_ref, k_ref, vt_ref = rest
        v_ref = None
    else:
        (k_ref, v_ref), wvt_ref, vt_ref = rest, None, None
    h = _rms(x_ref[...], g_ref[...]).astype(BF16)
    kv = _dot(h, wkv_ref[...])
    n = k_ref.shape[-1]
    k, k_partner = kv[:, :n], kv[:, n:2 * n]
    scale = lax.rsqrt(_seg_mean_sq(k, ones_ref) + EPS)
    k_ref[...] = scale * (k * (kg_ref[...] * _tile_lanes(cos_ref[...], n))
                          + k_partner * (kgp_ref[...] * _tile_lanes(sin_ref[...], n)))
    if v_ref is not None:
        v_ref[...] = kv[:, 2 * n:]
    else:
        vt_ref[...] = lax.dot_general(wvt_ref[...], h, (((1,), (1,)), ((), ())), preferred_element_type=F32)


def _kv(x, g, wk, wv, kgain, ones, cos, sin, table_block, seq=None):
    rows = x.shape[0]
    n, nv = wk.shape[1], wv.shape[1]
    partner = _rope_partner(n)
    wkv = jnp.concatenate([wk, wk[:, partner]] + ([wv] if seq is None else []), axis=1).astype(BF16)
    tm = min(KV_ROWS, rows)
    in_specs = [pl.BlockSpec((tm, D_MODEL), lambda i: (i, 0)),
                _const_spec((1, D_MODEL)),
                _const_spec(wkv.shape),
                _const_spec((1, n)),
                _const_spec((1, n)),
                _const_spec(ones.shape),
                pl.BlockSpec((tm, LANES), lambda i: (table_block(i), 0)),
                pl.BlockSpec((tm, LANES), lambda i: (table_block(i), 0))]
    out_specs = [pl.BlockSpec((tm, n), lambda i: (i, 0))]
    out_shape = [jax.ShapeDtypeStruct((rows, n), F32)]
    args = [x, g, wkv, kgain, kgain[:, partner], ones, cos, sin]
    if seq is None:
        out_specs.append(pl.BlockSpec((tm, nv), lambda i: (i, 0)))
        out_shape.append(jax.ShapeDtypeStruct((rows, nv), F32))
    else:
        per_seq = seq // tm
        in_specs.append(_const_spec((nv, D_MODEL)))
        args.append(wv.T.astype(BF16))
        out_specs.append(pl.BlockSpec((None, nv, tm), lambda i: (i // per_seq, 0, i % per_seq)))
        out_shape.append(jax.ShapeDtypeStruct((rows // seq, nv, seq), F32))
    return pl.pallas_call(
        _kv_kernel,
        grid=(rows // tm,),
        in_specs=in_specs,
        out_specs=out_specs,
        out_shape=out_shape,
        compiler_params=_params(("parallel",)),
        name="kv_proj",
    )(*args)


def _attn_prompt_kernel(sinks_ref, x_ref, g_ref, wq_ref, qg_ref, ones_ref, cos_ref, sin_ref,
                        kp_ref, kc_ref, vp_ref, vc_ref, wo_ref, o_ref):
    tile = pl.program_id(1)
    x = x_ref[...]
    n_blocks = x.shape[0] // WINDOW
    heads_per_kv = N_HEADS // N_KV_HEADS
    cols = heads_per_kv * WINDOW
    h = _rms(x, g_ref[...]).astype(BF16)
    q = _dot(h, wq_ref[...])
    q = q * lax.rsqrt(_seg_mean_sq(q, ones_ref) + EPS) * qg_ref[...]
    q = _rope(q, _tile_lanes(cos_ref[...], D_MODEL), _tile_lanes(sin_ref[...], D_MODEL))
    qb = (q * (HEAD_DIM ** -0.5)).astype(BF16)

    kall = jnp.concatenate([kp_ref[...], kc_ref[...]], axis=0).astype(BF16)
    vall = jnp.concatenate([vp_ref[...], vc_ref[...]], axis=1).astype(BF16)

    kj = lax.broadcasted_iota(jnp.int32, (WINDOW, cols), 0)
    col = lax.broadcasted_iota(jnp.int32, (WINDOW, cols), 1)
    from_prev = kj > (col & (WINDOW - 1))
    col_head = lax.broadcasted_iota(jnp.int32, (1, cols), 1) // WINDOW
    low_half = lax.broadcasted_iota(jnp.int32, (WINDOW, LANES), 1) < HEAD_DIM
    neg_inf = jnp.full((WINDOW, cols), -jnp.inf, F32)

    pairs = [(blk, g) for blk in range(n_blocks) for g in range(N_KV_HEADS)]
    scores, sinks = [], []
    for blk, g in pairs:
        r0 = WINDOW * blk
        kg = kall[r0:r0 + 2 * WINDOW, LANES * g:LANES * (g + 1)]
        qs = []
        sink = jnp.zeros((1, cols), F32)
        for hh in range(heads_per_kv):
            head = heads_per_kv * g + hh
            qp = qb[r0:r0 + WINDOW, LANES * (head // 2):LANES * (head // 2 + 1)]
            keep = low_half if head % 2 == 0 else jnp.logical_not(low_half)
            qs.append(jnp.where(keep, qp, jnp.zeros_like(qp)))
            sink = jnp.where(col_head == hh, sinks_ref[head], sink)
        sinks.append(sink)
        scores.append(lax.dot_general(kg, jnp.concatenate(qs, axis=0), (((1,), (1,)), ((), ())),
                                      preferred_element_type=F32))
    probs, denoms = [], []
    for (blk, g), s, sink in zip(pairs, scores, sinks):
        has_prev = tile * n_blocks + blk > 0
        t = jnp.where(from_prev, jnp.where(has_prev, s[:WINDOW], neg_inf), s[WINDOW:])
        m = jnp.maximum(jnp.max(t, axis=0, keepdims=True), sink)
        p = jnp.exp(t - m)
        denoms.append(jnp.sum(p, axis=0, keepdims=True) + jnp.exp(sink - m))
        zero = jnp.zeros_like(p)
        probs.append(jnp.concatenate([jnp.where(from_prev, p, zero), jnp.where(from_prev, zero, p)],
                                     axis=0).astype(BF16))
    head_outs = [[] for _ in range(n_blocks)]
    for (blk, g), pcat, denom in zip(pairs, probs, denoms):
        r0 = WINDOW * blk
        vg = vall[HEAD_DIM * g:HEAD_DIM * (g + 1), r0:r0 + 2 * WINDOW]
        o = _dot(vg, pcat) / denom
        head_outs[blk] += [o[:, WINDOW * hh:WINDOW * (hh + 1)] for hh in range(heads_per_kv)]
    block_outs = [jnp.concatenate(outs, axis=0).T for outs in head_outs]
    attn = jnp.concatenate(block_outs, axis=0).astype(BF16)
    o_ref[...] = x + _dot(attn, wo_ref[...])


def _attn_prompt(x, g, wq, qgain, ones, cos, sin, kd, vt, sinks, wo):
    bsz, seq, _ = x.shape
    nk, nv = kd.shape[-1], vt.shape[1]
    nb = ATTN_BLOCKS
    tq = nb * WINDOW
    x_spec = pl.BlockSpec((None, tq, D_MODEL), lambda b, i, *_: (b, i, 0))
    k_cur = pl.BlockSpec((None, tq, nk), lambda b, i, *_: (b, i, 0))
    k_prev = pl.BlockSpec((None, WINDOW, nk), lambda b, i, *_: (b, jnp.maximum(nb * i - 1, 0), 0))
    v_cur = pl.BlockSpec((None, nv, tq), lambda b, i, *_: (b, 0, i))
    v_prev = pl.BlockSpec((None, nv, WINDOW), lambda b, i, *_: (b, 0, jnp.maximum(nb * i - 1, 0)))
    tab = pl.BlockSpec((tq, LANES), lambda b, i, *_: (i, 0))
    grid_spec = pltpu.PrefetchScalarGridSpec(
        num_scalar_prefetch=1,
        grid=(bsz, seq // tq),
        in_specs=[x_spec,
                  _const_spec((1, D_MODEL)),
                  _const_spec(wq.shape),
                  _const_spec((1, D_MODEL)),
                  _const_spec(ones.shape),
                  tab, tab, k_prev, k_cur, v_prev, v_cur,
                  _const_spec(wo.shape)],
        out_specs=x_spec)
    return pl.pallas_call(
        _attn_prompt_kernel,
        grid_spec=grid_spec,
        out_shape=jax.ShapeDtypeStruct(x.shape, F32),
        compiler_params=_params(("parallel", "arbitrary")),
        name="attn_prompt",
    )(sinks, x, g, wq, qgain, ones, cos, sin, kd, kd, vt, vt, wo)


def _own_head(shape):
    row_head = lax.broadcasted_iota(jnp.int32, shape, 0) & (N_HEADS - 1)
    lane_head = lax.broadcasted_iota(jnp.int32, shape, 1) // HEAD_DIM
    return row_head == lane_head


def _q_sample_kernel(x_ref, g_ref, wq_ref, qg_ref, ones_ref, cos_ref, sin_ref, rep_ref, place_ref, q_ref):
    h = _rms(x_ref[...], g_ref[...]).astype(BF16)
    q = _dot(h, wq_ref[...])
    q = q * lax.rsqrt(_seg_mean_sq(q, ones_ref) + EPS) * qg_ref[...]
    q = _rope(q, _tile_lanes(cos_ref[...], D_MODEL), _tile_lanes(sin_ref[...], D_MODEL))
    qb = (q * (HEAD_DIM ** -0.5)).astype(BF16)
    q_rep = _dot(rep_ref[...], qb)
    q_own = jnp.where(_own_head(q_rep.shape), q_rep, 0.0).astype(BF16)
    q_ref[...] = _dot(q_own, place_ref[...])


def _q_sample(x, g, wq, qgain, ones, cos, sin, rep, place):
    rows = x.shape[0]
    return pl.pallas_call(
        _q_sample_kernel,
        out_shape=jax.ShapeDtypeStruct((rows * N_HEADS, place.shape[1]), F32),
        compiler_params=pltpu.CompilerParams(vmem_limit_bytes=VMEM_LIMIT_BYTES),
        name="q_sample",
    )(x, g, wq, qgain, ones, cos, sin, rep, place)


def _cache_append_kernel(k_ref, v_ref, kn_ref, vn_ref, ko_ref, vo_ref):
    nb, lanes_kv, keys = k_ref.shape
    first = pl.program_id(0) * nb
    key = lax.broadcasted_iota(jnp.int32, (lanes_kv, keys), 1)
    seq = lax.broadcasted_iota(jnp.int32, (kn_ref.shape[1], keys), 0)

    def split3(x):
        hi = x.astype(BF16)
        r = x - hi.astype(F32)
        mid = r.astype(BF16)
        return hi, mid, (r - mid.astype(F32)).astype(BF16)

    parts = [(split3(kn_ref[...]), k_ref, ko_ref), (split3(vn_ref[...]), v_ref, vo_ref)]
    for b in range(nb):
        pick = (seq == first + b).astype(BF16)
        for (hi, mid, lo), old_ref, out_ref in parts:
            new_col = _dot(hi, pick) + _dot(mid, pick) + _dot(lo, pick)
            out_ref[b] = jnp.where(key == keys - 1, new_col, pltpu.roll(old_ref[b], keys - 1, 1))


def _cache_append(kt, vt, kn_t, vn_t):
    bsz, lanes_kv, keys = kt.shape
    nb = SUBLANES
    blk = pl.BlockSpec((nb, lanes_kv, keys), lambda i: (i, 0, 0))
    return pl.pallas_call(
        _cache_append_kernel,
        grid=(bsz // nb,),
        in_specs=[blk, blk, _const_spec(kn_t.shape), _const_spec(vn_t.shape)],
        out_specs=[blk, blk],
        out_shape=[jax.ShapeDtypeStruct(kt.shape, F32), jax.ShapeDtypeStruct(vt.shape, F32)],
        compiler_params=_params(("parallel",)),
        name="cache_append",
    )(kt, vt, kn_t, vn_t)


def _attn_sample_kernel(q_ref, k_ref, v_ref, sink_ref, o_ref):
    sink = sink_ref[...]
    for b in range(q_ref.shape[0]):
        s = _dot(q_ref[b].astype(BF16), k_ref[b].astype(BF16))
        m = jnp.maximum(jnp.max(s, axis=-1, keepdims=True), sink)
        p = jnp.exp(s - m)
        denom = jnp.sum(p, axis=-1, keepdims=True) + jnp.exp(sink - m)
        o = lax.dot_general(p.astype(BF16), v_ref[b].astype(BF16), (((1,), (1,)), ((), ())),
                            preferred_element_type=F32)
        o_ref[b] = o / denom


def _attn_sample(q3, kt, vt, sink_col):
    bsz, nh, nkv = q3.shape
    keys = kt.shape[2]
    bb = 2 * SUBLANES
    return pl.pallas_call(
        _attn_sample_kernel,
        grid=(bsz // bb,),
        in_specs=[pl.BlockSpec((bb, nh, nkv), lambda i: (i, 0, 0)),
                  pl.BlockSpec((bb, nkv, keys), lambda i: (i, 0, 0)),
                  pl.BlockSpec((bb, nkv, keys), lambda i: (i, 0, 0)),
                  _const_spec(sink_col.shape)],
        out_specs=pl.BlockSpec((bb, nh, nkv), lambda i: (i, 0, 0)),
        out_shape=jax.ShapeDtypeStruct((bsz, nh, nkv), F32),
        compiler_params=_params(("parallel",)),
        name="attn_sample",
    )(q3, kt, vt, sink_col)


def _proj_residual_kernel(x_ref, a_ref, place_t_ref, rep_t_ref, w_ref, o_ref):
    z = _dot(a_ref[...].astype(BF16), place_t_ref[...])
    z_own = jnp.where(_own_head(z.shape), z, 0.0).astype(BF16)
    attn = _dot(rep_t_ref[...], z_own).astype(BF16)
    o_ref[...] = x_ref[...] + _dot(attn, w_ref[...])


def _proj_residual(x, a, place_t, rep_t, w):
    return pl.pallas_call(
        _proj_residual_kernel,
        out_shape=jax.ShapeDtypeStruct(x.shape, F32),
        compiler_params=pltpu.CompilerParams(vmem_limit_bytes=VMEM_LIMIT_BYTES),
        name="o_proj_sample",
    )(x, a, place_t, rep_t, w)


def _rope_tables(pos):
    half = HEAD_DIM // 2
    inv = ROPE_THETA ** (-jnp.arange(half, dtype=F32) / half)
    ang = pos.astype(F32)[:, None] * inv[None, :]
    cos, sin = jnp.cos(ang), jnp.sin(ang)
    reps = LANES // HEAD_DIM
    return (jnp.tile(jnp.concatenate([cos, cos], axis=-1), (1, reps)),
            jnp.tile(jnp.concatenate([-sin, sin], axis=-1), (1, reps)))


def _block_ones(width, block):
    idx = jnp.arange(width) // block
    return (idx[:, None] == idx[None, :]).astype(BF16)


def _dup_heads(w):
    w3 = w.reshape(w.shape[0], N_KV_HEADS, 1, HEAD_DIM)
    return jnp.broadcast_to(w3, (w.shape[0], N_KV_HEADS, LANES // HEAD_DIM, HEAD_DIM)).reshape(w.shape[0], -1)


def _head_placement():
    lane = jnp.arange(N_HEADS * HEAD_DIM)
    dst = (lane // HEAD_DIM) // (N_HEADS // N_KV_HEADS) * HEAD_DIM + lane % HEAD_DIM
    return (dst[:, None] == jnp.arange(N_KV_HEADS * HEAD_DIM)[None, :]).astype(BF16)


def _row_replication(n):
    return (jnp.arange(n * N_HEADS)[:, None] // N_HEADS == jnp.arange(n)[None, :]).astype(BF16)


def kernel(x_prompt, x_sample, state_ssm_re, state_ssm_im, cache_k, cache_v, norm_mix, norm_mlp, ssm_a_re, ssm_a_im, ssm_log_dt, ssm_b_re, ssm_b_im, ssm_c_re, ssm_c_im, ssm_d, w_glu, b_glu, norm_kv, w_k, w_v, k_norm, w_q, q_norm, attn_sinks, w_o, w_mlp_in, w_mlp_out):
    bsz, seq, _ = x_prompt.shape
    dec = x_sample.shape[0]
    n_a = ssm_a_re.shape[0]
    depth = norm_mix.shape[0]
    past = 8192
    nkv = N_KV_HEADS * HEAD_DIM
    row = lambda v: v.astype(F32).reshape(1, -1)

    xp = x_prompt.astype(F32)
    xs = x_sample.reshape(dec, D_MODEL)
    rows3 = lambda v: v.astype(F32).reshape(v.shape[0], 1, -1)

    w1 = w_mlp_in.astype(BF16)
    w2 = w_mlp_out.astype(BF16)
    g_mlp = rows3(norm_mlp)
    def mlp_both(x, xs, layer):
        y, ys = _mlp(x.reshape(bsz * seq, D_MODEL), xs, g_mlp, w1, w2, layer)
        return y.reshape(bsz, seq, D_MODEL), ys


    b2, c2, m0, lam = _s5_weights(ssm_a_re, ssm_a_im, ssm_log_dt, ssm_b_re, ssm_b_im, ssm_c_re, ssm_c_im)
    b2_16 = b2.astype(BF16)
    wglu = w_glu.astype(BF16)
    g_mix, d_skip, bg = rows3(norm_mix), rows3(ssm_d), rows3(b_glu)
    sp_re, sp_im, ss_re, ss_im = [], [], [], []
    for i in range(n_a):
        xp, fin = _s5_prompt(xp, i, g_mix, b2_16, lam, c2, m0, d_skip, wglu, bg)
        re, im = _lanes_to_state(fin[bsz:])
        sp_re.append(re)
        sp_im.append(im)
        s0 = _state_to_lanes(state_ssm_re[i], state_ssm_im[i])
        xs, snew = _s5_sample(xs, i, g_mix, b2, lam, c2, d_skip, wglu, bg, s0)
        re, im = _lanes_to_state(snew)
        ss_re.append(re)
        ss_im.append(im)
        xp, xs = mlp_both(xp, xs, i)

    cos_p, sin_p = _rope_tables(jnp.arange(seq, dtype=jnp.int32))
    cos_s, sin_s = _rope_tables(jnp.full((dec,), past, dtype=jnp.int32))
    ones_head = _block_ones(MXU_TILE, HEAD_DIM)
    place = _head_placement()
    rep = _row_replication(dec)
    kgain = row(k_norm)
    tile_gain = lambda gvec, n: jnp.tile(gvec, (1, n // HEAD_DIM))

    kd, vt = _kv(xp.reshape(bsz * seq, D_MODEL), row(norm_kv), _dup_heads(w_k), w_v,
                     tile_gain(kgain, 2 * nkv), ones_head, cos_p, sin_p, lambda i: i % (seq // KV_ROWS), seq=seq)
    ks_new, vs_new = _kv(xs, row(norm_kv), w_k, w_v, tile_gain(kgain, nkv), ones_head, cos_s, sin_s,
                         lambda i: i)
    kd = kd.reshape(bsz, seq, 2 * nkv)
    new_k_p = kd[:, -WINDOW:].reshape(bsz, WINDOW, N_KV_HEADS, LANES // HEAD_DIM, HEAD_DIM)[:, :, :, 0]
    new_v_p = jnp.transpose(vt[:, :, -WINDOW:].reshape(bsz, N_KV_HEADS, HEAD_DIM, WINDOW), (0, 3, 1, 2))
    key_minor = lambda c: jnp.transpose(c.astype(F32), (0, 2, 3, 1)).reshape(dec, nkv, c.shape[1])
    keys_s, vals_s = _cache_append(key_minor(cache_k), key_minor(cache_v), ks_new.T, vs_new.T)
    key_major = lambda c: jnp.transpose(c.reshape(dec, N_KV_HEADS, HEAD_DIM, c.shape[2]), (0, 3, 1, 2))
    new_k_s, new_v_s = key_major(keys_s), key_major(vals_s)

    for j in range(depth - n_a):
        layer = n_a + j
        gn = row(norm_mix[layer])
        qgain = row(q_norm[j])
        wq = w_q[j].astype(BF16)
        wo = w_o[j].astype(BF16)
        xp = _attn_prompt(xp, gn, wq, tile_gain(qgain, D_MODEL), ones_head, cos_p, sin_p, kd, vt,
                          attn_sinks[j].astype(F32), wo)
        q_pad = _q_sample(xs, gn, wq, tile_gain(qgain, D_MODEL), ones_head, cos_s, sin_s, rep, place)
        o_pad = _attn_sample(q_pad.reshape(dec, N_HEADS, nkv), keys_s, vals_s,
                             attn_sinks[j].astype(F32).reshape(N_HEADS, 1))
        xs = _proj_residual(xs, o_pad.reshape(dec * N_HEADS, nkv), place.T, rep.T, wo)
        xp, xs = mlp_both(xp, xs, layer)

    return (xp, xs.reshape(dec, 1, D_MODEL),
            jnp.stack(sp_re), jnp.stack(sp_im), new_k_p, new_v_p,
            jnp.stack(ss_re), jnp.stack(ss_im), new_k_s, new_v_s)
```

```python
import jax
import jax.numpy as jnp
from jax import lax
from jax.experimental import pallas as pl
from jax.experimental.pallas import tpu as pltpu

F32 = jnp.float32
BF16 = jnp.bfloat16

D_MODEL = 1024
N_GROUPS = 64
GROUP_SIZE = 16
STATE_DIM = 64
HEAD_DIM = 64
N_HEADS = 16
N_KV_HEADS = 4
WINDOW = 128
ROPE_THETA = 10000.0
D_FF = 4 * D_MODEL
EPS = 1e-6

SUBLANES = 8
LANES = 128
MXU_TILE = 256
VMEM_LIMIT_BYTES = 56 * 1024 * 1024

N_OCTETS = D_MODEL // LANES
OCTET_STATE = (LANES // GROUP_SIZE) * STATE_DIM
STATE_LANES = 2 * OCTET_STATE * N_OCTETS

S5_TIME_CHUNK = 128
MLP_ROWS = 1024
FF_CHUNK = 1024
KV_ROWS = 1024
ATTN_BLOCKS = 8


def _const_spec(shape):
    zeros = (0,) * len(shape)
    return pl.BlockSpec(shape, lambda *_: zeros, pipeline_mode=pl.Buffered(1))


def _layer_spec(shape, layer):
    idx = (layer,) + (0,) * (len(shape) - 1)
    return pl.BlockSpec((None,) + tuple(shape[1:]), lambda *_: idx, pipeline_mode=pl.Buffered(1))


def _params(sem):
    return pltpu.CompilerParams(dimension_semantics=sem, vmem_limit_bytes=VMEM_LIMIT_BYTES)


def _rms(x, g):
    return x * lax.rsqrt(jnp.mean(x * x, axis=-1, keepdims=True) + EPS) * g


def _dot(a, b):
    return jnp.dot(a, b, preferred_element_type=F32)


def _seg_mean_sq(x, ones_ref):
    sq = x * x
    hi = sq.astype(BF16)
    lo = (sq - hi.astype(F32)).astype(BF16)
    ones = ones_ref[...]
    outs = []
    for t in range(x.shape[-1] // MXU_TILE):
        sl = slice(MXU_TILE * t, MXU_TILE * (t + 1))
        outs.append(_dot(hi[:, sl], ones) + _dot(lo[:, sl], ones))
    return jnp.concatenate(outs, axis=-1) * (1.0 / HEAD_DIM)


def _tile_lanes(t, n):
    return jnp.concatenate([t] * (n // t.shape[-1]), axis=-1)


def _rope(x, cos, sin_signed):
    n = x.shape[-1]
    lane = lax.broadcasted_iota(jnp.int32, x.shape, 1)
    first = (lane & (HEAD_DIM - 1)) < (HEAD_DIM // 2)
    partner = jnp.where(first, pltpu.roll(x, n - HEAD_DIM // 2, 1), pltpu.roll(x, HEAD_DIM // 2, 1))
    return x * cos + partner * sin_signed


def _mlp_rows(x, g_ref, w1_ref, w2_ref):
    h = _rms(x, g_ref[...]).astype(BF16)
    acc = x
    for c in range(D_FF // FF_CHUNK):
        sl = slice(FF_CHUNK * c, FF_CHUNK * (c + 1))
        a = jnp.square(jnp.maximum(_dot(h, w1_ref[:, sl]), 0.0)).astype(BF16)
        acc = acc + _dot(a, w2_ref[sl, :])
    return acc


def _mlp_kernel(x_ref, xs_ref, g_ref, w1_ref, w2_ref, o_ref, os_ref):
    last = pl.num_programs(0) - 1

    @pl.when(pl.program_id(0) < last)
    def _():
        o_ref[...] = _mlp_rows(x_ref[...], g_ref, w1_ref, w2_ref)

    @pl.when(pl.program_id(0) == last)
    def _():
        os_ref[...] = _mlp_rows(xs_ref[...], g_ref, w1_ref, w2_ref)


def _mlp(x, xs, g, w1, w2, layer):
    rows = x.shape[0]
    tm = MLP_ROWS
    tiles = rows // tm
    x_spec = pl.BlockSpec((tm, D_MODEL), lambda i: (jnp.minimum(i, tiles - 1), 0))
    xs_spec = pl.BlockSpec(xs.shape, lambda i: (0, 0))
    return pl.pallas_call(
        _mlp_kernel,
        grid=(tiles + 1,),
        in_specs=[x_spec, xs_spec,
                  _layer_spec(g.shape, layer),
                  _layer_spec(w1.shape, layer),
                  _layer_spec(w2.shape, layer)],
        out_specs=[x_spec, xs_spec],
        out_shape=[jax.ShapeDtypeStruct((rows, D_MODEL), F32), jax.ShapeDtypeStruct(xs.shape, F32)],
        compiler_params=_params(("arbitrary",)),
        name="mlp",
    )(x, xs, g, w1, w2)


def _s5_gate(y, wglu_ref, bglu_ref):
    g = jax.nn.gelu(y).astype(BF16)
    z = _dot(g, wglu_ref[...]) + bglu_ref[...]
    return z[:, :D_MODEL] * jax.nn.sigmoid(z[:, D_MODEL:])


def _slabs(ref, rows=slice(None)):
    return jnp.concatenate([ref[c, rows, :] for c in range(ref.shape[0])], axis=-1)


def _s5_prompt_kernel(xa_ref, xb_ref, gn_ref, b2_ref, lam_ref, c2_ref, m0_ref, d_ref, wglu_ref, bglu_ref,
                      o_ref, fin_ref, st_ref, carry_ref, nat_ref, ue_ref, uo_ref, up_ref, tb2_ref, nat2_ref, y_ref):
    batch, tc, _ = xa_ref.shape
    assert 2 * batch == SUBLANES
    n_slabs = D_MODEL // LANES
    pairs = tc // 2
    rows_k = pairs * batch
    step = pl.program_id(0)
    n_chunks = pl.num_programs(0) - 1
    live = step < n_chunks

    @pl.when(step == 0)
    def _():
        carry_ref[...] = jnp.zeros_like(carry_ref)
        y_ref[...] = jnp.zeros_like(y_ref)
        up_ref[...] = jnp.zeros_like(up_ref)

    g_prev = jax.nn.gelu(y_ref[...]).astype(BF16)
    glu_cols = 2 * D_MODEL // (N_OCTETS // 2)
    z = []

    def glu_part(c):
        sl = slice(glu_cols * c, glu_cols * (c + 1))
        z.append(_dot(g_prev, wglu_ref[:, sl]) + bglu_ref[:, sl])

    def finish_prev():
        zs = jnp.concatenate(z, axis=-1)
        upd = zs[:, :D_MODEL] * jax.nn.sigmoid(zs[:, D_MODEL:])
        for c in range(n_slabs):
            tb2_ref[c] = upd[:, LANES * c:LANES * (c + 1)]
        for b in range(batch):
            for c in range(n_slabs):
                nat2_ref[c, pl.ds(0, pairs, stride=2), :] = tb2_ref[c, pl.ds(b, pairs, stride=batch), :]
                nat2_ref[c, pl.ds(1, pairs, stride=2), :] = tb2_ref[c, pl.ds(rows_k + b, pairs, stride=batch), :]
            o_ref[b] = xb_ref[b] + _slabs(nat2_ref)

    for c in range(n_slabs):
        up_ref[c, 0:batch, :] = up_ref[c, rows_k:rows_k + batch, :]
    for b in range(batch):
        u_nat = _rms(xa_ref[b], gn_ref[...])
        for c in range(n_slabs):
            nat_ref[c] = u_nat[:, LANES * c:LANES * (c + 1)]
        for c in range(n_slabs):
            even = nat_ref[c, pl.ds(0, pairs, stride=2), :]
            odd = nat_ref[c, pl.ds(1, pairs, stride=2), :]
            ue_ref[c, pl.ds(b, pairs, stride=batch), :] = even
            uo_ref[c, pl.ds(b, pairs, stride=batch), :] = odd
            up_ref[c, pl.ds(batch + b, pairs, stride=batch), :] = odd
    ue, uo = _slabs(ue_ref), _slabs(uo_ref)
    ue_b, uo_b, up_b = ue.astype(BF16), uo.astype(BF16), _slabs(up_ref, slice(0, rows_k)).astype(BF16)

    lo = lax.broadcasted_iota(jnp.int32, (SUBLANES, LANES), 0) < batch
    swap = lambda v: pltpu.roll(v, batch, 0)
    n_vregs = rows_k // SUBLANES
    half = OCTET_STATE // LANES
    y_even, y_odd = [], []
    for q in range(N_OCTETS):
        ch = slice(LANES * q, LANES * (q + 1))
        lhs = jnp.concatenate([up_b[:, ch], ue_b[:, ch]], axis=1)
        st_ref[:, 2 * OCTET_STATE * q:2 * OCTET_STATE * (q + 1)] = _dot(lhs, b2_ref[q])

    def project_out(octets):
        for q in octets:
            ch = slice(LANES * q, LANES * (q + 1))
            y2 = _dot(st_ref[:, 2 * OCTET_STATE * q:2 * OCTET_STATE * (q + 1)].astype(BF16), c2_ref[q])
            y_even.append(y2[:, :LANES])
            y_odd.append(y2[:, LANES:] + _dot(uo_b[:, ch], m0_ref[q]))

    for q0 in range(0, N_OCTETS, 2):
        octets = (q0, q0 + 1)
        glu_part(q0 // 2)
        if q0 > 0:
            project_out((q0 - 2, q0 - 1))
        if q0 == N_OCTETS - 2:
            finish_prev()
        chains = []
        for q in octets:
            for j in range(half // 2):
                cols = [2 * OCTET_STATE * q + LANES * jj for jj in (j, j + half // 2)]
                lanes = [OCTET_STATE * q + LANES * jj for jj in (j, j + half // 2)]
                chains.append((cols, lanes))
        lam_r, lam_i, sr, si, old = [], [], [], [], []
        for (ca, cb), (la, lb) in chains:
            bc = lambda r, c: jnp.broadcast_to(lam_ref[r:r + 1, c:c + LANES], (SUBLANES, LANES))
            lam_r.append(jnp.where(lo, bc(2, la), bc(2, lb)))
            lam_i.append(jnp.where(lo, bc(3, la), bc(3, lb)))
            prev = [carry_ref[:, c:c + LANES] for c in (ca, cb, ca + OCTET_STATE, cb + OCTET_STATE)]
            old.append(prev)
            sr.append(jnp.where(lo, swap(prev[0]), prev[1]))
            si.append(jnp.where(lo, swap(prev[2]), prev[3]))
        last = [None] * len(chains)
        for m in range(n_vregs):
            rows = slice(SUBLANES * m, SUBLANES * (m + 1))
            for j, ((ca, cb), _) in enumerate(chains):
                ar, br = st_ref[rows, ca:ca + LANES], st_ref[rows, cb:cb + LANES]
                ai, bi = (st_ref[rows, ca + OCTET_STATE:ca + OCTET_STATE + LANES],
                          st_ref[rows, cb + OCTET_STATE:cb + OCTET_STATE + LANES])
                d0r, d1r = jnp.where(lo, ar, swap(br)), jnp.where(lo, swap(ar), br)
                d0i, d1i = jnp.where(lo, ai, swap(bi)), jnp.where(lo, swap(ai), bi)
                s0r = lam_r[j] * sr[j] - lam_i[j] * si[j] + d0r
                s0i = lam_r[j] * si[j] + lam_i[j] * sr[j] + d0i
                s1r = lam_r[j] * s0r - lam_i[j] * s0i + d1r
                s1i = lam_r[j] * s0i + lam_i[j] * s0r + d1i
                sr[j], si[j] = s1r, s1i
                out = (jnp.where(lo, s0r, swap(s1r)), jnp.where(lo, swap(s0r), s1r),
                       jnp.where(lo, s0i, swap(s1i)), jnp.where(lo, swap(s0i), s1i))
                st_ref[rows, ca:ca + LANES] = out[0]
                st_ref[rows, cb:cb + LANES] = out[1]
                st_ref[rows, ca + OCTET_STATE:ca + OCTET_STATE + LANES] = out[2]
                st_ref[rows, cb + OCTET_STATE:cb + OCTET_STATE + LANES] = out[3]
                last[j] = out
        for j, ((ca, cb), _) in enumerate(chains):
            for c, new, prev in zip((ca, cb, ca + OCTET_STATE, cb + OCTET_STATE), last[j], old[j]):
                carry_ref[:, c:c + LANES] = jnp.where(live, new, prev)
    project_out((N_OCTETS - 2, N_OCTETS - 1))
    y_ref[0:rows_k, :] = jnp.concatenate(y_even, axis=-1) + d_ref[...] * ue
    y_ref[rows_k:2 * rows_k, :] = jnp.concatenate(y_odd, axis=-1) + d_ref[...] * uo

    @pl.when(step == n_chunks - 1)
    def _():
        tail = uo_b[rows_k - 2 * SUBLANES:, :]
        zeros = jnp.zeros((2 * SUBLANES, LANES), BF16)
        for q in range(N_OCTETS):
            bu = _dot(jnp.concatenate([zeros, tail[:, LANES * q:LANES * (q + 1)]], axis=1), b2_ref[q])[SUBLANES:]
            base = 2 * OCTET_STATE * q
            lr = lam_ref[0:1, OCTET_STATE * q:OCTET_STATE * (q + 1)]
            li = lam_ref[1:2, OCTET_STATE * q:OCTET_STATE * (q + 1)]
            s_r = carry_ref[:, base:base + OCTET_STATE]
            s_i = carry_ref[:, base + OCTET_STATE:base + 2 * OCTET_STATE]
            fin_ref[:, base:base + OCTET_STATE] = lr * s_r - li * s_i + bu[:, :OCTET_STATE]
            fin_ref[:, base + OCTET_STATE:base + 2 * OCTET_STATE] = lr * s_i + li * s_r + bu[:, OCTET_STATE:]


def _s5_prompt(x, layer, gn, b2, lam, c2, m0, d, wglu, bglu):
    batch, seq, _ = x.shape
    tc = S5_TIME_CHUNK
    n_chunks = seq // tc
    rows_k = tc // 2 * batch
    n_slabs = D_MODEL // LANES
    blk = (batch, tc, D_MODEL)
    this_chunk = pl.BlockSpec(blk, lambda s: (0, jnp.minimum(s, n_chunks - 1), 0))
    prev_chunk = pl.BlockSpec(blk, lambda s: (0, jnp.maximum(s - 1, 0), 0))
    slab = lambda rows: pltpu.VMEM((n_slabs, rows, LANES), F32)
    return pl.pallas_call(
        _s5_prompt_kernel,
        grid=(n_chunks + 1,),
        in_specs=[this_chunk, prev_chunk]
        + [_layer_spec(w.shape, layer) for w in (gn, b2, lam, c2, m0, d, wglu, bglu)],
        out_specs=[prev_chunk, pl.BlockSpec((SUBLANES, STATE_LANES), lambda s: (0, 0))],
        out_shape=[jax.ShapeDtypeStruct(x.shape, F32),
                   jax.ShapeDtypeStruct((SUBLANES, STATE_LANES), F32)],
        scratch_shapes=[pltpu.VMEM((rows_k, STATE_LANES), F32),
                        pltpu.VMEM((SUBLANES, STATE_LANES), F32),
                        slab(tc), slab(rows_k), slab(rows_k), slab(rows_k + SUBLANES),
                        slab(2 * rows_k), slab(tc),
                        pltpu.VMEM((2 * rows_k, D_MODEL), F32)],
        compiler_params=_params(("arbitrary",)),
        name="s5_prompt",
    )(x, x, gn, b2, lam, c2, m0, d, wglu, bglu)


def _s5_sample_kernel(x_ref, gn_ref, b2_ref, lam_ref, c2_ref, d_ref, wglu_ref, bglu_ref, s0_ref,
                      o_ref, snew_ref):
    x = x_ref[...]
    u = _rms(x, gn_ref[...])
    ys = []
    for q in range(N_OCTETS):
        base = 2 * OCTET_STATE * q
        bu = jnp.dot(u[:, LANES * q:LANES * (q + 1)], b2_ref[q, LANES:, :],
                     preferred_element_type=F32, precision=lax.Precision.HIGHEST)
        lr = lam_ref[0:1, OCTET_STATE * q:OCTET_STATE * (q + 1)]
        li = lam_ref[1:2, OCTET_STATE * q:OCTET_STATE * (q + 1)]
        s0r = s0_ref[:, base:base + OCTET_STATE]
        s0i = s0_ref[:, base + OCTET_STATE:base + 2 * OCTET_STATE]
        sr = lr * s0r - li * s0i + bu[:, :OCTET_STATE]
        si = lr * s0i + li * s0r + bu[:, OCTET_STATE:]
        snew_ref[:, base:base + OCTET_STATE] = sr
        snew_ref[:, base + OCTET_STATE:base + 2 * OCTET_STATE] = si
        sb = jnp.concatenate([sr, si], axis=-1).astype(BF16)
        ys.append(_dot(sb, c2_ref[q, :, :LANES]))
    y = jnp.concatenate(ys, axis=-1)
    o_ref[...] = x + _s5_gate(y + d_ref[...] * u, wglu_ref, bglu_ref)


def _s5_sample(x, layer, gn, b2_f32, lam, c2, d, wglu, bglu, s0):
    rows = x.shape[0]
    return pl.pallas_call(
        _s5_sample_kernel,
        grid=(1,),
        in_specs=[_const_spec(x.shape)] + [_layer_spec(w.shape, layer) for w in (gn, b2_f32, lam, c2, d, wglu, bglu)]
        + [_const_spec(s0.shape)],
        out_specs=[pl.BlockSpec((rows, D_MODEL), lambda i: (0, 0)),
                   pl.BlockSpec((rows, STATE_LANES), lambda i: (0, 0))],
        out_shape=[jax.ShapeDtypeStruct((rows, D_MODEL), F32),
                   jax.ShapeDtypeStruct((rows, STATE_LANES), F32)],
        compiler_params=_params(("arbitrary",)),
        name="s5_sample",
    )(x, gn, b2_f32, lam, c2, d, wglu, bglu, s0)


def _block_diag(w):
    n, octets, groups, r, c = w.shape
    col = jnp.arange(groups * c)
    tile = (jnp.arange(c)[:, None] == col % c).astype(F32)
    own = (jnp.arange(groups * r)[:, None] // r == col // c).astype(F32)
    tiled = jnp.einsum("nqrc,cl->nqrl", w.reshape(n, octets, groups * r, c), tile, precision=lax.Precision.HIGHEST)
    return tiled * own


def _s5_weights(a_re, a_im, log_dt, b_re, b_im, c_re, c_im):
    n = a_re.shape[0]
    dt = jnp.exp(log_dt.astype(F32))
    lam = lax.complex(a_re.astype(F32), a_im.astype(F32))
    lam_bar = jnp.exp(lam * dt)
    b = lax.complex(b_re.astype(F32), b_im.astype(F32))
    b_bar = ((lam_bar - 1.0) / lam)[..., None] * b
    c = lax.complex(c_re.astype(F32), c_im.astype(F32))
    gpo = N_GROUPS // N_OCTETS
    octets = lambda w: w.reshape((n, N_OCTETS, gpo) + w.shape[2:])

    def into_state(w):
        wt = jnp.swapaxes(octets(w), 3, 4)
        return jnp.concatenate([_block_diag(jnp.real(wt)), _block_diag(jnp.imag(wt))], axis=-1)

    def out_of_state(w):
        wt = jnp.swapaxes(octets(w), 3, 4)
        return jnp.concatenate([_block_diag(jnp.real(wt)), -_block_diag(jnp.imag(wt))], axis=2)

    b2 = jnp.concatenate([into_state(lam_bar[..., None] * b_bar), into_state(b_bar)], axis=2)
    c2 = jnp.concatenate([out_of_state(c), out_of_state(c * lam_bar[:, :, None, :])], axis=-1)
    cb = jnp.real(jnp.einsum("ngcp,ngpd->ngdc", c, b_bar))
    m0 = _block_diag(octets(cb))
    lam2 = lam_bar * lam_bar
    lam_rows = jnp.stack([jnp.real(lam_bar).reshape(n, -1), jnp.imag(lam_bar).reshape(n, -1),
                          jnp.real(lam2).reshape(n, -1), jnp.imag(lam2).reshape(n, -1)], axis=1)
    return b2, c2.astype(BF16), m0.astype(BF16), lam_rows


def _state_to_lanes(s_re, s_im):
    n = s_re.shape[0]
    re = s_re.astype(F32).reshape(n, N_OCTETS, OCTET_STATE)
    im = s_im.astype(F32).reshape(n, N_OCTETS, OCTET_STATE)
    return jnp.concatenate([re, im], axis=-1).reshape(n, STATE_LANES)


def _lanes_to_state(s):
    n = s.shape[0]
    s4 = s.reshape(n, N_OCTETS, 2, OCTET_STATE)
    return (s4[:, :, 0].reshape(n, N_GROUPS, STATE_DIM), s4[:, :, 1].reshape(n, N_GROUPS, STATE_DIM))


def _rope_partner(n):
    lane = jnp.arange(n)
    return jnp.where(lane % HEAD_DIM < HEAD_DIM // 2, lane + HEAD_DIM // 2, lane - HEAD_DIM // 2)


def _kv_kernel(x_ref, g_ref, wkv_ref, kg_ref, kgp_ref, ones_ref, cos_ref, sin_ref, *rest):
    if len(rest) == 3:
        wvt_ref, k_ref, vt_ref = rest
        v_ref = None
    else:
        (k_ref, v_ref), wvt_ref, vt_ref = rest, None, None
    h = _rms(x_ref[...], g_ref[...]).astype(BF16)
    kv = _dot(h, wkv_ref[...])
    n = k_ref.shape[-1]
    k, k_partner = kv[:, :n], kv[:, n:2 * n]
    scale = lax.rsqrt(_seg_mean_sq(k, ones_ref) + EPS)
    k_ref[...] = scale * (k * (kg_ref[...] * _tile_lanes(cos_ref[...], n))
                          + k_partner * (kgp_ref[...] * _tile_lanes(sin_ref[...], n)))
    if v_ref is not None:
        v_ref[...] = kv[:, 2 * n:]
    else:
        vt_ref[...] = lax.dot_general(wvt_ref[...], h, (((1,), (1,)), ((), ())), preferred_element_type=F32)


def _kv(x, g, wk, wv, kgain, ones, cos, sin, table_block, seq=None):
    rows = x.shape[0]
    n, nv = wk.shape[1], wv.shape[1]
    partner = _rope_partner(n)
    wkv = jnp.concatenate([wk, wk[:, partner]] + ([wv] if seq is None else []), axis=1).astype(BF16)
    tm = min(KV_ROWS, rows)
    in_specs = [pl.BlockSpec((tm, D_MODEL), lambda i: (i, 0)),
                _const_spec((1, D_MODEL)),
                _const_spec(wkv.shape),
                _const_spec((1, n)),
                _const_spec((1, n)),
                _const_spec(ones.shape),
                pl.BlockSpec((tm, LANES), lambda i: (table_block(i), 0)),
                pl.BlockSpec((tm, LANES), lambda i: (table_block(i), 0))]
    out_specs = [pl.BlockSpec((tm, n), lambda i: (i, 0))]
    out_shape = [jax.ShapeDtypeStruct((rows, n), F32)]
    args = [x, g, wkv, kgain, kgain[:, partner], ones, cos, sin]
    if seq is None:
        out_specs.append(pl.BlockSpec((tm, nv), lambda i: (i, 0)))
        out_shape.append(jax.ShapeDtypeStruct((rows, nv), F32))
    else:
        per_seq = seq // tm
        in_specs.append(_const_spec((nv, D_MODEL)))
        args.append(wv.T.astype(BF16))
        out_specs.append(pl.BlockSpec((None, nv, tm), lambda i: (i // per_seq, 0, i % per_seq)))
        out_shape.append(jax.ShapeDtypeStruct((rows // seq, nv, seq), F32))
    return pl.pallas_call(
        _kv_kernel,
        grid=(rows // tm,),
        in_specs=in_specs,
        out_specs=out_specs,
        out_shape=out_shape,
        compiler_params=_params(("parallel",)),
        name="kv_proj",
    )(*args)


def _attn_prompt_kernel(sinks_ref, x_ref, g_ref, wq_ref, qg_ref, ones_ref, cos_ref, sin_ref,
                        kp_ref, kc_ref, vp_ref, vc_ref, wo_ref, o_ref):
    tile = pl.program_id(1)
    x = x_ref[...]
    n_blocks = x.shape[0] // WINDOW
    heads_per_kv = N_HEADS // N_KV_HEADS
    cols = heads_per_kv * WINDOW
    h = _rms(x, g_ref[...]).astype(BF16)
    q = _dot(h, wq_ref[...])
    q = q * lax.rsqrt(_seg_mean_sq(q, ones_ref) + EPS) * qg_ref[...]
    q = _rope(q, _tile_lanes(cos_ref[...], D_MODEL), _tile_lanes(sin_ref[...], D_MODEL))
    qb = (q * (HEAD_DIM ** -0.5)).astype(BF16)

    kall = jnp.concatenate([kp_ref[...], kc_ref[...]], axis=0).astype(BF16)
    vall = jnp.concatenate([vp_ref[...], vc_ref[...]], axis=1).astype(BF16)

    kj = lax.broadcasted_iota(jnp.int32, (WINDOW, cols), 0)
    col = lax.broadcasted_iota(jnp.int32, (WINDOW, cols), 1)
    from_prev = kj > (col & (WINDOW - 1))
    col_head = lax.broadcasted_iota(jnp.int32, (1, cols), 1) // WINDOW
    low_half = lax.broadcasted_iota(jnp.int32, (WINDOW, LANES), 1) < HEAD_DIM
    neg_inf = jnp.full((WINDOW, cols), -jnp.inf, F32)

    pairs = [(blk, g) for blk in range(n_blocks) for g in range(N_KV_HEADS)]
    scores, sinks = [], []
    for blk, g in pairs:
        r0 = WINDOW * blk
        kg = kall[r0:r0 + 2 * WINDOW, LANES * g:LANES * (g + 1)]
        qs = []
        sink = jnp.zeros((1, cols), F32)
        for hh in range(heads_per_kv):
            head = heads_per_kv * g + hh
            qp = qb[r0:r0 + WINDOW, LANES * (head // 2):LANES * (head // 2 + 1)]
            keep = low_half if head % 2 == 0 else jnp.logical_not(low_half)
            qs.append(jnp.where(keep, qp, jnp.zeros_like(qp)))
            sink = jnp.where(col_head == hh, sinks_ref[head], sink)
        sinks.append(sink)
        scores.append(lax.dot_general(kg, jnp.concatenate(qs, axis=0), (((1,), (1,)), ((), ())),
                                      preferred_element_type=F32))
    probs, denoms = [], []
    for (blk, g), s, sink in zip(pairs, scores, sinks):
        has_prev = tile * n_blocks + blk > 0
        t = jnp.where(from_prev, jnp.where(has_prev, s[:WINDOW], neg_inf), s[WINDOW:])
        m = jnp.maximum(jnp.max(t, axis=0, keepdims=True), sink)
        p = jnp.exp(t - m)
        denoms.append(jnp.sum(p, axis=0, keepdims=True) + jnp.exp(sink - m))
        zero = jnp.zeros_like(p)
        probs.append(jnp.concatenate([jnp.where(from_prev, p, zero), jnp.where(from_prev, zero, p)],
                                     axis=0).astype(BF16))
    head_outs = [[] for _ in range(n_blocks)]
    for (blk, g), pcat, denom in zip(pairs, probs, denoms):
        r0 = WINDOW * blk
        vg = vall[HEAD_DIM * g:HEAD_DIM * (g + 1), r0:r0 + 2 * WINDOW]
        o = _dot(vg, pcat) / denom
        head_outs[blk] += [o[:, WINDOW * hh:WINDOW * (hh + 1)] for hh in range(heads_per_kv)]
    block_outs = [jnp.concatenate(outs, axis=0).T for outs in head_outs]
    attn = jnp.concatenate(block_outs, axis=0).astype(BF16)
    o_ref[...] = x + _dot(attn, wo_ref[...])


def _attn_prompt(x, g, wq, qgain, ones, cos, sin, kd, vt, sinks, wo):
    bsz, seq, _ = x.shape
    nk, nv = kd.shape[-1], vt.shape[1]
    nb = ATTN_BLOCKS
    tq = nb * WINDOW
    x_spec = pl.BlockSpec((None, tq, D_MODEL), lambda b, i, *_: (b, i, 0))
    k_cur = pl.BlockSpec((None, tq, nk), lambda b, i, *_: (b, i, 0))
    k_prev = pl.BlockSpec((None, WINDOW, nk), lambda b, i, *_: (b, jnp.maximum(nb * i - 1, 0), 0))
    v_cur = pl.BlockSpec((None, nv, tq), lambda b, i, *_: (b, 0, i))
    v_prev = pl.BlockSpec((None, nv, WINDOW), lambda b, i, *_: (b, 0, jnp.maximum(nb * i - 1, 0)))
    tab = pl.BlockSpec((tq, LANES), lambda b, i, *_: (i, 0))
    grid_spec = pltpu.PrefetchScalarGridSpec(
        num_scalar_prefetch=1,
        grid=(bsz, seq // tq),
        in_specs=[x_spec,
                  _const_spec((1, D_MODEL)),
                  _const_spec(wq.shape),
                  _const_spec((1, D_MODEL)),
                  _const_spec(ones.shape),
                  tab, tab, k_prev, k_cur, v_prev, v_cur,
                  _const_spec(wo.shape)],
        out_specs=x_spec)
    return pl.pallas_call(
        _attn_prompt_kernel,
        grid_spec=grid_spec,
        out_shape=jax.ShapeDtypeStruct(x.shape, F32),
        compiler_params=_params(("parallel", "arbitrary")),
        name="attn_prompt",
    )(sinks, x, g, wq, qgain, ones, cos, sin, kd, kd, vt, vt, wo)


def _own_head(shape):
    row_head = lax.broadcasted_iota(jnp.int32, shape, 0) & (N_HEADS - 1)
    lane_head = lax.broadcasted_iota(jnp.int32, shape, 1) // HEAD_DIM
    return row_head == lane_head


def _q_sample_kernel(x_ref, g_ref, wq_ref, qg_ref, ones_ref, cos_ref, sin_ref, rep_ref, place_ref, q_ref):
    h = _rms(x_ref[...], g_ref[...]).astype(BF16)
    q = _dot(h, wq_ref[...])
    q = q * lax.rsqrt(_seg_mean_sq(q, ones_ref) + EPS) * qg_ref[...]
    q = _rope(q, _tile_lanes(cos_ref[...], D_MODEL), _tile_lanes(sin_ref[...], D_MODEL))
    qb = (q * (HEAD_DIM ** -0.5)).astype(BF16)
    q_rep = _dot(rep_ref[...], qb)
    q_own = jnp.where(_own_head(q_rep.shape), q_rep, 0.0).astype(BF16)
    q_ref[...] = _dot(q_own, place_ref[...])


def _q_sample(x, g, wq, qgain, ones, cos, sin, rep, place):
    rows = x.shape[0]
    return pl.pallas_call(
        _q_sample_kernel,
        out_shape=jax.ShapeDtypeStruct((rows * N_HEADS, place.shape[1]), F32),
        compiler_params=pltpu.CompilerParams(vmem_limit_bytes=VMEM_LIMIT_BYTES),
        name="q_sample",
    )(x, g, wq, qgain, ones, cos, sin, rep, place)


def _cache_append_kernel(k_ref, v_ref, kn_ref, vn_ref, ko_ref, vo_ref):
    nb, lanes_kv, keys = k_ref.shape
    first = pl.program_id(0) * nb
    key = lax.broadcasted_iota(jnp.int32, (lanes_kv, keys), 1)
    seq = lax.broadcasted_iota(jnp.int32, (kn_ref.shape[1], keys), 0)

    def split3(x):
        hi = x.astype(BF16)
        r = x - hi.astype(F32)
        mid = r.astype(BF16)
        return hi, mid, (r - mid.astype(F32)).astype(BF16)

    parts = [(split3(kn_ref[...]), k_ref, ko_ref), (split3(vn_ref[...]), v_ref, vo_ref)]
    for b in range(nb):
        pick = (seq == first + b).astype(BF16)
        for (hi, mid, lo), old_ref, out_ref in parts:
            new_col = _dot(hi, pick) + _dot(mid, pick) + _dot(lo, pick)
            out_ref[b] = jnp.where(key == keys - 1, new_col, pltpu.roll(old_ref[b], keys - 1, 1))


def _cache_append(kt, vt, kn_t, vn_t):
    bsz, lanes_kv, keys = kt.shape
    nb = 2 * SUBLANES
    blk = pl.BlockSpec((nb, lanes_kv, keys), lambda i: (i, 0, 0))
    return pl.pallas_call(
        _cache_append_kernel,
        grid=(bsz // nb,),
        in_specs=[blk, blk, _const_spec(kn_t.shape), _const_spec(vn_t.shape)],
        out_specs=[blk, blk],
        out_shape=[jax.ShapeDtypeStruct(kt.shape, F32), jax.ShapeDtypeStruct(vt.shape, F32)],
        compiler_params=_params(("parallel",)),
        name="cache_append",
    )(kt, vt, kn_t, vn_t)


def _attn_sample_kernel(q_ref, k_ref, v_ref, sink_ref, o_ref):
    sink = sink_ref[...]
    for b in range(q_ref.shape[0]):
        s = _dot(q_ref[b].astype(BF16), k_ref[b].astype(BF16))
        m = jnp.maximum(jnp.max(s, axis=-1, keepdims=True), sink)
        p = jnp.exp(s - m)
        denom = jnp.sum(p, axis=-1, keepdims=True) + jnp.exp(sink - m)
        o = lax.dot_general(p.astype(BF16), v_ref[b].astype(BF16), (((1,), (1,)), ((), ())),
                            preferred_element_type=F32)
        o_ref[b] = o / denom


def _attn_sample(q3, kt, vt, sink_col):
    bsz, nh, nkv = q3.shape
    keys = kt.shape[2]
    bb = 4 * SUBLANES
    return pl.pallas_call(
        _attn_sample_kernel,
        grid=(bsz // bb,),
        in_specs=[pl.BlockSpec((bb, nh, nkv), lambda i: (i, 0, 0)),
                  pl.BlockSpec((bb, nkv, keys), lambda i: (i, 0, 0)),
                  pl.BlockSpec((bb, nkv, keys), lambda i: (i, 0, 0)),
                  _const_spec(sink_col.shape)],
        out_specs=pl.BlockSpec((bb, nh, nkv), lambda i: (i, 0, 0)),
        out_shape=jax.ShapeDtypeStruct((bsz, nh, nkv), F32),
        compiler_params=_params(("parallel",)),
        name="attn_sample",
    )(q3, kt, vt, sink_col)


def _proj_residual_kernel(x_ref, a_ref, place_t_ref, rep_t_ref, w_ref, o_ref):
    z = _dot(a_ref[...].astype(BF16), place_t_ref[...])
    z_own = jnp.where(_own_head(z.shape), z, 0.0).astype(BF16)
    attn = _dot(rep_t_ref[...], z_own).astype(BF16)
    o_ref[...] = x_ref[...] + _dot(attn, w_ref[...])


def _proj_residual(x, a, place_t, rep_t, w):
    return pl.pallas_call(
        _proj_residual_kernel,
        out_shape=jax.ShapeDtypeStruct(x.shape, F32),
        compiler_params=pltpu.CompilerParams(vmem_limit_bytes=VMEM_LIMIT_BYTES),
        name="o_proj_sample",
    )(x, a, place_t, rep_t, w)


def _rope_tables(pos):
    half = HEAD_DIM // 2
    inv = ROPE_THETA ** (-jnp.arange(half, dtype=F32) / half)
    ang = pos.astype(F32)[:, None] * inv[None, :]
    cos, sin = jnp.cos(ang), jnp.sin(ang)
    reps = LANES // HEAD_DIM
    return (jnp.tile(jnp.concatenate([cos, cos], axis=-1), (1, reps)),
            jnp.tile(jnp.concatenate([-sin, sin], axis=-1), (1, reps)))


def _block_ones(width, block):
    idx = jnp.arange(width) // block
    return (idx[:, None] == idx[None, :]).astype(BF16)


def _dup_heads(w):
    w3 = w.reshape(w.shape[0], N_KV_HEADS, 1, HEAD_DIM)
    return jnp.broadcast_to(w3, (w.shape[0], N_KV_HEADS, LANES // HEAD_DIM, HEAD_DIM)).reshape(w.shape[0], -1)


def _head_placement():
    lane = jnp.arange(N_HEADS * HEAD_DIM)
    dst = (lane // HEAD_DIM) // (N_HEADS // N_KV_HEADS) * HEAD_DIM + lane % HEAD_DIM
    return (dst[:, None] == jnp.arange(N_KV_HEADS * HEAD_DIM)[None, :]).astype(BF16)


def _row_replication(n):
    return (jnp.arange(n * N_HEADS)[:, None] // N_HEADS == jnp.arange(n)[None, :]).astype(BF16)


def kernel(x_prompt, x_sample, state_ssm_re, state_ssm_im, cache_k, cache_v, norm_mix, norm_mlp, ssm_a_re, ssm_a_im, ssm_log_dt, ssm_b_re, ssm_b_im, ssm_c_re, ssm_c_im, ssm_d, w_glu, b_glu, norm_kv, w_k, w_v, k_norm, w_q, q_norm, attn_sinks, w_o, w_mlp_in, w_mlp_out):
    bsz, seq, _ = x_prompt.shape
    dec = x_sample.shape[0]
    n_a = ssm_a_re.shape[0]
    depth = norm_mix.shape[0]
    past = 8192
    nkv = N_KV_HEADS * HEAD_DIM
    row = lambda v: v.astype(F32).reshape(1, -1)

    xp = x_prompt.astype(F32)
    xs = x_sample.reshape(dec, D_MODEL)
    rows3 = lambda v: v.astype(F32).reshape(v.shape[0], 1, -1)

    w1 = w_mlp_in.astype(BF16)
    w2 = w_mlp_out.astype(BF16)
    g_mlp = rows3(norm_mlp)
    def mlp_both(x, xs, layer):
        y, ys = _mlp(x.reshape(bsz * seq, D_MODEL), xs, g_mlp, w1, w2, layer)
        return y.reshape(bsz, seq, D_MODEL), ys


    b2, c2, m0, lam = _s5_weights(ssm_a_re, ssm_a_im, ssm_log_dt, ssm_b_re, ssm_b_im, ssm_c_re, ssm_c_im)
    b2_16 = b2.astype(BF16)
    wglu = w_glu.astype(BF16)
    g_mix, d_skip, bg = rows3(norm_mix), rows3(ssm_d), rows3(b_glu)
    sp_re, sp_im, ss_re, ss_im = [], [], [], []
    for i in range(n_a):
        xp, fin = _s5_prompt(xp, i, g_mix, b2_16, lam, c2, m0, d_skip, wglu, bg)
        re, im = _lanes_to_state(fin[bsz:])
        sp_re.append(re)
        sp_im.append(im)
        s0 = _state_to_lanes(state_ssm_re[i], state_ssm_im[i])
        xs, snew = _s5_sample(xs, i, g_mix, b2, lam, c2, d_skip, wglu, bg, s0)
        re, im = _lanes_to_state(snew)
        ss_re.append(re)
        ss_im.append(im)
        xp, xs = mlp_both(xp, xs, i)

    cos_p, sin_p = _rope_tables(jnp.arange(seq, dtype=jnp.int32))
    cos_s, sin_s = _rope_tables(jnp.full((dec,), past, dtype=jnp.int32))
    ones_head = _block_ones(MXU_TILE, HEAD_DIM)
    place = _head_placement()
    rep = _row_replication(dec)
    kgain = row(k_norm)
    tile_gain = lambda gvec, n: jnp.tile(gvec, (1, n // HEAD_DIM))

    kd, vt = _kv(xp.reshape(bsz * seq, D_MODEL), row(norm_kv), _dup_heads(w_k), w_v,
                     tile_gain(kgain, 2 * nkv), ones_head, cos_p, sin_p, lambda i: i % (seq // KV_ROWS), seq=seq)
    ks_new, vs_new = _kv(xs, row(norm_kv), w_k, w_v, tile_gain(kgain, nkv), ones_head, cos_s, sin_s,
                         lambda i: i)
    kd = kd.reshape(bsz, seq, 2 * nkv)
    new_k_p = kd[:, -WINDOW:].reshape(bsz, WINDOW, N_KV_HEADS, LANES // HEAD_DIM, HEAD_DIM)[:, :, :, 0]
    new_v_p = jnp.transpose(vt[:, :, -WINDOW:].reshape(bsz, N_KV_HEADS, HEAD_DIM, WINDOW), (0, 3, 1, 2))
    key_minor = lambda c: jnp.transpose(c.astype(F32), (0, 2, 3, 1)).reshape(dec, nkv, c.shape[1])
    keys_s, vals_s = _cache_append(key_minor(cache_k), key_minor(cache_v), ks_new.T, vs_new.T)
    key_major = lambda c: jnp.transpose(c.reshape(dec, N_KV_HEADS, HEAD_DIM, c.shape[2]), (0, 3, 1, 2))
    new_k_s, new_v_s = key_major(keys_s), key_major(vals_s)

    for j in range(depth - n_a):
        layer = n_a + j
        gn = row(norm_mix[layer])
        qgain = row(q_norm[j])
        wq = w_q[j].astype(BF16)
        wo = w_o[j].astype(BF16)
        xp = _attn_prompt(xp, gn, wq, tile_gain(qgain, D_MODEL), ones_head, cos_p, sin_p, kd, vt,
                          attn_sinks[j].astype(F32), wo)
        q_pad = _q_sample(xs, gn, wq, tile_gain(qgain, D_MODEL), ones_head, cos_s, sin_s, rep, place)
        o_pad = _attn_sample(q_pad.reshape(dec, N_HEADS, nkv), keys_s, vals_s,
                             attn_sinks[j].astype(F32).reshape(N_HEADS, 1))
        xs = _proj_residual(xs, o_pad.reshape(dec * N_HEADS, nkv), place.T, rep.T, wo)
        xp, xs = mlp_both(xp, xs, layer)

    return (xp, xs.reshape(dec, 1, D_MODEL),
            jnp.stack(sp_re), jnp.stack(sp_im), new_k_p, new_v_p,
            jnp.stack(ss_re), jnp.stack(ss_im), new_k_s, new_v_s)
```

```python
import jax
import jax.numpy as jnp
from jax import lax
from jax.experimental import pallas as pl
from jax.experimental.pallas import tpu as pltpu

F32 = jnp.float32
BF16 = jnp.bfloat16

D_MODEL = 1024
N_GROUPS = 64
GROUP_SIZE = 16
STATE_DIM = 64
HEAD_DIM = 64
N_HEADS = 16
N_KV_HEADS = 4
WINDOW = 128
ROPE_THETA = 10000.0
D_FF = 4 * D_MODEL
EPS = 1e-6

SUBLANES = 8
LANES = 128
MXU_TILE = 256
VMEM_LIMIT_BYTES = 56 * 1024 * 1024

N_OCTETS = D_MODEL // LANES
OCTET_STATE = (LANES // GROUP_SIZE) * STATE_DIM
STATE_LANES = 2 * OCTET_STATE * N_OCTETS

S5_TIME_CHUNK = 128
MLP_ROWS = 1024
FF_CHUNK = 1024
KV_ROWS = 2048
ATTN_BLOCKS = 8


def _const_spec(shape):
    zeros = (0,) * len(shape)
    return pl.BlockSpec(shape, lambda *_: zeros, pipeline_mode=pl.Buffered(1))


def _layer_spec(shape, layer):
    idx = (layer,) + (0,) * (len(shape) - 1)
    return pl.BlockSpec((None,) + tuple(shape[1:]), lambda *_: idx, pipeline_mode=pl.Buffered(1))


def _params(sem):
    return pltpu.CompilerParams(dimension_semantics=sem, vmem_limit_bytes=VMEM_LIMIT_BYTES)


def _rms(x, g):
    return x * lax.rsqrt(jnp.mean(x * x, axis=-1, keepdims=True) + EPS) * g


def _dot(a, b):
    return jnp.dot(a, b, preferred_element_type=F32)


def _seg_mean_sq(x, ones_ref):
    sq = x * x
    hi = sq.astype(BF16)
    lo = (sq - hi.astype(F32)).astype(BF16)
    ones = ones_ref[...]
    outs = []
    for t in range(x.shape[-1] // MXU_TILE):
        sl = slice(MXU_TILE * t, MXU_TILE * (t + 1))
        outs.append(_dot(hi[:, sl], ones) + _dot(lo[:, sl], ones))
    return jnp.concatenate(outs, axis=-1) * (1.0 / HEAD_DIM)


def _tile_lanes(t, n):
    return jnp.concatenate([t] * (n // t.shape[-1]), axis=-1)


def _rope(x, cos, sin_signed):
    n = x.shape[-1]
    lane = lax.broadcasted_iota(jnp.int32, x.shape, 1)
    first = (lane & (HEAD_DIM - 1)) < (HEAD_DIM // 2)
    partner = jnp.where(first, pltpu.roll(x, n - HEAD_DIM // 2, 1), pltpu.roll(x, HEAD_DIM // 2, 1))
    return x * cos + partner * sin_signed


def _mlp_rows(x, g_ref, w1_ref, w2_ref):
    h = _rms(x, g_ref[...]).astype(BF16)
    acc = x
    for c in range(D_FF // FF_CHUNK):
        sl = slice(FF_CHUNK * c, FF_CHUNK * (c + 1))
        a = jnp.square(jnp.maximum(_dot(h, w1_ref[:, sl]), 0.0)).astype(BF16)
        acc = acc + _dot(a, w2_ref[sl, :])
    return acc


def _mlp_kernel(x_ref, xs_ref, g_ref, w1_ref, w2_ref, o_ref, os_ref):
    last = pl.num_programs(0) - 1

    @pl.when(pl.program_id(0) < last)
    def _():
        o_ref[...] = _mlp_rows(x_ref[...], g_ref, w1_ref, w2_ref)

    @pl.when(pl.program_id(0) == last)
    def _():
        os_ref[...] = _mlp_rows(xs_ref[...], g_ref, w1_ref, w2_ref)


def _mlp(x, xs, g, w1, w2, layer):
    rows = x.shape[0]
    tm = MLP_ROWS
    tiles = rows // tm
    x_spec = pl.BlockSpec((tm, D_MODEL), lambda i: (jnp.minimum(i, tiles - 1), 0))
    xs_spec = pl.BlockSpec(xs.shape, lambda i: (0, 0))
    return pl.pallas_call(
        _mlp_kernel,
        grid=(tiles + 1,),
        in_specs=[x_spec, xs_spec,
                  _layer_spec(g.shape, layer),
                  _layer_spec(w1.shape, layer),
                  _layer_spec(w2.shape, layer)],
        out_specs=[x_spec, xs_spec],
        out_shape=[jax.ShapeDtypeStruct((rows, D_MODEL), F32), jax.ShapeDtypeStruct(xs.shape, F32)],
        compiler_params=_params(("arbitrary",)),
        name="mlp",
    )(x, xs, g, w1, w2)


def _s5_gate(y, wglu_ref, bglu_ref):
    g = jax.nn.gelu(y).astype(BF16)
    z = _dot(g, wglu_ref[...]) + bglu_ref[...]
    return z[:, :D_MODEL] * jax.nn.sigmoid(z[:, D_MODEL:])


def _slabs(ref, rows=slice(None)):
    return jnp.concatenate([ref[c, rows, :] for c in range(ref.shape[0])], axis=-1)


def _s5_prompt_kernel(xa_ref, xb_ref, gn_ref, b2_ref, lam_ref, c2_ref, m0_ref, d_ref, wglu_ref, bglu_ref,
                      o_ref, fin_ref, st_ref, carry_ref, nat_ref, ue_ref, uo_ref, up_ref, tb2_ref, nat2_ref, y_ref):
    batch, tc, _ = xa_ref.shape
    assert 2 * batch == SUBLANES
    n_slabs = D_MODEL // LANES
    pairs = tc // 2
    rows_k = pairs * batch
    step = pl.program_id(0)
    n_chunks = pl.num_programs(0) - 1
    live = step < n_chunks

    @pl.when(step == 0)
    def _():
        carry_ref[...] = jnp.zeros_like(carry_ref)
        y_ref[...] = jnp.zeros_like(y_ref)
        up_ref[...] = jnp.zeros_like(up_ref)

    g_prev = jax.nn.gelu(y_ref[...]).astype(BF16)
    glu_cols = 2 * D_MODEL // (N_OCTETS // 2)
    z = []

    def glu_part(c):
        sl = slice(glu_cols * c, glu_cols * (c + 1))
        z.append(_dot(g_prev, wglu_ref[:, sl]) + bglu_ref[:, sl])

    def finish_prev():
        zs = jnp.concatenate(z, axis=-1)
        upd = zs[:, :D_MODEL] * jax.nn.sigmoid(zs[:, D_MODEL:])
        for c in range(n_slabs):
            tb2_ref[c] = upd[:, LANES * c:LANES * (c + 1)]
        for b in range(batch):
            for c in range(n_slabs):
                nat2_ref[c, pl.ds(0, pairs, stride=2), :] = tb2_ref[c, pl.ds(b, pairs, stride=batch), :]
                nat2_ref[c, pl.ds(1, pairs, stride=2), :] = tb2_ref[c, pl.ds(rows_k + b, pairs, stride=batch), :]
            o_ref[b] = xb_ref[b] + _slabs(nat2_ref)

    for c in range(n_slabs):
        up_ref[c, 0:batch, :] = up_ref[c, rows_k:rows_k + batch, :]
    for b in range(batch):
        u_nat = _rms(xa_ref[b], gn_ref[...])
        for c in range(n_slabs):
            nat_ref[c] = u_nat[:, LANES * c:LANES * (c + 1)]
        for c in range(n_slabs):
            even = nat_ref[c, pl.ds(0, pairs, stride=2), :]
            odd = nat_ref[c, pl.ds(1, pairs, stride=2), :]
            ue_ref[c, pl.ds(b, pairs, stride=batch), :] = even
            uo_ref[c, pl.ds(b, pairs, stride=batch), :] = odd
            up_ref[c, pl.ds(batch + b, pairs, stride=batch), :] = odd
    ue, uo = _slabs(ue_ref), _slabs(uo_ref)
    ue_b, uo_b, up_b = ue.astype(BF16), uo.astype(BF16), _slabs(up_ref, slice(0, rows_k)).astype(BF16)

    lo = lax.broadcasted_iota(jnp.int32, (SUBLANES, LANES), 0) < batch
    swap = lambda v: pltpu.roll(v, batch, 0)
    n_vregs = rows_k // SUBLANES
    half = OCTET_STATE // LANES
    y_even, y_odd = [], []
    for q in range(N_OCTETS):
        ch = slice(LANES * q, LANES * (q + 1))
        lhs = jnp.concatenate([up_b[:, ch], ue_b[:, ch]], axis=1)
        st_ref[:, 2 * OCTET_STATE * q:2 * OCTET_STATE * (q + 1)] = _dot(lhs, b2_ref[q])

    def project_out(octets):
        for q in octets:
            ch = slice(LANES * q, LANES * (q + 1))
            y2 = _dot(st_ref[:, 2 * OCTET_STATE * q:2 * OCTET_STATE * (q + 1)].astype(BF16), c2_ref[q])
            y_even.append(y2[:, :LANES])
            y_odd.append(y2[:, LANES:] + _dot(uo_b[:, ch], m0_ref[q]))

    for q0 in range(0, N_OCTETS, 2):
        octets = (q0, q0 + 1)
        glu_part(q0 // 2)
        if q0 > 0:
            project_out((q0 - 2, q0 - 1))
        if q0 == N_OCTETS - 2:
            finish_prev()
        chains = []
        for q in octets:
            for j in range(half // 2):
                cols = [2 * OCTET_STATE * q + LANES * jj for jj in (j, j + half // 2)]
                lanes = [OCTET_STATE * q + LANES * jj for jj in (j, j + half // 2)]
                chains.append((cols, lanes))
        lam_r, lam_i, sr, si, old = [], [], [], [], []
        for (ca, cb), (la, lb) in chains:
            bc = lambda r, c: jnp.broadcast_to(lam_ref[r:r + 1, c:c + LANES], (SUBLANES, LANES))
            lam_r.append(jnp.where(lo, bc(2, la), bc(2, lb)))
            lam_i.append(jnp.where(lo, bc(3, la), bc(3, lb)))
            prev = [carry_ref[:, c:c + LANES] for c in (ca, cb, ca + OCTET_STATE, cb + OCTET_STATE)]
            old.append(prev)
            sr.append(jnp.where(lo, swap(prev[0]), prev[1]))
            si.append(jnp.where(lo, swap(prev[2]), prev[3]))
        last = [None] * len(chains)
        for m in range(n_vregs):
            rows = slice(SUBLANES * m, SUBLANES * (m + 1))
            for j, ((ca, cb), _) in enumerate(chains):
                ar, br = st_ref[rows, ca:ca + LANES], st_ref[rows, cb:cb + LANES]
                ai, bi = (st_ref[rows, ca + OCTET_STATE:ca + OCTET_STATE + LANES],
                          st_ref[rows, cb + OCTET_STATE:cb + OCTET_STATE + LANES])
                d0r, d1r = jnp.where(lo, ar, swap(br)), jnp.where(lo, swap(ar), br)
                d0i, d1i = jnp.where(lo, ai, swap(bi)), jnp.where(lo, swap(ai), bi)
                s0r = lam_r[j] * sr[j] - lam_i[j] * si[j] + d0r
                s0i = lam_r[j] * si[j] + lam_i[j] * sr[j] + d0i
                s1r = lam_r[j] * s0r - lam_i[j] * s0i + d1r
                s1i = lam_r[j] * s0i + lam_i[j] * s0r + d1i
                sr[j], si[j] = s1r, s1i
                out = (jnp.where(lo, s0r, swap(s1r)), jnp.where(lo, swap(s0r), s1r),
                       jnp.where(lo, s0i, swap(s1i)), jnp.where(lo, swap(s0i), s1i))
                st_ref[rows, ca:ca + LANES] = out[0]
                st_ref[rows, cb:cb + LANES] = out[1]
                st_ref[rows, ca + OCTET_STATE:ca + OCTET_STATE + LANES] = out[2]
                st_ref[rows, cb + OCTET_STATE:cb + OCTET_STATE + LANES] = out[3]
                last[j] = out
        for j, ((ca, cb), _) in enumerate(chains):
            for c, new, prev in zip((ca, cb, ca + OCTET_STATE, cb + OCTET_STATE), last[j], old[j]):
                carry_ref[:, c:c + LANES] = jnp.where(live, new, prev)
    project_out((N_OCTETS - 2, N_OCTETS - 1))
    y_ref[0:rows_k, :] = jnp.concatenate(y_even, axis=-1) + d_ref[...] * ue
    y_ref[rows_k:2 * rows_k, :] = jnp.concatenate(y_odd, axis=-1) + d_ref[...] * uo

    @pl.when(step == n_chunks - 1)
    def _():
        tail = uo_b[rows_k - 2 * SUBLANES:, :]
        zeros = jnp.zeros((2 * SUBLANES, LANES), BF16)
        for q in range(N_OCTETS):
            bu = _dot(jnp.concatenate([zeros, tail[:, LANES * q:LANES * (q + 1)]], axis=1), b2_ref[q])[SUBLANES:]
            base = 2 * OCTET_STATE * q
            lr = lam_ref[0:1, OCTET_STATE * q:OCTET_STATE * (q + 1)]
            li = lam_ref[1:2, OCTET_STATE * q:OCTET_STATE * (q + 1)]
            s_r = carry_ref[:, base:base + OCTET_STATE]
            s_i = carry_ref[:, base + OCTET_STATE:base + 2 * OCTET_STATE]
            fin_ref[:, base:base + OCTET_STATE] = lr * s_r - li * s_i + bu[:, :OCTET_STATE]
            fin_ref[:, base + OCTET_STATE:base + 2 * OCTET_STATE] = lr * s_i + li * s_r + bu[:, OCTET_STATE:]


def _s5_prompt(x, layer, gn, b2, lam, c2, m0, d, wglu, bglu):
    batch, seq, _ = x.shape
    tc = S5_TIME_CHUNK
    n_chunks = seq // tc
    rows_k = tc // 2 * batch
    n_slabs = D_MODEL // LANES
    blk = (batch, tc, D_MODEL)
    this_chunk = pl.BlockSpec(blk, lambda s: (0, jnp.minimum(s, n_chunks - 1), 0))
    prev_chunk = pl.BlockSpec(blk, lambda s: (0, jnp.maximum(s - 1, 0), 0))
    slab = lambda rows: pltpu.VMEM((n_slabs, rows, LANES), F32)
    return pl.pallas_call(
        _s5_prompt_kernel,
        grid=(n_chunks + 1,),
        in_specs=[this_chunk, prev_chunk]
        + [_layer_spec(w.shape, layer) for w in (gn, b2, lam, c2, m0, d, wglu, bglu)],
        out_specs=[prev_chunk, pl.BlockSpec((SUBLANES, STATE_LANES), lambda s: (0, 0))],
        out_shape=[jax.ShapeDtypeStruct(x.shape, F32),
                   jax.ShapeDtypeStruct((SUBLANES, STATE_LANES), F32)],
        scratch_shapes=[pltpu.VMEM((rows_k, STATE_LANES), F32),
                        pltpu.VMEM((SUBLANES, STATE_LANES), F32),
                        slab(tc), slab(rows_k), slab(rows_k), slab(rows_k + SUBLANES),
                        slab(2 * rows_k), slab(tc),
                        pltpu.VMEM((2 * rows_k, D_MODEL), F32)],
        compiler_params=_params(("arbitrary",)),
        name="s5_prompt",
    )(x, x, gn, b2, lam, c2, m0, d, wglu, bglu)


def _s5_sample_kernel(x_ref, gn_ref, b2_ref, lam_ref, c2_ref, d_ref, wglu_ref, bglu_ref, s0_ref,
                      o_ref, snew_ref):
    x = x_ref[...]
    u = _rms(x, gn_ref[...])
    ys = []
    for q in range(N_OCTETS):
        base = 2 * OCTET_STATE * q
        bu = jnp.dot(u[:, LANES * q:LANES * (q + 1)], b2_ref[q, LANES:, :],
                     preferred_element_type=F32, precision=lax.Precision.HIGHEST)
        lr = lam_ref[0:1, OCTET_STATE * q:OCTET_STATE * (q + 1)]
        li = lam_ref[1:2, OCTET_STATE * q:OCTET_STATE * (q + 1)]
        s0r = s0_ref[:, base:base + OCTET_STATE]
        s0i = s0_ref[:, base + OCTET_STATE:base + 2 * OCTET_STATE]
        sr = lr * s0r - li * s0i + bu[:, :OCTET_STATE]
        si = lr * s0i + li * s0r + bu[:, OCTET_STATE:]
        snew_ref[:, base:base + OCTET_STATE] = sr
        snew_ref[:, base + OCTET_STATE:base + 2 * OCTET_STATE] = si
        sb = jnp.concatenate([sr, si], axis=-1).astype(BF16)
        ys.append(_dot(sb, c2_ref[q, :, :LANES]))
    y = jnp.concatenate(ys, axis=-1)
    o_ref[...] = x + _s5_gate(y + d_ref[...] * u, wglu_ref, bglu_ref)


def _s5_sample(x, layer, gn, b2_f32, lam, c2, d, wglu, bglu, s0):
    rows = x.shape[0]
    return pl.pallas_call(
        _s5_sample_kernel,
        grid=(1,),
        in_specs=[_const_spec(x.shape)] + [_layer_spec(w.shape, layer) for w in (gn, b2_f32, lam, c2, d, wglu, bglu)]
        + [_const_spec(s0.shape)],
        out_specs=[pl.BlockSpec((rows, D_MODEL), lambda i: (0, 0)),
                   pl.BlockSpec((rows, STATE_LANES), lambda i: (0, 0))],
        out_shape=[jax.ShapeDtypeStruct((rows, D_MODEL), F32),
                   jax.ShapeDtypeStruct((rows, STATE_LANES), F32)],
        compiler_params=_params(("arbitrary",)),
        name="s5_sample",
    )(x, gn, b2_f32, lam, c2, d, wglu, bglu, s0)


def _block_diag(w):
    n, octets, groups, r, c = w.shape
    col = jnp.arange(groups * c)
    tile = (jnp.arange(c)[:, None] == col % c).astype(F32)
    own = (jnp.arange(groups * r)[:, None] // r == col // c).astype(F32)
    tiled = jnp.einsum("nqrc,cl->nqrl", w.reshape(n, octets, groups * r, c), tile, precision=lax.Precision.HIGHEST)
    return tiled * own


def _s5_weights(a_re, a_im, log_dt, b_re, b_im, c_re, c_im):
    n = a_re.shape[0]
    dt = jnp.exp(log_dt.astype(F32))
    lam = lax.complex(a_re.astype(F32), a_im.astype(F32))
    lam_bar = jnp.exp(lam * dt)
    b = lax.complex(b_re.astype(F32), b_im.astype(F32))
    b_bar = ((lam_bar - 1.0) / lam)[..., None] * b
    c = lax.complex(c_re.astype(F32), c_im.astype(F32))
    gpo = N_GROUPS // N_OCTETS
    octets = lambda w: w.reshape((n, N_OCTETS, gpo) + w.shape[2:])

    def into_state(w):
        wt = jnp.swapaxes(octets(w), 3, 4)
        return jnp.concatenate([_block_diag(jnp.real(wt)), _block_diag(jnp.imag(wt))], axis=-1)

    def out_of_state(w):
        wt = jnp.swapaxes(octets(w), 3, 4)
        return jnp.concatenate([_block_diag(jnp.real(wt)), -_block_diag(jnp.imag(wt))], axis=2)

    b2 = jnp.concatenate([into_state(lam_bar[..., None] * b_bar), into_state(b_bar)], axis=2)
    c2 = jnp.concatenate([out_of_state(c), out_of_state(c * lam_bar[:, :, None, :])], axis=-1)
    cb = jnp.real(jnp.einsum("ngcp,ngpd->ngdc", c, b_bar))
    m0 = _block_diag(octets(cb))
    lam2 = lam_bar * lam_bar
    lam_rows = jnp.stack([jnp.real(lam_bar).reshape(n, -1), jnp.imag(lam_bar).reshape(n, -1),
                          jnp.real(lam2).reshape(n, -1), jnp.imag(lam2).reshape(n, -1)], axis=1)
    return b2, c2.astype(BF16), m0.astype(BF16), lam_rows


def _state_to_lanes(s_re, s_im):
    n = s_re.shape[0]
    re = s_re.astype(F32).reshape(n, N_OCTETS, OCTET_STATE)
    im = s_im.astype(F32).reshape(n, N_OCTETS, OCTET_STATE)
    return jnp.concatenate([re, im], axis=-1).reshape(n, STATE_LANES)


def _lanes_to_state(s):
    n = s.shape[0]
    s4 = s.reshape(n, N_OCTETS, 2, OCTET_STATE)
    return (s4[:, :, 0].reshape(n, N_GROUPS, STATE_DIM), s4[:, :, 1].reshape(n, N_GROUPS, STATE_DIM))


def _rope_partner(n):
    lane = jnp.arange(n)
    return jnp.where(lane % HEAD_DIM < HEAD_DIM // 2, lane + HEAD_DIM // 2, lane - HEAD_DIM // 2)


def _kv_kernel(x_ref, g_ref, wkv_ref, kg_ref, kgp_ref, ones_ref, cos_ref, sin_ref, *rest):
    if len(rest) == 3:
        wvt_ref, k_ref, vt_ref = rest
        v_ref = None
    else:
        (k_ref, v_ref), wvt_ref, vt_ref = rest, None, None
    h = _rms(x_ref[...], g_ref[...]).astype(BF16)
    kv = _dot(h, wkv_ref[...])
    n = k_ref.shape[-1]
    k, k_partner = kv[:, :n], kv[:, n:2 * n]
    scale = lax.rsqrt(_seg_mean_sq(k, ones_ref) + EPS)
    k_ref[...] = scale * (k * (kg_ref[...] * _tile_lanes(cos_ref[...], n))
                          + k_partner * (kgp_ref[...] * _tile_lanes(sin_ref[...], n)))
    if v_ref is not None:
        v_ref[...] = kv[:, 2 * n:]
    else:
        vt_ref[...] = lax.dot_general(wvt_ref[...], h, (((1,), (1,)), ((), ())), preferred_element_type=F32)


def _kv(x, g, wk, wv, kgain, ones, cos, sin, table_block, seq=None):
    rows = x.shape[0]
    n, nv = wk.shape[1], wv.shape[1]
    partner = _rope_partner(n)
    wkv = jnp.concatenate([wk, wk[:, partner]] + ([wv] if seq is None else []), axis=1).astype(BF16)
    tm = min(KV_ROWS, rows)
    in_specs = [pl.BlockSpec((tm, D_MODEL), lambda i: (i, 0)),
                _const_spec((1, D_MODEL)),
                _const_spec(wkv.shape),
                _const_spec((1, n)),
                _const_spec((1, n)),
                _const_spec(ones.shape),
                pl.BlockSpec((tm, LANES), lambda i: (table_block(i), 0)),
                pl.BlockSpec((tm, LANES), lambda i: (table_block(i), 0))]
    out_specs = [pl.BlockSpec((tm, n), lambda i: (i, 0))]
    out_shape = [jax.ShapeDtypeStruct((rows, n), F32)]
    args = [x, g, wkv, kgain, kgain[:, partner], ones, cos, sin]
    if seq is None:
        out_specs.append(pl.BlockSpec((tm, nv), lambda i: (i, 0)))
        out_shape.append(jax.ShapeDtypeStruct((rows, nv), F32))
    else:
        per_seq = seq // tm
        in_specs.append(_const_spec((nv, D_MODEL)))
        args.append(wv.T.astype(BF16))
        out_specs.append(pl.BlockSpec((None, nv, tm), lambda i: (i // per_seq, 0, i % per_seq)))
        out_shape.append(jax.ShapeDtypeStruct((rows // seq, nv, seq), F32))
    return pl.pallas_call(
        _kv_kernel,
        grid=(rows // tm,),
        in_specs=in_specs,
        out_specs=out_specs,
        out_shape=out_shape,
        compiler_params=_params(("parallel",)),
        name="kv_proj",
    )(*args)


def _attn_prompt_kernel(sinks_ref, x_ref, g_ref, wq_ref, qg_ref, ones_ref, cos_ref, sin_ref,
                        kp_ref, kc_ref, vp_ref, vc_ref, wo_ref, o_ref):
    tile = pl.program_id(1)
    x = x_ref[...]
    n_blocks = x.shape[0] // WINDOW
    heads_per_kv = N_HEADS // N_KV_HEADS
    cols = heads_per_kv * WINDOW
    h = _rms(x, g_ref[...]).astype(BF16)
    q = _dot(h, wq_ref[...])
    q = q * lax.rsqrt(_seg_mean_sq(q, ones_ref) + EPS) * qg_ref[...]
    q = _rope(q, _tile_lanes(cos_ref[...], D_MODEL), _tile_lanes(sin_ref[...], D_MODEL))
    qb = (q * (HEAD_DIM ** -0.5)).astype(BF16)

    kall = jnp.concatenate([kp_ref[...], kc_ref[...]], axis=0).astype(BF16)
    vall = jnp.concatenate([vp_ref[...], vc_ref[...]], axis=1).astype(BF16)

    kj = lax.broadcasted_iota(jnp.int32, (WINDOW, cols), 0)
    col = lax.broadcasted_iota(jnp.int32, (WINDOW, cols), 1)
    from_prev = kj > (col & (WINDOW - 1))
    col_head = lax.broadcasted_iota(jnp.int32, (1, cols), 1) // WINDOW
    low_half = lax.broadcasted_iota(jnp.int32, (WINDOW, LANES), 1) < HEAD_DIM
    neg_inf = jnp.full((WINDOW, cols), -jnp.inf, F32)

    pairs = [(blk, g) for blk in range(n_blocks) for g in range(N_KV_HEADS)]
    scores, sinks = [], []
    for blk, g in pairs:
        r0 = WINDOW * blk
        kg = kall[r0:r0 + 2 * WINDOW, LANES * g:LANES * (g + 1)]
        qs = []
        sink = jnp.zeros((1, cols), F32)
        for hh in range(heads_per_kv):
            head = heads_per_kv * g + hh
            qp = qb[r0:r0 + WINDOW, LANES * (head // 2):LANES * (head // 2 + 1)]
            keep = low_half if head % 2 == 0 else jnp.logical_not(low_half)
            qs.append(jnp.where(keep, qp, jnp.zeros_like(qp)))
            sink = jnp.where(col_head == hh, sinks_ref[head], sink)
        sinks.append(sink)
        scores.append(lax.dot_general(kg, jnp.concatenate(qs, axis=0), (((1,), (1,)), ((), ())),
                                      preferred_element_type=F32))
    probs, denoms = [], []
    for (blk, g), s, sink in zip(pairs, scores, sinks):
        has_prev = tile * n_blocks + blk > 0
        t = jnp.where(from_prev, jnp.where(has_prev, s[:WINDOW], neg_inf), s[WINDOW:])
        m = jnp.maximum(jnp.max(t, axis=0, keepdims=True), sink)
        p = jnp.exp(t - m)
        denoms.append(jnp.sum(p, axis=0, keepdims=True) + jnp.exp(sink - m))
        zero = jnp.zeros_like(p)
        probs.append(jnp.concatenate([jnp.where(from_prev, p, zero), jnp.where(from_prev, zero, p)],
                                     axis=0).astype(BF16))
    head_outs = [[] for _ in range(n_blocks)]
    for (blk, g), pcat, denom in zip(pairs, probs, denoms):
        r0 = WINDOW * blk
        vg = vall[HEAD_DIM * g:HEAD_DIM * (g + 1), r0:r0 + 2 * WINDOW]
        o = _dot(vg, pcat) / denom
        head_outs[blk] += [o[:, WINDOW * hh:WINDOW * (hh + 1)] for hh in range(heads_per_kv)]
    block_outs = [jnp.concatenate(outs, axis=0).T for outs in head_outs]
    attn = jnp.concatenate(block_outs, axis=0).astype(BF16)
    o_ref[...] = x + _dot(attn, wo_ref[...])


def _attn_prompt(x, g, wq, qgain, ones, cos, sin, kd, vt, sinks, wo):
    bsz, seq, _ = x.shape
    nk, nv = kd.shape[-1], vt.shape[1]
    nb = ATTN_BLOCKS
    tq = nb * WINDOW
    x_spec = pl.BlockSpec((None, tq, D_MODEL), lambda b, i, *_: (b, i, 0))
    k_cur = pl.BlockSpec((None, tq, nk), lambda b, i, *_: (b, i, 0))
    k_prev = pl.BlockSpec((None, WINDOW, nk), lambda b, i, *_: (b, jnp.maximum(nb * i - 1, 0), 0))
    v_cur = pl.BlockSpec((None, nv, tq), lambda b, i, *_: (b, 0, i))
    v_prev = pl.BlockSpec((None, nv, WINDOW), lambda b, i, *_: (b, 0, jnp.maximum(nb * i - 1, 0)))
    tab = pl.BlockSpec((tq, LANES), lambda b, i, *_: (i, 0))
    grid_spec = pltpu.PrefetchScalarGridSpec(
        num_scalar_prefetch=1,
        grid=(bsz, seq // tq),
        in_specs=[x_spec,
                  _const_spec((1, D_MODEL)),
                  _const_spec(wq.shape),
                  _const_spec((1, D_MODEL)),
                  _const_spec(ones.shape),
                  tab, tab, k_prev, k_cur, v_prev, v_cur,
                  _const_spec(wo.shape)],
        out_specs=x_spec)
    return pl.pallas_call(
        _attn_prompt_kernel,
        grid_spec=grid_spec,
        out_shape=jax.ShapeDtypeStruct(x.shape, F32),
        compiler_params=_params(("parallel", "arbitrary")),
        name="attn_prompt",
    )(sinks, x, g, wq, qgain, ones, cos, sin, kd, kd, vt, vt, wo)


def _own_head(shape):
    row_head = lax.broadcasted_iota(jnp.int32, shape, 0) & (N_HEADS - 1)
    lane_head = lax.broadcasted_iota(jnp.int32, shape, 1) // HEAD_DIM
    return row_head == lane_head


def _q_sample_kernel(x_ref, g_ref, wq_ref, qg_ref, ones_ref, cos_ref, sin_ref, rep_ref, place_ref, q_ref):
    h = _rms(x_ref[...], g_ref[...]).astype(BF16)
    q = _dot(h, wq_ref[...])
    q = q * lax.rsqrt(_seg_mean_sq(q, ones_ref) + EPS) * qg_ref[...]
    q = _rope(q, _tile_lanes(cos_ref[...], D_MODEL), _tile_lanes(sin_ref[...], D_MODEL))
    qb = (q * (HEAD_DIM ** -0.5)).astype(BF16)
    q_rep = _dot(rep_ref[...], qb)
    q_own = jnp.where(_own_head(q_rep.shape), q_rep, 0.0).astype(BF16)
    q_ref[...] = _dot(q_own, place_ref[...])


def _q_sample(x, g, wq, qgain, ones, cos, sin, rep, place):
    rows = x.shape[0]
    return pl.pallas_call(
        _q_sample_kernel,
        out_shape=jax.ShapeDtypeStruct((rows * N_HEADS, place.shape[1]), F32),
        compiler_params=pltpu.CompilerParams(vmem_limit_bytes=VMEM_LIMIT_BYTES),
        name="q_sample",
    )(x, g, wq, qgain, ones, cos, sin, rep, place)


def _cache_append_kernel(k_ref, v_ref, kn_ref, vn_ref, ko_ref, vo_ref):
    nb, lanes_kv, keys = k_ref.shape
    first = pl.program_id(0) * nb
    key = lax.broadcasted_iota(jnp.int32, (lanes_kv, keys), 1)
    seq = lax.broadcasted_iota(jnp.int32, (kn_ref.shape[1], keys), 0)

    def split3(x):
        hi = x.astype(BF16)
        r = x - hi.astype(F32)
        mid = r.astype(BF16)
        return hi, mid, (r - mid.astype(F32)).astype(BF16)

    parts = [(split3(kn_ref[...]), k_ref, ko_ref), (split3(vn_ref[...]), v_ref, vo_ref)]
    for b in range(nb):
        pick = (seq == first + b).astype(BF16)
        for (hi, mid, lo), old_ref, out_ref in parts:
            new_col = _dot(hi, pick) + _dot(mid, pick) + _dot(lo, pick)
            out_ref[b] = jnp.where(key == keys - 1, new_col, pltpu.roll(old_ref[b], keys - 1, 1))


def _cache_append(kt, vt, kn_t, vn_t):
    bsz, lanes_kv, keys = kt.shape
    nb = 4 * SUBLANES
    blk = pl.BlockSpec((nb, lanes_kv, keys), lambda i: (i, 0, 0))
    return pl.pallas_call(
        _cache_append_kernel,
        grid=(bsz // nb,),
        in_specs=[blk, blk, _const_spec(kn_t.shape), _const_spec(vn_t.shape)],
        out_specs=[blk, blk],
        out_shape=[jax.ShapeDtypeStruct(kt.shape, F32), jax.ShapeDtypeStruct(vt.shape, F32)],
        compiler_params=_params(("parallel",)),
        name="cache_append",
    )(kt, vt, kn_t, vn_t)


def _attn_sample_kernel(q_ref, k_ref, v_ref, sink_ref, o_ref):
    sink = sink_ref[...]
    for b in range(q_ref.shape[0]):
        s = _dot(q_ref[b].astype(BF16), k_ref[b].astype(BF16))
        m = jnp.maximum(jnp.max(s, axis=-1, keepdims=True), sink)
        p = jnp.exp(s - m)
        denom = jnp.sum(p, axis=-1, keepdims=True) + jnp.exp(sink - m)
        o = lax.dot_general(p.astype(BF16), v_ref[b].astype(BF16), (((1,), (1,)), ((), ())),
                            preferred_element_type=F32)
        o_ref[b] = o / denom


def _attn_sample(q3, kt, vt, sink_col):
    bsz, nh, nkv = q3.shape
    keys = kt.shape[2]
    bb = 2 * SUBLANES
    return pl.pallas_call(
        _attn_sample_kernel,
        grid=(bsz // bb,),
        in_specs=[pl.BlockSpec((bb, nh, nkv), lambda i: (i, 0, 0)),
                  pl.BlockSpec((bb, nkv, keys), lambda i: (i, 0, 0)),
                  pl.BlockSpec((bb, nkv, keys), lambda i: (i, 0, 0)),
                  _const_spec(sink_col.shape)],
        out_specs=pl.BlockSpec((bb, nh, nkv), lambda i: (i, 0, 0)),
        out_shape=jax.ShapeDtypeStruct((bsz, nh, nkv), F32),
        compiler_params=_params(("parallel",)),
        name="attn_sample",
    )(q3, kt, vt, sink_col)


def _proj_residual_kernel(x_ref, a_ref, place_t_ref, rep_t_ref, w_ref, o_ref):
    z = _dot(a_ref[...].astype(BF16), place_t_ref[...])
    z_own = jnp.where(_own_head(z.shape), z, 0.0).astype(BF16)
    attn = _dot(rep_t_ref[...], z_own).astype(BF16)
    o_ref[...] = x_ref[...] + _dot(attn, w_ref[...])


def _proj_residual(x, a, place_t, rep_t, w):
    return pl.pallas_call(
        _proj_residual_kernel,
        out_shape=jax.ShapeDtypeStruct(x.shape, F32),
        compiler_params=pltpu.CompilerParams(vmem_limit_bytes=VMEM_LIMIT_BYTES),
        name="o_proj_sample",
    )(x, a, place_t, rep_t, w)


def _rope_tables(pos):
    half = HEAD_DIM // 2
    inv = ROPE_THETA ** (-jnp.arange(half, dtype=F32) / half)
    ang = pos.astype(F32)[:, None] * inv[None, :]
    cos, sin = jnp.cos(ang), jnp.sin(ang)
    reps = LANES // HEAD_DIM
    return (jnp.tile(jnp.concatenate([cos, cos], axis=-1), (1, reps)),
            jnp.tile(jnp.concatenate([-sin, sin], axis=-1), (1, reps)))


def _block_ones(width, block):
    idx = jnp.arange(width) // block
    return (idx[:, None] == idx[None, :]).astype(BF16)


def _dup_heads(w):
    w3 = w.reshape(w.shape[0], N_KV_HEADS, 1, HEAD_DIM)
    return jnp.broadcast_to(w3, (w.shape[0], N_KV_HEADS, LANES // HEAD_DIM, HEAD_DIM)).reshape(w.shape[0], -1)


def _head_placement():
    lane = jnp.arange(N_HEADS * HEAD_DIM)
    dst = (lane // HEAD_DIM) // (N_HEADS // N_KV_HEADS) * HEAD_DIM + lane % HEAD_DIM
    return (dst[:, None] == jnp.arange(N_KV_HEADS * HEAD_DIM)[None, :]).astype(BF16)


def _row_replication(n):
    return (jnp.arange(n * N_HEADS)[:, None] // N_HEADS == jnp.arange(n)[None, :]).astype(BF16)


def kernel(x_prompt, x_sample, state_ssm_re, state_ssm_im, cache_k, cache_v, norm_mix, norm_mlp, ssm_a_re, ssm_a_im, ssm_log_dt, ssm_b_re, ssm_b_im, ssm_c_re, ssm_c_im, ssm_d, w_glu, b_glu, norm_kv, w_k, w_v, k_norm, w_q, q_norm, attn_sinks, w_o, w_mlp_in, w_mlp_out):
    bsz, seq, _ = x_prompt.shape
    dec = x_sample.shape[0]
    n_a = ssm_a_re.shape[0]
    depth = norm_mix.shape[0]
    past = 8192
    nkv = N_KV_HEADS * HEAD_DIM
    row = lambda v: v.astype(F32).reshape(1, -1)

    xp = x_prompt.astype(F32)
    xs = x_sample.reshape(dec, D_MODEL)
    rows3 = lambda v: v.astype(F32).reshape(v.shape[0], 1, -1)

    w1 = w_mlp_in.astype(BF16)
    w2 = w_mlp_out.astype(BF16)
    g_mlp = rows3(norm_mlp)
    def mlp_both(x, xs, layer):
        y, ys = _mlp(x.reshape(bsz * seq, D_MODEL), xs, g_mlp, w1, w2, layer)
        return y.reshape(bsz, seq, D_MODEL), ys


    b2, c2, m0, lam = _s5_weights(ssm_a_re, ssm_a_im, ssm_log_dt, ssm_b_re, ssm_b_im, ssm_c_re, ssm_c_im)
    b2_16 = b2.astype(BF16)
    wglu = w_glu.astype(BF16)
    g_mix, d_skip, bg = rows3(norm_mix), rows3(ssm_d), rows3(b_glu)
    sp_re, sp_im, ss_re, ss_im = [], [], [], []
    for i in range(n_a):
        xp, fin = _s5_prompt(xp, i, g_mix, b2_16, lam, c2, m0, d_skip, wglu, bg)
        re, im = _lanes_to_state(fin[bsz:])
        sp_re.append(re)
        sp_im.append(im)
        s0 = _state_to_lanes(state_ssm_re[i], state_ssm_im[i])
        xs, snew = _s5_sample(xs, i, g_mix, b2, lam, c2, d_skip, wglu, bg, s0)
        re, im = _lanes_to_state(snew)
        ss_re.append(re)
        ss_im.append(im)
        xp, xs = mlp_both(xp, xs, i)

    cos_p, sin_p = _rope_tables(jnp.arange(seq, dtype=jnp.int32))
    cos_s, sin_s = _rope_tables(jnp.full((dec,), past, dtype=jnp.int32))
    ones_head = _block_ones(MXU_TILE, HEAD_DIM)
    place = _head_placement()
    rep = _row_replication(dec)
    kgain = row(k_norm)
    tile_gain = lambda gvec, n: jnp.tile(gvec, (1, n // HEAD_DIM))

    kd, vt = _kv(xp.reshape(bsz * seq, D_MODEL), row(norm_kv), _dup_heads(w_k), w_v,
                     tile_gain(kgain, 2 * nkv), ones_head, cos_p, sin_p, lambda i: i % (seq // KV_ROWS), seq=seq)
    ks_new, vs_new = _kv(xs, row(norm_kv), w_k, w_v, tile_gain(kgain, nkv), ones_head, cos_s, sin_s,
                         lambda i: i)
    kd = kd.reshape(bsz, seq, 2 * nkv)
    new_k_p = kd[:, -WINDOW:].reshape(bsz, WINDOW, N_KV_HEADS, LANES // HEAD_DIM, HEAD_DIM)[:, :, :, 0]
    new_v_p = jnp.transpose(vt[:, :, -WINDOW:].reshape(bsz, N_KV_HEADS, HEAD_DIM, WINDOW), (0, 3, 1, 2))
    key_minor = lambda c: jnp.transpose(c.astype(F32), (0, 2, 3, 1)).reshape(dec, nkv, c.shape[1])
    keys_s, vals_s = _cache_append(key_minor(cache_k), key_minor(cache_v), ks_new.T, vs_new.T)
    key_major = lambda c: jnp.transpose(c.reshape(dec, N_KV_HEADS, HEAD_DIM, c.shape[2]), (0, 3, 1, 2))
    new_k_s, new_v_s = key_major(keys_s), key_major(vals_s)

    for j in range(depth - n_a):
        layer = n_a + j
        gn = row(norm_mix[layer])
        qgain = row(q_norm[j])
        wq = w_q[j].astype(BF16)
        wo = w_o[j].astype(BF16)
        xp = _attn_prompt(xp, gn, wq, tile_gain(qgain, D_MODEL), ones_head, cos_p, sin_p, kd, vt,
                          attn_sinks[j].astype(F32), wo)
        q_pad = _q_sample(xs, gn, wq, tile_gain(qgain, D_MODEL), ones_head, cos_s, sin_s, rep, place)
        o_pad = _attn_sample(q_pad.reshape(dec, N_HEADS, nkv), keys_s, vals_s,
                             attn_sinks[j].astype(F32).reshape(N_HEADS, 1))
        xs = _proj_residual(xs, o_pad.reshape(dec * N_HEADS, nkv), place.T, rep.T, wo)
        xp, xs = mlp_both(xp, xs, layer)

    return (xp, xs.reshape(dec, 1, D_MODEL),
            jnp.stack(sp_re), jnp.stack(sp_im), new_k_p, new_v_p,
            jnp.stack(ss_re), jnp.stack(ss_im), new_k_s, new_v_s)
```

```python
import jax
import jax.numpy as jnp
from jax import lax
from jax.experimental import pallas as pl
from jax.experimental.pallas import tpu as pltpu

F32 = jnp.float32
BF16 = jnp.bfloat16

D_MODEL = 1024
N_GROUPS = 64
GROUP_SIZE = 16
STATE_DIM = 64
HEAD_DIM = 64
N_HEADS = 16
N_KV_HEADS = 4
WINDOW = 128
ROPE_THETA = 10000.0
D_FF = 4 * D_MODEL
EPS = 1e-6

SUBLANES = 8
LANES = 128
MXU_TILE = 256
VMEM_LIMIT_BYTES = 56 * 1024 * 1024

N_OCTETS = D_MODEL // LANES
OCTET_STATE = (LANES // GROUP_SIZE) * STATE_DIM
STATE_LANES = 2 * OCTET_STATE * N_OCTETS

S5_TIME_CHUNK = 128
MLP_ROWS = 1024
FF_CHUNK = 512
KV_ROWS = 1024
ATTN_BLOCKS = 8


def _const_spec(shape):
    zeros = (0,) * len(shape)
    return pl.BlockSpec(shape, lambda *_: zeros, pipeline_mode=pl.Buffered(1))


def _layer_spec(shape, layer):
    idx = (layer,) + (0,) * (len(shape) - 1)
    return pl.BlockSpec((None,) + tuple(shape[1:]), lambda *_: idx, pipeline_mode=pl.Buffered(1))


def _params(sem):
    return pltpu.CompilerParams(dimension_semantics=sem, vmem_limit_bytes=VMEM_LIMIT_BYTES)


def _rms(x, g):
    return x * lax.rsqrt(jnp.mean(x * x, axis=-1, keepdims=True) + EPS) * g


def _dot(a, b):
    return jnp.dot(a, b, preferred_element_type=F32)


def _seg_mean_sq(x, ones_ref):
    sq = x * x
    hi = sq.astype(BF16)
    lo = (sq - hi.astype(F32)).astype(BF16)
    ones = ones_ref[...]
    outs = []
    for t in range(x.shape[-1] // MXU_TILE):
        sl = slice(MXU_TILE * t, MXU_TILE * (t + 1))
        outs.append(_dot(hi[:, sl], ones) + _dot(lo[:, sl], ones))
    return jnp.concatenate(outs, axis=-1) * (1.0 / HEAD_DIM)


def _tile_lanes(t, n):
    return jnp.concatenate([t] * (n // t.shape[-1]), axis=-1)


def _rope(x, cos, sin_signed):
    n = x.shape[-1]
    lane = lax.broadcasted_iota(jnp.int32, x.shape, 1)
    first = (lane & (HEAD_DIM - 1)) < (HEAD_DIM // 2)
    partner = jnp.where(first, pltpu.roll(x, n - HEAD_DIM // 2, 1), pltpu.roll(x, HEAD_DIM // 2, 1))
    return x * cos + partner * sin_signed


def _mlp_rows(x, g_ref, w1_ref, w2_ref):
    h = _rms(x, g_ref[...]).astype(BF16)
    acc = x
    for c in range(D_FF // FF_CHUNK):
        sl = slice(FF_CHUNK * c, FF_CHUNK * (c + 1))
        a = jnp.square(jnp.maximum(_dot(h, w1_ref[:, sl]), 0.0)).astype(BF16)
        acc = acc + _dot(a, w2_ref[sl, :])
    return acc


def _mlp_kernel(x_ref, xs_ref, g_ref, w1_ref, w2_ref, o_ref, os_ref):
    last = pl.num_programs(0) - 1

    @pl.when(pl.program_id(0) < last)
    def _():
        o_ref[...] = _mlp_rows(x_ref[...], g_ref, w1_ref, w2_ref)

    @pl.when(pl.program_id(0) == last)
    def _():
        os_ref[...] = _mlp_rows(xs_ref[...], g_ref, w1_ref, w2_ref)


def _mlp(x, xs, g, w1, w2, layer):
    rows = x.shape[0]
    tm = MLP_ROWS
    tiles = rows // tm
    x_spec = pl.BlockSpec((tm, D_MODEL), lambda i: (jnp.minimum(i, tiles - 1), 0))
    xs_spec = pl.BlockSpec(xs.shape, lambda i: (0, 0))
    return pl.pallas_call(
        _mlp_kernel,
        grid=(tiles + 1,),
        in_specs=[x_spec, xs_spec,
                  _layer_spec(g.shape, layer),
                  _layer_spec(w1.shape, layer),
                  _layer_spec(w2.shape, layer)],
        out_specs=[x_spec, xs_spec],
        out_shape=[jax.ShapeDtypeStruct((rows, D_MODEL), F32), jax.ShapeDtypeStruct(xs.shape, F32)],
        compiler_params=_params(("arbitrary",)),
        name="mlp",
    )(x, xs, g, w1, w2)


def _s5_gate(y, wglu_ref, bglu_ref):
    g = jax.nn.gelu(y).astype(BF16)
    z = _dot(g, wglu_ref[...]) + bglu_ref[...]
    return z[:, :D_MODEL] * jax.nn.sigmoid(z[:, D_MODEL:])


def _slabs(ref, rows=slice(None)):
    return jnp.concatenate([ref[c, rows, :] for c in range(ref.shape[0])], axis=-1)


def _s5_prompt_kernel(xa_ref, xb_ref, gn_ref, b2_ref, lam_ref, c2_ref, m0_ref, d_ref, wglu_ref, bglu_ref,
                      o_ref, fin_ref, st_ref, carry_ref, nat_ref, ue_ref, uo_ref, up_ref, tb2_ref, nat2_ref, y_ref):
    batch, tc, _ = xa_ref.shape
    assert 2 * batch == SUBLANES
    n_slabs = D_MODEL // LANES
    pairs = tc // 2
    rows_k = pairs * batch
    step = pl.program_id(0)
    n_chunks = pl.num_programs(0) - 1
    live = step < n_chunks

    @pl.when(step == 0)
    def _():
        carry_ref[...] = jnp.zeros_like(carry_ref)
        y_ref[...] = jnp.zeros_like(y_ref)
        up_ref[...] = jnp.zeros_like(up_ref)

    g_prev = jax.nn.gelu(y_ref[...]).astype(BF16)
    glu_cols = 2 * D_MODEL // (N_OCTETS // 2)
    z = []

    def glu_part(c):
        sl = slice(glu_cols * c, glu_cols * (c + 1))
        z.append(_dot(g_prev, wglu_ref[:, sl]) + bglu_ref[:, sl])

    def finish_prev():
        zs = jnp.concatenate(z, axis=-1)
        upd = zs[:, :D_MODEL] * jax.nn.sigmoid(zs[:, D_MODEL:])
        for c in range(n_slabs):
            tb2_ref[c] = upd[:, LANES * c:LANES * (c + 1)]
        for b in range(batch):
            for c in range(n_slabs):
                nat2_ref[c, pl.ds(0, pairs, stride=2), :] = tb2_ref[c, pl.ds(b, pairs, stride=batch), :]
                nat2_ref[c, pl.ds(1, pairs, stride=2), :] = tb2_ref[c, pl.ds(rows_k + b, pairs, stride=batch), :]
            o_ref[b] = xb_ref[b] + _slabs(nat2_ref)

    for c in range(n_slabs):
        up_ref[c, 0:batch, :] = up_ref[c, rows_k:rows_k + batch, :]
    for b in range(batch):
        u_nat = _rms(xa_ref[b], gn_ref[...])
        for c in range(n_slabs):
            nat_ref[c] = u_nat[:, LANES * c:LANES * (c + 1)]
        for c in range(n_slabs):
            even = nat_ref[c, pl.ds(0, pairs, stride=2), :]
            odd = nat_ref[c, pl.ds(1, pairs, stride=2), :]
            ue_ref[c, pl.ds(b, pairs, stride=batch), :] = even
            uo_ref[c, pl.ds(b, pairs, stride=batch), :] = odd
            up_ref[c, pl.ds(batch + b, pairs, stride=batch), :] = odd
    ue, uo = _slabs(ue_ref), _slabs(uo_ref)
    ue_b, uo_b, up_b = ue.astype(BF16), uo.astype(BF16), _slabs(up_ref, slice(0, rows_k)).astype(BF16)

    lo = lax.broadcasted_iota(jnp.int32, (SUBLANES, LANES), 0) < batch
    swap = lambda v: pltpu.roll(v, batch, 0)
    n_vregs = rows_k // SUBLANES
    half = OCTET_STATE // LANES
    y_even, y_odd = [], []
    for q in range(N_OCTETS):
        ch = slice(LANES * q, LANES * (q + 1))
        lhs = jnp.concatenate([up_b[:, ch], ue_b[:, ch]], axis=1)
        st_ref[:, 2 * OCTET_STATE * q:2 * OCTET_STATE * (q + 1)] = _dot(lhs, b2_ref[q])

    def project_out(octets):
        for q in octets:
            ch = slice(LANES * q, LANES * (q + 1))
            y2 = _dot(st_ref[:, 2 * OCTET_STATE * q:2 * OCTET_STATE * (q + 1)].astype(BF16), c2_ref[q])
            y_even.append(y2[:, :LANES])
            y_odd.append(y2[:, LANES:] + _dot(uo_b[:, ch], m0_ref[q]))

    for q0 in range(0, N_OCTETS, 2):
        octets = (q0, q0 + 1)
        glu_part(q0 // 2)
        if q0 > 0:
            project_out((q0 - 2, q0 - 1))
        if q0 == N_OCTETS - 2:
            finish_prev()
        chains = []
        for q in octets:
            for j in range(half // 2):
                cols = [2 * OCTET_STATE * q + LANES * jj for jj in (j, j + half // 2)]
                lanes = [OCTET_STATE * q + LANES * jj for jj in (j, j + half // 2)]
                chains.append((cols, lanes))
        lam_r, lam_i, sr, si, old = [], [], [], [], []
        for (ca, cb), (la, lb) in chains:
            bc = lambda r, c: jnp.broadcast_to(lam_ref[r:r + 1, c:c + LANES], (SUBLANES, LANES))
            lam_r.append(jnp.where(lo, bc(2, la), bc(2, lb)))
            lam_i.append(jnp.where(lo, bc(3, la), bc(3, lb)))
            prev = [carry_ref[:, c:c + LANES] for c in (ca, cb, ca + OCTET_STATE, cb + OCTET_STATE)]
            old.append(prev)
            sr.append(jnp.where(lo, swap(prev[0]), prev[1]))
            si.append(jnp.where(lo, swap(prev[2]), prev[3]))
        last = [None] * len(chains)
        for m in range(n_vregs):
            rows = slice(SUBLANES * m, SUBLANES * (m + 1))
            for j, ((ca, cb), _) in enumerate(chains):
                ar, br = st_ref[rows, ca:ca + LANES], st_ref[rows, cb:cb + LANES]
                ai, bi = (st_ref[rows, ca + OCTET_STATE:ca + OCTET_STATE + LANES],
                          st_ref[rows, cb + OCTET_STATE:cb + OCTET_STATE + LANES])
                d0r, d1r = jnp.where(lo, ar, swap(br)), jnp.where(lo, swap(ar), br)
                d0i, d1i = jnp.where(lo, ai, swap(bi)), jnp.where(lo, swap(ai), bi)
                s0r = lam_r[j] * sr[j] - lam_i[j] * si[j] + d0r
                s0i = lam_r[j] * si[j] + lam_i[j] * sr[j] + d0i
                s1r = lam_r[j] * s0r - lam_i[j] * s0i + d1r
                s1i = lam_r[j] * s0i + lam_i[j] * s0r + d1i
                sr[j], si[j] = s1r, s1i
                out = (jnp.where(lo, s0r, swap(s1r)), jnp.where(lo, swap(s0r), s1r),
                       jnp.where(lo, s0i, swap(s1i)), jnp.where(lo, swap(s0i), s1i))
                st_ref[rows, ca:ca + LANES] = out[0]
                st_ref[rows, cb:cb + LANES] = out[1]
                st_ref[rows, ca + OCTET_STATE:ca + OCTET_STATE + LANES] = out[2]
                st_ref[rows, cb + OCTET_STATE:cb + OCTET_STATE + LANES] = out[3]
                last[j] = out
        for j, ((ca, cb), _) in enumerate(chains):
            for c, new, prev in zip((ca, cb, ca + OCTET_STATE, cb + OCTET_STATE), last[j], old[j]):
                carry_ref[:, c:c + LANES] = jnp.where(live, new, prev)
    project_out((N_OCTETS - 2, N_OCTETS - 1))
    y_ref[0:rows_k, :] = jnp.concatenate(y_even, axis=-1) + d_ref[...] * ue
    y_ref[rows_k:2 * rows_k, :] = jnp.concatenate(y_odd, axis=-1) + d_ref[...] * uo

    @pl.when(step == n_chunks - 1)
    def _():
        tail = uo_b[rows_k - 2 * SUBLANES:, :]
        zeros = jnp.zeros((2 * SUBLANES, LANES), BF16)
        for q in range(N_OCTETS):
            bu = _dot(jnp.concatenate([zeros, tail[:, LANES * q:LANES * (q + 1)]], axis=1), b2_ref[q])[SUBLANES:]
            base = 2 * OCTET_STATE * q
            lr = lam_ref[0:1, OCTET_STATE * q:OCTET_STATE * (q + 1)]
            li = lam_ref[1:2, OCTET_STATE * q:OCTET_STATE * (q + 1)]
            s_r = carry_ref[:, base:base + OCTET_STATE]
            s_i = carry_ref[:, base + OCTET_STATE:base + 2 * OCTET_STATE]
            fin_ref[:, base:base + OCTET_STATE] = lr * s_r - li * s_i + bu[:, :OCTET_STATE]
            fin_ref[:, base + OCTET_STATE:base + 2 * OCTET_STATE] = lr * s_i + li * s_r + bu[:, OCTET_STATE:]


def _s5_prompt(x, layer, gn, b2, lam, c2, m0, d, wglu, bglu):
    batch, seq, _ = x.shape
    tc = S5_TIME_CHUNK
    n_chunks = seq // tc
    rows_k = tc // 2 * batch
    n_slabs = D_MODEL // LANES
    blk = (batch, tc, D_MODEL)
    this_chunk = pl.BlockSpec(blk, lambda s: (0, jnp.minimum(s, n_chunks - 1), 0))
    prev_chunk = pl.BlockSpec(blk, lambda s: (0, jnp.maximum(s - 1, 0), 0))
    slab = lambda rows: pltpu.VMEM((n_slabs, rows, LANES), F32)
    return pl.pallas_call(
        _s5_prompt_kernel,
        grid=(n_chunks + 1,),
        in_specs=[this_chunk, prev_chunk]
        + [_layer_spec(w.shape, layer) for w in (gn, b2, lam, c2, m0, d, wglu, bglu)],
        out_specs=[prev_chunk, pl.BlockSpec((SUBLANES, STATE_LANES), lambda s: (0, 0))],
        out_shape=[jax.ShapeDtypeStruct(x.shape, F32),
                   jax.ShapeDtypeStruct((SUBLANES, STATE_LANES), F32)],
        scratch_shapes=[pltpu.VMEM((rows_k, STATE_LANES), F32),
                        pltpu.VMEM((SUBLANES, STATE_LANES), F32),
                        slab(tc), slab(rows_k), slab(rows_k), slab(rows_k + SUBLANES),
                        slab(2 * rows_k), slab(tc),
                        pltpu.VMEM((2 * rows_k, D_MODEL), F32)],
        compiler_params=_params(("arbitrary",)),
        name="s5_prompt",
    )(x, x, gn, b2, lam, c2, m0, d, wglu, bglu)


def _s5_sample_kernel(x_ref, gn_ref, b2_ref, lam_ref, c2_ref, d_ref, wglu_ref, bglu_ref, s0_ref,
                      o_ref, snew_ref):
    x = x_ref[...]
    u = _rms(x, gn_ref[...])
    ys = []
    for q in range(N_OCTETS):
        base = 2 * OCTET_STATE * q
        bu = jnp.dot(u[:, LANES * q:LANES * (q + 1)], b2_ref[q, LANES:, :],
                     preferred_element_type=F32, precision=lax.Precision.HIGHEST)
        lr = lam_ref[0:1, OCTET_STATE * q:OCTET_STATE * (q + 1)]
        li = lam_ref[1:2, OCTET_STATE * q:OCTET_STATE * (q + 1)]
        s0r = s0_ref[:, base:base + OCTET_STATE]
        s0i = s0_ref[:, base + OCTET_STATE:base + 2 * OCTET_STATE]
        sr = lr * s0r - li * s0i + bu[:, :OCTET_STATE]
        si = lr * s0i + li * s0r + bu[:, OCTET_STATE:]
        snew_ref[:, base:base + OCTET_STATE] = sr
        snew_ref[:, base + OCTET_STATE:base + 2 * OCTET_STATE] = si
        sb = jnp.concatenate([sr, si], axis=-1).astype(BF16)
        ys.append(_dot(sb, c2_ref[q, :, :LANES]))
    y = jnp.concatenate(ys, axis=-1)
    o_ref[...] = x + _s5_gate(y + d_ref[...] * u, wglu_ref, bglu_ref)


def _s5_sample(x, layer, gn, b2_f32, lam, c2, d, wglu, bglu, s0):
    rows = x.shape[0]
    return pl.pallas_call(
        _s5_sample_kernel,
        grid=(1,),
        in_specs=[_const_spec(x.shape)] + [_layer_spec(w.shape, layer) for w in (gn, b2_f32, lam, c2, d, wglu, bglu)]
        + [_const_spec(s0.shape)],
        out_specs=[pl.BlockSpec((rows, D_MODEL), lambda i: (0, 0)),
                   pl.BlockSpec((rows, STATE_LANES), lambda i: (0, 0))],
        out_shape=[jax.ShapeDtypeStruct((rows, D_MODEL), F32),
                   jax.ShapeDtypeStruct((rows, STATE_LANES), F32)],
        compiler_params=_params(("arbitrary",)),
        name="s5_sample",
    )(x, gn, b2_f32, lam, c2, d, wglu, bglu, s0)


def _block_diag(w):
    n, octets, groups, r, c = w.shape
    col = jnp.arange(groups * c)
    tile = (jnp.arange(c)[:, None] == col % c).astype(F32)
    own = (jnp.arange(groups * r)[:, None] // r == col // c).astype(F32)
    tiled = jnp.einsum("nqrc,cl->nqrl", w.reshape(n, octets, groups * r, c), tile, precision=lax.Precision.HIGHEST)
    return tiled * own


def _s5_weights(a_re, a_im, log_dt, b_re, b_im, c_re, c_im):
    n = a_re.shape[0]
    dt = jnp.exp(log_dt.astype(F32))
    lam = lax.complex(a_re.astype(F32), a_im.astype(F32))
    lam_bar = jnp.exp(lam * dt)
    b = lax.complex(b_re.astype(F32), b_im.astype(F32))
    b_bar = ((lam_bar - 1.0) / lam)[..., None] * b
    c = lax.complex(c_re.astype(F32), c_im.astype(F32))
    gpo = N_GROUPS // N_OCTETS
    octets = lambda w: w.reshape((n, N_OCTETS, gpo) + w.shape[2:])

    def into_state(w):
        wt = jnp.swapaxes(octets(w), 3, 4)
        return jnp.concatenate([_block_diag(jnp.real(wt)), _block_diag(jnp.imag(wt))], axis=-1)

    def out_of_state(w):
        wt = jnp.swapaxes(octets(w), 3, 4)
        return jnp.concatenate([_block_diag(jnp.real(wt)), -_block_diag(jnp.imag(wt))], axis=2)

    b2 = jnp.concatenate([into_state(lam_bar[..., None] * b_bar), into_state(b_bar)], axis=2)
    c2 = jnp.concatenate([out_of_state(c), out_of_state(c * lam_bar[:, :, None, :])], axis=-1)
    cb = jnp.real(jnp.einsum("ngcp,ngpd->ngdc", c, b_bar))
    m0 = _block_diag(octets(cb))
    lam2 = lam_bar * lam_bar
    lam_rows = jnp.stack([jnp.real(lam_bar).reshape(n, -1), jnp.imag(lam_bar).reshape(n, -1),
                          jnp.real(lam2).reshape(n, -1), jnp.imag(lam2).reshape(n, -1)], axis=1)
    return b2, c2.astype(BF16), m0.astype(BF16), lam_rows


def _state_to_lanes(s_re, s_im):
    n = s_re.shape[0]
    re = s_re.astype(F32).reshape(n, N_OCTETS, OCTET_STATE)
    im = s_im.astype(F32).reshape(n, N_OCTETS, OCTET_STATE)
    return jnp.concatenate([re, im], axis=-1).reshape(n, STATE_LANES)


def _lanes_to_state(s):
    n = s.shape[0]
    s4 = s.reshape(n, N_OCTETS, 2, OCTET_STATE)
    return (s4[:, :, 0].reshape(n, N_GROUPS, STATE_DIM), s4[:, :, 1].reshape(n, N_GROUPS, STATE_DIM))


def _rope_partner(n):
    lane = jnp.arange(n)
    return jnp.where(lane % HEAD_DIM < HEAD_DIM // 2, lane + HEAD_DIM // 2, lane - HEAD_DIM // 2)


def _kv_kernel(x_ref, g_ref, wkv_ref, kg_ref, kgp_ref, ones_ref, cos_ref, sin_ref, *rest):
    if len(rest) == 3:
        wvt_ref, k_ref, vt_ref = rest
        v_ref = None
    else:
        (k_ref, v_ref), wvt_ref, vt_ref = rest, None, None
    h = _rms(x_ref[...], g_ref[...]).astype(BF16)
    kv = _dot(h, wkv_ref[...])
    n = k_ref.shape[-1]
    k, k_partner = kv[:, :n], kv[:, n:2 * n]
    scale = lax.rsqrt(_seg_mean_sq(k, ones_ref) + EPS)
    k_ref[...] = scale * (k * (kg_ref[...] * _tile_lanes(cos_ref[...], n))
                          + k_partner * (kgp_ref[...] * _tile_lanes(sin_ref[...], n)))
    if v_ref is not None:
        v_ref[...] = kv[:, 2 * n:]
    else:
        vt_ref[...] = lax.dot_general(wvt_ref[...], h, (((1,), (1,)), ((), ())), preferred_element_type=F32)


def _kv(x, g, wk, wv, kgain, ones, cos, sin, table_block, seq=None):
    rows = x.shape[0]
    n, nv = wk.shape[1], wv.shape[1]
    partner = _rope_partner(n)
    wkv = jnp.concatenate([wk, wk[:, partner]] + ([wv] if seq is None else []), axis=1).astype(BF16)
    tm = min(KV_ROWS, rows)
    in_specs = [pl.BlockSpec((tm, D_MODEL), lambda i: (i, 0)),
                _const_spec((1, D_MODEL)),
                _const_spec(wkv.shape),
                _const_spec((1, n)),
                _const_spec((1, n)),
                _const_spec(ones.shape),
                pl.BlockSpec((tm, LANES), lambda i: (table_block(i), 0)),
                pl.BlockSpec((tm, LANES), lambda i: (table_block(i), 0))]
    out_specs = [pl.BlockSpec((tm, n), lambda i: (i, 0))]
    out_shape = [jax.ShapeDtypeStruct((rows, n), F32)]
    args = [x, g, wkv, kgain, kgain[:, partner], ones, cos, sin]
    if seq is None:
        out_specs.append(pl.BlockSpec((tm, nv), lambda i: (i, 0)))
        out_shape.append(jax.ShapeDtypeStruct((rows, nv), F32))
    else:
        per_seq = seq // tm
        in_specs.append(_const_spec((nv, D_MODEL)))
        args.append(wv.T.astype(BF16))
        out_specs.append(pl.BlockSpec((None, nv, tm), lambda i: (i // per_seq, 0, i % per_seq)))
        out_shape.append(jax.ShapeDtypeStruct((rows // seq, nv, seq), F32))
    return pl.pallas_call(
        _kv_kernel,
        grid=(rows // tm,),
        in_specs=in_specs,
        out_specs=out_specs,
        out_shape=out_shape,
        compiler_params=_params(("parallel",)),
        name="kv_proj",
    )(*args)


def _attn_prompt_kernel(sinks_ref, x_ref, g_ref, wq_ref, qg_ref, ones_ref, cos_ref, sin_ref,
                        kp_ref, kc_ref, vp_ref, vc_ref, wo_ref, o_ref):
    tile = pl.program_id(1)
    x = x_ref[...]
    n_blocks = x.shape[0] // WINDOW
    heads_per_kv = N_HEADS // N_KV_HEADS
    cols = heads_per_kv * WINDOW
    h = _rms(x, g_ref[...]).astype(BF16)
    q = _dot(h, wq_ref[...])
    q = q * lax.rsqrt(_seg_mean_sq(q, ones_ref) + EPS) * qg_ref[...]
    q = _rope(q, _tile_lanes(cos_ref[...], D_MODEL), _tile_lanes(sin_ref[...], D_MODEL))
    qb = (q * (HEAD_DIM ** -0.5)).astype(BF16)

    kall = jnp.concatenate([kp_ref[...], kc_ref[...]], axis=0).astype(BF16)
    vall = jnp.concatenate([vp_ref[...], vc_ref[...]], axis=1).astype(BF16)

    kj = lax.broadcasted_iota(jnp.int32, (WINDOW, cols), 0)
    col = lax.broadcasted_iota(jnp.int32, (WINDOW, cols), 1)
    from_prev = kj > (col & (WINDOW - 1))
    col_head = lax.broadcasted_iota(jnp.int32, (1, cols), 1) // WINDOW
    low_half = lax.broadcasted_iota(jnp.int32, (WINDOW, LANES), 1) < HEAD_DIM
    neg_inf = jnp.full((WINDOW, cols), -jnp.inf, F32)

    pairs = [(blk, g) for blk in range(n_blocks) for g in range(N_KV_HEADS)]
    scores, sinks = [], []
    for blk, g in pairs:
        r0 = WINDOW * blk
        kg = kall[r0:r0 + 2 * WINDOW, LANES * g:LANES * (g + 1)]
        qs = []
        sink = jnp.zeros((1, cols), F32)
        for hh in range(heads_per_kv):
            head = heads_per_kv * g + hh
            qp = qb[r0:r0 + WINDOW, LANES * (head // 2):LANES * (head // 2 + 1)]
            keep = low_half if head % 2 == 0 else jnp.logical_not(low_half)
            qs.append(jnp.where(keep, qp, jnp.zeros_like(qp)))
            sink = jnp.where(col_head == hh, sinks_ref[head], sink)
        sinks.append(sink)
        scores.append(lax.dot_general(kg, jnp.concatenate(qs, axis=0), (((1,), (1,)), ((), ())),
                                      preferred_element_type=F32))
    probs, denoms = [], []
    for (blk, g), s, sink in zip(pairs, scores, sinks):
        has_prev = tile * n_blocks + blk > 0
        t = jnp.where(from_prev, jnp.where(has_prev, s[:WINDOW], neg_inf), s[WINDOW:])
        m = jnp.maximum(jnp.max(t, axis=0, keepdims=True), sink)
        p = jnp.exp(t - m)
        denoms.append(jnp.sum(p, axis=0, keepdims=True) + jnp.exp(sink - m))
        zero = jnp.zeros_like(p)
        probs.append(jnp.concatenate([jnp.where(from_prev, p, zero), jnp.where(from_prev, zero, p)],
                                     axis=0).astype(BF16))
    head_outs = [[] for _ in range(n_blocks)]
    for (blk, g), pcat, denom in zip(pairs, probs, denoms):
        r0 = WINDOW * blk
        vg = vall[HEAD_DIM * g:HEAD_DIM * (g + 1), r0:r0 + 2 * WINDOW]
        o = _dot(vg, pcat) / denom
        head_outs[blk] += [o[:, WINDOW * hh:WINDOW * (hh + 1)] for hh in range(heads_per_kv)]
    block_outs = [jnp.concatenate(outs, axis=0).T for outs in head_outs]
    attn = jnp.concatenate(block_outs, axis=0).astype(BF16)
    o_ref[...] = x + _dot(attn, wo_ref[...])


def _attn_prompt(x, g, wq, qgain, ones, cos, sin, kd, vt, sinks, wo):
    bsz, seq, _ = x.shape
    nk, nv = kd.shape[-1], vt.shape[1]
    nb = ATTN_BLOCKS
    tq = nb * WINDOW
    x_spec = pl.BlockSpec((None, tq, D_MODEL), lambda b, i, *_: (b, i, 0))
    k_cur = pl.BlockSpec((None, tq, nk), lambda b, i, *_: (b, i, 0))
    k_prev = pl.BlockSpec((None, WINDOW, nk), lambda b, i, *_: (b, jnp.maximum(nb * i - 1, 0), 0))
    v_cur = pl.BlockSpec((None, nv, tq), lambda b, i, *_: (b, 0, i))
    v_prev = pl.BlockSpec((None, nv, WINDOW), lambda b, i, *_: (b, 0, jnp.maximum(nb * i - 1, 0)))
    tab = pl.BlockSpec((tq, LANES), lambda b, i, *_: (i, 0))
    grid_spec = pltpu.PrefetchScalarGridSpec(
        num_scalar_prefetch=1,
        grid=(bsz, seq // tq),
        in_specs=[x_spec,
                  _const_spec((1, D_MODEL)),
                  _const_spec(wq.shape),
                  _const_spec((1, D_MODEL)),
                  _const_spec(ones.shape),
                  tab, tab, k_prev, k_cur, v_prev, v_cur,
                  _const_spec(wo.shape)],
        out_specs=x_spec)
    return pl.pallas_call(
        _attn_prompt_kernel,
        grid_spec=grid_spec,
        out_shape=jax.ShapeDtypeStruct(x.shape, F32),
        compiler_params=_params(("parallel", "arbitrary")),
        name="attn_prompt",
    )(sinks, x, g, wq, qgain, ones, cos, sin, kd, kd, vt, vt, wo)


def _own_head(shape):
    row_head = lax.broadcasted_iota(jnp.int32, shape, 0) & (N_HEADS - 1)
    lane_head = lax.broadcasted_iota(jnp.int32, shape, 1) // HEAD_DIM
    return row_head == lane_head


def _q_sample_kernel(x_ref, g_ref, wq_ref, qg_ref, ones_ref, cos_ref, sin_ref, rep_ref, place_ref, q_ref):
    h = _rms(x_ref[...], g_ref[...]).astype(BF16)
    q = _dot(h, wq_ref[...])
    q = q * lax.rsqrt(_seg_mean_sq(q, ones_ref) + EPS) * qg_ref[...]
    q = _rope(q, _tile_lanes(cos_ref[...], D_MODEL), _tile_lanes(sin_ref[...], D_MODEL))
    qb = (q * (HEAD_DIM ** -0.5)).astype(BF16)
    q_rep = _dot(rep_ref[...], qb)
    q_own = jnp.where(_own_head(q_rep.shape), q_rep, 0.0).astype(BF16)
    q_ref[...] = _dot(q_own, place_ref[...])


def _q_sample(x, g, wq, qgain, ones, cos, sin, rep, place):
    rows = x.shape[0]
    return pl.pallas_call(
        _q_sample_kernel,
        out_shape=jax.ShapeDtypeStruct((rows * N_HEADS, place.shape[1]), F32),
        compiler_params=pltpu.CompilerParams(vmem_limit_bytes=VMEM_LIMIT_BYTES),
        name="q_sample",
    )(x, g, wq, qgain, ones, cos, sin, rep, place)


def _cache_append_kernel(k_ref, v_ref, kn_ref, vn_ref, ko_ref, vo_ref):
    nb, lanes_kv, keys = k_ref.shape
    first = pl.program_id(0) * nb
    key = lax.broadcasted_iota(jnp.int32, (lanes_kv, keys), 1)
    seq = lax.broadcasted_iota(jnp.int32, (kn_ref.shape[1], keys), 0)

    def split3(x):
        hi = x.astype(BF16)
        r = x - hi.astype(F32)
        mid = r.astype(BF16)
        return hi, mid, (r - mid.astype(F32)).astype(BF16)

    parts = [(split3(kn_ref[...]), k_ref, ko_ref), (split3(vn_ref[...]), v_ref, vo_ref)]
    for b in range(nb):
        pick = (seq == first + b).astype(BF16)
        for (hi, mid, lo), old_ref, out_ref in parts:
            new_col = _dot(hi, pick) + _dot(mid, pick) + _dot(lo, pick)
            out_ref[b] = jnp.where(key == keys - 1, new_col, pltpu.roll(old_ref[b], keys - 1, 1))


def _cache_append(kt, vt, kn_t, vn_t):
    bsz, lanes_kv, keys = kt.shape
    nb = 2 * SUBLANES
    blk = pl.BlockSpec((nb, lanes_kv, keys), lambda i: (i, 0, 0))
    return pl.pallas_call(
        _cache_append_kernel,
        grid=(bsz // nb,),
        in_specs=[blk, blk, _const_spec(kn_t.shape), _const_spec(vn_t.shape)],
        out_specs=[blk, blk],
        out_shape=[jax.ShapeDtypeStruct(kt.shape, F32), jax.ShapeDtypeStruct(vt.shape, F32)],
        compiler_params=_params(("parallel",)),
        name="cache_append",
    )(kt, vt, kn_t, vn_t)


def _attn_sample_kernel(q_ref, k_ref, v_ref, sink_ref, o_ref):
    sink = sink_ref[...]
    for b in range(q_ref.shape[0]):
        s = _dot(q_ref[b].astype(BF16), k_ref[b].astype(BF16))
        m = jnp.maximum(jnp.max(s, axis=-1, keepdims=True), sink)
        p = jnp.exp(s - m)
        denom = jnp.sum(p, axis=-1, keepdims=True) + jnp.exp(sink - m)
        o = lax.dot_general(p.astype(BF16), v_ref[b].astype(BF16), (((1,), (1,)), ((), ())),
                            preferred_element_type=F32)
        o_ref[b] = o / denom


def _attn_sample(q3, kt, vt, sink_col):
    bsz, nh, nkv = q3.shape
    keys = kt.shape[2]
    bb = 4 * SUBLANES
    return pl.pallas_call(
        _attn_sample_kernel,
        grid=(bsz // bb,),
        in_specs=[pl.BlockSpec((bb, nh, nkv), lambda i: (i, 0, 0)),
                  pl.BlockSpec((bb, nkv, keys), lambda i: (i, 0, 0)),
                  pl.BlockSpec((bb, nkv, keys), lambda i: (i, 0, 0)),
                  _const_spec(sink_col.shape)],
        out_specs=pl.BlockSpec((bb, nh, nkv), lambda i: (i, 0, 0)),
        out_shape=jax.ShapeDtypeStruct((bsz, nh, nkv), F32),
        compiler_params=_params(("parallel",)),
        name="attn_sample",
    )(q3, kt, vt, sink_col)


def _proj_residual_kernel(x_ref, a_ref, place_t_ref, rep_t_ref, w_ref, o_ref):
    z = _dot(a_ref[...].astype(BF16), place_t_ref[...])
    z_own = jnp.where(_own_head(z.shape), z, 0.0).astype(BF16)
    attn = _dot(rep_t_ref[...], z_own).astype(BF16)
    o_ref[...] = x_ref[...] + _dot(attn, w_ref[...])


def _proj_residual(x, a, place_t, rep_t, w):
    return pl.pallas_call(
        _proj_residual_kernel,
        out_shape=jax.ShapeDtypeStruct(x.shape, F32),
        compiler_params=pltpu.CompilerParams(vmem_limit_bytes=VMEM_LIMIT_BYTES),
        name="o_proj_sample",
    )(x, a, place_t, rep_t, w)


def _rope_tables(pos):
    half = HEAD_DIM // 2
    inv = ROPE_THETA ** (-jnp.arange(half, dtype=F32) / half)
    ang = pos.astype(F32)[:, None] * inv[None, :]
    cos, sin = jnp.cos(ang), jnp.sin(ang)
    reps = LANES // HEAD_DIM
    return (jnp.tile(jnp.concatenate([cos, cos], axis=-1), (1, reps)),
            jnp.tile(jnp.concatenate([-sin, sin], axis=-1), (1, reps)))


def _block_ones(width, block):
    idx = jnp.arange(width) // block
    return (idx[:, None] == idx[None, :]).astype(BF16)


def _dup_heads(w):
    w3 = w.reshape(w.shape[0], N_KV_HEADS, 1, HEAD_DIM)
    return jnp.broadcast_to(w3, (w.shape[0], N_KV_HEADS, LANES // HEAD_DIM, HEAD_DIM)).reshape(w.shape[0], -1)


def _head_placement():
    lane = jnp.arange(N_HEADS * HEAD_DIM)
    dst = (lane // HEAD_DIM) // (N_HEADS // N_KV_HEADS) * HEAD_DIM + lane % HEAD_DIM
    return (dst[:, None] == jnp.arange(N_KV_HEADS * HEAD_DIM)[None, :]).astype(BF16)


def _row_replication(n):
    return (jnp.arange(n * N_HEADS)[:, None] // N_HEADS == jnp.arange(n)[None, :]).astype(BF16)


def kernel(x_prompt, x_sample, state_ssm_re, state_ssm_im, cache_k, cache_v, norm_mix, norm_mlp, ssm_a_re, ssm_a_im, ssm_log_dt, ssm_b_re, ssm_b_im, ssm_c_re, ssm_c_im, ssm_d, w_glu, b_glu, norm_kv, w_k, w_v, k_norm, w_q, q_norm, attn_sinks, w_o, w_mlp_in, w_mlp_out):
    bsz, seq, _ = x_prompt.shape
    dec = x_sample.shape[0]
    n_a = ssm_a_re.shape[0]
    depth = norm_mix.shape[0]
    past = 8192
    nkv = N_KV_HEADS * HEAD_DIM
    row = lambda v: v.astype(F32).reshape(1, -1)

    xp = x_prompt.astype(F32)
    xs = x_sample.reshape(dec, D_MODEL)
    rows3 = lambda v: v.astype(F32).reshape(v.shape[0], 1, -1)

    w1 = w_mlp_in.astype(BF16)
    w2 = w_mlp_out.astype(BF16)
    g_mlp = rows3(norm_mlp)
    def mlp_both(x, xs, layer):
        y, ys = _mlp(x.reshape(bsz * seq, D_MODEL), xs, g_mlp, w1, w2, layer)
        return y.reshape(bsz, seq, D_MODEL), ys


    b2, c2, m0, lam = _s5_weights(ssm_a_re, ssm_a_im, ssm_log_dt, ssm_b_re, ssm_b_im, ssm_c_re, ssm_c_im)
    b2_16 = b2.astype(BF16)
    wglu = w_glu.astype(BF16)
    g_mix, d_skip, bg = rows3(norm_mix), rows3(ssm_d), rows3(b_glu)
    sp_re, sp_im, ss_re, ss_im = [], [], [], []
    for i in range(n_a):
        xp, fin = _s5_prompt(xp, i, g_mix, b2_16, lam, c2, m0, d_skip, wglu, bg)
        re, im = _lanes_to_state(fin[bsz:])
        sp_re.append(re)
        sp_im.append(im)
        s0 = _state_to_lanes(state_ssm_re[i], state_ssm_im[i])
        xs, snew = _s5_sample(xs, i, g_mix, b2, lam, c2, d_skip, wglu, bg, s0)
        re, im = _lanes_to_state(snew)
        ss_re.append(re)
        ss_im.append(im)
        xp, xs = mlp_both(xp, xs, i)

    cos_p, sin_p = _rope_tables(jnp.arange(seq, dtype=jnp.int32))
    cos_s, sin_s = _rope_tables(jnp.full((dec,), past, dtype=jnp.int32))
    ones_head = _block_ones(MXU_TILE, HEAD_DIM)
    place = _head_placement()
    rep = _row_replication(dec)
    kgain = row(k_norm)
    tile_gain = lambda gvec, n: jnp.tile(gvec, (1, n // HEAD_DIM))

    kd, vt = _kv(xp.reshape(bsz * seq, D_MODEL), row(norm_kv), _dup_heads(w_k), w_v,
                     tile_gain(kgain, 2 * nkv), ones_head, cos_p, sin_p, lambda i: i % (seq // KV_ROWS), seq=seq)
    ks_new, vs_new = _kv(xs, row(norm_kv), w_k, w_v, tile_gain(kgain, nkv), ones_head, cos_s, sin_s,
                         lambda i: i)
    kd = kd.reshape(bsz, seq, 2 * nkv)
    new_k_p = kd[:, -WINDOW:].reshape(bsz, WINDOW, N_KV_HEADS, LANES // HEAD_DIM, HEAD_DIM)[:, :, :, 0]
    new_v_p = jnp.transpose(vt[:, :, -WINDOW:].reshape(bsz, N_KV_HEADS, HEAD_DIM, WINDOW), (0, 3, 1, 2))
    key_minor = lambda c: jnp.transpose(c.astype(F32), (0, 2, 3, 1)).reshape(dec, nkv, c.shape[1])
    keys_s, vals_s = _cache_append(key_minor(cache_k), key_minor(cache_v), ks_new.T, vs_new.T)
    key_major = lambda c: jnp.transpose(c.reshape(dec, N_KV_HEADS, HEAD_DIM, c.shape[2]), (0, 3, 1, 2))
    new_k_s, new_v_s = key_major(keys_s), key_major(vals_s)

    for j in range(depth - n_a):
        layer = n_a + j
        gn = row(norm_mix[layer])
        qgain = row(q_norm[j])
        wq = w_q[j].astype(BF16)
        wo = w_o[j].astype(BF16)
        xp = _attn_prompt(xp, gn, wq, tile_gain(qgain, D_MODEL), ones_head, cos_p, sin_p, kd, vt,
                          attn_sinks[j].astype(F32), wo)
        q_pad = _q_sample(xs, gn, wq, tile_gain(qgain, D_MODEL), ones_head, cos_s, sin_s, rep, place)
        o_pad = _attn_sample(q_pad.reshape(dec, N_HEADS, nkv), keys_s, vals_s,
                             attn_sinks[j].astype(F32).reshape(N_HEADS, 1))
        xs = _proj_residual(xs, o_pad.reshape(dec * N_HEADS, nkv), place.T, rep.T, wo)
        xp, xs = mlp_both(xp, xs, layer)

    return (xp, xs.reshape(dec, 1, D_MODEL),
            jnp.stack(sp_re), jnp.stack(sp_im), new_k_p, new_v_p,
            jnp.stack(ss_re), jnp.stack(ss_im), new_k_s, new_v_s)
```

```python
import jax
import jax.numpy as jnp
from jax import lax
from jax.experimental import pallas as pl
from jax.experimental.pallas import tpu as pltpu

F32 = jnp.float32
BF16 = jnp.bfloat16

D_MODEL = 1024
N_GROUPS = 64
GROUP_SIZE = 16
STATE_DIM = 64
HEAD_DIM = 64
N_HEADS = 16
N_KV_HEADS = 4
WINDOW = 128
ROPE_THETA = 10000.0
D_FF = 4 * D_MODEL
EPS = 1e-6

SUBLANES = 8
LANES = 128
MXU_TILE = 256
VMEM_LIMIT_BYTES = 56 * 1024 * 1024

N_OCTETS = D_MODEL // LANES
OCTET_STATE = (LANES // GROUP_SIZE) * STATE_DIM
STATE_LANES = 2 * OCTET_STATE * N_OCTETS

S5_TIME_CHUNK = 128
MLP_ROWS = 1024
FF_CHUNK = 1024
KV_ROWS = 1024
ATTN_BLOCKS = 8
ATTN_GROUP_BLOCKS = 2


def _const_spec(shape):
    zeros = (0,) * len(shape)
    return pl.BlockSpec(shape, lambda *_: zeros, pipeline_mode=pl.Buffered(1))


def _layer_spec(shape, layer):
    idx = (layer,) + (0,) * (len(shape) - 1)
    return pl.BlockSpec((None,) + tuple(shape[1:]), lambda *_: idx, pipeline_mode=pl.Buffered(1))


def _params(sem):
    return pltpu.CompilerParams(dimension_semantics=sem, vmem_limit_bytes=VMEM_LIMIT_BYTES)


def _rms(x, g):
    return x * lax.rsqrt(jnp.mean(x * x, axis=-1, keepdims=True) + EPS) * g


def _dot(a, b):
    return jnp.dot(a, b, preferred_element_type=F32)


def _seg_mean_sq(x, ones_ref):
    sq = x * x
    hi = sq.astype(BF16)
    lo = (sq - hi.astype(F32)).astype(BF16)
    ones = ones_ref[...]
    outs = []
    for t in range(x.shape[-1] // MXU_TILE):
        sl = slice(MXU_TILE * t, MXU_TILE * (t + 1))
        outs.append(_dot(hi[:, sl], ones) + _dot(lo[:, sl], ones))
    return jnp.concatenate(outs, axis=-1) * (1.0 / HEAD_DIM)


def _tile_lanes(t, n):
    return jnp.concatenate([t] * (n // t.shape[-1]), axis=-1)


def _rope(x, cos, sin_signed):
    n = x.shape[-1]
    lane = lax.broadcasted_iota(jnp.int32, x.shape, 1)
    first = (lane & (HEAD_DIM - 1)) < (HEAD_DIM // 2)
    partner = jnp.where(first, pltpu.roll(x, n - HEAD_DIM // 2, 1), pltpu.roll(x, HEAD_DIM // 2, 1))
    return x * cos + partner * sin_signed


def _mlp_rows(x, g_ref, w1_ref, w2_ref):
    h = _rms(x, g_ref[...]).astype(BF16)
    acc = x
    for c in range(D_FF // FF_CHUNK):
        sl = slice(FF_CHUNK * c, FF_CHUNK * (c + 1))
        a = jnp.square(jnp.maximum(_dot(h, w1_ref[:, sl]), 0.0)).astype(BF16)
        acc = acc + _dot(a, w2_ref[sl, :])
    return acc


def _mlp_kernel(x_ref, xs_ref, g_ref, w1_ref, w2_ref, o_ref, os_ref):
    last = pl.num_programs(0) - 1

    @pl.when(pl.program_id(0) < last)
    def _():
        o_ref[...] = _mlp_rows(x_ref[...], g_ref, w1_ref, w2_ref)

    @pl.when(pl.program_id(0) == last)
    def _():
        os_ref[...] = _mlp_rows(xs_ref[...], g_ref, w1_ref, w2_ref)


def _mlp(x, xs, g, w1, w2, layer):
    rows = x.shape[0]
    tm = MLP_ROWS
    tiles = rows // tm
    x_spec = pl.BlockSpec((tm, D_MODEL), lambda i: (jnp.minimum(i, tiles - 1), 0))
    xs_spec = pl.BlockSpec(xs.shape, lambda i: (0, 0))
    return pl.pallas_call(
        _mlp_kernel,
        grid=(tiles + 1,),
        in_specs=[x_spec, xs_spec,
                  _layer_spec(g.shape, layer),
                  _layer_spec(w1.shape, layer),
                  _layer_spec(w2.shape, layer)],
        out_specs=[x_spec, xs_spec],
        out_shape=[jax.ShapeDtypeStruct((rows, D_MODEL), F32), jax.ShapeDtypeStruct(xs.shape, F32)],
        compiler_params=_params(("arbitrary",)),
        name="mlp",
    )(x, xs, g, w1, w2)


def _s5_gate(y, wglu_ref, bglu_ref):
    g = jax.nn.gelu(y).astype(BF16)
    z = _dot(g, wglu_ref[...]) + bglu_ref[...]
    return z[:, :D_MODEL] * jax.nn.sigmoid(z[:, D_MODEL:])


def _slabs(ref, rows=slice(None)):
    return jnp.concatenate([ref[c, rows, :] for c in range(ref.shape[0])], axis=-1)


def _s5_prompt_kernel(xa_ref, xb_ref, gn_ref, b2_ref, lam_ref, c2_ref, m0_ref, d_ref, wglu_ref, bglu_ref,
                      o_ref, fin_ref, st_ref, carry_ref, nat_ref, ue_ref, uo_ref, up_ref, tb2_ref, nat2_ref, y_ref):
    batch, tc, _ = xa_ref.shape
    assert 2 * batch == SUBLANES
    n_slabs = D_MODEL // LANES
    pairs = tc // 2
    rows_k = pairs * batch
    step = pl.program_id(0)
    n_chunks = pl.num_programs(0) - 1
    live = step < n_chunks

    @pl.when(step == 0)
    def _():
        carry_ref[...] = jnp.zeros_like(carry_ref)
        y_ref[...] = jnp.zeros_like(y_ref)
        up_ref[...] = jnp.zeros_like(up_ref)

    g_prev = jax.nn.gelu(y_ref[...]).astype(BF16)
    glu_cols = 2 * D_MODEL // (N_OCTETS // 2)
    z = []

    def glu_part(c):
        sl = slice(glu_cols * c, glu_cols * (c + 1))
        z.append(_dot(g_prev, wglu_ref[:, sl]) + bglu_ref[:, sl])

    def finish_prev():
        zs = jnp.concatenate(z, axis=-1)
        upd = zs[:, :D_MODEL] * jax.nn.sigmoid(zs[:, D_MODEL:])
        for c in range(n_slabs):
            tb2_ref[c] = upd[:, LANES * c:LANES * (c + 1)]
        for b in range(batch):
            for c in range(n_slabs):
                nat2_ref[c, pl.ds(0, pairs, stride=2), :] = tb2_ref[c, pl.ds(b, pairs, stride=batch), :]
                nat2_ref[c, pl.ds(1, pairs, stride=2), :] = tb2_ref[c, pl.ds(rows_k + b, pairs, stride=batch), :]
            o_ref[b] = xb_ref[b] + _slabs(nat2_ref)

    for c in range(n_slabs):
        up_ref[c, 0:batch, :] = up_ref[c, rows_k:rows_k + batch, :]
    for b in range(batch):
        u_nat = _rms(xa_ref[b], gn_ref[...])
        for c in range(n_slabs):
            nat_ref[c] = u_nat[:, LANES * c:LANES * (c + 1)]
        for c in range(n_slabs):
            even = nat_ref[c, pl.ds(0, pairs, stride=2), :]
            odd = nat_ref[c, pl.ds(1, pairs, stride=2), :]
            ue_ref[c, pl.ds(b, pairs, stride=batch), :] = even
            uo_ref[c, pl.ds(b, pairs, stride=batch), :] = odd
            up_ref[c, pl.ds(batch + b, pairs, stride=batch), :] = odd
    ue, uo = _slabs(ue_ref), _slabs(uo_ref)
    ue_b, uo_b, up_b = ue.astype(BF16), uo.astype(BF16), _slabs(up_ref, slice(0, rows_k)).astype(BF16)

    lo = lax.broadcasted_iota(jnp.int32, (SUBLANES, LANES), 0) < batch
    swap = lambda v: pltpu.roll(v, batch, 0)
    n_vregs = rows_k // SUBLANES
    half = OCTET_STATE // LANES
    y_even, y_odd = [], []
    for q in range(N_OCTETS):
        ch = slice(LANES * q, LANES * (q + 1))
        lhs = jnp.concatenate([up_b[:, ch], ue_b[:, ch]], axis=1)
        st_ref[:, 2 * OCTET_STATE * q:2 * OCTET_STATE * (q + 1)] = _dot(lhs, b2_ref[q])

    def project_out(octets):
        for q in octets:
            ch = slice(LANES * q, LANES * (q + 1))
            y2 = _dot(st_ref[:, 2 * OCTET_STATE * q:2 * OCTET_STATE * (q + 1)].astype(BF16), c2_ref[q])
            y_even.append(y2[:, :LANES])
            y_odd.append(y2[:, LANES:] + _dot(uo_b[:, ch], m0_ref[q]))

    for q0 in range(0, N_OCTETS, 2):
        octets = (q0, q0 + 1)
        glu_part(q0 // 2)
        if q0 > 0:
            project_out((q0 - 2, q0 - 1))
        if q0 == N_OCTETS - 2:
            finish_prev()
        chains = []
        for q in octets:
            for j in range(half // 2):
                cols = [2 * OCTET_STATE * q + LANES * jj for jj in (j, j + half // 2)]
                lanes = [OCTET_STATE * q + LANES * jj for jj in (j, j + half // 2)]
                chains.append((cols, lanes))
        lam_r, lam_i, sr, si, old = [], [], [], [], []
        for (ca, cb), (la, lb) in chains:
            bc = lambda r, c: jnp.broadcast_to(lam_ref[r:r + 1, c:c + LANES], (SUBLANES, LANES))
            lam_r.append(jnp.where(lo, bc(2, la), bc(2, lb)))
            lam_i.append(jnp.where(lo, bc(3, la), bc(3, lb)))
            prev = [carry_ref[:, c:c + LANES] for c in (ca, cb, ca + OCTET_STATE, cb + OCTET_STATE)]
            old.append(prev)
            sr.append(jnp.where(lo, swap(prev[0]), prev[1]))
            si.append(jnp.where(lo, swap(prev[2]), prev[3]))
        last = [None] * len(chains)
        for m in range(n_vregs):
            rows = slice(SUBLANES * m, SUBLANES * (m + 1))
            for j, ((ca, cb), _) in enumerate(chains):
                ar, br = st_ref[rows, ca:ca + LANES], st_ref[rows, cb:cb + LANES]
                ai, bi = (st_ref[rows, ca + OCTET_STATE:ca + OCTET_STATE + LANES],
                          st_ref[rows, cb + OCTET_STATE:cb + OCTET_STATE + LANES])
                d0r, d1r = jnp.where(lo, ar, swap(br)), jnp.where(lo, swap(ar), br)
                d0i, d1i = jnp.where(lo, ai, swap(bi)), jnp.where(lo, swap(ai), bi)
                s0r = lam_r[j] * sr[j] - lam_i[j] * si[j] + d0r
                s0i = lam_r[j] * si[j] + lam_i[j] * sr[j] + d0i
                s1r = lam_r[j] * s0r - lam_i[j] * s0i + d1r
                s1i = lam_r[j] * s0i + lam_i[j] * s0r + d1i
                sr[j], si[j] = s1r, s1i
                out = (jnp.where(lo, s0r, swap(s1r)), jnp.where(lo, swap(s0r), s1r),
                       jnp.where(lo, s0i, swap(s1i)), jnp.where(lo, swap(s0i), s1i))
                st_ref[rows, ca:ca + LANES] = out[0]
                st_ref[rows, cb:cb + LANES] = out[1]
                st_ref[rows, ca + OCTET_STATE:ca + OCTET_STATE + LANES] = out[2]
                st_ref[rows, cb + OCTET_STATE:cb + OCTET_STATE + LANES] = out[3]
                last[j] = out
        for j, ((ca, cb), _) in enumerate(chains):
            for c, new, prev in zip((ca, cb, ca + OCTET_STATE, cb + OCTET_STATE), last[j], old[j]):
                carry_ref[:, c:c + LANES] = jnp.where(live, new, prev)
    project_out((N_OCTETS - 2, N_OCTETS - 1))
    y_ref[0:rows_k, :] = jnp.concatenate(y_even, axis=-1) + d_ref[...] * ue
    y_ref[rows_k:2 * rows_k, :] = jnp.concatenate(y_odd, axis=-1) + d_ref[...] * uo

    @pl.when(step == n_chunks - 1)
    def _():
        tail = uo_b[rows_k - 2 * SUBLANES:, :]
        zeros = jnp.zeros((2 * SUBLANES, LANES), BF16)
        for q in range(N_OCTETS):
            bu = _dot(jnp.concatenate([zeros, tail[:, LANES * q:LANES * (q + 1)]], axis=1), b2_ref[q])[SUBLANES:]
            base = 2 * OCTET_STATE * q
            lr = lam_ref[0:1, OCTET_STATE * q:OCTET_STATE * (q + 1)]
            li = lam_ref[1:2, OCTET_STATE * q:OCTET_STATE * (q + 1)]
            s_r = carry_ref[:, base:base + OCTET_STATE]
            s_i = carry_ref[:, base + OCTET_STATE:base + 2 * OCTET_STATE]
            fin_ref[:, base:base + OCTET_STATE] = lr * s_r - li * s_i + bu[:, :OCTET_STATE]
            fin_ref[:, base + OCTET_STATE:base + 2 * OCTET_STATE] = lr * s_i + li * s_r + bu[:, OCTET_STATE:]


def _s5_prompt(x, layer, gn, b2, lam, c2, m0, d, wglu, bglu):
    batch, seq, _ = x.shape
    tc = S5_TIME_CHUNK
    n_chunks = seq // tc
    rows_k = tc // 2 * batch
    n_slabs = D_MODEL // LANES
    blk = (batch, tc, D_MODEL)
    this_chunk = pl.BlockSpec(blk, lambda s: (0, jnp.minimum(s, n_chunks - 1), 0))
    prev_chunk = pl.BlockSpec(blk, lambda s: (0, jnp.maximum(s - 1, 0), 0))
    slab = lambda rows: pltpu.VMEM((n_slabs, rows, LANES), F32)
    return pl.pallas_call(
        _s5_prompt_kernel,
        grid=(n_chunks + 1,),
        in_specs=[this_chunk, prev_chunk]
        + [_layer_spec(w.shape, layer) for w in (gn, b2, lam, c2, m0, d, wglu, bglu)],
        out_specs=[prev_chunk, pl.BlockSpec((SUBLANES, STATE_LANES), lambda s: (0, 0))],
        out_shape=[jax.ShapeDtypeStruct(x.shape, F32),
                   jax.ShapeDtypeStruct((SUBLANES, STATE_LANES), F32)],
        scratch_shapes=[pltpu.VMEM((rows_k, STATE_LANES), F32),
                        pltpu.VMEM((SUBLANES, STATE_LANES), F32),
                        slab(tc), slab(rows_k), slab(rows_k), slab(rows_k + SUBLANES),
                        slab(2 * rows_k), slab(tc),
                        pltpu.VMEM((2 * rows_k, D_MODEL), F32)],
        compiler_params=_params(("arbitrary",)),
        name="s5_prompt",
    )(x, x, gn, b2, lam, c2, m0, d, wglu, bglu)


def _s5_sample_kernel(x_ref, gn_ref, b2_ref, lam_ref, c2_ref, d_ref, wglu_ref, bglu_ref, s0_ref,
                      o_ref, snew_ref):
    x = x_ref[...]
    u = _rms(x, gn_ref[...])
    ys = []
    for q in range(N_OCTETS):
        base = 2 * OCTET_STATE * q
        bu = jnp.dot(u[:, LANES * q:LANES * (q + 1)], b2_ref[q, LANES:, :],
                     preferred_element_type=F32, precision=lax.Precision.HIGHEST)
        lr = lam_ref[0:1, OCTET_STATE * q:OCTET_STATE * (q + 1)]
        li = lam_ref[1:2, OCTET_STATE * q:OCTET_STATE * (q + 1)]
        s0r = s0_ref[:, base:base + OCTET_STATE]
        s0i = s0_ref[:, base + OCTET_STATE:base + 2 * OCTET_STATE]
        sr = lr * s0r - li * s0i + bu[:, :OCTET_STATE]
        si = lr * s0i + li * s0r + bu[:, OCTET_STATE:]
        snew_ref[:, base:base + OCTET_STATE] = sr
        snew_ref[:, base + OCTET_STATE:base + 2 * OCTET_STATE] = si
        sb = jnp.concatenate([sr, si], axis=-1).astype(BF16)
        ys.append(_dot(sb, c2_ref[q, :, :LANES]))
    y = jnp.concatenate(ys, axis=-1)
    o_ref[...] = x + _s5_gate(y + d_ref[...] * u, wglu_ref, bglu_ref)


def _s5_sample(x, layer, gn, b2_f32, lam, c2, d, wglu, bglu, s0):
    rows = x.shape[0]
    return pl.pallas_call(
        _s5_sample_kernel,
        grid=(1,),
        in_specs=[_const_spec(x.shape)] + [_layer_spec(w.shape, layer) for w in (gn, b2_f32, lam, c2, d, wglu, bglu)]
        + [_const_spec(s0.shape)],
        out_specs=[pl.BlockSpec((rows, D_MODEL), lambda i: (0, 0)),
                   pl.BlockSpec((rows, STATE_LANES), lambda i: (0, 0))],
        out_shape=[jax.ShapeDtypeStruct((rows, D_MODEL), F32),
                   jax.ShapeDtypeStruct((rows, STATE_LANES), F32)],
        compiler_params=_params(("arbitrary",)),
        name="s5_sample",
    )(x, gn, b2_f32, lam, c2, d, wglu, bglu, s0)


def _block_diag(w):
    n, octets, groups, r, c = w.shape
    col = jnp.arange(groups * c)
    tile = (jnp.arange(c)[:, None] == col % c).astype(F32)
    own = (jnp.arange(groups * r)[:, None] // r == col // c).astype(F32)
    tiled = jnp.einsum("nqrc,cl->nqrl", w.reshape(n, octets, groups * r, c), tile, precision=lax.Precision.HIGHEST)
    return tiled * own


def _s5_weights(a_re, a_im, log_dt, b_re, b_im, c_re, c_im):
    n = a_re.shape[0]
    dt = jnp.exp(log_dt.astype(F32))
    lam = lax.complex(a_re.astype(F32), a_im.astype(F32))
    lam_bar = jnp.exp(lam * dt)
    b = lax.complex(b_re.astype(F32), b_im.astype(F32))
    b_bar = ((lam_bar - 1.0) / lam)[..., None] * b
    c = lax.complex(c_re.astype(F32), c_im.astype(F32))
    gpo = N_GROUPS // N_OCTETS
    octets = lambda w: w.reshape((n, N_OCTETS, gpo) + w.shape[2:])

    def into_state(w):
        wt = jnp.swapaxes(octets(w), 3, 4)
        return jnp.concatenate([_block_diag(jnp.real(wt)), _block_diag(jnp.imag(wt))], axis=-1)

    def out_of_state(w):
        wt = jnp.swapaxes(octets(w), 3, 4)
        return jnp.concatenate([_block_diag(jnp.real(wt)), -_block_diag(jnp.imag(wt))], axis=2)

    b2 = jnp.concatenate([into_state(lam_bar[..., None] * b_bar), into_state(b_bar)], axis=2)
    c2 = jnp.concatenate([out_of_state(c), out_of_state(c * lam_bar[:, :, None, :])], axis=-1)
    cb = jnp.real(jnp.einsum("ngcp,ngpd->ngdc", c, b_bar))
    m0 = _block_diag(octets(cb))
    lam2 = lam_bar * lam_bar
    lam_rows = jnp.stack([jnp.real(lam_bar).reshape(n, -1), jnp.imag(lam_bar).reshape(n, -1),
                          jnp.real(lam2).reshape(n, -1), jnp.imag(lam2).reshape(n, -1)], axis=1)
    return b2, c2.astype(BF16), m0.astype(BF16), lam_rows


def _state_to_lanes(s_re, s_im):
    n = s_re.shape[0]
    re = s_re.astype(F32).reshape(n, N_OCTETS, OCTET_STATE)
    im = s_im.astype(F32).reshape(n, N_OCTETS, OCTET_STATE)
    return jnp.concatenate([re, im], axis=-1).reshape(n, STATE_LANES)


def _lanes_to_state(s):
    n = s.shape[0]
    s4 = s.reshape(n, N_OCTETS, 2, OCTET_STATE)
    return (s4[:, :, 0].reshape(n, N_GROUPS, STATE_DIM), s4[:, :, 1].reshape(n, N_GROUPS, STATE_DIM))


def _rope_partner(n):
    lane = jnp.arange(n)
    return jnp.where(lane % HEAD_DIM < HEAD_DIM // 2, lane + HEAD_DIM // 2, lane - HEAD_DIM // 2)


def _kv_kernel(x_ref, g_ref, wkv_ref, kg_ref, kgp_ref, ones_ref, cos_ref, sin_ref, *rest):
    if len(rest) == 3:
        wvt_ref, k_ref, vt_ref = rest
        v_ref = None
    else:
        (k_ref, v_ref), wvt_ref, vt_ref = rest, None, None
    h = _rms(x_ref[...], g_ref[...]).astype(BF16)
    kv = _dot(h, wkv_ref[...])
    n = k_ref.shape[-1]
    k, k_partner = kv[:, :n], kv[:, n:2 * n]
    scale = lax.rsqrt(_seg_mean_sq(k, ones_ref) + EPS)
    k_ref[...] = scale * (k * (kg_ref[...] * _tile_lanes(cos_ref[...], n))
                          + k_partner * (kgp_ref[...] * _tile_lanes(sin_ref[...], n)))
    if v_ref is not None:
        v_ref[...] = kv[:, 2 * n:]
    else:
        vt_ref[...] = lax.dot_general(wvt_ref[...], h, (((1,), (1,)), ((), ())), preferred_element_type=F32)


def _kv(x, g, wk, wv, kgain, ones, cos, sin, table_block, seq=None):
    rows = x.shape[0]
    n, nv = wk.shape[1], wv.shape[1]
    partner = _rope_partner(n)
    wkv = jnp.concatenate([wk, wk[:, partner]] + ([wv] if seq is None else []), axis=1).astype(BF16)
    tm = min(KV_ROWS, rows)
    in_specs = [pl.BlockSpec((tm, D_MODEL), lambda i: (i, 0)),
                _const_spec((1, D_MODEL)),
                _const_spec(wkv.shape),
                _const_spec((1, n)),
                _const_spec((1, n)),
                _const_spec(ones.shape),
                pl.BlockSpec((tm, LANES), lambda i: (table_block(i), 0)),
                pl.BlockSpec((tm, LANES), lambda i: (table_block(i), 0))]
    out_specs = [pl.BlockSpec((tm, n), lambda i: (i, 0))]
    out_shape = [jax.ShapeDtypeStruct((rows, n), F32)]
    args = [x, g, wkv, kgain, kgain[:, partner], ones, cos, sin]
    if seq is None:
        out_specs.append(pl.BlockSpec((tm, nv), lambda i: (i, 0)))
        out_shape.append(jax.ShapeDtypeStruct((rows, nv), F32))
    else:
        per_seq = seq // tm
        in_specs.append(_const_spec((nv, D_MODEL)))
        args.append(wv.T.astype(BF16))
        out_specs.append(pl.BlockSpec((None, nv, tm), lambda i: (i // per_seq, 0, i % per_seq)))
        out_shape.append(jax.ShapeDtypeStruct((rows // seq, nv, seq), F32))
    return pl.pallas_call(
        _kv_kernel,
        grid=(rows // tm,),
        in_specs=in_specs,
        out_specs=out_specs,
        out_shape=out_shape,
        compiler_params=_params(("parallel",)),
        name="kv_proj",
    )(*args)


def _attn_prompt_kernel(sinks_ref, x_ref, g_ref, wq_ref, qg_ref, ones_ref, cos_ref, sin_ref,
                        kp_ref, kc_ref, vp_ref, vc_ref, wo_ref, o_ref):
    tile = pl.program_id(1)
    x = x_ref[...]
    n_blocks = x.shape[0] // WINDOW
    heads_per_kv = N_HEADS // N_KV_HEADS
    cols = heads_per_kv * WINDOW
    h = _rms(x, g_ref[...]).astype(BF16)
    q = _dot(h, wq_ref[...])
    q = q * lax.rsqrt(_seg_mean_sq(q, ones_ref) + EPS) * qg_ref[...]
    q = _rope(q, _tile_lanes(cos_ref[...], D_MODEL), _tile_lanes(sin_ref[...], D_MODEL))
    qb = (q * (HEAD_DIM ** -0.5)).astype(BF16)

    kall = jnp.concatenate([kp_ref[...], kc_ref[...]], axis=0).astype(BF16)
    vall = jnp.concatenate([vp_ref[...], vc_ref[...]], axis=1).astype(BF16)

    kj = lax.broadcasted_iota(jnp.int32, (WINDOW, cols), 0)
    col = lax.broadcasted_iota(jnp.int32, (WINDOW, cols), 1)
    from_prev = kj > (col & (WINDOW - 1))
    col_head = lax.broadcasted_iota(jnp.int32, (1, cols), 1) // WINDOW
    low_half = lax.broadcasted_iota(jnp.int32, (WINDOW, LANES), 1) < HEAD_DIM
    neg_inf = jnp.full((WINDOW, cols), -jnp.inf, F32)

    all_pairs = [(blk, g) for blk in range(n_blocks) for g in range(N_KV_HEADS)]
    head_outs = [[] for _ in range(n_blocks)]
    group = ATTN_GROUP_BLOCKS * N_KV_HEADS
    for p0 in range(0, len(all_pairs), group):
        pairs = all_pairs[p0:p0 + group]
        scores, sinks = [], []
        for blk, g in pairs:
            r0 = WINDOW * blk
            kg = kall[r0:r0 + 2 * WINDOW, LANES * g:LANES * (g + 1)]
            qs = []
            sink = jnp.zeros((1, cols), F32)
            for hh in range(heads_per_kv):
                head = heads_per_kv * g + hh
                qp = qb[r0:r0 + WINDOW, LANES * (head // 2):LANES * (head // 2 + 1)]
                keep = low_half if head % 2 == 0 else jnp.logical_not(low_half)
                qs.append(jnp.where(keep, qp, jnp.zeros_like(qp)))
                sink = jnp.where(col_head == hh, sinks_ref[head], sink)
            sinks.append(sink)
            scores.append(lax.dot_general(kg, jnp.concatenate(qs, axis=0), (((1,), (1,)), ((), ())),
                                          preferred_element_type=F32))
        probs, denoms = [], []
        for (blk, g), s, sink in zip(pairs, scores, sinks):
            has_prev = tile * n_blocks + blk > 0
            t = jnp.where(from_prev, jnp.where(has_prev, s[:WINDOW], neg_inf), s[WINDOW:])
            m = jnp.maximum(jnp.max(t, axis=0, keepdims=True), sink)
            p = jnp.exp(t - m)
            denoms.append(jnp.sum(p, axis=0, keepdims=True) + jnp.exp(sink - m))
            zero = jnp.zeros_like(p)
            probs.append(jnp.concatenate([jnp.where(from_prev, p, zero), jnp.where(from_prev, zero, p)],
                                         axis=0).astype(BF16))
        for (blk, g), pcat, denom in zip(pairs, probs, denoms):
            r0 = WINDOW * blk
            vg = vall[HEAD_DIM * g:HEAD_DIM * (g + 1), r0:r0 + 2 * WINDOW]
            o = _dot(vg, pcat) / denom
            head_outs[blk] += [o[:, WINDOW * hh:WINDOW * (hh + 1)] for hh in range(heads_per_kv)]
    block_outs = [jnp.concatenate(outs, axis=0).T for outs in head_outs]
    attn = jnp.concatenate(block_outs, axis=0).astype(BF16)
    o_ref[...] = x + _dot(attn, wo_ref[...])


def _attn_prompt(x, g, wq, qgain, ones, cos, sin, kd, vt, sinks, wo):
    bsz, seq, _ = x.shape
    nk, nv = kd.shape[-1], vt.shape[1]
    nb = ATTN_BLOCKS
    tq = nb * WINDOW
    x_spec = pl.BlockSpec((None, tq, D_MODEL), lambda b, i, *_: (b, i, 0))
    k_cur = pl.BlockSpec((None, tq, nk), lambda b, i, *_: (b, i, 0))
    k_prev = pl.BlockSpec((None, WINDOW, nk), lambda b, i, *_: (b, jnp.maximum(nb * i - 1, 0), 0))
    v_cur = pl.BlockSpec((None, nv, tq), lambda b, i, *_: (b, 0, i))
    v_prev = pl.BlockSpec((None, nv, WINDOW), lambda b, i, *_: (b, 0, jnp.maximum(nb * i - 1, 0)))
    tab = pl.BlockSpec((tq, LANES), lambda b, i, *_: (i, 0))
    grid_spec = pltpu.PrefetchScalarGridSpec(
        num_scalar_prefetch=1,
        grid=(bsz, seq // tq),
        in_specs=[x_spec,
                  _const_spec((1, D_MODEL)),
                  _const_spec(wq.shape),
                  _const_spec((1, D_MODEL)),
                  _const_spec(ones.shape),
                  tab, tab, k_prev, k_cur, v_prev, v_cur,
                  _const_spec(wo.shape)],
        out_specs=x_spec)
    return pl.pallas_call(
        _attn_prompt_kernel,
        grid_spec=grid_spec,
        out_shape=jax.ShapeDtypeStruct(x.shape, F32),
        compiler_params=_params(("parallel", "arbitrary")),
        name="attn_prompt",
    )(sinks, x, g, wq, qgain, ones, cos, sin, kd, kd, vt, vt, wo)


def _own_head(shape):
    row_head = lax.broadcasted_iota(jnp.int32, shape, 0) & (N_HEADS - 1)
    lane_head = lax.broadcasted_iota(jnp.int32, shape, 1) // HEAD_DIM
    return row_head == lane_head


def _q_sample_kernel(x_ref, g_ref, wq_ref, qg_ref, ones_ref, cos_ref, sin_ref, rep_ref, place_ref, q_ref):
    h = _rms(x_ref[...], g_ref[...]).astype(BF16)
    q = _dot(h, wq_ref[...])
    q = q * lax.rsqrt(_seg_mean_sq(q, ones_ref) + EPS) * qg_ref[...]
    q = _rope(q, _tile_lanes(cos_ref[...], D_MODEL), _tile_lanes(sin_ref[...], D_MODEL))
    qb = (q * (HEAD_DIM ** -0.5)).astype(BF16)
    q_rep = _dot(rep_ref[...], qb)
    q_own = jnp.where(_own_head(q_rep.shape), q_rep, 0.0).astype(BF16)
    q_ref[...] = _dot(q_own, place_ref[...])


def _q_sample(x, g, wq, qgain, ones, cos, sin, rep, place):
    rows = x.shape[0]
    return pl.pallas_call(
        _q_sample_kernel,
        out_shape=jax.ShapeDtypeStruct((rows * N_HEADS, place.shape[1]), F32),
        compiler_params=pltpu.CompilerParams(vmem_limit_bytes=VMEM_LIMIT_BYTES),
        name="q_sample",
    )(x, g, wq, qgain, ones, cos, sin, rep, place)


def _cache_append_kernel(k_ref, v_ref, kn_ref, vn_ref, ko_ref, vo_ref):
    nb, lanes_kv, keys = k_ref.shape
    first = pl.program_id(0) * nb
    key = lax.broadcasted_iota(jnp.int32, (lanes_kv, keys), 1)
    seq = lax.broadcasted_iota(jnp.int32, (kn_ref.shape[1], keys), 0)

    def split3(x):
        hi = x.astype(BF16)
        r = x - hi.astype(F32)
        mid = r.astype(BF16)
        return hi, mid, (r - mid.astype(F32)).astype(BF16)

    parts = [(split3(kn_ref[...]), k_ref, ko_ref), (split3(vn_ref[...]), v_ref, vo_ref)]
    for b in range(nb):
        pick = (seq == first + b).astype(BF16)
        for (hi, mid, lo), old_ref, out_ref in parts:
            new_col = _dot(hi, pick) + _dot(mid, pick) + _dot(lo, pick)
            out_ref[b] = jnp.where(key == keys - 1, new_col, pltpu.roll(old_ref[b], keys - 1, 1))


def _cache_append(kt, vt, kn_t, vn_t):
    bsz, lanes_kv, keys = kt.shape
    nb = 2 * SUBLANES
    blk = pl.BlockSpec((nb, lanes_kv, keys), lambda i: (i, 0, 0))
    return pl.pallas_call(
        _cache_append_kernel,
        grid=(bsz // nb,),
        in_specs=[blk, blk, _const_spec(kn_t.shape), _const_spec(vn_t.shape)],
        out_specs=[blk, blk],
        out_shape=[jax.ShapeDtypeStruct(kt.shape, F32), jax.ShapeDtypeStruct(vt.shape, F32)],
        compiler_params=_params(("parallel",)),
        name="cache_append",
    )(kt, vt, kn_t, vn_t)


def _attn_sample_kernel(q_ref, k_ref, v_ref, sink_ref, o_ref):
    sink = sink_ref[...]
    for b in range(q_ref.shape[0]):
        s = _dot(q_ref[b].astype(BF16), k_ref[b].astype(BF16))
        m = jnp.maximum(jnp.max(s, axis=-1, keepdims=True), sink)
        p = jnp.exp(s - m)
        denom = jnp.sum(p, axis=-1, keepdims=True) + jnp.exp(sink - m)
        o = lax.dot_general(p.astype(BF16), v_ref[b].astype(BF16), (((1,), (1,)), ((), ())),
                            preferred_element_type=F32)
        o_ref[b] = o / denom


def _attn_sample(q3, kt, vt, sink_col):
    bsz, nh, nkv = q3.shape
    keys = kt.shape[2]
    bb = 4 * SUBLANES
    return pl.pallas_call(
        _attn_sample_kernel,
        grid=(bsz // bb,),
        in_specs=[pl.BlockSpec((bb, nh, nkv), lambda i: (i, 0, 0)),
                  pl.BlockSpec((bb, nkv, keys), lambda i: (i, 0, 0)),
                  pl.BlockSpec((bb, nkv, keys), lambda i: (i, 0, 0)),
                  _const_spec(sink_col.shape)],
        out_specs=pl.BlockSpec((bb, nh, nkv), lambda i: (i, 0, 0)),
        out_shape=jax.ShapeDtypeStruct((bsz, nh, nkv), F32),
        compiler_params=_params(("parallel",)),
        name="attn_sample",
    )(q3, kt, vt, sink_col)


def _proj_residual_kernel(x_ref, a_ref, place_t_ref, rep_t_ref, w_ref, o_ref):
    z = _dot(a_ref[...].astype(BF16), place_t_ref[...])
    z_own = jnp.where(_own_head(z.shape), z, 0.0).astype(BF16)
    attn = _dot(rep_t_ref[...], z_own).astype(BF16)
    o_ref[...] = x_ref[...] + _dot(attn, w_ref[...])


def _proj_residual(x, a, place_t, rep_t, w):
    return pl.pallas_call(
        _proj_residual_kernel,
        out_shape=jax.ShapeDtypeStruct(x.shape, F32),
        compiler_params=pltpu.CompilerParams(vmem_limit_bytes=VMEM_LIMIT_BYTES),
        name="o_proj_sample",
    )(x, a, place_t, rep_t, w)


def _rope_tables(pos):
    half = HEAD_DIM // 2
    inv = ROPE_THETA ** (-jnp.arange(half, dtype=F32) / half)
    ang = pos.astype(F32)[:, None] * inv[None, :]
    cos, sin = jnp.cos(ang), jnp.sin(ang)
    reps = LANES // HEAD_DIM
    return (jnp.tile(jnp.concatenate([cos, cos], axis=-1), (1, reps)),
            jnp.tile(jnp.concatenate([-sin, sin], axis=-1), (1, reps)))


def _block_ones(width, block):
    idx = jnp.arange(width) // block
    return (idx[:, None] == idx[None, :]).astype(BF16)


def _dup_heads(w):
    w3 = w.reshape(w.shape[0], N_KV_HEADS, 1, HEAD_DIM)
    return jnp.broadcast_to(w3, (w.shape[0], N_KV_HEADS, LANES // HEAD_DIM, HEAD_DIM)).reshape(w.shape[0], -1)


def _head_placement():
    lane = jnp.arange(N_HEADS * HEAD_DIM)
    dst = (lane // HEAD_DIM) // (N_HEADS // N_KV_HEADS) * HEAD_DIM + lane % HEAD_DIM
    return (dst[:, None] == jnp.arange(N_KV_HEADS * HEAD_DIM)[None, :]).astype(BF16)


def _row_replication(n):
    return (jnp.arange(n * N_HEADS)[:, None] // N_HEADS == jnp.arange(n)[None, :]).astype(BF16)


def kernel(x_prompt, x_sample, state_ssm_re, state_ssm_im, cache_k, cache_v, norm_mix, norm_mlp, ssm_a_re, ssm_a_im, ssm_log_dt, ssm_b_re, ssm_b_im, ssm_c_re, ssm_c_im, ssm_d, w_glu, b_glu, norm_kv, w_k, w_v, k_norm, w_q, q_norm, attn_sinks, w_o, w_mlp_in, w_mlp_out):
    bsz, seq, _ = x_prompt.shape
    dec = x_sample.shape[0]
    n_a = ssm_a_re.shape[0]
    depth = norm_mix.shape[0]
    past = 8192
    nkv = N_KV_HEADS * HEAD_DIM
    row = lambda v: v.astype(F32).reshape(1, -1)

    xp = x_prompt.astype(F32)
    xs = x_sample.reshape(dec, D_MODEL)
    rows3 = lambda v: v.astype(F32).reshape(v.shape[0], 1, -1)

    w1 = w_mlp_in.astype(BF16)
    w2 = w_mlp_out.astype(BF16)
    g_mlp = rows3(norm_mlp)
    def mlp_both(x, xs, layer):
        y, ys = _mlp(x.reshape(bsz * seq, D_MODEL), xs, g_mlp, w1, w2, layer)
        return y.reshape(bsz, seq, D_MODEL), ys


    b2, c2, m0, lam = _s5_weights(ssm_a_re, ssm_a_im, ssm_log_dt, ssm_b_re, ssm_b_im, ssm_c_re, ssm_c_im)
    b2_16 = b2.astype(BF16)
    wglu = w_glu.astype(BF16)
    g_mix, d_skip, bg = rows3(norm_mix), rows3(ssm_d), rows3(b_glu)
    sp_re, sp_im, ss_re, ss_im = [], [], [], []
    for i in range(n_a):
        xp, fin = _s5_prompt(xp, i, g_mix, b2_16, lam, c2, m0, d_skip, wglu, bg)
        re, im = _lanes_to_state(fin[bsz:])
        sp_re.append(re)
        sp_im.append(im)
        s0 = _state_to_lanes(state_ssm_re[i], state_ssm_im[i])
        xs, snew = _s5_sample(xs, i, g_mix, b2, lam, c2, d_skip, wglu, bg, s0)
        re, im = _lanes_to_state(snew)
        ss_re.append(re)
        ss_im.append(im)
        xp, xs = mlp_both(xp, xs, i)

    cos_p, sin_p = _rope_tables(jnp.arange(seq, dtype=jnp.int32))
    cos_s, sin_s = _rope_tables(jnp.full((dec,), past, dtype=jnp.int32))
    ones_head = _block_ones(MXU_TILE, HEAD_DIM)
    place = _head_placement()
    rep = _row_replication(dec)
    kgain = row(k_norm)
    tile_gain = lambda gvec, n: jnp.tile(gvec, (1, n // HEAD_DIM))

    kd, vt = _kv(xp.reshape(bsz * seq, D_MODEL), row(norm_kv), _dup_heads(w_k), w_v,
                     tile_gain(kgain, 2 * nkv), ones_head, cos_p, sin_p, lambda i: i % (seq // KV_ROWS), seq=seq)
    ks_new, vs_new = _kv(xs, row(norm_kv), w_k, w_v, tile_gain(kgain, nkv), ones_head, cos_s, sin_s,
                         lambda i: i)
    kd = kd.reshape(bsz, seq, 2 * nkv)
    new_k_p = kd[:, -WINDOW:].reshape(bsz, WINDOW, N_KV_HEADS, LANES // HEAD_DIM, HEAD_DIM)[:, :, :, 0]
    new_v_p = jnp.transpose(vt[:, :, -WINDOW:].reshape(bsz, N_KV_HEADS, HEAD_DIM, WINDOW), (0, 3, 1, 2))
    key_minor = lambda c: jnp.transpose(c.astype(F32), (0, 2, 3, 1)).reshape(dec, nkv, c.shape[1])
    keys_s, vals_s = _cache_append(key_minor(cache_k), key_minor(cache_v), ks_new.T, vs_new.T)
    key_major = lambda c: jnp.transpose(c.reshape(dec, N_KV_HEADS, HEAD_DIM, c.shape[2]), (0, 3, 1, 2))
    new_k_s, new_v_s = key_major(keys_s), key_major(vals_s)

    for j in range(depth - n_a):
        layer = n_a + j
        gn = row(norm_mix[layer])
        qgain = row(q_norm[j])
        wq = w_q[j].astype(BF16)
        wo = w_o[j].astype(BF16)
        xp = _attn_prompt(xp, gn, wq, tile_gain(qgain, D_MODEL), ones_head, cos_p, sin_p, kd, vt,
                          attn_sinks[j].astype(F32), wo)
        q_pad = _q_sample(xs, gn, wq, tile_gain(qgain, D_MODEL), ones_head, cos_s, sin_s, rep, place)
        o_pad = _attn_sample(q_pad.reshape(dec, N_HEADS, nkv), keys_s, vals_s,
                             attn_sinks[j].astype(F32).reshape(N_HEADS, 1))
        xs = _proj_residual(xs, o_pad.reshape(dec * N_HEADS, nkv), place.T, rep.T, wo)
        xp, xs = mlp_both(xp, xs, layer)

    return (xp, xs.reshape(dec, 1, D_MODEL),
            jnp.stack(sp_re), jnp.stack(sp_im), new_k_p, new_v_p,
            jnp.stack(ss_re), jnp.stack(ss_im), new_k_s, new_v_s)
```

```python
import jax
import jax.numpy as jnp
from jax import lax
from jax.experimental import pallas as pl
from jax.experimental.pallas import tpu as pltpu

F32 = jnp.float32
BF16 = jnp.bfloat16

D_MODEL = 1024
N_GROUPS = 64
GROUP_SIZE = 16
STATE_DIM = 64
HEAD_DIM = 64
N_HEADS = 16
N_KV_HEADS = 4
WINDOW = 128
ROPE_THETA = 10000.0
D_FF = 4 * D_MODEL
EPS = 1e-6

SUBLANES = 8
LANES = 128
MXU_TILE = 256
VMEM_LIMIT_BYTES = 56 * 1024 * 1024

N_OCTETS = D_MODEL // LANES
OCTET_STATE = (LANES // GROUP_SIZE) * STATE_DIM
STATE_LANES = 2 * OCTET_STATE * N_OCTETS

S5_TIME_CHUNK = 128
MLP_ROWS = 1024
FF_CHUNK = 1024
KV_ROWS = 1024
ATTN_BLOCKS = 8
ATTN_GROUP_BLOCKS = 1


def _const_spec(shape):
    zeros = (0,) * len(shape)
    return pl.BlockSpec(shape, lambda *_: zeros, pipeline_mode=pl.Buffered(1))


def _layer_spec(shape, layer):
    idx = (layer,) + (0,) * (len(shape) - 1)
    return pl.BlockSpec((None,) + tuple(shape[1:]), lambda *_: idx, pipeline_mode=pl.Buffered(1))


def _params(sem):
    return pltpu.CompilerParams(dimension_semantics=sem, vmem_limit_bytes=VMEM_LIMIT_BYTES)


def _rms(x, g):
    return x * lax.rsqrt(jnp.mean(x * x, axis=-1, keepdims=True) + EPS) * g


def _dot(a, b):
    return jnp.dot(a, b, preferred_element_type=F32)


def _seg_mean_sq(x, ones_ref):
    sq = x * x
    hi = sq.astype(BF16)
    lo = (sq - hi.astype(F32)).astype(BF16)
    ones = ones_ref[...]
    outs = []
    for t in range(x.shape[-1] // MXU_TILE):
        sl = slice(MXU_TILE * t, MXU_TILE * (t + 1))
        outs.append(_dot(hi[:, sl], ones) + _dot(lo[:, sl], ones))
    return jnp.concatenate(outs, axis=-1) * (1.0 / HEAD_DIM)


def _tile_lanes(t, n):
    return jnp.concatenate([t] * (n // t.shape[-1]), axis=-1)


def _rope(x, cos, sin_signed):
    n = x.shape[-1]
    lane = lax.broadcasted_iota(jnp.int32, x.shape, 1)
    first = (lane & (HEAD_DIM - 1)) < (HEAD_DIM // 2)
    partner = jnp.where(first, pltpu.roll(x, n - HEAD_DIM // 2, 1), pltpu.roll(x, HEAD_DIM // 2, 1))
    return x * cos + partner * sin_signed


def _mlp_rows(x, g_ref, w1_ref, w2_ref):
    h = _rms(x, g_ref[...]).astype(BF16)
    acc = x
    for c in range(D_FF // FF_CHUNK):
        sl = slice(FF_CHUNK * c, FF_CHUNK * (c + 1))
        a = jnp.square(jnp.maximum(_dot(h, w1_ref[:, sl]), 0.0)).astype(BF16)
        acc = acc + _dot(a, w2_ref[sl, :])
    return acc


def _mlp_kernel(x_ref, xs_ref, g_ref, w1_ref, w2_ref, o_ref, os_ref):
    last = pl.num_programs(0) - 1

    @pl.when(pl.program_id(0) < last)
    def _():
        o_ref[...] = _mlp_rows(x_ref[...], g_ref, w1_ref, w2_ref)

    @pl.when(pl.program_id(0) == last)
    def _():
        os_ref[...] = _mlp_rows(xs_ref[...], g_ref, w1_ref, w2_ref)


def _mlp(x, xs, g, w1, w2, layer):
    rows = x.shape[0]
    tm = MLP_ROWS
    tiles = rows // tm
    x_spec = pl.BlockSpec((tm, D_MODEL), lambda i: (jnp.minimum(i, tiles - 1), 0))
    xs_spec = pl.BlockSpec(xs.shape, lambda i: (0, 0))
    return pl.pallas_call(
        _mlp_kernel,
        grid=(tiles + 1,),
        in_specs=[x_spec, xs_spec,
                  _layer_spec(g.shape, layer),
                  _layer_spec(w1.shape, layer),
                  _layer_spec(w2.shape, layer)],
        out_specs=[x_spec, xs_spec],
        out_shape=[jax.ShapeDtypeStruct((rows, D_MODEL), F32), jax.ShapeDtypeStruct(xs.shape, F32)],
        compiler_params=_params(("arbitrary",)),
        name="mlp",
    )(x, xs, g, w1, w2)


def _s5_gate(y, wglu_ref, bglu_ref):
    g = jax.nn.gelu(y).astype(BF16)
    z = _dot(g, wglu_ref[...]) + bglu_ref[...]
    return z[:, :D_MODEL] * jax.nn.sigmoid(z[:, D_MODEL:])


def _slabs(ref, rows=slice(None)):
    return jnp.concatenate([ref[c, rows, :] for c in range(ref.shape[0])], axis=-1)


def _s5_prompt_kernel(xa_ref, xb_ref, gn_ref, b2_ref, lam_ref, c2_ref, m0_ref, d_ref, wglu_ref, bglu_ref,
                      o_ref, fin_ref, st_ref, carry_ref, nat_ref, ue_ref, uo_ref, up_ref, tb2_ref, nat2_ref, y_ref):
    batch, tc, _ = xa_ref.shape
    assert 2 * batch == SUBLANES
    n_slabs = D_MODEL // LANES
    pairs = tc // 2
    rows_k = pairs * batch
    step = pl.program_id(0)
    n_chunks = pl.num_programs(0) - 1
    live = step < n_chunks

    @pl.when(step == 0)
    def _():
        carry_ref[...] = jnp.zeros_like(carry_ref)
        y_ref[...] = jnp.zeros_like(y_ref)
        up_ref[...] = jnp.zeros_like(up_ref)

    g_prev = jax.nn.gelu(y_ref[...]).astype(BF16)
    glu_cols = 2 * D_MODEL // (N_OCTETS // 2)
    z = []

    def glu_part(c):
        sl = slice(glu_cols * c, glu_cols * (c + 1))
        z.append(_dot(g_prev, wglu_ref[:, sl]) + bglu_ref[:, sl])

    def finish_prev():
        zs = jnp.concatenate(z, axis=-1)
        upd = zs[:, :D_MODEL] * jax.nn.sigmoid(zs[:, D_MODEL:])
        for c in range(n_slabs):
            tb2_ref[c] = upd[:, LANES * c:LANES * (c + 1)]
        for b in range(batch):
            for c in range(n_slabs):
                nat2_ref[c, pl.ds(0, pairs, stride=2), :] = tb2_ref[c, pl.ds(b, pairs, stride=batch), :]
                nat2_ref[c, pl.ds(1, pairs, stride=2), :] = tb2_ref[c, pl.ds(rows_k + b, pairs, stride=batch), :]
            o_ref[b] = xb_ref[b] + _slabs(nat2_ref)

    for c in range(n_slabs):
        up_ref[c, 0:batch, :] = up_ref[c, rows_k:rows_k + batch, :]
    for b in range(batch):
        u_nat = _rms(xa_ref[b], gn_ref[...])
        for c in range(n_slabs):
            nat_ref[c] = u_nat[:, LANES * c:LANES * (c + 1)]
        for c in range(n_slabs):
            even = nat_ref[c, pl.ds(0, pairs, stride=2), :]
            odd = nat_ref[c, pl.ds(1, pairs, stride=2), :]
            ue_ref[c, pl.ds(b, pairs, stride=batch), :] = even
            uo_ref[c, pl.ds(b, pairs, stride=batch), :] = odd
            up_ref[c, pl.ds(batch + b, pairs, stride=batch), :] = odd
    ue, uo = _slabs(ue_ref), _slabs(uo_ref)
    ue_b, uo_b, up_b = ue.astype(BF16), uo.astype(BF16), _slabs(up_ref, slice(0, rows_k)).astype(BF16)

    lo = lax.broadcasted_iota(jnp.int32, (SUBLANES, LANES), 0) < batch
    swap = lambda v: pltpu.roll(v, batch, 0)
    n_vregs = rows_k // SUBLANES
    half = OCTET_STATE // LANES
    y_even, y_odd = [], []
    for q in range(N_OCTETS):
        ch = slice(LANES * q, LANES * (q + 1))
        lhs = jnp.concatenate([up_b[:, ch], ue_b[:, ch]], axis=1)
        st_ref[:, 2 * OCTET_STATE * q:2 * OCTET_STATE * (q + 1)] = _dot(lhs, b2_ref[q])

    def project_out(octets):
        for q in octets:
            ch = slice(LANES * q, LANES * (q + 1))
            y2 = _dot(st_ref[:, 2 * OCTET_STATE * q:2 * OCTET_STATE * (q + 1)].astype(BF16), c2_ref[q])
            y_even.append(y2[:, :LANES])
            y_odd.append(y2[:, LANES:] + _dot(uo_b[:, ch], m0_ref[q]))

    for q0 in range(0, N_OCTETS, 2):
        octets = (q0, q0 + 1)
        glu_part(q0 // 2)
        if q0 > 0:
            project_out((q0 - 2, q0 - 1))
        if q0 == N_OCTETS - 2:
            finish_prev()
        chains = []
        for q in octets:
            for j in range(half // 2):
                cols = [2 * OCTET_STATE * q + LANES * jj for jj in (j, j + half // 2)]
                lanes = [OCTET_STATE * q + LANES * jj for jj in (j, j + half // 2)]
                chains.append((cols, lanes))
        lam_r, lam_i, sr, si, old = [], [], [], [], []
        for (ca, cb), (la, lb) in chains:
            bc = lambda r, c: jnp.broadcast_to(lam_ref[r:r + 1, c:c + LANES], (SUBLANES, LANES))
            lam_r.append(jnp.where(lo, bc(2, la), bc(2, lb)))
            lam_i.append(jnp.where(lo, bc(3, la), bc(3, lb)))
            prev = [carry_ref[:, c:c + LANES] for c in (ca, cb, ca + OCTET_STATE, cb + OCTET_STATE)]
            old.append(prev)
            sr.append(jnp.where(lo, swap(prev[0]), prev[1]))
            si.append(jnp.where(lo, swap(prev[2]), prev[3]))
        last = [None] * len(chains)
        for m in range(n_vregs):
            rows = slice(SUBLANES * m, SUBLANES * (m + 1))
            for j, ((ca, cb), _) in enumerate(chains):
                ar, br = st_ref[rows, ca:ca + LANES], st_ref[rows, cb:cb + LANES]
                ai, bi = (st_ref[rows, ca + OCTET_STATE:ca + OCTET_STATE + LANES],
                          st_ref[rows, cb + OCTET_STATE:cb + OCTET_STATE + LANES])
                d0r, d1r = jnp.where(lo, ar, swap(br)), jnp.where(lo, swap(ar), br)
                d0i, d1i = jnp.where(lo, ai, swap(bi)), jnp.where(lo, swap(ai), bi)
                s0r = lam_r[j] * sr[j] - lam_i[j] * si[j] + d0r
                s0i = lam_r[j] * si[j] + lam_i[j] * sr[j] + d0i
                s1r = lam_r[j] * s0r - lam_i[j] * s0i + d1r
                s1i = lam_r[j] * s0i + lam_i[j] * s0r + d1i
                sr[j], si[j] = s1r, s1i
                out = (jnp.where(lo, s0r, swap(s1r)), jnp.where(lo, swap(s0r), s1r),
                       jnp.where(lo, s0i, swap(s1i)), jnp.where(lo, swap(s0i), s1i))
                st_ref[rows, ca:ca + LANES] = out[0]
                st_ref[rows, cb:cb + LANES] = out[1]
                st_ref[rows, ca + OCTET_STATE:ca + OCTET_STATE + LANES] = out[2]
                st_ref[rows, cb + OCTET_STATE:cb + OCTET_STATE + LANES] = out[3]
                last[j] = out
        for j, ((ca, cb), _) in enumerate(chains):
            for c, new, prev in zip((ca, cb, ca + OCTET_STATE, cb + OCTET_STATE), last[j], old[j]):
                carry_ref[:, c:c + LANES] = jnp.where(live, new, prev)
    project_out((N_OCTETS - 2, N_OCTETS - 1))
    y_ref[0:rows_k, :] = jnp.concatenate(y_even, axis=-1) + d_ref[...] * ue
    y_ref[rows_k:2 * rows_k, :] = jnp.concatenate(y_odd, axis=-1) + d_ref[...] * uo

    @pl.when(step == n_chunks - 1)
    def _():
        tail = uo_b[rows_k - 2 * SUBLANES:, :]
        zeros = jnp.zeros((2 * SUBLANES, LANES), BF16)
        for q in range(N_OCTETS):
            bu = _dot(jnp.concatenate([zeros, tail[:, LANES * q:LANES * (q + 1)]], axis=1), b2_ref[q])[SUBLANES:]
            base = 2 * OCTET_STATE * q
            lr = lam_ref[0:1, OCTET_STATE * q:OCTET_STATE * (q + 1)]
            li = lam_ref[1:2, OCTET_STATE * q:OCTET_STATE * (q + 1)]
            s_r = carry_ref[:, base:base + OCTET_STATE]
            s_i = carry_ref[:, base + OCTET_STATE:base + 2 * OCTET_STATE]
            fin_ref[:, base:base + OCTET_STATE] = lr * s_r - li * s_i + bu[:, :OCTET_STATE]
            fin_ref[:, base + OCTET_STATE:base + 2 * OCTET_STATE] = lr * s_i + li * s_r + bu[:, OCTET_STATE:]


def _s5_prompt(x, layer, gn, b2, lam, c2, m0, d, wglu, bglu):
    batch, seq, _ = x.shape
    tc = S5_TIME_CHUNK
    n_chunks = seq // tc
    rows_k = tc // 2 * batch
    n_slabs = D_MODEL // LANES
    blk = (batch, tc, D_MODEL)
    this_chunk = pl.BlockSpec(blk, lambda s: (0, jnp.minimum(s, n_chunks - 1), 0))
    prev_chunk = pl.BlockSpec(blk, lambda s: (0, jnp.maximum(s - 1, 0), 0))
    slab = lambda rows: pltpu.VMEM((n_slabs, rows, LANES), F32)
    return pl.pallas_call(
        _s5_prompt_kernel,
        grid=(n_chunks + 1,),
        in_specs=[this_chunk, prev_chunk]
        + [_layer_spec(w.shape, layer) for w in (gn, b2, lam, c2, m0, d, wglu, bglu)],
        out_specs=[prev_chunk, pl.BlockSpec((SUBLANES, STATE_LANES), lambda s: (0, 0))],
        out_shape=[jax.ShapeDtypeStruct(x.shape, F32),
                   jax.ShapeDtypeStruct((SUBLANES, STATE_LANES), F32)],
        scratch_shapes=[pltpu.VMEM((rows_k, STATE_LANES), F32),
                        pltpu.VMEM((SUBLANES, STATE_LANES), F32),
                        slab(tc), slab(rows_k), slab(rows_k), slab(rows_k + SUBLANES),
                        slab(2 * rows_k), slab(tc),
                        pltpu.VMEM((2 * rows_k, D_MODEL), F32)],
        compiler_params=_params(("arbitrary",)),
        name="s5_prompt",
    )(x, x, gn, b2, lam, c2, m0, d, wglu, bglu)


def _s5_sample_kernel(x_ref, gn_ref, b2_ref, lam_ref, c2_ref, d_ref, wglu_ref, bglu_ref, s0_ref,
                      o_ref, snew_ref):
    x = x_ref[...]
    u = _rms(x, gn_ref[...])
    ys = []
    for q in range(N_OCTETS):
        base = 2 * OCTET_STATE * q
        bu = jnp.dot(u[:, LANES * q:LANES * (q + 1)], b2_ref[q, LANES:, :],
                     preferred_element_type=F32, precision=lax.Precision.HIGHEST)
        lr = lam_ref[0:1, OCTET_STATE * q:OCTET_STATE * (q + 1)]
        li = lam_ref[1:2, OCTET_STATE * q:OCTET_STATE * (q + 1)]
        s0r = s0_ref[:, base:base + OCTET_STATE]
        s0i = s0_ref[:, base + OCTET_STATE:base + 2 * OCTET_STATE]
        sr = lr * s0r - li * s0i + bu[:, :OCTET_STATE]
        si = lr * s0i + li * s0r + bu[:, OCTET_STATE:]
        snew_ref[:, base:base + OCTET_STATE] = sr
        snew_ref[:, base + OCTET_STATE:base + 2 * OCTET_STATE] = si
        sb = jnp.concatenate([sr, si], axis=-1).astype(BF16)
        ys.append(_dot(sb, c2_ref[q, :, :LANES]))
    y = jnp.concatenate(ys, axis=-1)
    o_ref[...] = x + _s5_gate(y + d_ref[...] * u, wglu_ref, bglu_ref)


def _s5_sample(x, layer, gn, b2_f32, lam, c2, d, wglu, bglu, s0):
    rows = x.shape[0]
    return pl.pallas_call(
        _s5_sample_kernel,
        grid=(1,),
        in_specs=[_const_spec(x.shape)] + [_layer_spec(w.shape, layer) for w in (gn, b2_f32, lam, c2, d, wglu, bglu)]
        + [_const_spec(s0.shape)],
        out_specs=[pl.BlockSpec((rows, D_MODEL), lambda i: (0, 0)),
                   pl.BlockSpec((rows, STATE_LANES), lambda i: (0, 0))],
        out_shape=[jax.ShapeDtypeStruct((rows, D_MODEL), F32),
                   jax.ShapeDtypeStruct((rows, STATE_LANES), F32)],
        compiler_params=_params(("arbitrary",)),
        name="s5_sample",
    )(x, gn, b2_f32, lam, c2, d, wglu, bglu, s0)


def _block_diag(w):
    n, octets, groups, r, c = w.shape
    col = jnp.arange(groups * c)
    tile = (jnp.arange(c)[:, None] == col % c).astype(F32)
    own = (jnp.arange(groups * r)[:, None] // r == col // c).astype(F32)
    tiled = jnp.einsum("nqrc,cl->nqrl", w.reshape(n, octets, groups * r, c), tile, precision=lax.Precision.HIGHEST)
    return tiled * own


def _s5_weights(a_re, a_im, log_dt, b_re, b_im, c_re, c_im):
    n = a_re.shape[0]
    dt = jnp.exp(log_dt.astype(F32))
    lam = lax.complex(a_re.astype(F32), a_im.astype(F32))
    lam_bar = jnp.exp(lam * dt)
    b = lax.complex(b_re.astype(F32), b_im.astype(F32))
    b_bar = ((lam_bar - 1.0) / lam)[..., None] * b
    c = lax.complex(c_re.astype(F32), c_im.astype(F32))
    gpo = N_GROUPS // N_OCTETS
    octets = lambda w: w.reshape((n, N_OCTETS, gpo) + w.shape[2:])

    def into_state(w):
        wt = jnp.swapaxes(octets(w), 3, 4)
        return jnp.concatenate([_block_diag(jnp.real(wt)), _block_diag(jnp.imag(wt))], axis=-1)

    def out_of_state(w):
        wt = jnp.swapaxes(octets(w), 3, 4)
        return jnp.concatenate([_block_diag(jnp.real(wt)), -_block_diag(jnp.imag(wt))], axis=2)

    b2 = jnp.concatenate([into_state(lam_bar[..., None] * b_bar), into_state(b_bar)], axis=2)
    c2 = jnp.concatenate([out_of_state(c), out_of_state(c * lam_bar[:, :, None, :])], axis=-1)
    cb = jnp.real(jnp.einsum("ngcp,ngpd->ngdc", c, b_bar))
    m0 = _block_diag(octets(cb))
    lam2 = lam_bar * lam_bar
    lam_rows = jnp.stack([jnp.real(lam_bar).reshape(n, -1), jnp.imag(lam_bar).reshape(n, -1),
                          jnp.real(lam2).reshape(n, -1), jnp.imag(lam2).reshape(n, -1)], axis=1)
    return b2, c2.astype(BF16), m0.astype(BF16), lam_rows


def _state_to_lanes(s_re, s_im):
    n = s_re.shape[0]
    re = s_re.astype(F32).reshape(n, N_OCTETS, OCTET_STATE)
    im = s_im.astype(F32).reshape(n, N_OCTETS, OCTET_STATE)
    return jnp.concatenate([re, im], axis=-1).reshape(n, STATE_LANES)


def _lanes_to_state(s):
    n = s.shape[0]
    s4 = s.reshape(n, N_OCTETS, 2, OCTET_STATE)
    return (s4[:, :, 0].reshape(n, N_GROUPS, STATE_DIM), s4[:, :, 1].reshape(n, N_GROUPS, STATE_DIM))


def _rope_partner(n):
    lane = jnp.arange(n)
    return jnp.where(lane % HEAD_DIM < HEAD_DIM // 2, lane + HEAD_DIM // 2, lane - HEAD_DIM // 2)


def _kv_kernel(x_ref, g_ref, wkv_ref, kg_ref, kgp_ref, ones_ref, cos_ref, sin_ref, *rest):
    if len(rest) == 3:
        wvt_ref, k_ref, vt_ref = rest
        v_ref = None
    else:
        (k_ref, v_ref), wvt_ref, vt_ref = rest, None, None
    h = _rms(x_ref[...], g_ref[...]).astype(BF16)
    kv = _dot(h, wkv_ref[...])
    n = k_ref.shape[-1]
    k, k_partner = kv[:, :n], kv[:, n:2 * n]
    scale = lax.rsqrt(_seg_mean_sq(k, ones_ref) + EPS)
    k_ref[...] = scale * (k * (kg_ref[...] * _tile_lanes(cos_ref[...], n))
                          + k_partner * (kgp_ref[...] * _tile_lanes(sin_ref[...], n)))
    if v_ref is not None:
        v_ref[...] = kv[:, 2 * n:]
    else:
        vt_ref[...] = lax.dot_general(wvt_ref[...], h, (((1,), (1,)), ((), ())), preferred_element_type=F32)


def _kv(x, g, wk, wv, kgain, ones, cos, sin, table_block, seq=None):
    rows = x.shape[0]
    n, nv = wk.shape[1], wv.shape[1]
    partner = _rope_partner(n)
    wkv = jnp.concatenate([wk, wk[:, partner]] + ([wv] if seq is None else []), axis=1).astype(BF16)
    tm = min(KV_ROWS, rows)
    in_specs = [pl.BlockSpec((tm, D_MODEL), lambda i: (i, 0)),
                _const_spec((1, D_MODEL)),
                _const_spec(wkv.shape),
                _const_spec((1, n)),
                _const_spec((1, n)),
                _const_spec(ones.shape),
                pl.BlockSpec((tm, LANES), lambda i: (table_block(i), 0)),
                pl.BlockSpec((tm, LANES), lambda i: (table_block(i), 0))]
    out_specs = [pl.BlockSpec((tm, n), lambda i: (i, 0))]
    out_shape = [jax.ShapeDtypeStruct((rows, n), F32)]
    args = [x, g, wkv, kgain, kgain[:, partner], ones, cos, sin]
    if seq is None:
        out_specs.append(pl.BlockSpec((tm, nv), lambda i: (i, 0)))
        out_shape.append(jax.ShapeDtypeStruct((rows, nv), F32))
    else:
        per_seq = seq // tm
        in_specs.append(_const_spec((nv, D_MODEL)))
        args.append(wv.T.astype(BF16))
        out_specs.append(pl.BlockSpec((None, nv, tm), lambda i: (i // per_seq, 0, i % per_seq)))
        out_shape.append(jax.ShapeDtypeStruct((rows // seq, nv, seq), F32))
    return pl.pallas_call(
        _kv_kernel,
        grid=(rows // tm,),
        in_specs=in_specs,
        out_specs=out_specs,
        out_shape=out_shape,
        compiler_params=_params(("parallel",)),
        name="kv_proj",
    )(*args)


def _attn_prompt_kernel(sinks_ref, x_ref, g_ref, wq_ref, qg_ref, ones_ref, cos_ref, sin_ref,
                        kp_ref, kc_ref, vp_ref, vc_ref, wo_ref, o_ref):
    tile = pl.program_id(1)
    x = x_ref[...]
    n_blocks = x.shape[0] // WINDOW
    heads_per_kv = N_HEADS // N_KV_HEADS
    cols = heads_per_kv * WINDOW
    h = _rms(x, g_ref[...]).astype(BF16)
    q = _dot(h, wq_ref[...])
    q = q * lax.rsqrt(_seg_mean_sq(q, ones_ref) + EPS) * qg_ref[...]
    q = _rope(q, _tile_lanes(cos_ref[...], D_MODEL), _tile_lanes(sin_ref[...], D_MODEL))
    qb = (q * (HEAD_DIM ** -0.5)).astype(BF16)

    kall = jnp.concatenate([kp_ref[...], kc_ref[...]], axis=0).astype(BF16)
    vall = jnp.concatenate([vp_ref[...], vc_ref[...]], axis=1).astype(BF16)

    kj = lax.broadcasted_iota(jnp.int32, (WINDOW, cols), 0)
    col = lax.broadcasted_iota(jnp.int32, (WINDOW, cols), 1)
    from_prev = kj > (col & (WINDOW - 1))
    col_head = lax.broadcasted_iota(jnp.int32, (1, cols), 1) // WINDOW
    low_half = lax.broadcasted_iota(jnp.int32, (WINDOW, LANES), 1) < HEAD_DIM
    neg_inf = jnp.full((WINDOW, cols), -jnp.inf, F32)

    all_pairs = [(blk, g) for blk in range(n_blocks) for g in range(N_KV_HEADS)]
    head_outs = [[] for _ in range(n_blocks)]
    group = ATTN_GROUP_BLOCKS * N_KV_HEADS
    for p0 in range(0, len(all_pairs), group):
        pairs = all_pairs[p0:p0 + group]
        scores, sinks = [], []
        for blk, g in pairs:
            r0 = WINDOW * blk
            kg = kall[r0:r0 + 2 * WINDOW, LANES * g:LANES * (g + 1)]
            qs = []
            sink = jnp.zeros((1, cols), F32)
            for hh in range(heads_per_kv):
                head = heads_per_kv * g + hh
                qp = qb[r0:r0 + WINDOW, LANES * (head // 2):LANES * (head // 2 + 1)]
                keep = low_half if head % 2 == 0 else jnp.logical_not(low_half)
                qs.append(jnp.where(keep, qp, jnp.zeros_like(qp)))
                sink = jnp.where(col_head == hh, sinks_ref[head], sink)
            sinks.append(sink)
            scores.append(lax.dot_general(kg, jnp.concatenate(qs, axis=0), (((1,), (1,)), ((), ())),
                                          preferred_element_type=F32))
        probs, denoms = [], []
        for (blk, g), s, sink in zip(pairs, scores, sinks):
            has_prev = tile * n_blocks + blk > 0
            t = jnp.where(from_prev, jnp.where(has_prev, s[:WINDOW], neg_inf), s[WINDOW:])
            m = jnp.maximum(jnp.max(t, axis=0, keepdims=True), sink)
            p = jnp.exp(t - m)
            denoms.append(jnp.sum(p, axis=0, keepdims=True) + jnp.exp(sink - m))
            zero = jnp.zeros_like(p)
            probs.append(jnp.concatenate([jnp.where(from_prev, p, zero), jnp.where(from_prev, zero, p)],
                                         axis=0).astype(BF16))
        for (blk, g), pcat, denom in zip(pairs, probs, denoms):
            r0 = WINDOW * blk
            vg = vall[HEAD_DIM * g:HEAD_DIM * (g + 1), r0:r0 + 2 * WINDOW]
            o = _dot(vg, pcat) / denom
            head_outs[blk] += [o[:, WINDOW * hh:WINDOW * (hh + 1)] for hh in range(heads_per_kv)]
    block_outs = [jnp.concatenate(outs, axis=0).T for outs in head_outs]
    attn = jnp.concatenate(block_outs, axis=0).astype(BF16)
    o_ref[...] = x + _dot(attn, wo_ref[...])


def _attn_prompt(x, g, wq, qgain, ones, cos, sin, kd, vt, sinks, wo):
    bsz, seq, _ = x.shape
    nk, nv = kd.shape[-1], vt.shape[1]
    nb = ATTN_BLOCKS
    tq = nb * WINDOW
    x_spec = pl.BlockSpec((None, tq, D_MODEL), lambda b, i, *_: (b, i, 0))
    k_cur = pl.BlockSpec((None, tq, nk), lambda b, i, *_: (b, i, 0))
    k_prev = pl.BlockSpec((None, WINDOW, nk), lambda b, i, *_: (b, jnp.maximum(nb * i - 1, 0), 0))
    v_cur = pl.BlockSpec((None, nv, tq), lambda b, i, *_: (b, 0, i))
    v_prev = pl.BlockSpec((None, nv, WINDOW), lambda b, i, *_: (b, 0, jnp.maximum(nb * i - 1, 0)))
    tab = pl.BlockSpec((tq, LANES), lambda b, i, *_: (i, 0))
    grid_spec = pltpu.PrefetchScalarGridSpec(
        num_scalar_prefetch=1,
        grid=(bsz, seq // tq),
        in_specs=[x_spec,
                  _const_spec((1, D_MODEL)),
                  _const_spec(wq.shape),
                  _const_spec((1, D_MODEL)),
                  _const_spec(ones.shape),
                  tab, tab, k_prev, k_cur, v_prev, v_cur,
                  _const_spec(wo.shape)],
        out_specs=x_spec)
    return pl.pallas_call(
        _attn_prompt_kernel,
        grid_spec=grid_spec,
        out_shape=jax.ShapeDtypeStruct(x.shape, F32),
        compiler_params=_params(("parallel", "arbitrary")),
        name="attn_prompt",
    )(sinks, x, g, wq, qgain, ones, cos, sin, kd, kd, vt, vt, wo)


def _own_head(shape):
    row_head = lax.broadcasted_iota(jnp.int32, shape, 0) & (N_HEADS - 1)
    lane_head = lax.broadcasted_iota(jnp.int32, shape, 1) // HEAD_DIM
    return row_head == lane_head


def _q_sample_kernel(x_ref, g_ref, wq_ref, qg_ref, ones_ref, cos_ref, sin_ref, rep_ref, place_ref, q_ref):
    h = _rms(x_ref[...], g_ref[...]).astype(BF16)
    q = _dot(h, wq_ref[...])
    q = q * lax.rsqrt(_seg_mean_sq(q, ones_ref) + EPS) * qg_ref[...]
    q = _rope(q, _tile_lanes(cos_ref[...], D_MODEL), _tile_lanes(sin_ref[...], D_MODEL))
    qb = (q * (HEAD_DIM ** -0.5)).astype(BF16)
    q_rep = _dot(rep_ref[...], qb)
    q_own = jnp.where(_own_head(q_rep.shape), q_rep, 0.0).astype(BF16)
    q_ref[...] = _dot(q_own, place_ref[...])


def _q_sample(x, g, wq, qgain, ones, cos, sin, rep, place):
    rows = x.shape[0]
    return pl.pallas_call(
        _q_sample_kernel,
        out_shape=jax.ShapeDtypeStruct((rows * N_HEADS, place.shape[1]), F32),
        compiler_params=pltpu.CompilerParams(vmem_limit_bytes=VMEM_LIMIT_BYTES),
        name="q_sample",
    )(x, g, wq, qgain, ones, cos, sin, rep, place)


def _cache_append_kernel(k_ref, v_ref, kn_ref, vn_ref, ko_ref, vo_ref):
    nb, lanes_kv, keys = k_ref.shape
    first = pl.program_id(0) * nb
    key = lax.broadcasted_iota(jnp.int32, (lanes_kv, keys), 1)
    seq = lax.broadcasted_iota(jnp.int32, (kn_ref.shape[1], keys), 0)

    def split3(x):
        hi = x.astype(BF16)
        r = x - hi.astype(F32)
        mid = r.astype(BF16)
        return hi, mid, (r - mid.astype(F32)).astype(BF16)

    parts = [(split3(kn_ref[...]), k_ref, ko_ref), (split3(vn_ref[...]), v_ref, vo_ref)]
    for b in range(nb):
        pick = (seq == first + b).astype(BF16)
        for (hi, mid, lo), old_ref, out_ref in parts:
            new_col = _dot(hi, pick) + _dot(mid, pick) + _dot(lo, pick)
            out_ref[b] = jnp.where(key == keys - 1, new_col, pltpu.roll(old_ref[b], keys - 1, 1))


def _cache_append(kt, vt, kn_t, vn_t):
    bsz, lanes_kv, keys = kt.shape
    nb = 2 * SUBLANES
    blk = pl.BlockSpec((nb, lanes_kv, keys), lambda i: (i, 0, 0))
    return pl.pallas_call(
        _cache_append_kernel,
        grid=(bsz // nb,),
        in_specs=[blk, blk, _const_spec(kn_t.shape), _const_spec(vn_t.shape)],
        out_specs=[blk, blk],
        out_shape=[jax.ShapeDtypeStruct(kt.shape, F32), jax.ShapeDtypeStruct(vt.shape, F32)],
        compiler_params=_params(("parallel",)),
        name="cache_append",
    )(kt, vt, kn_t, vn_t)


def _attn_sample_kernel(q_ref, k_ref, v_ref, sink_ref, o_ref):
    sink = sink_ref[...]
    for b in range(q_ref.shape[0]):
        s = _dot(q_ref[b].astype(BF16), k_ref[b].astype(BF16))
        m = jnp.maximum(jnp.max(s, axis=-1, keepdims=True), sink)
        p = jnp.exp(s - m)
        denom = jnp.sum(p, axis=-1, keepdims=True) + jnp.exp(sink - m)
        o = lax.dot_general(p.astype(BF16), v_ref[b].astype(BF16), (((1,), (1,)), ((), ())),
                            preferred_element_type=F32)
        o_ref[b] = o / denom


def _attn_sample(q3, kt, vt, sink_col):
    bsz, nh, nkv = q3.shape
    keys = kt.shape[2]
    bb = 4 * SUBLANES
    return pl.pallas_call(
        _attn_sample_kernel,
        grid=(bsz // bb,),
        in_specs=[pl.BlockSpec((bb, nh, nkv), lambda i: (i, 0, 0)),
                  pl.BlockSpec((bb, nkv, keys), lambda i: (i, 0, 0)),
                  pl.BlockSpec((bb, nkv, keys), lambda i: (i, 0, 0)),
                  _const_spec(sink_col.shape)],
        out_specs=pl.BlockSpec((bb, nh, nkv), lambda i: (i, 0, 0)),
        out_shape=jax.ShapeDtypeStruct((bsz, nh, nkv), F32),
        compiler_params=_params(("parallel",)),
        name="attn_sample",
    )(q3, kt, vt, sink_col)


def _proj_residual_kernel(x_ref, a_ref, place_t_ref, rep_t_ref, w_ref, o_ref):
    z = _dot(a_ref[...].astype(BF16), place_t_ref[...])
    z_own = jnp.where(_own_head(z.shape), z, 0.0).astype(BF16)
    attn = _dot(rep_t_ref[...], z_own).astype(BF16)
    o_ref[...] = x_ref[...] + _dot(attn, w_ref[...])


def _proj_residual(x, a, place_t, rep_t, w):
    return pl.pallas_call(
        _proj_residual_kernel,
        out_shape=jax.ShapeDtypeStruct(x.shape, F32),
        compiler_params=pltpu.CompilerParams(vmem_limit_bytes=VMEM_LIMIT_BYTES),
        name="o_proj_sample",
    )(x, a, place_t, rep_t, w)


def _rope_tables(pos):
    half = HEAD_DIM // 2
    inv = ROPE_THETA ** (-jnp.arange(half, dtype=F32) / half)
    ang = pos.astype(F32)[:, None] * inv[None, :]
    cos, sin = jnp.cos(ang), jnp.sin(ang)
    reps = LANES // HEAD_DIM
    return (jnp.tile(jnp.concatenate([cos, cos], axis=-1), (1, reps)),
            jnp.tile(jnp.concatenate([-sin, sin], axis=-1), (1, reps)))


def _block_ones(width, block):
    idx = jnp.arange(width) // block
    return (idx[:, None] == idx[None, :]).astype(BF16)


def _dup_heads(w):
    w3 = w.reshape(w.shape[0], N_KV_HEADS, 1, HEAD_DIM)
    return jnp.broadcast_to(w3, (w.shape[0], N_KV_HEADS, LANES // HEAD_DIM, HEAD_DIM)).reshape(w.shape[0], -1)


def _head_placement():
    lane = jnp.arange(N_HEADS * HEAD_DIM)
    dst = (lane // HEAD_DIM) // (N_HEADS // N_KV_HEADS) * HEAD_DIM + lane % HEAD_DIM
    return (dst[:, None] == jnp.arange(N_KV_HEADS * HEAD_DIM)[None, :]).astype(BF16)


def _row_replication(n):
    return (jnp.arange(n * N_HEADS)[:, None] // N_HEADS == jnp.arange(n)[None, :]).astype(BF16)


def kernel(x_prompt, x_sample, state_ssm_re, state_ssm_im, cache_k, cache_v, norm_mix, norm_mlp, ssm_a_re, ssm_a_im, ssm_log_dt, ssm_b_re, ssm_b_im, ssm_c_re, ssm_c_im, ssm_d, w_glu, b_glu, norm_kv, w_k, w_v, k_norm, w_q, q_norm, attn_sinks, w_o, w_mlp_in, w_mlp_out):
    bsz, seq, _ = x_prompt.shape
    dec = x_sample.shape[0]
    n_a = ssm_a_re.shape[0]
    depth = norm_mix.shape[0]
    past = 8192
    nkv = N_KV_HEADS * HEAD_DIM
    row = lambda v: v.astype(F32).reshape(1, -1)

    xp = x_prompt.astype(F32)
    xs = x_sample.reshape(dec, D_MODEL)
    rows3 = lambda v: v.astype(F32).reshape(v.shape[0], 1, -1)

    w1 = w_mlp_in.astype(BF16)
    w2 = w_mlp_out.astype(BF16)
    g_mlp = rows3(norm_mlp)
    def mlp_both(x, xs, layer):
        y, ys = _mlp(x.reshape(bsz * seq, D_MODEL), xs, g_mlp, w1, w2, layer)
        return y.reshape(bsz, seq, D_MODEL), ys


    b2, c2, m0, lam = _s5_weights(ssm_a_re, ssm_a_im, ssm_log_dt, ssm_b_re, ssm_b_im, ssm_c_re, ssm_c_im)
    b2_16 = b2.astype(BF16)
    wglu = w_glu.astype(BF16)
    g_mix, d_skip, bg = rows3(norm_mix), rows3(ssm_d), rows3(b_glu)
    sp_re, sp_im, ss_re, ss_im = [], [], [], []
    for i in range(n_a):
        xp, fin = _s5_prompt(xp, i, g_mix, b2_16, lam, c2, m0, d_skip, wglu, bg)
        re, im = _lanes_to_state(fin[bsz:])
        sp_re.append(re)
        sp_im.append(im)
        s0 = _state_to_lanes(state_ssm_re[i], state_ssm_im[i])
        xs, snew = _s5_sample(xs, i, g_mix, b2, lam, c2, d_skip, wglu, bg, s0)
        re, im = _lanes_to_state(snew)
        ss_re.append(re)
        ss_im.append(im)
        xp, xs = mlp_both(xp, xs, i)

    cos_p, sin_p = _rope_tables(jnp.arange(seq, dtype=jnp.int32))
    cos_s, sin_s = _rope_tables(jnp.full((dec,), past, dtype=jnp.int32))
    ones_head = _block_ones(MXU_TILE, HEAD_DIM)
    place = _head_placement()
    rep = _row_replication(dec)
    kgain = row(k_norm)
    tile_gain = lambda gvec, n: jnp.tile(gvec, (1, n // HEAD_DIM))

    kd, vt = _kv(xp.reshape(bsz * seq, D_MODEL), row(norm_kv), _dup_heads(w_k), w_v,
                     tile_gain(kgain, 2 * nkv), ones_head, cos_p, sin_p, lambda i: i % (seq // KV_ROWS), seq=seq)
    ks_new, vs_new = _kv(xs, row(norm_kv), w_k, w_v, tile_gain(kgain, nkv), ones_head, cos_s, sin_s,
                         lambda i: i)
    kd = kd.reshape(bsz, seq, 2 * nkv)
    new_k_p = kd[:, -WINDOW:].reshape(bsz, WINDOW, N_KV_HEADS, LANES // HEAD_DIM, HEAD_DIM)[:, :, :, 0]
    new_v_p = jnp.transpose(vt[:, :, -WINDOW:].reshape(bsz, N_KV_HEADS, HEAD_DIM, WINDOW), (0, 3, 1, 2))
    key_minor = lambda c: jnp.transpose(c.astype(F32), (0, 2, 3, 1)).reshape(dec, nkv, c.shape[1])
    keys_s, vals_s = _cache_append(key_minor(cache_k), key_minor(cache_v), ks_new.T, vs_new.T)
    key_major = lambda c: jnp.transpose(c.reshape(dec, N_KV_HEADS, HEAD_DIM, c.shape[2]), (0, 3, 1, 2))
    new_k_s, new_v_s = key_major(keys_s), key_major(vals_s)

    for j in range(depth - n_a):
        layer = n_a + j
        gn = row(norm_mix[layer])
        qgain = row(q_norm[j])
        wq = w_q[j].astype(BF16)
        wo = w_o[j].astype(BF16)
        xp = _attn_prompt(xp, gn, wq, tile_gain(qgain, D_MODEL), ones_head, cos_p, sin_p, kd, vt,
                          attn_sinks[j].astype(F32), wo)
        q_pad = _q_sample(xs, gn, wq, tile_gain(qgain, D_MODEL), ones_head, cos_s, sin_s, rep, place)
        o_pad = _attn_sample(q_pad.reshape(dec, N_HEADS, nkv), keys_s, vals_s,
                             attn_sinks[j].astype(F32).reshape(N_HEADS, 1))
        xs = _proj_residual(xs, o_pad.reshape(dec * N_HEADS, nkv), place.T, rep.T, wo)
        xp, xs = mlp_both(xp, xs, layer)

    return (xp, xs.reshape(dec, 1, D_MODEL),
            jnp.stack(sp_re), jnp.stack(sp_im), new_k_p, new_v_p,
            jnp.stack(ss_re), jnp.stack(ss_im), new_k_s, new_v_s)
```
